```python
import math
import jax, jax.numpy as jnp
from jax import lax
import numpy as np


D_MODEL = 1024
BATCH = 4
SEQ = 4096
DEPTH = 2
DEC_BATCH = 128
DEC_SEQ = 8
PAST_LEN = 16384
PAGE_SIZE = 128

N_A_LAYERS = DEPTH // 2
N_B_LAYERS = DEPTH - N_A_LAYERS
GROUP_SIZE = 16
N_GROUPS = D_MODEL // GROUP_SIZE
STATE_DIM = 64
HEAD_DIM = 64
N_Q_HEADS = D_MODEL // HEAD_DIM
N_KV_HEADS = max(1, N_Q_HEADS // 8)
Q_PER_KV = N_Q_HEADS // N_KV_HEADS
WINDOW = 128
WIN_BUF = min(WINDOW, PAST_LEN)
ATTN_SCALE = 1.0 / math.sqrt(HEAD_DIM)
PEER_HEADS = 8
N_KEYS = 128
N_EXPERTS = N_KEYS * N_KEYS
PEER_TOPK = 16
PEER_DKEY = 256
PEER_DHALF = PEER_DKEY // 2
PEER_BLOCK = 128
EPS = 1e-5

kernel_name = 'yoco_s5_swa_sink_peer_step'


def _rmsnorm(x, g):
    xf = x.astype(jnp.float32)
    r = lax.rsqrt(jnp.mean(xf * xf, axis=-1, keepdims=True) + EPS)
    return (xf * r * g.astype(jnp.float32)).astype(x.dtype)


def _s5_mixer(u, h0_re, h0_im, lam_re, lam_im, log_dt, b_re, b_im, c_re, c_im, d_skip, w_glu):
    N, L, Dm = u.shape
    f32 = jnp.float32
    uf = u.astype(f32).reshape(N, L, N_GROUPS, GROUP_SIZE)
    lr = lam_re.astype(f32)
    li = lam_im.astype(f32)
    dt = jnp.exp(log_dt.astype(f32))[:, None]
    mag = jnp.exp(lr * dt)
    ab_re = mag * jnp.cos(li * dt)
    ab_im = mag * jnp.sin(li * dt)
    den = lr * lr + li * li
    f_re = ((ab_re - 1.0) * lr + ab_im * li) / den
    f_im = (ab_im * lr - (ab_re - 1.0) * li) / den
    br = b_re.astype(f32)
    bi = b_im.astype(f32)
    bb_re = f_re[..., None] * br - f_im[..., None] * bi
    bb_im = f_re[..., None] * bi + f_im[..., None] * br
    bu_re = jnp.einsum('nlgj,gpj->nlgp', uf, bb_re)
    bu_im = jnp.einsum('nlgj,gpj->nlgp', uf, bb_im)
    hr0 = h0_re.astype(f32)
    hi0 = h0_im.astype(f32)
    bu_re = bu_re.at[:, 0].add(ab_re * hr0 - ab_im * hi0)
    bu_im = bu_im.at[:, 0].add(ab_re * hi0 + ab_im * hr0)
    a_re = jnp.broadcast_to(ab_re, (1, L, N_GROUPS, STATE_DIM))
    a_im = jnp.broadcast_to(ab_im, (1, L, N_GROUPS, STATE_DIM))

    def combine(e1, e2):
        a1r, a1i, b1r, b1i = e1
        a2r, a2i, b2r, b2i = e2
        return (a1r * a2r - a1i * a2i,
                a1r * a2i + a1i * a2r,
                a2r * b1r - a2i * b1i + b2r,
                a2r * b1i + a2i * b1r + b2i)

    _, _, hr, hi = lax.associative_scan(combine, (a_re, a_im, bu_re, bu_im), axis=1)
    y = (jnp.einsum('nlgp,gjp->nlgj', hr, c_re.astype(f32))
         - jnp.einsum('nlgp,gjp->nlgj', hi, c_im.astype(f32))
         + d_skip.astype(f32).reshape(N_GROUPS, GROUP_SIZE) * uf)
    z = jax.nn.gelu(y.reshape(N, L, Dm), approximate=False)
    zz = z @ w_glu.astype(f32)
    out = zz[..., :Dm] * jax.nn.sigmoid(zz[..., Dm:])
    return out.astype(u.dtype), hr[:, -1].astype(u.dtype), hi[:, -1].astype(u.dtype)


def _prompt_context(k, v):
    N, L = k.shape[:2]
    nb = L // WINDOW
    kb = k.reshape(N, nb, WINDOW, N_KV_HEADS, HEAD_DIM)
    vb = v.reshape(N, nb, WINDOW, N_KV_HEADS, HEAD_DIM)
    kprev = jnp.concatenate([jnp.zeros_like(kb[:, :1]), kb[:, :-1]], axis=1)
    vprev = jnp.concatenate([jnp.zeros_like(vb[:, :1]), vb[:, :-1]], axis=1)
    k_ctx = jnp.concatenate([kprev, kb], axis=2)
    v_ctx = jnp.concatenate([vprev, vb], axis=2)
    i = jnp.arange(WINDOW)[None, :, None]
    j = jnp.arange(2 * WINDOW)[None, None, :]
    blk = jnp.arange(nb)[:, None, None]
    d = i + WINDOW - j
    mask = (d >= 0) & (d < WINDOW) & (blk * WINDOW + j - WINDOW >= 0)
    return (k_ctx, v_ctx, mask, k[:, L - WIN_BUF:], v[:, L - WIN_BUF:])


def _sample_context(k, v, k_buf, v_buf):
    L = k.shape[1]
    k_all = jnp.concatenate([k_buf.astype(k.dtype), k], axis=1)
    v_all = jnp.concatenate([v_buf.astype(v.dtype), v], axis=1)
    kpos = jnp.concatenate([PAST_LEN - WIN_BUF + jnp.arange(WIN_BUF), PAST_LEN + jnp.arange(L)])
    qpos = PAST_LEN + jnp.arange(L)
    d = qpos[:, None] - kpos[None, :]
    mask = ((d >= 0) & (d < WINDOW))[None]
    return (k_all[:, None], v_all[:, None], mask, k_all[:, -WIN_BUF:], v_all[:, -WIN_BUF:])


def _window_attention(q, k_ctx, v_ctx, mask, sinks):
    N, L = q.shape[:2]
    nb, qb, kb = mask.shape
    qbk = q.reshape(N, nb, qb, N_KV_HEADS, Q_PER_KV, HEAD_DIM)
    s = jnp.einsum('nbqkgd,nbjkd->nbkgqj', qbk, k_ctx).astype(jnp.float32) * ATTN_SCALE
    s = jnp.where(mask[None, :, None, None], s, -jnp.inf)
    sink = sinks.astype(jnp.float32).reshape(1, 1, N_KV_HEADS, Q_PER_KV, 1, 1)
    m = jnp.maximum(jnp.max(s, axis=-1, keepdims=True), sink)
    e = jnp.exp(s - m)
    w = e / (jnp.sum(e, axis=-1, keepdims=True) + jnp.exp(sink - m))
    o = jnp.einsum('nbkgqj,nbjkd->nbqkgd', w.astype(v_ctx.dtype), v_ctx)
    return o.reshape(N, L, N_Q_HEADS * HEAD_DIM)


def _peer(x, w_q, sub_keys, u_tab, v_tab):
    N, L, Dm = x.shape
    T = N * L
    nblk = -(-T // PEER_BLOCK)
    t = jnp.pad(x.reshape(T, Dm), ((0, nblk * PEER_BLOCK - T), (0, 0)))
    t = t.reshape(nblk, PEER_BLOCK, Dm)
    sk = sub_keys.astype(jnp.float32)

    def one(xb):
        q = (xb @ w_q).astype(jnp.float32).reshape(PEER_BLOCK, PEER_HEADS, 2, PEER_DHALF)
        s = jnp.einsum('bhcd,hcnd->bhcn', q, sk)
        sv, si = lax.top_k(s, PEER_TOPK)
        cand = (sv[:, :, 0, :, None] + sv[:, :, 1, None, :]).reshape(PEER_BLOCK, PEER_HEADS, PEER_TOPK * PEER_TOPK)
        cid = (si[:, :, 0, :, None] * N_KEYS + si[:, :, 1, None, :]).reshape(PEER_BLOCK, PEER_HEADS, PEER_TOPK * PEER_TOPK)
        fv, fi = lax.top_k(cand, PEER_TOPK)
        eid = jnp.take_along_axis(cid, fi, axis=-1)
        g = jax.nn.softmax(fv, axis=-1)
        hpre = jnp.einsum('bd,bhkd->bhk', xb, u_tab[eid])
        a = (jax.nn.gelu(hpre.astype(jnp.float32), approximate=False) * g).astype(xb.dtype)
        return jnp.einsum('bhk,bhkd->bd', a, v_tab[eid])

    out = lax.map(one, t).reshape(nblk * PEER_BLOCK, Dm)[:T]
    return out.reshape(N, L, Dm).astype(x.dtype)


def _trunk(x, h0_re, h0_im, k_buf, v_buf, p):
    N, L, _ = x.shape
    h = x
    fin_re = []
    fin_im = []
    ctx = None
    for layer in range(DEPTH):
        if layer < N_A_LAYERS:
            a = layer
            y, sr, si = _s5_mixer(_rmsnorm(h, p['norm_mix'][layer]), h0_re[a], h0_im[a],
                                  p['ssm_lam_re'][a], p['ssm_lam_im'][a], p['ssm_log_dt'][a],
                                  p['ssm_b_re'][a], p['ssm_b_im'][a], p['ssm_c_re'][a], p['ssm_c_im'][a],
                                  p['ssm_d'][a], p['ssm_w_glu'][a])
            fin_re.append(sr)
            fin_im.append(si)
            h = h + y
        else:
            if ctx is None:
                kv = (_rmsnorm(h, p['norm_kv']) @ p['w_kv']).reshape(N, L, 2, N_KV_HEADS, HEAD_DIM)
                k, v = kv[:, :, 0], kv[:, :, 1]
                if k_buf is None:
                    ctx = _prompt_context(k, v)
                else:
                    ctx = _sample_context(k, v, k_buf, v_buf)
            b = layer - N_A_LAYERS
            q = (_rmsnorm(h, p['norm_mix'][layer]) @ p['w_q'][b]).reshape(N, L, N_KV_HEADS, Q_PER_KV, HEAD_DIM)
            o = _window_attention(q, ctx[0], ctx[1], ctx[2], p['attn_sinks'][b])
            h = h + o @ p['w_o'][b]
        h = h + _peer(_rmsnorm(h, p['norm_ffn'][layer]), p['peer_w_q'][layer], p['peer_sub_keys'][layer],
                      p['peer_u'][layer], p['peer_v'][layer])
    y = _rmsnorm(h, p['norm_final'])
    return y, jnp.stack(fin_re), jnp.stack(fin_im), ctx[3], ctx[4]


def setup_inputs(seed: int = 0) -> dict:
    key = jax.random.key(seed)
    ks = list(jax.random.split(key, 32))
    f32 = jnp.float32

    def nrm(i, shape, scale):
        return jax.random.normal(ks[i], shape, f32) * scale

    n_pos = jnp.pi * jnp.arange(STATE_DIM, dtype=f32)
    return {
        'x_prompt': nrm(0, (BATCH, SEQ, D_MODEL), 1.0),
        'x_sample': nrm(1, (DEC_BATCH, DEC_SEQ, D_MODEL), 1.0),
        'state_ssm_re': nrm(2, (N_A_LAYERS, DEC_BATCH, N_GROUPS, STATE_DIM), 0.1),
        'state_ssm_im': nrm(3, (N_A_LAYERS, DEC_BATCH, N_GROUPS, STATE_DIM), 0.1),
        'cache_k_win': nrm(4, (DEC_BATCH, WIN_BUF, N_KV_HEADS, HEAD_DIM), 1.0),
        'cache_v_win': nrm(5, (DEC_BATCH, WIN_BUF, N_KV_HEADS, HEAD_DIM), 1.0),
        'norm_mix': 1.0 + nrm(6, (DEPTH, D_MODEL), 0.02),
        'norm_ffn': 1.0 + nrm(7, (DEPTH, D_MODEL), 0.02),
        'norm_kv': 1.0 + nrm(8, (D_MODEL,), 0.02),
        'norm_final': 1.0 + nrm(9, (D_MODEL,), 0.02),
        'ssm_lam_re': -0.5 + nrm(10, (N_A_LAYERS, N_GROUPS, STATE_DIM), 0.01),
        'ssm_lam_im': n_pos + nrm(11, (N_A_LAYERS, N_GROUPS, STATE_DIM), 0.01),
        'ssm_log_dt': jax.random.uniform(ks[12], (N_A_LAYERS, N_GROUPS), f32, math.log(1e-3), math.log(1e-1)),
        'ssm_b_re': nrm(13, (N_A_LAYERS, N_GROUPS, STATE_DIM, GROUP_SIZE), (2 * GROUP_SIZE) ** -0.5),
        'ssm_b_im': nrm(14, (N_A_LAYERS, N_GROUPS, STATE_DIM, GROUP_SIZE), (2 * GROUP_SIZE) ** -0.5),
        'ssm_c_re': nrm(15, (N_A_LAYERS, N_GROUPS, GROUP_SIZE, STATE_DIM), 1.0),
        'ssm_c_im': nrm(16, (N_A_LAYERS, N_GROUPS, GROUP_SIZE, STATE_DIM), 1.0),
        'ssm_d': nrm(17, (N_A_LAYERS, D_MODEL), 1.0),
        'ssm_w_glu': nrm(18, (N_A_LAYERS, D_MODEL, 2 * D_MODEL), D_MODEL ** -0.5),
        'w_kv': nrm(19, (D_MODEL, 2 * N_KV_HEADS * HEAD_DIM), D_MODEL ** -0.5),
        'w_q': nrm(20, (N_B_LAYERS, D_MODEL, N_Q_HEADS * HEAD_DIM), D_MODEL ** -0.5),
        'attn_sinks': nrm(21, (N_B_LAYERS, N_Q_HEADS), 0.5),
        'w_o': nrm(22, (N_B_LAYERS, N_Q_HEADS * HEAD_DIM, D_MODEL), (N_Q_HEADS * HEAD_DIM) ** -0.5),
        'peer_w_q': nrm(23, (DEPTH, D_MODEL, PEER_HEADS * PEER_DKEY), D_MODEL ** -0.5),
        'peer_sub_keys': nrm(24, (DEPTH, PEER_HEADS, 2, N_KEYS, PEER_DHALF), PEER_DHALF ** -0.5),
        'peer_u': nrm(25, (DEPTH, N_EXPERTS, D_MODEL), D_MODEL ** -0.5),
        'peer_v': nrm(26, (DEPTH, N_EXPERTS, D_MODEL), 0.1),
    }


def reference(x_prompt, x_sample, state_ssm_re, state_ssm_im, cache_k_win, cache_v_win,
              norm_mix, norm_ffn, norm_kv, norm_final,
              ssm_lam_re, ssm_lam_im, ssm_log_dt, ssm_b_re, ssm_b_im, ssm_c_re, ssm_c_im, ssm_d, ssm_w_glu,
              w_kv, w_q, attn_sinks, w_o,
              peer_w_q, peer_sub_keys, peer_u, peer_v):
    p = {'norm_mix': norm_mix, 'norm_ffn': norm_ffn, 'norm_kv': norm_kv, 'norm_final': norm_final,
         'ssm_lam_re': ssm_lam_re, 'ssm_lam_im': ssm_lam_im, 'ssm_log_dt': ssm_log_dt,
         'ssm_b_re': ssm_b_re, 'ssm_b_im': ssm_b_im, 'ssm_c_re': ssm_c_re, 'ssm_c_im': ssm_c_im,
         'ssm_d': ssm_d, 'ssm_w_glu': ssm_w_glu, 'w_kv': w_kv, 'w_q': w_q, 'attn_sinks': attn_sinks,
         'w_o': w_o, 'peer_w_q': peer_w_q, 'peer_sub_keys': peer_sub_keys, 'peer_u': peer_u, 'peer_v': peer_v}
    z0 = jnp.zeros((N_A_LAYERS, x_prompt.shape[0], N_GROUPS, STATE_DIM), x_prompt.dtype)
    y_prompt, sre_p, sim_p, kw_p, vw_p = _trunk(x_prompt, z0, z0, None, None, p)
    y_sample, sre_s, sim_s, kw_s, vw_s = _trunk(x_sample, state_ssm_re, state_ssm_im, cache_k_win, cache_v_win, p)
    return (y_prompt, y_sample, sre_p, sim_p, kw_p, vw_p, sre_s, sim_s, kw_s, vw_s)
```

```python
import functools
import math

import jax
import jax.numpy as jnp
from jax import lax
from jax.experimental import pallas as pl
from jax.experimental.pallas import tpu as pltpu

D_MODEL = 1024
GROUP_SIZE = 16
N_GROUPS = D_MODEL // GROUP_SIZE
STATE_DIM = 64
HEAD_DIM = 64
N_Q_HEADS = D_MODEL // HEAD_DIM
N_KV_HEADS = N_Q_HEADS // 8
Q_PER_KV = N_Q_HEADS // N_KV_HEADS
WINDOW = 128
PAST_LEN = 16384
ATTN_SCALE = 1.0 / math.sqrt(HEAD_DIM)
PEER_HEADS = 8
N_KEYS = 128
PEER_TOPK = 16
PEER_DHALF = 128
EPS = 1e-5

LANES = 128
SUBLANES = 8
VMEM_LIMIT_BYTES = 56 * 1024 * 1024

GROUPS_PER_CHUNK = LANES // GROUP_SIZE
N_CHUNKS = N_GROUPS // GROUPS_PER_CHUNK
CHUNK_STATE = GROUPS_PER_CHUNK * STATE_DIM
S5_ROWS = 256


def _rmsnorm_rows(x, g):
    r = lax.rsqrt(jnp.mean(x * x, axis=-1, keepdims=True) + EPS)
    return x * r * g


def _gelu(x):
    return 0.5 * x * (1.0 + lax.erf(x * (1.0 / math.sqrt(2.0))))


def _s5_discretize(lam_re, lam_im, log_dt, b_re, b_im, c_re, c_im):
    f32 = jnp.float32
    lr = lam_re.astype(f32)
    li = lam_im.astype(f32)
    dt = jnp.exp(log_dt.astype(f32))[:, None]
    mag = jnp.exp(lr * dt)
    ab_re = mag * jnp.cos(li * dt)
    ab_im = mag * jnp.sin(li * dt)
    den = lr * lr + li * li
    f_re = ((ab_re - 1.0) * lr + ab_im * li) / den
    f_im = (ab_im * lr - (ab_re - 1.0) * li) / den
    br = b_re.astype(f32)
    bi = b_im.astype(f32)
    bb_re = f_re[..., None] * br - f_im[..., None] * bi
    bb_im = f_re[..., None] * bi + f_im[..., None] * br
    eye = jnp.eye(GROUPS_PER_CHUNK, dtype=f32)

    def chunk_rows(v):
        return v.reshape(N_CHUNKS, 1, CHUNK_STATE)

    def in_blocks(bb):
        t = bb.reshape(N_CHUNKS, GROUPS_PER_CHUNK, STATE_DIM, GROUP_SIZE).transpose(0, 1, 3, 2)
        return jnp.einsum('mgjp,gh->mgjhp', t, eye).reshape(N_CHUNKS, LANES, CHUNK_STATE)

    def out_blocks(c):
        t = c.astype(f32).reshape(N_CHUNKS, GROUPS_PER_CHUNK, GROUP_SIZE, STATE_DIM).transpose(0, 1, 3, 2)
        return jnp.einsum('mgpj,gh->mgphj', t, eye).reshape(N_CHUNKS, CHUNK_STATE, LANES)

    bmat = jnp.concatenate([in_blocks(bb_re), in_blocks(bb_im)], axis=2)
    cmat = jnp.concatenate([out_blocks(c_re), -out_blocks(c_im)], axis=1)
    pr, pi = [ab_re], [ab_im]
    for _ in range(SUBLANES - 1):
        pr, pi = pr + [pr[-1] * ab_re - pi[-1] * ab_im], pi + [pr[-1] * ab_im + pi[-1] * ab_re]
    apr = jnp.concatenate([chunk_rows(v) for v in pr], axis=1)
    api = jnp.concatenate([chunk_rows(v) for v in pi], axis=1)
    return bmat, cmat, apr, api


def _s5_kernel(x_ref, g_ref, d_ref, b_ref, c_ref, apr_ref, api_ref, h0r_ref, h0i_ref,
               z_ref, sr_ref, si_ref, u_ref, st_ref, cr_ref, ci_ref, *, chain, blocks_per_seq):
    rb = pl.program_id(0)
    m = pl.program_id(1)
    rows = x_ref.shape[0]
    n_tiles = rows // SUBLANES

    @pl.when(m == 0)
    def _():
        u = _rmsnorm_rows(x_ref[...], g_ref[...])
        for mm in range(N_CHUNKS):
            u_ref[mm] = u[:, mm * LANES:(mm + 1) * LANES]

    u = u_ref[m]
    st_ref[...] = jnp.dot(u, b_ref[0], preferred_element_type=jnp.float32, precision=lax.Precision.HIGHEST)

    apr = apr_ref[0]
    api = api_ref[0]
    row = lax.broadcasted_iota(jnp.int32, (SUBLANES, CHUNK_STATE), 0)

    if chain:
        @pl.when(rb % blocks_per_seq == 0)
        def _():
            cr_ref[m] = h0r_ref[0, 0]
            ci_ref[m] = h0i_ref[0, 0]

    def tile_step(k, carry):
        r0 = pl.multiple_of(k * SUBLANES, SUBLANES)
        xr = st_ref[pl.ds(r0, SUBLANES), 0:CHUNK_STATE]
        xi = st_ref[pl.ds(r0, SUBLANES), CHUNK_STATE:2 * CHUNK_STATE]
        for d in (1, 2, 4):
            ar = apr[d - 1:d, :]
            ai = api[d - 1:d, :]
            sr = jnp.where(row >= d, pltpu.roll(xr, d, axis=0), 0.0)
            si = jnp.where(row >= d, pltpu.roll(xi, d, axis=0), 0.0)
            xr, xi = xr + ar * sr - ai * si, xi + ar * si + ai * sr
        if chain:
            cr, ci = carry
        else:
            cr = h0r_ref[k, 0]
            ci = h0i_ref[k, 0]
        hr = xr + apr * cr - api * ci
        hi = xi + apr * ci + api * cr
        st_ref[pl.ds(r0, SUBLANES), 0:CHUNK_STATE] = hr
        st_ref[pl.ds(r0, SUBLANES), CHUNK_STATE:2 * CHUNK_STATE] = hi
        lr_ = hr[SUBLANES - 1:SUBLANES, :]
        li_ = hi[SUBLANES - 1:SUBLANES, :]
        if chain:
            return lr_, li_
        sr_ref[k, m] = lr_
        si_ref[k, m] = li_
        return carry

    if chain:
        cr, ci = lax.fori_loop(0, n_tiles, tile_step, (cr_ref[m], ci_ref[m]))
        cr_ref[m] = cr
        ci_ref[m] = ci
        sr_ref[0, m] = cr
        si_ref[0, m] = ci
    else:
        lax.fori_loop(0, n_tiles, tile_step, 0)

    y = jnp.dot(st_ref[...].astype(jnp.bfloat16), c_ref[0], preferred_element_type=jnp.float32)
    y = y + d_ref[0] * u
    z_ref[...] = _gelu(y)


def _s5_mixer(x, h0r, h0i, g, d_skip, bmat, cmat_bf16, apr, api, *, chain, seq_len):
    T = x.shape[0]
    nseq = h0r.shape[0]
    rows = S5_ROWS
    if chain:
        blocks_per_seq = seq_len // rows
        seq_blk = 1
        seq_map = lambda rb, m: (rb // blocks_per_seq, m, 0, 0)
        out_map = lambda rb, m: (rb // blocks_per_seq, 0, 0, 0)
    else:
        assert seq_len == SUBLANES
        blocks_per_seq = 1
        seq_blk = rows // SUBLANES
        seq_map = lambda rb, m: (rb, m, 0, 0)
        out_map = lambda rb, m: (rb, 0, 0, 0)
    grid = (T // rows, N_CHUNKS)
    kern = functools.partial(_s5_kernel, chain=chain, blocks_per_seq=blocks_per_seq)
    st_spec = pl.BlockSpec((seq_blk, 1, 1, CHUNK_STATE), seq_map)
    out_st_spec = pl.BlockSpec((seq_blk, N_CHUNKS, 1, CHUNK_STATE), out_map)
    z, sr, si = pl.pallas_call(
        kern,
        grid=grid,
        in_specs=[
            pl.BlockSpec((rows, D_MODEL), lambda rb, m: (rb, 0)),
            pl.BlockSpec((1, D_MODEL), lambda rb, m: (0, 0)),
            pl.BlockSpec((1, 1, LANES), lambda rb, m: (m, 0, 0)),
            pl.BlockSpec((1, LANES, 2 * CHUNK_STATE), lambda rb, m: (m, 0, 0)),
            pl.BlockSpec((1, 2 * CHUNK_STATE, LANES), lambda rb, m: (m, 0, 0)),
            pl.BlockSpec((1, SUBLANES, CHUNK_STATE), lambda rb, m: (m, 0, 0)),
            pl.BlockSpec((1, SUBLANES, CHUNK_STATE), lambda rb, m: (m, 0, 0)),
            st_spec, st_spec,
        ],
        out_specs=[
            pl.BlockSpec((rows, LANES), lambda rb, m: (rb, m)),
            out_st_spec, out_st_spec,
        ],
        out_shape=[
            jax.ShapeDtypeStruct((T, D_MODEL), jnp.float32),
            jax.ShapeDtypeStruct((nseq, N_CHUNKS, 1, CHUNK_STATE), jnp.float32),
            jax.ShapeDtypeStruct((nseq, N_CHUNKS, 1, CHUNK_STATE), jnp.float32),
        ],
        scratch_shapes=[
            pltpu.VMEM((N_CHUNKS, rows, LANES), jnp.float32),
            pltpu.VMEM((rows, 2 * CHUNK_STATE), jnp.float32),
            pltpu.VMEM((N_CHUNKS, 1, CHUNK_STATE), jnp.float32),
            pltpu.VMEM((N_CHUNKS, 1, CHUNK_STATE), jnp.float32),
        ],
        compiler_params=pltpu.CompilerParams(
            dimension_semantics=("arbitrary", "arbitrary"), vmem_limit_bytes=VMEM_LIMIT_BYTES),
        name="s5_mixer",
    )(x, g.reshape(1, D_MODEL), d_skip.reshape(N_CHUNKS, 1, LANES), bmat, cmat_bf16, apr, api, h0r, h0i)
    return z, sr, si


def _state_to_chunks(h):
    return h.reshape(h.shape[0], N_CHUNKS, 1, CHUNK_STATE)


def _chunks_to_state(s):
    return s.reshape(s.shape[0], N_GROUPS, STATE_DIM)


def _glu_kernel(z_ref, res_ref, w_ref, o_ref):
    zz = jnp.dot(z_ref[...].astype(jnp.bfloat16), w_ref[...], preferred_element_type=jnp.float32)
    a = zz[:, :D_MODEL]
    b = zz[:, D_MODEL:]
    o_ref[...] = res_ref[...] + a * (1.0 / (1.0 + jnp.exp(-b)))


def _glu_residual(z, res, w_bf16, rows=512):
    T = z.shape[0]
    return pl.pallas_call(
        _glu_kernel,
        grid=(T // rows,),
        in_specs=[
            pl.BlockSpec((rows, D_MODEL), lambda i: (i, 0)),
            pl.BlockSpec((rows, D_MODEL), lambda i: (i, 0)),
            pl.BlockSpec((D_MODEL, 2 * D_MODEL), lambda i: (0, 0)),
        ],
        out_specs=pl.BlockSpec((rows, D_MODEL), lambda i: (i, 0)),
        out_shape=jax.ShapeDtypeStruct((T, D_MODEL), jnp.float32),
        compiler_params=pltpu.CompilerParams(
            dimension_semantics=("arbitrary",), vmem_limit_bytes=VMEM_LIMIT_BYTES),
        name="glu_residual",
    )(z, res, w_bf16)


ROUTE_ROWS = 256
CAND = PEER_TOPK * PEER_TOPK


def _topk_rows(s, payload):
    n_rows = s.shape[0]
    row = lax.broadcasted_iota(jnp.int32, s.shape, 0)
    vals, picks = [], []
    for _ in range(PEER_TOPK):
        m = jnp.max(s, axis=0, keepdims=True)
        pos = jnp.min(jnp.where(s == m, row, n_rows), axis=0, keepdims=True)
        sel = row == pos
        vals.append(m)
        if payload is None:
            picks.append(pos)
        else:
            picks.append(jnp.max(jnp.where(sel, payload, -1), axis=0, keepdims=True))
        s = jnp.where(sel, -jnp.inf, s)
    return jnp.concatenate(vals, axis=0), jnp.concatenate(picks, axis=0)


def _peer_route_kernel(h_ref, g_ref, wq_ref, sk_ref, xn_ref, eid_ref, gate_ref, xb_ref):
    hd = pl.program_id(1)

    @pl.when(hd == 0)
    def _():
        xn = _rmsnorm_rows(h_ref[...], g_ref[...])
        xn_ref[...] = xn
        xb_ref[...] = xn.astype(jnp.bfloat16)

    q = jnp.dot(xb_ref[...], wq_ref[...], preferred_element_type=jnp.float32)
    sv, si = [], []
    for c in range(2):
        qc = q[:, c * PEER_DHALF:(c + 1) * PEER_DHALF].astype(jnp.bfloat16)
        st = lax.dot_general(sk_ref[0, c], qc, (((1,), (1,)), ((), ())), preferred_element_type=jnp.float32)
        v, i = _topk_rows(st, None)
        sv.append(v)
        si.append(i)
    cand = jnp.concatenate([sv[0][i:i + 1, :] + sv[1] for i in range(PEER_TOPK)], axis=0)
    cid = jnp.concatenate([si[0][i:i + 1, :] * N_KEYS + si[1] for i in range(PEER_TOPK)], axis=0)
    fv, eid = _topk_rows(cand, cid)
    e = jnp.exp(fv - fv[0:1, :])
    gate_ref[...] = e / jnp.sum(e, axis=0, keepdims=True)
    eid_ref[...] = eid


def _peer_route(h, g, wq_bf16, sk_bf16):
    T = h.shape[0]
    rows = ROUTE_ROWS
    n_sel = PEER_HEADS * PEER_TOPK
    return pl.pallas_call(
        _peer_route_kernel,
        grid=(T // rows, PEER_HEADS),
        in_specs=[
            pl.BlockSpec((rows, D_MODEL), lambda tb, hd: (tb, 0)),
            pl.BlockSpec((1, D_MODEL), lambda tb, hd: (0, 0)),
            pl.BlockSpec((D_MODEL, 2 * PEER_DHALF), lambda tb, hd: (0, hd)),
            pl.BlockSpec((1, 2, N_KEYS, PEER_DHALF), lambda tb, hd: (hd, 0, 0, 0)),
        ],
        out_specs=[
            pl.BlockSpec((rows, D_MODEL), lambda tb, hd: (tb, 0)),
            pl.BlockSpec((PEER_TOPK, rows), lambda tb, hd: (hd, tb)),
            pl.BlockSpec((PEER_TOPK, rows), lambda tb, hd: (hd, tb)),
        ],
        out_shape=[
            jax.ShapeDtypeStruct((T, D_MODEL), jnp.float32),
            jax.ShapeDtypeStruct((n_sel, T), jnp.int32),
            jax.ShapeDtypeStruct((n_sel, T), jnp.float32),
        ],
        scratch_shapes=[pltpu.VMEM((rows, D_MODEL), jnp.bfloat16)],
        compiler_params=pltpu.CompilerParams(
            dimension_semantics=("arbitrary", "arbitrary"), vmem_limit_bytes=VMEM_LIMIT_BYTES),
        name="peer_route",
    )(h, g.reshape(1, D_MODEL), wq_bf16, sk_bf16)


N_SEL = PEER_HEADS * PEER_TOPK
ROW_TILE = (D_MODEL // LANES, LANES)
GATHER_TOKENS = 32


def _peer_expert_kernel(eid_ref, xn_ref, gate_ref, h_ref, gfin_ref, u_hbm, v_hbm, out_ref,
                        ubuf, vbuf, sem, *, final_norm):
    n_tok = xn_ref.shape[0]

    def row_copy(tab, buf, which, slot, e, r):
        return pltpu.make_async_copy(tab.at[e], buf.at[slot, r], sem.at[which, slot])

    def issue(t, slot):
        for r in range(N_SEL):
            e = eid_ref[t, r]
            row_copy(u_hbm, ubuf, 0, slot, e, r).start()
            row_copy(v_hbm, vbuf, 1, slot, e, r).start()

    def wait(slot):
        for r in range(N_SEL):
            row_copy(u_hbm, ubuf, 0, slot, 0, r).wait()
            row_copy(v_hbm, vbuf, 1, slot, 0, r).wait()

    ri = lax.broadcasted_iota(jnp.int32, (N_SEL, LANES), 0)
    li = lax.broadcasted_iota(jnp.int32, (N_SEL, LANES), 1)
    diag = ri == li

    def compute(t, slot):
        x = xn_ref[t]
        u2 = ubuf.at[slot]
        v2 = vbuf.at[slot]
        partial = None
        for s in range(ROW_TILE[0]):
            term = u2[:, s, :] * x[s:s + 1, :]
            partial = term if partial is None else partial + term
        hcol = jnp.sum(partial, axis=1, keepdims=True)
        hrow = jnp.sum(jnp.where(diag, hcol, 0.0), axis=0, keepdims=True)
        arow = _gelu(hrow) * gate_ref[pl.ds(t, 1), :]
        acol = jnp.sum(jnp.where(diag, arow, 0.0), axis=1, keepdims=True)
        rows = [jnp.sum(v2[:, s, :] * acol, axis=0, keepdims=True) for s in range(ROW_TILE[0])]
        y = h_ref[t] + jnp.concatenate(rows, axis=0)
        if final_norm:
            r = lax.rsqrt(jnp.sum(y * y) * (1.0 / D_MODEL) + EPS)
            y = y * r * gfin_ref[...]
        out_ref[t] = y

    issue(0, 0)

    def body(t, carry):
        slot = t % 2

        @pl.when(t + 1 < n_tok)
        def _():
            issue(t + 1, 1 - slot)

        wait(slot)
        compute(t, slot)
        return carry

    lax.fori_loop(0, n_tok, body, 0)


def _peer_experts(eid, xn, gate, h, gfin, u_tab, v_tab, *, final_norm):
    T = xn.shape[0]
    nt = GATHER_TOKENS
    tile = lambda a: a.reshape((a.shape[0],) + ROW_TILE)
    tok_spec = pl.BlockSpec((nt,) + ROW_TILE, lambda i: (i, 0, 0))
    out = pl.pallas_call(
        functools.partial(_peer_expert_kernel, final_norm=final_norm),
        grid=(T // nt,),
        in_specs=[
            pl.BlockSpec((nt, N_SEL), lambda i: (i, 0), memory_space=pltpu.SMEM),
            tok_spec,
            pl.BlockSpec((nt, N_SEL), lambda i: (i, 0)),
            tok_spec,
            pl.BlockSpec(ROW_TILE, lambda i: (0, 0)),
            pl.BlockSpec(memory_space=pl.ANY),
            pl.BlockSpec(memory_space=pl.ANY),
        ],
        out_specs=tok_spec,
        out_shape=jax.ShapeDtypeStruct((T,) + ROW_TILE, jnp.float32),
        scratch_shapes=[
            pltpu.VMEM((2, N_SEL) + ROW_TILE, jnp.float32),
            pltpu.VMEM((2, N_SEL) + ROW_TILE, jnp.float32),
            pltpu.SemaphoreType.DMA((2, 2)),
        ],
        compiler_params=pltpu.CompilerParams(
            dimension_semantics=("arbitrary",), vmem_limit_bytes=VMEM_LIMIT_BYTES),
        name="peer_experts",
    )(eid, tile(xn), gate, tile(h), gfin.reshape(ROW_TILE), tile(u_tab), tile(v_tab))
    return out.reshape(T, D_MODEL)


KV_WIDTH = N_KV_HEADS * HEAD_DIM
BF16 = jnp.bfloat16


def _qkv(h, gkv, gq, wkv_ref, wq_ref):
    kv = jnp.dot(_rmsnorm_rows(h, gkv).astype(BF16), wkv_ref[...], preferred_element_type=jnp.float32)
    q = jnp.dot(_rmsnorm_rows(h, gq).astype(BF16), wq_ref[...], preferred_element_type=jnp.float32)
    return kv, q


def _sink_softmax_pv(parts, sink):
    m = sink
    for s, _ in parts:
        m = jnp.maximum(m, jnp.max(s, axis=-1, keepdims=True))
    den = jnp.exp(sink - m)
    acc = None
    for s, v in parts:
        e = jnp.exp(s - m)
        den = den + jnp.sum(e, axis=-1, keepdims=True)
        pv = jnp.dot(e.astype(BF16), v.astype(BF16), preferred_element_type=jnp.float32)
        acc = pv if acc is None else acc + pv
    return acc / den


def _nt_dot(a, b):
    return lax.dot_general(a.astype(BF16), b.astype(BF16), (((1,), (1,)), ((), ())),
                           preferred_element_type=jnp.float32)


def _attn_prompt_kernel(h_ref, gkv_ref, gq_ref, wkv_ref, wq_ref, wo_ref, sink_ref,
                        o_ref, kvw_ref, prev_ref):
    blk = pl.program_id(1)
    h = h_ref[...]
    kv, q = _qkv(h, gkv_ref[...], gq_ref[...], wkv_ref, wq_ref)
    kvw_ref[0] = kv

    @pl.when(blk == 0)
    def _():
        prev_ref[...] = jnp.zeros_like(prev_ref)

    prev = prev_ref[...]
    qi = lax.broadcasted_iota(jnp.int32, (WINDOW, WINDOW), 0)
    kj = lax.broadcasted_iota(jnp.int32, (WINDOW, WINDOW), 1)
    prev_ok = jnp.logical_and(kj > qi, blk > 0)
    cur_ok = kj <= qi
    heads = []
    for kvh in range(N_KV_HEADS):
        ks = slice(kvh * HEAD_DIM, (kvh + 1) * HEAD_DIM)
        vs = slice(KV_WIDTH + kvh * HEAD_DIM, KV_WIDTH + (kvh + 1) * HEAD_DIM)
        for g in range(Q_PER_KV):
            hq = kvh * Q_PER_KV + g
            qh = q[:, hq * HEAD_DIM:(hq + 1) * HEAD_DIM]
            sp = jnp.where(prev_ok, _nt_dot(qh, prev[:, ks]) * ATTN_SCALE, -jnp.inf)
            sc = jnp.where(cur_ok, _nt_dot(qh, kv[:, ks]) * ATTN_SCALE, -jnp.inf)
            heads.append(_sink_softmax_pv([(sp, prev[:, vs]), (sc, kv[:, vs])], sink_ref[hq]))
    o = jnp.concatenate(heads, axis=1)
    o_ref[...] = h + jnp.dot(o.astype(BF16), wo_ref[...], preferred_element_type=jnp.float32)
    prev_ref[...] = kv


def _attn_prompt(h, gkv, gq, wkv, wq, wo, sinks, *, n_seq, seq_len):
    nb = seq_len // WINDOW
    row_spec = pl.BlockSpec((WINDOW, D_MODEL), lambda n, b: (n * nb + b, 0))
    full = lambda shape: pl.BlockSpec(shape, lambda n, b: (0,) * len(shape))
    return pl.pallas_call(
        _attn_prompt_kernel,
        grid=(n_seq, nb),
        in_specs=[
            row_spec, full((1, D_MODEL)), full((1, D_MODEL)),
            full((D_MODEL, 2 * KV_WIDTH)), full((D_MODEL, D_MODEL)), full((D_MODEL, D_MODEL)),
            pl.BlockSpec(memory_space=pltpu.SMEM),
        ],
        out_specs=[row_spec, pl.BlockSpec((1, WINDOW, 2 * KV_WIDTH), lambda n, b: (n, 0, 0))],
        out_shape=[
            jax.ShapeDtypeStruct(h.shape, jnp.float32),
            jax.ShapeDtypeStruct((n_seq, WINDOW, 2 * KV_WIDTH), jnp.float32),
        ],
        scratch_shapes=[pltpu.VMEM((WINDOW, 2 * KV_WIDTH), jnp.float32)],
        compiler_params=pltpu.CompilerParams(
            dimension_semantics=("arbitrary", "arbitrary"), vmem_limit_bytes=VMEM_LIMIT_BYTES),
        name="attn_prompt",
    )(h, gkv.reshape(1, D_MODEL), gq.reshape(1, D_MODEL), wkv, wq, wo, sinks)


ATTN_SEQS = 16


def _attn_sample_kernel(h_ref, ck_ref, cv_ref, gkv_ref, gq_ref, wkv_ref, wq_ref, wo_ref, sink_ref,
                        o_ref, kw_ref, vw_ref, kv_ref, q_ref, att_ref, *, n_new):
    h = h_ref[...]
    kv, q = _qkv(h, gkv_ref[...], gq_ref[...], wkv_ref, wq_ref)
    kv_ref[...] = kv
    q_ref[...] = q
    n_seq = h.shape[0] // n_new
    rows = Q_PER_KV * n_new
    qpos_c = lax.broadcasted_iota(jnp.int32, (rows, WINDOW), 0) % n_new
    cache_ok = lax.broadcasted_iota(jnp.int32, (rows, WINDOW), 1) > qpos_c
    qpos_n = lax.broadcasted_iota(jnp.int32, (rows, n_new), 0) % n_new
    new_ok = lax.broadcasted_iota(jnp.int32, (rows, n_new), 1) <= qpos_n

    def per_seq(n, carry):
        r0 = pl.multiple_of(n * n_new, n_new)
        kvn = kv_ref[pl.ds(r0, n_new), :]
        qn = q_ref[pl.ds(r0, n_new), :]
        ck = ck_ref[n]
        cv = cv_ref[n]
        kw_ref[n, 0:WINDOW - n_new, :] = ck[n_new:, :]
        kw_ref[n, WINDOW - n_new:WINDOW, :] = kvn[:, 0:KV_WIDTH]
        vw_ref[n, 0:WINDOW - n_new, :] = cv[n_new:, :]
        vw_ref[n, WINDOW - n_new:WINDOW, :] = kvn[:, KV_WIDTH:2 * KV_WIDTH]
        outs = []
        for kvh in range(N_KV_HEADS):
            ks = slice(kvh * HEAD_DIM, (kvh + 1) * HEAD_DIM)
            vs = slice(KV_WIDTH + kvh * HEAD_DIM, KV_WIDTH + (kvh + 1) * HEAD_DIM)
            qs = jnp.concatenate(
                [qn[:, (kvh * Q_PER_KV + g) * HEAD_DIM:(kvh * Q_PER_KV + g + 1) * HEAD_DIM] for g in range(Q_PER_KV)],
                axis=0)
            s_c = jnp.where(cache_ok, _nt_dot(qs, ck[:, ks]) * ATTN_SCALE, -jnp.inf)
            s_n = jnp.where(new_ok, _nt_dot(qs, kvn[:, ks]) * ATTN_SCALE, -jnp.inf)
            sink = sink_ref[kvh][:, 0:1]
            o = _sink_softmax_pv([(s_c, cv[:, ks]), (s_n, kvn[:, vs])], sink)
            outs += [o[g * n_new:(g + 1) * n_new, :] for g in range(Q_PER_KV)]
        att_ref[pl.ds(r0, n_new), :] = jnp.concatenate(outs, axis=1)
        return carry

    lax.fori_loop(0, n_seq, per_seq, 0)
    o_ref[...] = h + jnp.dot(att_ref[...].astype(BF16), wo_ref[...], preferred_element_type=jnp.float32)


def _attn_sample(h, cache_k, cache_v, gkv, gq, wkv, wq, wo, sinks, *, n_new):
    n_seq = cache_k.shape[0]
    sb = ATTN_SEQS
    rows = sb * n_new
    row_spec = pl.BlockSpec((rows, D_MODEL), lambda i: (i, 0))
    win_spec = pl.BlockSpec((sb, WINDOW, KV_WIDTH), lambda i: (i, 0, 0))
    full = lambda shape: pl.BlockSpec(shape, lambda i: (0,) * len(shape))
    sink_rows = jnp.repeat(sinks.reshape(N_KV_HEADS, Q_PER_KV), n_new, axis=1)[:, :, None]
    sink_rows = jnp.broadcast_to(sink_rows, (N_KV_HEADS, Q_PER_KV * n_new, LANES))
    return pl.pallas_call(
        functools.partial(_attn_sample_kernel, n_new=n_new),
        grid=(n_seq // sb,),
        in_specs=[
            row_spec, win_spec, win_spec, full((1, D_MODEL)), full((1, D_MODEL)),
            full((D_MODEL, 2 * KV_WIDTH)), full((D_MODEL, D_MODEL)), full((D_MODEL, D_MODEL)),
            full((N_KV_HEADS, Q_PER_KV * n_new, LANES)),
        ],
        out_specs=[row_spec, win_spec, win_spec],
        out_shape=[
            jax.ShapeDtypeStruct(h.shape, jnp.float32),
            jax.ShapeDtypeStruct((n_seq, WINDOW, KV_WIDTH), jnp.float32),
            jax.ShapeDtypeStruct((n_seq, WINDOW, KV_WIDTH), jnp.float32),
        ],
        scratch_shapes=[
            pltpu.VMEM((rows, 2 * KV_WIDTH), jnp.float32),
            pltpu.VMEM((rows, D_MODEL), jnp.float32),
            pltpu.VMEM((rows, D_MODEL), jnp.float32),
        ],
        compiler_params=pltpu.CompilerParams(
            dimension_semantics=("arbitrary",), vmem_limit_bytes=VMEM_LIMIT_BYTES),
        name="attn_sample",
    )(h, cache_k, cache_v, gkv.reshape(1, D_MODEL), gq.reshape(1, D_MODEL), wkv, wq, wo, sink_rows)


def kernel(x_prompt, x_sample, state_ssm_re, state_ssm_im, cache_k_win, cache_v_win, norm_mix, norm_ffn, norm_kv, norm_final, ssm_lam_re, ssm_lam_im, ssm_log_dt, ssm_b_re, ssm_b_im, ssm_c_re, ssm_c_im, ssm_d, ssm_w_glu, w_kv, w_q, attn_sinks, w_o, peer_w_q, peer_sub_keys, peer_u, peer_v):
    bmat, cmat, apr, api = _s5_discretize(ssm_lam_re[0], ssm_lam_im[0], ssm_log_dt[0], ssm_b_re[0], ssm_b_im[0], ssm_c_re[0], ssm_c_im[0])
    cmat = cmat.astype(jnp.bfloat16)
    wglu = ssm_w_glu[0].astype(jnp.bfloat16)
    xp = x_prompt.reshape(-1, D_MODEL)
    xs = x_sample.reshape(-1, D_MODEL)
    z0 = jnp.zeros((x_prompt.shape[0], N_CHUNKS, 1, CHUNK_STATE), jnp.float32)
    zp, srp, sip = _s5_mixer(xp, z0, z0, norm_mix[0], ssm_d[0], bmat, cmat, apr, api, chain=True, seq_len=x_prompt.shape[1])
    zs, srs, sis = _s5_mixer(xs, _state_to_chunks(state_ssm_re[0]), _state_to_chunks(state_ssm_im[0]), norm_mix[0], ssm_d[0], bmat, cmat, apr, api, chain=False, seq_len=x_sample.shape[1])
    h1p = _glu_residual(zp, xp, wglu)
    h1s = _glu_residual(zs, xs, wglu)
    h1 = jnp.concatenate([h1p, h1s], axis=0)
    n_prompt_rows = xp.shape[0]

    def peer(h, layer, final_norm):
        xn, eid_t, gate_t = _peer_route(h, norm_ffn[layer], peer_w_q[layer].astype(BF16),
                                        peer_sub_keys[layer].astype(BF16))
        return _peer_experts(eid_t.T, xn, gate_t.T, h, norm_final, peer_u[layer], peer_v[layer],
                             final_norm=final_norm)

    h2 = peer(h1, 0, False)
    wkv = w_kv.astype(BF16)
    wq = w_q[0].astype(BF16)
    wo = w_o[0].astype(BF16)
    n_dec = x_sample.shape[0]
    h3p, kvw_p = _attn_prompt(h2[:n_prompt_rows], norm_kv, norm_mix[1], wkv, wq, wo, attn_sinks[0],
                              n_seq=x_prompt.shape[0], seq_len=x_prompt.shape[1])
    h3s, kw_s, vw_s = _attn_sample(h2[n_prompt_rows:], cache_k_win.reshape(n_dec, WINDOW, KV_WIDTH),
                                   cache_v_win.reshape(n_dec, WINDOW, KV_WIDTH), norm_kv, norm_mix[1],
                                   wkv, wq, wo, attn_sinks[0], n_new=x_sample.shape[1])
    y = peer(jnp.concatenate([h3p, h3s], axis=0), 1, True)
    win = lambda a: a.reshape(a.shape[0], WINDOW, N_KV_HEADS, HEAD_DIM)
    return (y[:n_prompt_rows].reshape(x_prompt.shape), y[n_prompt_rows:].reshape(x_sample.shape),
            _chunks_to_state(srp)[None], _chunks_to_state(sip)[None],
            win(kvw_p[:, :, :KV_WIDTH]), win(kvw_p[:, :, KV_WIDTH:]),
            _chunks_to_state(srs)[None], _chunks_to_state(sis)[None],
            win(kw_s), win(vw_s))
```

```python
import functools
import math

import jax
import jax.numpy as jnp
from jax import lax
from jax.experimental import pallas as pl
from jax.experimental.pallas import tpu as pltpu
from jax.experimental.pallas import tpu_sc as plsc

D_MODEL = 1024
GROUP_SIZE = 16
N_GROUPS = D_MODEL // GROUP_SIZE
STATE_DIM = 64
HEAD_DIM = 64
N_Q_HEADS = D_MODEL // HEAD_DIM
N_KV_HEADS = N_Q_HEADS // 8
Q_PER_KV = N_Q_HEADS // N_KV_HEADS
WINDOW = 128
PAST_LEN = 16384
ATTN_SCALE = 1.0 / math.sqrt(HEAD_DIM)
PEER_HEADS = 8
N_KEYS = 128
PEER_TOPK = 16
PEER_DHALF = 128
EPS = 1e-5

LANES = 128
SUBLANES = 8
VMEM_LIMIT_BYTES = 56 * 1024 * 1024

GROUPS_PER_CHUNK = LANES // GROUP_SIZE
N_CHUNKS = N_GROUPS // GROUPS_PER_CHUNK
CHUNK_STATE = GROUPS_PER_CHUNK * STATE_DIM
S5_ROWS = 256


def _rmsnorm_rows(x, g):
    r = lax.rsqrt(jnp.mean(x * x, axis=-1, keepdims=True) + EPS)
    return x * r * g


def _gelu(x):
    return 0.5 * x * (1.0 + lax.erf(x * (1.0 / math.sqrt(2.0))))


def _s5_discretize(lam_re, lam_im, log_dt, b_re, b_im, c_re, c_im):
    f32 = jnp.float32
    lr = lam_re.astype(f32)
    li = lam_im.astype(f32)
    dt = jnp.exp(log_dt.astype(f32))[:, None]
    mag = jnp.exp(lr * dt)
    ab_re = mag * jnp.cos(li * dt)
    ab_im = mag * jnp.sin(li * dt)
    den = lr * lr + li * li
    f_re = ((ab_re - 1.0) * lr + ab_im * li) / den
    f_im = (ab_im * lr - (ab_re - 1.0) * li) / den
    br = b_re.astype(f32)
    bi = b_im.astype(f32)
    bb_re = f_re[..., None] * br - f_im[..., None] * bi
    bb_im = f_re[..., None] * bi + f_im[..., None] * br
    eye = jnp.eye(GROUPS_PER_CHUNK, dtype=f32)

    def chunk_rows(v):
        return v.reshape(N_CHUNKS, 1, CHUNK_STATE)

    def in_blocks(bb):
        t = bb.reshape(N_CHUNKS, GROUPS_PER_CHUNK, STATE_DIM, GROUP_SIZE).transpose(0, 1, 3, 2)
        return jnp.einsum('mgjp,gh->mgjhp', t, eye).reshape(N_CHUNKS, LANES, CHUNK_STATE)

    def out_blocks(c):
        t = c.astype(f32).reshape(N_CHUNKS, GROUPS_PER_CHUNK, GROUP_SIZE, STATE_DIM).transpose(0, 1, 3, 2)
        return jnp.einsum('mgpj,gh->mgphj', t, eye).reshape(N_CHUNKS, CHUNK_STATE, LANES)

    bmat = jnp.concatenate([in_blocks(bb_re), in_blocks(bb_im)], axis=2)
    cmat = jnp.concatenate([out_blocks(c_re), -out_blocks(c_im)], axis=1)
    pr, pi = [ab_re], [ab_im]
    for _ in range(SUBLANES - 1):
        pr, pi = pr + [pr[-1] * ab_re - pi[-1] * ab_im], pi + [pr[-1] * ab_im + pi[-1] * ab_re]
    apr = jnp.concatenate([chunk_rows(v) for v in pr], axis=1)
    api = jnp.concatenate([chunk_rows(v) for v in pi], axis=1)
    return bmat, cmat, apr, api


def _s5_kernel(x_ref, g_ref, d_ref, b_ref, c_ref, apr_ref, api_ref, h0r_ref, h0i_ref,
               z_ref, sr_ref, si_ref, u_ref, st_ref, cr_ref, ci_ref, *, chain, blocks_per_seq):
    rb = pl.program_id(0)
    m = pl.program_id(1)
    rows = x_ref.shape[0]
    n_tiles = rows // SUBLANES

    @pl.when(m == 0)
    def _():
        u = _rmsnorm_rows(x_ref[...], g_ref[...])
        for mm in range(N_CHUNKS):
            u_ref[mm] = u[:, mm * LANES:(mm + 1) * LANES]

    u = u_ref[m]
    st_ref[...] = jnp.dot(u, b_ref[0], preferred_element_type=jnp.float32, precision=lax.Precision.HIGHEST)

    apr = apr_ref[0]
    api = api_ref[0]
    row = lax.broadcasted_iota(jnp.int32, (SUBLANES, CHUNK_STATE), 0)

    if chain:
        @pl.when(rb % blocks_per_seq == 0)
        def _():
            cr_ref[m] = h0r_ref[0, 0]
            ci_ref[m] = h0i_ref[0, 0]

    def tile_step(k, carry):
        r0 = pl.multiple_of(k * SUBLANES, SUBLANES)
        xr = st_ref[pl.ds(r0, SUBLANES), 0:CHUNK_STATE]
        xi = st_ref[pl.ds(r0, SUBLANES), CHUNK_STATE:2 * CHUNK_STATE]
        for d in (1, 2, 4):
            ar = apr[d - 1:d, :]
            ai = api[d - 1:d, :]
            sr = jnp.where(row >= d, pltpu.roll(xr, d, axis=0), 0.0)
            si = jnp.where(row >= d, pltpu.roll(xi, d, axis=0), 0.0)
            xr, xi = xr + ar * sr - ai * si, xi + ar * si + ai * sr
        if chain:
            cr, ci = carry
        else:
            cr = h0r_ref[k, 0]
            ci = h0i_ref[k, 0]
        hr = xr + apr * cr - api * ci
        hi = xi + apr * ci + api * cr
        st_ref[pl.ds(r0, SUBLANES), 0:CHUNK_STATE] = hr
        st_ref[pl.ds(r0, SUBLANES), CHUNK_STATE:2 * CHUNK_STATE] = hi
        lr_ = hr[SUBLANES - 1:SUBLANES, :]
        li_ = hi[SUBLANES - 1:SUBLANES, :]
        if chain:
            return lr_, li_
        sr_ref[k, m] = lr_
        si_ref[k, m] = li_
        return carry

    if chain:
        cr, ci = lax.fori_loop(0, n_tiles, tile_step, (cr_ref[m], ci_ref[m]))
        cr_ref[m] = cr
        ci_ref[m] = ci
        sr_ref[0, m] = cr
        si_ref[0, m] = ci
    else:
        lax.fori_loop(0, n_tiles, tile_step, 0)

    y = jnp.dot(st_ref[...].astype(jnp.bfloat16), c_ref[0], preferred_element_type=jnp.float32)
    y = y + d_ref[0] * u
    z_ref[...] = _gelu(y)


def _s5_mixer(x, h0r, h0i, g, d_skip, bmat, cmat_bf16, apr, api, *, chain, seq_len):
    T = x.shape[0]
    nseq = h0r.shape[0]
    rows = S5_ROWS
    if chain:
        blocks_per_seq = seq_len // rows
        seq_blk = 1
        seq_map = lambda rb, m: (rb // blocks_per_seq, m, 0, 0)
        out_map = lambda rb, m: (rb // blocks_per_seq, 0, 0, 0)
    else:
        assert seq_len == SUBLANES
        blocks_per_seq = 1
        seq_blk = rows // SUBLANES
        seq_map = lambda rb, m: (rb, m, 0, 0)
        out_map = lambda rb, m: (rb, 0, 0, 0)
    grid = (T // rows, N_CHUNKS)
    kern = functools.partial(_s5_kernel, chain=chain, blocks_per_seq=blocks_per_seq)
    st_spec = pl.BlockSpec((seq_blk, 1, 1, CHUNK_STATE), seq_map)
    out_st_spec = pl.BlockSpec((seq_blk, N_CHUNKS, 1, CHUNK_STATE), out_map)
    z, sr, si = pl.pallas_call(
        kern,
        grid=grid,
        in_specs=[
            pl.BlockSpec((rows, D_MODEL), lambda rb, m: (rb, 0)),
            pl.BlockSpec((1, D_MODEL), lambda rb, m: (0, 0)),
            pl.BlockSpec((1, 1, LANES), lambda rb, m: (m, 0, 0)),
            pl.BlockSpec((1, LANES, 2 * CHUNK_STATE), lambda rb, m: (m, 0, 0)),
            pl.BlockSpec((1, 2 * CHUNK_STATE, LANES), lambda rb, m: (m, 0, 0)),
            pl.BlockSpec((1, SUBLANES, CHUNK_STATE), lambda rb, m: (m, 0, 0)),
            pl.BlockSpec((1, SUBLANES, CHUNK_STATE), lambda rb, m: (m, 0, 0)),
            st_spec, st_spec,
        ],
        out_specs=[
            pl.BlockSpec((rows, LANES), lambda rb, m: (rb, m)),
            out_st_spec, out_st_spec,
        ],
        out_shape=[
            jax.ShapeDtypeStruct((T, D_MODEL), jnp.float32),
            jax.ShapeDtypeStruct((nseq, N_CHUNKS, 1, CHUNK_STATE), jnp.float32),
            jax.ShapeDtypeStruct((nseq, N_CHUNKS, 1, CHUNK_STATE), jnp.float32),
        ],
        scratch_shapes=[
            pltpu.VMEM((N_CHUNKS, rows, LANES), jnp.float32),
            pltpu.VMEM((rows, 2 * CHUNK_STATE), jnp.float32),
            pltpu.VMEM((N_CHUNKS, 1, CHUNK_STATE), jnp.float32),
            pltpu.VMEM((N_CHUNKS, 1, CHUNK_STATE), jnp.float32),
        ],
        compiler_params=pltpu.CompilerParams(
            dimension_semantics=("arbitrary", "arbitrary"), vmem_limit_bytes=VMEM_LIMIT_BYTES),
        name="s5_mixer",
    )(x, g.reshape(1, D_MODEL), d_skip.reshape(N_CHUNKS, 1, LANES), bmat, cmat_bf16, apr, api, h0r, h0i)
    return z, sr, si


def _state_to_chunks(h):
    return h.reshape(h.shape[0], N_CHUNKS, 1, CHUNK_STATE)


def _chunks_to_state(s):
    return s.reshape(s.shape[0], N_GROUPS, STATE_DIM)


def _glu_kernel(z_ref, res_ref, w_ref, o_ref):
    zz = jnp.dot(z_ref[...].astype(jnp.bfloat16), w_ref[...], preferred_element_type=jnp.float32)
    a = zz[:, :D_MODEL]
    b = zz[:, D_MODEL:]
    o_ref[...] = res_ref[...] + a * (1.0 / (1.0 + jnp.exp(-b)))


def _glu_residual(z, res, w_bf16, rows=512):
    T = z.shape[0]
    return pl.pallas_call(
        _glu_kernel,
        grid=(T // rows,),
        in_specs=[
            pl.BlockSpec((rows, D_MODEL), lambda i: (i, 0)),
            pl.BlockSpec((rows, D_MODEL), lambda i: (i, 0)),
            pl.BlockSpec((D_MODEL, 2 * D_MODEL), lambda i: (0, 0)),
        ],
        out_specs=pl.BlockSpec((rows, D_MODEL), lambda i: (i, 0)),
        out_shape=jax.ShapeDtypeStruct((T, D_MODEL), jnp.float32),
        compiler_params=pltpu.CompilerParams(
            dimension_semantics=("arbitrary",), vmem_limit_bytes=VMEM_LIMIT_BYTES),
        name="glu_residual",
    )(z, res, w_bf16)


ROUTE_ROWS = 256
CAND = PEER_TOPK * PEER_TOPK


def _topk_rows(s, payload):
    n_rows = s.shape[0]
    row = lax.broadcasted_iota(jnp.int32, s.shape, 0)
    vals, picks = [], []
    for _ in range(PEER_TOPK):
        m = jnp.max(s, axis=0, keepdims=True)
        pos = jnp.min(jnp.where(s == m, row, n_rows), axis=0, keepdims=True)
        sel = row == pos
        vals.append(m)
        if payload is None:
            picks.append(pos)
        else:
            picks.append(jnp.max(jnp.where(sel, payload, -1), axis=0, keepdims=True))
        s = jnp.where(sel, -jnp.inf, s)
    return jnp.concatenate(vals, axis=0), jnp.concatenate(picks, axis=0)


def _peer_route_kernel(h_ref, g_ref, wq_ref, sk_ref, xn_ref, eid_ref, gate_ref, xb_ref):
    hd = pl.program_id(1)

    @pl.when(hd == 0)
    def _():
        xn = _rmsnorm_rows(h_ref[...], g_ref[...])
        xn_ref[...] = xn
        xb_ref[...] = xn.astype(jnp.bfloat16)

    q = jnp.dot(xb_ref[...], wq_ref[...], preferred_element_type=jnp.float32)
    sv, si = [], []
    for c in range(2):
        qc = q[:, c * PEER_DHALF:(c + 1) * PEER_DHALF].astype(jnp.bfloat16)
        st = lax.dot_general(sk_ref[0, c], qc, (((1,), (1,)), ((), ())), preferred_element_type=jnp.float32)
        v, i = _topk_rows(st, None)
        sv.append(v)
        si.append(i)
    cand = jnp.concatenate([sv[0][i:i + 1, :] + sv[1] for i in range(PEER_TOPK)], axis=0)
    cid = jnp.concatenate([si[0][i:i + 1, :] * N_KEYS + si[1] for i in range(PEER_TOPK)], axis=0)
    fv, eid = _topk_rows(cand, cid)
    e = jnp.exp(fv - fv[0:1, :])
    gate_ref[...] = e / jnp.sum(e, axis=0, keepdims=True)
    eid_ref[...] = eid


def _peer_route(h, g, wq_bf16, sk_bf16):
    T = h.shape[0]
    rows = ROUTE_ROWS
    n_sel = PEER_HEADS * PEER_TOPK
    return pl.pallas_call(
        _peer_route_kernel,
        grid=(T // rows, PEER_HEADS),
        in_specs=[
            pl.BlockSpec((rows, D_MODEL), lambda tb, hd: (tb, 0)),
            pl.BlockSpec((1, D_MODEL), lambda tb, hd: (0, 0)),
            pl.BlockSpec((D_MODEL, 2 * PEER_DHALF), lambda tb, hd: (0, hd)),
            pl.BlockSpec((1, 2, N_KEYS, PEER_DHALF), lambda tb, hd: (hd, 0, 0, 0)),
        ],
        out_specs=[
            pl.BlockSpec((rows, D_MODEL), lambda tb, hd: (tb, 0)),
            pl.BlockSpec((PEER_TOPK, rows), lambda tb, hd: (hd, tb)),
            pl.BlockSpec((PEER_TOPK, rows), lambda tb, hd: (hd, tb)),
        ],
        out_shape=[
            jax.ShapeDtypeStruct((T, D_MODEL), jnp.float32),
            jax.ShapeDtypeStruct((n_sel, T), jnp.int32),
            jax.ShapeDtypeStruct((n_sel, T), jnp.float32),
        ],
        scratch_shapes=[pltpu.VMEM((rows, D_MODEL), jnp.bfloat16)],
        compiler_params=pltpu.CompilerParams(
            dimension_semantics=("arbitrary", "arbitrary"), vmem_limit_bytes=VMEM_LIMIT_BYTES),
        name="peer_route",
    )(h, g.reshape(1, D_MODEL), wq_bf16, sk_bf16)


N_SEL = PEER_HEADS * PEER_TOPK
SC_LANES = 16
GATHER_ROWS = PEER_TOPK
GATHERS_PER_TOKEN = N_SEL // GATHER_ROWS
GATHER_BUFS = 4
SC_TOKENS = 16


def _sc_gather_loop(wid, n_batches, tab_hbm, idx_v, bufs, sems, load_batch, compute, store_batch):
    def start(t, kk, b):
        idx = idx_v[t, pl.ds(kk * GATHER_ROWS, GATHER_ROWS)]
        pltpu.async_copy(tab_hbm.at[idx], bufs.at[b], sems.at[b])

    def wait(b):
        pltpu.make_async_copy(tab_hbm.at[pl.ds(0, GATHER_ROWS)], bufs.at[b], sems.at[b]).wait()

    def batch(bi, carry):
        base = (wid * n_batches + bi) * SC_TOKENS
        load_batch(base)
        for q in range(GATHER_BUFS - 1):
            start(q // GATHERS_PER_TOKEN, q % GATHERS_PER_TOKEN, q % GATHER_BUFS)

        def tok(t, carry):
            for kk in range(GATHERS_PER_TOKEN):
                nq = kk + GATHER_BUFS - 1
                nt = t + nq // GATHERS_PER_TOKEN

                @pl.when(nt < SC_TOKENS)
                def _():
                    start(nt, nq % GATHERS_PER_TOKEN, nq % GATHER_BUFS)

                wait(kk % GATHER_BUFS)
                compute(t, kk, kk % GATHER_BUFS)
            return carry

        lax.fori_loop(0, SC_TOKENS, tok, 0)
        store_batch(base)
        return carry

    lax.fori_loop(0, n_batches, batch, 0)


def _sc_mesh_and_batches(n_tokens):
    info = plsc.get_sparse_core_info()
    assert info.num_lanes == SC_LANES
    n_workers = info.num_cores * info.num_subcores
    assert n_tokens % (n_workers * SC_TOKENS) == 0
    mesh = plsc.VectorSubcoreMesh(core_axis_name="c", subcore_axis_name="s")
    return info, mesh, n_tokens // (n_workers * SC_TOKENS)


def _peer_hidden_sc(eid, xn, u_tab):
    T = eid.shape[0]
    info, mesh, n_batches = _sc_mesh_and_batches(T)

    @functools.partial(
        pl.kernel, mesh=mesh,
        out_type=jax.ShapeDtypeStruct((T, N_SEL), jnp.float32),
        scratch_types=[
            pltpu.VMEM((SC_TOKENS, N_SEL), jnp.int32),
            pltpu.VMEM((SC_TOKENS, D_MODEL), jnp.float32),
            pltpu.VMEM((SC_TOKENS, N_SEL), jnp.float32),
            pltpu.VMEM((GATHER_BUFS, GATHER_ROWS, D_MODEL), jnp.float32),
            pltpu.VMEM((GATHER_ROWS, SC_LANES), jnp.float32),
            pltpu.SemaphoreType.DMA((GATHER_BUFS,)),
        ],
        compiler_params=pltpu.CompilerParams(needs_layout_passes=False),
        name="peer_hidden_sc",
    )
    def k(eid_hbm, xn_hbm, u_hbm, out_hbm, idx_v, x_v, o_v, bufs, acc_v, sems):
        wid = lax.axis_index("s") * info.num_cores + lax.axis_index("c")
        lane = lax.iota(jnp.int32, SC_LANES)

        def load_batch(base):
            pltpu.sync_copy(eid_hbm.at[pl.ds(base, SC_TOKENS)], idx_v)
            pltpu.sync_copy(xn_hbm.at[pl.ds(base, SC_TOKENS)], x_v)

        def store_batch(base):
            pltpu.sync_copy(o_v, out_hbm.at[pl.ds(base, SC_TOKENS)])

        def compute(t, kk, b):
            def cbody(c, accs):
                xc = x_v[t, pl.ds(c * SC_LANES, SC_LANES)]
                return tuple(accs[r] + bufs[b, r, pl.ds(c * SC_LANES, SC_LANES)] * xc for r in range(GATHER_ROWS))

            zero = jnp.zeros((SC_LANES,), jnp.float32)
            accs = lax.fori_loop(0, D_MODEL // SC_LANES, cbody, (zero,) * GATHER_ROWS)
            for r in range(GATHER_ROWS):
                acc_v[r, :] = accs[r]
            tot = zero
            for c in range(SC_LANES):
                tot = tot + plsc.load_gather(acc_v, [lane, jnp.full((SC_LANES,), c, jnp.int32)])
            o_v[t, pl.ds(kk * GATHER_ROWS, GATHER_ROWS)] = tot

        _sc_gather_loop(wid, n_batches, u_hbm, idx_v, bufs, sems, load_batch, compute, store_batch)

    return k(eid, xn, u_tab)


def _peer_combine_sc(eid, a, v_tab):
    T = eid.shape[0]
    info, mesh, n_batches = _sc_mesh_and_batches(T)

    @functools.partial(
        pl.kernel, mesh=mesh,
        out_type=jax.ShapeDtypeStruct((T, D_MODEL), jnp.float32),
        scratch_types=[
            pltpu.VMEM((SC_TOKENS, N_SEL), jnp.int32),
            pltpu.VMEM((SC_TOKENS, N_SEL), jnp.float32),
            pltpu.VMEM((SC_TOKENS, D_MODEL), jnp.float32),
            pltpu.VMEM((GATHER_BUFS, GATHER_ROWS, D_MODEL), jnp.float32),
            pltpu.SemaphoreType.DMA((GATHER_BUFS,)),
        ],
        compiler_params=pltpu.CompilerParams(needs_layout_passes=False),
        name="peer_combine_sc",
    )
    def k(eid_hbm, a_hbm, v_hbm, out_hbm, idx_v, a_v, o_v, bufs, sems):
        wid = lax.axis_index("s") * info.num_cores + lax.axis_index("c")

        def load_batch(base):
            pltpu.sync_copy(eid_hbm.at[pl.ds(base, SC_TOKENS)], idx_v)
            pltpu.sync_copy(a_hbm.at[pl.ds(base, SC_TOKENS)], a_v)

        def store_batch(base):
            pltpu.sync_copy(o_v, out_hbm.at[pl.ds(base, SC_TOKENS)])

        def compute(t, kk, b):
            tvec = jnp.full((SC_LANES,), t, jnp.int32)
            ws = [plsc.load_gather(a_v, [tvec, jnp.full((SC_LANES,), kk * GATHER_ROWS + r, jnp.int32)])
                  for r in range(GATHER_ROWS)]

            def cbody(c, carry):
                sl = pl.ds(c * SC_LANES, SC_LANES)
                acc = ws[0] * bufs[b, 0, sl]
                for r in range(1, GATHER_ROWS):
                    acc = acc + ws[r] * bufs[b, r, sl]
                if kk == 0:
                    o_v[t, sl] = acc
                else:
                    o_v[t, sl] = o_v[t, sl] + acc
                return carry

            lax.fori_loop(0, D_MODEL // SC_LANES, cbody, 0)

        _sc_gather_loop(wid, n_batches, v_hbm, idx_v, bufs, sems, load_batch, compute, store_batch)

    return k(eid, a, v_tab)


def _peer_act_kernel(hp_ref, gate_ref, a_ref):
    a_ref[...] = _gelu(hp_ref[...]) * gate_ref[...]


def _peer_act(hpre, gate, rows=1088):
    T = hpre.shape[0]
    spec = pl.BlockSpec((rows, N_SEL), lambda i: (i, 0))
    return pl.pallas_call(
        _peer_act_kernel, grid=(T // rows,), in_specs=[spec, spec], out_specs=spec,
        out_shape=jax.ShapeDtypeStruct((T, N_SEL), jnp.float32),
        compiler_params=pltpu.CompilerParams(dimension_semantics=("arbitrary",)),
        name="peer_act",
    )(hpre, gate)


def _residual_kernel(h_ref, c_ref, g_ref, o_ref, *, final_norm):
    y = h_ref[...] + c_ref[...]
    o_ref[...] = _rmsnorm_rows(y, g_ref[...]) if final_norm else y


def _residual(h, c, gfin, *, final_norm, rows=1088):
    T = h.shape[0]
    spec = pl.BlockSpec((rows, D_MODEL), lambda i: (i, 0))
    return pl.pallas_call(
        functools.partial(_residual_kernel, final_norm=final_norm), grid=(T // rows,),
        in_specs=[spec, spec, pl.BlockSpec((1, D_MODEL), lambda i: (0, 0))], out_specs=spec,
        out_shape=jax.ShapeDtypeStruct((T, D_MODEL), jnp.float32),
        compiler_params=pltpu.CompilerParams(dimension_semantics=("arbitrary",)),
        name="peer_residual",
    )(h, c, gfin.reshape(1, D_MODEL))


def _peer_experts(eid, xn, gate, h, gfin, u_tab, v_tab, *, final_norm):
    hpre = _peer_hidden_sc(eid, xn, u_tab)
    a = _peer_act(hpre, gate)
    c = _peer_combine_sc(eid, a, v_tab)
    return _residual(h, c, gfin, final_norm=final_norm)


KV_WIDTH = N_KV_HEADS * HEAD_DIM
BF16 = jnp.bfloat16


def _qkv(h, gkv, gq, wkv_ref, wq_ref):
    kv = jnp.dot(_rmsnorm_rows(h, gkv).astype(BF16), wkv_ref[...], preferred_element_type=jnp.float32)
    q = jnp.dot(_rmsnorm_rows(h, gq).astype(BF16), wq_ref[...], preferred_element_type=jnp.float32)
    return kv, q


def _sink_softmax_pv(parts, sink):
    m = sink
    for s, _ in parts:
        m = jnp.maximum(m, jnp.max(s, axis=-1, keepdims=True))
    den = jnp.exp(sink - m)
    acc = None
    for s, v in parts:
        e = jnp.exp(s - m)
        den = den + jnp.sum(e, axis=-1, keepdims=True)
        pv = jnp.dot(e.astype(BF16), v.astype(BF16), preferred_element_type=jnp.float32)
        acc = pv if acc is None else acc + pv
    return acc / den


def _nt_dot(a, b):
    return lax.dot_general(a.astype(BF16), b.astype(BF16), (((1,), (1,)), ((), ())),
                           preferred_element_type=jnp.float32)


def _attn_prompt_kernel(h_ref, gkv_ref, gq_ref, wkv_ref, wq_ref, wo_ref, sink_ref,
                        o_ref, kvw_ref, prev_ref):
    blk = pl.program_id(1)
    h = h_ref[...]
    kv, q = _qkv(h, gkv_ref[...], gq_ref[...], wkv_ref, wq_ref)
    kvw_ref[0] = kv

    @pl.when(blk == 0)
    def _():
        prev_ref[...] = jnp.zeros_like(prev_ref)

    prev = prev_ref[...]
    qi = lax.broadcasted_iota(jnp.int32, (WINDOW, WINDOW), 0)
    kj = lax.broadcasted_iota(jnp.int32, (WINDOW, WINDOW), 1)
    prev_ok = jnp.logical_and(kj > qi, blk > 0)
    cur_ok = kj <= qi
    heads = []
    for kvh in range(N_KV_HEADS):
        ks = slice(kvh * HEAD_DIM, (kvh + 1) * HEAD_DIM)
        vs = slice(KV_WIDTH + kvh * HEAD_DIM, KV_WIDTH + (kvh + 1) * HEAD_DIM)
        for g in range(Q_PER_KV):
            hq = kvh * Q_PER_KV + g
            qh = q[:, hq * HEAD_DIM:(hq + 1) * HEAD_DIM]
            sp = jnp.where(prev_ok, _nt_dot(qh, prev[:, ks]) * ATTN_SCALE, -jnp.inf)
            sc = jnp.where(cur_ok, _nt_dot(qh, kv[:, ks]) * ATTN_SCALE, -jnp.inf)
            heads.append(_sink_softmax_pv([(sp, prev[:, vs]), (sc, kv[:, vs])], sink_ref[hq]))
    o = jnp.concatenate(heads, axis=1)
    o_ref[...] = h + jnp.dot(o.astype(BF16), wo_ref[...], preferred_element_type=jnp.float32)
    prev_ref[...] = kv


def _attn_prompt(h, gkv, gq, wkv, wq, wo, sinks, *, n_seq, seq_len):
    nb = seq_len // WINDOW
    row_spec = pl.BlockSpec((WINDOW, D_MODEL), lambda n, b: (n * nb + b, 0))
    full = lambda shape: pl.BlockSpec(shape, lambda n, b: (0,) * len(shape))
    return pl.pallas_call(
        _attn_prompt_kernel,
        grid=(n_seq, nb),
        in_specs=[
            row_spec, full((1, D_MODEL)), full((1, D_MODEL)),
            full((D_MODEL, 2 * KV_WIDTH)), full((D_MODEL, D_MODEL)), full((D_MODEL, D_MODEL)),
            pl.BlockSpec(memory_space=pltpu.SMEM),
        ],
        out_specs=[row_spec, pl.BlockSpec((1, WINDOW, 2 * KV_WIDTH), lambda n, b: (n, 0, 0))],
        out_shape=[
            jax.ShapeDtypeStruct(h.shape, jnp.float32),
            jax.ShapeDtypeStruct((n_seq, WINDOW, 2 * KV_WIDTH), jnp.float32),
        ],
        scratch_shapes=[pltpu.VMEM((WINDOW, 2 * KV_WIDTH), jnp.float32)],
        compiler_params=pltpu.CompilerParams(
            dimension_semantics=("arbitrary", "arbitrary"), vmem_limit_bytes=VMEM_LIMIT_BYTES),
        name="attn_prompt",
    )(h, gkv.reshape(1, D_MODEL), gq.reshape(1, D_MODEL), wkv, wq, wo, sinks)


ATTN_SEQS = 16


def _attn_sample_kernel(h_ref, ck_ref, cv_ref, gkv_ref, gq_ref, wkv_ref, wq_ref, wo_ref, sink_ref,
                        o_ref, kw_ref, vw_ref, kv_ref, q_ref, att_ref, *, n_new):
    h = h_ref[...]
    kv, q = _qkv(h, gkv_ref[...], gq_ref[...], wkv_ref, wq_ref)
    kv_ref[...] = kv
    q_ref[...] = q
    n_seq = h.shape[0] // n_new
    rows = Q_PER_KV * n_new
    qpos_c = lax.broadcasted_iota(jnp.int32, (rows, WINDOW), 0) % n_new
    cache_ok = lax.broadcasted_iota(jnp.int32, (rows, WINDOW), 1) > qpos_c
    qpos_n = lax.broadcasted_iota(jnp.int32, (rows, n_new), 0) % n_new
    new_ok = lax.broadcasted_iota(jnp.int32, (rows, n_new), 1) <= qpos_n

    def per_seq(n, carry):
        r0 = pl.multiple_of(n * n_new, n_new)
        kvn = kv_ref[pl.ds(r0, n_new), :]
        qn = q_ref[pl.ds(r0, n_new), :]
        ck = ck_ref[n]
        cv = cv_ref[n]
        kw_ref[n, 0:WINDOW - n_new, :] = ck[n_new:, :]
        kw_ref[n, WINDOW - n_new:WINDOW, :] = kvn[:, 0:KV_WIDTH]
        vw_ref[n, 0:WINDOW - n_new, :] = cv[n_new:, :]
        vw_ref[n, WINDOW - n_new:WINDOW, :] = kvn[:, KV_WIDTH:2 * KV_WIDTH]
        outs = []
        for kvh in range(N_KV_HEADS):
            ks = slice(kvh * HEAD_DIM, (kvh + 1) * HEAD_DIM)
            vs = slice(KV_WIDTH + kvh * HEAD_DIM, KV_WIDTH + (kvh + 1) * HEAD_DIM)
            qs = jnp.concatenate(
                [qn[:, (kvh * Q_PER_KV + g) * HEAD_DIM:(kvh * Q_PER_KV + g + 1) * HEAD_DIM] for g in range(Q_PER_KV)],
                axis=0)
            s_c = jnp.where(cache_ok, _nt_dot(qs, ck[:, ks]) * ATTN_SCALE, -jnp.inf)
            s_n = jnp.where(new_ok, _nt_dot(qs, kvn[:, ks]) * ATTN_SCALE, -jnp.inf)
            sink = sink_ref[kvh][:, 0:1]
            o = _sink_softmax_pv([(s_c, cv[:, ks]), (s_n, kvn[:, vs])], sink)
            outs += [o[g * n_new:(g + 1) * n_new, :] for g in range(Q_PER_KV)]
        att_ref[pl.ds(r0, n_new), :] = jnp.concatenate(outs, axis=1)
        return carry

    lax.fori_loop(0, n_seq, per_seq, 0)
    o_ref[...] = h + jnp.dot(att_ref[...].astype(BF16), wo_ref[...], preferred_element_type=jnp.float32)


def _attn_sample(h, cache_k, cache_v, gkv, gq, wkv, wq, wo, sinks, *, n_new):
    n_seq = cache_k.shape[0]
    sb = ATTN_SEQS
    rows = sb * n_new
    row_spec = pl.BlockSpec((rows, D_MODEL), lambda i: (i, 0))
    win_spec = pl.BlockSpec((sb, WINDOW, KV_WIDTH), lambda i: (i, 0, 0))
    full = lambda shape: pl.BlockSpec(shape, lambda i: (0,) * len(shape))
    sink_rows = jnp.repeat(sinks.reshape(N_KV_HEADS, Q_PER_KV), n_new, axis=1)[:, :, None]
    sink_rows = jnp.broadcast_to(sink_rows, (N_KV_HEADS, Q_PER_KV * n_new, LANES))
    return pl.pallas_call(
        functools.partial(_attn_sample_kernel, n_new=n_new),
        grid=(n_seq // sb,),
        in_specs=[
            row_spec, win_spec, win_spec, full((1, D_MODEL)), full((1, D_MODEL)),
            full((D_MODEL, 2 * KV_WIDTH)), full((D_MODEL, D_MODEL)), full((D_MODEL, D_MODEL)),
            full((N_KV_HEADS, Q_PER_KV * n_new, LANES)),
        ],
        out_specs=[row_spec, win_spec, win_spec],
        out_shape=[
            jax.ShapeDtypeStruct(h.shape, jnp.float32),
            jax.ShapeDtypeStruct((n_seq, WINDOW, KV_WIDTH), jnp.float32),
            jax.ShapeDtypeStruct((n_seq, WINDOW, KV_WIDTH), jnp.float32),
        ],
        scratch_shapes=[
            pltpu.VMEM((rows, 2 * KV_WIDTH), jnp.float32),
            pltpu.VMEM((rows, D_MODEL), jnp.float32),
            pltpu.VMEM((rows, D_MODEL), jnp.float32),
        ],
        compiler_params=pltpu.CompilerParams(
            dimension_semantics=("arbitrary",), vmem_limit_bytes=VMEM_LIMIT_BYTES),
        name="attn_sample",
    )(h, cache_k, cache_v, gkv.reshape(1, D_MODEL), gq.reshape(1, D_MODEL), wkv, wq, wo, sink_rows)


def kernel(x_prompt, x_sample, state_ssm_re, state_ssm_im, cache_k_win, cache_v_win, norm_mix, norm_ffn, norm_kv, norm_final, ssm_lam_re, ssm_lam_im, ssm_log_dt, ssm_b_re, ssm_b_im, ssm_c_re, ssm_c_im, ssm_d, ssm_w_glu, w_kv, w_q, attn_sinks, w_o, peer_w_q, peer_sub_keys, peer_u, peer_v):
    bmat, cmat, apr, api = _s5_discretize(ssm_lam_re[0], ssm_lam_im[0], ssm_log_dt[0], ssm_b_re[0], ssm_b_im[0], ssm_c_re[0], ssm_c_im[0])
    cmat = cmat.astype(jnp.bfloat16)
    wglu = ssm_w_glu[0].astype(jnp.bfloat16)
    xp = x_prompt.reshape(-1, D_MODEL)
    xs = x_sample.reshape(-1, D_MODEL)
    z0 = jnp.zeros((x_prompt.shape[0], N_CHUNKS, 1, CHUNK_STATE), jnp.float32)
    zp, srp, sip = _s5_mixer(xp, z0, z0, norm_mix[0], ssm_d[0], bmat, cmat, apr, api, chain=True, seq_len=x_prompt.shape[1])
    zs, srs, sis = _s5_mixer(xs, _state_to_chunks(state_ssm_re[0]), _state_to_chunks(state_ssm_im[0]), norm_mix[0], ssm_d[0], bmat, cmat, apr, api, chain=False, seq_len=x_sample.shape[1])
    h1p = _glu_residual(zp, xp, wglu)
    h1s = _glu_residual(zs, xs, wglu)
    h1 = jnp.concatenate([h1p, h1s], axis=0)
    n_prompt_rows = xp.shape[0]

    def peer(h, layer, final_norm):
        xn, eid_t, gate_t = _peer_route(h, norm_ffn[layer], peer_w_q[layer].astype(BF16),
                                        peer_sub_keys[layer].astype(BF16))
        return _peer_experts(eid_t.T, xn, gate_t.T, h, norm_final, peer_u[layer], peer_v[layer],
                             final_norm=final_norm)

    h2 = peer(h1, 0, False)
    wkv = w_kv.astype(BF16)
    wq = w_q[0].astype(BF16)
    wo = w_o[0].astype(BF16)
    n_dec = x_sample.shape[0]
    h3p, kvw_p = _attn_prompt(h2[:n_prompt_rows], norm_kv, norm_mix[1], wkv, wq, wo, attn_sinks[0],
                              n_seq=x_prompt.shape[0], seq_len=x_prompt.shape[1])
    h3s, kw_s, vw_s = _attn_sample(h2[n_prompt_rows:], cache_k_win.reshape(n_dec, WINDOW, KV_WIDTH),
                                   cache_v_win.reshape(n_dec, WINDOW, KV_WIDTH), norm_kv, norm_mix[1],
                                   wkv, wq, wo, attn_sinks[0], n_new=x_sample.shape[1])
    y = peer(jnp.concatenate([h3p, h3s], axis=0), 1, True)
    win = lambda a: a.reshape(a.shape[0], WINDOW, N_KV_HEADS, HEAD_DIM)
    return (y[:n_prompt_rows].reshape(x_prompt.shape), y[n_prompt_rows:].reshape(x_sample.shape),
            _chunks_to_state(srp)[None], _chunks_to_state(sip)[None],
            win(kvw_p[:, :, :KV_WIDTH]), win(kvw_p[:, :, KV_WIDTH:]),
            _chunks_to_state(srs)[None], _chunks_to_state(sis)[None],
            win(kw_s), win(vw_s))
```

```python
import functools
import math

import jax
import jax.numpy as jnp
from jax import lax
from jax.experimental import pallas as pl
from jax.experimental.pallas import tpu as pltpu
from jax.experimental.pallas import tpu_sc as plsc

D_MODEL = 1024
GROUP_SIZE = 16
N_GROUPS = D_MODEL // GROUP_SIZE
STATE_DIM = 64
HEAD_DIM = 64
N_Q_HEADS = D_MODEL // HEAD_DIM
N_KV_HEADS = N_Q_HEADS // 8
Q_PER_KV = N_Q_HEADS // N_KV_HEADS
WINDOW = 128
PAST_LEN = 16384
ATTN_SCALE = 1.0 / math.sqrt(HEAD_DIM)
PEER_HEADS = 8
N_KEYS = 128
PEER_TOPK = 16
PEER_DHALF = 128
EPS = 1e-5

LANES = 128
SUBLANES = 8
VMEM_LIMIT_BYTES = 56 * 1024 * 1024

GROUPS_PER_CHUNK = LANES // GROUP_SIZE
N_CHUNKS = N_GROUPS // GROUPS_PER_CHUNK
CHUNK_STATE = GROUPS_PER_CHUNK * STATE_DIM
S5_ROWS = 256


def _rmsnorm_rows(x, g):
    r = lax.rsqrt(jnp.mean(x * x, axis=-1, keepdims=True) + EPS)
    return x * r * g


def _gelu(x):
    return 0.5 * x * (1.0 + lax.erf(x * (1.0 / math.sqrt(2.0))))


def _s5_discretize(lam_re, lam_im, log_dt, b_re, b_im, c_re, c_im):
    f32 = jnp.float32
    lr = lam_re.astype(f32)
    li = lam_im.astype(f32)
    dt = jnp.exp(log_dt.astype(f32))[:, None]
    mag = jnp.exp(lr * dt)
    ab_re = mag * jnp.cos(li * dt)
    ab_im = mag * jnp.sin(li * dt)
    den = lr * lr + li * li
    f_re = ((ab_re - 1.0) * lr + ab_im * li) / den
    f_im = (ab_im * lr - (ab_re - 1.0) * li) / den
    br = b_re.astype(f32)
    bi = b_im.astype(f32)
    bb_re = f_re[..., None] * br - f_im[..., None] * bi
    bb_im = f_re[..., None] * bi + f_im[..., None] * br
    eye = jnp.eye(GROUPS_PER_CHUNK, dtype=f32)

    def chunk_rows(v):
        return v.reshape(N_CHUNKS, 1, CHUNK_STATE)

    def in_blocks(bb):
        t = bb.reshape(N_CHUNKS, GROUPS_PER_CHUNK, STATE_DIM, GROUP_SIZE).transpose(0, 1, 3, 2)
        return jnp.einsum('mgjp,gh->mgjhp', t, eye).reshape(N_CHUNKS, LANES, CHUNK_STATE)

    def out_blocks(c):
        t = c.astype(f32).reshape(N_CHUNKS, GROUPS_PER_CHUNK, GROUP_SIZE, STATE_DIM).transpose(0, 1, 3, 2)
        return jnp.einsum('mgpj,gh->mgphj', t, eye).reshape(N_CHUNKS, CHUNK_STATE, LANES)

    bmat = jnp.concatenate([in_blocks(bb_re), in_blocks(bb_im)], axis=2)
    cmat = jnp.concatenate([out_blocks(c_re), -out_blocks(c_im)], axis=1)
    pr, pi = [ab_re], [ab_im]
    for _ in range(SUBLANES - 1):
        pr, pi = pr + [pr[-1] * ab_re - pi[-1] * ab_im], pi + [pr[-1] * ab_im + pi[-1] * ab_re]
    apr = jnp.concatenate([chunk_rows(v) for v in pr], axis=1)
    api = jnp.concatenate([chunk_rows(v) for v in pi], axis=1)
    return bmat, cmat, apr, api


def _s5_kernel(x_ref, g_ref, d_ref, b_ref, c_ref, apr_ref, api_ref, h0r_ref, h0i_ref,
               z_ref, sr_ref, si_ref, u_ref, st_ref, cr_ref, ci_ref, *, chain, blocks_per_seq):
    rb = pl.program_id(0)
    m = pl.program_id(1)
    rows = x_ref.shape[0]
    n_tiles = rows // SUBLANES

    @pl.when(m == 0)
    def _():
        u = _rmsnorm_rows(x_ref[...], g_ref[...])
        for mm in range(N_CHUNKS):
            u_ref[mm] = u[:, mm * LANES:(mm + 1) * LANES]

    u = u_ref[m]
    st_ref[...] = jnp.dot(u, b_ref[0], preferred_element_type=jnp.float32, precision=lax.Precision.HIGHEST)

    apr = apr_ref[0]
    api = api_ref[0]
    row = lax.broadcasted_iota(jnp.int32, (SUBLANES, CHUNK_STATE), 0)

    if chain:
        @pl.when(rb % blocks_per_seq == 0)
        def _():
            cr_ref[m] = h0r_ref[0, 0]
            ci_ref[m] = h0i_ref[0, 0]

    def tile_step(k, carry):
        r0 = pl.multiple_of(k * SUBLANES, SUBLANES)
        xr = st_ref[pl.ds(r0, SUBLANES), 0:CHUNK_STATE]
        xi = st_ref[pl.ds(r0, SUBLANES), CHUNK_STATE:2 * CHUNK_STATE]
        for d in (1, 2, 4):
            ar = apr[d - 1:d, :]
            ai = api[d - 1:d, :]
            sr = jnp.where(row >= d, pltpu.roll(xr, d, axis=0), 0.0)
            si = jnp.where(row >= d, pltpu.roll(xi, d, axis=0), 0.0)
            xr, xi = xr + ar * sr - ai * si, xi + ar * si + ai * sr
        if chain:
            cr, ci = carry
        else:
            cr = h0r_ref[k, 0]
            ci = h0i_ref[k, 0]
        hr = xr + apr * cr - api * ci
        hi = xi + apr * ci + api * cr
        st_ref[pl.ds(r0, SUBLANES), 0:CHUNK_STATE] = hr
        st_ref[pl.ds(r0, SUBLANES), CHUNK_STATE:2 * CHUNK_STATE] = hi
        lr_ = hr[SUBLANES - 1:SUBLANES, :]
        li_ = hi[SUBLANES - 1:SUBLANES, :]
        if chain:
            return lr_, li_
        sr_ref[k, m] = lr_
        si_ref[k, m] = li_
        return carry

    if chain:
        cr, ci = lax.fori_loop(0, n_tiles, tile_step, (cr_ref[m], ci_ref[m]))
        cr_ref[m] = cr
        ci_ref[m] = ci
        sr_ref[0, m] = cr
        si_ref[0, m] = ci
    else:
        lax.fori_loop(0, n_tiles, tile_step, 0)

    y = jnp.dot(st_ref[...].astype(jnp.bfloat16), c_ref[0], preferred_element_type=jnp.float32)
    y = y + d_ref[0] * u
    z_ref[...] = _gelu(y)


def _s5_mixer(x, h0r, h0i, g, d_skip, bmat, cmat_bf16, apr, api, *, chain, seq_len):
    T = x.shape[0]
    nseq = h0r.shape[0]
    rows = S5_ROWS
    if chain:
        blocks_per_seq = seq_len // rows
        seq_blk = 1
        seq_map = lambda rb, m: (rb // blocks_per_seq, m, 0, 0)
        out_map = lambda rb, m: (rb // blocks_per_seq, 0, 0, 0)
    else:
        assert seq_len == SUBLANES
        blocks_per_seq = 1
        seq_blk = rows // SUBLANES
        seq_map = lambda rb, m: (rb, m, 0, 0)
        out_map = lambda rb, m: (rb, 0, 0, 0)
    grid = (T // rows, N_CHUNKS)
    kern = functools.partial(_s5_kernel, chain=chain, blocks_per_seq=blocks_per_seq)
    st_spec = pl.BlockSpec((seq_blk, 1, 1, CHUNK_STATE), seq_map)
    out_st_spec = pl.BlockSpec((seq_blk, N_CHUNKS, 1, CHUNK_STATE), out_map)
    z, sr, si = pl.pallas_call(
        kern,
        grid=grid,
        in_specs=[
            pl.BlockSpec((rows, D_MODEL), lambda rb, m: (rb, 0)),
            pl.BlockSpec((1, D_MODEL), lambda rb, m: (0, 0)),
            pl.BlockSpec((1, 1, LANES), lambda rb, m: (m, 0, 0)),
            pl.BlockSpec((1, LANES, 2 * CHUNK_STATE), lambda rb, m: (m, 0, 0)),
            pl.BlockSpec((1, 2 * CHUNK_STATE, LANES), lambda rb, m: (m, 0, 0)),
            pl.BlockSpec((1, SUBLANES, CHUNK_STATE), lambda rb, m: (m, 0, 0)),
            pl.BlockSpec((1, SUBLANES, CHUNK_STATE), lambda rb, m: (m, 0, 0)),
            st_spec, st_spec,
        ],
        out_specs=[
            pl.BlockSpec((rows, LANES), lambda rb, m: (rb, m)),
            out_st_spec, out_st_spec,
        ],
        out_shape=[
            jax.ShapeDtypeStruct((T, D_MODEL), jnp.float32),
            jax.ShapeDtypeStruct((nseq, N_CHUNKS, 1, CHUNK_STATE), jnp.float32),
            jax.ShapeDtypeStruct((nseq, N_CHUNKS, 1, CHUNK_STATE), jnp.float32),
        ],
        scratch_shapes=[
            pltpu.VMEM((N_CHUNKS, rows, LANES), jnp.float32),
            pltpu.VMEM((rows, 2 * CHUNK_STATE), jnp.float32),
            pltpu.VMEM((N_CHUNKS, 1, CHUNK_STATE), jnp.float32),
            pltpu.VMEM((N_CHUNKS, 1, CHUNK_STATE), jnp.float32),
        ],
        compiler_params=pltpu.CompilerParams(
            dimension_semantics=("arbitrary", "arbitrary"), vmem_limit_bytes=VMEM_LIMIT_BYTES),
        name="s5_mixer",
    )(x, g.reshape(1, D_MODEL), d_skip.reshape(N_CHUNKS, 1, LANES), bmat, cmat_bf16, apr, api, h0r, h0i)
    return z, sr, si


def _state_to_chunks(h):
    return h.reshape(h.shape[0], N_CHUNKS, 1, CHUNK_STATE)


def _chunks_to_state(s):
    return s.reshape(s.shape[0], N_GROUPS, STATE_DIM)


def _glu_kernel(z_ref, res_ref, w_ref, o_ref):
    zz = jnp.dot(z_ref[...].astype(jnp.bfloat16), w_ref[...], preferred_element_type=jnp.float32)
    a = zz[:, :D_MODEL]
    b = zz[:, D_MODEL:]
    o_ref[...] = res_ref[...] + a * (1.0 / (1.0 + jnp.exp(-b)))


def _glu_residual(z, res, w_bf16, rows=512):
    T = z.shape[0]
    return pl.pallas_call(
        _glu_kernel,
        grid=(T // rows,),
        in_specs=[
            pl.BlockSpec((rows, D_MODEL), lambda i: (i, 0)),
            pl.BlockSpec((rows, D_MODEL), lambda i: (i, 0)),
            pl.BlockSpec((D_MODEL, 2 * D_MODEL), lambda i: (0, 0)),
        ],
        out_specs=pl.BlockSpec((rows, D_MODEL), lambda i: (i, 0)),
        out_shape=jax.ShapeDtypeStruct((T, D_MODEL), jnp.float32),
        compiler_params=pltpu.CompilerParams(
            dimension_semantics=("arbitrary",), vmem_limit_bytes=VMEM_LIMIT_BYTES),
        name="glu_residual",
    )(z, res, w_bf16)


ROUTE_ROWS = 256
CAND = PEER_TOPK * PEER_TOPK


def _topk_rows(s, payload):
    n_rows = s.shape[0]
    row = lax.broadcasted_iota(jnp.int32, s.shape, 0)
    vals, picks = [], []
    for _ in range(PEER_TOPK):
        m = jnp.max(s, axis=0, keepdims=True)
        pos = jnp.min(jnp.where(s == m, row, n_rows), axis=0, keepdims=True)
        sel = row == pos
        vals.append(m)
        if payload is None:
            picks.append(pos)
        else:
            picks.append(jnp.max(jnp.where(sel, payload, -1), axis=0, keepdims=True))
        s = jnp.where(sel, -jnp.inf, s)
    return jnp.concatenate(vals, axis=0), jnp.concatenate(picks, axis=0)


def _peer_route_kernel(h_ref, g_ref, wq_ref, sk_ref, xn_ref, eid_ref, gate_ref, xb_ref):
    hd = pl.program_id(1)

    @pl.when(hd == 0)
    def _():
        xn = _rmsnorm_rows(h_ref[...], g_ref[...])
        xn_ref[...] = xn
        xb_ref[...] = xn.astype(jnp.bfloat16)

    q = jnp.dot(xb_ref[...], wq_ref[...], preferred_element_type=jnp.float32)
    sv, si = [], []
    for c in range(2):
        qc = q[:, c * PEER_DHALF:(c + 1) * PEER_DHALF].astype(jnp.bfloat16)
        st = lax.dot_general(sk_ref[0, c], qc, (((1,), (1,)), ((), ())), preferred_element_type=jnp.float32)
        v, i = _topk_rows(st, None)
        sv.append(v)
        si.append(i)
    cand = jnp.concatenate([sv[0][i:i + 1, :] + sv[1] for i in range(PEER_TOPK)], axis=0)
    cid = jnp.concatenate([si[0][i:i + 1, :] * N_KEYS + si[1] for i in range(PEER_TOPK)], axis=0)
    fv, eid = _topk_rows(cand, cid)
    e = jnp.exp(fv - fv[0:1, :])
    gate_ref[...] = e / jnp.sum(e, axis=0, keepdims=True)
    eid_ref[...] = eid


def _peer_route(h, g, wq_bf16, sk_bf16):
    T = h.shape[0]
    rows = ROUTE_ROWS
    n_sel = PEER_HEADS * PEER_TOPK
    return pl.pallas_call(
        _peer_route_kernel,
        grid=(T // rows, PEER_HEADS),
        in_specs=[
            pl.BlockSpec((rows, D_MODEL), lambda tb, hd: (tb, 0)),
            pl.BlockSpec((1, D_MODEL), lambda tb, hd: (0, 0)),
            pl.BlockSpec((D_MODEL, 2 * PEER_DHALF), lambda tb, hd: (0, hd)),
            pl.BlockSpec((1, 2, N_KEYS, PEER_DHALF), lambda tb, hd: (hd, 0, 0, 0)),
        ],
        out_specs=[
            pl.BlockSpec((rows, D_MODEL), lambda tb, hd: (tb, 0)),
            pl.BlockSpec((PEER_TOPK, rows), lambda tb, hd: (hd, tb)),
            pl.BlockSpec((PEER_TOPK, rows), lambda tb, hd: (hd, tb)),
        ],
        out_shape=[
            jax.ShapeDtypeStruct((T, D_MODEL), jnp.float32),
            jax.ShapeDtypeStruct((n_sel, T), jnp.int32),
            jax.ShapeDtypeStruct((n_sel, T), jnp.float32),
        ],
        scratch_shapes=[pltpu.VMEM((rows, D_MODEL), jnp.bfloat16)],
        compiler_params=pltpu.CompilerParams(
            dimension_semantics=("arbitrary", "arbitrary"), vmem_limit_bytes=VMEM_LIMIT_BYTES),
        name="peer_route",
    )(h, g.reshape(1, D_MODEL), wq_bf16, sk_bf16)


N_SEL = PEER_HEADS * PEER_TOPK
SC_LANES = 16
GATHER_ROWS = PEER_TOPK
GATHERS_PER_TOKEN = N_SEL // GATHER_ROWS
GATHER_BUFS = 4
SC_TOKENS = 16


def _sc_gather_loop(wid, n_batches, tab_hbm, idx_v, bufs, sems, load_batch, compute, store_batch):
    def start(t, kk, b):
        idx = idx_v[t, pl.ds(kk * GATHER_ROWS, GATHER_ROWS)]
        pltpu.async_copy(tab_hbm.at[idx], bufs.at[b], sems.at[b])

    def wait(b):
        pltpu.make_async_copy(tab_hbm.at[pl.ds(0, GATHER_ROWS)], bufs.at[b], sems.at[b]).wait()

    def batch(bi, carry):
        base = (wid * n_batches + bi) * SC_TOKENS
        load_batch(base)
        for q in range(GATHER_BUFS - 1):
            start(q // GATHERS_PER_TOKEN, q % GATHERS_PER_TOKEN, q % GATHER_BUFS)

        def tok(t, carry):
            for kk in range(GATHERS_PER_TOKEN):
                nq = kk + GATHER_BUFS - 1
                nt = t + nq // GATHERS_PER_TOKEN

                @pl.when(nt < SC_TOKENS)
                def _():
                    start(nt, nq % GATHERS_PER_TOKEN, nq % GATHER_BUFS)

                wait(kk % GATHER_BUFS)
                compute(t, kk, kk % GATHER_BUFS)
            return carry

        lax.fori_loop(0, SC_TOKENS, tok, 0)
        store_batch(base)
        return carry

    lax.fori_loop(0, n_batches, batch, 0)


def _sc_mesh_and_batches(n_tokens):
    info = plsc.get_sparse_core_info()
    assert info.num_lanes == SC_LANES
    n_workers = info.num_cores * info.num_subcores
    assert n_tokens % (n_workers * SC_TOKENS) == 0
    mesh = plsc.VectorSubcoreMesh(core_axis_name="c", subcore_axis_name="s")
    return info, mesh, n_tokens // (n_workers * SC_TOKENS)


def _peer_hidden_sc(eid, xn, u_tab):
    T = eid.shape[0]
    info, mesh, n_batches = _sc_mesh_and_batches(T)

    @functools.partial(
        pl.kernel, mesh=mesh,
        out_type=jax.ShapeDtypeStruct((T, N_SEL), jnp.float32),
        scratch_types=[
            pltpu.VMEM((SC_TOKENS, N_SEL), jnp.int32),
            pltpu.VMEM((SC_TOKENS, D_MODEL), jnp.float32),
            pltpu.VMEM((SC_TOKENS, N_SEL), jnp.float32),
            pltpu.VMEM((GATHER_BUFS, GATHER_ROWS, D_MODEL), jnp.float32),
            pltpu.VMEM((GATHER_ROWS, SC_LANES), jnp.float32),
            pltpu.SemaphoreType.DMA((GATHER_BUFS,)),
        ],
        compiler_params=pltpu.CompilerParams(needs_layout_passes=False),
        name="peer_hidden_sc",
    )
    def k(eid_hbm, xn_hbm, u_hbm, out_hbm, idx_v, x_v, o_v, bufs, acc_v, sems):
        wid = lax.axis_index("s") * info.num_cores + lax.axis_index("c")
        lane = lax.iota(jnp.int32, SC_LANES)

        def load_batch(base):
            pltpu.sync_copy(eid_hbm.at[pl.ds(base, SC_TOKENS)], idx_v)
            pltpu.sync_copy(xn_hbm.at[pl.ds(base, SC_TOKENS)], x_v)

        def store_batch(base):
            pltpu.sync_copy(o_v, out_hbm.at[pl.ds(base, SC_TOKENS)])

        def compute(t, kk, b):
            zero = jnp.zeros((SC_LANES,), jnp.float32)

            @plsc.parallel_loop(0, D_MODEL // SC_LANES, carry=(zero,) * GATHER_ROWS)
            def accs(c, accs):
                xc = x_v[t, pl.ds(c * SC_LANES, SC_LANES)]
                return tuple(accs[r] + bufs[b, r, pl.ds(c * SC_LANES, SC_LANES)] * xc for r in range(GATHER_ROWS))

            for r in range(GATHER_ROWS):
                acc_v[r, :] = accs[r]
            tot = zero
            for c in range(SC_LANES):
                tot = tot + plsc.load_gather(acc_v, [lane, jnp.full((SC_LANES,), c, jnp.int32)])
            o_v[t, pl.ds(kk * GATHER_ROWS, GATHER_ROWS)] = tot

        _sc_gather_loop(wid, n_batches, u_hbm, idx_v, bufs, sems, load_batch, compute, store_batch)

    return k(eid, xn, u_tab)


def _peer_combine_sc(eid, a, v_tab):
    T = eid.shape[0]
    info, mesh, n_batches = _sc_mesh_and_batches(T)

    @functools.partial(
        pl.kernel, mesh=mesh,
        out_type=jax.ShapeDtypeStruct((T, D_MODEL), jnp.float32),
        scratch_types=[
            pltpu.VMEM((SC_TOKENS, N_SEL), jnp.int32),
            pltpu.VMEM((SC_TOKENS, N_SEL), jnp.float32),
            pltpu.VMEM((SC_TOKENS, D_MODEL), jnp.float32),
            pltpu.VMEM((GATHER_BUFS, GATHER_ROWS, D_MODEL), jnp.float32),
            pltpu.SemaphoreType.DMA((GATHER_BUFS,)),
        ],
        compiler_params=pltpu.CompilerParams(needs_layout_passes=False),
        name="peer_combine_sc",
    )
    def k(eid_hbm, a_hbm, v_hbm, out_hbm, idx_v, a_v, o_v, bufs, sems):
        wid = lax.axis_index("s") * info.num_cores + lax.axis_index("c")

        def load_batch(base):
            pltpu.sync_copy(eid_hbm.at[pl.ds(base, SC_TOKENS)], idx_v)
            pltpu.sync_copy(a_hbm.at[pl.ds(base, SC_TOKENS)], a_v)

        def store_batch(base):
            pltpu.sync_copy(o_v, out_hbm.at[pl.ds(base, SC_TOKENS)])

        def compute(t, kk, b):
            tvec = jnp.full((SC_LANES,), t, jnp.int32)
            ws = [plsc.load_gather(a_v, [tvec, jnp.full((SC_LANES,), kk * GATHER_ROWS + r, jnp.int32)])
                  for r in range(GATHER_ROWS)]

            @plsc.parallel_loop(0, D_MODEL // SC_LANES, unroll=2)
            def _(c):
                sl = pl.ds(c * SC_LANES, SC_LANES)
                terms = [ws[r] * bufs[b, r, sl] for r in range(GATHER_ROWS)]
                if kk != 0:
                    terms.append(o_v[t, sl])
                while len(terms) > 1:
                    pairs = [terms[i] + terms[i + 1] for i in range(0, len(terms) - 1, 2)]
                    terms = pairs + ([terms[-1]] if len(terms) % 2 else [])
                o_v[t, sl] = terms[0]

        _sc_gather_loop(wid, n_batches, v_hbm, idx_v, bufs, sems, load_batch, compute, store_batch)

    return k(eid, a, v_tab)


def _peer_act_kernel(hp_ref, gate_ref, a_ref):
    a_ref[...] = _gelu(hp_ref[...]) * gate_ref[...]


def _peer_act(hpre, gate, rows=1088):
    T = hpre.shape[0]
    spec = pl.BlockSpec((rows, N_SEL), lambda i: (i, 0))
    return pl.pallas_call(
        _peer_act_kernel, grid=(T // rows,), in_specs=[spec, spec], out_specs=spec,
        out_shape=jax.ShapeDtypeStruct((T, N_SEL), jnp.float32),
        compiler_params=pltpu.CompilerParams(dimension_semantics=("arbitrary",)),
        name="peer_act",
    )(hpre, gate)


def _residual_kernel(h_ref, c_ref, g_ref, o_ref, *, final_norm):
    y = h_ref[...] + c_ref[...]
    o_ref[...] = _rmsnorm_rows(y, g_ref[...]) if final_norm else y


def _residual(h, c, gfin, *, final_norm, rows=1088):
    T = h.shape[0]
    spec = pl.BlockSpec((rows, D_MODEL), lambda i: (i, 0))
    return pl.pallas_call(
        functools.partial(_residual_kernel, final_norm=final_norm), grid=(T // rows,),
        in_specs=[spec, spec, pl.BlockSpec((1, D_MODEL), lambda i: (0, 0))], out_specs=spec,
        out_shape=jax.ShapeDtypeStruct((T, D_MODEL), jnp.float32),
        compiler_params=pltpu.CompilerParams(dimension_semantics=("arbitrary",)),
        name="peer_residual",
    )(h, c, gfin.reshape(1, D_MODEL))


def _peer_experts(eid, xn, gate, h, gfin, u_tab, v_tab, *, final_norm):
    hpre = _peer_hidden_sc(eid, xn, u_tab)
    a = _peer_act(hpre, gate)
    c = _peer_combine_sc(eid, a, v_tab)
    return _residual(h, c, gfin, final_norm=final_norm)


KV_WIDTH = N_KV_HEADS * HEAD_DIM
BF16 = jnp.bfloat16


def _qkv(h, gkv, gq, wkv_ref, wq_ref):
    kv = jnp.dot(_rmsnorm_rows(h, gkv).astype(BF16), wkv_ref[...], preferred_element_type=jnp.float32)
    q = jnp.dot(_rmsnorm_rows(h, gq).astype(BF16), wq_ref[...], preferred_element_type=jnp.float32)
    return kv, q


def _sink_softmax_pv(parts, sink):
    m = sink
    for s, _ in parts:
        m = jnp.maximum(m, jnp.max(s, axis=-1, keepdims=True))
    den = jnp.exp(sink - m)
    acc = None
    for s, v in parts:
        e = jnp.exp(s - m)
        den = den + jnp.sum(e, axis=-1, keepdims=True)
        pv = jnp.dot(e.astype(BF16), v.astype(BF16), preferred_element_type=jnp.float32)
        acc = pv if acc is None else acc + pv
    return acc / den


def _nt_dot(a, b):
    return lax.dot_general(a.astype(BF16), b.astype(BF16), (((1,), (1,)), ((), ())),
                           preferred_element_type=jnp.float32)


def _attn_prompt_kernel(h_ref, gkv_ref, gq_ref, wkv_ref, wq_ref, wo_ref, sink_ref,
                        o_ref, kvw_ref, prev_ref):
    blk = pl.program_id(1)
    h = h_ref[...]
    kv, q = _qkv(h, gkv_ref[...], gq_ref[...], wkv_ref, wq_ref)
    kvw_ref[0] = kv

    @pl.when(blk == 0)
    def _():
        prev_ref[...] = jnp.zeros_like(prev_ref)

    prev = prev_ref[...]
    qi = lax.broadcasted_iota(jnp.int32, (WINDOW, WINDOW), 0)
    kj = lax.broadcasted_iota(jnp.int32, (WINDOW, WINDOW), 1)
    prev_ok = jnp.logical_and(kj > qi, blk > 0)
    cur_ok = kj <= qi
    heads = []
    for kvh in range(N_KV_HEADS):
        ks = slice(kvh * HEAD_DIM, (kvh + 1) * HEAD_DIM)
        vs = slice(KV_WIDTH + kvh * HEAD_DIM, KV_WIDTH + (kvh + 1) * HEAD_DIM)
        for g in range(Q_PER_KV):
            hq = kvh * Q_PER_KV + g
            qh = q[:, hq * HEAD_DIM:(hq + 1) * HEAD_DIM]
            sp = jnp.where(prev_ok, _nt_dot(qh, prev[:, ks]) * ATTN_SCALE, -jnp.inf)
            sc = jnp.where(cur_ok, _nt_dot(qh, kv[:, ks]) * ATTN_SCALE, -jnp.inf)
            heads.append(_sink_softmax_pv([(sp, prev[:, vs]), (sc, kv[:, vs])], sink_ref[hq]))
    o = jnp.concatenate(heads, axis=1)
    o_ref[...] = h + jnp.dot(o.astype(BF16), wo_ref[...], preferred_element_type=jnp.float32)
    prev_ref[...] = kv


def _attn_prompt(h, gkv, gq, wkv, wq, wo, sinks, *, n_seq, seq_len):
    nb = seq_len // WINDOW
    row_spec = pl.BlockSpec((WINDOW, D_MODEL), lambda n, b: (n * nb + b, 0))
    full = lambda shape: pl.BlockSpec(shape, lambda n, b: (0,) * len(shape))
    return pl.pallas_call(
        _attn_prompt_kernel,
        grid=(n_seq, nb),
        in_specs=[
            row_spec, full((1, D_MODEL)), full((1, D_MODEL)),
            full((D_MODEL, 2 * KV_WIDTH)), full((D_MODEL, D_MODEL)), full((D_MODEL, D_MODEL)),
            pl.BlockSpec(memory_space=pltpu.SMEM),
        ],
        out_specs=[row_spec, pl.BlockSpec((1, WINDOW, 2 * KV_WIDTH), lambda n, b: (n, 0, 0))],
        out_shape=[
            jax.ShapeDtypeStruct(h.shape, jnp.float32),
            jax.ShapeDtypeStruct((n_seq, WINDOW, 2 * KV_WIDTH), jnp.float32),
        ],
        scratch_shapes=[pltpu.VMEM((WINDOW, 2 * KV_WIDTH), jnp.float32)],
        compiler_params=pltpu.CompilerParams(
            dimension_semantics=("arbitrary", "arbitrary"), vmem_limit_bytes=VMEM_LIMIT_BYTES),
        name="attn_prompt",
    )(h, gkv.reshape(1, D_MODEL), gq.reshape(1, D_MODEL), wkv, wq, wo, sinks)


ATTN_SEQS = 16


def _attn_sample_kernel(h_ref, ck_ref, cv_ref, gkv_ref, gq_ref, wkv_ref, wq_ref, wo_ref, sink_ref,
                        o_ref, kw_ref, vw_ref, kv_ref, q_ref, att_ref, *, n_new):
    h = h_ref[...]
    kv, q = _qkv(h, gkv_ref[...], gq_ref[...], wkv_ref, wq_ref)
    kv_ref[...] = kv
    q_ref[...] = q
    n_seq = h.shape[0] // n_new
    rows = Q_PER_KV * n_new
    qpos_c = lax.broadcasted_iota(jnp.int32, (rows, WINDOW), 0) % n_new
    cache_ok = lax.broadcasted_iota(jnp.int32, (rows, WINDOW), 1) > qpos_c
    qpos_n = lax.broadcasted_iota(jnp.int32, (rows, n_new), 0) % n_new
    new_ok = lax.broadcasted_iota(jnp.int32, (rows, n_new), 1) <= qpos_n

    def per_seq(n, carry):
        r0 = pl.multiple_of(n * n_new, n_new)
        kvn = kv_ref[pl.ds(r0, n_new), :]
        qn = q_ref[pl.ds(r0, n_new), :]
        ck = ck_ref[n]
        cv = cv_ref[n]
        kw_ref[n, 0:WINDOW - n_new, :] = ck[n_new:, :]
        kw_ref[n, WINDOW - n_new:WINDOW, :] = kvn[:, 0:KV_WIDTH]
        vw_ref[n, 0:WINDOW - n_new, :] = cv[n_new:, :]
        vw_ref[n, WINDOW - n_new:WINDOW, :] = kvn[:, KV_WIDTH:2 * KV_WIDTH]
        outs = []
        for kvh in range(N_KV_HEADS):
            ks = slice(kvh * HEAD_DIM, (kvh + 1) * HEAD_DIM)
            vs = slice(KV_WIDTH + kvh * HEAD_DIM, KV_WIDTH + (kvh + 1) * HEAD_DIM)
            qs = jnp.concatenate(
                [qn[:, (kvh * Q_PER_KV + g) * HEAD_DIM:(kvh * Q_PER_KV + g + 1) * HEAD_DIM] for g in range(Q_PER_KV)],
                axis=0)
            s_c = jnp.where(cache_ok, _nt_dot(qs, ck[:, ks]) * ATTN_SCALE, -jnp.inf)
            s_n = jnp.where(new_ok, _nt_dot(qs, kvn[:, ks]) * ATTN_SCALE, -jnp.inf)
            sink = sink_ref[kvh][:, 0:1]
            o = _sink_softmax_pv([(s_c, cv[:, ks]), (s_n, kvn[:, vs])], sink)
            outs += [o[g * n_new:(g + 1) * n_new, :] for g in range(Q_PER_KV)]
        att_ref[pl.ds(r0, n_new), :] = jnp.concatenate(outs, axis=1)
        return carry

    lax.fori_loop(0, n_seq, per_seq, 0)
    o_ref[...] = h + jnp.dot(att_ref[...].astype(BF16), wo_ref[...], preferred_element_type=jnp.float32)


def _attn_sample(h, cache_k, cache_v, gkv, gq, wkv, wq, wo, sinks, *, n_new):
    n_seq = cache_k.shape[0]
    sb = ATTN_SEQS
    rows = sb * n_new
    row_spec = pl.BlockSpec((rows, D_MODEL), lambda i: (i, 0))
    win_spec = pl.BlockSpec((sb, WINDOW, KV_WIDTH), lambda i: (i, 0, 0))
    full = lambda shape: pl.BlockSpec(shape, lambda i: (0,) * len(shape))
    sink_rows = jnp.repeat(sinks.reshape(N_KV_HEADS, Q_PER_KV), n_new, axis=1)[:, :, None]
    sink_rows = jnp.broadcast_to(sink_rows, (N_KV_HEADS, Q_PER_KV * n_new, LANES))
    return pl.pallas_call(
        functools.partial(_attn_sample_kernel, n_new=n_new),
        grid=(n_seq // sb,),
        in_specs=[
            row_spec, win_spec, win_spec, full((1, D_MODEL)), full((1, D_MODEL)),
            full((D_MODEL, 2 * KV_WIDTH)), full((D_MODEL, D_MODEL)), full((D_MODEL, D_MODEL)),
            full((N_KV_HEADS, Q_PER_KV * n_new, LANES)),
        ],
        out_specs=[row_spec, win_spec, win_spec],
        out_shape=[
            jax.ShapeDtypeStruct(h.shape, jnp.float32),
            jax.ShapeDtypeStruct((n_seq, WINDOW, KV_WIDTH), jnp.float32),
            jax.ShapeDtypeStruct((n_seq, WINDOW, KV_WIDTH), jnp.float32),
        ],
        scratch_shapes=[
            pltpu.VMEM((rows, 2 * KV_WIDTH), jnp.float32),
            pltpu.VMEM((rows, D_MODEL), jnp.float32),
            pltpu.VMEM((rows, D_MODEL), jnp.float32),
        ],
        compiler_params=pltpu.CompilerParams(
            dimension_semantics=("arbitrary",), vmem_limit_bytes=VMEM_LIMIT_BYTES),
        name="attn_sample",
    )(h, cache_k, cache_v, gkv.reshape(1, D_MODEL), gq.reshape(1, D_MODEL), wkv, wq, wo, sink_rows)


def kernel(x_prompt, x_sample, state_ssm_re, state_ssm_im, cache_k_win, cache_v_win, norm_mix, norm_ffn, norm_kv, norm_final, ssm_lam_re, ssm_lam_im, ssm_log_dt, ssm_b_re, ssm_b_im, ssm_c_re, ssm_c_im, ssm_d, ssm_w_glu, w_kv, w_q, attn_sinks, w_o, peer_w_q, peer_sub_keys, peer_u, peer_v):
    bmat, cmat, apr, api = _s5_discretize(ssm_lam_re[0], ssm_lam_im[0], ssm_log_dt[0], ssm_b_re[0], ssm_b_im[0], ssm_c_re[0], ssm_c_im[0])
    cmat = cmat.astype(jnp.bfloat16)
    wglu = ssm_w_glu[0].astype(jnp.bfloat16)
    wkv = w_kv.astype(BF16)
    wq = w_q[0].astype(BF16)
    wo = w_o[0].astype(BF16)
    peer_wq = [peer_w_q[layer].astype(BF16) for layer in range(2)]
    peer_sk = [peer_sub_keys[layer].astype(BF16) for layer in range(2)]

    def peer(h, layer, final_norm):
        xn, eid_t, gate_t = _peer_route(h, norm_ffn[layer], peer_wq[layer], peer_sk[layer])
        return _peer_experts(eid_t.T, xn, gate_t.T, h, norm_final, peer_u[layer], peer_v[layer],
                             final_norm=final_norm)

    def trunk(x_p, x_s, s_re, s_im, c_k, c_v):
        n_p, n_s = x_p.shape[0], x_s.shape[0]
        xp = x_p.reshape(-1, D_MODEL)
        xs = x_s.reshape(-1, D_MODEL)
        n_prompt_rows = xp.shape[0]
        z0 = jnp.zeros((n_p, N_CHUNKS, 1, CHUNK_STATE), jnp.float32)
        zp, srp, sip = _s5_mixer(xp, z0, z0, norm_mix[0], ssm_d[0], bmat, cmat, apr, api,
                                 chain=True, seq_len=x_p.shape[1])
        zs, srs, sis = _s5_mixer(xs, _state_to_chunks(s_re), _state_to_chunks(s_im), norm_mix[0], ssm_d[0],
                                 bmat, cmat, apr, api, chain=False, seq_len=x_s.shape[1])
        h1 = jnp.concatenate([_glu_residual(zp, xp, wglu), _glu_residual(zs, xs, wglu)], axis=0)
        h2 = peer(h1, 0, False)
        h3p, kvw_p = _attn_prompt(h2[:n_prompt_rows], norm_kv, norm_mix[1], wkv, wq, wo, attn_sinks[0],
                                  n_seq=n_p, seq_len=x_p.shape[1])
        h3s, kw_s, vw_s = _attn_sample(h2[n_prompt_rows:], c_k.reshape(n_s, WINDOW, KV_WIDTH),
                                       c_v.reshape(n_s, WINDOW, KV_WIDTH), norm_kv, norm_mix[1],
                                       wkv, wq, wo, attn_sinks[0], n_new=x_s.shape[1])
        y = peer(jnp.concatenate([h3p, h3s], axis=0), 1, True)
        win = lambda a: a.reshape(a.shape[0], WINDOW, N_KV_HEADS, HEAD_DIM)
        return (y[:n_prompt_rows].reshape(x_p.shape), y[n_prompt_rows:].reshape(x_s.shape),
                _chunks_to_state(srp), _chunks_to_state(sip),
                win(kvw_p[:, :, :KV_WIDTH]), win(kvw_p[:, :, KV_WIDTH:]),
                _chunks_to_state(srs), _chunks_to_state(sis), win(kw_s), win(vw_s))

    n_groups = 2
    gp = x_prompt.shape[0] // n_groups
    gs = x_sample.shape[0] // n_groups
    outs = [trunk(x_prompt[i * gp:(i + 1) * gp], x_sample[i * gs:(i + 1) * gs],
                  state_ssm_re[0, i * gs:(i + 1) * gs], state_ssm_im[0, i * gs:(i + 1) * gs],
                  cache_k_win[i * gs:(i + 1) * gs], cache_v_win[i * gs:(i + 1) * gs])
            for i in range(n_groups)]
    cat = [jnp.concatenate(parts, axis=0) for parts in zip(*outs)]
    return (cat[0], cat[1], cat[2][None], cat[3][None], cat[4], cat[5], cat[6][None], cat[7][None],
            cat[8], cat[9])
```

```python
import functools
import math

import jax
import jax.numpy as jnp
from jax import lax
from jax.experimental import pallas as pl
from jax.experimental.pallas import tpu as pltpu
from jax.experimental.pallas import tpu_sc as plsc

D_MODEL = 1024
GROUP_SIZE = 16
N_GROUPS = D_MODEL // GROUP_SIZE
STATE_DIM = 64
HEAD_DIM = 64
N_Q_HEADS = D_MODEL // HEAD_DIM
N_KV_HEADS = N_Q_HEADS // 8
Q_PER_KV = N_Q_HEADS // N_KV_HEADS
WINDOW = 128
PAST_LEN = 16384
ATTN_SCALE = 1.0 / math.sqrt(HEAD_DIM)
PEER_HEADS = 8
N_KEYS = 128
PEER_TOPK = 16
PEER_DHALF = 128
EPS = 1e-5

LANES = 128
SUBLANES = 8
VMEM_LIMIT_BYTES = 56 * 1024 * 1024

GROUPS_PER_CHUNK = LANES // GROUP_SIZE
N_CHUNKS = N_GROUPS // GROUPS_PER_CHUNK
CHUNK_STATE = GROUPS_PER_CHUNK * STATE_DIM
S5_ROWS = 256


def _rmsnorm_rows(x, g):
    r = lax.rsqrt(jnp.mean(x * x, axis=-1, keepdims=True) + EPS)
    return x * r * g


def _gelu(x):
    return 0.5 * x * (1.0 + lax.erf(x * (1.0 / math.sqrt(2.0))))


def _s5_discretize(lam_re, lam_im, log_dt, b_re, b_im, c_re, c_im):
    f32 = jnp.float32
    lr = lam_re.astype(f32)
    li = lam_im.astype(f32)
    dt = jnp.exp(log_dt.astype(f32))[:, None]
    mag = jnp.exp(lr * dt)
    ab_re = mag * jnp.cos(li * dt)
    ab_im = mag * jnp.sin(li * dt)
    den = lr * lr + li * li
    f_re = ((ab_re - 1.0) * lr + ab_im * li) / den
    f_im = (ab_im * lr - (ab_re - 1.0) * li) / den
    br = b_re.astype(f32)
    bi = b_im.astype(f32)
    bb_re = f_re[..., None] * br - f_im[..., None] * bi
    bb_im = f_re[..., None] * bi + f_im[..., None] * br
    eye = jnp.eye(GROUPS_PER_CHUNK, dtype=f32)

    def chunk_rows(v):
        return v.reshape(N_CHUNKS, 1, CHUNK_STATE)

    def in_blocks(bb):
        t = bb.reshape(N_CHUNKS, GROUPS_PER_CHUNK, STATE_DIM, GROUP_SIZE).transpose(0, 1, 3, 2)
        return jnp.einsum('mgjp,gh->mgjhp', t, eye).reshape(N_CHUNKS, LANES, CHUNK_STATE)

    def out_blocks(c):
        t = c.astype(f32).reshape(N_CHUNKS, GROUPS_PER_CHUNK, GROUP_SIZE, STATE_DIM).transpose(0, 1, 3, 2)
        return jnp.einsum('mgpj,gh->mgphj', t, eye).reshape(N_CHUNKS, CHUNK_STATE, LANES)

    bmat = jnp.concatenate([in_blocks(bb_re), in_blocks(bb_im)], axis=2)
    cmat = jnp.concatenate([out_blocks(c_re), -out_blocks(c_im)], axis=1)
    pr, pi = [ab_re], [ab_im]
    for _ in range(SUBLANES - 1):
        pr, pi = pr + [pr[-1] * ab_re - pi[-1] * ab_im], pi + [pr[-1] * ab_im + pi[-1] * ab_re]
    apr = jnp.concatenate([chunk_rows(v) for v in pr], axis=1)
    api = jnp.concatenate([chunk_rows(v) for v in pi], axis=1)
    return bmat, cmat, apr, api


def _s5_kernel(x_ref, g_ref, d_ref, b_ref, c_ref, apr_ref, api_ref, h0r_ref, h0i_ref,
               z_ref, sr_ref, si_ref, u_ref, st_ref, cr_ref, ci_ref, *, chain, blocks_per_seq):
    rb = pl.program_id(0)
    m = pl.program_id(1)
    rows = x_ref.shape[0]
    n_tiles = rows // SUBLANES

    @pl.when(m == 0)
    def _():
        u = _rmsnorm_rows(x_ref[...], g_ref[...])
        for mm in range(N_CHUNKS):
            u_ref[mm] = u[:, mm * LANES:(mm + 1) * LANES]

    u = u_ref[m]
    st_ref[...] = jnp.dot(u, b_ref[0], preferred_element_type=jnp.float32, precision=lax.Precision.HIGHEST)

    apr = apr_ref[0]
    api = api_ref[0]
    row = lax.broadcasted_iota(jnp.int32, (SUBLANES, CHUNK_STATE), 0)

    if chain:
        @pl.when(rb % blocks_per_seq == 0)
        def _():
            cr_ref[m] = h0r_ref[0, 0]
            ci_ref[m] = h0i_ref[0, 0]

    def tile_step(k, carry):
        r0 = pl.multiple_of(k * SUBLANES, SUBLANES)
        xr = st_ref[pl.ds(r0, SUBLANES), 0:CHUNK_STATE]
        xi = st_ref[pl.ds(r0, SUBLANES), CHUNK_STATE:2 * CHUNK_STATE]
        for d in (1, 2, 4):
            ar = apr[d - 1:d, :]
            ai = api[d - 1:d, :]
            sr = jnp.where(row >= d, pltpu.roll(xr, d, axis=0), 0.0)
            si = jnp.where(row >= d, pltpu.roll(xi, d, axis=0), 0.0)
            xr, xi = xr + ar * sr - ai * si, xi + ar * si + ai * sr
        if chain:
            cr, ci = carry
        else:
            cr = h0r_ref[k, 0]
            ci = h0i_ref[k, 0]
        hr = xr + apr * cr - api * ci
        hi = xi + apr * ci + api * cr
        st_ref[pl.ds(r0, SUBLANES), 0:CHUNK_STATE] = hr
        st_ref[pl.ds(r0, SUBLANES), CHUNK_STATE:2 * CHUNK_STATE] = hi
        lr_ = hr[SUBLANES - 1:SUBLANES, :]
        li_ = hi[SUBLANES - 1:SUBLANES, :]
        if chain:
            return lr_, li_
        sr_ref[k, m] = lr_
        si_ref[k, m] = li_
        return carry

    if chain:
        cr, ci = lax.fori_loop(0, n_tiles, tile_step, (cr_ref[m], ci_ref[m]))
        cr_ref[m] = cr
        ci_ref[m] = ci
        sr_ref[0, m] = cr
        si_ref[0, m] = ci
    else:
        lax.fori_loop(0, n_tiles, tile_step, 0)

    y = jnp.dot(st_ref[...].astype(jnp.bfloat16), c_ref[0], preferred_element_type=jnp.float32)
    y = y + d_ref[0] * u
    z_ref[...] = _gelu(y)


def _s5_mixer(x, h0r, h0i, g, d_skip, bmat, cmat_bf16, apr, api, *, chain, seq_len):
    T = x.shape[0]
    nseq = h0r.shape[0]
    rows = S5_ROWS
    if chain:
        blocks_per_seq = seq_len // rows
        seq_blk = 1
        seq_map = lambda rb, m: (rb // blocks_per_seq, m, 0, 0)
        out_map = lambda rb, m: (rb // blocks_per_seq, 0, 0, 0)
    else:
        assert seq_len == SUBLANES
        blocks_per_seq = 1
        seq_blk = rows // SUBLANES
        seq_map = lambda rb, m: (rb, m, 0, 0)
        out_map = lambda rb, m: (rb, 0, 0, 0)
    grid = (T // rows, N_CHUNKS)
    kern = functools.partial(_s5_kernel, chain=chain, blocks_per_seq=blocks_per_seq)
    st_spec = pl.BlockSpec((seq_blk, 1, 1, CHUNK_STATE), seq_map)
    out_st_spec = pl.BlockSpec((seq_blk, N_CHUNKS, 1, CHUNK_STATE), out_map)
    z, sr, si = pl.pallas_call(
        kern,
        grid=grid,
        in_specs=[
            pl.BlockSpec((rows, D_MODEL), lambda rb, m: (rb, 0)),
            pl.BlockSpec((1, D_MODEL), lambda rb, m: (0, 0)),
            pl.BlockSpec((1, 1, LANES), lambda rb, m: (m, 0, 0)),
            pl.BlockSpec((1, LANES, 2 * CHUNK_STATE), lambda rb, m: (m, 0, 0)),
            pl.BlockSpec((1, 2 * CHUNK_STATE, LANES), lambda rb, m: (m, 0, 0)),
            pl.BlockSpec((1, SUBLANES, CHUNK_STATE), lambda rb, m: (m, 0, 0)),
            pl.BlockSpec((1, SUBLANES, CHUNK_STATE), lambda rb, m: (m, 0, 0)),
            st_spec, st_spec,
        ],
        out_specs=[
            pl.BlockSpec((rows, LANES), lambda rb, m: (rb, m)),
            out_st_spec, out_st_spec,
        ],
        out_shape=[
            jax.ShapeDtypeStruct((T, D_MODEL), jnp.float32),
            jax.ShapeDtypeStruct((nseq, N_CHUNKS, 1, CHUNK_STATE), jnp.float32),
            jax.ShapeDtypeStruct((nseq, N_CHUNKS, 1, CHUNK_STATE), jnp.float32),
        ],
        scratch_shapes=[
            pltpu.VMEM((N_CHUNKS, rows, LANES), jnp.float32),
            pltpu.VMEM((rows, 2 * CHUNK_STATE), jnp.float32),
            pltpu.VMEM((N_CHUNKS, 1, CHUNK_STATE), jnp.float32),
            pltpu.VMEM((N_CHUNKS, 1, CHUNK_STATE), jnp.float32),
        ],
        compiler_params=pltpu.CompilerParams(
            dimension_semantics=("arbitrary", "arbitrary"), vmem_limit_bytes=VMEM_LIMIT_BYTES),
        name="s5_mixer",
    )(x, g.reshape(1, D_MODEL), d_skip.reshape(N_CHUNKS, 1, LANES), bmat, cmat_bf16, apr, api, h0r, h0i)
    return z, sr, si


def _state_to_chunks(h):
    return h.reshape(h.shape[0], N_CHUNKS, 1, CHUNK_STATE)


def _chunks_to_state(s):
    return s.reshape(s.shape[0], N_GROUPS, STATE_DIM)


def _glu_kernel(z_ref, res_ref, w_ref, o_ref):
    zz = jnp.dot(z_ref[...].astype(jnp.bfloat16), w_ref[...], preferred_element_type=jnp.float32)
    a = zz[:, :D_MODEL]
    b = zz[:, D_MODEL:]
    o_ref[...] = res_ref[...] + a * (1.0 / (1.0 + jnp.exp(-b)))


def _glu_residual(z, res, w_bf16):
    T = z.shape[0]
    rows = math.gcd(T, 512)
    return pl.pallas_call(
        _glu_kernel,
        grid=(T // rows,),
        in_specs=[
            pl.BlockSpec((rows, D_MODEL), lambda i: (i, 0)),
            pl.BlockSpec((rows, D_MODEL), lambda i: (i, 0)),
            pl.BlockSpec((D_MODEL, 2 * D_MODEL), lambda i: (0, 0)),
        ],
        out_specs=pl.BlockSpec((rows, D_MODEL), lambda i: (i, 0)),
        out_shape=jax.ShapeDtypeStruct((T, D_MODEL), jnp.float32),
        compiler_params=pltpu.CompilerParams(
            dimension_semantics=("arbitrary",), vmem_limit_bytes=VMEM_LIMIT_BYTES),
        name="glu_residual",
    )(z, res, w_bf16)


ROUTE_ROWS = 256
CAND = PEER_TOPK * PEER_TOPK


def _topk_rows(s, payload):
    n_rows = s.shape[0]
    row = lax.broadcasted_iota(jnp.int32, s.shape, 0)
    vals, picks = [], []
    for _ in range(PEER_TOPK):
        m = jnp.max(s, axis=0, keepdims=True)
        pos = jnp.min(jnp.where(s == m, row, n_rows), axis=0, keepdims=True)
        sel = row == pos
        vals.append(m)
        if payload is None:
            picks.append(pos)
        else:
            picks.append(jnp.max(jnp.where(sel, payload, -1), axis=0, keepdims=True))
        s = jnp.where(sel, -jnp.inf, s)
    return jnp.concatenate(vals, axis=0), jnp.concatenate(picks, axis=0)


def _peer_route_kernel(h_ref, g_ref, wq_ref, sk_ref, xn_ref, eid_ref, gate_ref, xb_ref):
    hd = pl.program_id(1)

    @pl.when(hd == 0)
    def _():
        xn = _rmsnorm_rows(h_ref[...], g_ref[...])
        xn_ref[...] = xn
        xb_ref[...] = xn.astype(jnp.bfloat16)

    q = jnp.dot(xb_ref[...], wq_ref[...], preferred_element_type=jnp.float32)
    sv, si = [], []
    for c in range(2):
        qc = q[:, c * PEER_DHALF:(c + 1) * PEER_DHALF].astype(jnp.bfloat16)
        st = lax.dot_general(sk_ref[0, c], qc, (((1,), (1,)), ((), ())), preferred_element_type=jnp.float32)
        v, i = _topk_rows(st, None)
        sv.append(v)
        si.append(i)
    cand = jnp.concatenate([sv[0][i:i + 1, :] + sv[1] for i in range(PEER_TOPK)], axis=0)
    cid = jnp.concatenate([si[0][i:i + 1, :] * N_KEYS + si[1] for i in range(PEER_TOPK)], axis=0)
    fv, eid = _topk_rows(cand, cid)
    e = jnp.exp(fv - fv[0:1, :])
    gate_ref[...] = e / jnp.sum(e, axis=0, keepdims=True)
    eid_ref[...] = eid


def _peer_route(h, g, wq_bf16, sk_bf16):
    T = h.shape[0]
    rows = ROUTE_ROWS
    n_sel = PEER_HEADS * PEER_TOPK
    return pl.pallas_call(
        _peer_route_kernel,
        grid=(T // rows, PEER_HEADS),
        in_specs=[
            pl.BlockSpec((rows, D_MODEL), lambda tb, hd: (tb, 0)),
            pl.BlockSpec((1, D_MODEL), lambda tb, hd: (0, 0)),
            pl.BlockSpec((D_MODEL, 2 * PEER_DHALF), lambda tb, hd: (0, hd)),
            pl.BlockSpec((1, 2, N_KEYS, PEER_DHALF), lambda tb, hd: (hd, 0, 0, 0)),
        ],
        out_specs=[
            pl.BlockSpec((rows, D_MODEL), lambda tb, hd: (tb, 0)),
            pl.BlockSpec((PEER_TOPK, rows), lambda tb, hd: (hd, tb)),
            pl.BlockSpec((PEER_TOPK, rows), lambda tb, hd: (hd, tb)),
        ],
        out_shape=[
            jax.ShapeDtypeStruct((T, D_MODEL), jnp.float32),
            jax.ShapeDtypeStruct((n_sel, T), jnp.int32),
            jax.ShapeDtypeStruct((n_sel, T), jnp.float32),
        ],
        scratch_shapes=[pltpu.VMEM((rows, D_MODEL), jnp.bfloat16)],
        compiler_params=pltpu.CompilerParams(
            dimension_semantics=("arbitrary", "arbitrary"), vmem_limit_bytes=VMEM_LIMIT_BYTES),
        name="peer_route",
    )(h, g.reshape(1, D_MODEL), wq_bf16, sk_bf16)


N_SEL = PEER_HEADS * PEER_TOPK
SC_LANES = 16
GATHER_ROWS = PEER_TOPK
GATHERS_PER_TOKEN = N_SEL // GATHER_ROWS
GATHER_BUFS = 4
SC_TOKENS = 8


def _sc_gather_loop(wid, n_batches, tab_hbm, idx_v, bufs, sems, load_batch, compute, store_batch):
    def start(t, kk, b):
        idx = idx_v[t, pl.ds(kk * GATHER_ROWS, GATHER_ROWS)]
        pltpu.async_copy(tab_hbm.at[idx], bufs.at[b], sems.at[b])

    def wait(b):
        pltpu.make_async_copy(tab_hbm.at[pl.ds(0, GATHER_ROWS)], bufs.at[b], sems.at[b]).wait()

    def batch(bi, carry):
        base = (wid * n_batches + bi) * SC_TOKENS
        load_batch(base)
        for q in range(GATHER_BUFS - 1):
            start(q // GATHERS_PER_TOKEN, q % GATHERS_PER_TOKEN, q % GATHER_BUFS)

        def tok(t, carry):
            for kk in range(GATHERS_PER_TOKEN):
                nq = kk + GATHER_BUFS - 1
                nt = t + nq // GATHERS_PER_TOKEN

                @pl.when(nt < SC_TOKENS)
                def _():
                    start(nt, nq % GATHERS_PER_TOKEN, nq % GATHER_BUFS)

                wait(kk % GATHER_BUFS)
                compute(t, kk, kk % GATHER_BUFS)
            return carry

        lax.fori_loop(0, SC_TOKENS, tok, 0)
        store_batch(base)
        return carry

    lax.fori_loop(0, n_batches, batch, 0)


def _sc_mesh_and_batches(n_tokens):
    info = plsc.get_sparse_core_info()
    assert info.num_lanes == SC_LANES
    n_workers = info.num_cores * info.num_subcores
    assert n_tokens % (n_workers * SC_TOKENS) == 0
    mesh = plsc.VectorSubcoreMesh(core_axis_name="c", subcore_axis_name="s")
    return info, mesh, n_tokens // (n_workers * SC_TOKENS)


def _peer_hidden_sc(eid, xn, u_tab):
    T = eid.shape[0]
    info, mesh, n_batches = _sc_mesh_and_batches(T)

    @functools.partial(
        pl.kernel, mesh=mesh,
        out_type=jax.ShapeDtypeStruct((T, N_SEL), jnp.float32),
        scratch_types=[
            pltpu.VMEM((SC_TOKENS, N_SEL), jnp.int32),
            pltpu.VMEM((SC_TOKENS, D_MODEL), jnp.float32),
            pltpu.VMEM((SC_TOKENS, N_SEL), jnp.float32),
            pltpu.VMEM((GATHER_BUFS, GATHER_ROWS, D_MODEL), jnp.float32),
            pltpu.VMEM((GATHER_ROWS, SC_LANES), jnp.float32),
            pltpu.SemaphoreType.DMA((GATHER_BUFS,)),
        ],
        compiler_params=pltpu.CompilerParams(needs_layout_passes=False),
        name="peer_hidden_sc",
    )
    def k(eid_hbm, xn_hbm, u_hbm, out_hbm, idx_v, x_v, o_v, bufs, acc_v, sems):
        wid = lax.axis_index("s") * info.num_cores + lax.axis_index("c")
        lane = lax.iota(jnp.int32, SC_LANES)

        def load_batch(base):
            pltpu.sync_copy(eid_hbm.at[pl.ds(base, SC_TOKENS)], idx_v)
            pltpu.sync_copy(xn_hbm.at[pl.ds(base, SC_TOKENS)], x_v)

        def store_batch(base):
            pltpu.sync_copy(o_v, out_hbm.at[pl.ds(base, SC_TOKENS)])

        def compute(t, kk, b):
            zero = jnp.zeros((SC_LANES,), jnp.float32)

            @plsc.parallel_loop(0, D_MODEL // SC_LANES, carry=(zero,) * GATHER_ROWS)
            def accs(c, accs):
                xc = x_v[t, pl.ds(c * SC_LANES, SC_LANES)]
                return tuple(accs[r] + bufs[b, r, pl.ds(c * SC_LANES, SC_LANES)] * xc for r in range(GATHER_ROWS))

            for r in range(GATHER_ROWS):
                acc_v[r, :] = accs[r]
            tot = zero
            for c in range(SC_LANES):
                tot = tot + plsc.load_gather(acc_v, [lane, jnp.full((SC_LANES,), c, jnp.int32)])
            o_v[t, pl.ds(kk * GATHER_ROWS, GATHER_ROWS)] = tot

        _sc_gather_loop(wid, n_batches, u_hbm, idx_v, bufs, sems, load_batch, compute, store_batch)

    return k(eid, xn, u_tab)


def _peer_combine_sc(eid, a, v_tab):
    T = eid.shape[0]
    info, mesh, n_batches = _sc_mesh_and_batches(T)

    @functools.partial(
        pl.kernel, mesh=mesh,
        out_type=jax.ShapeDtypeStruct((T, D_MODEL), jnp.float32),
        scratch_types=[
            pltpu.VMEM((SC_TOKENS, N_SEL), jnp.int32),
            pltpu.VMEM((SC_TOKENS, N_SEL), jnp.float32),
            pltpu.VMEM((SC_TOKENS, D_MODEL), jnp.float32),
            pltpu.VMEM((GATHER_BUFS, GATHER_ROWS, D_MODEL), jnp.float32),
            pltpu.SemaphoreType.DMA((GATHER_BUFS,)),
        ],
        compiler_params=pltpu.CompilerParams(needs_layout_passes=False),
        name="peer_combine_sc",
    )
    def k(eid_hbm, a_hbm, v_hbm, out_hbm, idx_v, a_v, o_v, bufs, sems):
        wid = lax.axis_index("s") * info.num_cores + lax.axis_index("c")

        def load_batch(base):
            pltpu.sync_copy(eid_hbm.at[pl.ds(base, SC_TOKENS)], idx_v)
            pltpu.sync_copy(a_hbm.at[pl.ds(base, SC_TOKENS)], a_v)

        def store_batch(base):
            pltpu.sync_copy(o_v, out_hbm.at[pl.ds(base, SC_TOKENS)])

        def compute(t, kk, b):
            tvec = jnp.full((SC_LANES,), t, jnp.int32)
            ws = [plsc.load_gather(a_v, [tvec, jnp.full((SC_LANES,), kk * GATHER_ROWS + r, jnp.int32)])
                  for r in range(GATHER_ROWS)]

            @plsc.parallel_loop(0, D_MODEL // SC_LANES, unroll=2)
            def _(c):
                sl = pl.ds(c * SC_LANES, SC_LANES)
                terms = [ws[r] * bufs[b, r, sl] for r in range(GATHER_ROWS)]
                if kk != 0:
                    terms.append(o_v[t, sl])
                while len(terms) > 1:
                    pairs = [terms[i] + terms[i + 1] for i in range(0, len(terms) - 1, 2)]
                    terms = pairs + ([terms[-1]] if len(terms) % 2 else [])
                o_v[t, sl] = terms[0]

        _sc_gather_loop(wid, n_batches, v_hbm, idx_v, bufs, sems, load_batch, compute, store_batch)

    return k(eid, a, v_tab)


def _peer_act_kernel(hp_ref, gate_ref, a_ref):
    a_ref[...] = _gelu(hp_ref[...]) * gate_ref[...]


def _peer_act(hpre, gate, rows=1088):
    T = hpre.shape[0]
    spec = pl.BlockSpec((rows, N_SEL), lambda i: (i, 0))
    return pl.pallas_call(
        _peer_act_kernel, grid=(T // rows,), in_specs=[spec, spec], out_specs=spec,
        out_shape=jax.ShapeDtypeStruct((T, N_SEL), jnp.float32),
        compiler_params=pltpu.CompilerParams(dimension_semantics=("arbitrary",)),
        name="peer_act",
    )(hpre, gate)


def _residual_kernel(h_ref, c_ref, g_ref, o_ref, *, final_norm):
    y = h_ref[...] + c_ref[...]
    o_ref[...] = _rmsnorm_rows(y, g_ref[...]) if final_norm else y


def _residual(h, c, gfin, *, final_norm, rows=1088):
    T = h.shape[0]
    spec = pl.BlockSpec((rows, D_MODEL), lambda i: (i, 0))
    return pl.pallas_call(
        functools.partial(_residual_kernel, final_norm=final_norm), grid=(T // rows,),
        in_specs=[spec, spec, pl.BlockSpec((1, D_MODEL), lambda i: (0, 0))], out_specs=spec,
        out_shape=jax.ShapeDtypeStruct((T, D_MODEL), jnp.float32),
        compiler_params=pltpu.CompilerParams(dimension_semantics=("arbitrary",)),
        name="peer_residual",
    )(h, c, gfin.reshape(1, D_MODEL))


def _peer_experts(eid, xn, gate, h, gfin, u_tab, v_tab, *, final_norm):
    hpre = _peer_hidden_sc(eid, xn, u_tab)
    a = _peer_act(hpre, gate)
    c = _peer_combine_sc(eid, a, v_tab)
    return _residual(h, c, gfin, final_norm=final_norm)


KV_WIDTH = N_KV_HEADS * HEAD_DIM
BF16 = jnp.bfloat16


def _qkv(h, gkv, gq, wkv_ref, wq_ref):
    kv = jnp.dot(_rmsnorm_rows(h, gkv).astype(BF16), wkv_ref[...], preferred_element_type=jnp.float32)
    q = jnp.dot(_rmsnorm_rows(h, gq).astype(BF16), wq_ref[...], preferred_element_type=jnp.float32)
    return kv, q


def _sink_softmax_pv(parts, sink):
    m = sink
    for s, _ in parts:
        m = jnp.maximum(m, jnp.max(s, axis=-1, keepdims=True))
    den = jnp.exp(sink - m)
    acc = None
    for s, v in parts:
        e = jnp.exp(s - m)
        den = den + jnp.sum(e, axis=-1, keepdims=True)
        pv = jnp.dot(e.astype(BF16), v.astype(BF16), preferred_element_type=jnp.float32)
        acc = pv if acc is None else acc + pv
    return acc / den


def _nt_dot(a, b):
    return lax.dot_general(a.astype(BF16), b.astype(BF16), (((1,), (1,)), ((), ())),
                           preferred_element_type=jnp.float32)


def _attn_prompt_kernel(h_ref, gkv_ref, gq_ref, wkv_ref, wq_ref, wo_ref, sink_ref,
                        o_ref, kvw_ref, prev_ref):
    blk = pl.program_id(1)
    h = h_ref[...]
    kv, q = _qkv(h, gkv_ref[...], gq_ref[...], wkv_ref, wq_ref)
    kvw_ref[0] = kv

    @pl.when(blk == 0)
    def _():
        prev_ref[...] = jnp.zeros_like(prev_ref)

    prev = prev_ref[...]
    qi = lax.broadcasted_iota(jnp.int32, (WINDOW, WINDOW), 0)
    kj = lax.broadcasted_iota(jnp.int32, (WINDOW, WINDOW), 1)
    prev_ok = jnp.logical_and(kj > qi, blk > 0)
    cur_ok = kj <= qi
    heads = []
    for kvh in range(N_KV_HEADS):
        ks = slice(kvh * HEAD_DIM, (kvh + 1) * HEAD_DIM)
        vs = slice(KV_WIDTH + kvh * HEAD_DIM, KV_WIDTH + (kvh + 1) * HEAD_DIM)
        for g in range(Q_PER_KV):
            hq = kvh * Q_PER_KV + g
            qh = q[:, hq * HEAD_DIM:(hq + 1) * HEAD_DIM]
            sp = jnp.where(prev_ok, _nt_dot(qh, prev[:, ks]) * ATTN_SCALE, -jnp.inf)
            sc = jnp.where(cur_ok, _nt_dot(qh, kv[:, ks]) * ATTN_SCALE, -jnp.inf)
            heads.append(_sink_softmax_pv([(sp, prev[:, vs]), (sc, kv[:, vs])], sink_ref[hq]))
    o = jnp.concatenate(heads, axis=1)
    o_ref[...] = h + jnp.dot(o.astype(BF16), wo_ref[...], preferred_element_type=jnp.float32)
    prev_ref[...] = kv


def _attn_prompt(h, gkv, gq, wkv, wq, wo, sinks, *, n_seq, seq_len):
    nb = seq_len // WINDOW
    row_spec = pl.BlockSpec((WINDOW, D_MODEL), lambda n, b: (n * nb + b, 0))
    full = lambda shape: pl.BlockSpec(shape, lambda n, b: (0,) * len(shape))
    return pl.pallas_call(
        _attn_prompt_kernel,
        grid=(n_seq, nb),
        in_specs=[
            row_spec, full((1, D_MODEL)), full((1, D_MODEL)),
            full((D_MODEL, 2 * KV_WIDTH)), full((D_MODEL, D_MODEL)), full((D_MODEL, D_MODEL)),
            pl.BlockSpec(memory_space=pltpu.SMEM),
        ],
        out_specs=[row_spec, pl.BlockSpec((1, WINDOW, 2 * KV_WIDTH), lambda n, b: (n, 0, 0))],
        out_shape=[
            jax.ShapeDtypeStruct(h.shape, jnp.float32),
            jax.ShapeDtypeStruct((n_seq, WINDOW, 2 * KV_WIDTH), jnp.float32),
        ],
        scratch_shapes=[pltpu.VMEM((WINDOW, 2 * KV_WIDTH), jnp.float32)],
        compiler_params=pltpu.CompilerParams(
            dimension_semantics=("arbitrary", "arbitrary"), vmem_limit_bytes=VMEM_LIMIT_BYTES),
        name="attn_prompt",
    )(h, gkv.reshape(1, D_MODEL), gq.reshape(1, D_MODEL), wkv, wq, wo, sinks)


ATTN_SEQS = 16


def _attn_sample_kernel(h_ref, ck_ref, cv_ref, gkv_ref, gq_ref, wkv_ref, wq_ref, wo_ref, sink_ref,
                        o_ref, kw_ref, vw_ref, kv_ref, q_ref, att_ref, *, n_new):
    h = h_ref[...]
    kv, q = _qkv(h, gkv_ref[...], gq_ref[...], wkv_ref, wq_ref)
    kv_ref[...] = kv
    q_ref[...] = q
    n_seq = h.shape[0] // n_new
    rows = Q_PER_KV * n_new
    qpos_c = lax.broadcasted_iota(jnp.int32, (rows, WINDOW), 0) % n_new
    cache_ok = lax.broadcasted_iota(jnp.int32, (rows, WINDOW), 1) > qpos_c
    qpos_n = lax.broadcasted_iota(jnp.int32, (rows, n_new), 0) % n_new
    new_ok = lax.broadcasted_iota(jnp.int32, (rows, n_new), 1) <= qpos_n

    def per_seq(n, carry):
        r0 = pl.multiple_of(n * n_new, n_new)
        kvn = kv_ref[pl.ds(r0, n_new), :]
        qn = q_ref[pl.ds(r0, n_new), :]
        ck = ck_ref[n]
        cv = cv_ref[n]
        kw_ref[n, 0:WINDOW - n_new, :] = ck[n_new:, :]
        kw_ref[n, WINDOW - n_new:WINDOW, :] = kvn[:, 0:KV_WIDTH]
        vw_ref[n, 0:WINDOW - n_new, :] = cv[n_new:, :]
        vw_ref[n, WINDOW - n_new:WINDOW, :] = kvn[:, KV_WIDTH:2 * KV_WIDTH]
        outs = []
        for kvh in range(N_KV_HEADS):
            ks = slice(kvh * HEAD_DIM, (kvh + 1) * HEAD_DIM)
            vs = slice(KV_WIDTH + kvh * HEAD_DIM, KV_WIDTH + (kvh + 1) * HEAD_DIM)
            qs = jnp.concatenate(
                [qn[:, (kvh * Q_PER_KV + g) * HEAD_DIM:(kvh * Q_PER_KV + g + 1) * HEAD_DIM] for g in range(Q_PER_KV)],
                axis=0)
            s_c = jnp.where(cache_ok, _nt_dot(qs, ck[:, ks]) * ATTN_SCALE, -jnp.inf)
            s_n = jnp.where(new_ok, _nt_dot(qs, kvn[:, ks]) * ATTN_SCALE, -jnp.inf)
            sink = sink_ref[kvh][:, 0:1]
            o = _sink_softmax_pv([(s_c, cv[:, ks]), (s_n, kvn[:, vs])], sink)
            outs += [o[g * n_new:(g + 1) * n_new, :] for g in range(Q_PER_KV)]
        att_ref[pl.ds(r0, n_new), :] = jnp.concatenate(outs, axis=1)
        return carry

    lax.fori_loop(0, n_seq, per_seq, 0)
    o_ref[...] = h + jnp.dot(att_ref[...].astype(BF16), wo_ref[...], preferred_element_type=jnp.float32)


def _attn_sample(h, cache_k, cache_v, gkv, gq, wkv, wq, wo, sinks, *, n_new):
    n_seq = cache_k.shape[0]
    sb = ATTN_SEQS
    rows = sb * n_new
    row_spec = pl.BlockSpec((rows, D_MODEL), lambda i: (i, 0))
    win_spec = pl.BlockSpec((sb, WINDOW, KV_WIDTH), lambda i: (i, 0, 0))
    full = lambda shape: pl.BlockSpec(shape, lambda i: (0,) * len(shape))
    sink_rows = jnp.repeat(sinks.reshape(N_KV_HEADS, Q_PER_KV), n_new, axis=1)[:, :, None]
    sink_rows = jnp.broadcast_to(sink_rows, (N_KV_HEADS, Q_PER_KV * n_new, LANES))
    return pl.pallas_call(
        functools.partial(_attn_sample_kernel, n_new=n_new),
        grid=(n_seq // sb,),
        in_specs=[
            row_spec, win_spec, win_spec, full((1, D_MODEL)), full((1, D_MODEL)),
            full((D_MODEL, 2 * KV_WIDTH)), full((D_MODEL, D_MODEL)), full((D_MODEL, D_MODEL)),
            full((N_KV_HEADS, Q_PER_KV * n_new, LANES)),
        ],
        out_specs=[row_spec, win_spec, win_spec],
        out_shape=[
            jax.ShapeDtypeStruct(h.shape, jnp.float32),
            jax.ShapeDtypeStruct((n_seq, WINDOW, KV_WIDTH), jnp.float32),
            jax.ShapeDtypeStruct((n_seq, WINDOW, KV_WIDTH), jnp.float32),
        ],
        scratch_shapes=[
            pltpu.VMEM((rows, 2 * KV_WIDTH), jnp.float32),
            pltpu.VMEM((rows, D_MODEL), jnp.float32),
            pltpu.VMEM((rows, D_MODEL), jnp.float32),
        ],
        compiler_params=pltpu.CompilerParams(
            dimension_semantics=("arbitrary",), vmem_limit_bytes=VMEM_LIMIT_BYTES),
        name="attn_sample",
    )(h, cache_k, cache_v, gkv.reshape(1, D_MODEL), gq.reshape(1, D_MODEL), wkv, wq, wo, sink_rows)


def kernel(x_prompt, x_sample, state_ssm_re, state_ssm_im, cache_k_win, cache_v_win, norm_mix, norm_ffn, norm_kv, norm_final, ssm_lam_re, ssm_lam_im, ssm_log_dt, ssm_b_re, ssm_b_im, ssm_c_re, ssm_c_im, ssm_d, ssm_w_glu, w_kv, w_q, attn_sinks, w_o, peer_w_q, peer_sub_keys, peer_u, peer_v):
    bmat, cmat, apr, api = _s5_discretize(ssm_lam_re[0], ssm_lam_im[0], ssm_log_dt[0], ssm_b_re[0], ssm_b_im[0], ssm_c_re[0], ssm_c_im[0])
    cmat = cmat.astype(jnp.bfloat16)
    wglu = ssm_w_glu[0].astype(jnp.bfloat16)
    wkv = w_kv.astype(BF16)
    wq = w_q[0].astype(BF16)
    wo = w_o[0].astype(BF16)
    peer_wq = [peer_w_q[layer].astype(BF16) for layer in range(2)]
    peer_sk = [peer_sub_keys[layer].astype(BF16) for layer in range(2)]

    def peer(h, layer, final_norm):
        xn, eid_t, gate_t = _peer_route(h, norm_ffn[layer], peer_wq[layer], peer_sk[layer])
        return _peer_experts(eid_t.T, xn, gate_t.T, h, norm_final, peer_u[layer], peer_v[layer],
                             final_norm=final_norm)

    def trunk(x_p, x_s, s_re, s_im, c_k, c_v):
        n_p, n_s = x_p.shape[0], x_s.shape[0]
        xp = x_p.reshape(-1, D_MODEL)
        xs = x_s.reshape(-1, D_MODEL)
        n_prompt_rows = xp.shape[0]
        z0 = jnp.zeros((n_p, N_CHUNKS, 1, CHUNK_STATE), jnp.float32)
        zp, srp, sip = _s5_mixer(xp, z0, z0, norm_mix[0], ssm_d[0], bmat, cmat, apr, api,
                                 chain=True, seq_len=x_p.shape[1])
        zs, srs, sis = _s5_mixer(xs, _state_to_chunks(s_re), _state_to_chunks(s_im), norm_mix[0], ssm_d[0],
                                 bmat, cmat, apr, api, chain=False, seq_len=x_s.shape[1])
        h1 = jnp.concatenate([_glu_residual(zp, xp, wglu), _glu_residual(zs, xs, wglu)], axis=0)
        h2 = peer(h1, 0, False)
        h3p, kvw_p = _attn_prompt(h2[:n_prompt_rows], norm_kv, norm_mix[1], wkv, wq, wo, attn_sinks[0],
                                  n_seq=n_p, seq_len=x_p.shape[1])
        h3s, kw_s, vw_s = _attn_sample(h2[n_prompt_rows:], c_k.reshape(n_s, WINDOW, KV_WIDTH),
                                       c_v.reshape(n_s, WINDOW, KV_WIDTH), norm_kv, norm_mix[1],
                                       wkv, wq, wo, attn_sinks[0], n_new=x_s.shape[1])
        y = peer(jnp.concatenate([h3p, h3s], axis=0), 1, True)
        win = lambda a: a.reshape(a.shape[0], WINDOW, N_KV_HEADS, HEAD_DIM)
        return (y[:n_prompt_rows].reshape(x_p.shape), y[n_prompt_rows:].reshape(x_s.shape),
                _chunks_to_state(srp), _chunks_to_state(sip),
                win(kvw_p[:, :, :KV_WIDTH]), win(kvw_p[:, :, KV_WIDTH:]),
                _chunks_to_state(srs), _chunks_to_state(sis), win(kw_s), win(vw_s))

    n_groups = 4
    gp = x_prompt.shape[0] // n_groups
    gs = x_sample.shape[0] // n_groups
    outs = [trunk(x_prompt[i * gp:(i + 1) * gp], x_sample[i * gs:(i + 1) * gs],
                  state_ssm_re[0, i * gs:(i + 1) * gs], state_ssm_im[0, i * gs:(i + 1) * gs],
                  cache_k_win[i * gs:(i + 1) * gs], cache_v_win[i * gs:(i + 1) * gs])
            for i in range(n_groups)]
    cat = [jnp.concatenate(parts, axis=0) for parts in zip(*outs)]
    return (cat[0], cat[1], cat[2][None], cat[3][None], cat[4], cat[5], cat[6][None], cat[7][None],
            cat[8], cat[9])
```

```python
import functools
import math

import jax
import jax.numpy as jnp
from jax import lax
from jax.experimental import pallas as pl
from jax.experimental.pallas import tpu as pltpu
from jax.experimental.pallas import tpu_sc as plsc

D_MODEL = 1024
GROUP_SIZE = 16
N_GROUPS = D_MODEL // GROUP_SIZE
STATE_DIM = 64
HEAD_DIM = 64
N_Q_HEADS = D_MODEL // HEAD_DIM
N_KV_HEADS = N_Q_HEADS // 8
Q_PER_KV = N_Q_HEADS // N_KV_HEADS
WINDOW = 128
PAST_LEN = 16384
ATTN_SCALE = 1.0 / math.sqrt(HEAD_DIM)
PEER_HEADS = 8
N_KEYS = 128
PEER_TOPK = 16
PEER_DHALF = 128
EPS = 1e-5

LANES = 128
SUBLANES = 8
VMEM_LIMIT_BYTES = 56 * 1024 * 1024

GROUPS_PER_CHUNK = LANES // GROUP_SIZE
N_CHUNKS = N_GROUPS // GROUPS_PER_CHUNK
CHUNK_STATE = GROUPS_PER_CHUNK * STATE_DIM
S5_ROWS = 256


def _rmsnorm_rows(x, g):
    r = lax.rsqrt(jnp.mean(x * x, axis=-1, keepdims=True) + EPS)
    return x * r * g


def _gelu(x):
    return 0.5 * x * (1.0 + lax.erf(x * (1.0 / math.sqrt(2.0))))


def _s5_discretize(lam_re, lam_im, log_dt, b_re, b_im, c_re, c_im):
    f32 = jnp.float32
    lr = lam_re.astype(f32)
    li = lam_im.astype(f32)
    dt = jnp.exp(log_dt.astype(f32))[:, None]
    mag = jnp.exp(lr * dt)
    ab_re = mag * jnp.cos(li * dt)
    ab_im = mag * jnp.sin(li * dt)
    den = lr * lr + li * li
    f_re = ((ab_re - 1.0) * lr + ab_im * li) / den
    f_im = (ab_im * lr - (ab_re - 1.0) * li) / den
    br = b_re.astype(f32)
    bi = b_im.astype(f32)
    bb_re = f_re[..., None] * br - f_im[..., None] * bi
    bb_im = f_re[..., None] * bi + f_im[..., None] * br
    eye = jnp.eye(GROUPS_PER_CHUNK, dtype=f32)

    def chunk_rows(v):
        return v.reshape(N_CHUNKS, 1, CHUNK_STATE)

    def in_blocks(bb):
        t = bb.reshape(N_CHUNKS, GROUPS_PER_CHUNK, STATE_DIM, GROUP_SIZE).transpose(0, 1, 3, 2)
        return jnp.einsum('mgjp,gh->mgjhp', t, eye).reshape(N_CHUNKS, LANES, CHUNK_STATE)

    def out_blocks(c):
        t = c.astype(f32).reshape(N_CHUNKS, GROUPS_PER_CHUNK, GROUP_SIZE, STATE_DIM).transpose(0, 1, 3, 2)
        return jnp.einsum('mgpj,gh->mgphj', t, eye).reshape(N_CHUNKS, CHUNK_STATE, LANES)

    bmat = jnp.concatenate([in_blocks(bb_re), in_blocks(bb_im)], axis=2)
    cmat = jnp.concatenate([out_blocks(c_re), -out_blocks(c_im)], axis=1)
    pr, pi = [ab_re], [ab_im]
    for _ in range(SUBLANES - 1):
        pr, pi = pr + [pr[-1] * ab_re - pi[-1] * ab_im], pi + [pr[-1] * ab_im + pi[-1] * ab_re]
    apr = jnp.concatenate([chunk_rows(v) for v in pr], axis=1)
    api = jnp.concatenate([chunk_rows(v) for v in pi], axis=1)
    return bmat, cmat, apr, api


def _s5_kernel(x_ref, g_ref, d_ref, b_ref, c_ref, apr_ref, api_ref, h0r_ref, h0i_ref,
               z_ref, sr_ref, si_ref, u_ref, st_ref, cr_ref, ci_ref, *, chain, blocks_per_seq):
    rb = pl.program_id(0)
    m = pl.program_id(1)
    rows = x_ref.shape[0]
    n_tiles = rows // SUBLANES

    @pl.when(m == 0)
    def _():
        u = _rmsnorm_rows(x_ref[...], g_ref[...])
        for mm in range(N_CHUNKS):
            u_ref[mm] = u[:, mm * LANES:(mm + 1) * LANES]

    u = u_ref[m]
    st_ref[...] = jnp.dot(u, b_ref[0], preferred_element_type=jnp.float32, precision=lax.Precision.HIGHEST)

    apr = apr_ref[0]
    api = api_ref[0]
    row = lax.broadcasted_iota(jnp.int32, (SUBLANES, CHUNK_STATE), 0)

    if chain:
        @pl.when(rb % blocks_per_seq == 0)
        def _():
            cr_ref[m] = h0r_ref[0, 0]
            ci_ref[m] = h0i_ref[0, 0]

    def tile_step(k, carry):
        r0 = pl.multiple_of(k * SUBLANES, SUBLANES)
        xr = st_ref[pl.ds(r0, SUBLANES), 0:CHUNK_STATE]
        xi = st_ref[pl.ds(r0, SUBLANES), CHUNK_STATE:2 * CHUNK_STATE]
        for d in (1, 2, 4):
            ar = apr[d - 1:d, :]
            ai = api[d - 1:d, :]
            sr = jnp.where(row >= d, pltpu.roll(xr, d, axis=0), 0.0)
            si = jnp.where(row >= d, pltpu.roll(xi, d, axis=0), 0.0)
            xr, xi = xr + ar * sr - ai * si, xi + ar * si + ai * sr
        if chain:
            cr, ci = carry
        else:
            cr = h0r_ref[k, 0]
            ci = h0i_ref[k, 0]
        hr = xr + apr * cr - api * ci
        hi = xi + apr * ci + api * cr
        st_ref[pl.ds(r0, SUBLANES), 0:CHUNK_STATE] = hr
        st_ref[pl.ds(r0, SUBLANES), CHUNK_STATE:2 * CHUNK_STATE] = hi
        lr_ = hr[SUBLANES - 1:SUBLANES, :]
        li_ = hi[SUBLANES - 1:SUBLANES, :]
        if chain:
            return lr_, li_
        sr_ref[k, m] = lr_
        si_ref[k, m] = li_
        return carry

    if chain:
        cr, ci = lax.fori_loop(0, n_tiles, tile_step, (cr_ref[m], ci_ref[m]))
        cr_ref[m] = cr
        ci_ref[m] = ci
        sr_ref[0, m] = cr
        si_ref[0, m] = ci
    else:
        lax.fori_loop(0, n_tiles, tile_step, 0)

    y = jnp.dot(st_ref[...].astype(jnp.bfloat16), c_ref[0], preferred_element_type=jnp.float32)
    y = y + d_ref[0] * u
    z_ref[...] = _gelu(y)


def _s5_mixer(x, h0r, h0i, g, d_skip, bmat, cmat_bf16, apr, api, *, chain, seq_len):
    T = x.shape[0]
    nseq = h0r.shape[0]
    rows = S5_ROWS
    if chain:
        blocks_per_seq = seq_len // rows
        seq_blk = 1
        seq_map = lambda rb, m: (rb // blocks_per_seq, m, 0, 0)
        out_map = lambda rb, m: (rb // blocks_per_seq, 0, 0, 0)
    else:
        assert seq_len == SUBLANES
        blocks_per_seq = 1
        seq_blk = rows // SUBLANES
        seq_map = lambda rb, m: (rb, m, 0, 0)
        out_map = lambda rb, m: (rb, 0, 0, 0)
    grid = (T // rows, N_CHUNKS)
    kern = functools.partial(_s5_kernel, chain=chain, blocks_per_seq=blocks_per_seq)
    st_spec = pl.BlockSpec((seq_blk, 1, 1, CHUNK_STATE), seq_map)
    out_st_spec = pl.BlockSpec((seq_blk, N_CHUNKS, 1, CHUNK_STATE), out_map)
    z, sr, si = pl.pallas_call(
        kern,
        grid=grid,
        in_specs=[
            pl.BlockSpec((rows, D_MODEL), lambda rb, m: (rb, 0)),
            pl.BlockSpec((1, D_MODEL), lambda rb, m: (0, 0)),
            pl.BlockSpec((1, 1, LANES), lambda rb, m: (m, 0, 0)),
            pl.BlockSpec((1, LANES, 2 * CHUNK_STATE), lambda rb, m: (m, 0, 0)),
            pl.BlockSpec((1, 2 * CHUNK_STATE, LANES), lambda rb, m: (m, 0, 0)),
            pl.BlockSpec((1, SUBLANES, CHUNK_STATE), lambda rb, m: (m, 0, 0)),
            pl.BlockSpec((1, SUBLANES, CHUNK_STATE), lambda rb, m: (m, 0, 0)),
            st_spec, st_spec,
        ],
        out_specs=[
            pl.BlockSpec((rows, LANES), lambda rb, m: (rb, m)),
            out_st_spec, out_st_spec,
        ],
        out_shape=[
            jax.ShapeDtypeStruct((T, D_MODEL), jnp.float32),
            jax.ShapeDtypeStruct((nseq, N_CHUNKS, 1, CHUNK_STATE), jnp.float32),
            jax.ShapeDtypeStruct((nseq, N_CHUNKS, 1, CHUNK_STATE), jnp.float32),
        ],
        scratch_shapes=[
            pltpu.VMEM((N_CHUNKS, rows, LANES), jnp.float32),
            pltpu.VMEM((rows, 2 * CHUNK_STATE), jnp.float32),
            pltpu.VMEM((N_CHUNKS, 1, CHUNK_STATE), jnp.float32),
            pltpu.VMEM((N_CHUNKS, 1, CHUNK_STATE), jnp.float32),
        ],
        compiler_params=pltpu.CompilerParams(
            dimension_semantics=("arbitrary", "arbitrary"), vmem_limit_bytes=VMEM_LIMIT_BYTES),
        name="s5_mixer",
    )(x, g.reshape(1, D_MODEL), d_skip.reshape(N_CHUNKS, 1, LANES), bmat, cmat_bf16, apr, api, h0r, h0i)
    return z, sr, si


def _state_to_chunks(h):
    return h.reshape(h.shape[0], N_CHUNKS, 1, CHUNK_STATE)


def _chunks_to_state(s):
    return s.reshape(s.shape[0], N_GROUPS, STATE_DIM)


def _glu_kernel(z_ref, res_ref, w_ref, o_ref):
    zz = jnp.dot(z_ref[...].astype(jnp.bfloat16), w_ref[...], preferred_element_type=jnp.float32)
    a = zz[:, :D_MODEL]
    b = zz[:, D_MODEL:]
    o_ref[...] = res_ref[...] + a * (1.0 / (1.0 + jnp.exp(-b)))


def _glu_residual(z, res, w_bf16):
    T = z.shape[0]
    rows = math.gcd(T, 512)
    return pl.pallas_call(
        _glu_kernel,
        grid=(T // rows,),
        in_specs=[
            pl.BlockSpec((rows, D_MODEL), lambda i: (i, 0)),
            pl.BlockSpec((rows, D_MODEL), lambda i: (i, 0)),
            pl.BlockSpec((D_MODEL, 2 * D_MODEL), lambda i: (0, 0)),
        ],
        out_specs=pl.BlockSpec((rows, D_MODEL), lambda i: (i, 0)),
        out_shape=jax.ShapeDtypeStruct((T, D_MODEL), jnp.float32),
        compiler_params=pltpu.CompilerParams(
            dimension_semantics=("arbitrary",), vmem_limit_bytes=VMEM_LIMIT_BYTES),
        name="glu_residual",
    )(z, res, w_bf16)


ROUTE_ROWS = 256
CAND = PEER_TOPK * PEER_TOPK


def _topk_rows(s, payload):
    n_rows = s.shape[0]
    row = lax.broadcasted_iota(jnp.int32, s.shape, 0)
    vals, picks = [], []
    for _ in range(PEER_TOPK):
        m = jnp.max(s, axis=0, keepdims=True)
        pos = jnp.min(jnp.where(s == m, row, n_rows), axis=0, keepdims=True)
        sel = row == pos
        vals.append(m)
        if payload is None:
            picks.append(pos)
        else:
            picks.append(jnp.max(jnp.where(sel, payload, -1), axis=0, keepdims=True))
        s = jnp.where(sel, -jnp.inf, s)
    return jnp.concatenate(vals, axis=0), jnp.concatenate(picks, axis=0)


def _peer_route_kernel(h_ref, g_ref, wq_ref, sk_ref, xn_ref, eid_ref, gate_ref, xb_ref):
    hd = pl.program_id(1)

    @pl.when(hd == 0)
    def _():
        xn = _rmsnorm_rows(h_ref[...], g_ref[...])
        xn_ref[...] = xn
        xb_ref[...] = xn.astype(jnp.bfloat16)

    q = jnp.dot(xb_ref[...], wq_ref[...], preferred_element_type=jnp.float32)
    sv, si = [], []
    for c in range(2):
        qc = q[:, c * PEER_DHALF:(c + 1) * PEER_DHALF].astype(jnp.bfloat16)
        st = lax.dot_general(sk_ref[0, c], qc, (((1,), (1,)), ((), ())), preferred_element_type=jnp.float32)
        v, i = _topk_rows(st, None)
        sv.append(v)
        si.append(i)
    cand = jnp.concatenate([sv[0][i:i + 1, :] + sv[1] for i in range(PEER_TOPK)], axis=0)
    cid = jnp.concatenate([si[0][i:i + 1, :] * N_KEYS + si[1] for i in range(PEER_TOPK)], axis=0)
    fv, eid = _topk_rows(cand, cid)
    e = jnp.exp(fv - fv[0:1, :])
    gate_ref[...] = e / jnp.sum(e, axis=0, keepdims=True)
    eid_ref[...] = eid


def _peer_route(h, g, wq_bf16, sk_bf16):
    T = h.shape[0]
    rows = ROUTE_ROWS
    n_sel = PEER_HEADS * PEER_TOPK
    return pl.pallas_call(
        _peer_route_kernel,
        grid=(T // rows, PEER_HEADS),
        in_specs=[
            pl.BlockSpec((rows, D_MODEL), lambda tb, hd: (tb, 0)),
            pl.BlockSpec((1, D_MODEL), lambda tb, hd: (0, 0)),
            pl.BlockSpec((D_MODEL, 2 * PEER_DHALF), lambda tb, hd: (0, hd)),
            pl.BlockSpec((1, 2, N_KEYS, PEER_DHALF), lambda tb, hd: (hd, 0, 0, 0)),
        ],
        out_specs=[
            pl.BlockSpec((rows, D_MODEL), lambda tb, hd: (tb, 0)),
            pl.BlockSpec((PEER_TOPK, rows), lambda tb, hd: (hd, tb)),
            pl.BlockSpec((PEER_TOPK, rows), lambda tb, hd: (hd, tb)),
        ],
        out_shape=[
            jax.ShapeDtypeStruct((T, D_MODEL), jnp.float32),
            jax.ShapeDtypeStruct((n_sel, T), jnp.int32),
            jax.ShapeDtypeStruct((n_sel, T), jnp.float32),
        ],
        scratch_shapes=[pltpu.VMEM((rows, D_MODEL), jnp.bfloat16)],
        compiler_params=pltpu.CompilerParams(
            dimension_semantics=("arbitrary", "arbitrary"), vmem_limit_bytes=VMEM_LIMIT_BYTES),
        name="peer_route",
    )(h, g.reshape(1, D_MODEL), wq_bf16, sk_bf16)


N_SEL = PEER_HEADS * PEER_TOPK
SC_LANES = 16
GATHER_ROWS = PEER_TOPK
GATHERS_PER_TOKEN = N_SEL // GATHER_ROWS
GATHER_BUFS = 4
SC_TOKENS = 8


def _sc_gather_stream(wid, n_batches, tab_hbm, stage_srcs, stage_bufs, out_hbm, o_v, bufs, gsem, ssem, osem,
                      compute):
    idx_v = stage_bufs[0]

    def stage_copies(bi, slot):
        base = (wid * n_batches + bi) * SC_TOKENS
        return [pltpu.make_async_copy(src.at[pl.ds(base, SC_TOKENS)], buf.at[slot], ssem.at[slot])
                for src, buf in zip(stage_srcs, stage_bufs)]

    def out_copy(bi, slot):
        base = (wid * n_batches + bi) * SC_TOKENS
        return pltpu.make_async_copy(o_v.at[slot], out_hbm.at[pl.ds(base, SC_TOKENS)], osem.at[slot])

    def start(slot, t, kk, b):
        idx = idx_v[slot, t, pl.ds(kk * GATHER_ROWS, GATHER_ROWS)]
        pltpu.async_copy(tab_hbm.at[idx], bufs.at[b], gsem.at[b])

    def wait(b):
        pltpu.make_async_copy(tab_hbm.at[pl.ds(0, GATHER_ROWS)], bufs.at[b], gsem.at[b]).wait()

    for c in stage_copies(0, 0):
        c.start()
    for c in stage_copies(0, 0):
        c.wait()
    for q in range(GATHER_BUFS - 1):
        start(0, q // GATHERS_PER_TOKEN, q % GATHERS_PER_TOKEN, q % GATHER_BUFS)

    def batch(bi, carry):
        slot = bi % 2
        has_next = bi + 1 < n_batches

        @pl.when(has_next)
        def _():
            for c in stage_copies(bi + 1, 1 - slot):
                c.start()

        @pl.when(bi >= 2)
        def _():
            out_copy(bi, slot).wait()

        def tok(t, carry):
            @pl.when(jnp.logical_and(t == SC_TOKENS - 1, has_next))
            def _():
                for c in stage_copies(bi + 1, 1 - slot):
                    c.wait()

            for kk in range(GATHERS_PER_TOKEN):
                nq = kk + GATHER_BUFS - 1
                nk, nb = nq % GATHERS_PER_TOKEN, nq % GATHER_BUFS
                if nq < GATHERS_PER_TOKEN:
                    start(slot, t, nk, nb)
                else:
                    @pl.when(t + 1 < SC_TOKENS)
                    def _():
                        start(slot, t + 1, nk, nb)

                    @pl.when(jnp.logical_and(t + 1 == SC_TOKENS, has_next))
                    def _():
                        start(1 - slot, 0, nk, nb)

                wait(kk % GATHER_BUFS)
                compute(slot, t, kk, kk % GATHER_BUFS)
            return carry

        lax.fori_loop(0, SC_TOKENS, tok, 0)
        out_copy(bi, slot).start()
        return carry

    lax.fori_loop(0, n_batches, batch, 0)
    if n_batches >= 2:
        out_copy(n_batches - 2, (n_batches - 2) % 2).wait()
    out_copy(n_batches - 1, (n_batches - 1) % 2).wait()


def _sc_mesh_and_batches(n_tokens):
    info = plsc.get_sparse_core_info()
    assert info.num_lanes == SC_LANES
    n_workers = info.num_cores * info.num_subcores
    assert n_tokens % (n_workers * SC_TOKENS) == 0
    mesh = plsc.VectorSubcoreMesh(core_axis_name="c", subcore_axis_name="s")
    return info, mesh, n_tokens // (n_workers * SC_TOKENS)


def _peer_hidden_sc(eid, xn, u_tab):
    T = eid.shape[0]
    info, mesh, n_batches = _sc_mesh_and_batches(T)

    @functools.partial(
        pl.kernel, mesh=mesh,
        out_type=jax.ShapeDtypeStruct((T, N_SEL), jnp.float32),
        scratch_types=[
            pltpu.VMEM((2, SC_TOKENS, N_SEL), jnp.int32),
            pltpu.VMEM((2, SC_TOKENS, D_MODEL), jnp.float32),
            pltpu.VMEM((2, SC_TOKENS, N_SEL), jnp.float32),
            pltpu.VMEM((GATHER_BUFS, GATHER_ROWS, D_MODEL), jnp.float32),
            pltpu.VMEM((GATHER_ROWS, SC_LANES), jnp.float32),
            pltpu.SemaphoreType.DMA((GATHER_BUFS,)),
            pltpu.SemaphoreType.DMA((2,)),
            pltpu.SemaphoreType.DMA((2,)),
        ],
        compiler_params=pltpu.CompilerParams(needs_layout_passes=False),
        name="peer_hidden_sc",
    )
    def k(eid_hbm, xn_hbm, u_hbm, out_hbm, idx_v, x_v, o_v, bufs, acc_v, gsem, ssem, osem):
        wid = lax.axis_index("s") * info.num_cores + lax.axis_index("c")
        lane = lax.iota(jnp.int32, SC_LANES)
        zero = jnp.zeros((SC_LANES,), jnp.float32)

        def compute(slot, t, kk, b):
            @plsc.parallel_loop(0, D_MODEL // SC_LANES, carry=(zero,) * GATHER_ROWS)
            def accs(c, accs):
                xc = x_v[slot, t, pl.ds(c * SC_LANES, SC_LANES)]
                return tuple(accs[r] + bufs[b, r, pl.ds(c * SC_LANES, SC_LANES)] * xc for r in range(GATHER_ROWS))

            for r in range(GATHER_ROWS):
                acc_v[r, :] = accs[r]
            tot = zero
            for c in range(SC_LANES):
                tot = tot + plsc.load_gather(acc_v, [lane, jnp.full((SC_LANES,), c, jnp.int32)])
            o_v[slot, t, pl.ds(kk * GATHER_ROWS, GATHER_ROWS)] = tot

        _sc_gather_stream(wid, n_batches, u_hbm, [eid_hbm, xn_hbm], [idx_v, x_v], out_hbm, o_v, bufs,
                          gsem, ssem, osem, compute)

    return k(eid, xn, u_tab)


def _peer_combine_sc(eid, a, v_tab):
    T = eid.shape[0]
    info, mesh, n_batches = _sc_mesh_and_batches(T)

    @functools.partial(
        pl.kernel, mesh=mesh,
        out_type=jax.ShapeDtypeStruct((T, D_MODEL), jnp.float32),
        scratch_types=[
            pltpu.VMEM((2, SC_TOKENS, N_SEL), jnp.int32),
            pltpu.VMEM((2, SC_TOKENS, N_SEL), jnp.float32),
            pltpu.VMEM((2, SC_TOKENS, D_MODEL), jnp.float32),
            pltpu.VMEM((GATHER_BUFS, GATHER_ROWS, D_MODEL), jnp.float32),
            pltpu.SemaphoreType.DMA((GATHER_BUFS,)),
            pltpu.SemaphoreType.DMA((2,)),
            pltpu.SemaphoreType.DMA((2,)),
        ],
        compiler_params=pltpu.CompilerParams(needs_layout_passes=False),
        name="peer_combine_sc",
    )
    def k(eid_hbm, a_hbm, v_hbm, out_hbm, idx_v, a_v, o_v, bufs, gsem, ssem, osem):
        wid = lax.axis_index("s") * info.num_cores + lax.axis_index("c")

        def compute(slot, t, kk, b):
            svec = jnp.full((SC_LANES,), slot, jnp.int32)
            tvec = jnp.full((SC_LANES,), t, jnp.int32)
            ws = [plsc.load_gather(a_v, [svec, tvec, jnp.full((SC_LANES,), kk * GATHER_ROWS + r, jnp.int32)])
                  for r in range(GATHER_ROWS)]

            @plsc.parallel_loop(0, D_MODEL // SC_LANES, unroll=2)
            def _(c):
                sl = pl.ds(c * SC_LANES, SC_LANES)
                terms = [ws[r] * bufs[b, r, sl] for r in range(GATHER_ROWS)]
                if kk != 0:
                    terms.append(o_v[slot, t, sl])
                while len(terms) > 1:
                    pairs = [terms[i] + terms[i + 1] for i in range(0, len(terms) - 1, 2)]
                    terms = pairs + ([terms[-1]] if len(terms) % 2 else [])
                o_v[slot, t, sl] = terms[0]

        _sc_gather_stream(wid, n_batches, v_hbm, [eid_hbm, a_hbm], [idx_v, a_v], out_hbm, o_v, bufs,
                          gsem, ssem, osem, compute)

    return k(eid, a, v_tab)


def _peer_act_kernel(hp_ref, gate_ref, a_ref):
    a_ref[...] = _gelu(hp_ref[...]) * gate_ref[...]


def _peer_act(hpre, gate, rows=1088):
    T = hpre.shape[0]
    spec = pl.BlockSpec((rows, N_SEL), lambda i: (i, 0))
    return pl.pallas_call(
        _peer_act_kernel, grid=(T // rows,), in_specs=[spec, spec], out_specs=spec,
        out_shape=jax.ShapeDtypeStruct((T, N_SEL), jnp.float32),
        compiler_params=pltpu.CompilerParams(dimension_semantics=("arbitrary",)),
        name="peer_act",
    )(hpre, gate)


def _residual_kernel(h_ref, c_ref, g_ref, o_ref, *, final_norm):
    y = h_ref[...] + c_ref[...]
    o_ref[...] = _rmsnorm_rows(y, g_ref[...]) if final_norm else y


def _residual(h, c, gfin, *, final_norm, rows=1088):
    T = h.shape[0]
    spec = pl.BlockSpec((rows, D_MODEL), lambda i: (i, 0))
    return pl.pallas_call(
        functools.partial(_residual_kernel, final_norm=final_norm), grid=(T // rows,),
        in_specs=[spec, spec, pl.BlockSpec((1, D_MODEL), lambda i: (0, 0))], out_specs=spec,
        out_shape=jax.ShapeDtypeStruct((T, D_MODEL), jnp.float32),
        compiler_params=pltpu.CompilerParams(dimension_semantics=("arbitrary",)),
        name="peer_residual",
    )(h, c, gfin.reshape(1, D_MODEL))


def _peer_experts(eid, xn, gate, h, gfin, u_tab, v_tab, *, final_norm):
    hpre = _peer_hidden_sc(eid, xn, u_tab)
    a = _peer_act(hpre, gate)
    c = _peer_combine_sc(eid, a, v_tab)
    return _residual(h, c, gfin, final_norm=final_norm)


KV_WIDTH = N_KV_HEADS * HEAD_DIM
BF16 = jnp.bfloat16


def _qkv(h, gkv, gq, wkv_ref, wq_ref):
    kv = jnp.dot(_rmsnorm_rows(h, gkv).astype(BF16), wkv_ref[...], preferred_element_type=jnp.float32)
    q = jnp.dot(_rmsnorm_rows(h, gq).astype(BF16), wq_ref[...], preferred_element_type=jnp.float32)
    return kv, q


def _sink_softmax_pv(parts, sink):
    m = sink
    for s, _ in parts:
        m = jnp.maximum(m, jnp.max(s, axis=-1, keepdims=True))
    den = jnp.exp(sink - m)
    acc = None
    for s, v in parts:
        e = jnp.exp(s - m)
        den = den + jnp.sum(e, axis=-1, keepdims=True)
        pv = jnp.dot(e.astype(BF16), v.astype(BF16), preferred_element_type=jnp.float32)
        acc = pv if acc is None else acc + pv
    return acc / den


def _nt_dot(a, b):
    return lax.dot_general(a.astype(BF16), b.astype(BF16), (((1,), (1,)), ((), ())),
                           preferred_element_type=jnp.float32)


def _attn_prompt_kernel(h_ref, gkv_ref, gq_ref, wkv_ref, wq_ref, wo_ref, sink_ref,
                        o_ref, kvw_ref, prev_ref):
    blk = pl.program_id(1)
    h = h_ref[...]
    kv, q = _qkv(h, gkv_ref[...], gq_ref[...], wkv_ref, wq_ref)
    kvw_ref[0] = kv

    @pl.when(blk == 0)
    def _():
        prev_ref[...] = jnp.zeros_like(prev_ref)

    prev = prev_ref[...]
    qi = lax.broadcasted_iota(jnp.int32, (WINDOW, WINDOW), 0)
    kj = lax.broadcasted_iota(jnp.int32, (WINDOW, WINDOW), 1)
    prev_ok = jnp.logical_and(kj > qi, blk > 0)
    cur_ok = kj <= qi
    heads = []
    for kvh in range(N_KV_HEADS):
        ks = slice(kvh * HEAD_DIM, (kvh + 1) * HEAD_DIM)
        vs = slice(KV_WIDTH + kvh * HEAD_DIM, KV_WIDTH + (kvh + 1) * HEAD_DIM)
        for g in range(Q_PER_KV):
            hq = kvh * Q_PER_KV + g
            qh = q[:, hq * HEAD_DIM:(hq + 1) * HEAD_DIM]
            sp = jnp.where(prev_ok, _nt_dot(qh, prev[:, ks]) * ATTN_SCALE, -jnp.inf)
            sc = jnp.where(cur_ok, _nt_dot(qh, kv[:, ks]) * ATTN_SCALE, -jnp.inf)
            heads.append(_sink_softmax_pv([(sp, prev[:, vs]), (sc, kv[:, vs])], sink_ref[hq]))
    o = jnp.concatenate(heads, axis=1)
    o_ref[...] = h + jnp.dot(o.astype(BF16), wo_ref[...], preferred_element_type=jnp.float32)
    prev_ref[...] = kv


def _attn_prompt(h, gkv, gq, wkv, wq, wo, sinks, *, n_seq, seq_len):
    nb = seq_len // WINDOW
    row_spec = pl.BlockSpec((WINDOW, D_MODEL), lambda n, b: (n * nb + b, 0))
    full = lambda shape: pl.BlockSpec(shape, lambda n, b: (0,) * len(shape))
    return pl.pallas_call(
        _attn_prompt_kernel,
        grid=(n_seq, nb),
        in_specs=[
            row_spec, full((1, D_MODEL)), full((1, D_MODEL)),
            full((D_MODEL, 2 * KV_WIDTH)), full((D_MODEL, D_MODEL)), full((D_MODEL, D_MODEL)),
            pl.BlockSpec(memory_space=pltpu.SMEM),
        ],
        out_specs=[row_spec, pl.BlockSpec((1, WINDOW, 2 * KV_WIDTH), lambda n, b: (n, 0, 0))],
        out_shape=[
            jax.ShapeDtypeStruct(h.shape, jnp.float32),
            jax.ShapeDtypeStruct((n_seq, WINDOW, 2 * KV_WIDTH), jnp.float32),
        ],
        scratch_shapes=[pltpu.VMEM((WINDOW, 2 * KV_WIDTH), jnp.float32)],
        compiler_params=pltpu.CompilerParams(
            dimension_semantics=("arbitrary", "arbitrary"), vmem_limit_bytes=VMEM_LIMIT_BYTES),
        name="attn_prompt",
    )(h, gkv.reshape(1, D_MODEL), gq.reshape(1, D_MODEL), wkv, wq, wo, sinks)


ATTN_SEQS = 16


def _attn_sample_kernel(h_ref, ck_ref, cv_ref, gkv_ref, gq_ref, wkv_ref, wq_ref, wo_ref, sink_ref,
                        o_ref, kw_ref, vw_ref, kv_ref, q_ref, att_ref, *, n_new):
    h = h_ref[...]
    kv, q = _qkv(h, gkv_ref[...], gq_ref[...], wkv_ref, wq_ref)
    kv_ref[...] = kv
    q_ref[...] = q
    n_seq = h.shape[0] // n_new
    rows = Q_PER_KV * n_new
    qpos_c = lax.broadcasted_iota(jnp.int32, (rows, WINDOW), 0) % n_new
    cache_ok = lax.broadcasted_iota(jnp.int32, (rows, WINDOW), 1) > qpos_c
    qpos_n = lax.broadcasted_iota(jnp.int32, (rows, n_new), 0) % n_new
    new_ok = lax.broadcasted_iota(jnp.int32, (rows, n_new), 1) <= qpos_n

    def per_seq(n, carry):
        r0 = pl.multiple_of(n * n_new, n_new)
        kvn = kv_ref[pl.ds(r0, n_new), :]
        qn = q_ref[pl.ds(r0, n_new), :]
        ck = ck_ref[n]
        cv = cv_ref[n]
        kw_ref[n, 0:WINDOW - n_new, :] = ck[n_new:, :]
        kw_ref[n, WINDOW - n_new:WINDOW, :] = kvn[:, 0:KV_WIDTH]
        vw_ref[n, 0:WINDOW - n_new, :] = cv[n_new:, :]
        vw_ref[n, WINDOW - n_new:WINDOW, :] = kvn[:, KV_WIDTH:2 * KV_WIDTH]
        outs = []
        for kvh in range(N_KV_HEADS):
            ks = slice(kvh * HEAD_DIM, (kvh + 1) * HEAD_DIM)
            vs = slice(KV_WIDTH + kvh * HEAD_DIM, KV_WIDTH + (kvh + 1) * HEAD_DIM)
            qs = jnp.concatenate(
                [qn[:, (kvh * Q_PER_KV + g) * HEAD_DIM:(kvh * Q_PER_KV + g + 1) * HEAD_DIM] for g in range(Q_PER_KV)],
                axis=0)
            s_c = jnp.where(cache_ok, _nt_dot(qs, ck[:, ks]) * ATTN_SCALE, -jnp.inf)
            s_n = jnp.where(new_ok, _nt_dot(qs, kvn[:, ks]) * ATTN_SCALE, -jnp.inf)
            sink = sink_ref[kvh][:, 0:1]
            o = _sink_softmax_pv([(s_c, cv[:, ks]), (s_n, kvn[:, vs])], sink)
            outs += [o[g * n_new:(g + 1) * n_new, :] for g in range(Q_PER_KV)]
        att_ref[pl.ds(r0, n_new), :] = jnp.concatenate(outs, axis=1)
        return carry

    lax.fori_loop(0, n_seq, per_seq, 0)
    o_ref[...] = h + jnp.dot(att_ref[...].astype(BF16), wo_ref[...], preferred_element_type=jnp.float32)


def _attn_sample(h, cache_k, cache_v, gkv, gq, wkv, wq, wo, sinks, *, n_new):
    n_seq = cache_k.shape[0]
    sb = ATTN_SEQS
    rows = sb * n_new
    row_spec = pl.BlockSpec((rows, D_MODEL), lambda i: (i, 0))
    win_spec = pl.BlockSpec((sb, WINDOW, KV_WIDTH), lambda i: (i, 0, 0))
    full = lambda shape: pl.BlockSpec(shape, lambda i: (0,) * len(shape))
    sink_rows = jnp.repeat(sinks.reshape(N_KV_HEADS, Q_PER_KV), n_new, axis=1)[:, :, None]
    sink_rows = jnp.broadcast_to(sink_rows, (N_KV_HEADS, Q_PER_KV * n_new, LANES))
    return pl.pallas_call(
        functools.partial(_attn_sample_kernel, n_new=n_new),
        grid=(n_seq // sb,),
        in_specs=[
            row_spec, win_spec, win_spec, full((1, D_MODEL)), full((1, D_MODEL)),
            full((D_MODEL, 2 * KV_WIDTH)), full((D_MODEL, D_MODEL)), full((D_MODEL, D_MODEL)),
            full((N_KV_HEADS, Q_PER_KV * n_new, LANES)),
        ],
        out_specs=[row_spec, win_spec, win_spec],
        out_shape=[
            jax.ShapeDtypeStruct(h.shape, jnp.float32),
            jax.ShapeDtypeStruct((n_seq, WINDOW, KV_WIDTH), jnp.float32),
            jax.ShapeDtypeStruct((n_seq, WINDOW, KV_WIDTH), jnp.float32),
        ],
        scratch_shapes=[
            pltpu.VMEM((rows, 2 * KV_WIDTH), jnp.float32),
            pltpu.VMEM((rows, D_MODEL), jnp.float32),
            pltpu.VMEM((rows, D_MODEL), jnp.float32),
        ],
        compiler_params=pltpu.CompilerParams(
            dimension_semantics=("arbitrary",), vmem_limit_bytes=VMEM_LIMIT_BYTES),
        name="attn_sample",
    )(h, cache_k, cache_v, gkv.reshape(1, D_MODEL), gq.reshape(1, D_MODEL), wkv, wq, wo, sink_rows)


def kernel(x_prompt, x_sample, state_ssm_re, state_ssm_im, cache_k_win, cache_v_win, norm_mix, norm_ffn, norm_kv, norm_final, ssm_lam_re, ssm_lam_im, ssm_log_dt, ssm_b_re, ssm_b_im, ssm_c_re, ssm_c_im, ssm_d, ssm_w_glu, w_kv, w_q, attn_sinks, w_o, peer_w_q, peer_sub_keys, peer_u, peer_v):
    bmat, cmat, apr, api = _s5_discretize(ssm_lam_re[0], ssm_lam_im[0], ssm_log_dt[0], ssm_b_re[0], ssm_b_im[0], ssm_c_re[0], ssm_c_im[0])
    cmat = cmat.astype(jnp.bfloat16)
    wglu = ssm_w_glu[0].astype(jnp.bfloat16)
    wkv = w_kv.astype(BF16)
    wq = w_q[0].astype(BF16)
    wo = w_o[0].astype(BF16)
    peer_wq = [peer_w_q[layer].astype(BF16) for layer in range(2)]
    peer_sk = [peer_sub_keys[layer].astype(BF16) for layer in range(2)]

    def peer(h, layer, final_norm):
        xn, eid_t, gate_t = _peer_route(h, norm_ffn[layer], peer_wq[layer], peer_sk[layer])
        return _peer_experts(eid_t.T, xn, gate_t.T, h, norm_final, peer_u[layer], peer_v[layer],
                             final_norm=final_norm)

    def trunk(x_p, x_s, s_re, s_im, c_k, c_v):
        n_p, n_s = x_p.shape[0], x_s.shape[0]
        xp = x_p.reshape(-1, D_MODEL)
        xs = x_s.reshape(-1, D_MODEL)
        n_prompt_rows = xp.shape[0]
        z0 = jnp.zeros((n_p, N_CHUNKS, 1, CHUNK_STATE), jnp.float32)
        zp, srp, sip = _s5_mixer(xp, z0, z0, norm_mix[0], ssm_d[0], bmat, cmat, apr, api,
                                 chain=True, seq_len=x_p.shape[1])
        zs, srs, sis = _s5_mixer(xs, _state_to_chunks(s_re), _state_to_chunks(s_im), norm_mix[0], ssm_d[0],
                                 bmat, cmat, apr, api, chain=False, seq_len=x_s.shape[1])
        h1 = jnp.concatenate([_glu_residual(zp, xp, wglu), _glu_residual(zs, xs, wglu)], axis=0)
        h2 = peer(h1, 0, False)
        h3p, kvw_p = _attn_prompt(h2[:n_prompt_rows], norm_kv, norm_mix[1], wkv, wq, wo, attn_sinks[0],
                                  n_seq=n_p, seq_len=x_p.shape[1])
        h3s, kw_s, vw_s = _attn_sample(h2[n_prompt_rows:], c_k.reshape(n_s, WINDOW, KV_WIDTH),
                                       c_v.reshape(n_s, WINDOW, KV_WIDTH), norm_kv, norm_mix[1],
                                       wkv, wq, wo, attn_sinks[0], n_new=x_s.shape[1])
        y = peer(jnp.concatenate([h3p, h3s], axis=0), 1, True)
        win = lambda a: a.reshape(a.shape[0], WINDOW, N_KV_HEADS, HEAD_DIM)
        return (y[:n_prompt_rows].reshape(x_p.shape), y[n_prompt_rows:].reshape(x_s.shape),
                _chunks_to_state(srp), _chunks_to_state(sip),
                win(kvw_p[:, :, :KV_WIDTH]), win(kvw_p[:, :, KV_WIDTH:]),
                _chunks_to_state(srs), _chunks_to_state(sis), win(kw_s), win(vw_s))

    n_groups = 4
    gp = x_prompt.shape[0] // n_groups
    gs = x_sample.shape[0] // n_groups
    outs = [trunk(x_prompt[i * gp:(i + 1) * gp], x_sample[i * gs:(i + 1) * gs],
                  state_ssm_re[0, i * gs:(i + 1) * gs], state_ssm_im[0, i * gs:(i + 1) * gs],
                  cache_k_win[i * gs:(i + 1) * gs], cache_v_win[i * gs:(i + 1) * gs])
            for i in range(n_groups)]
    cat = [jnp.concatenate(parts, axis=0) for parts in zip(*outs)]
    return (cat[0], cat[1], cat[2][None], cat[3][None], cat[4], cat[5], cat[6][None], cat[7][None],
            cat[8], cat[9])
```

```python
import functools
import math

import jax
import jax.numpy as jnp
from jax import lax
from jax.experimental import pallas as pl
from jax.experimental.pallas import tpu as pltpu
from jax.experimental.pallas import tpu_sc as plsc

D_MODEL = 1024
GROUP_SIZE = 16
N_GROUPS = D_MODEL // GROUP_SIZE
STATE_DIM = 64
HEAD_DIM = 64
N_Q_HEADS = D_MODEL // HEAD_DIM
N_KV_HEADS = N_Q_HEADS // 8
Q_PER_KV = N_Q_HEADS // N_KV_HEADS
WINDOW = 128
PAST_LEN = 16384
ATTN_SCALE = 1.0 / math.sqrt(HEAD_DIM)
PEER_HEADS = 8
N_KEYS = 128
PEER_TOPK = 16
PEER_DHALF = 128
EPS = 1e-5

LANES = 128
SUBLANES = 8
VMEM_LIMIT_BYTES = 56 * 1024 * 1024

GROUPS_PER_CHUNK = LANES // GROUP_SIZE
N_CHUNKS = N_GROUPS // GROUPS_PER_CHUNK
CHUNK_STATE = GROUPS_PER_CHUNK * STATE_DIM
S5_ROWS = 256


def _rmsnorm_rows(x, g):
    r = lax.rsqrt(jnp.mean(x * x, axis=-1, keepdims=True) + EPS)
    return x * r * g


def _gelu(x):
    return 0.5 * x * (1.0 + lax.erf(x * (1.0 / math.sqrt(2.0))))


def _s5_discretize(lam_re, lam_im, log_dt, b_re, b_im, c_re, c_im):
    f32 = jnp.float32
    lr = lam_re.astype(f32)
    li = lam_im.astype(f32)
    dt = jnp.exp(log_dt.astype(f32))[:, None]
    mag = jnp.exp(lr * dt)
    ab_re = mag * jnp.cos(li * dt)
    ab_im = mag * jnp.sin(li * dt)
    den = lr * lr + li * li
    f_re = ((ab_re - 1.0) * lr + ab_im * li) / den
    f_im = (ab_im * lr - (ab_re - 1.0) * li) / den
    br = b_re.astype(f32)
    bi = b_im.astype(f32)
    bb_re = f_re[..., None] * br - f_im[..., None] * bi
    bb_im = f_re[..., None] * bi + f_im[..., None] * br
    eye = jnp.eye(GROUPS_PER_CHUNK, dtype=f32)

    def chunk_rows(v):
        return v.reshape(N_CHUNKS, 1, CHUNK_STATE)

    def in_blocks(bb):
        t = bb.reshape(N_CHUNKS, GROUPS_PER_CHUNK, STATE_DIM, GROUP_SIZE).transpose(0, 1, 3, 2)
        return jnp.einsum('mgjp,gh->mgjhp', t, eye).reshape(N_CHUNKS, LANES, CHUNK_STATE)

    def out_blocks(c):
        t = c.astype(f32).reshape(N_CHUNKS, GROUPS_PER_CHUNK, GROUP_SIZE, STATE_DIM).transpose(0, 1, 3, 2)
        return jnp.einsum('mgpj,gh->mgphj', t, eye).reshape(N_CHUNKS, CHUNK_STATE, LANES)

    bfull = jnp.concatenate([in_blocks(bb_re), in_blocks(bb_im)], axis=2)
    b_hi = bfull.astype(jnp.bfloat16)
    b_lo = (bfull - b_hi.astype(f32)).astype(jnp.bfloat16)
    bmat = (jnp.concatenate([b_hi, b_hi], axis=1), b_lo)
    cmat = jnp.concatenate([out_blocks(c_re), -out_blocks(c_im)], axis=1)
    pr, pi = [ab_re], [ab_im]
    for _ in range(SUBLANES - 1):
        pr, pi = pr + [pr[-1] * ab_re - pi[-1] * ab_im], pi + [pr[-1] * ab_im + pi[-1] * ab_re]
    apr = jnp.concatenate([chunk_rows(v) for v in pr], axis=1)
    api = jnp.concatenate([chunk_rows(v) for v in pi], axis=1)
    return bmat, cmat, apr, api


def _split_bf16(x):
    hi = x.astype(jnp.bfloat16)
    return hi, (x - hi.astype(jnp.float32)).astype(jnp.bfloat16)


def _s5_kernel(x_ref, g_ref, d_ref, bhh_ref, blo_ref, c_ref, apr_ref, api_ref, h0r_ref, h0i_ref,
               z_ref, sr_ref, si_ref, u_ref, us_ref, st_ref, cr_ref, ci_ref, *, chain, blocks_per_seq):
    rb = pl.program_id(0)
    m = pl.program_id(1)
    rows = x_ref.shape[0]
    n_tiles = rows // SUBLANES

    @pl.when(m == 0)
    def _():
        u = _rmsnorm_rows(x_ref[...], g_ref[...])
        for mm in range(N_CHUNKS):
            uc = u[:, mm * LANES:(mm + 1) * LANES]
            u_ref[mm] = uc
            hi, lo = _split_bf16(uc)
            us_ref[mm, :, 0:LANES] = hi
            us_ref[mm, :, LANES:2 * LANES] = lo

    u = u_ref[m]
    us = us_ref[m]
    st_ref[...] = (jnp.dot(us, bhh_ref[0], preferred_element_type=jnp.float32)
                   + jnp.dot(us[:, 0:LANES], blo_ref[0], preferred_element_type=jnp.float32))

    apr = apr_ref[0]
    api = api_ref[0]
    row = lax.broadcasted_iota(jnp.int32, (SUBLANES, CHUNK_STATE), 0)

    if chain:
        @pl.when(rb % blocks_per_seq == 0)
        def _():
            cr_ref[m] = h0r_ref[0, 0]
            ci_ref[m] = h0i_ref[0, 0]

    def tile_step(k, carry):
        r0 = pl.multiple_of(k * SUBLANES, SUBLANES)
        xr = st_ref[pl.ds(r0, SUBLANES), 0:CHUNK_STATE]
        xi = st_ref[pl.ds(r0, SUBLANES), CHUNK_STATE:2 * CHUNK_STATE]
        for d in (1, 2, 4):
            ar = apr[d - 1:d, :]
            ai = api[d - 1:d, :]
            sr = jnp.where(row >= d, pltpu.roll(xr, d, axis=0), 0.0)
            si = jnp.where(row >= d, pltpu.roll(xi, d, axis=0), 0.0)
            xr, xi = xr + ar * sr - ai * si, xi + ar * si + ai * sr
        if chain:
            cr, ci = carry
        else:
            cr = h0r_ref[k, 0]
            ci = h0i_ref[k, 0]
        hr = xr + apr * cr - api * ci
        hi = xi + apr * ci + api * cr
        st_ref[pl.ds(r0, SUBLANES), 0:CHUNK_STATE] = hr
        st_ref[pl.ds(r0, SUBLANES), CHUNK_STATE:2 * CHUNK_STATE] = hi
        lr_ = hr[SUBLANES - 1:SUBLANES, :]
        li_ = hi[SUBLANES - 1:SUBLANES, :]
        if chain:
            return lr_, li_
        sr_ref[k, m] = lr_
        si_ref[k, m] = li_
        return carry

    if chain:
        cr, ci = lax.fori_loop(0, n_tiles, tile_step, (cr_ref[m], ci_ref[m]))
        cr_ref[m] = cr
        ci_ref[m] = ci
        sr_ref[0, m] = cr
        si_ref[0, m] = ci
    else:
        lax.fori_loop(0, n_tiles, tile_step, 0)

    y = jnp.dot(st_ref[...].astype(jnp.bfloat16), c_ref[0], preferred_element_type=jnp.float32)
    y = y + d_ref[0] * u
    z_ref[...] = _gelu(y)


def _s5_mixer(x, h0r, h0i, g, d_skip, bmat, cmat_bf16, apr, api, *, chain, seq_len):
    T = x.shape[0]
    nseq = h0r.shape[0]
    rows = S5_ROWS
    if chain:
        blocks_per_seq = seq_len // rows
        seq_blk = 1
        seq_map = lambda rb, m: (rb // blocks_per_seq, m, 0, 0)
        out_map = lambda rb, m: (rb // blocks_per_seq, 0, 0, 0)
    else:
        assert seq_len == SUBLANES
        blocks_per_seq = 1
        seq_blk = rows // SUBLANES
        seq_map = lambda rb, m: (rb, m, 0, 0)
        out_map = lambda rb, m: (rb, 0, 0, 0)
    grid = (T // rows, N_CHUNKS)
    kern = functools.partial(_s5_kernel, chain=chain, blocks_per_seq=blocks_per_seq)
    st_spec = pl.BlockSpec((seq_blk, 1, 1, CHUNK_STATE), seq_map)
    out_st_spec = pl.BlockSpec((seq_blk, N_CHUNKS, 1, CHUNK_STATE), out_map)
    z, sr, si = pl.pallas_call(
        kern,
        grid=grid,
        in_specs=[
            pl.BlockSpec((rows, D_MODEL), lambda rb, m: (rb, 0)),
            pl.BlockSpec((1, D_MODEL), lambda rb, m: (0, 0)),
            pl.BlockSpec((1, 1, LANES), lambda rb, m: (m, 0, 0)),
            pl.BlockSpec((1, 2 * LANES, 2 * CHUNK_STATE), lambda rb, m: (m, 0, 0)),
            pl.BlockSpec((1, LANES, 2 * CHUNK_STATE), lambda rb, m: (m, 0, 0)),
            pl.BlockSpec((1, 2 * CHUNK_STATE, LANES), lambda rb, m: (m, 0, 0)),
            pl.BlockSpec((1, SUBLANES, CHUNK_STATE), lambda rb, m: (m, 0, 0)),
            pl.BlockSpec((1, SUBLANES, CHUNK_STATE), lambda rb, m: (m, 0, 0)),
            st_spec, st_spec,
        ],
        out_specs=[
            pl.BlockSpec((rows, LANES), lambda rb, m: (rb, m)),
            out_st_spec, out_st_spec,
        ],
        out_shape=[
            jax.ShapeDtypeStruct((T, D_MODEL), jnp.float32),
            jax.ShapeDtypeStruct((nseq, N_CHUNKS, 1, CHUNK_STATE), jnp.float32),
            jax.ShapeDtypeStruct((nseq, N_CHUNKS, 1, CHUNK_STATE), jnp.float32),
        ],
        scratch_shapes=[
            pltpu.VMEM((N_CHUNKS, rows, LANES), jnp.float32),
            pltpu.VMEM((N_CHUNKS, rows, 2 * LANES), jnp.bfloat16),
            pltpu.VMEM((rows, 2 * CHUNK_STATE), jnp.float32),
            pltpu.VMEM((N_CHUNKS, 1, CHUNK_STATE), jnp.float32),
            pltpu.VMEM((N_CHUNKS, 1, CHUNK_STATE), jnp.float32),
        ],
        compiler_params=pltpu.CompilerParams(
            dimension_semantics=("arbitrary", "arbitrary"), vmem_limit_bytes=VMEM_LIMIT_BYTES),
        name="s5_mixer",
    )(x, g.reshape(1, D_MODEL), d_skip.reshape(N_CHUNKS, 1, LANES), bmat[0], bmat[1], cmat_bf16, apr, api,
      h0r, h0i)
    return z, sr, si


def _state_to_chunks(h):
    return h.reshape(h.shape[0], N_CHUNKS, 1, CHUNK_STATE)


def _chunks_to_state(s):
    return s.reshape(s.shape[0], N_GROUPS, STATE_DIM)


def _glu_kernel(z_ref, res_ref, w_ref, o_ref):
    zz = jnp.dot(z_ref[...].astype(jnp.bfloat16), w_ref[...], preferred_element_type=jnp.float32)
    a = zz[:, :D_MODEL]
    b = zz[:, D_MODEL:]
    o_ref[...] = res_ref[...] + a * (1.0 / (1.0 + jnp.exp(-b)))


def _glu_residual(z, res, w_bf16):
    T = z.shape[0]
    rows = math.gcd(T, 512)
    return pl.pallas_call(
        _glu_kernel,
        grid=(T // rows,),
        in_specs=[
            pl.BlockSpec((rows, D_MODEL), lambda i: (i, 0)),
            pl.BlockSpec((rows, D_MODEL), lambda i: (i, 0)),
            pl.BlockSpec((D_MODEL, 2 * D_MODEL), lambda i: (0, 0)),
        ],
        out_specs=pl.BlockSpec((rows, D_MODEL), lambda i: (i, 0)),
        out_shape=jax.ShapeDtypeStruct((T, D_MODEL), jnp.float32),
        compiler_params=pltpu.CompilerParams(
            dimension_semantics=("arbitrary",), vmem_limit_bytes=VMEM_LIMIT_BYTES),
        name="glu_residual",
    )(z, res, w_bf16)


ROUTE_ROWS = 256
CAND = PEER_TOPK * PEER_TOPK


def _topk_rows(s, payload):
    n_rows = s.shape[0]
    row = lax.broadcasted_iota(jnp.int32, s.shape, 0)
    vals, picks = [], []
    for _ in range(PEER_TOPK):
        m = jnp.max(s, axis=0, keepdims=True)
        pos = jnp.min(jnp.where(s == m, row, n_rows), axis=0, keepdims=True)
        sel = row == pos
        vals.append(m)
        if payload is None:
            picks.append(pos)
        else:
            picks.append(jnp.max(jnp.where(sel, payload, -1), axis=0, keepdims=True))
        s = jnp.where(sel, -jnp.inf, s)
    return jnp.concatenate(vals, axis=0), jnp.concatenate(picks, axis=0)


def _pair_rows(a0, a1, combine):
    half = PEER_TOPK // 2
    parts = [combine(a0[0:1, :], a1)]
    parts += [combine(a0[i:i + 1, :], a1[0:half, :]) for i in range(1, half)]
    parts.append(combine(a0[half:PEER_TOPK, :], a1[0:1, :]))
    return jnp.concatenate(parts, axis=0)


def _peer_route_kernel(h_ref, g_ref, wq_ref, sk_ref, xn_ref, eid_ref, gate_ref, xb_ref):
    hd = pl.program_id(1)

    @pl.when(hd == 0)
    def _():
        xn = _rmsnorm_rows(h_ref[...], g_ref[...])
        xn_ref[...] = xn
        xb_ref[...] = xn.astype(jnp.bfloat16)

    q = jnp.dot(xb_ref[...], wq_ref[...], preferred_element_type=jnp.float32)
    sv, si = [], []
    for c in range(2):
        qc = q[:, c * PEER_DHALF:(c + 1) * PEER_DHALF].astype(jnp.bfloat16)
        st = lax.dot_general(sk_ref[0, c], qc, (((1,), (1,)), ((), ())), preferred_element_type=jnp.float32)
        v, i = _topk_rows(st, None)
        sv.append(v)
        si.append(i)
    cand = _pair_rows(sv[0], sv[1], lambda a, b: a + b)
    cid = _pair_rows(si[0], si[1], lambda a, b: a * N_KEYS + b)
    fv, eid = _topk_rows(cand, cid)
    e = jnp.exp(fv - fv[0:1, :])
    gate_ref[...] = e / jnp.sum(e, axis=0, keepdims=True)
    eid_ref[...] = eid


def _peer_route(h, g, wq_bf16, sk_bf16):
    T = h.shape[0]
    rows = ROUTE_ROWS
    n_sel = PEER_HEADS * PEER_TOPK
    return pl.pallas_call(
        _peer_route_kernel,
        grid=(T // rows, PEER_HEADS),
        in_specs=[
            pl.BlockSpec((rows, D_MODEL), lambda tb, hd: (tb, 0)),
            pl.BlockSpec((1, D_MODEL), lambda tb, hd: (0, 0)),
            pl.BlockSpec((D_MODEL, 2 * PEER_DHALF), lambda tb, hd: (0, hd)),
            pl.BlockSpec((1, 2, N_KEYS, PEER_DHALF), lambda tb, hd: (hd, 0, 0, 0)),
        ],
        out_specs=[
            pl.BlockSpec((rows, D_MODEL), lambda tb, hd: (tb, 0)),
            pl.BlockSpec((PEER_TOPK, rows), lambda tb, hd: (hd, tb)),
            pl.BlockSpec((PEER_TOPK, rows), lambda tb, hd: (hd, tb)),
        ],
        out_shape=[
            jax.ShapeDtypeStruct((T, D_MODEL), jnp.float32),
            jax.ShapeDtypeStruct((n_sel, T), jnp.int32),
            jax.ShapeDtypeStruct((n_sel, T), jnp.float32),
        ],
        scratch_shapes=[pltpu.VMEM((rows, D_MODEL), jnp.bfloat16)],
        compiler_params=pltpu.CompilerParams(
            dimension_semantics=("arbitrary", "arbitrary"), vmem_limit_bytes=VMEM_LIMIT_BYTES),
        name="peer_route",
    )(h, g.reshape(1, D_MODEL), wq_bf16, sk_bf16)


N_SEL = PEER_HEADS * PEER_TOPK
SC_LANES = 16
GATHER_ROWS = PEER_TOPK
GATHERS_PER_TOKEN = N_SEL // GATHER_ROWS
GATHER_BUFS = 4
SC_TOKENS = 8


def _sc_gather_stream(wid, n_batches, tab_hbm, stage_srcs, stage_bufs, out_hbm, o_v, bufs, gsem, ssem, osem,
                      compute):
    idx_v = stage_bufs[0]

    def stage_copies(bi, slot):
        base = (wid * n_batches + bi) * SC_TOKENS
        return [pltpu.make_async_copy(src.at[pl.ds(base, SC_TOKENS)], buf.at[slot], ssem.at[slot])
                for src, buf in zip(stage_srcs, stage_bufs)]

    def out_copy(bi, slot):
        base = (wid * n_batches + bi) * SC_TOKENS
        return pltpu.make_async_copy(o_v.at[slot], out_hbm.at[pl.ds(base, SC_TOKENS)], osem.at[slot])

    def start(slot, t, kk, b):
        idx = idx_v[slot, t, pl.ds(kk * GATHER_ROWS, GATHER_ROWS)]
        pltpu.async_copy(tab_hbm.at[idx], bufs.at[b], gsem.at[b])

    def wait(b):
        pltpu.make_async_copy(tab_hbm.at[pl.ds(0, GATHER_ROWS)], bufs.at[b], gsem.at[b]).wait()

    for c in stage_copies(0, 0):
        c.start()
    for c in stage_copies(0, 0):
        c.wait()
    for q in range(GATHER_BUFS - 1):
        start(0, q // GATHERS_PER_TOKEN, q % GATHERS_PER_TOKEN, q % GATHER_BUFS)

    def batch(bi, carry):
        slot = bi % 2
        has_next = bi + 1 < n_batches

        @pl.when(has_next)
        def _():
            for c in stage_copies(bi + 1, 1 - slot):
                c.start()

        @pl.when(bi >= 2)
        def _():
            out_copy(bi, slot).wait()

        def tok(t, carry):
            @pl.when(jnp.logical_and(t == SC_TOKENS - 1, has_next))
            def _():
                for c in stage_copies(bi + 1, 1 - slot):
                    c.wait()

            for kk in range(GATHERS_PER_TOKEN):
                nq = kk + GATHER_BUFS - 1
                nk, nb = nq % GATHERS_PER_TOKEN, nq % GATHER_BUFS
                if nq < GATHERS_PER_TOKEN:
                    start(slot, t, nk, nb)
                else:
                    @pl.when(t + 1 < SC_TOKENS)
                    def _():
                        start(slot, t + 1, nk, nb)

                    @pl.when(jnp.logical_and(t + 1 == SC_TOKENS, has_next))
                    def _():
                        start(1 - slot, 0, nk, nb)

                wait(kk % GATHER_BUFS)
                compute(slot, t, kk, kk % GATHER_BUFS)
            return carry

        lax.fori_loop(0, SC_TOKENS, tok, 0)
        out_copy(bi, slot).start()
        return carry

    lax.fori_loop(0, n_batches, batch, 0)
    if n_batches >= 2:
        out_copy(n_batches - 2, (n_batches - 2) % 2).wait()
    out_copy(n_batches - 1, (n_batches - 1) % 2).wait()


def _sc_mesh_and_batches(n_tokens):
    info = plsc.get_sparse_core_info()
    assert info.num_lanes == SC_LANES
    n_workers = info.num_cores * info.num_subcores
    assert n_tokens % (n_workers * SC_TOKENS) == 0
    mesh = plsc.VectorSubcoreMesh(core_axis_name="c", subcore_axis_name="s")
    return info, mesh, n_tokens // (n_workers * SC_TOKENS)


def _peer_hidden_sc(eid, xn, u_tab):
    T = eid.shape[0]
    info, mesh, n_batches = _sc_mesh_and_batches(T)

    @functools.partial(
        pl.kernel, mesh=mesh,
        out_type=jax.ShapeDtypeStruct((T, N_SEL), jnp.float32),
        scratch_types=[
            pltpu.VMEM((2, SC_TOKENS, N_SEL), jnp.int32),
            pltpu.VMEM((2, SC_TOKENS, D_MODEL), jnp.float32),
            pltpu.VMEM((2, SC_TOKENS, N_SEL), jnp.float32),
            pltpu.VMEM((GATHER_BUFS, GATHER_ROWS, D_MODEL), jnp.float32),
            pltpu.VMEM((GATHER_ROWS, SC_LANES), jnp.float32),
            pltpu.SemaphoreType.DMA((GATHER_BUFS,)),
            pltpu.SemaphoreType.DMA((2,)),
            pltpu.SemaphoreType.DMA((2,)),
        ],
        compiler_params=pltpu.CompilerParams(needs_layout_passes=False),
        name="peer_hidden_sc",
    )
    def k(eid_hbm, xn_hbm, u_hbm, out_hbm, idx_v, x_v, o_v, bufs, acc_v, gsem, ssem, osem):
        wid = lax.axis_index("s") * info.num_cores + lax.axis_index("c")
        lane = lax.iota(jnp.int32, SC_LANES)
        zero = jnp.zeros((SC_LANES,), jnp.float32)

        def compute(slot, t, kk, b):
            @plsc.parallel_loop(0, D_MODEL // SC_LANES, carry=(zero,) * GATHER_ROWS)
            def accs(c, accs):
                xc = x_v[slot, t, pl.ds(c * SC_LANES, SC_LANES)]
                return tuple(accs[r] + bufs[b, r, pl.ds(c * SC_LANES, SC_LANES)] * xc for r in range(GATHER_ROWS))

            for r in range(GATHER_ROWS):
                acc_v[r, :] = accs[r]
            tot = zero
            for c in range(SC_LANES):
                tot = tot + plsc.load_gather(acc_v, [lane, jnp.full((SC_LANES,), c, jnp.int32)])
            o_v[slot, t, pl.ds(kk * GATHER_ROWS, GATHER_ROWS)] = tot

        _sc_gather_stream(wid, n_batches, u_hbm, [eid_hbm, xn_hbm], [idx_v, x_v], out_hbm, o_v, bufs,
                          gsem, ssem, osem, compute)

    return k(eid, xn, u_tab)


def _peer_combine_sc(eid, a, v_tab):
    T = eid.shape[0]
    info, mesh, n_batches = _sc_mesh_and_batches(T)

    @functools.partial(
        pl.kernel, mesh=mesh,
        out_type=jax.ShapeDtypeStruct((T, D_MODEL), jnp.float32),
        scratch_types=[
            pltpu.VMEM((2, SC_TOKENS, N_SEL), jnp.int32),
            pltpu.VMEM((2, SC_TOKENS, N_SEL), jnp.float32),
            pltpu.VMEM((2, SC_TOKENS, D_MODEL), jnp.float32),
            pltpu.VMEM((GATHER_BUFS, GATHER_ROWS, D_MODEL), jnp.float32),
            pltpu.SemaphoreType.DMA((GATHER_BUFS,)),
            pltpu.SemaphoreType.DMA((2,)),
            pltpu.SemaphoreType.DMA((2,)),
        ],
        compiler_params=pltpu.CompilerParams(needs_layout_passes=False),
        name="peer_combine_sc",
    )
    def k(eid_hbm, a_hbm, v_hbm, out_hbm, idx_v, a_v, o_v, bufs, gsem, ssem, osem):
        wid = lax.axis_index("s") * info.num_cores + lax.axis_index("c")

        def compute(slot, t, kk, b):
            svec = jnp.full((SC_LANES,), slot, jnp.int32)
            tvec = jnp.full((SC_LANES,), t, jnp.int32)
            ws = [plsc.load_gather(a_v, [svec, tvec, jnp.full((SC_LANES,), kk * GATHER_ROWS + r, jnp.int32)])
                  for r in range(GATHER_ROWS)]

            @plsc.parallel_loop(0, D_MODEL // SC_LANES, unroll=2)
            def _(c):
                sl = pl.ds(c * SC_LANES, SC_LANES)
                terms = [ws[r] * bufs[b, r, sl] for r in range(GATHER_ROWS)]
                if kk != 0:
                    terms.append(o_v[slot, t, sl])
                while len(terms) > 1:
                    pairs = [terms[i] + terms[i + 1] for i in range(0, len(terms) - 1, 2)]
                    terms = pairs + ([terms[-1]] if len(terms) % 2 else [])
                o_v[slot, t, sl] = terms[0]

        _sc_gather_stream(wid, n_batches, v_hbm, [eid_hbm, a_hbm], [idx_v, a_v], out_hbm, o_v, bufs,
                          gsem, ssem, osem, compute)

    return k(eid, a, v_tab)


def _peer_act_kernel(hp_ref, gate_ref, a_ref):
    a_ref[...] = _gelu(hp_ref[...]) * gate_ref[...]


ELEMENTWISE_ROWS = 1088
FIRST_PEER_ROWS = 1024


def _row_block(n_rows, cap):
    return max(r for r in range(SUBLANES, cap + 1, SUBLANES) if n_rows % r == 0)


def _peer_act(hpre, gate):
    T = hpre.shape[0]
    rows = _row_block(T, ELEMENTWISE_ROWS)
    spec = pl.BlockSpec((rows, N_SEL), lambda i: (i, 0))
    return pl.pallas_call(
        _peer_act_kernel, grid=(T // rows,), in_specs=[spec, spec], out_specs=spec,
        out_shape=jax.ShapeDtypeStruct((T, N_SEL), jnp.float32),
        compiler_params=pltpu.CompilerParams(dimension_semantics=("arbitrary",)),
        name="peer_act",
    )(hpre, gate)


def _residual_kernel(h_ref, c_ref, g_ref, o_ref, *, final_norm):
    y = h_ref[...] + c_ref[...]
    o_ref[...] = _rmsnorm_rows(y, g_ref[...]) if final_norm else y


def _residual(h, c, gfin, *, final_norm):
    T = h.shape[0]
    rows = _row_block(T, ELEMENTWISE_ROWS)
    spec = pl.BlockSpec((rows, D_MODEL), lambda i: (i, 0))
    return pl.pallas_call(
        functools.partial(_residual_kernel, final_norm=final_norm), grid=(T // rows,),
        in_specs=[spec, spec, pl.BlockSpec((1, D_MODEL), lambda i: (0, 0))], out_specs=spec,
        out_shape=jax.ShapeDtypeStruct((T, D_MODEL), jnp.float32),
        compiler_params=pltpu.CompilerParams(dimension_semantics=("arbitrary",)),
        name="peer_residual",
    )(h, c, gfin.reshape(1, D_MODEL))


def _peer_experts(eid, xn, gate, h, gfin, u_tab, v_tab, *, final_norm):
    hpre = _peer_hidden_sc(eid, xn, u_tab)
    a = _peer_act(hpre, gate)
    c = _peer_combine_sc(eid, a, v_tab)
    return _residual(h, c, gfin, final_norm=final_norm)


KV_WIDTH = N_KV_HEADS * HEAD_DIM
BF16 = jnp.bfloat16


def _qkv(h, gkv, gq, wkv_ref, wq_ref):
    kv = jnp.dot(_rmsnorm_rows(h, gkv).astype(BF16), wkv_ref[...], preferred_element_type=jnp.float32)
    q = jnp.dot(_rmsnorm_rows(h, gq).astype(BF16), wq_ref[...], preferred_element_type=jnp.float32)
    return kv, q


def _sink_softmax_pv(parts, sink):
    m = sink
    for s, _ in parts:
        m = jnp.maximum(m, jnp.max(s, axis=-1, keepdims=True))
    den = jnp.exp(sink - m)
    acc = None
    for s, v in parts:
        e = jnp.exp(s - m)
        den = den + jnp.sum(e, axis=-1, keepdims=True)
        pv = jnp.dot(e.astype(BF16), v.astype(BF16), preferred_element_type=jnp.float32)
        acc = pv if acc is None else acc + pv
    return acc / den


def _nt_dot(a, b):
    return lax.dot_general(a.astype(BF16), b.astype(BF16), (((1,), (1,)), ((), ())),
                           preferred_element_type=jnp.float32)


def _attn_prompt_kernel(h_ref, gkv_ref, gq_ref, wkv_ref, wq_ref, wo_ref, sink_ref,
                        o_ref, kvw_ref, prev_ref):
    blk = pl.program_id(1)
    h = h_ref[...]
    kv, q = _qkv(h, gkv_ref[...], gq_ref[...], wkv_ref, wq_ref)
    kvw_ref[0] = kv

    @pl.when(blk == 0)
    def _():
        prev_ref[...] = jnp.zeros_like(prev_ref)

    prev = prev_ref[...]
    qi = lax.broadcasted_iota(jnp.int32, (WINDOW, WINDOW), 0)
    kj = lax.broadcasted_iota(jnp.int32, (WINDOW, WINDOW), 1)
    prev_ok = jnp.logical_and(kj > qi, blk > 0)
    cur_ok = kj <= qi
    heads = []
    for kvh in range(N_KV_HEADS):
        ks = slice(kvh * HEAD_DIM, (kvh + 1) * HEAD_DIM)
        vs = slice(KV_WIDTH + kvh * HEAD_DIM, KV_WIDTH + (kvh + 1) * HEAD_DIM)
        for g in range(Q_PER_KV):
            hq = kvh * Q_PER_KV + g
            qh = q[:, hq * HEAD_DIM:(hq + 1) * HEAD_DIM]
            sp = jnp.where(prev_ok, _nt_dot(qh, prev[:, ks]) * ATTN_SCALE, -jnp.inf)
            sc = jnp.where(cur_ok, _nt_dot(qh, kv[:, ks]) * ATTN_SCALE, -jnp.inf)
            heads.append(_sink_softmax_pv([(sp, prev[:, vs]), (sc, kv[:, vs])], sink_ref[hq]))
    o = jnp.concatenate(heads, axis=1)
    o_ref[...] = h + jnp.dot(o.astype(BF16), wo_ref[...], preferred_element_type=jnp.float32)
    prev_ref[...] = kv


def _attn_prompt(h, gkv, gq, wkv, wq, wo, sinks, *, n_seq, seq_len):
    nb = seq_len // WINDOW
    row_spec = pl.BlockSpec((WINDOW, D_MODEL), lambda n, b: (n * nb + b, 0))
    full = lambda shape: pl.BlockSpec(shape, lambda n, b: (0,) * len(shape))
    return pl.pallas_call(
        _attn_prompt_kernel,
        grid=(n_seq, nb),
        in_specs=[
            row_spec, full((1, D_MODEL)), full((1, D_MODEL)),
            full((D_MODEL, 2 * KV_WIDTH)), full((D_MODEL, D_MODEL)), full((D_MODEL, D_MODEL)),
            pl.BlockSpec(memory_space=pltpu.SMEM),
        ],
        out_specs=[row_spec, pl.BlockSpec((1, WINDOW, 2 * KV_WIDTH), lambda n, b: (n, 0, 0))],
        out_shape=[
            jax.ShapeDtypeStruct(h.shape, jnp.float32),
            jax.ShapeDtypeStruct((n_seq, WINDOW, 2 * KV_WIDTH), jnp.float32),
        ],
        scratch_shapes=[pltpu.VMEM((WINDOW, 2 * KV_WIDTH), jnp.float32)],
        compiler_params=pltpu.CompilerParams(
            dimension_semantics=("arbitrary", "arbitrary"), vmem_limit_bytes=VMEM_LIMIT_BYTES),
        name="attn_prompt",
    )(h, gkv.reshape(1, D_MODEL), gq.reshape(1, D_MODEL), wkv, wq, wo, sinks)


ATTN_SEQS = 16


def _attn_sample_kernel(h_ref, ck_ref, cv_ref, gkv_ref, gq_ref, wkv_ref, wq_ref, wo_ref, sink_ref,
                        o_ref, kw_ref, vw_ref, kv_ref, q_ref, att_ref, *, n_new):
    h = h_ref[...]
    kv, q = _qkv(h, gkv_ref[...], gq_ref[...], wkv_ref, wq_ref)
    kv_ref[...] = kv
    q_ref[...] = q
    n_seq = h.shape[0] // n_new
    rows = Q_PER_KV * n_new
    qpos_c = lax.broadcasted_iota(jnp.int32, (rows, WINDOW), 0) % n_new
    cache_ok = lax.broadcasted_iota(jnp.int32, (rows, WINDOW), 1) > qpos_c
    qpos_n = lax.broadcasted_iota(jnp.int32, (rows, n_new), 0) % n_new
    new_ok = lax.broadcasted_iota(jnp.int32, (rows, n_new), 1) <= qpos_n

    def per_seq(n, carry):
        r0 = pl.multiple_of(n * n_new, n_new)
        kvn = kv_ref[pl.ds(r0, n_new), :]
        qn = q_ref[pl.ds(r0, n_new), :]
        ck = ck_ref[n]
        cv = cv_ref[n]
        kw_ref[n, 0:WINDOW - n_new, :] = ck[n_new:, :]
        kw_ref[n, WINDOW - n_new:WINDOW, :] = kvn[:, 0:KV_WIDTH]
        vw_ref[n, 0:WINDOW - n_new, :] = cv[n_new:, :]
        vw_ref[n, WINDOW - n_new:WINDOW, :] = kvn[:, KV_WIDTH:2 * KV_WIDTH]
        outs = []
        for kvh in range(N_KV_HEADS):
            ks = slice(kvh * HEAD_DIM, (kvh + 1) * HEAD_DIM)
            vs = slice(KV_WIDTH + kvh * HEAD_DIM, KV_WIDTH + (kvh + 1) * HEAD_DIM)
            qs = jnp.concatenate(
                [qn[:, (kvh * Q_PER_KV + g) * HEAD_DIM:(kvh * Q_PER_KV + g + 1) * HEAD_DIM] for g in range(Q_PER_KV)],
                axis=0)
            s_c = jnp.where(cache_ok, _nt_dot(qs, ck[:, ks]) * ATTN_SCALE, -jnp.inf)
            s_n = jnp.where(new_ok, _nt_dot(qs, kvn[:, ks]) * ATTN_SCALE, -jnp.inf)
            sink = sink_ref[kvh][:, 0:1]
            o = _sink_softmax_pv([(s_c, cv[:, ks]), (s_n, kvn[:, vs])], sink)
            outs += [o[g * n_new:(g + 1) * n_new, :] for g in range(Q_PER_KV)]
        att_ref[pl.ds(r0, n_new), :] = jnp.concatenate(outs, axis=1)
        return carry

    lax.fori_loop(0, n_seq, per_seq, 0)
    o_ref[...] = h + jnp.dot(att_ref[...].astype(BF16), wo_ref[...], preferred_element_type=jnp.float32)


def _attn_sample(h, cache_k, cache_v, gkv, gq, wkv, wq, wo, sinks, *, n_new):
    n_seq = cache_k.shape[0]
    sb = ATTN_SEQS
    rows = sb * n_new
    row_spec = pl.BlockSpec((rows, D_MODEL), lambda i: (i, 0))
    win_spec = pl.BlockSpec((sb, WINDOW, KV_WIDTH), lambda i: (i, 0, 0))
    full = lambda shape: pl.BlockSpec(shape, lambda i: (0,) * len(shape))
    sink_rows = jnp.repeat(sinks.reshape(N_KV_HEADS, Q_PER_KV), n_new, axis=1)[:, :, None]
    sink_rows = jnp.broadcast_to(sink_rows, (N_KV_HEADS, Q_PER_KV * n_new, LANES))
    return pl.pallas_call(
        functools.partial(_attn_sample_kernel, n_new=n_new),
        grid=(n_seq // sb,),
        in_specs=[
            row_spec, win_spec, win_spec, full((1, D_MODEL)), full((1, D_MODEL)),
            full((D_MODEL, 2 * KV_WIDTH)), full((D_MODEL, D_MODEL)), full((D_MODEL, D_MODEL)),
            full((N_KV_HEADS, Q_PER_KV * n_new, LANES)),
        ],
        out_specs=[row_spec, win_spec, win_spec],
        out_shape=[
            jax.ShapeDtypeStruct(h.shape, jnp.float32),
            jax.ShapeDtypeStruct((n_seq, WINDOW, KV_WIDTH), jnp.float32),
            jax.ShapeDtypeStruct((n_seq, WINDOW, KV_WIDTH), jnp.float32),
        ],
        scratch_shapes=[
            pltpu.VMEM((rows, 2 * KV_WIDTH), jnp.float32),
            pltpu.VMEM((rows, D_MODEL), jnp.float32),
            pltpu.VMEM((rows, D_MODEL), jnp.float32),
        ],
        compiler_params=pltpu.CompilerParams(
            dimension_semantics=("arbitrary",), vmem_limit_bytes=VMEM_LIMIT_BYTES),
        name="attn_sample",
    )(h, cache_k, cache_v, gkv.reshape(1, D_MODEL), gq.reshape(1, D_MODEL), wkv, wq, wo, sink_rows)


def kernel(x_prompt, x_sample, state_ssm_re, state_ssm_im, cache_k_win, cache_v_win, norm_mix, norm_ffn, norm_kv, norm_final, ssm_lam_re, ssm_lam_im, ssm_log_dt, ssm_b_re, ssm_b_im, ssm_c_re, ssm_c_im, ssm_d, ssm_w_glu, w_kv, w_q, attn_sinks, w_o, peer_w_q, peer_sub_keys, peer_u, peer_v):
    bmat, cmat, apr, api = _s5_discretize(ssm_lam_re[0], ssm_lam_im[0], ssm_log_dt[0], ssm_b_re[0], ssm_b_im[0], ssm_c_re[0], ssm_c_im[0])
    cmat = cmat.astype(jnp.bfloat16)
    wglu = ssm_w_glu[0].astype(jnp.bfloat16)
    wkv = w_kv.astype(BF16)
    wq = w_q[0].astype(BF16)
    wo = w_o[0].astype(BF16)
    peer_wq = [peer_w_q[layer].astype(BF16) for layer in range(2)]
    peer_sk = [peer_sub_keys[layer].astype(BF16) for layer in range(2)]

    def peer(h, layer, final_norm, head_rows=0):
        if head_rows:
            return jnp.concatenate([peer(h[:head_rows], layer, final_norm),
                                    peer(h[head_rows:], layer, final_norm)], axis=0)
        xn, eid_t, gate_t = _peer_route(h, norm_ffn[layer], peer_wq[layer], peer_sk[layer])
        return _peer_experts(eid_t.T, xn, gate_t.T, h, norm_final, peer_u[layer], peer_v[layer],
                             final_norm=final_norm)

    def trunk(x_p, x_s, s_re, s_im, c_k, c_v, first):
        n_p, n_s = x_p.shape[0], x_s.shape[0]
        xp = x_p.reshape(-1, D_MODEL)
        xs = x_s.reshape(-1, D_MODEL)
        n_prompt_rows = xp.shape[0]
        z0 = jnp.zeros((n_p, N_CHUNKS, 1, CHUNK_STATE), jnp.float32)
        zp, srp, sip = _s5_mixer(xp, z0, z0, norm_mix[0], ssm_d[0], bmat, cmat, apr, api,
                                 chain=True, seq_len=x_p.shape[1])
        zs, srs, sis = _s5_mixer(xs, _state_to_chunks(s_re), _state_to_chunks(s_im), norm_mix[0], ssm_d[0],
                                 bmat, cmat, apr, api, chain=False, seq_len=x_s.shape[1])
        h1 = jnp.concatenate([_glu_residual(zp, xp, wglu), _glu_residual(zs, xs, wglu)], axis=0)
        h2 = peer(h1, 0, False, head_rows=FIRST_PEER_ROWS if first else 0)
        h3p, kvw_p = _attn_prompt(h2[:n_prompt_rows], norm_kv, norm_mix[1], wkv, wq, wo, attn_sinks[0],
                                  n_seq=n_p, seq_len=x_p.shape[1])
        h3s, kw_s, vw_s = _attn_sample(h2[n_prompt_rows:], c_k.reshape(n_s, WINDOW, KV_WIDTH),
                                       c_v.reshape(n_s, WINDOW, KV_WIDTH), norm_kv, norm_mix[1],
                                       wkv, wq, wo, attn_sinks[0], n_new=x_s.shape[1])
        y = peer(jnp.concatenate([h3p, h3s], axis=0), 1, True)
        win = lambda a: a.reshape(a.shape[0], WINDOW, N_KV_HEADS, HEAD_DIM)
        return (y[:n_prompt_rows].reshape(x_p.shape), y[n_prompt_rows:].reshape(x_s.shape),
                _chunks_to_state(srp), _chunks_to_state(sip),
                win(kvw_p[:, :, :KV_WIDTH]), win(kvw_p[:, :, KV_WIDTH:]),
                _chunks_to_state(srs), _chunks_to_state(sis), win(kw_s), win(vw_s))

    n_groups = 4
    gp = x_prompt.shape[0] // n_groups
    gs = x_sample.shape[0] // n_groups
    outs = [trunk(x_prompt[i * gp:(i + 1) * gp], x_sample[i * gs:(i + 1) * gs],
                  state_ssm_re[0, i * gs:(i + 1) * gs], state_ssm_im[0, i * gs:(i + 1) * gs],
                  cache_k_win[i * gs:(i + 1) * gs], cache_v_win[i * gs:(i + 1) * gs], first=(i == 0))
            for i in range(n_groups)]
    cat = [jnp.concatenate(parts, axis=0) for parts in zip(*outs)]
    return (cat[0], cat[1], cat[2][None], cat[3][None], cat[4], cat[5], cat[6][None], cat[7][None],
            cat[8], cat[9])
```

```python
import functools
import math

import jax
import jax.numpy as jnp
from jax import lax
from jax.experimental import pallas as pl
from jax.experimental.pallas import tpu as pltpu
from jax.experimental.pallas import tpu_sc as plsc

D_MODEL = 1024
GROUP_SIZE = 16
N_GROUPS = D_MODEL // GROUP_SIZE
STATE_DIM = 64
HEAD_DIM = 64
N_Q_HEADS = D_MODEL // HEAD_DIM
N_KV_HEADS = N_Q_HEADS // 8
Q_PER_KV = N_Q_HEADS // N_KV_HEADS
WINDOW = 128
PAST_LEN = 16384
ATTN_SCALE = 1.0 / math.sqrt(HEAD_DIM)
PEER_HEADS = 8
N_KEYS = 128
PEER_TOPK = 16
PEER_DHALF = 128
EPS = 1e-5

LANES = 128
SUBLANES = 8
VMEM_LIMIT_BYTES = 56 * 1024 * 1024

GROUPS_PER_CHUNK = LANES // GROUP_SIZE
N_CHUNKS = N_GROUPS // GROUPS_PER_CHUNK
CHUNK_STATE = GROUPS_PER_CHUNK * STATE_DIM
S5_ROWS = 256


def _rmsnorm_rows(x, g):
    r = lax.rsqrt(jnp.mean(x * x, axis=-1, keepdims=True) + EPS)
    return x * r * g


def _gelu(x):
    return 0.5 * x * (1.0 + lax.erf(x * (1.0 / math.sqrt(2.0))))


def _s5_discretize(lam_re, lam_im, log_dt, b_re, b_im, c_re, c_im):
    f32 = jnp.float32
    lr = lam_re.astype(f32)
    li = lam_im.astype(f32)
    dt = jnp.exp(log_dt.astype(f32))[:, None]
    mag = jnp.exp(lr * dt)
    ab_re = mag * jnp.cos(li * dt)
    ab_im = mag * jnp.sin(li * dt)
    den = lr * lr + li * li
    f_re = ((ab_re - 1.0) * lr + ab_im * li) / den
    f_im = (ab_im * lr - (ab_re - 1.0) * li) / den
    br = b_re.astype(f32)
    bi = b_im.astype(f32)
    bb_re = f_re[..., None] * br - f_im[..., None] * bi
    bb_im = f_re[..., None] * bi + f_im[..., None] * br
    eye = jnp.eye(GROUPS_PER_CHUNK, dtype=f32)

    def chunk_rows(v):
        return v.reshape(N_CHUNKS, 1, CHUNK_STATE)

    def in_blocks(bb):
        t = bb.reshape(N_CHUNKS, GROUPS_PER_CHUNK, STATE_DIM, GROUP_SIZE).transpose(0, 1, 3, 2)
        return jnp.einsum('mgjp,gh->mgjhp', t, eye).reshape(N_CHUNKS, LANES, CHUNK_STATE)

    def out_blocks(c):
        t = c.astype(f32).reshape(N_CHUNKS, GROUPS_PER_CHUNK, GROUP_SIZE, STATE_DIM).transpose(0, 1, 3, 2)
        return jnp.einsum('mgpj,gh->mgphj', t, eye).reshape(N_CHUNKS, CHUNK_STATE, LANES)

    bfull = jnp.concatenate([in_blocks(bb_re), in_blocks(bb_im)], axis=2)
    b_hi = bfull.astype(jnp.bfloat16)
    b_lo = (bfull - b_hi.astype(f32)).astype(jnp.bfloat16)
    bmat = (jnp.concatenate([b_hi, b_hi], axis=1), b_lo)
    cmat = jnp.concatenate([out_blocks(c_re), -out_blocks(c_im)], axis=1)
    pr, pi = [ab_re], [ab_im]
    for _ in range(SUBLANES - 1):
        pr, pi = pr + [pr[-1] * ab_re - pi[-1] * ab_im], pi + [pr[-1] * ab_im + pi[-1] * ab_re]
    apr = jnp.concatenate([chunk_rows(v) for v in pr], axis=1)
    api = jnp.concatenate([chunk_rows(v) for v in pi], axis=1)
    return bmat, cmat, apr, api


def _split_bf16(x):
    hi = x.astype(jnp.bfloat16)
    return hi, (x - hi.astype(jnp.float32)).astype(jnp.bfloat16)


def _s5_kernel(x_ref, g_ref, d_ref, bhh_ref, blo_ref, c_ref, apr_ref, api_ref, h0r_ref, h0i_ref,
               z_ref, sr_ref, si_ref, u_ref, us_ref, st_ref, cr_ref, ci_ref, *, chain, blocks_per_seq):
    rb = pl.program_id(0)
    m = pl.program_id(1)
    rows = x_ref.shape[0]
    n_tiles = rows // SUBLANES

    @pl.when(m == 0)
    def _():
        u = _rmsnorm_rows(x_ref[...], g_ref[...])
        for mm in range(N_CHUNKS):
            uc = u[:, mm * LANES:(mm + 1) * LANES]
            u_ref[mm] = uc
            hi, lo = _split_bf16(uc)
            us_ref[mm, :, 0:LANES] = hi
            us_ref[mm, :, LANES:2 * LANES] = lo

    u = u_ref[m]
    us = us_ref[m]
    st_ref[...] = (jnp.dot(us, bhh_ref[0], preferred_element_type=jnp.float32)
                   + jnp.dot(us[:, 0:LANES], blo_ref[0], preferred_element_type=jnp.float32))

    apr = apr_ref[0]
    api = api_ref[0]
    row = lax.broadcasted_iota(jnp.int32, (SUBLANES, CHUNK_STATE), 0)

    if chain:
        @pl.when(rb % blocks_per_seq == 0)
        def _():
            cr_ref[m] = h0r_ref[0, 0]
            ci_ref[m] = h0i_ref[0, 0]

    def tile_step(k, carry):
        r0 = pl.multiple_of(k * SUBLANES, SUBLANES)
        xr = st_ref[pl.ds(r0, SUBLANES), 0:CHUNK_STATE]
        xi = st_ref[pl.ds(r0, SUBLANES), CHUNK_STATE:2 * CHUNK_STATE]
        for d in (1, 2, 4):
            ar = apr[d - 1:d, :]
            ai = api[d - 1:d, :]
            sr = jnp.where(row >= d, pltpu.roll(xr, d, axis=0), 0.0)
            si = jnp.where(row >= d, pltpu.roll(xi, d, axis=0), 0.0)
            xr, xi = xr + ar * sr - ai * si, xi + ar * si + ai * sr
        if chain:
            cr, ci = carry
        else:
            cr = h0r_ref[k, 0]
            ci = h0i_ref[k, 0]
        hr = xr + apr * cr - api * ci
        hi = xi + apr * ci + api * cr
        st_ref[pl.ds(r0, SUBLANES), 0:CHUNK_STATE] = hr
        st_ref[pl.ds(r0, SUBLANES), CHUNK_STATE:2 * CHUNK_STATE] = hi
        lr_ = hr[SUBLANES - 1:SUBLANES, :]
        li_ = hi[SUBLANES - 1:SUBLANES, :]
        if chain:
            return lr_, li_
        sr_ref[k, m] = lr_
        si_ref[k, m] = li_
        return carry

    if chain:
        cr, ci = lax.fori_loop(0, n_tiles, tile_step, (cr_ref[m], ci_ref[m]))
        cr_ref[m] = cr
        ci_ref[m] = ci
        sr_ref[0, m] = cr
        si_ref[0, m] = ci
    else:
        lax.fori_loop(0, n_tiles, tile_step, 0)

    y = jnp.dot(st_ref[...].astype(jnp.bfloat16), c_ref[0], preferred_element_type=jnp.float32)
    y = y + d_ref[0] * u
    z_ref[...] = _gelu(y)


def _s5_mixer(x, h0r, h0i, g, d_skip, bmat, cmat_bf16, apr, api, *, chain, seq_len):
    T = x.shape[0]
    nseq = h0r.shape[0]
    rows = S5_ROWS
    if chain:
        blocks_per_seq = seq_len // rows
        seq_blk = 1
        seq_map = lambda rb, m: (rb // blocks_per_seq, m, 0, 0)
        out_map = lambda rb, m: (rb // blocks_per_seq, 0, 0, 0)
    else:
        assert seq_len == SUBLANES
        blocks_per_seq = 1
        seq_blk = rows // SUBLANES
        seq_map = lambda rb, m: (rb, m, 0, 0)
        out_map = lambda rb, m: (rb, 0, 0, 0)
    grid = (T // rows, N_CHUNKS)
    kern = functools.partial(_s5_kernel, chain=chain, blocks_per_seq=blocks_per_seq)
    st_spec = pl.BlockSpec((seq_blk, 1, 1, CHUNK_STATE), seq_map)
    out_st_spec = pl.BlockSpec((seq_blk, N_CHUNKS, 1, CHUNK_STATE), out_map)
    z, sr, si = pl.pallas_call(
        kern,
        grid=grid,
        in_specs=[
            pl.BlockSpec((rows, D_MODEL), lambda rb, m: (rb, 0)),
            pl.BlockSpec((1, D_MODEL), lambda rb, m: (0, 0)),
            pl.BlockSpec((1, 1, LANES), lambda rb, m: (m, 0, 0)),
            pl.BlockSpec((1, 2 * LANES, 2 * CHUNK_STATE), lambda rb, m: (m, 0, 0)),
            pl.BlockSpec((1, LANES, 2 * CHUNK_STATE), lambda rb, m: (m, 0, 0)),
            pl.BlockSpec((1, 2 * CHUNK_STATE, LANES), lambda rb, m: (m, 0, 0)),
            pl.BlockSpec((1, SUBLANES, CHUNK_STATE), lambda rb, m: (m, 0, 0)),
            pl.BlockSpec((1, SUBLANES, CHUNK_STATE), lambda rb, m: (m, 0, 0)),
            st_spec, st_spec,
        ],
        out_specs=[
            pl.BlockSpec((rows, LANES), lambda rb, m: (rb, m)),
            out_st_spec, out_st_spec,
        ],
        out_shape=[
            jax.ShapeDtypeStruct((T, D_MODEL), jnp.float32),
            jax.ShapeDtypeStruct((nseq, N_CHUNKS, 1, CHUNK_STATE), jnp.float32),
            jax.ShapeDtypeStruct((nseq, N_CHUNKS, 1, CHUNK_STATE), jnp.float32),
        ],
        scratch_shapes=[
            pltpu.VMEM((N_CHUNKS, rows, LANES), jnp.float32),
            pltpu.VMEM((N_CHUNKS, rows, 2 * LANES), jnp.bfloat16),
            pltpu.VMEM((rows, 2 * CHUNK_STATE), jnp.float32),
            pltpu.VMEM((N_CHUNKS, 1, CHUNK_STATE), jnp.float32),
            pltpu.VMEM((N_CHUNKS, 1, CHUNK_STATE), jnp.float32),
        ],
        compiler_params=pltpu.CompilerParams(
            dimension_semantics=("arbitrary", "arbitrary"), vmem_limit_bytes=VMEM_LIMIT_BYTES),
        name="s5_mixer",
    )(x, g.reshape(1, D_MODEL), d_skip.reshape(N_CHUNKS, 1, LANES), bmat[0], bmat[1], cmat_bf16, apr, api,
      h0r, h0i)
    return z, sr, si


def _state_to_chunks(h):
    return h.reshape(h.shape[0], N_CHUNKS, 1, CHUNK_STATE)


def _chunks_to_state(s):
    return s.reshape(s.shape[0], N_GROUPS, STATE_DIM)


def _glu_kernel(z_ref, res_ref, w_ref, o_ref):
    zz = jnp.dot(z_ref[...].astype(jnp.bfloat16), w_ref[...], preferred_element_type=jnp.float32)
    a = zz[:, :D_MODEL]
    b = zz[:, D_MODEL:]
    o_ref[...] = res_ref[...] + a * (1.0 / (1.0 + jnp.exp(-b)))


def _glu_residual(z, res, w_bf16):
    T = z.shape[0]
    rows = math.gcd(T, 512)
    return pl.pallas_call(
        _glu_kernel,
        grid=(T // rows,),
        in_specs=[
            pl.BlockSpec((rows, D_MODEL), lambda i: (i, 0)),
            pl.BlockSpec((rows, D_MODEL), lambda i: (i, 0)),
            pl.BlockSpec((D_MODEL, 2 * D_MODEL), lambda i: (0, 0)),
        ],
        out_specs=pl.BlockSpec((rows, D_MODEL), lambda i: (i, 0)),
        out_shape=jax.ShapeDtypeStruct((T, D_MODEL), jnp.float32),
        compiler_params=pltpu.CompilerParams(
            dimension_semantics=("arbitrary",), vmem_limit_bytes=VMEM_LIMIT_BYTES),
        name="glu_residual",
    )(z, res, w_bf16)


ROUTE_ROWS = 256
CAND = PEER_TOPK * PEER_TOPK


def _topk_rows(s, payload):
    n_rows = s.shape[0]
    row = lax.broadcasted_iota(jnp.int32, s.shape, 0)
    vals, picks = [], []
    for _ in range(PEER_TOPK):
        m = jnp.max(s, axis=0, keepdims=True)
        pos = jnp.min(jnp.where(s == m, row, n_rows), axis=0, keepdims=True)
        sel = row == pos
        vals.append(m)
        if payload is None:
            picks.append(pos)
        else:
            picks.append(jnp.max(jnp.where(sel, payload, -1), axis=0, keepdims=True))
        s = jnp.where(sel, -jnp.inf, s)
    return jnp.concatenate(vals, axis=0), jnp.concatenate(picks, axis=0)


def _pair_rows(a0, a1, combine):
    half = PEER_TOPK // 2
    parts = [combine(a0[0:1, :], a1)]
    parts += [combine(a0[i:i + 1, :], a1[0:half, :]) for i in range(1, half)]
    parts.append(combine(a0[half:PEER_TOPK, :], a1[0:1, :]))
    return jnp.concatenate(parts, axis=0)


def _peer_route_kernel(h_ref, g_ref, wq_ref, sk_ref, xn_ref, eid_ref, gate_ref, xb_ref):
    hd = pl.program_id(1)

    @pl.when(hd == 0)
    def _():
        xn = _rmsnorm_rows(h_ref[...], g_ref[...])
        xn_ref[...] = xn
        xb_ref[...] = xn.astype(jnp.bfloat16)

    q = jnp.dot(xb_ref[...], wq_ref[...], preferred_element_type=jnp.float32)
    sv, si = [], []
    for c in range(2):
        qc = q[:, c * PEER_DHALF:(c + 1) * PEER_DHALF].astype(jnp.bfloat16)
        st = lax.dot_general(sk_ref[0, c], qc, (((1,), (1,)), ((), ())), preferred_element_type=jnp.float32)
        v, i = _topk_rows(st, None)
        sv.append(v)
        si.append(i)
    cand = _pair_rows(sv[0], sv[1], lambda a, b: a + b)
    cid = _pair_rows(si[0], si[1], lambda a, b: a * N_KEYS + b)
    fv, eid = _topk_rows(cand, cid)
    e = jnp.exp(fv - fv[0:1, :])
    gate_ref[...] = e / jnp.sum(e, axis=0, keepdims=True)
    eid_ref[...] = eid


def _peer_route(h, g, wq_bf16, sk_bf16):
    T = h.shape[0]
    rows = ROUTE_ROWS
    n_sel = PEER_HEADS * PEER_TOPK
    return pl.pallas_call(
        _peer_route_kernel,
        grid=(T // rows, PEER_HEADS),
        in_specs=[
            pl.BlockSpec((rows, D_MODEL), lambda tb, hd: (tb, 0)),
            pl.BlockSpec((1, D_MODEL), lambda tb, hd: (0, 0)),
            pl.BlockSpec((D_MODEL, 2 * PEER_DHALF), lambda tb, hd: (0, hd)),
            pl.BlockSpec((1, 2, N_KEYS, PEER_DHALF), lambda tb, hd: (hd, 0, 0, 0)),
        ],
        out_specs=[
            pl.BlockSpec((rows, D_MODEL), lambda tb, hd: (tb, 0)),
            pl.BlockSpec((PEER_TOPK, rows), lambda tb, hd: (hd, tb)),
            pl.BlockSpec((PEER_TOPK, rows), lambda tb, hd: (hd, tb)),
        ],
        out_shape=[
            jax.ShapeDtypeStruct((T, D_MODEL), jnp.float32),
            jax.ShapeDtypeStruct((n_sel, T), jnp.int32),
            jax.ShapeDtypeStruct((n_sel, T), jnp.float32),
        ],
        scratch_shapes=[pltpu.VMEM((rows, D_MODEL), jnp.bfloat16)],
        compiler_params=pltpu.CompilerParams(
            dimension_semantics=("arbitrary", "arbitrary"), vmem_limit_bytes=VMEM_LIMIT_BYTES),
        name="peer_route",
    )(h, g.reshape(1, D_MODEL), wq_bf16, sk_bf16)


N_SEL = PEER_HEADS * PEER_TOPK
SC_LANES = 16
GATHER_ROWS = PEER_TOPK
GATHERS_PER_TOKEN = N_SEL // GATHER_ROWS
GATHER_BUFS = 4
SC_TOKENS = 8


def _sc_gather_stream(wid, n_batches, tab_hbm, stage_srcs, stage_bufs, out_hbm, o_v, bufs, gsem, ssem, osem,
                      compute):
    idx_v = stage_bufs[0]

    def stage_copies(bi, slot):
        base = (wid * n_batches + bi) * SC_TOKENS
        return [pltpu.make_async_copy(src.at[pl.ds(base, SC_TOKENS)], buf.at[slot], ssem.at[slot])
                for src, buf in zip(stage_srcs, stage_bufs)]

    def out_copy(bi, slot):
        base = (wid * n_batches + bi) * SC_TOKENS
        return pltpu.make_async_copy(o_v.at[slot], out_hbm.at[pl.ds(base, SC_TOKENS)], osem.at[slot])

    def start(slot, t, kk, b):
        idx = idx_v[slot, t, pl.ds(kk * GATHER_ROWS, GATHER_ROWS)]
        pltpu.async_copy(tab_hbm.at[idx], bufs.at[b], gsem.at[b])

    def wait(b):
        pltpu.make_async_copy(tab_hbm.at[pl.ds(0, GATHER_ROWS)], bufs.at[b], gsem.at[b]).wait()

    for c in stage_copies(0, 0):
        c.start()
    for c in stage_copies(0, 0):
        c.wait()
    for q in range(GATHER_BUFS - 1):
        start(0, q // GATHERS_PER_TOKEN, q % GATHERS_PER_TOKEN, q % GATHER_BUFS)

    def batch(bi, carry):
        slot = bi % 2
        has_next = bi + 1 < n_batches

        @pl.when(has_next)
        def _():
            for c in stage_copies(bi + 1, 1 - slot):
                c.start()

        @pl.when(bi >= 2)
        def _():
            out_copy(bi, slot).wait()

        def tok(t, carry):
            @pl.when(jnp.logical_and(t == SC_TOKENS - 1, has_next))
            def _():
                for c in stage_copies(bi + 1, 1 - slot):
                    c.wait()

            for kk in range(GATHERS_PER_TOKEN):
                nq = kk + GATHER_BUFS - 1
                nk, nb = nq % GATHERS_PER_TOKEN, nq % GATHER_BUFS
                if nq < GATHERS_PER_TOKEN:
                    start(slot, t, nk, nb)
                else:
                    @pl.when(t + 1 < SC_TOKENS)
                    def _():
                        start(slot, t + 1, nk, nb)

                    @pl.when(jnp.logical_and(t + 1 == SC_TOKENS, has_next))
                    def _():
                        start(1 - slot, 0, nk, nb)

                wait(kk % GATHER_BUFS)
                compute(slot, t, kk, kk % GATHER_BUFS)
            return carry

        lax.fori_loop(0, SC_TOKENS, tok, 0)
        out_copy(bi, slot).start()
        return carry

    lax.fori_loop(0, n_batches, batch, 0)
    if n_batches >= 2:
        out_copy(n_batches - 2, (n_batches - 2) % 2).wait()
    out_copy(n_batches - 1, (n_batches - 1) % 2).wait()


def _sc_mesh_and_batches(n_tokens):
    info = plsc.get_sparse_core_info()
    assert info.num_lanes == SC_LANES
    n_workers = info.num_cores * info.num_subcores
    assert n_tokens % (n_workers * SC_TOKENS) == 0
    mesh = plsc.VectorSubcoreMesh(core_axis_name="c", subcore_axis_name="s")
    return info, mesh, n_tokens // (n_workers * SC_TOKENS)


def _peer_hidden_sc(eid, xn, u_tab):
    T = eid.shape[0]
    info, mesh, n_batches = _sc_mesh_and_batches(T)

    @functools.partial(
        pl.kernel, mesh=mesh,
        out_type=jax.ShapeDtypeStruct((T, N_SEL), jnp.float32),
        scratch_types=[
            pltpu.VMEM((2, SC_TOKENS, N_SEL), jnp.int32),
            pltpu.VMEM((2, SC_TOKENS, D_MODEL), jnp.float32),
            pltpu.VMEM((2, SC_TOKENS, N_SEL), jnp.float32),
            pltpu.VMEM((GATHER_BUFS, GATHER_ROWS, D_MODEL), jnp.float32),
            pltpu.VMEM((GATHER_ROWS, SC_LANES), jnp.float32),
            pltpu.SemaphoreType.DMA((GATHER_BUFS,)),
            pltpu.SemaphoreType.DMA((2,)),
            pltpu.SemaphoreType.DMA((2,)),
        ],
        compiler_params=pltpu.CompilerParams(needs_layout_passes=False),
        name="peer_hidden_sc",
    )
    def k(eid_hbm, xn_hbm, u_hbm, out_hbm, idx_v, x_v, o_v, bufs, acc_v, gsem, ssem, osem):
        wid = lax.axis_index("s") * info.num_cores + lax.axis_index("c")
        lane = lax.iota(jnp.int32, SC_LANES)
        zero = jnp.zeros((SC_LANES,), jnp.float32)

        def compute(slot, t, kk, b):
            @plsc.parallel_loop(0, D_MODEL // SC_LANES, carry=(zero,) * GATHER_ROWS)
            def accs(c, accs):
                xc = x_v[slot, t, pl.ds(c * SC_LANES, SC_LANES)]
                return tuple(accs[r] + bufs[b, r, pl.ds(c * SC_LANES, SC_LANES)] * xc for r in range(GATHER_ROWS))

            for r in range(GATHER_ROWS):
                acc_v[r, :] = accs[r]
            tot = zero
            for c in range(SC_LANES):
                tot = tot + plsc.load_gather(acc_v, [lane, jnp.full((SC_LANES,), c, jnp.int32)])
            o_v[slot, t, pl.ds(kk * GATHER_ROWS, GATHER_ROWS)] = tot

        _sc_gather_stream(wid, n_batches, u_hbm, [eid_hbm, xn_hbm], [idx_v, x_v], out_hbm, o_v, bufs,
                          gsem, ssem, osem, compute)

    return k(eid, xn, u_tab)


def _peer_combine_sc(eid, a, v_tab):
    T = eid.shape[0]
    info, mesh, n_batches = _sc_mesh_and_batches(T)

    @functools.partial(
        pl.kernel, mesh=mesh,
        out_type=jax.ShapeDtypeStruct((T, D_MODEL), jnp.float32),
        scratch_types=[
            pltpu.VMEM((2, SC_TOKENS, N_SEL), jnp.int32),
            pltpu.VMEM((2, SC_TOKENS, N_SEL), jnp.float32),
            pltpu.VMEM((2, SC_TOKENS, D_MODEL), jnp.float32),
            pltpu.VMEM((GATHER_BUFS, GATHER_ROWS, D_MODEL), jnp.float32),
            pltpu.SemaphoreType.DMA((GATHER_BUFS,)),
            pltpu.SemaphoreType.DMA((2,)),
            pltpu.SemaphoreType.DMA((2,)),
        ],
        compiler_params=pltpu.CompilerParams(needs_layout_passes=False),
        name="peer_combine_sc",
    )
    def k(eid_hbm, a_hbm, v_hbm, out_hbm, idx_v, a_v, o_v, bufs, gsem, ssem, osem):
        wid = lax.axis_index("s") * info.num_cores + lax.axis_index("c")

        def compute(slot, t, kk, b):
            svec = jnp.full((SC_LANES,), slot, jnp.int32)
            tvec = jnp.full((SC_LANES,), t, jnp.int32)
            ws = [plsc.load_gather(a_v, [svec, tvec, jnp.full((SC_LANES,), kk * GATHER_ROWS + r, jnp.int32)])
                  for r in range(GATHER_ROWS)]

            @plsc.parallel_loop(0, D_MODEL // SC_LANES, unroll=2)
            def _(c):
                sl = pl.ds(c * SC_LANES, SC_LANES)
                terms = [ws[r] * bufs[b, r, sl] for r in range(GATHER_ROWS)]
                if kk != 0:
                    terms.append(o_v[slot, t, sl])
                while len(terms) > 1:
                    pairs = [terms[i] + terms[i + 1] for i in range(0, len(terms) - 1, 2)]
                    terms = pairs + ([terms[-1]] if len(terms) % 2 else [])
                o_v[slot, t, sl] = terms[0]

        _sc_gather_stream(wid, n_batches, v_hbm, [eid_hbm, a_hbm], [idx_v, a_v], out_hbm, o_v, bufs,
                          gsem, ssem, osem, compute)

    return k(eid, a, v_tab)


def _peer_act_kernel(hp_ref, gate_ref, a_ref):
    a_ref[...] = _gelu(hp_ref[...]) * gate_ref[...]


ELEMENTWISE_ROWS = 1088
FIRST_PEER_ROWS = 1024


def _row_block(n_rows, cap):
    return max(r for r in range(SUBLANES, cap + 1, SUBLANES) if n_rows % r == 0)


def _peer_act(hpre, gate):
    T = hpre.shape[0]
    rows = _row_block(T, ELEMENTWISE_ROWS)
    spec = pl.BlockSpec((rows, N_SEL), lambda i: (i, 0))
    return pl.pallas_call(
        _peer_act_kernel, grid=(T // rows,), in_specs=[spec, spec], out_specs=spec,
        out_shape=jax.ShapeDtypeStruct((T, N_SEL), jnp.float32),
        compiler_params=pltpu.CompilerParams(dimension_semantics=("arbitrary",)),
        name="peer_act",
    )(hpre, gate)


def _residual_kernel(h_ref, c_ref, g_ref, o_ref, *, final_norm):
    y = h_ref[...] + c_ref[...]
    o_ref[...] = _rmsnorm_rows(y, g_ref[...]) if final_norm else y


def _residual(h, c, gfin, *, final_norm):
    T = h.shape[0]
    rows = _row_block(T, ELEMENTWISE_ROWS)
    spec = pl.BlockSpec((rows, D_MODEL), lambda i: (i, 0))
    return pl.pallas_call(
        functools.partial(_residual_kernel, final_norm=final_norm), grid=(T // rows,),
        in_specs=[spec, spec, pl.BlockSpec((1, D_MODEL), lambda i: (0, 0))], out_specs=spec,
        out_shape=jax.ShapeDtypeStruct((T, D_MODEL), jnp.float32),
        compiler_params=pltpu.CompilerParams(dimension_semantics=("arbitrary",)),
        name="peer_residual",
    )(h, c, gfin.reshape(1, D_MODEL))


def _peer_experts(eid, xn, gate, h, gfin, u_tab, v_tab, *, final_norm):
    hpre = _peer_hidden_sc(eid, xn, u_tab)
    a = _peer_act(hpre, gate)
    c = _peer_combine_sc(eid, a, v_tab)
    return _residual(h, c, gfin, final_norm=final_norm)


KV_WIDTH = N_KV_HEADS * HEAD_DIM
BF16 = jnp.bfloat16


def _qkv(h, gkv, gq, wkv_ref, wq_ref):
    kv = jnp.dot(_rmsnorm_rows(h, gkv).astype(BF16), wkv_ref[...], preferred_element_type=jnp.float32)
    q = jnp.dot(_rmsnorm_rows(h, gq).astype(BF16), wq_ref[...], preferred_element_type=jnp.float32)
    return kv, q


def _sink_softmax_pv(parts, sink):
    m = sink
    for s, _ in parts:
        m = jnp.maximum(m, jnp.max(s, axis=-1, keepdims=True))
    den = jnp.exp(sink - m)
    acc = None
    for s, v in parts:
        e = jnp.exp(s - m)
        den = den + jnp.sum(e, axis=-1, keepdims=True)
        pv = jnp.dot(e.astype(BF16), v.astype(BF16), preferred_element_type=jnp.float32)
        acc = pv if acc is None else acc + pv
    return acc / den


def _nt_dot(a, b):
    return lax.dot_general(a.astype(BF16), b.astype(BF16), (((1,), (1,)), ((), ())),
                           preferred_element_type=jnp.float32)


def _attn_prompt_kernel(h_ref, gkv_ref, gq_ref, wkv_ref, wq_ref, wo_ref, sink_ref,
                        o_ref, kvw_ref, prev_ref):
    blk = pl.program_id(1)
    h = h_ref[...]
    kv, q = _qkv(h, gkv_ref[...], gq_ref[...], wkv_ref, wq_ref)
    kvw_ref[0] = kv

    @pl.when(blk == 0)
    def _():
        prev_ref[...] = jnp.zeros_like(prev_ref)

    prev = prev_ref[...]
    qi = lax.broadcasted_iota(jnp.int32, (WINDOW, WINDOW), 0)
    kj = lax.broadcasted_iota(jnp.int32, (WINDOW, WINDOW), 1)
    prev_ok = jnp.logical_and(kj > qi, blk > 0)
    cur_ok = kj <= qi
    heads = []
    for kvh in range(N_KV_HEADS):
        ks = slice(kvh * HEAD_DIM, (kvh + 1) * HEAD_DIM)
        vs = slice(KV_WIDTH + kvh * HEAD_DIM, KV_WIDTH + (kvh + 1) * HEAD_DIM)
        for g in range(Q_PER_KV):
            hq = kvh * Q_PER_KV + g
            qh = q[:, hq * HEAD_DIM:(hq + 1) * HEAD_DIM]
            sp = jnp.where(prev_ok, _nt_dot(qh, prev[:, ks]) * ATTN_SCALE, -jnp.inf)
            sc = jnp.where(cur_ok, _nt_dot(qh, kv[:, ks]) * ATTN_SCALE, -jnp.inf)
            heads.append(_sink_softmax_pv([(sp, prev[:, vs]), (sc, kv[:, vs])], sink_ref[hq]))
    o = jnp.concatenate(heads, axis=1)
    o_ref[...] = h + jnp.dot(o.astype(BF16), wo_ref[...], preferred_element_type=jnp.float32)
    prev_ref[...] = kv


def _attn_prompt(h, gkv, gq, wkv, wq, wo, sinks, *, n_seq, seq_len):
    nb = seq_len // WINDOW
    row_spec = pl.BlockSpec((WINDOW, D_MODEL), lambda n, b: (n * nb + b, 0))
    full = lambda shape: pl.BlockSpec(shape, lambda n, b: (0,) * len(shape))
    return pl.pallas_call(
        _attn_prompt_kernel,
        grid=(n_seq, nb),
        in_specs=[
            row_spec, full((1, D_MODEL)), full((1, D_MODEL)),
            full((D_MODEL, 2 * KV_WIDTH)), full((D_MODEL, D_MODEL)), full((D_MODEL, D_MODEL)),
            pl.BlockSpec(memory_space=pltpu.SMEM),
        ],
        out_specs=[row_spec, pl.BlockSpec((1, WINDOW, 2 * KV_WIDTH), lambda n, b: (n, 0, 0))],
        out_shape=[
            jax.ShapeDtypeStruct(h.shape, jnp.float32),
            jax.ShapeDtypeStruct((n_seq, WINDOW, 2 * KV_WIDTH), jnp.float32),
        ],
        scratch_shapes=[pltpu.VMEM((WINDOW, 2 * KV_WIDTH), jnp.float32)],
        compiler_params=pltpu.CompilerParams(
            dimension_semantics=("arbitrary", "arbitrary"), vmem_limit_bytes=VMEM_LIMIT_BYTES),
        name="attn_prompt",
    )(h, gkv.reshape(1, D_MODEL), gq.reshape(1, D_MODEL), wkv, wq, wo, sinks)


ATTN_SEQS = 16


def _attn_sample_kernel(h_ref, ck_ref, cv_ref, gkv_ref, gq_ref, wkv_ref, wq_ref, wo_ref, sink_ref,
                        o_ref, kw_ref, vw_ref, kv_ref, q_ref, att_ref, *, n_new):
    h = h_ref[...]
    kv, q = _qkv(h, gkv_ref[...], gq_ref[...], wkv_ref, wq_ref)
    kv_ref[...] = kv
    q_ref[...] = q
    n_seq = h.shape[0] // n_new
    rows = Q_PER_KV * n_new
    qpos_c = lax.broadcasted_iota(jnp.int32, (rows, WINDOW), 0) % n_new
    cache_ok = lax.broadcasted_iota(jnp.int32, (rows, WINDOW), 1) > qpos_c
    qpos_n = lax.broadcasted_iota(jnp.int32, (rows, n_new), 0) % n_new
    new_ok = lax.broadcasted_iota(jnp.int32, (rows, n_new), 1) <= qpos_n

    def per_seq(n, carry):
        r0 = pl.multiple_of(n * n_new, n_new)
        kvn = kv_ref[pl.ds(r0, n_new), :]
        qn = q_ref[pl.ds(r0, n_new), :]
        ck = ck_ref[n]
        cv = cv_ref[n]
        kw_ref[n, 0:WINDOW - n_new, :] = ck[n_new:, :]
        kw_ref[n, WINDOW - n_new:WINDOW, :] = kvn[:, 0:KV_WIDTH]
        vw_ref[n, 0:WINDOW - n_new, :] = cv[n_new:, :]
        vw_ref[n, WINDOW - n_new:WINDOW, :] = kvn[:, KV_WIDTH:2 * KV_WIDTH]
        outs = []
        for kvh in range(N_KV_HEADS):
            ks = slice(kvh * HEAD_DIM, (kvh + 1) * HEAD_DIM)
            vs = slice(KV_WIDTH + kvh * HEAD_DIM, KV_WIDTH + (kvh + 1) * HEAD_DIM)
            qs = jnp.concatenate(
                [qn[:, (kvh * Q_PER_KV + g) * HEAD_DIM:(kvh * Q_PER_KV + g + 1) * HEAD_DIM] for g in range(Q_PER_KV)],
                axis=0)
            s_c = jnp.where(cache_ok, _nt_dot(qs, ck[:, ks]) * ATTN_SCALE, -jnp.inf)
            s_n = jnp.where(new_ok, _nt_dot(qs, kvn[:, ks]) * ATTN_SCALE, -jnp.inf)
            sink = sink_ref[kvh][:, 0:1]
            o = _sink_softmax_pv([(s_c, cv[:, ks]), (s_n, kvn[:, vs])], sink)
            outs += [o[g * n_new:(g + 1) * n_new, :] for g in range(Q_PER_KV)]
        att_ref[pl.ds(r0, n_new), :] = jnp.concatenate(outs, axis=1)
        return carry

    lax.fori_loop(0, n_seq, per_seq, 0)
    o_ref[...] = h + jnp.dot(att_ref[...].astype(BF16), wo_ref[...], preferred_element_type=jnp.float32)


def _attn_sample(h, cache_k, cache_v, gkv, gq, wkv, wq, wo, sinks, *, n_new):
    n_seq = cache_k.shape[0]
    sb = ATTN_SEQS
    rows = sb * n_new
    row_spec = pl.BlockSpec((rows, D_MODEL), lambda i: (i, 0))
    win_spec = pl.BlockSpec((sb, WINDOW, KV_WIDTH), lambda i: (i, 0, 0))
    full = lambda shape: pl.BlockSpec(shape, lambda i: (0,) * len(shape))
    sink_rows = jnp.repeat(sinks.reshape(N_KV_HEADS, Q_PER_KV), n_new, axis=1)[:, :, None]
    sink_rows = jnp.broadcast_to(sink_rows, (N_KV_HEADS, Q_PER_KV * n_new, LANES))
    return pl.pallas_call(
        functools.partial(_attn_sample_kernel, n_new=n_new),
        grid=(n_seq // sb,),
        in_specs=[
            row_spec, win_spec, win_spec, full((1, D_MODEL)), full((1, D_MODEL)),
            full((D_MODEL, 2 * KV_WIDTH)), full((D_MODEL, D_MODEL)), full((D_MODEL, D_MODEL)),
            full((N_KV_HEADS, Q_PER_KV * n_new, LANES)),
        ],
        out_specs=[row_spec, win_spec, win_spec],
        out_shape=[
            jax.ShapeDtypeStruct(h.shape, jnp.float32),
            jax.ShapeDtypeStruct((n_seq, WINDOW, KV_WIDTH), jnp.float32),
            jax.ShapeDtypeStruct((n_seq, WINDOW, KV_WIDTH), jnp.float32),
        ],
        scratch_shapes=[
            pltpu.VMEM((rows, 2 * KV_WIDTH), jnp.float32),
            pltpu.VMEM((rows, D_MODEL), jnp.float32),
            pltpu.VMEM((rows, D_MODEL), jnp.float32),
        ],
        compiler_params=pltpu.CompilerParams(
            dimension_semantics=("arbitrary",), vmem_limit_bytes=VMEM_LIMIT_BYTES),
        name="attn_sample",
    )(h, cache_k, cache_v, gkv.reshape(1, D_MODEL), gq.reshape(1, D_MODEL), wkv, wq, wo, sink_rows)


def kernel(x_prompt, x_sample, state_ssm_re, state_ssm_im, cache_k_win, cache_v_win, norm_mix, norm_ffn, norm_kv, norm_final, ssm_lam_re, ssm_lam_im, ssm_log_dt, ssm_b_re, ssm_b_im, ssm_c_re, ssm_c_im, ssm_d, ssm_w_glu, w_kv, w_q, attn_sinks, w_o, peer_w_q, peer_sub_keys, peer_u, peer_v):
    bmat, cmat, apr, api = _s5_discretize(ssm_lam_re[0], ssm_lam_im[0], ssm_log_dt[0], ssm_b_re[0], ssm_b_im[0], ssm_c_re[0], ssm_c_im[0])
    cmat = cmat.astype(jnp.bfloat16)
    wglu = ssm_w_glu[0].astype(jnp.bfloat16)
    wkv = w_kv.astype(BF16)
    wq = w_q[0].astype(BF16)
    wo = w_o[0].astype(BF16)
    peer_wq = [peer_w_q[layer].astype(BF16) for layer in range(2)]
    peer_sk = [peer_sub_keys[layer].astype(BF16) for layer in range(2)]

    def peer(h, layer, final_norm, head_rows=0):
        if head_rows:
            return jnp.concatenate([peer(h[:head_rows], layer, final_norm),
                                    peer(h[head_rows:], layer, final_norm)], axis=0)
        xn, eid_t, gate_t = _peer_route(h, norm_ffn[layer], peer_wq[layer], peer_sk[layer])
        return _peer_experts(eid_t.T, xn, gate_t.T, h, norm_final, peer_u[layer], peer_v[layer],
                             final_norm=final_norm)

    def trunk(x_p, x_s, s_re, s_im, c_k, c_v):
        n_p, n_s = x_p.shape[0], x_s.shape[0]
        xp = x_p.reshape(-1, D_MODEL)
        xs = x_s.reshape(-1, D_MODEL)
        n_prompt_rows = xp.shape[0]
        z0 = jnp.zeros((n_p, N_CHUNKS, 1, CHUNK_STATE), jnp.float32)
        zp, srp, sip = _s5_mixer(xp, z0, z0, norm_mix[0], ssm_d[0], bmat, cmat, apr, api,
                                 chain=True, seq_len=x_p.shape[1])
        zs, srs, sis = _s5_mixer(xs, _state_to_chunks(s_re), _state_to_chunks(s_im), norm_mix[0], ssm_d[0],
                                 bmat, cmat, apr, api, chain=False, seq_len=x_s.shape[1])
        h1 = jnp.concatenate([_glu_residual(zp, xp, wglu), _glu_residual(zs, xs, wglu)], axis=0)
        h2 = peer(h1, 0, False, head_rows=FIRST_PEER_ROWS)
        h3p, kvw_p = _attn_prompt(h2[:n_prompt_rows], norm_kv, norm_mix[1], wkv, wq, wo, attn_sinks[0],
                                  n_seq=n_p, seq_len=x_p.shape[1])
        h3s, kw_s, vw_s = _attn_sample(h2[n_prompt_rows:], c_k.reshape(n_s, WINDOW, KV_WIDTH),
                                       c_v.reshape(n_s, WINDOW, KV_WIDTH), norm_kv, norm_mix[1],
                                       wkv, wq, wo, attn_sinks[0], n_new=x_s.shape[1])
        y = peer(jnp.concatenate([h3p, h3s], axis=0), 1, True)
        win = lambda a: a.reshape(a.shape[0], WINDOW, N_KV_HEADS, HEAD_DIM)
        return (y[:n_prompt_rows].reshape(x_p.shape), y[n_prompt_rows:].reshape(x_s.shape),
                _chunks_to_state(srp), _chunks_to_state(sip),
                win(kvw_p[:, :, :KV_WIDTH]), win(kvw_p[:, :, KV_WIDTH:]),
                _chunks_to_state(srs), _chunks_to_state(sis), win(kw_s), win(vw_s))

    n_groups = 4
    gp = x_prompt.shape[0] // n_groups
    gs = x_sample.shape[0] // n_groups
    outs = [trunk(x_prompt[i * gp:(i + 1) * gp], x_sample[i * gs:(i + 1) * gs],
                  state_ssm_re[0, i * gs:(i + 1) * gs], state_ssm_im[0, i * gs:(i + 1) * gs],
                  cache_k_win[i * gs:(i + 1) * gs], cache_v_win[i * gs:(i + 1) * gs])
            for i in range(n_groups)]
    cat = [jnp.concatenate(parts, axis=0) for parts in zip(*outs)]
    return (cat[0], cat[1], cat[2][None], cat[3][None], cat[4], cat[5], cat[6][None], cat[7][None],
            cat[8], cat[9])
```

```python
import functools
import math

import jax
import jax.numpy as jnp
from jax import lax
from jax.experimental import pallas as pl
from jax.experimental.pallas import tpu as pltpu
from jax.experimental.pallas import tpu_sc as plsc

D_MODEL = 1024
GROUP_SIZE = 16
N_GROUPS = D_MODEL // GROUP_SIZE
STATE_DIM = 64
HEAD_DIM = 64
N_Q_HEADS = D_MODEL // HEAD_DIM
N_KV_HEADS = N_Q_HEADS // 8
Q_PER_KV = N_Q_HEADS // N_KV_HEADS
WINDOW = 128
PAST_LEN = 16384
ATTN_SCALE = 1.0 / math.sqrt(HEAD_DIM)
PEER_HEADS = 8
N_KEYS = 128
PEER_TOPK = 16
PEER_DHALF = 128
EPS = 1e-5

LANES = 128
SUBLANES = 8
VMEM_LIMIT_BYTES = 56 * 1024 * 1024

GROUPS_PER_CHUNK = LANES // GROUP_SIZE
N_CHUNKS = N_GROUPS // GROUPS_PER_CHUNK
CHUNK_STATE = GROUPS_PER_CHUNK * STATE_DIM
S5_ROWS = 256


def _rmsnorm_rows(x, g):
    r = lax.rsqrt(jnp.mean(x * x, axis=-1, keepdims=True) + EPS)
    return x * r * g


def _gelu(x):
    return 0.5 * x * (1.0 + lax.erf(x * (1.0 / math.sqrt(2.0))))


def _s5_discretize(lam_re, lam_im, log_dt, b_re, b_im, c_re, c_im):
    f32 = jnp.float32
    lr = lam_re.astype(f32)
    li = lam_im.astype(f32)
    dt = jnp.exp(log_dt.astype(f32))[:, None]
    mag = jnp.exp(lr * dt)
    ab_re = mag * jnp.cos(li * dt)
    ab_im = mag * jnp.sin(li * dt)
    den = lr * lr + li * li
    f_re = ((ab_re - 1.0) * lr + ab_im * li) / den
    f_im = (ab_im * lr - (ab_re - 1.0) * li) / den
    br = b_re.astype(f32)
    bi = b_im.astype(f32)
    bb_re = f_re[..., None] * br - f_im[..., None] * bi
    bb_im = f_re[..., None] * bi + f_im[..., None] * br
    eye = jnp.eye(GROUPS_PER_CHUNK, dtype=f32)

    def chunk_rows(v):
        return v.reshape(N_CHUNKS, 1, CHUNK_STATE)

    def in_blocks(bb):
        t = bb.reshape(N_CHUNKS, GROUPS_PER_CHUNK, STATE_DIM, GROUP_SIZE).transpose(0, 1, 3, 2)
        return jnp.einsum('mgjp,gh->mgjhp', t, eye).reshape(N_CHUNKS, LANES, CHUNK_STATE)

    def out_blocks(c):
        t = c.astype(f32).reshape(N_CHUNKS, GROUPS_PER_CHUNK, GROUP_SIZE, STATE_DIM).transpose(0, 1, 3, 2)
        return jnp.einsum('mgpj,gh->mgphj', t, eye).reshape(N_CHUNKS, CHUNK_STATE, LANES)

    bfull = jnp.concatenate([in_blocks(bb_re), in_blocks(bb_im)], axis=2)
    b_hi = bfull.astype(jnp.bfloat16)
    b_lo = (bfull - b_hi.astype(f32)).astype(jnp.bfloat16)
    bmat = (jnp.concatenate([b_hi, b_hi], axis=1), b_lo)
    cmat = jnp.concatenate([out_blocks(c_re), -out_blocks(c_im)], axis=1)
    pr, pi = [ab_re], [ab_im]
    for _ in range(SUBLANES - 1):
        pr, pi = pr + [pr[-1] * ab_re - pi[-1] * ab_im], pi + [pr[-1] * ab_im + pi[-1] * ab_re]
    apr = jnp.concatenate([chunk_rows(v) for v in pr], axis=1)
    api = jnp.concatenate([chunk_rows(v) for v in pi], axis=1)
    return bmat, cmat, apr, api


def _split_bf16(x):
    hi = x.astype(jnp.bfloat16)
    return hi, (x - hi.astype(jnp.float32)).astype(jnp.bfloat16)


def _s5_kernel(x_ref, g_ref, d_ref, bhh_ref, blo_ref, c_ref, apr_ref, api_ref, h0r_ref, h0i_ref,
               z_ref, sr_ref, si_ref, u_ref, us_ref, st_ref, cr_ref, ci_ref, *, chain, blocks_per_seq):
    rb = pl.program_id(0)
    m = pl.program_id(1)
    rows = x_ref.shape[0]
    n_tiles = rows // SUBLANES

    @pl.when(m == 0)
    def _():
        u = _rmsnorm_rows(x_ref[...], g_ref[...])
        for mm in range(N_CHUNKS):
            uc = u[:, mm * LANES:(mm + 1) * LANES]
            u_ref[mm] = uc
            hi, lo = _split_bf16(uc)
            us_ref[mm, :, 0:LANES] = hi
            us_ref[mm, :, LANES:2 * LANES] = lo

    u = u_ref[m]
    us = us_ref[m]
    st_ref[...] = (jnp.dot(us, bhh_ref[0], preferred_element_type=jnp.float32)
                   + jnp.dot(us[:, 0:LANES], blo_ref[0], preferred_element_type=jnp.float32))

    apr = apr_ref[0]
    api = api_ref[0]
    row = lax.broadcasted_iota(jnp.int32, (SUBLANES, CHUNK_STATE), 0)

    if chain:
        @pl.when(rb % blocks_per_seq == 0)
        def _():
            cr_ref[m] = h0r_ref[0, 0]
            ci_ref[m] = h0i_ref[0, 0]

    def tile_step(k, carry):
        r0 = pl.multiple_of(k * SUBLANES, SUBLANES)
        xr = st_ref[pl.ds(r0, SUBLANES), 0:CHUNK_STATE]
        xi = st_ref[pl.ds(r0, SUBLANES), CHUNK_STATE:2 * CHUNK_STATE]
        for d in (1, 2, 4):
            ar = apr[d - 1:d, :]
            ai = api[d - 1:d, :]
            sr = jnp.where(row >= d, pltpu.roll(xr, d, axis=0), 0.0)
            si = jnp.where(row >= d, pltpu.roll(xi, d, axis=0), 0.0)
            xr, xi = xr + ar * sr - ai * si, xi + ar * si + ai * sr
        if chain:
            cr, ci = carry
        else:
            cr = h0r_ref[k, 0]
            ci = h0i_ref[k, 0]
        hr = xr + apr * cr - api * ci
        hi = xi + apr * ci + api * cr
        st_ref[pl.ds(r0, SUBLANES), 0:CHUNK_STATE] = hr
        st_ref[pl.ds(r0, SUBLANES), CHUNK_STATE:2 * CHUNK_STATE] = hi
        lr_ = hr[SUBLANES - 1:SUBLANES, :]
        li_ = hi[SUBLANES - 1:SUBLANES, :]
        if chain:
            return lr_, li_
        sr_ref[k, m] = lr_
        si_ref[k, m] = li_
        return carry

    if chain:
        cr, ci = lax.fori_loop(0, n_tiles, tile_step, (cr_ref[m], ci_ref[m]))
        cr_ref[m] = cr
        ci_ref[m] = ci
        sr_ref[0, m] = cr
        si_ref[0, m] = ci
    else:
        lax.fori_loop(0, n_tiles, tile_step, 0)

    y = jnp.dot(st_ref[...].astype(jnp.bfloat16), c_ref[0], preferred_element_type=jnp.float32)
    y = y + d_ref[0] * u
    z_ref[...] = _gelu(y)


def _s5_mixer(x, h0r, h0i, g, d_skip, bmat, cmat_bf16, apr, api, *, chain, seq_len):
    T = x.shape[0]
    nseq = h0r.shape[0]
    rows = S5_ROWS
    if chain:
        blocks_per_seq = seq_len // rows
        seq_blk = 1
        seq_map = lambda rb, m: (rb // blocks_per_seq, m, 0, 0)
        out_map = lambda rb, m: (rb // blocks_per_seq, 0, 0, 0)
    else:
        assert seq_len == SUBLANES
        blocks_per_seq = 1
        seq_blk = rows // SUBLANES
        seq_map = lambda rb, m: (rb, m, 0, 0)
        out_map = lambda rb, m: (rb, 0, 0, 0)
    grid = (T // rows, N_CHUNKS)
    kern = functools.partial(_s5_kernel, chain=chain, blocks_per_seq=blocks_per_seq)
    st_spec = pl.BlockSpec((seq_blk, 1, 1, CHUNK_STATE), seq_map)
    out_st_spec = pl.BlockSpec((seq_blk, N_CHUNKS, 1, CHUNK_STATE), out_map)
    z, sr, si = pl.pallas_call(
        kern,
        grid=grid,
        in_specs=[
            pl.BlockSpec((rows, D_MODEL), lambda rb, m: (rb, 0)),
            pl.BlockSpec((1, D_MODEL), lambda rb, m: (0, 0)),
            pl.BlockSpec((1, 1, LANES), lambda rb, m: (m, 0, 0)),
            pl.BlockSpec((1, 2 * LANES, 2 * CHUNK_STATE), lambda rb, m: (m, 0, 0)),
            pl.BlockSpec((1, LANES, 2 * CHUNK_STATE), lambda rb, m: (m, 0, 0)),
            pl.BlockSpec((1, 2 * CHUNK_STATE, LANES), lambda rb, m: (m, 0, 0)),
            pl.BlockSpec((1, SUBLANES, CHUNK_STATE), lambda rb, m: (m, 0, 0)),
            pl.BlockSpec((1, SUBLANES, CHUNK_STATE), lambda rb, m: (m, 0, 0)),
            st_spec, st_spec,
        ],
        out_specs=[
            pl.BlockSpec((rows, LANES), lambda rb, m: (rb, m)),
            out_st_spec, out_st_spec,
        ],
        out_shape=[
            jax.ShapeDtypeStruct((T, D_MODEL), jnp.float32),
            jax.ShapeDtypeStruct((nseq, N_CHUNKS, 1, CHUNK_STATE), jnp.float32),
            jax.ShapeDtypeStruct((nseq, N_CHUNKS, 1, CHUNK_STATE), jnp.float32),
        ],
        scratch_shapes=[
            pltpu.VMEM((N_CHUNKS, rows, LANES), jnp.float32),
            pltpu.VMEM((N_CHUNKS, rows, 2 * LANES), jnp.bfloat16),
            pltpu.VMEM((rows, 2 * CHUNK_STATE), jnp.float32),
            pltpu.VMEM((N_CHUNKS, 1, CHUNK_STATE), jnp.float32),
            pltpu.VMEM((N_CHUNKS, 1, CHUNK_STATE), jnp.float32),
        ],
        compiler_params=pltpu.CompilerParams(
            dimension_semantics=("arbitrary", "arbitrary"), vmem_limit_bytes=VMEM_LIMIT_BYTES),
        name="s5_mixer",
    )(x, g.reshape(1, D_MODEL), d_skip.reshape(N_CHUNKS, 1, LANES), bmat[0], bmat[1], cmat_bf16, apr, api,
      h0r, h0i)
    return z, sr, si


def _state_to_chunks(h):
    return h.reshape(h.shape[0], N_CHUNKS, 1, CHUNK_STATE)


def _chunks_to_state(s):
    return s.reshape(s.shape[0], N_GROUPS, STATE_DIM)


def _glu_kernel(z_ref, res_ref, w_ref, o_ref):
    zz = jnp.dot(z_ref[...].astype(jnp.bfloat16), w_ref[...], preferred_element_type=jnp.float32)
    a = zz[:, :D_MODEL]
    b = zz[:, D_MODEL:]
    o_ref[...] = res_ref[...] + a * (1.0 / (1.0 + jnp.exp(-b)))


def _glu_residual(z, res, w_bf16):
    T = z.shape[0]
    rows = math.gcd(T, 512)
    return pl.pallas_call(
        _glu_kernel,
        grid=(T // rows,),
        in_specs=[
            pl.BlockSpec((rows, D_MODEL), lambda i: (i, 0)),
            pl.BlockSpec((rows, D_MODEL), lambda i: (i, 0)),
            pl.BlockSpec((D_MODEL, 2 * D_MODEL), lambda i: (0, 0)),
        ],
        out_specs=pl.BlockSpec((rows, D_MODEL), lambda i: (i, 0)),
        out_shape=jax.ShapeDtypeStruct((T, D_MODEL), jnp.float32),
        compiler_params=pltpu.CompilerParams(
            dimension_semantics=("arbitrary",), vmem_limit_bytes=VMEM_LIMIT_BYTES),
        name="glu_residual",
    )(z, res, w_bf16)


ROUTE_ROWS = 256
CAND = PEER_TOPK * PEER_TOPK


def _topk_rows(s, payload):
    n_rows = s.shape[0]
    row = lax.broadcasted_iota(jnp.int32, s.shape, 0)
    vals, picks = [], []
    for _ in range(PEER_TOPK):
        m = jnp.max(s, axis=0, keepdims=True)
        pos = jnp.min(jnp.where(s == m, row, n_rows), axis=0, keepdims=True)
        sel = row == pos
        vals.append(m)
        if payload is None:
            picks.append(pos)
        else:
            picks.append(jnp.max(jnp.where(sel, payload, -1), axis=0, keepdims=True))
        s = jnp.where(sel, -jnp.inf, s)
    return jnp.concatenate(vals, axis=0), jnp.concatenate(picks, axis=0)


def _pair_rows(a0, a1, combine):
    half = PEER_TOPK // 2
    parts = [combine(a0[0:1, :], a1)]
    parts += [combine(a0[i:i + 1, :], a1[0:half, :]) for i in range(1, half)]
    parts.append(combine(a0[half:PEER_TOPK, :], a1[0:1, :]))
    return jnp.concatenate(parts, axis=0)


def _peer_route_kernel(h_ref, g_ref, wq_ref, sk_ref, xn_ref, eid_ref, gate_ref, xb_ref):
    hd = pl.program_id(1)

    @pl.when(hd == 0)
    def _():
        xn = _rmsnorm_rows(h_ref[...], g_ref[...])
        xn_ref[...] = xn
        xb_ref[...] = xn.astype(jnp.bfloat16)

    q = jnp.dot(xb_ref[...], wq_ref[...], preferred_element_type=jnp.float32)
    sv, si = [], []
    for c in range(2):
        qc = q[:, c * PEER_DHALF:(c + 1) * PEER_DHALF].astype(jnp.bfloat16)
        st = lax.dot_general(sk_ref[0, c], qc, (((1,), (1,)), ((), ())), preferred_element_type=jnp.float32)
        v, i = _topk_rows(st, None)
        sv.append(v)
        si.append(i)
    cand = _pair_rows(sv[0], sv[1], lambda a, b: a + b)
    cid = _pair_rows(si[0], si[1], lambda a, b: a * N_KEYS + b)
    fv, eid = _topk_rows(cand, cid)
    e = jnp.exp(fv - fv[0:1, :])
    gate_ref[...] = e / jnp.sum(e, axis=0, keepdims=True)
    eid_ref[...] = eid


def _peer_route(h, g, wq_bf16, sk_bf16):
    T = h.shape[0]
    rows = ROUTE_ROWS
    n_sel = PEER_HEADS * PEER_TOPK
    return pl.pallas_call(
        _peer_route_kernel,
        grid=(T // rows, PEER_HEADS),
        in_specs=[
            pl.BlockSpec((rows, D_MODEL), lambda tb, hd: (tb, 0)),
            pl.BlockSpec((1, D_MODEL), lambda tb, hd: (0, 0)),
            pl.BlockSpec((D_MODEL, 2 * PEER_DHALF), lambda tb, hd: (0, hd)),
            pl.BlockSpec((1, 2, N_KEYS, PEER_DHALF), lambda tb, hd: (hd, 0, 0, 0)),
        ],
        out_specs=[
            pl.BlockSpec((rows, D_MODEL), lambda tb, hd: (tb, 0)),
            pl.BlockSpec((PEER_TOPK, rows), lambda tb, hd: (hd, tb)),
            pl.BlockSpec((PEER_TOPK, rows), lambda tb, hd: (hd, tb)),
        ],
        out_shape=[
            jax.ShapeDtypeStruct((T, D_MODEL), jnp.float32),
            jax.ShapeDtypeStruct((n_sel, T), jnp.int32),
            jax.ShapeDtypeStruct((n_sel, T), jnp.float32),
        ],
        scratch_shapes=[pltpu.VMEM((rows, D_MODEL), jnp.bfloat16)],
        compiler_params=pltpu.CompilerParams(
            dimension_semantics=("arbitrary", "arbitrary"), vmem_limit_bytes=VMEM_LIMIT_BYTES),
        name="peer_route",
    )(h, g.reshape(1, D_MODEL), wq_bf16, sk_bf16)


N_SEL = PEER_HEADS * PEER_TOPK
SC_LANES = 16
GATHER_ROWS = PEER_TOPK
GATHERS_PER_TOKEN = N_SEL // GATHER_ROWS
GATHER_BUFS = 4
SC_TOKENS = 8


def _sc_gather_stream(wid, n_batches, tab_hbm, stage_srcs, stage_bufs, out_hbm, o_v, bufs, gsem, ssem, osem,
                      compute):
    idx_v = stage_bufs[0]

    def stage_copies(bi, slot):
        base = (wid * n_batches + bi) * SC_TOKENS
        return [pltpu.make_async_copy(src.at[pl.ds(base, SC_TOKENS)], buf.at[slot], ssem.at[slot])
                for src, buf in zip(stage_srcs, stage_bufs)]

    def out_copy(bi, slot):
        base = (wid * n_batches + bi) * SC_TOKENS
        return pltpu.make_async_copy(o_v.at[slot], out_hbm.at[pl.ds(base, SC_TOKENS)], osem.at[slot])

    def start(slot, t, kk, b):
        idx = idx_v[slot, t, pl.ds(kk * GATHER_ROWS, GATHER_ROWS)]
        pltpu.async_copy(tab_hbm.at[idx], bufs.at[b], gsem.at[b])

    def wait(b):
        pltpu.make_async_copy(tab_hbm.at[pl.ds(0, GATHER_ROWS)], bufs.at[b], gsem.at[b]).wait()

    for c in stage_copies(0, 0):
        c.start()
    for c in stage_copies(0, 0):
        c.wait()
    for q in range(GATHER_BUFS - 1):
        start(0, q // GATHERS_PER_TOKEN, q % GATHERS_PER_TOKEN, q % GATHER_BUFS)

    def batch(bi, carry):
        slot = bi % 2
        has_next = bi + 1 < n_batches

        @pl.when(has_next)
        def _():
            for c in stage_copies(bi + 1, 1 - slot):
                c.start()

        @pl.when(bi >= 2)
        def _():
            out_copy(bi, slot).wait()

        def tok(t, carry):
            @pl.when(jnp.logical_and(t == SC_TOKENS - 1, has_next))
            def _():
                for c in stage_copies(bi + 1, 1 - slot):
                    c.wait()

            for kk in range(GATHERS_PER_TOKEN):
                nq = kk + GATHER_BUFS - 1
                nk, nb = nq % GATHERS_PER_TOKEN, nq % GATHER_BUFS
                if nq < GATHERS_PER_TOKEN:
                    start(slot, t, nk, nb)
                else:
                    @pl.when(t + 1 < SC_TOKENS)
                    def _():
                        start(slot, t + 1, nk, nb)

                    @pl.when(jnp.logical_and(t + 1 == SC_TOKENS, has_next))
                    def _():
                        start(1 - slot, 0, nk, nb)

                wait(kk % GATHER_BUFS)
                compute(slot, t, kk, kk % GATHER_BUFS)
            return carry

        lax.fori_loop(0, SC_TOKENS, tok, 0)
        out_copy(bi, slot).start()
        return carry

    lax.fori_loop(0, n_batches, batch, 0)
    if n_batches >= 2:
        out_copy(n_batches - 2, (n_batches - 2) % 2).wait()
    out_copy(n_batches - 1, (n_batches - 1) % 2).wait()


def _sc_mesh_and_batches(n_tokens):
    info = plsc.get_sparse_core_info()
    assert info.num_lanes == SC_LANES
    n_workers = info.num_cores * info.num_subcores
    assert n_tokens % (n_workers * SC_TOKENS) == 0
    mesh = plsc.VectorSubcoreMesh(core_axis_name="c", subcore_axis_name="s")
    return info, mesh, n_tokens // (n_workers * SC_TOKENS)


def _peer_hidden_sc(eid, xn, u_tab):
    T = eid.shape[0]
    info, mesh, n_batches = _sc_mesh_and_batches(T)

    @functools.partial(
        pl.kernel, mesh=mesh,
        out_type=jax.ShapeDtypeStruct((T, N_SEL), jnp.float32),
        scratch_types=[
            pltpu.VMEM((2, SC_TOKENS, N_SEL), jnp.int32),
            pltpu.VMEM((2, SC_TOKENS, D_MODEL), jnp.float32),
            pltpu.VMEM((2, SC_TOKENS, N_SEL), jnp.float32),
            pltpu.VMEM((GATHER_BUFS, GATHER_ROWS, D_MODEL), jnp.float32),
            pltpu.VMEM((GATHER_ROWS, SC_LANES), jnp.float32),
            pltpu.SemaphoreType.DMA((GATHER_BUFS,)),
            pltpu.SemaphoreType.DMA((2,)),
            pltpu.SemaphoreType.DMA((2,)),
        ],
        compiler_params=pltpu.CompilerParams(needs_layout_passes=False),
        name="peer_hidden_sc",
    )
    def k(eid_hbm, xn_hbm, u_hbm, out_hbm, idx_v, x_v, o_v, bufs, acc_v, gsem, ssem, osem):
        wid = lax.axis_index("s") * info.num_cores + lax.axis_index("c")
        lane = lax.iota(jnp.int32, SC_LANES)
        zero = jnp.zeros((SC_LANES,), jnp.float32)

        def compute(slot, t, kk, b):
            @plsc.parallel_loop(0, D_MODEL // SC_LANES, carry=(zero,) * GATHER_ROWS)
            def accs(c, accs):
                xc = x_v[slot, t, pl.ds(c * SC_LANES, SC_LANES)]
                return tuple(accs[r] + bufs[b, r, pl.ds(c * SC_LANES, SC_LANES)] * xc for r in range(GATHER_ROWS))

            for r in range(GATHER_ROWS):
                acc_v[r, :] = accs[r]
            tot = zero
            for c in range(SC_LANES):
                tot = tot + plsc.load_gather(acc_v, [lane, jnp.full((SC_LANES,), c, jnp.int32)])
            o_v[slot, t, pl.ds(kk * GATHER_ROWS, GATHER_ROWS)] = tot

        _sc_gather_stream(wid, n_batches, u_hbm, [eid_hbm, xn_hbm], [idx_v, x_v], out_hbm, o_v, bufs,
                          gsem, ssem, osem, compute)

    return k(eid, xn, u_tab)


def _peer_combine_sc(eid, a, v_tab):
    T = eid.shape[0]
    info, mesh, n_batches = _sc_mesh_and_batches(T)

    @functools.partial(
        pl.kernel, mesh=mesh,
        out_type=jax.ShapeDtypeStruct((T, D_MODEL), jnp.float32),
        scratch_types=[
            pltpu.VMEM((2, SC_TOKENS, N_SEL), jnp.int32),
            pltpu.VMEM((2, SC_TOKENS, N_SEL), jnp.float32),
            pltpu.VMEM((2, SC_TOKENS, D_MODEL), jnp.float32),
            pltpu.VMEM((GATHER_BUFS, GATHER_ROWS, D_MODEL), jnp.float32),
            pltpu.SemaphoreType.DMA((GATHER_BUFS,)),
            pltpu.SemaphoreType.DMA((2,)),
            pltpu.SemaphoreType.DMA((2,)),
        ],
        compiler_params=pltpu.CompilerParams(needs_layout_passes=False),
        name="peer_combine_sc",
    )
    def k(eid_hbm, a_hbm, v_hbm, out_hbm, idx_v, a_v, o_v, bufs, gsem, ssem, osem):
        wid = lax.axis_index("s") * info.num_cores + lax.axis_index("c")

        def compute(slot, t, kk, b):
            svec = jnp.full((SC_LANES,), slot, jnp.int32)
            tvec = jnp.full((SC_LANES,), t, jnp.int32)
            ws = [plsc.load_gather(a_v, [svec, tvec, jnp.full((SC_LANES,), kk * GATHER_ROWS + r, jnp.int32)])
                  for r in range(GATHER_ROWS)]

            @plsc.parallel_loop(0, D_MODEL // SC_LANES, unroll=2)
            def _(c):
                sl = pl.ds(c * SC_LANES, SC_LANES)
                terms = [ws[r] * bufs[b, r, sl] for r in range(GATHER_ROWS)]
                if kk != 0:
                    terms.append(o_v[slot, t, sl])
                while len(terms) > 1:
                    pairs = [terms[i] + terms[i + 1] for i in range(0, len(terms) - 1, 2)]
                    terms = pairs + ([terms[-1]] if len(terms) % 2 else [])
                o_v[slot, t, sl] = terms[0]

        _sc_gather_stream(wid, n_batches, v_hbm, [eid_hbm, a_hbm], [idx_v, a_v], out_hbm, o_v, bufs,
                          gsem, ssem, osem, compute)

    return k(eid, a, v_tab)


def _peer_act_kernel(hp_ref, gate_ref, a_ref):
    a_ref[...] = _gelu(hp_ref[...]) * gate_ref[...]


ELEMENTWISE_ROWS = 1088
FIRST_PEER_ROWS = 1024


def _row_block(n_rows, cap):
    return max(r for r in range(SUBLANES, cap + 1, SUBLANES) if n_rows % r == 0)


def _peer_act(hpre, gate):
    T = hpre.shape[0]
    rows = _row_block(T, ELEMENTWISE_ROWS)
    spec = pl.BlockSpec((rows, N_SEL), lambda i: (i, 0))
    return pl.pallas_call(
        _peer_act_kernel, grid=(T // rows,), in_specs=[spec, spec], out_specs=spec,
        out_shape=jax.ShapeDtypeStruct((T, N_SEL), jnp.float32),
        compiler_params=pltpu.CompilerParams(dimension_semantics=("arbitrary",)),
        name="peer_act",
    )(hpre, gate)


def _residual_kernel(h_ref, c_ref, g_ref, o_ref, *, final_norm):
    y = h_ref[...] + c_ref[...]
    o_ref[...] = _rmsnorm_rows(y, g_ref[...]) if final_norm else y


def _residual(h, c, gfin, *, final_norm):
    T = h.shape[0]
    rows = _row_block(T, ELEMENTWISE_ROWS)
    spec = pl.BlockSpec((rows, D_MODEL), lambda i: (i, 0))
    return pl.pallas_call(
        functools.partial(_residual_kernel, final_norm=final_norm), grid=(T // rows,),
        in_specs=[spec, spec, pl.BlockSpec((1, D_MODEL), lambda i: (0, 0))], out_specs=spec,
        out_shape=jax.ShapeDtypeStruct((T, D_MODEL), jnp.float32),
        compiler_params=pltpu.CompilerParams(dimension_semantics=("arbitrary",)),
        name="peer_residual",
    )(h, c, gfin.reshape(1, D_MODEL))


def _peer_experts_sc(eid, xn, gate, h, gfin, u_tab, v_tab, *, final_norm):
    hpre = _peer_hidden_sc(eid, xn, u_tab)
    a = _peer_act(hpre, gate)
    c = _peer_combine_sc(eid, a, v_tab)
    return _residual(h, c, gfin, final_norm=final_norm)


ROW_TILE = (D_MODEL // LANES, LANES)
PAIR_TILE = (2 * ROW_TILE[0], LANES)
TC_GATHER_TOKENS = 32
TC_PEER_ROWS = 512


def _peer_expert_tc_kernel(eid_ref, xn_ref, gate_ref, h_ref, gfin_ref, uv_hbm, out_ref, buf, sem, *, final_norm):
    n_tok = xn_ref.shape[0]
    n_sub = ROW_TILE[0]

    def row_copy(slot, e, r):
        return pltpu.make_async_copy(uv_hbm.at[e], buf.at[slot, r], sem.at[slot])

    def issue(t, slot):
        for r in range(N_SEL):
            row_copy(slot, eid_ref[t, r], r).start()

    def wait(slot):
        for r in range(N_SEL):
            row_copy(slot, 0, r).wait()

    ri = lax.broadcasted_iota(jnp.int32, (N_SEL, LANES), 0)
    li = lax.broadcasted_iota(jnp.int32, (N_SEL, LANES), 1)
    diag = ri == li

    def compute(t, slot):
        x = xn_ref[t]
        rows = buf.at[slot]
        partial = None
        for s in range(n_sub):
            term = rows[:, s, :] * x[s:s + 1, :]
            partial = term if partial is None else partial + term
        hcol = jnp.sum(partial, axis=1, keepdims=True)
        hrow = jnp.sum(jnp.where(diag, hcol, 0.0), axis=0, keepdims=True)
        arow = _gelu(hrow) * gate_ref[pl.ds(t, 1), :]
        acol = jnp.sum(jnp.where(diag, arow, 0.0), axis=1, keepdims=True)
        out = [jnp.sum(rows[:, n_sub + s, :] * acol, axis=0, keepdims=True) for s in range(n_sub)]
        y = h_ref[t] + jnp.concatenate(out, axis=0)
        if final_norm:
            r = lax.rsqrt(jnp.sum(y * y) * (1.0 / D_MODEL) + EPS)
            y = y * r * gfin_ref[...]
        out_ref[t] = y

    issue(0, 0)

    def body(t, carry):
        slot = t % 2

        @pl.when(t + 1 < n_tok)
        def _():
            issue(t + 1, 1 - slot)

        wait(slot)
        compute(t, slot)
        return carry

    lax.fori_loop(0, n_tok, body, 0)


def _pair_table(u_tab, v_tab):
    tile = lambda a: a.reshape((a.shape[0],) + ROW_TILE)
    return jnp.concatenate([tile(u_tab), tile(v_tab)], axis=1)


def _peer_experts_tc(eid, xn, gate, h, gfin, uv_tab, *, final_norm):
    T = xn.shape[0]
    nt = TC_GATHER_TOKENS
    tile = lambda a: a.reshape((a.shape[0],) + ROW_TILE)
    tok_spec = pl.BlockSpec((nt,) + ROW_TILE, lambda i: (i, 0, 0))
    out = pl.pallas_call(
        functools.partial(_peer_expert_tc_kernel, final_norm=final_norm),
        grid=(T // nt,),
        in_specs=[
            pl.BlockSpec((nt, N_SEL), lambda i: (i, 0), memory_space=pltpu.SMEM),
            tok_spec,
            pl.BlockSpec((nt, N_SEL), lambda i: (i, 0)),
            tok_spec,
            pl.BlockSpec(ROW_TILE, lambda i: (0, 0)),
            pl.BlockSpec(memory_space=pl.ANY),
        ],
        out_specs=tok_spec,
        out_shape=jax.ShapeDtypeStruct((T,) + ROW_TILE, jnp.float32),
        scratch_shapes=[
            pltpu.VMEM((2, N_SEL) + PAIR_TILE, jnp.float32),
            pltpu.SemaphoreType.DMA((2,)),
        ],
        compiler_params=pltpu.CompilerParams(
            dimension_semantics=("arbitrary",), vmem_limit_bytes=VMEM_LIMIT_BYTES),
        name="peer_experts_tc",
    )(eid, tile(xn), gate, tile(h), gfin.reshape(ROW_TILE), uv_tab)
    return out.reshape(T, D_MODEL)


KV_WIDTH = N_KV_HEADS * HEAD_DIM
BF16 = jnp.bfloat16


def _qkv(h, gkv, gq, wkv_ref, wq_ref):
    kv = jnp.dot(_rmsnorm_rows(h, gkv).astype(BF16), wkv_ref[...], preferred_element_type=jnp.float32)
    q = jnp.dot(_rmsnorm_rows(h, gq).astype(BF16), wq_ref[...], preferred_element_type=jnp.float32)
    return kv, q


def _sink_softmax_pv(parts, sink):
    m = sink
    for s, _ in parts:
        m = jnp.maximum(m, jnp.max(s, axis=-1, keepdims=True))
    den = jnp.exp(sink - m)
    acc = None
    for s, v in parts:
        e = jnp.exp(s - m)
        den = den + jnp.sum(e, axis=-1, keepdims=True)
        pv = jnp.dot(e.astype(BF16), v.astype(BF16), preferred_element_type=jnp.float32)
        acc = pv if acc is None else acc + pv
    return acc / den


def _nt_dot(a, b):
    return lax.dot_general(a.astype(BF16), b.astype(BF16), (((1,), (1,)), ((), ())),
                           preferred_element_type=jnp.float32)


def _attn_prompt_kernel(h_ref, gkv_ref, gq_ref, wkv_ref, wq_ref, wo_ref, sink_ref,
                        o_ref, kvw_ref, prev_ref):
    blk = pl.program_id(1)
    h = h_ref[...]
    kv, q = _qkv(h, gkv_ref[...], gq_ref[...], wkv_ref, wq_ref)
    kvw_ref[0] = kv

    @pl.when(blk == 0)
    def _():
        prev_ref[...] = jnp.zeros_like(prev_ref)

    prev = prev_ref[...]
    qi = lax.broadcasted_iota(jnp.int32, (WINDOW, WINDOW), 0)
    kj = lax.broadcasted_iota(jnp.int32, (WINDOW, WINDOW), 1)
    prev_ok = jnp.logical_and(kj > qi, blk > 0)
    cur_ok = kj <= qi
    heads = []
    for kvh in range(N_KV_HEADS):
        ks = slice(kvh * HEAD_DIM, (kvh + 1) * HEAD_DIM)
        vs = slice(KV_WIDTH + kvh * HEAD_DIM, KV_WIDTH + (kvh + 1) * HEAD_DIM)
        for g in range(Q_PER_KV):
            hq = kvh * Q_PER_KV + g
            qh = q[:, hq * HEAD_DIM:(hq + 1) * HEAD_DIM]
            sp = jnp.where(prev_ok, _nt_dot(qh, prev[:, ks]) * ATTN_SCALE, -jnp.inf)
            sc = jnp.where(cur_ok, _nt_dot(qh, kv[:, ks]) * ATTN_SCALE, -jnp.inf)
            heads.append(_sink_softmax_pv([(sp, prev[:, vs]), (sc, kv[:, vs])], sink_ref[hq]))
    o = jnp.concatenate(heads, axis=1)
    o_ref[...] = h + jnp.dot(o.astype(BF16), wo_ref[...], preferred_element_type=jnp.float32)
    prev_ref[...] = kv


def _attn_prompt(h, gkv, gq, wkv, wq, wo, sinks, *, n_seq, seq_len):
    nb = seq_len // WINDOW
    row_spec = pl.BlockSpec((WINDOW, D_MODEL), lambda n, b: (n * nb + b, 0))
    full = lambda shape: pl.BlockSpec(shape, lambda n, b: (0,) * len(shape))
    return pl.pallas_call(
        _attn_prompt_kernel,
        grid=(n_seq, nb),
        in_specs=[
            row_spec, full((1, D_MODEL)), full((1, D_MODEL)),
            full((D_MODEL, 2 * KV_WIDTH)), full((D_MODEL, D_MODEL)), full((D_MODEL, D_MODEL)),
            pl.BlockSpec(memory_space=pltpu.SMEM),
        ],
        out_specs=[row_spec, pl.BlockSpec((1, WINDOW, 2 * KV_WIDTH), lambda n, b: (n, 0, 0))],
        out_shape=[
            jax.ShapeDtypeStruct(h.shape, jnp.float32),
            jax.ShapeDtypeStruct((n_seq, WINDOW, 2 * KV_WIDTH), jnp.float32),
        ],
        scratch_shapes=[pltpu.VMEM((WINDOW, 2 * KV_WIDTH), jnp.float32)],
        compiler_params=pltpu.CompilerParams(
            dimension_semantics=("arbitrary", "arbitrary"), vmem_limit_bytes=VMEM_LIMIT_BYTES),
        name="attn_prompt",
    )(h, gkv.reshape(1, D_MODEL), gq.reshape(1, D_MODEL), wkv, wq, wo, sinks)


ATTN_SEQS = 16


def _attn_sample_kernel(h_ref, ck_ref, cv_ref, gkv_ref, gq_ref, wkv_ref, wq_ref, wo_ref, sink_ref,
                        o_ref, kw_ref, vw_ref, kv_ref, q_ref, att_ref, *, n_new):
    h = h_ref[...]
    kv, q = _qkv(h, gkv_ref[...], gq_ref[...], wkv_ref, wq_ref)
    kv_ref[...] = kv
    q_ref[...] = q
    n_seq = h.shape[0] // n_new
    rows = Q_PER_KV * n_new
    qpos_c = lax.broadcasted_iota(jnp.int32, (rows, WINDOW), 0) % n_new
    cache_ok = lax.broadcasted_iota(jnp.int32, (rows, WINDOW), 1) > qpos_c
    qpos_n = lax.broadcasted_iota(jnp.int32, (rows, n_new), 0) % n_new
    new_ok = lax.broadcasted_iota(jnp.int32, (rows, n_new), 1) <= qpos_n

    def per_seq(n, carry):
        r0 = pl.multiple_of(n * n_new, n_new)
        kvn = kv_ref[pl.ds(r0, n_new), :]
        qn = q_ref[pl.ds(r0, n_new), :]
        ck = ck_ref[n]
        cv = cv_ref[n]
        kw_ref[n, 0:WINDOW - n_new, :] = ck[n_new:, :]
        kw_ref[n, WINDOW - n_new:WINDOW, :] = kvn[:, 0:KV_WIDTH]
        vw_ref[n, 0:WINDOW - n_new, :] = cv[n_new:, :]
        vw_ref[n, WINDOW - n_new:WINDOW, :] = kvn[:, KV_WIDTH:2 * KV_WIDTH]
        outs = []
        for kvh in range(N_KV_HEADS):
            ks = slice(kvh * HEAD_DIM, (kvh + 1) * HEAD_DIM)
            vs = slice(KV_WIDTH + kvh * HEAD_DIM, KV_WIDTH + (kvh + 1) * HEAD_DIM)
            qs = jnp.concatenate(
                [qn[:, (kvh * Q_PER_KV + g) * HEAD_DIM:(kvh * Q_PER_KV + g + 1) * HEAD_DIM] for g in range(Q_PER_KV)],
                axis=0)
            s_c = jnp.where(cache_ok, _nt_dot(qs, ck[:, ks]) * ATTN_SCALE, -jnp.inf)
            s_n = jnp.where(new_ok, _nt_dot(qs, kvn[:, ks]) * ATTN_SCALE, -jnp.inf)
            sink = sink_ref[kvh][:, 0:1]
            o = _sink_softmax_pv([(s_c, cv[:, ks]), (s_n, kvn[:, vs])], sink)
            outs += [o[g * n_new:(g + 1) * n_new, :] for g in range(Q_PER_KV)]
        att_ref[pl.ds(r0, n_new), :] = jnp.concatenate(outs, axis=1)
        return carry

    lax.fori_loop(0, n_seq, per_seq, 0)
    o_ref[...] = h + jnp.dot(att_ref[...].astype(BF16), wo_ref[...], preferred_element_type=jnp.float32)


def _attn_sample(h, cache_k, cache_v, gkv, gq, wkv, wq, wo, sinks, *, n_new):
    n_seq = cache_k.shape[0]
    sb = ATTN_SEQS
    rows = sb * n_new
    row_spec = pl.BlockSpec((rows, D_MODEL), lambda i: (i, 0))
    win_spec = pl.BlockSpec((sb, WINDOW, KV_WIDTH), lambda i: (i, 0, 0))
    full = lambda shape: pl.BlockSpec(shape, lambda i: (0,) * len(shape))
    sink_rows = jnp.repeat(sinks.reshape(N_KV_HEADS, Q_PER_KV), n_new, axis=1)[:, :, None]
    sink_rows = jnp.broadcast_to(sink_rows, (N_KV_HEADS, Q_PER_KV * n_new, LANES))
    return pl.pallas_call(
        functools.partial(_attn_sample_kernel, n_new=n_new),
        grid=(n_seq // sb,),
        in_specs=[
            row_spec, win_spec, win_spec, full((1, D_MODEL)), full((1, D_MODEL)),
            full((D_MODEL, 2 * KV_WIDTH)), full((D_MODEL, D_MODEL)), full((D_MODEL, D_MODEL)),
            full((N_KV_HEADS, Q_PER_KV * n_new, LANES)),
        ],
        out_specs=[row_spec, win_spec, win_spec],
        out_shape=[
            jax.ShapeDtypeStruct(h.shape, jnp.float32),
            jax.ShapeDtypeStruct((n_seq, WINDOW, KV_WIDTH), jnp.float32),
            jax.ShapeDtypeStruct((n_seq, WINDOW, KV_WIDTH), jnp.float32),
        ],
        scratch_shapes=[
            pltpu.VMEM((rows, 2 * KV_WIDTH), jnp.float32),
            pltpu.VMEM((rows, D_MODEL), jnp.float32),
            pltpu.VMEM((rows, D_MODEL), jnp.float32),
        ],
        compiler_params=pltpu.CompilerParams(
            dimension_semantics=("arbitrary",), vmem_limit_bytes=VMEM_LIMIT_BYTES),
        name="attn_sample",
    )(h, cache_k, cache_v, gkv.reshape(1, D_MODEL), gq.reshape(1, D_MODEL), wkv, wq, wo, sink_rows)


def kernel(x_prompt, x_sample, state_ssm_re, state_ssm_im, cache_k_win, cache_v_win, norm_mix, norm_ffn, norm_kv, norm_final, ssm_lam_re, ssm_lam_im, ssm_log_dt, ssm_b_re, ssm_b_im, ssm_c_re, ssm_c_im, ssm_d, ssm_w_glu, w_kv, w_q, attn_sinks, w_o, peer_w_q, peer_sub_keys, peer_u, peer_v):
    bmat, cmat, apr, api = _s5_discretize(ssm_lam_re[0], ssm_lam_im[0], ssm_log_dt[0], ssm_b_re[0], ssm_b_im[0], ssm_c_re[0], ssm_c_im[0])
    cmat = cmat.astype(jnp.bfloat16)
    wglu = ssm_w_glu[0].astype(jnp.bfloat16)
    wkv = w_kv.astype(BF16)
    wq = w_q[0].astype(BF16)
    wo = w_o[0].astype(BF16)
    peer_wq = [peer_w_q[layer].astype(BF16) for layer in range(2)]
    peer_sk = [peer_sub_keys[layer].astype(BF16) for layer in range(2)]

    uv_tabs = [_pair_table(peer_u[layer], peer_v[layer]) for layer in range(2)]

    def peer(h, layer, final_norm, head_rows=0, tc_rows=0):
        if head_rows:
            return jnp.concatenate([peer(h[:head_rows], layer, final_norm),
                                    peer(h[head_rows:], layer, final_norm, tc_rows=tc_rows)], axis=0)
        xn, eid_t, gate_t = _peer_route(h, norm_ffn[layer], peer_wq[layer], peer_sk[layer])
        eid, gate = eid_t.T, gate_t.T
        n_sc = h.shape[0] - tc_rows
        out = _peer_experts_sc(eid[:n_sc], xn[:n_sc], gate[:n_sc], h[:n_sc], norm_final,
                               peer_u[layer], peer_v[layer], final_norm=final_norm)
        if tc_rows:
            out_tc = _peer_experts_tc(eid[n_sc:], xn[n_sc:], gate[n_sc:], h[n_sc:], norm_final, uv_tabs[layer],
                                      final_norm=final_norm)
            out = jnp.concatenate([out, out_tc], axis=0)
        return out

    def trunk(x_p, x_s, s_re, s_im, c_k, c_v):
        n_p, n_s = x_p.shape[0], x_s.shape[0]
        xp = x_p.reshape(-1, D_MODEL)
        xs = x_s.reshape(-1, D_MODEL)
        n_prompt_rows = xp.shape[0]
        z0 = jnp.zeros((n_p, N_CHUNKS, 1, CHUNK_STATE), jnp.float32)
        zp, srp, sip = _s5_mixer(xp, z0, z0, norm_mix[0], ssm_d[0], bmat, cmat, apr, api,
                                 chain=True, seq_len=x_p.shape[1])
        zs, srs, sis = _s5_mixer(xs, _state_to_chunks(s_re), _state_to_chunks(s_im), norm_mix[0], ssm_d[0],
                                 bmat, cmat, apr, api, chain=False, seq_len=x_s.shape[1])
        h1 = jnp.concatenate([_glu_residual(zp, xp, wglu), _glu_residual(zs, xs, wglu)], axis=0)
        h2 = peer(h1, 0, False, head_rows=FIRST_PEER_ROWS, tc_rows=TC_PEER_ROWS)
        h3p, kvw_p = _attn_prompt(h2[:n_prompt_rows], norm_kv, norm_mix[1], wkv, wq, wo, attn_sinks[0],
                                  n_seq=n_p, seq_len=x_p.shape[1])
        h3s, kw_s, vw_s = _attn_sample(h2[n_prompt_rows:], c_k.reshape(n_s, WINDOW, KV_WIDTH),
                                       c_v.reshape(n_s, WINDOW, KV_WIDTH), norm_kv, norm_mix[1],
                                       wkv, wq, wo, attn_sinks[0], n_new=x_s.shape[1])
        y = peer(jnp.concatenate([h3p, h3s], axis=0), 1, True, tc_rows=TC_PEER_ROWS)
        win = lambda a: a.reshape(a.shape[0], WINDOW, N_KV_HEADS, HEAD_DIM)
        return (y[:n_prompt_rows].reshape(x_p.shape), y[n_prompt_rows:].reshape(x_s.shape),
                _chunks_to_state(srp), _chunks_to_state(sip),
                win(kvw_p[:, :, :KV_WIDTH]), win(kvw_p[:, :, KV_WIDTH:]),
                _chunks_to_state(srs), _chunks_to_state(sis), win(kw_s), win(vw_s))

    n_groups = 4
    gp = x_prompt.shape[0] // n_groups
    gs = x_sample.shape[0] // n_groups
    outs = [trunk(x_prompt[i * gp:(i + 1) * gp], x_sample[i * gs:(i + 1) * gs],
                  state_ssm_re[0, i * gs:(i + 1) * gs], state_ssm_im[0, i * gs:(i + 1) * gs],
                  cache_k_win[i * gs:(i + 1) * gs], cache_v_win[i * gs:(i + 1) * gs])
            for i in range(n_groups)]
    cat = [jnp.concatenate(parts, axis=0) for parts in zip(*outs)]
    return (cat[0], cat[1], cat[2][None], cat[3][None], cat[4], cat[5], cat[6][None], cat[7][None],
            cat[8], cat[9])
```

```python
import functools
import math

import jax
import jax.numpy as jnp
from jax import lax
from jax.experimental import pallas as pl
from jax.experimental.pallas import tpu as pltpu
from jax.experimental.pallas import tpu_sc as plsc

D_MODEL = 1024
GROUP_SIZE = 16
N_GROUPS = D_MODEL // GROUP_SIZE
STATE_DIM = 64
HEAD_DIM = 64
N_Q_HEADS = D_MODEL // HEAD_DIM
N_KV_HEADS = N_Q_HEADS // 8
Q_PER_KV = N_Q_HEADS // N_KV_HEADS
WINDOW = 128
PAST_LEN = 16384
ATTN_SCALE = 1.0 / math.sqrt(HEAD_DIM)
PEER_HEADS = 8
N_KEYS = 128
PEER_TOPK = 16
PEER_DHALF = 128
EPS = 1e-5

LANES = 128
SUBLANES = 8
VMEM_LIMIT_BYTES = 56 * 1024 * 1024

GROUPS_PER_CHUNK = LANES // GROUP_SIZE
N_CHUNKS = N_GROUPS // GROUPS_PER_CHUNK
CHUNK_STATE = GROUPS_PER_CHUNK * STATE_DIM
S5_ROWS = 256


def _rmsnorm_rows(x, g):
    r = lax.rsqrt(jnp.mean(x * x, axis=-1, keepdims=True) + EPS)
    return x * r * g


def _gelu(x):
    return 0.5 * x * (1.0 + lax.erf(x * (1.0 / math.sqrt(2.0))))


def _s5_discretize(lam_re, lam_im, log_dt, b_re, b_im, c_re, c_im):
    f32 = jnp.float32
    lr = lam_re.astype(f32)
    li = lam_im.astype(f32)
    dt = jnp.exp(log_dt.astype(f32))[:, None]
    mag = jnp.exp(lr * dt)
    ab_re = mag * jnp.cos(li * dt)
    ab_im = mag * jnp.sin(li * dt)
    den = lr * lr + li * li
    f_re = ((ab_re - 1.0) * lr + ab_im * li) / den
    f_im = (ab_im * lr - (ab_re - 1.0) * li) / den
    br = b_re.astype(f32)
    bi = b_im.astype(f32)
    bb_re = f_re[..., None] * br - f_im[..., None] * bi
    bb_im = f_re[..., None] * bi + f_im[..., None] * br
    eye = jnp.eye(GROUPS_PER_CHUNK, dtype=f32)

    def chunk_rows(v):
        return v.reshape(N_CHUNKS, 1, CHUNK_STATE)

    def in_blocks(bb):
        t = bb.reshape(N_CHUNKS, GROUPS_PER_CHUNK, STATE_DIM, GROUP_SIZE).transpose(0, 1, 3, 2)
        return jnp.einsum('mgjp,gh->mgjhp', t, eye).reshape(N_CHUNKS, LANES, CHUNK_STATE)

    def out_blocks(c):
        t = c.astype(f32).reshape(N_CHUNKS, GROUPS_PER_CHUNK, GROUP_SIZE, STATE_DIM).transpose(0, 1, 3, 2)
        return jnp.einsum('mgpj,gh->mgphj', t, eye).reshape(N_CHUNKS, CHUNK_STATE, LANES)

    bfull = jnp.concatenate([in_blocks(bb_re), in_blocks(bb_im)], axis=2)
    b_hi = bfull.astype(jnp.bfloat16)
    b_lo = (bfull - b_hi.astype(f32)).astype(jnp.bfloat16)
    bmat = (jnp.concatenate([b_hi, b_hi], axis=1), b_lo)
    cmat = jnp.concatenate([out_blocks(c_re), -out_blocks(c_im)], axis=1)
    pr, pi = [ab_re], [ab_im]
    for _ in range(SUBLANES - 1):
        pr, pi = pr + [pr[-1] * ab_re - pi[-1] * ab_im], pi + [pr[-1] * ab_im + pi[-1] * ab_re]
    apr = jnp.concatenate([chunk_rows(v) for v in pr], axis=1)
    api = jnp.concatenate([chunk_rows(v) for v in pi], axis=1)
    return bmat, cmat, apr, api


def _split_bf16(x):
    hi = x.astype(jnp.bfloat16)
    return hi, (x - hi.astype(jnp.float32)).astype(jnp.bfloat16)


def _s5_kernel(x_ref, g_ref, d_ref, bhh_ref, blo_ref, c_ref, apr_ref, api_ref, h0r_ref, h0i_ref,
               z_ref, sr_ref, si_ref, u_ref, us_ref, st_ref, cr_ref, ci_ref, *, chain, blocks_per_seq):
    rb = pl.program_id(0)
    m = pl.program_id(1)
    rows = x_ref.shape[0]
    n_tiles = rows // SUBLANES

    @pl.when(m == 0)
    def _():
        u = _rmsnorm_rows(x_ref[...], g_ref[...])
        for mm in range(N_CHUNKS):
            uc = u[:, mm * LANES:(mm + 1) * LANES]
            u_ref[mm] = uc
            hi, lo = _split_bf16(uc)
            us_ref[mm, :, 0:LANES] = hi
            us_ref[mm, :, LANES:2 * LANES] = lo

    u = u_ref[m]
    us = us_ref[m]
    st_ref[...] = (jnp.dot(us, bhh_ref[0], preferred_element_type=jnp.float32)
                   + jnp.dot(us[:, 0:LANES], blo_ref[0], preferred_element_type=jnp.float32))

    apr = apr_ref[0]
    api = api_ref[0]
    row = lax.broadcasted_iota(jnp.int32, (SUBLANES, CHUNK_STATE), 0)

    if chain:
        @pl.when(rb % blocks_per_seq == 0)
        def _():
            cr_ref[m] = h0r_ref[0, 0]
            ci_ref[m] = h0i_ref[0, 0]

    def tile_step(k, carry):
        r0 = pl.multiple_of(k * SUBLANES, SUBLANES)
        xr = st_ref[pl.ds(r0, SUBLANES), 0:CHUNK_STATE]
        xi = st_ref[pl.ds(r0, SUBLANES), CHUNK_STATE:2 * CHUNK_STATE]
        for d in (1, 2, 4):
            ar = apr[d - 1:d, :]
            ai = api[d - 1:d, :]
            sr = jnp.where(row >= d, pltpu.roll(xr, d, axis=0), 0.0)
            si = jnp.where(row >= d, pltpu.roll(xi, d, axis=0), 0.0)
            xr, xi = xr + ar * sr - ai * si, xi + ar * si + ai * sr
        if chain:
            cr, ci = carry
        else:
            cr = h0r_ref[k, 0]
            ci = h0i_ref[k, 0]
        hr = xr + apr * cr - api * ci
        hi = xi + apr * ci + api * cr
        st_ref[pl.ds(r0, SUBLANES), 0:CHUNK_STATE] = hr
        st_ref[pl.ds(r0, SUBLANES), CHUNK_STATE:2 * CHUNK_STATE] = hi
        lr_ = hr[SUBLANES - 1:SUBLANES, :]
        li_ = hi[SUBLANES - 1:SUBLANES, :]
        if chain:
            return lr_, li_
        sr_ref[k, m] = lr_
        si_ref[k, m] = li_
        return carry

    if chain:
        cr, ci = lax.fori_loop(0, n_tiles, tile_step, (cr_ref[m], ci_ref[m]))
        cr_ref[m] = cr
        ci_ref[m] = ci
        sr_ref[0, m] = cr
        si_ref[0, m] = ci
    else:
        lax.fori_loop(0, n_tiles, tile_step, 0)

    y = jnp.dot(st_ref[...].astype(jnp.bfloat16), c_ref[0], preferred_element_type=jnp.float32)
    y = y + d_ref[0] * u
    z_ref[...] = _gelu(y)


def _s5_mixer(x, h0r, h0i, g, d_skip, bmat, cmat_bf16, apr, api, *, chain, seq_len):
    T = x.shape[0]
    nseq = h0r.shape[0]
    rows = S5_ROWS
    if chain:
        blocks_per_seq = seq_len // rows
        seq_blk = 1
        seq_map = lambda rb, m: (rb // blocks_per_seq, m, 0, 0)
        out_map = lambda rb, m: (rb // blocks_per_seq, 0, 0, 0)
    else:
        assert seq_len == SUBLANES
        blocks_per_seq = 1
        seq_blk = rows // SUBLANES
        seq_map = lambda rb, m: (rb, m, 0, 0)
        out_map = lambda rb, m: (rb, 0, 0, 0)
    grid = (T // rows, N_CHUNKS)
    kern = functools.partial(_s5_kernel, chain=chain, blocks_per_seq=blocks_per_seq)
    st_spec = pl.BlockSpec((seq_blk, 1, 1, CHUNK_STATE), seq_map)
    out_st_spec = pl.BlockSpec((seq_blk, N_CHUNKS, 1, CHUNK_STATE), out_map)
    z, sr, si = pl.pallas_call(
        kern,
        grid=grid,
        in_specs=[
            pl.BlockSpec((rows, D_MODEL), lambda rb, m: (rb, 0)),
            pl.BlockSpec((1, D_MODEL), lambda rb, m: (0, 0)),
            pl.BlockSpec((1, 1, LANES), lambda rb, m: (m, 0, 0)),
            pl.BlockSpec((1, 2 * LANES, 2 * CHUNK_STATE), lambda rb, m: (m, 0, 0)),
            pl.BlockSpec((1, LANES, 2 * CHUNK_STATE), lambda rb, m: (m, 0, 0)),
            pl.BlockSpec((1, 2 * CHUNK_STATE, LANES), lambda rb, m: (m, 0, 0)),
            pl.BlockSpec((1, SUBLANES, CHUNK_STATE), lambda rb, m: (m, 0, 0)),
            pl.BlockSpec((1, SUBLANES, CHUNK_STATE), lambda rb, m: (m, 0, 0)),
            st_spec, st_spec,
        ],
        out_specs=[
            pl.BlockSpec((rows, LANES), lambda rb, m: (rb, m)),
            out_st_spec, out_st_spec,
        ],
        out_shape=[
            jax.ShapeDtypeStruct((T, D_MODEL), jnp.float32),
            jax.ShapeDtypeStruct((nseq, N_CHUNKS, 1, CHUNK_STATE), jnp.float32),
            jax.ShapeDtypeStruct((nseq, N_CHUNKS, 1, CHUNK_STATE), jnp.float32),
        ],
        scratch_shapes=[
            pltpu.VMEM((N_CHUNKS, rows, LANES), jnp.float32),
            pltpu.VMEM((N_CHUNKS, rows, 2 * LANES), jnp.bfloat16),
            pltpu.VMEM((rows, 2 * CHUNK_STATE), jnp.float32),
            pltpu.VMEM((N_CHUNKS, 1, CHUNK_STATE), jnp.float32),
            pltpu.VMEM((N_CHUNKS, 1, CHUNK_STATE), jnp.float32),
        ],
        compiler_params=pltpu.CompilerParams(
            dimension_semantics=("arbitrary", "arbitrary"), vmem_limit_bytes=VMEM_LIMIT_BYTES),
        name="s5_mixer",
    )(x, g.reshape(1, D_MODEL), d_skip.reshape(N_CHUNKS, 1, LANES), bmat[0], bmat[1], cmat_bf16, apr, api,
      h0r, h0i)
    return z, sr, si


def _state_to_chunks(h):
    return h.reshape(h.shape[0], N_CHUNKS, 1, CHUNK_STATE)


def _chunks_to_state(s):
    return s.reshape(s.shape[0], N_GROUPS, STATE_DIM)


def _glu_kernel(z_ref, res_ref, w_ref, o_ref):
    zz = jnp.dot(z_ref[...].astype(jnp.bfloat16), w_ref[...], preferred_element_type=jnp.float32)
    a = zz[:, :D_MODEL]
    b = zz[:, D_MODEL:]
    o_ref[...] = res_ref[...] + a * (1.0 / (1.0 + jnp.exp(-b)))


def _glu_residual(z, res, w_bf16):
    T = z.shape[0]
    rows = math.gcd(T, 512)
    return pl.pallas_call(
        _glu_kernel,
        grid=(T // rows,),
        in_specs=[
            pl.BlockSpec((rows, D_MODEL), lambda i: (i, 0)),
            pl.BlockSpec((rows, D_MODEL), lambda i: (i, 0)),
            pl.BlockSpec((D_MODEL, 2 * D_MODEL), lambda i: (0, 0)),
        ],
        out_specs=pl.BlockSpec((rows, D_MODEL), lambda i: (i, 0)),
        out_shape=jax.ShapeDtypeStruct((T, D_MODEL), jnp.float32),
        compiler_params=pltpu.CompilerParams(
            dimension_semantics=("arbitrary",), vmem_limit_bytes=VMEM_LIMIT_BYTES),
        name="glu_residual",
    )(z, res, w_bf16)


ROUTE_ROWS = 256
CAND = PEER_TOPK * PEER_TOPK


def _topk_rows(s, payload):
    n_rows = s.shape[0]
    row = lax.broadcasted_iota(jnp.int32, s.shape, 0)
    vals, picks = [], []
    for _ in range(PEER_TOPK):
        m = jnp.max(s, axis=0, keepdims=True)
        pos = jnp.min(jnp.where(s == m, row, n_rows), axis=0, keepdims=True)
        sel = row == pos
        vals.append(m)
        if payload is None:
            picks.append(pos)
        else:
            picks.append(jnp.max(jnp.where(sel, payload, -1), axis=0, keepdims=True))
        s = jnp.where(sel, -jnp.inf, s)
    return jnp.concatenate(vals, axis=0), jnp.concatenate(picks, axis=0)


def _pair_rows(a0, a1, combine):
    half = PEER_TOPK // 2
    parts = [combine(a0[0:1, :], a1)]
    parts += [combine(a0[i:i + 1, :], a1[0:half, :]) for i in range(1, half)]
    parts.append(combine(a0[half:PEER_TOPK, :], a1[0:1, :]))
    return jnp.concatenate(parts, axis=0)


def _peer_route_kernel(h_ref, g_ref, wq_ref, sk_ref, xn_ref, eid_ref, gate_ref, xb_ref):
    hd = pl.program_id(1)

    @pl.when(hd == 0)
    def _():
        xn = _rmsnorm_rows(h_ref[...], g_ref[...])
        xn_ref[...] = xn
        xb_ref[...] = xn.astype(jnp.bfloat16)

    q = jnp.dot(xb_ref[...], wq_ref[...], preferred_element_type=jnp.float32)
    sv, si = [], []
    for c in range(2):
        qc = q[:, c * PEER_DHALF:(c + 1) * PEER_DHALF].astype(jnp.bfloat16)
        st = lax.dot_general(sk_ref[0, c], qc, (((1,), (1,)), ((), ())), preferred_element_type=jnp.float32)
        v, i = _topk_rows(st, None)
        sv.append(v)
        si.append(i)
    cand = _pair_rows(sv[0], sv[1], lambda a, b: a + b)
    cid = _pair_rows(si[0], si[1], lambda a, b: a * N_KEYS + b)
    fv, eid = _topk_rows(cand, cid)
    e = jnp.exp(fv - fv[0:1, :])
    gate_ref[...] = e / jnp.sum(e, axis=0, keepdims=True)
    eid_ref[...] = eid


def _peer_route(h, g, wq_bf16, sk_bf16):
    T = h.shape[0]
    rows = ROUTE_ROWS
    n_sel = PEER_HEADS * PEER_TOPK
    return pl.pallas_call(
        _peer_route_kernel,
        grid=(T // rows, PEER_HEADS),
        in_specs=[
            pl.BlockSpec((rows, D_MODEL), lambda tb, hd: (tb, 0)),
            pl.BlockSpec((1, D_MODEL), lambda tb, hd: (0, 0)),
            pl.BlockSpec((D_MODEL, 2 * PEER_DHALF), lambda tb, hd: (0, hd)),
            pl.BlockSpec((1, 2, N_KEYS, PEER_DHALF), lambda tb, hd: (hd, 0, 0, 0)),
        ],
        out_specs=[
            pl.BlockSpec((rows, D_MODEL), lambda tb, hd: (tb, 0)),
            pl.BlockSpec((PEER_TOPK, rows), lambda tb, hd: (hd, tb)),
            pl.BlockSpec((PEER_TOPK, rows), lambda tb, hd: (hd, tb)),
        ],
        out_shape=[
            jax.ShapeDtypeStruct((T, D_MODEL), jnp.float32),
            jax.ShapeDtypeStruct((n_sel, T), jnp.int32),
            jax.ShapeDtypeStruct((n_sel, T), jnp.float32),
        ],
        scratch_shapes=[pltpu.VMEM((rows, D_MODEL), jnp.bfloat16)],
        compiler_params=pltpu.CompilerParams(
            dimension_semantics=("arbitrary", "arbitrary"), vmem_limit_bytes=VMEM_LIMIT_BYTES),
        name="peer_route",
    )(h, g.reshape(1, D_MODEL), wq_bf16, sk_bf16)


N_SEL = PEER_HEADS * PEER_TOPK
SC_LANES = 16
GATHER_ROWS = PEER_TOPK
GATHERS_PER_TOKEN = N_SEL // GATHER_ROWS
GATHER_BUFS = 4
SC_TOKENS = 8


def _sc_gather_stream(wid, n_batches, tab_hbm, stage_srcs, stage_bufs, out_hbm, o_v, bufs, gsem, ssem, osem,
                      compute):
    idx_v = stage_bufs[0]

    def stage_copies(bi, slot):
        base = (wid * n_batches + bi) * SC_TOKENS
        return [pltpu.make_async_copy(src.at[pl.ds(base, SC_TOKENS)], buf.at[slot], ssem.at[slot])
                for src, buf in zip(stage_srcs, stage_bufs)]

    def out_copy(bi, slot):
        base = (wid * n_batches + bi) * SC_TOKENS
        return pltpu.make_async_copy(o_v.at[slot], out_hbm.at[pl.ds(base, SC_TOKENS)], osem.at[slot])

    def start(slot, t, kk, b):
        idx = idx_v[slot, t, pl.ds(kk * GATHER_ROWS, GATHER_ROWS)]
        pltpu.async_copy(tab_hbm.at[idx], bufs.at[b], gsem.at[b])

    def wait(b):
        pltpu.make_async_copy(tab_hbm.at[pl.ds(0, GATHER_ROWS)], bufs.at[b], gsem.at[b]).wait()

    for c in stage_copies(0, 0):
        c.start()
    for c in stage_copies(0, 0):
        c.wait()
    for q in range(GATHER_BUFS - 1):
        start(0, q // GATHERS_PER_TOKEN, q % GATHERS_PER_TOKEN, q % GATHER_BUFS)

    def batch(bi, carry):
        slot = bi % 2
        has_next = bi + 1 < n_batches

        @pl.when(has_next)
        def _():
            for c in stage_copies(bi + 1, 1 - slot):
                c.start()

        @pl.when(bi >= 2)
        def _():
            out_copy(bi, slot).wait()

        def tok(t, carry):
            @pl.when(jnp.logical_and(t == SC_TOKENS - 1, has_next))
            def _():
                for c in stage_copies(bi + 1, 1 - slot):
                    c.wait()

            for kk in range(GATHERS_PER_TOKEN):
                nq = kk + GATHER_BUFS - 1
                nk, nb = nq % GATHERS_PER_TOKEN, nq % GATHER_BUFS
                if nq < GATHERS_PER_TOKEN:
                    start(slot, t, nk, nb)
                else:
                    @pl.when(t + 1 < SC_TOKENS)
                    def _():
                        start(slot, t + 1, nk, nb)

                    @pl.when(jnp.logical_and(t + 1 == SC_TOKENS, has_next))
                    def _():
                        start(1 - slot, 0, nk, nb)

                wait(kk % GATHER_BUFS)
                compute(slot, t, kk, kk % GATHER_BUFS)
            return carry

        lax.fori_loop(0, SC_TOKENS, tok, 0)
        out_copy(bi, slot).start()
        return carry

    lax.fori_loop(0, n_batches, batch, 0)
    if n_batches >= 2:
        out_copy(n_batches - 2, (n_batches - 2) % 2).wait()
    out_copy(n_batches - 1, (n_batches - 1) % 2).wait()


def _sc_mesh_and_batches(n_tokens):
    info = plsc.get_sparse_core_info()
    assert info.num_lanes == SC_LANES
    n_workers = info.num_cores * info.num_subcores
    assert n_tokens % (n_workers * SC_TOKENS) == 0
    mesh = plsc.VectorSubcoreMesh(core_axis_name="c", subcore_axis_name="s")
    return info, mesh, n_tokens // (n_workers * SC_TOKENS)


def _peer_hidden_sc(eid, xn, u_tab):
    T = eid.shape[0]
    info, mesh, n_batches = _sc_mesh_and_batches(T)

    @functools.partial(
        pl.kernel, mesh=mesh,
        out_type=jax.ShapeDtypeStruct((T, N_SEL), jnp.float32),
        scratch_types=[
            pltpu.VMEM((2, SC_TOKENS, N_SEL), jnp.int32),
            pltpu.VMEM((2, SC_TOKENS, D_MODEL), jnp.float32),
            pltpu.VMEM((2, SC_TOKENS, N_SEL), jnp.float32),
            pltpu.VMEM((GATHER_BUFS, GATHER_ROWS, D_MODEL), jnp.float32),
            pltpu.VMEM((GATHER_ROWS, SC_LANES), jnp.float32),
            pltpu.SemaphoreType.DMA((GATHER_BUFS,)),
            pltpu.SemaphoreType.DMA((2,)),
            pltpu.SemaphoreType.DMA((2,)),
        ],
        compiler_params=pltpu.CompilerParams(needs_layout_passes=False),
        name="peer_hidden_sc",
    )
    def k(eid_hbm, xn_hbm, u_hbm, out_hbm, idx_v, x_v, o_v, bufs, acc_v, gsem, ssem, osem):
        wid = lax.axis_index("s") * info.num_cores + lax.axis_index("c")
        lane = lax.iota(jnp.int32, SC_LANES)
        zero = jnp.zeros((SC_LANES,), jnp.float32)

        def compute(slot, t, kk, b):
            @plsc.parallel_loop(0, D_MODEL // SC_LANES, carry=(zero,) * GATHER_ROWS)
            def accs(c, accs):
                xc = x_v[slot, t, pl.ds(c * SC_LANES, SC_LANES)]
                return tuple(accs[r] + bufs[b, r, pl.ds(c * SC_LANES, SC_LANES)] * xc for r in range(GATHER_ROWS))

            for r in range(GATHER_ROWS):
                acc_v[r, :] = accs[r]
            tot = zero
            for c in range(SC_LANES):
                tot = tot + plsc.load_gather(acc_v, [lane, jnp.full((SC_LANES,), c, jnp.int32)])
            o_v[slot, t, pl.ds(kk * GATHER_ROWS, GATHER_ROWS)] = tot

        _sc_gather_stream(wid, n_batches, u_hbm, [eid_hbm, xn_hbm], [idx_v, x_v], out_hbm, o_v, bufs,
                          gsem, ssem, osem, compute)

    return k(eid, xn, u_tab)


def _peer_combine_sc(eid, a, v_tab):
    T = eid.shape[0]
    info, mesh, n_batches = _sc_mesh_and_batches(T)

    @functools.partial(
        pl.kernel, mesh=mesh,
        out_type=jax.ShapeDtypeStruct((T, D_MODEL), jnp.float32),
        scratch_types=[
            pltpu.VMEM((2, SC_TOKENS, N_SEL), jnp.int32),
            pltpu.VMEM((2, SC_TOKENS, N_SEL), jnp.float32),
            pltpu.VMEM((2, SC_TOKENS, D_MODEL), jnp.float32),
            pltpu.VMEM((GATHER_BUFS, GATHER_ROWS, D_MODEL), jnp.float32),
            pltpu.SemaphoreType.DMA((GATHER_BUFS,)),
            pltpu.SemaphoreType.DMA((2,)),
            pltpu.SemaphoreType.DMA((2,)),
        ],
        compiler_params=pltpu.CompilerParams(needs_layout_passes=False),
        name="peer_combine_sc",
    )
    def k(eid_hbm, a_hbm, v_hbm, out_hbm, idx_v, a_v, o_v, bufs, gsem, ssem, osem):
        wid = lax.axis_index("s") * info.num_cores + lax.axis_index("c")

        def compute(slot, t, kk, b):
            svec = jnp.full((SC_LANES,), slot, jnp.int32)
            tvec = jnp.full((SC_LANES,), t, jnp.int32)
            ws = [plsc.load_gather(a_v, [svec, tvec, jnp.full((SC_LANES,), kk * GATHER_ROWS + r, jnp.int32)])
                  for r in range(GATHER_ROWS)]

            @plsc.parallel_loop(0, D_MODEL // SC_LANES, unroll=2)
            def _(c):
                sl = pl.ds(c * SC_LANES, SC_LANES)
                terms = [ws[r] * bufs[b, r, sl] for r in range(GATHER_ROWS)]
                if kk != 0:
                    terms.append(o_v[slot, t, sl])
                while len(terms) > 1:
                    pairs = [terms[i] + terms[i + 1] for i in range(0, len(terms) - 1, 2)]
                    terms = pairs + ([terms[-1]] if len(terms) % 2 else [])
                o_v[slot, t, sl] = terms[0]

        _sc_gather_stream(wid, n_batches, v_hbm, [eid_hbm, a_hbm], [idx_v, a_v], out_hbm, o_v, bufs,
                          gsem, ssem, osem, compute)

    return k(eid, a, v_tab)


def _peer_act_kernel(hp_ref, gate_ref, a_ref):
    a_ref[...] = _gelu(hp_ref[...]) * gate_ref[...]


ELEMENTWISE_ROWS = 1088
FIRST_PEER_ROWS = 1024


def _row_block(n_rows, cap):
    return max(r for r in range(SUBLANES, cap + 1, SUBLANES) if n_rows % r == 0)


def _peer_act(hpre, gate):
    T = hpre.shape[0]
    rows = _row_block(T, ELEMENTWISE_ROWS)
    spec = pl.BlockSpec((rows, N_SEL), lambda i: (i, 0))
    return pl.pallas_call(
        _peer_act_kernel, grid=(T // rows,), in_specs=[spec, spec], out_specs=spec,
        out_shape=jax.ShapeDtypeStruct((T, N_SEL), jnp.float32),
        compiler_params=pltpu.CompilerParams(dimension_semantics=("arbitrary",)),
        name="peer_act",
    )(hpre, gate)


def _residual_kernel(h_ref, c_ref, g_ref, o_ref, *, final_norm):
    y = h_ref[...] + c_ref[...]
    o_ref[...] = _rmsnorm_rows(y, g_ref[...]) if final_norm else y


def _residual(h, c, gfin, *, final_norm):
    T = h.shape[0]
    rows = _row_block(T, ELEMENTWISE_ROWS)
    spec = pl.BlockSpec((rows, D_MODEL), lambda i: (i, 0))
    return pl.pallas_call(
        functools.partial(_residual_kernel, final_norm=final_norm), grid=(T // rows,),
        in_specs=[spec, spec, pl.BlockSpec((1, D_MODEL), lambda i: (0, 0))], out_specs=spec,
        out_shape=jax.ShapeDtypeStruct((T, D_MODEL), jnp.float32),
        compiler_params=pltpu.CompilerParams(dimension_semantics=("arbitrary",)),
        name="peer_residual",
    )(h, c, gfin.reshape(1, D_MODEL))


def _peer_experts(eid, xn, gate, h, gfin, u_tab, v_tab, *, final_norm):
    hpre = _peer_hidden_sc(eid, xn, u_tab)
    a = _peer_act(hpre, gate)
    c = _peer_combine_sc(eid, a, v_tab)
    return _residual(h, c, gfin, final_norm=final_norm)


KV_WIDTH = N_KV_HEADS * HEAD_DIM
BF16 = jnp.bfloat16


def _qkv(h, gkv, gq, wkv_ref, wq_ref):
    kv = jnp.dot(_rmsnorm_rows(h, gkv).astype(BF16), wkv_ref[...], preferred_element_type=jnp.float32)
    q = jnp.dot(_rmsnorm_rows(h, gq).astype(BF16), wq_ref[...], preferred_element_type=jnp.float32)
    return kv, q


def _sink_softmax_pv(parts, sink):
    m = sink
    for s, _ in parts:
        m = jnp.maximum(m, jnp.max(s, axis=-1, keepdims=True))
    den = jnp.exp(sink - m)
    acc = None
    for s, v in parts:
        e = jnp.exp(s - m)
        den = den + jnp.sum(e, axis=-1, keepdims=True)
        pv = jnp.dot(e.astype(BF16), v.astype(BF16), preferred_element_type=jnp.float32)
        acc = pv if acc is None else acc + pv
    return acc / den


def _nt_dot(a, b):
    return lax.dot_general(a.astype(BF16), b.astype(BF16), (((1,), (1,)), ((), ())),
                           preferred_element_type=jnp.float32)


def _attn_prompt_kernel(h_ref, gkv_ref, gq_ref, wkv_ref, wq_ref, wo_ref, sink_ref,
                        o_ref, kvw_ref, prev_ref):
    blk = pl.program_id(1)
    h = h_ref[...]
    kv, q = _qkv(h, gkv_ref[...], gq_ref[...], wkv_ref, wq_ref)
    kvw_ref[0] = kv

    @pl.when(blk == 0)
    def _():
        prev_ref[...] = jnp.zeros_like(prev_ref)

    prev = prev_ref[...]
    qi = lax.broadcasted_iota(jnp.int32, (WINDOW, WINDOW), 0)
    kj = lax.broadcasted_iota(jnp.int32, (WINDOW, WINDOW), 1)
    prev_ok = jnp.logical_and(kj > qi, blk > 0)
    cur_ok = kj <= qi
    heads = []
    for kvh in range(N_KV_HEADS):
        ks = slice(kvh * HEAD_DIM, (kvh + 1) * HEAD_DIM)
        vs = slice(KV_WIDTH + kvh * HEAD_DIM, KV_WIDTH + (kvh + 1) * HEAD_DIM)
        for g in range(Q_PER_KV):
            hq = kvh * Q_PER_KV + g
            qh = q[:, hq * HEAD_DIM:(hq + 1) * HEAD_DIM]
            sp = jnp.where(prev_ok, _nt_dot(qh, prev[:, ks]) * ATTN_SCALE, -jnp.inf)
            sc = jnp.where(cur_ok, _nt_dot(qh, kv[:, ks]) * ATTN_SCALE, -jnp.inf)
            heads.append(_sink_softmax_pv([(sp, prev[:, vs]), (sc, kv[:, vs])], sink_ref[hq]))
    o = jnp.concatenate(heads, axis=1)
    o_ref[...] = h + jnp.dot(o.astype(BF16), wo_ref[...], preferred_element_type=jnp.float32)
    prev_ref[...] = kv


def _attn_prompt(h, gkv, gq, wkv, wq, wo, sinks, *, n_seq, seq_len):
    nb = seq_len // WINDOW
    row_spec = pl.BlockSpec((WINDOW, D_MODEL), lambda n, b: (n * nb + b, 0))
    full = lambda shape: pl.BlockSpec(shape, lambda n, b: (0,) * len(shape))
    return pl.pallas_call(
        _attn_prompt_kernel,
        grid=(n_seq, nb),
        in_specs=[
            row_spec, full((1, D_MODEL)), full((1, D_MODEL)),
            full((D_MODEL, 2 * KV_WIDTH)), full((D_MODEL, D_MODEL)), full((D_MODEL, D_MODEL)),
            pl.BlockSpec(memory_space=pltpu.SMEM),
        ],
        out_specs=[row_spec, pl.BlockSpec((1, WINDOW, 2 * KV_WIDTH), lambda n, b: (n, 0, 0))],
        out_shape=[
            jax.ShapeDtypeStruct(h.shape, jnp.float32),
            jax.ShapeDtypeStruct((n_seq, WINDOW, 2 * KV_WIDTH), jnp.float32),
        ],
        scratch_shapes=[pltpu.VMEM((WINDOW, 2 * KV_WIDTH), jnp.float32)],
        compiler_params=pltpu.CompilerParams(
            dimension_semantics=("arbitrary", "arbitrary"), vmem_limit_bytes=VMEM_LIMIT_BYTES),
        name="attn_prompt",
    )(h, gkv.reshape(1, D_MODEL), gq.reshape(1, D_MODEL), wkv, wq, wo, sinks)


ATTN_SEQS = 16


def _attn_sample_kernel(h_ref, ck_ref, cv_ref, gkv_ref, gq_ref, wkv_ref, wq_ref, wo_ref, sink_ref,
                        o_ref, kw_ref, vw_ref, kv_ref, q_ref, att_ref, *, n_new):
    h = h_ref[...]
    kv, q = _qkv(h, gkv_ref[...], gq_ref[...], wkv_ref, wq_ref)
    kv_ref[...] = kv
    q_ref[...] = q
    n_seq = h.shape[0] // n_new
    rows = Q_PER_KV * n_new
    qpos_c = lax.broadcasted_iota(jnp.int32, (rows, WINDOW), 0) % n_new
    cache_ok = lax.broadcasted_iota(jnp.int32, (rows, WINDOW), 1) > qpos_c
    qpos_n = lax.broadcasted_iota(jnp.int32, (rows, n_new), 0) % n_new
    new_ok = lax.broadcasted_iota(jnp.int32, (rows, n_new), 1) <= qpos_n

    def per_seq(n, carry):
        r0 = pl.multiple_of(n * n_new, n_new)
        kvn = kv_ref[pl.ds(r0, n_new), :]
        qn = q_ref[pl.ds(r0, n_new), :]
        ck = ck_ref[n]
        cv = cv_ref[n]
        kw_ref[n, 0:WINDOW - n_new, :] = ck[n_new:, :]
        kw_ref[n, WINDOW - n_new:WINDOW, :] = kvn[:, 0:KV_WIDTH]
        vw_ref[n, 0:WINDOW - n_new, :] = cv[n_new:, :]
        vw_ref[n, WINDOW - n_new:WINDOW, :] = kvn[:, KV_WIDTH:2 * KV_WIDTH]
        outs = []
        for kvh in range(N_KV_HEADS):
            ks = slice(kvh * HEAD_DIM, (kvh + 1) * HEAD_DIM)
            vs = slice(KV_WIDTH + kvh * HEAD_DIM, KV_WIDTH + (kvh + 1) * HEAD_DIM)
            qs = jnp.concatenate(
                [qn[:, (kvh * Q_PER_KV + g) * HEAD_DIM:(kvh * Q_PER_KV + g + 1) * HEAD_DIM] for g in range(Q_PER_KV)],
                axis=0)
            s_c = jnp.where(cache_ok, _nt_dot(qs, ck[:, ks]) * ATTN_SCALE, -jnp.inf)
            s_n = jnp.where(new_ok, _nt_dot(qs, kvn[:, ks]) * ATTN_SCALE, -jnp.inf)
            sink = sink_ref[kvh][:, 0:1]
            o = _sink_softmax_pv([(s_c, cv[:, ks]), (s_n, kvn[:, vs])], sink)
            outs += [o[g * n_new:(g + 1) * n_new, :] for g in range(Q_PER_KV)]
        att_ref[pl.ds(r0, n_new), :] = jnp.concatenate(outs, axis=1)
        return carry

    lax.fori_loop(0, n_seq, per_seq, 0)
    o_ref[...] = h + jnp.dot(att_ref[...].astype(BF16), wo_ref[...], preferred_element_type=jnp.float32)


def _attn_sample(h, cache_k, cache_v, gkv, gq, wkv, wq, wo, sinks, *, n_new):
    n_seq = cache_k.shape[0]
    sb = ATTN_SEQS
    rows = sb * n_new
    row_spec = pl.BlockSpec((rows, D_MODEL), lambda i: (i, 0))
    win_spec = pl.BlockSpec((sb, WINDOW, KV_WIDTH), lambda i: (i, 0, 0))
    full = lambda shape: pl.BlockSpec(shape, lambda i: (0,) * len(shape))
    sink_rows = jnp.repeat(sinks.reshape(N_KV_HEADS, Q_PER_KV), n_new, axis=1)[:, :, None]
    sink_rows = jnp.broadcast_to(sink_rows, (N_KV_HEADS, Q_PER_KV * n_new, LANES))
    return pl.pallas_call(
        functools.partial(_attn_sample_kernel, n_new=n_new),
        grid=(n_seq // sb,),
        in_specs=[
            row_spec, win_spec, win_spec, full((1, D_MODEL)), full((1, D_MODEL)),
            full((D_MODEL, 2 * KV_WIDTH)), full((D_MODEL, D_MODEL)), full((D_MODEL, D_MODEL)),
            full((N_KV_HEADS, Q_PER_KV * n_new, LANES)),
        ],
        out_specs=[row_spec, win_spec, win_spec],
        out_shape=[
            jax.ShapeDtypeStruct(h.shape, jnp.float32),
            jax.ShapeDtypeStruct((n_seq, WINDOW, KV_WIDTH), jnp.float32),
            jax.ShapeDtypeStruct((n_seq, WINDOW, KV_WIDTH), jnp.float32),
        ],
        scratch_shapes=[
            pltpu.VMEM((rows, 2 * KV_WIDTH), jnp.float32),
            pltpu.VMEM((rows, D_MODEL), jnp.float32),
            pltpu.VMEM((rows, D_MODEL), jnp.float32),
        ],
        compiler_params=pltpu.CompilerParams(
            dimension_semantics=("arbitrary",), vmem_limit_bytes=VMEM_LIMIT_BYTES),
        name="attn_sample",
    )(h, cache_k, cache_v, gkv.reshape(1, D_MODEL), gq.reshape(1, D_MODEL), wkv, wq, wo, sink_rows)


def kernel(x_prompt, x_sample, state_ssm_re, state_ssm_im, cache_k_win, cache_v_win, norm_mix, norm_ffn, norm_kv, norm_final, ssm_lam_re, ssm_lam_im, ssm_log_dt, ssm_b_re, ssm_b_im, ssm_c_re, ssm_c_im, ssm_d, ssm_w_glu, w_kv, w_q, attn_sinks, w_o, peer_w_q, peer_sub_keys, peer_u, peer_v):
    bmat, cmat, apr, api = _s5_discretize(ssm_lam_re[0], ssm_lam_im[0], ssm_log_dt[0], ssm_b_re[0], ssm_b_im[0], ssm_c_re[0], ssm_c_im[0])
    cmat = cmat.astype(jnp.bfloat16)
    wglu = ssm_w_glu[0].astype(jnp.bfloat16)
    wkv = w_kv.astype(BF16)
    wq = w_q[0].astype(BF16)
    wo = w_o[0].astype(BF16)
    peer_wq = [peer_w_q[layer].astype(BF16) for layer in range(2)]
    peer_sk = [peer_sub_keys[layer].astype(BF16) for layer in range(2)]

    def peer(h, layer, final_norm, head_rows=0):
        def route(rows):
            return _peer_route(rows, norm_ffn[layer], peer_wq[layer], peer_sk[layer])

        def experts(rows, routed):
            xn, eid_t, gate_t = routed
            return _peer_experts(eid_t.T, xn, gate_t.T, rows, norm_final, peer_u[layer], peer_v[layer],
                                 final_norm=final_norm)

        if not head_rows:
            return experts(h, route(h))
        h_head, h_tail = h[:head_rows], h[head_rows:]
        routed_head = route(h_head)
        routed_head, h_tail = lax.optimization_barrier((routed_head, h_tail))
        return jnp.concatenate([experts(h_head, routed_head), experts(h_tail, route(h_tail))], axis=0)

    def trunk(x_p, x_s, s_re, s_im, c_k, c_v):
        n_p, n_s = x_p.shape[0], x_s.shape[0]
        xp = x_p.reshape(-1, D_MODEL)
        xs = x_s.reshape(-1, D_MODEL)
        n_prompt_rows = xp.shape[0]
        z0 = jnp.zeros((n_p, N_CHUNKS, 1, CHUNK_STATE), jnp.float32)
        zp, srp, sip = _s5_mixer(xp, z0, z0, norm_mix[0], ssm_d[0], bmat, cmat, apr, api,
                                 chain=True, seq_len=x_p.shape[1])
        zs, srs, sis = _s5_mixer(xs, _state_to_chunks(s_re), _state_to_chunks(s_im), norm_mix[0], ssm_d[0],
                                 bmat, cmat, apr, api, chain=False, seq_len=x_s.shape[1])
        h1 = jnp.concatenate([_glu_residual(zp, xp, wglu), _glu_residual(zs, xs, wglu)], axis=0)
        h2 = peer(h1, 0, False, head_rows=FIRST_PEER_ROWS)
        h3p, kvw_p = _attn_prompt(h2[:n_prompt_rows], norm_kv, norm_mix[1], wkv, wq, wo, attn_sinks[0],
                                  n_seq=n_p, seq_len=x_p.shape[1])
        h3s, kw_s, vw_s = _attn_sample(h2[n_prompt_rows:], c_k.reshape(n_s, WINDOW, KV_WIDTH),
                                       c_v.reshape(n_s, WINDOW, KV_WIDTH), norm_kv, norm_mix[1],
                                       wkv, wq, wo, attn_sinks[0], n_new=x_s.shape[1])
        y = peer(jnp.concatenate([h3p, h3s], axis=0), 1, True)
        win = lambda a: a.reshape(a.shape[0], WINDOW, N_KV_HEADS, HEAD_DIM)
        return (y[:n_prompt_rows].reshape(x_p.shape), y[n_prompt_rows:].reshape(x_s.shape),
                _chunks_to_state(srp), _chunks_to_state(sip),
                win(kvw_p[:, :, :KV_WIDTH]), win(kvw_p[:, :, KV_WIDTH:]),
                _chunks_to_state(srs), _chunks_to_state(sis), win(kw_s), win(vw_s))

    n_groups = 4
    gp = x_prompt.shape[0] // n_groups
    gs = x_sample.shape[0] // n_groups
    outs = [trunk(x_prompt[i * gp:(i + 1) * gp], x_sample[i * gs:(i + 1) * gs],
                  state_ssm_re[0, i * gs:(i + 1) * gs], state_ssm_im[0, i * gs:(i + 1) * gs],
                  cache_k_win[i * gs:(i + 1) * gs], cache_v_win[i * gs:(i + 1) * gs])
            for i in range(n_groups)]
    cat = [jnp.concatenate(parts, axis=0) for parts in zip(*outs)]
    return (cat[0], cat[1], cat[2][None], cat[3][None], cat[4], cat[5], cat[6][None], cat[7][None],
            cat[8], cat[9])
```

```python
import functools
import math

import jax
import jax.numpy as jnp
from jax import lax
from jax.experimental import pallas as pl
from jax.experimental.pallas import tpu as pltpu
from jax.experimental.pallas import tpu_sc as plsc

D_MODEL = 1024
GROUP_SIZE = 16
N_GROUPS = D_MODEL // GROUP_SIZE
STATE_DIM = 64
HEAD_DIM = 64
N_Q_HEADS = D_MODEL // HEAD_DIM
N_KV_HEADS = N_Q_HEADS // 8
Q_PER_KV = N_Q_HEADS // N_KV_HEADS
WINDOW = 128
ATTN_SCALE = 1.0 / math.sqrt(HEAD_DIM)
PEER_HEADS = 8
N_KEYS = 128
PEER_TOPK = 16
PEER_DHALF = 128
EPS = 1e-5

LANES = 128
SUBLANES = 8
VMEM_LIMIT_BYTES = 56 * 1024 * 1024

GROUPS_PER_CHUNK = LANES // GROUP_SIZE
N_CHUNKS = N_GROUPS // GROUPS_PER_CHUNK
CHUNK_STATE = GROUPS_PER_CHUNK * STATE_DIM
S5_ROWS = 256


def _rmsnorm_rows(x, g):
    r = lax.rsqrt(jnp.mean(x * x, axis=-1, keepdims=True) + EPS)
    return x * r * g


def _gelu(x):
    return 0.5 * x * (1.0 + lax.erf(x * (1.0 / math.sqrt(2.0))))


def _s5_discretize(lam_re, lam_im, log_dt, b_re, b_im, c_re, c_im):
    f32 = jnp.float32
    lr = lam_re.astype(f32)
    li = lam_im.astype(f32)
    dt = jnp.exp(log_dt.astype(f32))[:, None]
    mag = jnp.exp(lr * dt)
    ab_re = mag * jnp.cos(li * dt)
    ab_im = mag * jnp.sin(li * dt)
    den = lr * lr + li * li
    f_re = ((ab_re - 1.0) * lr + ab_im * li) / den
    f_im = (ab_im * lr - (ab_re - 1.0) * li) / den
    br = b_re.astype(f32)
    bi = b_im.astype(f32)
    bb_re = f_re[..., None] * br - f_im[..., None] * bi
    bb_im = f_re[..., None] * bi + f_im[..., None] * br
    eye = jnp.eye(GROUPS_PER_CHUNK, dtype=f32)

    def chunk_rows(v):
        return v.reshape(N_CHUNKS, 1, CHUNK_STATE)

    def in_blocks(bb):
        t = bb.reshape(N_CHUNKS, GROUPS_PER_CHUNK, STATE_DIM, GROUP_SIZE).transpose(0, 1, 3, 2)
        return jnp.einsum('mgjp,gh->mgjhp', t, eye).reshape(N_CHUNKS, LANES, CHUNK_STATE)

    def out_blocks(c):
        t = c.astype(f32).reshape(N_CHUNKS, GROUPS_PER_CHUNK, GROUP_SIZE, STATE_DIM).transpose(0, 1, 3, 2)
        return jnp.einsum('mgpj,gh->mgphj', t, eye).reshape(N_CHUNKS, CHUNK_STATE, LANES)

    bfull = jnp.concatenate([in_blocks(bb_re), in_blocks(bb_im)], axis=2)
    b_hi = bfull.astype(jnp.bfloat16)
    b_lo = (bfull - b_hi.astype(f32)).astype(jnp.bfloat16)
    bmat = (jnp.concatenate([b_hi, b_hi], axis=1), b_lo)
    cmat = jnp.concatenate([out_blocks(c_re), -out_blocks(c_im)], axis=1)
    pr, pi = [ab_re], [ab_im]
    for _ in range(SUBLANES - 1):
        pr, pi = pr + [pr[-1] * ab_re - pi[-1] * ab_im], pi + [pr[-1] * ab_im + pi[-1] * ab_re]
    apr = jnp.concatenate([chunk_rows(v) for v in pr], axis=1)
    api = jnp.concatenate([chunk_rows(v) for v in pi], axis=1)
    return bmat, cmat, apr, api


def _split_bf16(x):
    hi = x.astype(jnp.bfloat16)
    return hi, (x - hi.astype(jnp.float32)).astype(jnp.bfloat16)


def _s5_kernel(x_ref, g_ref, d_ref, bhh_ref, blo_ref, c_ref, apr_ref, api_ref, h0r_ref, h0i_ref,
               z_ref, sr_ref, si_ref, u_ref, us_ref, st_ref, cr_ref, ci_ref, *, chain, blocks_per_seq):
    rb = pl.program_id(0)
    m = pl.program_id(1)
    rows = x_ref.shape[0]
    n_tiles = rows // SUBLANES

    @pl.when(m == 0)
    def _():
        u = _rmsnorm_rows(x_ref[...], g_ref[...])
        for mm in range(N_CHUNKS):
            uc = u[:, mm * LANES:(mm + 1) * LANES]
            u_ref[mm] = uc
            hi, lo = _split_bf16(uc)
            us_ref[mm, :, 0:LANES] = hi
            us_ref[mm, :, LANES:2 * LANES] = lo

    u = u_ref[m]
    us = us_ref[m]
    st_ref[...] = (jnp.dot(us, bhh_ref[0], preferred_element_type=jnp.float32)
                   + jnp.dot(us[:, 0:LANES], blo_ref[0], preferred_element_type=jnp.float32))

    apr = apr_ref[0]
    api = api_ref[0]
    row = lax.broadcasted_iota(jnp.int32, (SUBLANES, CHUNK_STATE), 0)

    if chain:
        @pl.when(rb % blocks_per_seq == 0)
        def _():
            cr_ref[m] = h0r_ref[0, 0]
            ci_ref[m] = h0i_ref[0, 0]

    def tile_step(k, carry):
        r0 = pl.multiple_of(k * SUBLANES, SUBLANES)
        xr = st_ref[pl.ds(r0, SUBLANES), 0:CHUNK_STATE]
        xi = st_ref[pl.ds(r0, SUBLANES), CHUNK_STATE:2 * CHUNK_STATE]
        for d in (1, 2, 4):
            ar = apr[d - 1:d, :]
            ai = api[d - 1:d, :]
            sr = jnp.where(row >= d, pltpu.roll(xr, d, axis=0), 0.0)
            si = jnp.where(row >= d, pltpu.roll(xi, d, axis=0), 0.0)
            xr, xi = xr + ar * sr - ai * si, xi + ar * si + ai * sr
        if chain:
            cr, ci = carry
        else:
            cr = h0r_ref[k, 0]
            ci = h0i_ref[k, 0]
        hr = xr + apr * cr - api * ci
        hi = xi + apr * ci + api * cr
        st_ref[pl.ds(r0, SUBLANES), 0:CHUNK_STATE] = hr
        st_ref[pl.ds(r0, SUBLANES), CHUNK_STATE:2 * CHUNK_STATE] = hi
        lr_ = hr[SUBLANES - 1:SUBLANES, :]
        li_ = hi[SUBLANES - 1:SUBLANES, :]
        if chain:
            return lr_, li_
        sr_ref[k, m] = lr_
        si_ref[k, m] = li_
        return carry

    if chain:
        cr, ci = lax.fori_loop(0, n_tiles, tile_step, (cr_ref[m], ci_ref[m]))
        cr_ref[m] = cr
        ci_ref[m] = ci
        sr_ref[0, m] = cr
        si_ref[0, m] = ci
    else:
        lax.fori_loop(0, n_tiles, tile_step, 0)

    y = jnp.dot(st_ref[...].astype(jnp.bfloat16), c_ref[0], preferred_element_type=jnp.float32)
    y = y + d_ref[0] * u
    z_ref[...] = _gelu(y)


def _s5_mixer(x, h0r, h0i, g, d_skip, bmat, cmat_bf16, apr, api, *, chain, seq_len):
    T = x.shape[0]
    nseq = h0r.shape[0]
    rows = S5_ROWS
    if chain:
        blocks_per_seq = seq_len // rows
        seq_blk = 1
        seq_map = lambda rb, m: (rb // blocks_per_seq, m, 0, 0)
        out_map = lambda rb, m: (rb // blocks_per_seq, 0, 0, 0)
    else:
        assert seq_len == SUBLANES
        blocks_per_seq = 1
        seq_blk = rows // SUBLANES
        seq_map = lambda rb, m: (rb, m, 0, 0)
        out_map = lambda rb, m: (rb, 0, 0, 0)
    grid = (T // rows, N_CHUNKS)
    kern = functools.partial(_s5_kernel, chain=chain, blocks_per_seq=blocks_per_seq)
    st_spec = pl.BlockSpec((seq_blk, 1, 1, CHUNK_STATE), seq_map)
    out_st_spec = pl.BlockSpec((seq_blk, N_CHUNKS, 1, CHUNK_STATE), out_map)
    z, sr, si = pl.pallas_call(
        kern,
        grid=grid,
        in_specs=[
            pl.BlockSpec((rows, D_MODEL), lambda rb, m: (rb, 0)),
            pl.BlockSpec((1, D_MODEL), lambda rb, m: (0, 0)),
            pl.BlockSpec((1, 1, LANES), lambda rb, m: (m, 0, 0)),
            pl.BlockSpec((1, 2 * LANES, 2 * CHUNK_STATE), lambda rb, m: (m, 0, 0)),
            pl.BlockSpec((1, LANES, 2 * CHUNK_STATE), lambda rb, m: (m, 0, 0)),
            pl.BlockSpec((1, 2 * CHUNK_STATE, LANES), lambda rb, m: (m, 0, 0)),
            pl.BlockSpec((1, SUBLANES, CHUNK_STATE), lambda rb, m: (m, 0, 0)),
            pl.BlockSpec((1, SUBLANES, CHUNK_STATE), lambda rb, m: (m, 0, 0)),
            st_spec, st_spec,
        ],
        out_specs=[
            pl.BlockSpec((rows, LANES), lambda rb, m: (rb, m)),
            out_st_spec, out_st_spec,
        ],
        out_shape=[
            jax.ShapeDtypeStruct((T, D_MODEL), jnp.float32),
            jax.ShapeDtypeStruct((nseq, N_CHUNKS, 1, CHUNK_STATE), jnp.float32),
            jax.ShapeDtypeStruct((nseq, N_CHUNKS, 1, CHUNK_STATE), jnp.float32),
        ],
        scratch_shapes=[
            pltpu.VMEM((N_CHUNKS, rows, LANES), jnp.float32),
            pltpu.VMEM((N_CHUNKS, rows, 2 * LANES), jnp.bfloat16),
            pltpu.VMEM((rows, 2 * CHUNK_STATE), jnp.float32),
            pltpu.VMEM((N_CHUNKS, 1, CHUNK_STATE), jnp.float32),
            pltpu.VMEM((N_CHUNKS, 1, CHUNK_STATE), jnp.float32),
        ],
        compiler_params=pltpu.CompilerParams(
            dimension_semantics=("arbitrary", "arbitrary"), vmem_limit_bytes=VMEM_LIMIT_BYTES),
        name="s5_mixer",
    )(x, g.reshape(1, D_MODEL), d_skip.reshape(N_CHUNKS, 1, LANES), bmat[0], bmat[1], cmat_bf16, apr, api,
      h0r, h0i)
    return z, sr, si


def _state_to_chunks(h):
    return h.reshape(h.shape[0], N_CHUNKS, 1, CHUNK_STATE)


def _chunks_to_state(s):
    return s.reshape(s.shape[0], N_GROUPS, STATE_DIM)


def _glu_kernel(z_ref, res_ref, w_ref, o_ref):
    zz = jnp.dot(z_ref[...].astype(jnp.bfloat16), w_ref[...], preferred_element_type=jnp.float32)
    a = zz[:, :D_MODEL]
    b = zz[:, D_MODEL:]
    o_ref[...] = res_ref[...] + a * (1.0 / (1.0 + jnp.exp(-b)))


def _glu_residual(z, res, w_bf16):
    T = z.shape[0]
    rows = math.gcd(T, 512)
    return pl.pallas_call(
        _glu_kernel,
        grid=(T // rows,),
        in_specs=[
            pl.BlockSpec((rows, D_MODEL), lambda i: (i, 0)),
            pl.BlockSpec((rows, D_MODEL), lambda i: (i, 0)),
            pl.BlockSpec((D_MODEL, 2 * D_MODEL), lambda i: (0, 0)),
        ],
        out_specs=pl.BlockSpec((rows, D_MODEL), lambda i: (i, 0)),
        out_shape=jax.ShapeDtypeStruct((T, D_MODEL), jnp.float32),
        compiler_params=pltpu.CompilerParams(
            dimension_semantics=("arbitrary",), vmem_limit_bytes=VMEM_LIMIT_BYTES),
        name="glu_residual",
    )(z, res, w_bf16)


ROUTE_ROWS = 256


def _topk_rows(s, payload):
    n_rows = s.shape[0]
    row = lax.broadcasted_iota(jnp.int32, s.shape, 0)
    vals, picks = [], []
    for _ in range(PEER_TOPK):
        m = jnp.max(s, axis=0, keepdims=True)
        pos = jnp.min(jnp.where(s == m, row, n_rows), axis=0, keepdims=True)
        sel = row == pos
        vals.append(m)
        if payload is None:
            picks.append(pos)
        else:
            picks.append(jnp.max(jnp.where(sel, payload, -1), axis=0, keepdims=True))
        s = jnp.where(sel, -jnp.inf, s)
    return jnp.concatenate(vals, axis=0), jnp.concatenate(picks, axis=0)


def _pair_rows(a0, a1, combine):
    half = PEER_TOPK // 2
    parts = [combine(a0[0:1, :], a1)]
    parts += [combine(a0[i:i + 1, :], a1[0:half, :]) for i in range(1, half)]
    parts.append(combine(a0[half:PEER_TOPK, :], a1[0:1, :]))
    return jnp.concatenate(parts, axis=0)


def _peer_route_kernel(h_ref, g_ref, wq_ref, sk_ref, xn_ref, eid_ref, gate_ref, xb_ref):
    hd = pl.program_id(1)

    @pl.when(hd == 0)
    def _():
        xn = _rmsnorm_rows(h_ref[...], g_ref[...])
        xn_ref[...] = xn
        xb_ref[...] = xn.astype(jnp.bfloat16)

    q = jnp.dot(xb_ref[...], wq_ref[...], preferred_element_type=jnp.float32)
    sv, si = [], []
    for c in range(2):
        qc = q[:, c * PEER_DHALF:(c + 1) * PEER_DHALF].astype(jnp.bfloat16)
        st = lax.dot_general(sk_ref[0, c], qc, (((1,), (1,)), ((), ())), preferred_element_type=jnp.float32)
        v, i = _topk_rows(st, None)
        sv.append(v)
        si.append(i)
    cand = _pair_rows(sv[0], sv[1], lambda a, b: a + b)
    cid = _pair_rows(si[0], si[1], lambda a, b: a * N_KEYS + b)
    fv, eid = _topk_rows(cand, cid)
    e = jnp.exp(fv - fv[0:1, :])
    gate_ref[...] = e / jnp.sum(e, axis=0, keepdims=True)
    eid_ref[...] = eid


def _peer_route(h, g, wq_bf16, sk_bf16):
    T = h.shape[0]
    rows = ROUTE_ROWS
    n_sel = PEER_HEADS * PEER_TOPK
    return pl.pallas_call(
        _peer_route_kernel,
        grid=(T // rows, PEER_HEADS),
        in_specs=[
            pl.BlockSpec((rows, D_MODEL), lambda tb, hd: (tb, 0)),
            pl.BlockSpec((1, D_MODEL), lambda tb, hd: (0, 0)),
            pl.BlockSpec((D_MODEL, 2 * PEER_DHALF), lambda tb, hd: (0, hd)),
            pl.BlockSpec((1, 2, N_KEYS, PEER_DHALF), lambda tb, hd: (hd, 0, 0, 0)),
        ],
        out_specs=[
            pl.BlockSpec((rows, D_MODEL), lambda tb, hd: (tb, 0)),
            pl.BlockSpec((PEER_TOPK, rows), lambda tb, hd: (hd, tb)),
            pl.BlockSpec((PEER_TOPK, rows), lambda tb, hd: (hd, tb)),
        ],
        out_shape=[
            jax.ShapeDtypeStruct((T, D_MODEL), jnp.float32),
            jax.ShapeDtypeStruct((n_sel, T), jnp.int32),
            jax.ShapeDtypeStruct((n_sel, T), jnp.float32),
        ],
        scratch_shapes=[pltpu.VMEM((rows, D_MODEL), jnp.bfloat16)],
        compiler_params=pltpu.CompilerParams(
            dimension_semantics=("arbitrary", "arbitrary"), vmem_limit_bytes=VMEM_LIMIT_BYTES),
        name="peer_route",
    )(h, g.reshape(1, D_MODEL), wq_bf16, sk_bf16)


N_SEL = PEER_HEADS * PEER_TOPK
SC_LANES = 16
GATHER_ROWS = PEER_TOPK
GATHERS_PER_TOKEN = N_SEL // GATHER_ROWS
GATHER_BUFS = 4
SC_TOKENS = 8


def _sc_gather_stream(wid, n_batches, tab_hbm, stage_srcs, stage_bufs, out_hbm, o_v, bufs, gsem, ssem, osem,
                      compute):
    idx_v = stage_bufs[0]

    def stage_copies(bi, slot):
        base = (wid * n_batches + bi) * SC_TOKENS
        return [pltpu.make_async_copy(src.at[pl.ds(base, SC_TOKENS)], buf.at[slot], ssem.at[slot])
                for src, buf in zip(stage_srcs, stage_bufs)]

    def out_copy(bi, slot):
        base = (wid * n_batches + bi) * SC_TOKENS
        return pltpu.make_async_copy(o_v.at[slot], out_hbm.at[pl.ds(base, SC_TOKENS)], osem.at[slot])

    def start(slot, t, kk, b):
        idx = idx_v[slot, t, pl.ds(kk * GATHER_ROWS, GATHER_ROWS)]
        pltpu.async_copy(tab_hbm.at[idx], bufs.at[b], gsem.at[b])

    def wait(b):
        pltpu.make_async_copy(tab_hbm.at[pl.ds(0, GATHER_ROWS)], bufs.at[b], gsem.at[b]).wait()

    for c in stage_copies(0, 0):
        c.start()
    for c in stage_copies(0, 0):
        c.wait()
    for q in range(GATHER_BUFS - 1):
        start(0, q // GATHERS_PER_TOKEN, q % GATHERS_PER_TOKEN, q % GATHER_BUFS)

    def batch(bi, carry):
        slot = bi % 2
        has_next = bi + 1 < n_batches

        @pl.when(has_next)
        def _():
            for c in stage_copies(bi + 1, 1 - slot):
                c.start()

        @pl.when(bi >= 2)
        def _():
            out_copy(bi, slot).wait()

        def tok(t, carry):
            @pl.when(jnp.logical_and(t == SC_TOKENS - 1, has_next))
            def _():
                for c in stage_copies(bi + 1, 1 - slot):
                    c.wait()

            for kk in range(GATHERS_PER_TOKEN):
                nq = kk + GATHER_BUFS - 1
                nk, nb = nq % GATHERS_PER_TOKEN, nq % GATHER_BUFS
                if nq < GATHERS_PER_TOKEN:
                    start(slot, t, nk, nb)
                else:
                    @pl.when(t + 1 < SC_TOKENS)
                    def _():
                        start(slot, t + 1, nk, nb)

                    @pl.when(jnp.logical_and(t + 1 == SC_TOKENS, has_next))
                    def _():
                        start(1 - slot, 0, nk, nb)

                wait(kk % GATHER_BUFS)
                compute(slot, t, kk, kk % GATHER_BUFS)
            return carry

        lax.fori_loop(0, SC_TOKENS, tok, 0)
        out_copy(bi, slot).start()
        return carry

    lax.fori_loop(0, n_batches, batch, 0)
    if n_batches >= 2:
        out_copy(n_batches - 2, (n_batches - 2) % 2).wait()
    out_copy(n_batches - 1, (n_batches - 1) % 2).wait()


def _sc_mesh_and_batches(n_tokens):
    info = plsc.get_sparse_core_info()
    assert info.num_lanes == SC_LANES
    n_workers = info.num_cores * info.num_subcores
    assert n_tokens % (n_workers * SC_TOKENS) == 0
    mesh = plsc.VectorSubcoreMesh(core_axis_name="c", subcore_axis_name="s")
    return info, mesh, n_tokens // (n_workers * SC_TOKENS)


def _peer_hidden_sc(eid, xn, u_tab):
    T = eid.shape[0]
    info, mesh, n_batches = _sc_mesh_and_batches(T)

    @functools.partial(
        pl.kernel, mesh=mesh,
        out_type=jax.ShapeDtypeStruct((T, N_SEL), jnp.float32),
        scratch_types=[
            pltpu.VMEM((2, SC_TOKENS, N_SEL), jnp.int32),
            pltpu.VMEM((2, SC_TOKENS, D_MODEL), jnp.float32),
            pltpu.VMEM((2, SC_TOKENS, N_SEL), jnp.float32),
            pltpu.VMEM((GATHER_BUFS, GATHER_ROWS, D_MODEL), jnp.float32),
            pltpu.VMEM((GATHER_ROWS, SC_LANES), jnp.float32),
            pltpu.SemaphoreType.DMA((GATHER_BUFS,)),
            pltpu.SemaphoreType.DMA((2,)),
            pltpu.SemaphoreType.DMA((2,)),
        ],
        compiler_params=pltpu.CompilerParams(needs_layout_passes=False),
        name="peer_hidden_sc",
    )
    def k(eid_hbm, xn_hbm, u_hbm, out_hbm, idx_v, x_v, o_v, bufs, acc_v, gsem, ssem, osem):
        wid = lax.axis_index("s") * info.num_cores + lax.axis_index("c")
        lane = lax.iota(jnp.int32, SC_LANES)
        zero = jnp.zeros((SC_LANES,), jnp.float32)

        def compute(slot, t, kk, b):
            @plsc.parallel_loop(0, D_MODEL // SC_LANES, carry=(zero,) * GATHER_ROWS)
            def accs(c, accs):
                xc = x_v[slot, t, pl.ds(c * SC_LANES, SC_LANES)]
                return tuple(accs[r] + bufs[b, r, pl.ds(c * SC_LANES, SC_LANES)] * xc for r in range(GATHER_ROWS))

            for r in range(GATHER_ROWS):
                acc_v[r, :] = accs[r]
            tot = zero
            for c in range(SC_LANES):
                tot = tot + plsc.load_gather(acc_v, [lane, jnp.full((SC_LANES,), c, jnp.int32)])
            o_v[slot, t, pl.ds(kk * GATHER_ROWS, GATHER_ROWS)] = tot

        _sc_gather_stream(wid, n_batches, u_hbm, [eid_hbm, xn_hbm], [idx_v, x_v], out_hbm, o_v, bufs,
                          gsem, ssem, osem, compute)

    return k(eid, xn, u_tab)


def _peer_combine_sc(eid, a, v_tab):
    T = eid.shape[0]
    info, mesh, n_batches = _sc_mesh_and_batches(T)

    @functools.partial(
        pl.kernel, mesh=mesh,
        out_type=jax.ShapeDtypeStruct((T, D_MODEL), jnp.float32),
        scratch_types=[
            pltpu.VMEM((2, SC_TOKENS, N_SEL), jnp.int32),
            pltpu.VMEM((2, SC_TOKENS, N_SEL), jnp.float32),
            pltpu.VMEM((2, SC_TOKENS, D_MODEL), jnp.float32),
            pltpu.VMEM((GATHER_BUFS, GATHER_ROWS, D_MODEL), jnp.float32),
            pltpu.SemaphoreType.DMA((GATHER_BUFS,)),
            pltpu.SemaphoreType.DMA((2,)),
            pltpu.SemaphoreType.DMA((2,)),
        ],
        compiler_params=pltpu.CompilerParams(needs_layout_passes=False),
        name="peer_combine_sc",
    )
    def k(eid_hbm, a_hbm, v_hbm, out_hbm, idx_v, a_v, o_v, bufs, gsem, ssem, osem):
        wid = lax.axis_index("s") * info.num_cores + lax.axis_index("c")

        def compute(slot, t, kk, b):
            svec = jnp.full((SC_LANES,), slot, jnp.int32)
            tvec = jnp.full((SC_LANES,), t, jnp.int32)
            ws = [plsc.load_gather(a_v, [svec, tvec, jnp.full((SC_LANES,), kk * GATHER_ROWS + r, jnp.int32)])
                  for r in range(GATHER_ROWS)]

            @plsc.parallel_loop(0, D_MODEL // SC_LANES, unroll=2)
            def _(c):
                sl = pl.ds(c * SC_LANES, SC_LANES)
                terms = [ws[r] * bufs[b, r, sl] for r in range(GATHER_ROWS)]
                if kk != 0:
                    terms.append(o_v[slot, t, sl])
                while len(terms) > 1:
                    pairs = [terms[i] + terms[i + 1] for i in range(0, len(terms) - 1, 2)]
                    terms = pairs + ([terms[-1]] if len(terms) % 2 else [])
                o_v[slot, t, sl] = terms[0]

        _sc_gather_stream(wid, n_batches, v_hbm, [eid_hbm, a_hbm], [idx_v, a_v], out_hbm, o_v, bufs,
                          gsem, ssem, osem, compute)

    return k(eid, a, v_tab)


def _peer_act_kernel(hp_ref, gate_ref, a_ref):
    a_ref[...] = _gelu(hp_ref[...]) * gate_ref[...]


ELEMENTWISE_ROWS = 1024


def _peer_act(hpre, gate):
    T = hpre.shape[0]
    rows = ELEMENTWISE_ROWS
    assert T % rows == 0
    spec = pl.BlockSpec((rows, N_SEL), lambda i: (i, 0))
    return pl.pallas_call(
        _peer_act_kernel, grid=(T // rows,), in_specs=[spec, spec], out_specs=spec,
        out_shape=jax.ShapeDtypeStruct((T, N_SEL), jnp.float32),
        compiler_params=pltpu.CompilerParams(dimension_semantics=("arbitrary",)),
        name="peer_act",
    )(hpre, gate)


def _residual_kernel(h_ref, c_ref, g_ref, o_ref, *, final_norm):
    y = h_ref[...] + c_ref[...]
    o_ref[...] = _rmsnorm_rows(y, g_ref[...]) if final_norm else y


def _residual(h, c, gfin, *, final_norm):
    T = h.shape[0]
    rows = ELEMENTWISE_ROWS
    assert T % rows == 0
    spec = pl.BlockSpec((rows, D_MODEL), lambda i: (i, 0))
    return pl.pallas_call(
        functools.partial(_residual_kernel, final_norm=final_norm), grid=(T // rows,),
        in_specs=[spec, spec, pl.BlockSpec((1, D_MODEL), lambda i: (0, 0))], out_specs=spec,
        out_shape=jax.ShapeDtypeStruct((T, D_MODEL), jnp.float32),
        compiler_params=pltpu.CompilerParams(dimension_semantics=("arbitrary",)),
        name="peer_residual",
    )(h, c, gfin.reshape(1, D_MODEL))


def _peer_experts(eid, xn, gate, h, gfin, u_tab, v_tab, *, final_norm):
    hpre = _peer_hidden_sc(eid, xn, u_tab)
    a = _peer_act(hpre, gate)
    c = _peer_combine_sc(eid, a, v_tab)
    return _residual(h, c, gfin, final_norm=final_norm)


KV_WIDTH = N_KV_HEADS * HEAD_DIM
BF16 = jnp.bfloat16


def _qkv(h, gkv, gq, wkv_ref, wq_ref):
    kv = jnp.dot(_rmsnorm_rows(h, gkv).astype(BF16), wkv_ref[...], preferred_element_type=jnp.float32)
    q = jnp.dot(_rmsnorm_rows(h, gq).astype(BF16), wq_ref[...], preferred_element_type=jnp.float32)
    return kv, q


def _sink_softmax_pv(parts, sink):
    m = sink
    for s, _ in parts:
        m = jnp.maximum(m, jnp.max(s, axis=-1, keepdims=True))
    den = jnp.exp(sink - m)
    acc = None
    for s, v in parts:
        e = jnp.exp(s - m)
        den = den + jnp.sum(e, axis=-1, keepdims=True)
        pv = jnp.dot(e.astype(BF16), v.astype(BF16), preferred_element_type=jnp.float32)
        acc = pv if acc is None else acc + pv
    return acc / den


def _nt_dot(a, b):
    return lax.dot_general(a.astype(BF16), b.astype(BF16), (((1,), (1,)), ((), ())),
                           preferred_element_type=jnp.float32)


def _attn_prompt_kernel(h_ref, gkv_ref, gq_ref, wkv_ref, wq_ref, wo_ref, sink_ref,
                        o_ref, kvw_ref, prev_ref):
    blk = pl.program_id(1)
    h = h_ref[...]
    kv, q = _qkv(h, gkv_ref[...], gq_ref[...], wkv_ref, wq_ref)
    kvw_ref[0] = kv

    @pl.when(blk == 0)
    def _():
        prev_ref[...] = jnp.zeros_like(prev_ref)

    prev = prev_ref[...]
    qi = lax.broadcasted_iota(jnp.int32, (WINDOW, WINDOW), 0)
    kj = lax.broadcasted_iota(jnp.int32, (WINDOW, WINDOW), 1)
    prev_ok = jnp.logical_and(kj > qi, blk > 0)
    cur_ok = kj <= qi
    heads = []
    for kvh in range(N_KV_HEADS):
        ks = slice(kvh * HEAD_DIM, (kvh + 1) * HEAD_DIM)
        vs = slice(KV_WIDTH + kvh * HEAD_DIM, KV_WIDTH + (kvh + 1) * HEAD_DIM)
        for g in range(Q_PER_KV):
            hq = kvh * Q_PER_KV + g
            qh = q[:, hq * HEAD_DIM:(hq + 1) * HEAD_DIM]
            sp = jnp.where(prev_ok, _nt_dot(qh, prev[:, ks]) * ATTN_SCALE, -jnp.inf)
            sc = jnp.where(cur_ok, _nt_dot(qh, kv[:, ks]) * ATTN_SCALE, -jnp.inf)
            heads.append(_sink_softmax_pv([(sp, prev[:, vs]), (sc, kv[:, vs])], sink_ref[hq]))
    o = jnp.concatenate(heads, axis=1)
    o_ref[...] = h + jnp.dot(o.astype(BF16), wo_ref[...], preferred_element_type=jnp.float32)
    prev_ref[...] = kv


def _attn_prompt(h, gkv, gq, wkv, wq, wo, sinks, *, n_seq, seq_len):
    nb = seq_len // WINDOW
    row_spec = pl.BlockSpec((WINDOW, D_MODEL), lambda n, b: (n * nb + b, 0))
    full = lambda shape: pl.BlockSpec(shape, lambda n, b: (0,) * len(shape))
    return pl.pallas_call(
        _attn_prompt_kernel,
        grid=(n_seq, nb),
        in_specs=[
            row_spec, full((1, D_MODEL)), full((1, D_MODEL)),
            full((D_MODEL, 2 * KV_WIDTH)), full((D_MODEL, D_MODEL)), full((D_MODEL, D_MODEL)),
            pl.BlockSpec(memory_space=pltpu.SMEM),
        ],
        out_specs=[row_spec, pl.BlockSpec((1, WINDOW, 2 * KV_WIDTH), lambda n, b: (n, 0, 0))],
        out_shape=[
            jax.ShapeDtypeStruct(h.shape, jnp.float32),
            jax.ShapeDtypeStruct((n_seq, WINDOW, 2 * KV_WIDTH), jnp.float32),
        ],
        scratch_shapes=[pltpu.VMEM((WINDOW, 2 * KV_WIDTH), jnp.float32)],
        compiler_params=pltpu.CompilerParams(
            dimension_semantics=("arbitrary", "arbitrary"), vmem_limit_bytes=VMEM_LIMIT_BYTES),
        name="attn_prompt",
    )(h, gkv.reshape(1, D_MODEL), gq.reshape(1, D_MODEL), wkv, wq, wo, sinks)


ATTN_SEQS = 16


def _attn_sample_kernel(h_ref, ck_ref, cv_ref, gkv_ref, gq_ref, wkv_ref, wq_ref, wo_ref, sink_ref,
                        o_ref, kw_ref, vw_ref, kv_ref, q_ref, att_ref, *, n_new):
    h = h_ref[...]
    kv, q = _qkv(h, gkv_ref[...], gq_ref[...], wkv_ref, wq_ref)
    kv_ref[...] = kv
    q_ref[...] = q
    n_seq = h.shape[0] // n_new
    rows = Q_PER_KV * n_new
    qpos_c = lax.broadcasted_iota(jnp.int32, (rows, WINDOW), 0) % n_new
    cache_ok = lax.broadcasted_iota(jnp.int32, (rows, WINDOW), 1) > qpos_c
    qpos_n = lax.broadcasted_iota(jnp.int32, (rows, n_new), 0) % n_new
    new_ok = lax.broadcasted_iota(jnp.int32, (rows, n_new), 1) <= qpos_n

    def per_seq(n, carry):
        r0 = pl.multiple_of(n * n_new, n_new)
        kvn = kv_ref[pl.ds(r0, n_new), :]
        qn = q_ref[pl.ds(r0, n_new), :]
        ck = ck_ref[n]
        cv = cv_ref[n]
        kw_ref[n, 0:WINDOW - n_new, :] = ck[n_new:, :]
        kw_ref[n, WINDOW - n_new:WINDOW, :] = kvn[:, 0:KV_WIDTH]
        vw_ref[n, 0:WINDOW - n_new, :] = cv[n_new:, :]
        vw_ref[n, WINDOW - n_new:WINDOW, :] = kvn[:, KV_WIDTH:2 * KV_WIDTH]
        outs = []
        for kvh in range(N_KV_HEADS):
            ks = slice(kvh * HEAD_DIM, (kvh + 1) * HEAD_DIM)
            vs = slice(KV_WIDTH + kvh * HEAD_DIM, KV_WIDTH + (kvh + 1) * HEAD_DIM)
            qs = jnp.concatenate(
                [qn[:, (kvh * Q_PER_KV + g) * HEAD_DIM:(kvh * Q_PER_KV + g + 1) * HEAD_DIM] for g in range(Q_PER_KV)],
                axis=0)
            s_c = jnp.where(cache_ok, _nt_dot(qs, ck[:, ks]) * ATTN_SCALE, -jnp.inf)
            s_n = jnp.where(new_ok, _nt_dot(qs, kvn[:, ks]) * ATTN_SCALE, -jnp.inf)
            sink = sink_ref[kvh][:, 0:1]
            o = _sink_softmax_pv([(s_c, cv[:, ks]), (s_n, kvn[:, vs])], sink)
            outs += [o[g * n_new:(g + 1) * n_new, :] for g in range(Q_PER_KV)]
        att_ref[pl.ds(r0, n_new), :] = jnp.concatenate(outs, axis=1)
        return carry

    lax.fori_loop(0, n_seq, per_seq, 0)
    o_ref[...] = h + jnp.dot(att_ref[...].astype(BF16), wo_ref[...], preferred_element_type=jnp.float32)


def _attn_sample(h, cache_k, cache_v, gkv, gq, wkv, wq, wo, sinks, *, n_new):
    n_seq = cache_k.shape[0]
    sb = ATTN_SEQS
    rows = sb * n_new
    row_spec = pl.BlockSpec((rows, D_MODEL), lambda i: (i, 0))
    win_spec = pl.BlockSpec((sb, WINDOW, KV_WIDTH), lambda i: (i, 0, 0))
    full = lambda shape: pl.BlockSpec(shape, lambda i: (0,) * len(shape))
    sink_rows = jnp.repeat(sinks.reshape(N_KV_HEADS, Q_PER_KV), n_new, axis=1)[:, :, None]
    sink_rows = jnp.broadcast_to(sink_rows, (N_KV_HEADS, Q_PER_KV * n_new, LANES))
    return pl.pallas_call(
        functools.partial(_attn_sample_kernel, n_new=n_new),
        grid=(n_seq // sb,),
        in_specs=[
            row_spec, win_spec, win_spec, full((1, D_MODEL)), full((1, D_MODEL)),
            full((D_MODEL, 2 * KV_WIDTH)), full((D_MODEL, D_MODEL)), full((D_MODEL, D_MODEL)),
            full((N_KV_HEADS, Q_PER_KV * n_new, LANES)),
        ],
        out_specs=[row_spec, win_spec, win_spec],
        out_shape=[
            jax.ShapeDtypeStruct(h.shape, jnp.float32),
            jax.ShapeDtypeStruct((n_seq, WINDOW, KV_WIDTH), jnp.float32),
            jax.ShapeDtypeStruct((n_seq, WINDOW, KV_WIDTH), jnp.float32),
        ],
        scratch_shapes=[
            pltpu.VMEM((rows, 2 * KV_WIDTH), jnp.float32),
            pltpu.VMEM((rows, D_MODEL), jnp.float32),
            pltpu.VMEM((rows, D_MODEL), jnp.float32),
        ],
        compiler_params=pltpu.CompilerParams(
            dimension_semantics=("arbitrary",), vmem_limit_bytes=VMEM_LIMIT_BYTES),
        name="attn_sample",
    )(h, cache_k, cache_v, gkv.reshape(1, D_MODEL), gq.reshape(1, D_MODEL), wkv, wq, wo, sink_rows)


def kernel(x_prompt, x_sample, state_ssm_re, state_ssm_im, cache_k_win, cache_v_win, norm_mix, norm_ffn, norm_kv, norm_final, ssm_lam_re, ssm_lam_im, ssm_log_dt, ssm_b_re, ssm_b_im, ssm_c_re, ssm_c_im, ssm_d, ssm_w_glu, w_kv, w_q, attn_sinks, w_o, peer_w_q, peer_sub_keys, peer_u, peer_v):
    bmat, cmat, apr, api = _s5_discretize(ssm_lam_re[0], ssm_lam_im[0], ssm_log_dt[0], ssm_b_re[0], ssm_b_im[0], ssm_c_re[0], ssm_c_im[0])
    cmat = cmat.astype(jnp.bfloat16)
    wglu = ssm_w_glu[0].astype(jnp.bfloat16)
    wkv = w_kv.astype(BF16)
    wq = w_q[0].astype(BF16)
    wo = w_o[0].astype(BF16)
    peer_wq = [peer_w_q[layer].astype(BF16) for layer in range(2)]
    peer_sk = [peer_sub_keys[layer].astype(BF16) for layer in range(2)]

    def peer(h, layer, final_norm):
        xn, eid_t, gate_t = _peer_route(h, norm_ffn[layer], peer_wq[layer], peer_sk[layer])
        return _peer_experts(eid_t.T, xn, gate_t.T, h, norm_final, peer_u[layer], peer_v[layer],
                             final_norm=final_norm)

    win = lambda a: a.reshape(a.shape[0], WINDOW, N_KV_HEADS, HEAD_DIM)

    def trunk_prompt(x_p):
        n_p = x_p.shape[0]
        xp = x_p.reshape(-1, D_MODEL)
        z0 = jnp.zeros((n_p, N_CHUNKS, 1, CHUNK_STATE), jnp.float32)
        z, sr, si = _s5_mixer(xp, z0, z0, norm_mix[0], ssm_d[0], bmat, cmat, apr, api,
                              chain=True, seq_len=x_p.shape[1])
        h2 = peer(_glu_residual(z, xp, wglu), 0, False)
        h3, kvw = _attn_prompt(h2, norm_kv, norm_mix[1], wkv, wq, wo, attn_sinks[0], n_seq=n_p, seq_len=x_p.shape[1])
        y = peer(h3, 1, True)
        return (y.reshape(x_p.shape), _chunks_to_state(sr), _chunks_to_state(si),
                win(kvw[:, :, :KV_WIDTH]), win(kvw[:, :, KV_WIDTH:]))

    def trunk_sample(x_s, s_re, s_im, c_k, c_v):
        n_s = x_s.shape[0]
        xs = x_s.reshape(-1, D_MODEL)
        z, sr, si = _s5_mixer(xs, _state_to_chunks(s_re), _state_to_chunks(s_im), norm_mix[0], ssm_d[0],
                              bmat, cmat, apr, api, chain=False, seq_len=x_s.shape[1])
        h2 = peer(_glu_residual(z, xs, wglu), 0, False)
        h3, kw, vw = _attn_sample(h2, c_k.reshape(n_s, WINDOW, KV_WIDTH), c_v.reshape(n_s, WINDOW, KV_WIDTH),
                                  norm_kv, norm_mix[1], wkv, wq, wo, attn_sinks[0], n_new=x_s.shape[1])
        y = peer(h3, 1, True)
        return y.reshape(x_s.shape), _chunks_to_state(sr), _chunks_to_state(si), win(kw), win(vw)

    y_s, sre_s, sim_s, kw_s, vw_s = trunk_sample(x_sample, state_ssm_re[0], state_ssm_im[0], cache_k_win, cache_v_win)
    n_groups = 4
    gp = x_prompt.shape[0] // n_groups
    outs = [trunk_prompt(x_prompt[i * gp:(i + 1) * gp]) for i in range(n_groups)]
    y_p, sre_p, sim_p, kw_p, vw_p = [jnp.concatenate(parts, axis=0) for parts in zip(*outs)]
    return (y_p, y_s, sre_p[None], sim_p[None], kw_p, vw_p, sre_s[None], sim_s[None], kw_s, vw_s)
```

```python
import functools
import math

import jax
import jax.numpy as jnp
from jax import lax
from jax.experimental import pallas as pl
from jax.experimental.pallas import tpu as pltpu
from jax.experimental.pallas import tpu_sc as plsc

D_MODEL = 1024
GROUP_SIZE = 16
N_GROUPS = D_MODEL // GROUP_SIZE
STATE_DIM = 64
HEAD_DIM = 64
N_Q_HEADS = D_MODEL // HEAD_DIM
N_KV_HEADS = N_Q_HEADS // 8
Q_PER_KV = N_Q_HEADS // N_KV_HEADS
WINDOW = 128
ATTN_SCALE = 1.0 / math.sqrt(HEAD_DIM)
PEER_HEADS = 8
N_KEYS = 128
PEER_TOPK = 16
PEER_DHALF = 128
EPS = 1e-5

LANES = 128
SUBLANES = 8
VMEM_LIMIT_BYTES = 56 * 1024 * 1024

GROUPS_PER_CHUNK = LANES // GROUP_SIZE
N_CHUNKS = N_GROUPS // GROUPS_PER_CHUNK
CHUNK_STATE = GROUPS_PER_CHUNK * STATE_DIM
S5_ROWS = 256


def _rmsnorm_rows(x, g):
    r = lax.rsqrt(jnp.mean(x * x, axis=-1, keepdims=True) + EPS)
    return x * r * g


def _gelu(x):
    return 0.5 * x * (1.0 + lax.erf(x * (1.0 / math.sqrt(2.0))))


def _s5_discretize(lam_re, lam_im, log_dt, b_re, b_im, c_re, c_im):
    f32 = jnp.float32
    lr = lam_re.astype(f32)
    li = lam_im.astype(f32)
    dt = jnp.exp(log_dt.astype(f32))[:, None]
    mag = jnp.exp(lr * dt)
    ab_re = mag * jnp.cos(li * dt)
    ab_im = mag * jnp.sin(li * dt)
    den = lr * lr + li * li
    f_re = ((ab_re - 1.0) * lr + ab_im * li) / den
    f_im = (ab_im * lr - (ab_re - 1.0) * li) / den
    br = b_re.astype(f32)
    bi = b_im.astype(f32)
    bb_re = f_re[..., None] * br - f_im[..., None] * bi
    bb_im = f_re[..., None] * bi + f_im[..., None] * br
    eye = jnp.eye(GROUPS_PER_CHUNK, dtype=f32)

    def chunk_rows(v):
        return v.reshape(N_CHUNKS, 1, CHUNK_STATE)

    def in_blocks(bb):
        t = bb.reshape(N_CHUNKS, GROUPS_PER_CHUNK, STATE_DIM, GROUP_SIZE).transpose(0, 1, 3, 2)
        return jnp.einsum('mgjp,gh->mgjhp', t, eye).reshape(N_CHUNKS, LANES, CHUNK_STATE)

    def out_blocks(c):
        t = c.astype(f32).reshape(N_CHUNKS, GROUPS_PER_CHUNK, GROUP_SIZE, STATE_DIM).transpose(0, 1, 3, 2)
        return jnp.einsum('mgpj,gh->mgphj', t, eye).reshape(N_CHUNKS, CHUNK_STATE, LANES)

    bfull = jnp.concatenate([in_blocks(bb_re), in_blocks(bb_im)], axis=2)
    b_hi = bfull.astype(jnp.bfloat16)
    b_lo = (bfull - b_hi.astype(f32)).astype(jnp.bfloat16)
    bmat = (jnp.concatenate([b_hi, b_hi], axis=1), b_lo)
    cmat = jnp.concatenate([out_blocks(c_re), -out_blocks(c_im)], axis=1)
    pr, pi = [ab_re], [ab_im]
    for _ in range(SUBLANES - 1):
        pr, pi = pr + [pr[-1] * ab_re - pi[-1] * ab_im], pi + [pr[-1] * ab_im + pi[-1] * ab_re]
    apr = jnp.concatenate([chunk_rows(v) for v in pr], axis=1)
    api = jnp.concatenate([chunk_rows(v) for v in pi], axis=1)
    return bmat, cmat, apr, api


def _split_bf16(x):
    hi = x.astype(jnp.bfloat16)
    return hi, (x - hi.astype(jnp.float32)).astype(jnp.bfloat16)


def _s5_kernel(x_ref, g_ref, d_ref, bhh_ref, blo_ref, c_ref, apr_ref, api_ref, h0r_ref, h0i_ref,
               z_ref, sr_ref, si_ref, u_ref, us_ref, st_ref, cr_ref, ci_ref, *, chain, blocks_per_seq):
    rb = pl.program_id(0)
    m = pl.program_id(1)
    rows = x_ref.shape[0]
    n_tiles = rows // SUBLANES

    @pl.when(m == 0)
    def _():
        u = _rmsnorm_rows(x_ref[...], g_ref[...])
        for mm in range(N_CHUNKS):
            uc = u[:, mm * LANES:(mm + 1) * LANES]
            u_ref[mm] = uc
            hi, lo = _split_bf16(uc)
            us_ref[mm, :, 0:LANES] = hi
            us_ref[mm, :, LANES:2 * LANES] = lo

    u = u_ref[m]
    us = us_ref[m]
    st_ref[...] = (jnp.dot(us, bhh_ref[0], preferred_element_type=jnp.float32)
                   + jnp.dot(us[:, 0:LANES], blo_ref[0], preferred_element_type=jnp.float32))

    apr = apr_ref[0]
    api = api_ref[0]
    row = lax.broadcasted_iota(jnp.int32, (SUBLANES, CHUNK_STATE), 0)

    if chain:
        @pl.when(rb % blocks_per_seq == 0)
        def _():
            cr_ref[m] = h0r_ref[0, 0]
            ci_ref[m] = h0i_ref[0, 0]

    def tile_step(k, carry):
        r0 = pl.multiple_of(k * SUBLANES, SUBLANES)
        xr = st_ref[pl.ds(r0, SUBLANES), 0:CHUNK_STATE]
        xi = st_ref[pl.ds(r0, SUBLANES), CHUNK_STATE:2 * CHUNK_STATE]
        for d in (1, 2, 4):
            ar = apr[d - 1:d, :]
            ai = api[d - 1:d, :]
            sr = jnp.where(row >= d, pltpu.roll(xr, d, axis=0), 0.0)
            si = jnp.where(row >= d, pltpu.roll(xi, d, axis=0), 0.0)
            xr, xi = xr + ar * sr - ai * si, xi + ar * si + ai * sr
        if chain:
            cr, ci = carry
        else:
            cr = h0r_ref[k, 0]
            ci = h0i_ref[k, 0]
        hr = xr + apr * cr - api * ci
        hi = xi + apr * ci + api * cr
        st_ref[pl.ds(r0, SUBLANES), 0:CHUNK_STATE] = hr
        st_ref[pl.ds(r0, SUBLANES), CHUNK_STATE:2 * CHUNK_STATE] = hi
        lr_ = hr[SUBLANES - 1:SUBLANES, :]
        li_ = hi[SUBLANES - 1:SUBLANES, :]
        if chain:
            return lr_, li_
        sr_ref[k, m] = lr_
        si_ref[k, m] = li_
        return carry

    if chain:
        cr, ci = lax.fori_loop(0, n_tiles, tile_step, (cr_ref[m], ci_ref[m]))
        cr_ref[m] = cr
        ci_ref[m] = ci
        sr_ref[0, m] = cr
        si_ref[0, m] = ci
    else:
        lax.fori_loop(0, n_tiles, tile_step, 0)

    y = jnp.dot(st_ref[...].astype(jnp.bfloat16), c_ref[0], preferred_element_type=jnp.float32)
    y = y + d_ref[0] * u
    z_ref[...] = _gelu(y)


def _s5_mixer(x, h0r, h0i, g, d_skip, bmat, cmat_bf16, apr, api, *, chain, seq_len, row0=0):
    nseq = h0r.shape[0]
    T = nseq * seq_len
    rows = S5_ROWS
    blk0 = row0 // rows
    if chain:
        blocks_per_seq = seq_len // rows
        seq_blk = 1
        seq_map = lambda rb, m: (rb // blocks_per_seq, m, 0, 0)
        out_map = lambda rb, m: (rb // blocks_per_seq, 0, 0, 0)
    else:
        assert seq_len == SUBLANES
        blocks_per_seq = 1
        seq_blk = rows // SUBLANES
        seq_map = lambda rb, m: (rb, m, 0, 0)
        out_map = lambda rb, m: (rb, 0, 0, 0)
    grid = (T // rows, N_CHUNKS)
    kern = functools.partial(_s5_kernel, chain=chain, blocks_per_seq=blocks_per_seq)
    st_spec = pl.BlockSpec((seq_blk, 1, 1, CHUNK_STATE), seq_map)
    out_st_spec = pl.BlockSpec((seq_blk, N_CHUNKS, 1, CHUNK_STATE), out_map)
    z, sr, si = pl.pallas_call(
        kern,
        grid=grid,
        in_specs=[
            pl.BlockSpec((rows, D_MODEL), lambda rb, m: (rb + blk0, 0)),
            pl.BlockSpec((1, D_MODEL), lambda rb, m: (0, 0)),
            pl.BlockSpec((1, 1, LANES), lambda rb, m: (m, 0, 0)),
            pl.BlockSpec((1, 2 * LANES, 2 * CHUNK_STATE), lambda rb, m: (m, 0, 0)),
            pl.BlockSpec((1, LANES, 2 * CHUNK_STATE), lambda rb, m: (m, 0, 0)),
            pl.BlockSpec((1, 2 * CHUNK_STATE, LANES), lambda rb, m: (m, 0, 0)),
            pl.BlockSpec((1, SUBLANES, CHUNK_STATE), lambda rb, m: (m, 0, 0)),
            pl.BlockSpec((1, SUBLANES, CHUNK_STATE), lambda rb, m: (m, 0, 0)),
            st_spec, st_spec,
        ],
        out_specs=[
            pl.BlockSpec((rows, LANES), lambda rb, m: (rb, m)),
            out_st_spec, out_st_spec,
        ],
        out_shape=[
            jax.ShapeDtypeStruct((T, D_MODEL), jnp.float32),
            jax.ShapeDtypeStruct((nseq, N_CHUNKS, 1, CHUNK_STATE), jnp.float32),
            jax.ShapeDtypeStruct((nseq, N_CHUNKS, 1, CHUNK_STATE), jnp.float32),
        ],
        scratch_shapes=[
            pltpu.VMEM((N_CHUNKS, rows, LANES), jnp.float32),
            pltpu.VMEM((N_CHUNKS, rows, 2 * LANES), jnp.bfloat16),
            pltpu.VMEM((rows, 2 * CHUNK_STATE), jnp.float32),
            pltpu.VMEM((N_CHUNKS, 1, CHUNK_STATE), jnp.float32),
            pltpu.VMEM((N_CHUNKS, 1, CHUNK_STATE), jnp.float32),
        ],
        compiler_params=pltpu.CompilerParams(
            dimension_semantics=("arbitrary", "arbitrary"), vmem_limit_bytes=VMEM_LIMIT_BYTES),
        name="s5_mixer",
    )(x, g.reshape(1, D_MODEL), d_skip.reshape(N_CHUNKS, 1, LANES), bmat[0], bmat[1], cmat_bf16, apr, api,
      h0r, h0i)
    return z, sr, si


def _state_to_chunks(h):
    return h.reshape(h.shape[0], N_CHUNKS, 1, CHUNK_STATE)


def _chunks_to_state(s):
    return s.reshape(s.shape[0], N_GROUPS, STATE_DIM)


def _glu_kernel(z_ref, res_ref, w_ref, o_ref):
    zz = jnp.dot(z_ref[...].astype(jnp.bfloat16), w_ref[...], preferred_element_type=jnp.float32)
    a = zz[:, :D_MODEL]
    b = zz[:, D_MODEL:]
    o_ref[...] = res_ref[...] + a * (1.0 / (1.0 + jnp.exp(-b)))


def _glu_residual(z, res, w_bf16, res_row0=0):
    T = z.shape[0]
    rows = math.gcd(T, 512)
    blk0 = res_row0 // rows
    return pl.pallas_call(
        _glu_kernel,
        grid=(T // rows,),
        in_specs=[
            pl.BlockSpec((rows, D_MODEL), lambda i: (i, 0)),
            pl.BlockSpec((rows, D_MODEL), lambda i: (i + blk0, 0)),
            pl.BlockSpec((D_MODEL, 2 * D_MODEL), lambda i: (0, 0)),
        ],
        out_specs=pl.BlockSpec((rows, D_MODEL), lambda i: (i, 0)),
        out_shape=jax.ShapeDtypeStruct((T, D_MODEL), jnp.float32),
        compiler_params=pltpu.CompilerParams(
            dimension_semantics=("arbitrary",), vmem_limit_bytes=VMEM_LIMIT_BYTES),
        name="glu_residual",
    )(z, res, w_bf16)


ROUTE_ROWS = 256


def _topk_rows(s, payload):
    n_rows = s.shape[0]
    row = lax.broadcasted_iota(jnp.int32, s.shape, 0)
    vals, picks = [], []
    for _ in range(PEER_TOPK):
        m = jnp.max(s, axis=0, keepdims=True)
        pos = jnp.min(jnp.where(s == m, row, n_rows), axis=0, keepdims=True)
        sel = row == pos
        vals.append(m)
        if payload is None:
            picks.append(pos)
        else:
            picks.append(jnp.max(jnp.where(sel, payload, -1), axis=0, keepdims=True))
        s = jnp.where(sel, -jnp.inf, s)
    return jnp.concatenate(vals, axis=0), jnp.concatenate(picks, axis=0)


def _pair_rows(a0, a1, combine):
    half = PEER_TOPK // 2
    parts = [combine(a0[0:1, :], a1)]
    parts += [combine(a0[i:i + 1, :], a1[0:half, :]) for i in range(1, half)]
    parts.append(combine(a0[half:PEER_TOPK, :], a1[0:1, :]))
    return jnp.concatenate(parts, axis=0)


def _peer_route_kernel(h_ref, g_ref, wq_ref, sk_ref, xn_ref, eid_ref, gate_ref, xb_ref):
    hd = pl.program_id(1)

    @pl.when(hd == 0)
    def _():
        xn = _rmsnorm_rows(h_ref[...], g_ref[...])
        xn_ref[...] = xn
        xb_ref[...] = xn.astype(jnp.bfloat16)

    q = jnp.dot(xb_ref[...], wq_ref[...], preferred_element_type=jnp.float32)
    sv, si = [], []
    for c in range(2):
        qc = q[:, c * PEER_DHALF:(c + 1) * PEER_DHALF].astype(jnp.bfloat16)
        st = lax.dot_general(sk_ref[0, c], qc, (((1,), (1,)), ((), ())), preferred_element_type=jnp.float32)
        v, i = _topk_rows(st, None)
        sv.append(v)
        si.append(i)
    cand = _pair_rows(sv[0], sv[1], lambda a, b: a + b)
    cid = _pair_rows(si[0], si[1], lambda a, b: a * N_KEYS + b)
    fv, eid = _topk_rows(cand, cid)
    e = jnp.exp(fv - fv[0:1, :])
    gate_ref[...] = e / jnp.sum(e, axis=0, keepdims=True)
    eid_ref[...] = eid


def _peer_route(h, g, wq_bf16, sk_bf16):
    T = h.shape[0]
    rows = ROUTE_ROWS
    n_sel = PEER_HEADS * PEER_TOPK
    return pl.pallas_call(
        _peer_route_kernel,
        grid=(T // rows, PEER_HEADS),
        in_specs=[
            pl.BlockSpec((rows, D_MODEL), lambda tb, hd: (tb, 0)),
            pl.BlockSpec((1, D_MODEL), lambda tb, hd: (0, 0)),
            pl.BlockSpec((D_MODEL, 2 * PEER_DHALF), lambda tb, hd: (0, hd)),
            pl.BlockSpec((1, 2, N_KEYS, PEER_DHALF), lambda tb, hd: (hd, 0, 0, 0)),
        ],
        out_specs=[
            pl.BlockSpec((rows, D_MODEL), lambda tb, hd: (tb, 0)),
            pl.BlockSpec((PEER_TOPK, rows), lambda tb, hd: (hd, tb)),
            pl.BlockSpec((PEER_TOPK, rows), lambda tb, hd: (hd, tb)),
        ],
        out_shape=[
            jax.ShapeDtypeStruct((T, D_MODEL), jnp.float32),
            jax.ShapeDtypeStruct((n_sel, T), jnp.int32),
            jax.ShapeDtypeStruct((n_sel, T), jnp.float32),
        ],
        scratch_shapes=[pltpu.VMEM((rows, D_MODEL), jnp.bfloat16)],
        compiler_params=pltpu.CompilerParams(
            dimension_semantics=("arbitrary", "arbitrary"), vmem_limit_bytes=VMEM_LIMIT_BYTES),
        name="peer_route",
    )(h, g.reshape(1, D_MODEL), wq_bf16, sk_bf16)


N_SEL = PEER_HEADS * PEER_TOPK
SC_LANES = 16
GATHER_ROWS = PEER_TOPK
GATHERS_PER_TOKEN = N_SEL // GATHER_ROWS
GATHER_BUFS = 4
SC_TOKENS = 8


def _sc_gather_stream(wid, n_batches, tab_hbm, stage_srcs, stage_bufs, out_hbm, o_v, bufs, gsem, ssem, osem,
                      compute):
    idx_v = stage_bufs[0]

    def stage_copies(bi, slot):
        base = (wid * n_batches + bi) * SC_TOKENS
        return [pltpu.make_async_copy(src.at[pl.ds(base, SC_TOKENS)], buf.at[slot], ssem.at[slot])
                for src, buf in zip(stage_srcs, stage_bufs)]

    def out_copy(bi, slot):
        base = (wid * n_batches + bi) * SC_TOKENS
        return pltpu.make_async_copy(o_v.at[slot], out_hbm.at[pl.ds(base, SC_TOKENS)], osem.at[slot])

    def start(slot, t, kk, b):
        idx = idx_v[slot, t, pl.ds(kk * GATHER_ROWS, GATHER_ROWS)]
        pltpu.async_copy(tab_hbm.at[idx], bufs.at[b], gsem.at[b])

    def wait(b):
        pltpu.make_async_copy(tab_hbm.at[pl.ds(0, GATHER_ROWS)], bufs.at[b], gsem.at[b]).wait()

    for c in stage_copies(0, 0):
        c.start()
    for c in stage_copies(0, 0):
        c.wait()
    for q in range(GATHER_BUFS - 1):
        start(0, q // GATHERS_PER_TOKEN, q % GATHERS_PER_TOKEN, q % GATHER_BUFS)

    def batch(bi, carry):
        slot = bi % 2
        has_next = bi + 1 < n_batches

        @pl.when(has_next)
        def _():
            for c in stage_copies(bi + 1, 1 - slot):
                c.start()

        @pl.when(bi >= 2)
        def _():
            out_copy(bi, slot).wait()

        def tok(t, carry):
            @pl.when(jnp.logical_and(t == SC_TOKENS - 1, has_next))
            def _():
                for c in stage_copies(bi + 1, 1 - slot):
                    c.wait()

            for kk in range(GATHERS_PER_TOKEN):
                nq = kk + GATHER_BUFS - 1
                nk, nb = nq % GATHERS_PER_TOKEN, nq % GATHER_BUFS
                if nq < GATHERS_PER_TOKEN:
                    start(slot, t, nk, nb)
                else:
                    @pl.when(t + 1 < SC_TOKENS)
                    def _():
                        start(slot, t + 1, nk, nb)

                    @pl.when(jnp.logical_and(t + 1 == SC_TOKENS, has_next))
                    def _():
                        start(1 - slot, 0, nk, nb)

                wait(kk % GATHER_BUFS)
                compute(slot, t, kk, kk % GATHER_BUFS)
            return carry

        lax.fori_loop(0, SC_TOKENS, tok, 0)
        out_copy(bi, slot).start()
        return carry

    lax.fori_loop(0, n_batches, batch, 0)
    if n_batches >= 2:
        out_copy(n_batches - 2, (n_batches - 2) % 2).wait()
    out_copy(n_batches - 1, (n_batches - 1) % 2).wait()


def _sc_mesh_and_batches(n_tokens):
    info = plsc.get_sparse_core_info()
    assert info.num_lanes == SC_LANES
    n_workers = info.num_cores * info.num_subcores
    assert n_tokens % (n_workers * SC_TOKENS) == 0
    mesh = plsc.VectorSubcoreMesh(core_axis_name="c", subcore_axis_name="s")
    return info, mesh, n_tokens // (n_workers * SC_TOKENS)


def _peer_hidden_sc(eid, xn, u_tabs, layer):
    T = eid.shape[0]
    info, mesh, n_batches = _sc_mesh_and_batches(T)

    @functools.partial(
        pl.kernel, mesh=mesh,
        out_type=jax.ShapeDtypeStruct((T, N_SEL), jnp.float32),
        scratch_types=[
            pltpu.VMEM((2, SC_TOKENS, N_SEL), jnp.int32),
            pltpu.VMEM((2, SC_TOKENS, D_MODEL), jnp.float32),
            pltpu.VMEM((2, SC_TOKENS, N_SEL), jnp.float32),
            pltpu.VMEM((GATHER_BUFS, GATHER_ROWS, D_MODEL), jnp.float32),
            pltpu.VMEM((GATHER_ROWS, SC_LANES), jnp.float32),
            pltpu.SemaphoreType.DMA((GATHER_BUFS,)),
            pltpu.SemaphoreType.DMA((2,)),
            pltpu.SemaphoreType.DMA((2,)),
        ],
        compiler_params=pltpu.CompilerParams(needs_layout_passes=False),
        name="peer_hidden_sc",
    )
    def k(eid_hbm, xn_hbm, u_hbm, out_hbm, idx_v, x_v, o_v, bufs, acc_v, gsem, ssem, osem):
        wid = lax.axis_index("s") * info.num_cores + lax.axis_index("c")
        lane = lax.iota(jnp.int32, SC_LANES)
        zero = jnp.zeros((SC_LANES,), jnp.float32)

        def compute(slot, t, kk, b):
            @plsc.parallel_loop(0, D_MODEL // SC_LANES, carry=(zero,) * GATHER_ROWS)
            def accs(c, accs):
                xc = x_v[slot, t, pl.ds(c * SC_LANES, SC_LANES)]
                return tuple(accs[r] + bufs[b, r, pl.ds(c * SC_LANES, SC_LANES)] * xc for r in range(GATHER_ROWS))

            for r in range(GATHER_ROWS):
                acc_v[r, :] = accs[r]
            tot = zero
            for c in range(SC_LANES):
                tot = tot + plsc.load_gather(acc_v, [lane, jnp.full((SC_LANES,), c, jnp.int32)])
            o_v[slot, t, pl.ds(kk * GATHER_ROWS, GATHER_ROWS)] = tot

        _sc_gather_stream(wid, n_batches, u_hbm.at[layer], [eid_hbm, xn_hbm], [idx_v, x_v], out_hbm, o_v, bufs,
                          gsem, ssem, osem, compute)

    return k(eid, xn, u_tabs)


def _peer_combine_sc(eid, a, v_tabs, layer):
    T = eid.shape[0]
    info, mesh, n_batches = _sc_mesh_and_batches(T)

    @functools.partial(
        pl.kernel, mesh=mesh,
        out_type=jax.ShapeDtypeStruct((T, D_MODEL), jnp.float32),
        scratch_types=[
            pltpu.VMEM((2, SC_TOKENS, N_SEL), jnp.int32),
            pltpu.VMEM((2, SC_TOKENS, N_SEL), jnp.float32),
            pltpu.VMEM((2, SC_TOKENS, D_MODEL), jnp.float32),
            pltpu.VMEM((GATHER_BUFS, GATHER_ROWS, D_MODEL), jnp.float32),
            pltpu.SemaphoreType.DMA((GATHER_BUFS,)),
            pltpu.SemaphoreType.DMA((2,)),
            pltpu.SemaphoreType.DMA((2,)),
        ],
        compiler_params=pltpu.CompilerParams(needs_layout_passes=False),
        name="peer_combine_sc",
    )
    def k(eid_hbm, a_hbm, v_hbm, out_hbm, idx_v, a_v, o_v, bufs, gsem, ssem, osem):
        wid = lax.axis_index("s") * info.num_cores + lax.axis_index("c")

        def compute(slot, t, kk, b):
            svec = jnp.full((SC_LANES,), slot, jnp.int32)
            tvec = jnp.full((SC_LANES,), t, jnp.int32)
            ws = [plsc.load_gather(a_v, [svec, tvec, jnp.full((SC_LANES,), kk * GATHER_ROWS + r, jnp.int32)])
                  for r in range(GATHER_ROWS)]

            @plsc.parallel_loop(0, D_MODEL // SC_LANES, unroll=2)
            def _(c):
                sl = pl.ds(c * SC_LANES, SC_LANES)
                terms = [ws[r] * bufs[b, r, sl] for r in range(GATHER_ROWS)]
                if kk != 0:
                    terms.append(o_v[slot, t, sl])
                while len(terms) > 1:
                    pairs = [terms[i] + terms[i + 1] for i in range(0, len(terms) - 1, 2)]
                    terms = pairs + ([terms[-1]] if len(terms) % 2 else [])
                o_v[slot, t, sl] = terms[0]

        _sc_gather_stream(wid, n_batches, v_hbm.at[layer], [eid_hbm, a_hbm], [idx_v, a_v], out_hbm, o_v, bufs,
                          gsem, ssem, osem, compute)

    return k(eid, a, v_tabs)


def _peer_act_kernel(hp_ref, gate_ref, a_ref):
    a_ref[...] = _gelu(hp_ref[...]) * gate_ref[...]


ELEMENTWISE_ROWS = 1024


def _peer_act(hpre, gate):
    T = hpre.shape[0]
    rows = ELEMENTWISE_ROWS
    assert T % rows == 0
    spec = pl.BlockSpec((rows, N_SEL), lambda i: (i, 0))
    return pl.pallas_call(
        _peer_act_kernel, grid=(T // rows,), in_specs=[spec, spec], out_specs=spec,
        out_shape=jax.ShapeDtypeStruct((T, N_SEL), jnp.float32),
        compiler_params=pltpu.CompilerParams(dimension_semantics=("arbitrary",)),
        name="peer_act",
    )(hpre, gate)


def _residual_kernel(h_ref, c_ref, g_ref, o_ref, *, final_norm):
    y = h_ref[...] + c_ref[...]
    o_ref[...] = _rmsnorm_rows(y, g_ref[...]) if final_norm else y


def _residual(h, c, gfin, *, final_norm):
    T = h.shape[0]
    rows = ELEMENTWISE_ROWS
    assert T % rows == 0
    spec = pl.BlockSpec((rows, D_MODEL), lambda i: (i, 0))
    return pl.pallas_call(
        functools.partial(_residual_kernel, final_norm=final_norm), grid=(T // rows,),
        in_specs=[spec, spec, pl.BlockSpec((1, D_MODEL), lambda i: (0, 0))], out_specs=spec,
        out_shape=jax.ShapeDtypeStruct((T, D_MODEL), jnp.float32),
        compiler_params=pltpu.CompilerParams(dimension_semantics=("arbitrary",)),
        name="peer_residual",
    )(h, c, gfin.reshape(1, D_MODEL))


def _peer_experts(eid, xn, gate, h, gfin, u_tabs, v_tabs, layer, *, final_norm):
    hpre = _peer_hidden_sc(eid, xn, u_tabs, layer)
    a = _peer_act(hpre, gate)
    c = _peer_combine_sc(eid, a, v_tabs, layer)
    return _residual(h, c, gfin, final_norm=final_norm)


KV_WIDTH = N_KV_HEADS * HEAD_DIM
BF16 = jnp.bfloat16


def _qkv(h, gkv, gq, wkv_ref, wq_ref):
    kv = jnp.dot(_rmsnorm_rows(h, gkv).astype(BF16), wkv_ref[...], preferred_element_type=jnp.float32)
    q = jnp.dot(_rmsnorm_rows(h, gq).astype(BF16), wq_ref[...], preferred_element_type=jnp.float32)
    return kv, q


def _sink_softmax_pv(parts, sink):
    m = sink
    for s, _ in parts:
        m = jnp.maximum(m, jnp.max(s, axis=-1, keepdims=True))
    den = jnp.exp(sink - m)
    acc = None
    for s, v in parts:
        e = jnp.exp(s - m)
        den = den + jnp.sum(e, axis=-1, keepdims=True)
        pv = jnp.dot(e.astype(BF16), v.astype(BF16), preferred_element_type=jnp.float32)
        acc = pv if acc is None else acc + pv
    return acc / den


def _nt_dot(a, b):
    return lax.dot_general(a.astype(BF16), b.astype(BF16), (((1,), (1,)), ((), ())),
                           preferred_element_type=jnp.float32)


def _attn_prompt_kernel(h_ref, gkv_ref, gq_ref, wkv_ref, wq_ref, wo_ref, sink_ref,
                        o_ref, kvw_ref, prev_ref):
    blk = pl.program_id(1)
    h = h_ref[...]
    kv, q = _qkv(h, gkv_ref[...], gq_ref[...], wkv_ref, wq_ref)
    kvw_ref[0] = kv

    @pl.when(blk == 0)
    def _():
        prev_ref[...] = jnp.zeros_like(prev_ref)

    prev = prev_ref[...]
    qi = lax.broadcasted_iota(jnp.int32, (WINDOW, WINDOW), 0)
    kj = lax.broadcasted_iota(jnp.int32, (WINDOW, WINDOW), 1)
    prev_ok = jnp.logical_and(kj > qi, blk > 0)
    cur_ok = kj <= qi
    heads = []
    for kvh in range(N_KV_HEADS):
        ks = slice(kvh * HEAD_DIM, (kvh + 1) * HEAD_DIM)
        vs = slice(KV_WIDTH + kvh * HEAD_DIM, KV_WIDTH + (kvh + 1) * HEAD_DIM)
        for g in range(Q_PER_KV):
            hq = kvh * Q_PER_KV + g
            qh = q[:, hq * HEAD_DIM:(hq + 1) * HEAD_DIM]
            sp = jnp.where(prev_ok, _nt_dot(qh, prev[:, ks]) * ATTN_SCALE, -jnp.inf)
            sc = jnp.where(cur_ok, _nt_dot(qh, kv[:, ks]) * ATTN_SCALE, -jnp.inf)
            heads.append(_sink_softmax_pv([(sp, prev[:, vs]), (sc, kv[:, vs])], sink_ref[hq]))
    o = jnp.concatenate(heads, axis=1)
    o_ref[...] = h + jnp.dot(o.astype(BF16), wo_ref[...], preferred_element_type=jnp.float32)
    prev_ref[...] = kv


def _attn_prompt(h, gkv, gq, wkv, wq, wo, sinks, *, n_seq, seq_len):
    nb = seq_len // WINDOW
    row_spec = pl.BlockSpec((WINDOW, D_MODEL), lambda n, b: (n * nb + b, 0))
    full = lambda shape: pl.BlockSpec(shape, lambda n, b: (0,) * len(shape))
    return pl.pallas_call(
        _attn_prompt_kernel,
        grid=(n_seq, nb),
        in_specs=[
            row_spec, full((1, D_MODEL)), full((1, D_MODEL)),
            full((D_MODEL, 2 * KV_WIDTH)), full((D_MODEL, D_MODEL)), full((D_MODEL, D_MODEL)),
            pl.BlockSpec(memory_space=pltpu.SMEM),
        ],
        out_specs=[row_spec, pl.BlockSpec((1, WINDOW, 2 * KV_WIDTH), lambda n, b: (n, 0, 0))],
        out_shape=[
            jax.ShapeDtypeStruct(h.shape, jnp.float32),
            jax.ShapeDtypeStruct((n_seq, WINDOW, 2 * KV_WIDTH), jnp.float32),
        ],
        scratch_shapes=[pltpu.VMEM((WINDOW, 2 * KV_WIDTH), jnp.float32)],
        compiler_params=pltpu.CompilerParams(
            dimension_semantics=("arbitrary", "arbitrary"), vmem_limit_bytes=VMEM_LIMIT_BYTES),
        name="attn_prompt",
    )(h, gkv.reshape(1, D_MODEL), gq.reshape(1, D_MODEL), wkv, wq, wo, sinks)


ATTN_SEQS = 16


def _attn_sample_kernel(h_ref, ck_ref, cv_ref, gkv_ref, gq_ref, wkv_ref, wq_ref, wo_ref, sink_ref,
                        o_ref, kw_ref, vw_ref, kv_ref, q_ref, att_ref, *, n_new):
    h = h_ref[...]
    kv, q = _qkv(h, gkv_ref[...], gq_ref[...], wkv_ref, wq_ref)
    kv_ref[...] = kv
    q_ref[...] = q
    n_seq = h.shape[0] // n_new
    rows = Q_PER_KV * n_new
    qpos_c = lax.broadcasted_iota(jnp.int32, (rows, WINDOW), 0) % n_new
    cache_ok = lax.broadcasted_iota(jnp.int32, (rows, WINDOW), 1) > qpos_c
    qpos_n = lax.broadcasted_iota(jnp.int32, (rows, n_new), 0) % n_new
    new_ok = lax.broadcasted_iota(jnp.int32, (rows, n_new), 1) <= qpos_n

    def per_seq(n, carry):
        r0 = pl.multiple_of(n * n_new, n_new)
        kvn = kv_ref[pl.ds(r0, n_new), :]
        qn = q_ref[pl.ds(r0, n_new), :]
        ck = ck_ref[n]
        cv = cv_ref[n]
        kw_ref[n, 0:WINDOW - n_new, :] = ck[n_new:, :]
        kw_ref[n, WINDOW - n_new:WINDOW, :] = kvn[:, 0:KV_WIDTH]
        vw_ref[n, 0:WINDOW - n_new, :] = cv[n_new:, :]
        vw_ref[n, WINDOW - n_new:WINDOW, :] = kvn[:, KV_WIDTH:2 * KV_WIDTH]
        outs = []
        for kvh in range(N_KV_HEADS):
            ks = slice(kvh * HEAD_DIM, (kvh + 1) * HEAD_DIM)
            vs = slice(KV_WIDTH + kvh * HEAD_DIM, KV_WIDTH + (kvh + 1) * HEAD_DIM)
            qs = jnp.concatenate(
                [qn[:, (kvh * Q_PER_KV + g) * HEAD_DIM:(kvh * Q_PER_KV + g + 1) * HEAD_DIM] for g in range(Q_PER_KV)],
                axis=0)
            s_c = jnp.where(cache_ok, _nt_dot(qs, ck[:, ks]) * ATTN_SCALE, -jnp.inf)
            s_n = jnp.where(new_ok, _nt_dot(qs, kvn[:, ks]) * ATTN_SCALE, -jnp.inf)
            sink = sink_ref[kvh][:, 0:1]
            o = _sink_softmax_pv([(s_c, cv[:, ks]), (s_n, kvn[:, vs])], sink)
            outs += [o[g * n_new:(g + 1) * n_new, :] for g in range(Q_PER_KV)]
        att_ref[pl.ds(r0, n_new), :] = jnp.concatenate(outs, axis=1)
        return carry

    lax.fori_loop(0, n_seq, per_seq, 0)
    o_ref[...] = h + jnp.dot(att_ref[...].astype(BF16), wo_ref[...], preferred_element_type=jnp.float32)


def _attn_sample(h, cache_k, cache_v, gkv, gq, wkv, wq, wo, sinks, *, n_new):
    n_seq = cache_k.shape[0]
    sb = ATTN_SEQS
    rows = sb * n_new
    row_spec = pl.BlockSpec((rows, D_MODEL), lambda i: (i, 0))
    win_spec = pl.BlockSpec((sb, WINDOW, KV_WIDTH), lambda i: (i, 0, 0))
    full = lambda shape: pl.BlockSpec(shape, lambda i: (0,) * len(shape))
    sink_rows = jnp.repeat(sinks.reshape(N_KV_HEADS, Q_PER_KV), n_new, axis=1)[:, :, None]
    sink_rows = jnp.broadcast_to(sink_rows, (N_KV_HEADS, Q_PER_KV * n_new, LANES))
    return pl.pallas_call(
        functools.partial(_attn_sample_kernel, n_new=n_new),
        grid=(n_seq // sb,),
        in_specs=[
            row_spec, win_spec, win_spec, full((1, D_MODEL)), full((1, D_MODEL)),
            full((D_MODEL, 2 * KV_WIDTH)), full((D_MODEL, D_MODEL)), full((D_MODEL, D_MODEL)),
            full((N_KV_HEADS, Q_PER_KV * n_new, LANES)),
        ],
        out_specs=[row_spec, win_spec, win_spec],
        out_shape=[
            jax.ShapeDtypeStruct(h.shape, jnp.float32),
            jax.ShapeDtypeStruct((n_seq, WINDOW, KV_WIDTH), jnp.float32),
            jax.ShapeDtypeStruct((n_seq, WINDOW, KV_WIDTH), jnp.float32),
        ],
        scratch_shapes=[
            pltpu.VMEM((rows, 2 * KV_WIDTH), jnp.float32),
            pltpu.VMEM((rows, D_MODEL), jnp.float32),
            pltpu.VMEM((rows, D_MODEL), jnp.float32),
        ],
        compiler_params=pltpu.CompilerParams(
            dimension_semantics=("arbitrary",), vmem_limit_bytes=VMEM_LIMIT_BYTES),
        name="attn_sample",
    )(h, cache_k, cache_v, gkv.reshape(1, D_MODEL), gq.reshape(1, D_MODEL), wkv, wq, wo, sink_rows)


def kernel(x_prompt, x_sample, state_ssm_re, state_ssm_im, cache_k_win, cache_v_win, norm_mix, norm_ffn, norm_kv, norm_final, ssm_lam_re, ssm_lam_im, ssm_log_dt, ssm_b_re, ssm_b_im, ssm_c_re, ssm_c_im, ssm_d, ssm_w_glu, w_kv, w_q, attn_sinks, w_o, peer_w_q, peer_sub_keys, peer_u, peer_v):
    bmat, cmat, apr, api = _s5_discretize(ssm_lam_re[0], ssm_lam_im[0], ssm_log_dt[0], ssm_b_re[0], ssm_b_im[0], ssm_c_re[0], ssm_c_im[0])
    cmat = cmat.astype(jnp.bfloat16)
    wglu = ssm_w_glu[0].astype(jnp.bfloat16)
    wkv = w_kv.astype(BF16)
    wq = w_q[0].astype(BF16)
    wo = w_o[0].astype(BF16)
    peer_wq = [peer_w_q[layer].astype(BF16) for layer in range(2)]
    peer_sk = [peer_sub_keys[layer].astype(BF16) for layer in range(2)]

    def peer(h, layer, final_norm):
        xn, eid_t, gate_t = _peer_route(h, norm_ffn[layer], peer_wq[layer], peer_sk[layer])
        return _peer_experts(eid_t.T, xn, gate_t.T, h, norm_final, peer_u, peer_v, layer, final_norm=final_norm)

    win = lambda a: a.reshape(a.shape[0], WINDOW, N_KV_HEADS, HEAD_DIM)

    seq_len = x_prompt.shape[1]
    xp_all = x_prompt.reshape(-1, D_MODEL)

    def trunk_prompt(seq0, n_p):
        row0 = seq0 * seq_len
        z0 = jnp.zeros((n_p, N_CHUNKS, 1, CHUNK_STATE), jnp.float32)
        z, sr, si = _s5_mixer(xp_all, z0, z0, norm_mix[0], ssm_d[0], bmat, cmat, apr, api,
                              chain=True, seq_len=seq_len, row0=row0)
        h2 = peer(_glu_residual(z, xp_all, wglu, res_row0=row0), 0, False)
        h3, kvw = _attn_prompt(h2, norm_kv, norm_mix[1], wkv, wq, wo, attn_sinks[0], n_seq=n_p, seq_len=seq_len)
        y = peer(h3, 1, True)
        return (y.reshape(n_p, seq_len, D_MODEL), _chunks_to_state(sr), _chunks_to_state(si),
                win(kvw[:, :, :KV_WIDTH]), win(kvw[:, :, KV_WIDTH:]))

    def trunk_sample(x_s, s_re, s_im, c_k, c_v):
        n_s = x_s.shape[0]
        xs = x_s.reshape(-1, D_MODEL)
        z, sr, si = _s5_mixer(xs, _state_to_chunks(s_re), _state_to_chunks(s_im), norm_mix[0], ssm_d[0],
                              bmat, cmat, apr, api, chain=False, seq_len=x_s.shape[1])
        h2 = peer(_glu_residual(z, xs, wglu), 0, False)
        h3, kw, vw = _attn_sample(h2, c_k.reshape(n_s, WINDOW, KV_WIDTH), c_v.reshape(n_s, WINDOW, KV_WIDTH),
                                  norm_kv, norm_mix[1], wkv, wq, wo, attn_sinks[0], n_new=x_s.shape[1])
        y = peer(h3, 1, True)
        return y.reshape(x_s.shape), _chunks_to_state(sr), _chunks_to_state(si), win(kw), win(vw)

    y_s, sre_s, sim_s, kw_s, vw_s = trunk_sample(x_sample, state_ssm_re[0], state_ssm_im[0], cache_k_win, cache_v_win)
    n_groups = 4
    gp = x_prompt.shape[0] // n_groups
    outs = [trunk_prompt(i * gp, gp) for i in range(n_groups)]
    y_p, sre_p, sim_p, kw_p, vw_p = [jnp.concatenate(parts, axis=0) for parts in zip(*outs)]
    return (y_p, y_s, sre_p[None], sim_p[None], kw_p, vw_p, sre_s[None], sim_s[None], kw_s, vw_s)
```

```python
import functools
import math

import jax
import jax.numpy as jnp
from jax import lax
from jax.experimental import pallas as pl
from jax.experimental.pallas import tpu as pltpu
from jax.experimental.pallas import tpu_sc as plsc

D_MODEL = 1024
GROUP_SIZE = 16
N_GROUPS = D_MODEL // GROUP_SIZE
STATE_DIM = 64
HEAD_DIM = 64
N_Q_HEADS = D_MODEL // HEAD_DIM
N_KV_HEADS = N_Q_HEADS // 8
Q_PER_KV = N_Q_HEADS // N_KV_HEADS
WINDOW = 128
ATTN_SCALE = 1.0 / math.sqrt(HEAD_DIM)
PEER_HEADS = 8
N_KEYS = 128
PEER_TOPK = 16
PEER_DHALF = 128
EPS = 1e-5

LANES = 128
SUBLANES = 8
VMEM_LIMIT_BYTES = 56 * 1024 * 1024

GROUPS_PER_CHUNK = LANES // GROUP_SIZE
N_CHUNKS = N_GROUPS // GROUPS_PER_CHUNK
CHUNK_STATE = GROUPS_PER_CHUNK * STATE_DIM
S5_ROWS = 256


def _rmsnorm_rows(x, g):
    r = lax.rsqrt(jnp.mean(x * x, axis=-1, keepdims=True) + EPS)
    return x * r * g


def _gelu(x):
    return 0.5 * x * (1.0 + lax.erf(x * (1.0 / math.sqrt(2.0))))


def _s5_discretize(lam_re, lam_im, log_dt, b_re, b_im, c_re, c_im):
    f32 = jnp.float32
    lr = lam_re.astype(f32)
    li = lam_im.astype(f32)
    dt = jnp.exp(log_dt.astype(f32))[:, None]
    mag = jnp.exp(lr * dt)
    ab_re = mag * jnp.cos(li * dt)
    ab_im = mag * jnp.sin(li * dt)
    den = lr * lr + li * li
    f_re = ((ab_re - 1.0) * lr + ab_im * li) / den
    f_im = (ab_im * lr - (ab_re - 1.0) * li) / den
    br = b_re.astype(f32)
    bi = b_im.astype(f32)
    bb_re = f_re[..., None] * br - f_im[..., None] * bi
    bb_im = f_re[..., None] * bi + f_im[..., None] * br
    eye = jnp.eye(GROUPS_PER_CHUNK, dtype=f32)

    def chunk_rows(v):
        return v.reshape(N_CHUNKS, 1, CHUNK_STATE)

    def in_blocks(bb):
        t = bb.reshape(N_CHUNKS, GROUPS_PER_CHUNK, STATE_DIM, GROUP_SIZE).transpose(0, 1, 3, 2)
        return jnp.einsum('mgjp,gh->mgjhp', t, eye).reshape(N_CHUNKS, LANES, CHUNK_STATE)

    def out_blocks(c):
        t = c.astype(f32).reshape(N_CHUNKS, GROUPS_PER_CHUNK, GROUP_SIZE, STATE_DIM).transpose(0, 1, 3, 2)
        return jnp.einsum('mgpj,gh->mgphj', t, eye).reshape(N_CHUNKS, CHUNK_STATE, LANES)

    bfull = jnp.concatenate([in_blocks(bb_re), in_blocks(bb_im)], axis=2)
    b_hi = bfull.astype(jnp.bfloat16)
    b_lo = (bfull - b_hi.astype(f32)).astype(jnp.bfloat16)
    bmat = (jnp.concatenate([b_hi, b_hi], axis=1), b_lo)
    cmat = jnp.concatenate([out_blocks(c_re), -out_blocks(c_im)], axis=1)
    pr, pi = [ab_re], [ab_im]
    for _ in range(SUBLANES - 1):
        pr, pi = pr + [pr[-1] * ab_re - pi[-1] * ab_im], pi + [pr[-1] * ab_im + pi[-1] * ab_re]
    apr = jnp.concatenate([chunk_rows(v) for v in pr], axis=1)
    api = jnp.concatenate([chunk_rows(v) for v in pi], axis=1)
    return bmat, cmat, apr, api


def _split_bf16(x):
    hi = x.astype(jnp.bfloat16)
    return hi, (x - hi.astype(jnp.float32)).astype(jnp.bfloat16)


def _s5_kernel(x_ref, g_ref, d_ref, bhh_ref, blo_ref, c_ref, apr_ref, api_ref, h0r_ref, h0i_ref,
               z_ref, sr_ref, si_ref, u_ref, us_ref, st_ref, cr_ref, ci_ref, *, chain, blocks_per_seq):
    rb = pl.program_id(0)
    m = pl.program_id(1)
    rows = x_ref.shape[0]
    n_tiles = rows // SUBLANES

    @pl.when(m == 0)
    def _():
        u = _rmsnorm_rows(x_ref[...], g_ref[...])
        for mm in range(N_CHUNKS):
            uc = u[:, mm * LANES:(mm + 1) * LANES]
            u_ref[mm] = uc
            hi, lo = _split_bf16(uc)
            us_ref[mm, :, 0:LANES] = hi
            us_ref[mm, :, LANES:2 * LANES] = lo

    u = u_ref[m]
    us = us_ref[m]
    st_ref[...] = (jnp.dot(us, bhh_ref[0], preferred_element_type=jnp.float32)
                   + jnp.dot(us[:, 0:LANES], blo_ref[0], preferred_element_type=jnp.float32))

    apr = apr_ref[0]
    api = api_ref[0]
    row = lax.broadcasted_iota(jnp.int32, (SUBLANES, CHUNK_STATE), 0)

    if chain:
        @pl.when(rb % blocks_per_seq == 0)
        def _():
            cr_ref[m] = h0r_ref[0, 0]
            ci_ref[m] = h0i_ref[0, 0]

    def tile_step(k, carry):
        r0 = pl.multiple_of(k * SUBLANES, SUBLANES)
        xr = st_ref[pl.ds(r0, SUBLANES), 0:CHUNK_STATE]
        xi = st_ref[pl.ds(r0, SUBLANES), CHUNK_STATE:2 * CHUNK_STATE]
        for d in (1, 2, 4):
            ar = apr[d - 1:d, :]
            ai = api[d - 1:d, :]
            sr = jnp.where(row >= d, pltpu.roll(xr, d, axis=0), 0.0)
            si = jnp.where(row >= d, pltpu.roll(xi, d, axis=0), 0.0)
            xr, xi = xr + ar * sr - ai * si, xi + ar * si + ai * sr
        if chain:
            cr, ci = carry
        else:
            cr = h0r_ref[k, 0]
            ci = h0i_ref[k, 0]
        hr = xr + apr * cr - api * ci
        hi = xi + apr * ci + api * cr
        st_ref[pl.ds(r0, SUBLANES), 0:CHUNK_STATE] = hr
        st_ref[pl.ds(r0, SUBLANES), CHUNK_STATE:2 * CHUNK_STATE] = hi
        lr_ = hr[SUBLANES - 1:SUBLANES, :]
        li_ = hi[SUBLANES - 1:SUBLANES, :]
        if chain:
            return lr_, li_
        sr_ref[k, m] = lr_
        si_ref[k, m] = li_
        return carry

    if chain:
        cr, ci = lax.fori_loop(0, n_tiles, tile_step, (cr_ref[m], ci_ref[m]))
        cr_ref[m] = cr
        ci_ref[m] = ci
        sr_ref[0, m] = cr
        si_ref[0, m] = ci
    else:
        lax.fori_loop(0, n_tiles, tile_step, 0)

    y = jnp.dot(st_ref[...].astype(jnp.bfloat16), c_ref[0], preferred_element_type=jnp.float32)
    y = y + d_ref[0] * u
    z_ref[...] = _gelu(y)


def _s5_mixer(x, h0r, h0i, g, d_skip, bmat, cmat_bf16, apr, api, *, chain, seq_len, row0=0):
    nseq = h0r.shape[0]
    T = nseq * seq_len
    rows = S5_ROWS
    blk0 = row0 // rows
    if chain:
        blocks_per_seq = seq_len // rows
        seq_blk = 1
        seq_map = lambda rb, m: (rb // blocks_per_seq, m, 0, 0)
        out_map = lambda rb, m: (rb // blocks_per_seq, 0, 0, 0)
    else:
        assert seq_len == SUBLANES
        blocks_per_seq = 1
        seq_blk = rows // SUBLANES
        seq_map = lambda rb, m: (rb, m, 0, 0)
        out_map = lambda rb, m: (rb, 0, 0, 0)
    grid = (T // rows, N_CHUNKS)
    kern = functools.partial(_s5_kernel, chain=chain, blocks_per_seq=blocks_per_seq)
    st_spec = pl.BlockSpec((seq_blk, 1, 1, CHUNK_STATE), seq_map)
    out_st_spec = pl.BlockSpec((seq_blk, N_CHUNKS, 1, CHUNK_STATE), out_map)
    z, sr, si = pl.pallas_call(
        kern,
        grid=grid,
        in_specs=[
            pl.BlockSpec((rows, D_MODEL), lambda rb, m: (rb + blk0, 0)),
            pl.BlockSpec((1, D_MODEL), lambda rb, m: (0, 0)),
            pl.BlockSpec((1, 1, LANES), lambda rb, m: (m, 0, 0)),
            pl.BlockSpec((1, 2 * LANES, 2 * CHUNK_STATE), lambda rb, m: (m, 0, 0)),
            pl.BlockSpec((1, LANES, 2 * CHUNK_STATE), lambda rb, m: (m, 0, 0)),
            pl.BlockSpec((1, 2 * CHUNK_STATE, LANES), lambda rb, m: (m, 0, 0)),
            pl.BlockSpec((1, SUBLANES, CHUNK_STATE), lambda rb, m: (m, 0, 0)),
            pl.BlockSpec((1, SUBLANES, CHUNK_STATE), lambda rb, m: (m, 0, 0)),
            st_spec, st_spec,
        ],
        out_specs=[
            pl.BlockSpec((rows, LANES), lambda rb, m: (rb, m)),
            out_st_spec, out_st_spec,
        ],
        out_shape=[
            jax.ShapeDtypeStruct((T, D_MODEL), jnp.float32),
            jax.ShapeDtypeStruct((nseq, N_CHUNKS, 1, CHUNK_STATE), jnp.float32),
            jax.ShapeDtypeStruct((nseq, N_CHUNKS, 1, CHUNK_STATE), jnp.float32),
        ],
        scratch_shapes=[
            pltpu.VMEM((N_CHUNKS, rows, LANES), jnp.float32),
            pltpu.VMEM((N_CHUNKS, rows, 2 * LANES), jnp.bfloat16),
            pltpu.VMEM((rows, 2 * CHUNK_STATE), jnp.float32),
            pltpu.VMEM((N_CHUNKS, 1, CHUNK_STATE), jnp.float32),
            pltpu.VMEM((N_CHUNKS, 1, CHUNK_STATE), jnp.float32),
        ],
        compiler_params=pltpu.CompilerParams(
            dimension_semantics=("arbitrary", "arbitrary"), vmem_limit_bytes=VMEM_LIMIT_BYTES),
        name="s5_mixer",
    )(x, g.reshape(1, D_MODEL), d_skip.reshape(N_CHUNKS, 1, LANES), bmat[0], bmat[1], cmat_bf16, apr, api,
      h0r, h0i)
    return z, sr, si


def _state_to_chunks(h):
    return h.reshape(h.shape[0], N_CHUNKS, 1, CHUNK_STATE)


def _chunks_to_state(s):
    return s.reshape(s.shape[0], N_GROUPS, STATE_DIM)


def _glu_kernel(z_ref, res_ref, w_ref, o_ref):
    zz = jnp.dot(z_ref[...].astype(jnp.bfloat16), w_ref[...], preferred_element_type=jnp.float32)
    a = zz[:, :D_MODEL]
    b = zz[:, D_MODEL:]
    o_ref[...] = res_ref[...] + a * (1.0 / (1.0 + jnp.exp(-b)))


def _glu_residual(z, res, w_bf16, res_row0=0):
    T = z.shape[0]
    rows = math.gcd(T, 512)
    blk0 = res_row0 // rows
    return pl.pallas_call(
        _glu_kernel,
        grid=(T // rows,),
        in_specs=[
            pl.BlockSpec((rows, D_MODEL), lambda i: (i, 0)),
            pl.BlockSpec((rows, D_MODEL), lambda i: (i + blk0, 0)),
            pl.BlockSpec((D_MODEL, 2 * D_MODEL), lambda i: (0, 0)),
        ],
        out_specs=pl.BlockSpec((rows, D_MODEL), lambda i: (i, 0)),
        out_shape=jax.ShapeDtypeStruct((T, D_MODEL), jnp.float32),
        compiler_params=pltpu.CompilerParams(
            dimension_semantics=("arbitrary",), vmem_limit_bytes=VMEM_LIMIT_BYTES),
        name="glu_residual",
    )(z, res, w_bf16)


ROUTE_ROWS = 256


def _topk_rows(s, payload):
    n_rows = s.shape[0]
    row = lax.broadcasted_iota(jnp.int32, s.shape, 0)
    vals, picks = [], []
    for _ in range(PEER_TOPK):
        m = jnp.max(s, axis=0, keepdims=True)
        pos = jnp.min(jnp.where(s == m, row, n_rows), axis=0, keepdims=True)
        sel = row == pos
        vals.append(m)
        if payload is None:
            picks.append(pos)
        else:
            picks.append(jnp.max(jnp.where(sel, payload, -1), axis=0, keepdims=True))
        s = jnp.where(sel, -jnp.inf, s)
    return jnp.concatenate(vals, axis=0), jnp.concatenate(picks, axis=0)


def _pair_rows(a0, a1, combine):
    half = PEER_TOPK // 2
    parts = [combine(a0[0:1, :], a1)]
    parts += [combine(a0[i:i + 1, :], a1[0:half, :]) for i in range(1, half)]
    parts.append(combine(a0[half:PEER_TOPK, :], a1[0:1, :]))
    return jnp.concatenate(parts, axis=0)


def _peer_route_kernel(h_ref, g_ref, wq_ref, sk_ref, xn_ref, eid_ref, gate_ref, xb_ref):
    hd = pl.program_id(1)

    @pl.when(hd == 0)
    def _():
        xn = _rmsnorm_rows(h_ref[...], g_ref[...])
        xn_ref[...] = xn
        xb_ref[...] = xn.astype(jnp.bfloat16)

    q = jnp.dot(xb_ref[...], wq_ref[...], preferred_element_type=jnp.float32)
    sv, si = [], []
    for c in range(2):
        qc = q[:, c * PEER_DHALF:(c + 1) * PEER_DHALF].astype(jnp.bfloat16)
        st = lax.dot_general(sk_ref[0, c], qc, (((1,), (1,)), ((), ())), preferred_element_type=jnp.float32)
        v, i = _topk_rows(st, None)
        sv.append(v)
        si.append(i)
    cand = _pair_rows(sv[0], sv[1], lambda a, b: a + b)
    cid = _pair_rows(si[0], si[1], lambda a, b: a * N_KEYS + b)
    fv, eid = _topk_rows(cand, cid)
    e = jnp.exp(fv - fv[0:1, :])
    gate_ref[...] = e / jnp.sum(e, axis=0, keepdims=True)
    eid_ref[...] = eid


def _peer_route(h, g, wq_bf16, sk_bf16):
    T = h.shape[0]
    rows = ROUTE_ROWS
    n_sel = PEER_HEADS * PEER_TOPK
    return pl.pallas_call(
        _peer_route_kernel,
        grid=(T // rows, PEER_HEADS),
        in_specs=[
            pl.BlockSpec((rows, D_MODEL), lambda tb, hd: (tb, 0)),
            pl.BlockSpec((1, D_MODEL), lambda tb, hd: (0, 0)),
            pl.BlockSpec((D_MODEL, 2 * PEER_DHALF), lambda tb, hd: (0, hd)),
            pl.BlockSpec((1, 2, N_KEYS, PEER_DHALF), lambda tb, hd: (hd, 0, 0, 0)),
        ],
        out_specs=[
            pl.BlockSpec((rows, D_MODEL), lambda tb, hd: (tb, 0)),
            pl.BlockSpec((PEER_TOPK, rows), lambda tb, hd: (hd, tb)),
            pl.BlockSpec((PEER_TOPK, rows), lambda tb, hd: (hd, tb)),
        ],
        out_shape=[
            jax.ShapeDtypeStruct((T, D_MODEL), jnp.float32),
            jax.ShapeDtypeStruct((n_sel, T), jnp.int32),
            jax.ShapeDtypeStruct((n_sel, T), jnp.float32),
        ],
        scratch_shapes=[pltpu.VMEM((rows, D_MODEL), jnp.bfloat16)],
        compiler_params=pltpu.CompilerParams(
            dimension_semantics=("arbitrary", "arbitrary"), vmem_limit_bytes=VMEM_LIMIT_BYTES),
        name="peer_route",
    )(h, g.reshape(1, D_MODEL), wq_bf16, sk_bf16)


N_SEL = PEER_HEADS * PEER_TOPK
SC_LANES = 16
GATHER_ROWS = PEER_TOPK
GATHERS_PER_TOKEN = N_SEL // GATHER_ROWS
GATHER_BUFS = 4
SC_TOKENS = 8


def _sc_gather_stream(wid, n_batches, tab_hbm, stage_srcs, stage_bufs, out_hbm, o_v, bufs, gsem, ssem, osem,
                      compute):
    idx_v = stage_bufs[0]

    def stage_copies(bi, slot):
        base = (wid * n_batches + bi) * SC_TOKENS
        return [pltpu.make_async_copy(src.at[pl.ds(base, SC_TOKENS)], buf.at[slot], ssem.at[slot])
                for src, buf in zip(stage_srcs, stage_bufs)]

    def out_copy(bi, slot):
        base = (wid * n_batches + bi) * SC_TOKENS
        return pltpu.make_async_copy(o_v.at[slot], out_hbm.at[pl.ds(base, SC_TOKENS)], osem.at[slot])

    def start(slot, t, kk, b):
        idx = idx_v[slot, t, pl.ds(kk * GATHER_ROWS, GATHER_ROWS)]
        pltpu.async_copy(tab_hbm.at[idx], bufs.at[b], gsem.at[b])

    def wait(b):
        pltpu.make_async_copy(tab_hbm.at[pl.ds(0, GATHER_ROWS)], bufs.at[b], gsem.at[b]).wait()

    for c in stage_copies(0, 0):
        c.start()
    for c in stage_copies(0, 0):
        c.wait()
    for q in range(GATHER_BUFS - 1):
        start(0, q // GATHERS_PER_TOKEN, q % GATHERS_PER_TOKEN, q % GATHER_BUFS)

    def batch(bi, carry):
        slot = bi % 2
        has_next = bi + 1 < n_batches

        @pl.when(has_next)
        def _():
            for c in stage_copies(bi + 1, 1 - slot):
                c.start()

        @pl.when(bi >= 2)
        def _():
            out_copy(bi, slot).wait()

        def tok(t, carry):
            @pl.when(jnp.logical_and(t == SC_TOKENS - 1, has_next))
            def _():
                for c in stage_copies(bi + 1, 1 - slot):
                    c.wait()

            for kk in range(GATHERS_PER_TOKEN):
                nq = kk + GATHER_BUFS - 1
                nk, nb = nq % GATHERS_PER_TOKEN, nq % GATHER_BUFS
                if nq < GATHERS_PER_TOKEN:
                    start(slot, t, nk, nb)
                else:
                    @pl.when(t + 1 < SC_TOKENS)
                    def _():
                        start(slot, t + 1, nk, nb)

                    @pl.when(jnp.logical_and(t + 1 == SC_TOKENS, has_next))
                    def _():
                        start(1 - slot, 0, nk, nb)

                wait(kk % GATHER_BUFS)
                compute(slot, t, kk, kk % GATHER_BUFS)
            return carry

        lax.fori_loop(0, SC_TOKENS, tok, 0)
        out_copy(bi, slot).start()
        return carry

    lax.fori_loop(0, n_batches, batch, 0)
    if n_batches >= 2:
        out_copy(n_batches - 2, (n_batches - 2) % 2).wait()
    out_copy(n_batches - 1, (n_batches - 1) % 2).wait()


def _sc_mesh_and_batches(n_tokens):
    info = plsc.get_sparse_core_info()
    assert info.num_lanes == SC_LANES
    n_workers = info.num_cores * info.num_subcores
    assert n_tokens % (n_workers * SC_TOKENS) == 0
    mesh = plsc.VectorSubcoreMesh(core_axis_name="c", subcore_axis_name="s")
    return info, mesh, n_tokens // (n_workers * SC_TOKENS)


def _peer_hidden_sc(eid, xn, u_tabs, layer):
    T = eid.shape[0]
    info, mesh, n_batches = _sc_mesh_and_batches(T)

    @functools.partial(
        pl.kernel, mesh=mesh,
        out_type=jax.ShapeDtypeStruct((T, N_SEL), jnp.float32),
        scratch_types=[
            pltpu.VMEM((2, SC_TOKENS, N_SEL), jnp.int32),
            pltpu.VMEM((2, SC_TOKENS, D_MODEL), jnp.float32),
            pltpu.VMEM((2, SC_TOKENS, N_SEL), jnp.float32),
            pltpu.VMEM((GATHER_BUFS, GATHER_ROWS, D_MODEL), jnp.float32),
            pltpu.VMEM((GATHER_ROWS, SC_LANES), jnp.float32),
            pltpu.SemaphoreType.DMA((GATHER_BUFS,)),
            pltpu.SemaphoreType.DMA((2,)),
            pltpu.SemaphoreType.DMA((2,)),
        ],
        compiler_params=pltpu.CompilerParams(needs_layout_passes=False),
        name="peer_hidden_sc",
    )
    def k(eid_hbm, xn_hbm, u_hbm, out_hbm, idx_v, x_v, o_v, bufs, acc_v, gsem, ssem, osem):
        wid = lax.axis_index("s") * info.num_cores + lax.axis_index("c")
        lane = lax.iota(jnp.int32, SC_LANES)
        zero = jnp.zeros((SC_LANES,), jnp.float32)

        def compute(slot, t, kk, b):
            @plsc.parallel_loop(0, D_MODEL // SC_LANES, carry=(zero,) * GATHER_ROWS)
            def accs(c, accs):
                xc = x_v[slot, t, pl.ds(c * SC_LANES, SC_LANES)]
                return tuple(accs[r] + bufs[b, r, pl.ds(c * SC_LANES, SC_LANES)] * xc for r in range(GATHER_ROWS))

            for r in range(GATHER_ROWS):
                acc_v[r, :] = accs[r]
            tot = zero
            for c in range(SC_LANES):
                tot = tot + plsc.load_gather(acc_v, [lane, jnp.full((SC_LANES,), c, jnp.int32)])
            o_v[slot, t, pl.ds(kk * GATHER_ROWS, GATHER_ROWS)] = tot

        _sc_gather_stream(wid, n_batches, u_hbm.at[layer], [eid_hbm, xn_hbm], [idx_v, x_v], out_hbm, o_v, bufs,
                          gsem, ssem, osem, compute)

    return k(eid, xn, u_tabs)


def _peer_combine_sc(eid, a, v_tabs, layer):
    T = eid.shape[0]
    info, mesh, n_batches = _sc_mesh_and_batches(T)

    @functools.partial(
        pl.kernel, mesh=mesh,
        out_type=jax.ShapeDtypeStruct((T, D_MODEL), jnp.float32),
        scratch_types=[
            pltpu.VMEM((2, SC_TOKENS, N_SEL), jnp.int32),
            pltpu.VMEM((2, SC_TOKENS, N_SEL), jnp.float32),
            pltpu.VMEM((2, SC_TOKENS, D_MODEL), jnp.float32),
            pltpu.VMEM((GATHER_BUFS, GATHER_ROWS, D_MODEL), jnp.float32),
            pltpu.SemaphoreType.DMA((GATHER_BUFS,)),
            pltpu.SemaphoreType.DMA((2,)),
            pltpu.SemaphoreType.DMA((2,)),
        ],
        compiler_params=pltpu.CompilerParams(needs_layout_passes=False),
        name="peer_combine_sc",
    )
    def k(eid_hbm, a_hbm, v_hbm, out_hbm, idx_v, a_v, o_v, bufs, gsem, ssem, osem):
        wid = lax.axis_index("s") * info.num_cores + lax.axis_index("c")

        def compute(slot, t, kk, b):
            svec = jnp.full((SC_LANES,), slot, jnp.int32)
            tvec = jnp.full((SC_LANES,), t, jnp.int32)
            ws = [plsc.load_gather(a_v, [svec, tvec, jnp.full((SC_LANES,), kk * GATHER_ROWS + r, jnp.int32)])
                  for r in range(GATHER_ROWS)]

            @plsc.parallel_loop(0, D_MODEL // SC_LANES, unroll=2)
            def _(c):
                sl = pl.ds(c * SC_LANES, SC_LANES)
                terms = [ws[r] * bufs[b, r, sl] for r in range(GATHER_ROWS)]
                if kk != 0:
                    terms.append(o_v[slot, t, sl])
                while len(terms) > 1:
                    pairs = [terms[i] + terms[i + 1] for i in range(0, len(terms) - 1, 2)]
                    terms = pairs + ([terms[-1]] if len(terms) % 2 else [])
                o_v[slot, t, sl] = terms[0]

        _sc_gather_stream(wid, n_batches, v_hbm.at[layer], [eid_hbm, a_hbm], [idx_v, a_v], out_hbm, o_v, bufs,
                          gsem, ssem, osem, compute)

    return k(eid, a, v_tabs)


def _peer_act_kernel(hp_ref, gate_ref, a_ref):
    a_ref[...] = _gelu(hp_ref[...]) * gate_ref[...]


ELEMENTWISE_ROWS = 1024


def _peer_act(hpre, gate):
    T = hpre.shape[0]
    rows = ELEMENTWISE_ROWS
    assert T % rows == 0
    spec = pl.BlockSpec((rows, N_SEL), lambda i: (i, 0))
    return pl.pallas_call(
        _peer_act_kernel, grid=(T // rows,), in_specs=[spec, spec], out_specs=spec,
        out_shape=jax.ShapeDtypeStruct((T, N_SEL), jnp.float32),
        compiler_params=pltpu.CompilerParams(dimension_semantics=("arbitrary",)),
        name="peer_act",
    )(hpre, gate)


def _residual_kernel(h_ref, c_ref, g_ref, o_ref, *, final_norm):
    y = h_ref[...] + c_ref[...]
    o_ref[...] = _rmsnorm_rows(y, g_ref[...]) if final_norm else y


def _residual(h, c, gfin, *, final_norm):
    T = h.shape[0]
    rows = ELEMENTWISE_ROWS
    assert T % rows == 0
    spec = pl.BlockSpec((rows, D_MODEL), lambda i: (i, 0))
    return pl.pallas_call(
        functools.partial(_residual_kernel, final_norm=final_norm), grid=(T // rows,),
        in_specs=[spec, spec, pl.BlockSpec((1, D_MODEL), lambda i: (0, 0))], out_specs=spec,
        out_shape=jax.ShapeDtypeStruct((T, D_MODEL), jnp.float32),
        compiler_params=pltpu.CompilerParams(dimension_semantics=("arbitrary",)),
        name="peer_residual",
    )(h, c, gfin.reshape(1, D_MODEL))


def _peer_experts(eid, xn, gate, h, gfin, u_tabs, v_tabs, layer, *, final_norm, after=None):
    if after is not None:
        eid, _ = lax.optimization_barrier((eid, after))
    hpre = _peer_hidden_sc(eid, xn, u_tabs, layer)
    a = _peer_act(hpre, gate)
    c = _peer_combine_sc(eid, a, v_tabs, layer)
    return _residual(h, c, gfin, final_norm=final_norm), c


KV_WIDTH = N_KV_HEADS * HEAD_DIM
BF16 = jnp.bfloat16


def _qkv(h, gkv, gq, wkv_ref, wq_ref):
    kv = jnp.dot(_rmsnorm_rows(h, gkv).astype(BF16), wkv_ref[...], preferred_element_type=jnp.float32)
    q = jnp.dot(_rmsnorm_rows(h, gq).astype(BF16), wq_ref[...], preferred_element_type=jnp.float32)
    return kv, q


def _sink_softmax_pv(parts, sink):
    m = sink
    for s, _ in parts:
        m = jnp.maximum(m, jnp.max(s, axis=-1, keepdims=True))
    den = jnp.exp(sink - m)
    acc = None
    for s, v in parts:
        e = jnp.exp(s - m)
        den = den + jnp.sum(e, axis=-1, keepdims=True)
        pv = jnp.dot(e.astype(BF16), v.astype(BF16), preferred_element_type=jnp.float32)
        acc = pv if acc is None else acc + pv
    return acc / den


def _nt_dot(a, b):
    return lax.dot_general(a.astype(BF16), b.astype(BF16), (((1,), (1,)), ((), ())),
                           preferred_element_type=jnp.float32)


def _attn_prompt_kernel(h_ref, gkv_ref, gq_ref, wkv_ref, wq_ref, wo_ref, sink_ref,
                        o_ref, kvw_ref, prev_ref):
    blk = pl.program_id(1)
    h = h_ref[...]
    kv, q = _qkv(h, gkv_ref[...], gq_ref[...], wkv_ref, wq_ref)
    kvw_ref[0] = kv

    @pl.when(blk == 0)
    def _():
        prev_ref[...] = jnp.zeros_like(prev_ref)

    prev = prev_ref[...]
    qi = lax.broadcasted_iota(jnp.int32, (WINDOW, WINDOW), 0)
    kj = lax.broadcasted_iota(jnp.int32, (WINDOW, WINDOW), 1)
    prev_ok = jnp.logical_and(kj > qi, blk > 0)
    cur_ok = kj <= qi
    heads = []
    for kvh in range(N_KV_HEADS):
        ks = slice(kvh * HEAD_DIM, (kvh + 1) * HEAD_DIM)
        vs = slice(KV_WIDTH + kvh * HEAD_DIM, KV_WIDTH + (kvh + 1) * HEAD_DIM)
        for g in range(Q_PER_KV):
            hq = kvh * Q_PER_KV + g
            qh = q[:, hq * HEAD_DIM:(hq + 1) * HEAD_DIM]
            sp = jnp.where(prev_ok, _nt_dot(qh, prev[:, ks]) * ATTN_SCALE, -jnp.inf)
            sc = jnp.where(cur_ok, _nt_dot(qh, kv[:, ks]) * ATTN_SCALE, -jnp.inf)
            heads.append(_sink_softmax_pv([(sp, prev[:, vs]), (sc, kv[:, vs])], sink_ref[hq]))
    o = jnp.concatenate(heads, axis=1)
    o_ref[...] = h + jnp.dot(o.astype(BF16), wo_ref[...], preferred_element_type=jnp.float32)
    prev_ref[...] = kv


def _attn_prompt(h, gkv, gq, wkv, wq, wo, sinks, *, n_seq, seq_len):
    nb = seq_len // WINDOW
    row_spec = pl.BlockSpec((WINDOW, D_MODEL), lambda n, b: (n * nb + b, 0))
    full = lambda shape: pl.BlockSpec(shape, lambda n, b: (0,) * len(shape))
    return pl.pallas_call(
        _attn_prompt_kernel,
        grid=(n_seq, nb),
        in_specs=[
            row_spec, full((1, D_MODEL)), full((1, D_MODEL)),
            full((D_MODEL, 2 * KV_WIDTH)), full((D_MODEL, D_MODEL)), full((D_MODEL, D_MODEL)),
            pl.BlockSpec(memory_space=pltpu.SMEM),
        ],
        out_specs=[row_spec, pl.BlockSpec((1, WINDOW, 2 * KV_WIDTH), lambda n, b: (n, 0, 0))],
        out_shape=[
            jax.ShapeDtypeStruct(h.shape, jnp.float32),
            jax.ShapeDtypeStruct((n_seq, WINDOW, 2 * KV_WIDTH), jnp.float32),
        ],
        scratch_shapes=[pltpu.VMEM((WINDOW, 2 * KV_WIDTH), jnp.float32)],
        compiler_params=pltpu.CompilerParams(
            dimension_semantics=("arbitrary", "arbitrary"), vmem_limit_bytes=VMEM_LIMIT_BYTES),
        name="attn_prompt",
    )(h, gkv.reshape(1, D_MODEL), gq.reshape(1, D_MODEL), wkv, wq, wo, sinks)


ATTN_SEQS = 16


def _attn_sample_kernel(h_ref, ck_ref, cv_ref, gkv_ref, gq_ref, wkv_ref, wq_ref, wo_ref, sink_ref,
                        o_ref, kw_ref, vw_ref, kv_ref, q_ref, att_ref, *, n_new):
    h = h_ref[...]
    kv, q = _qkv(h, gkv_ref[...], gq_ref[...], wkv_ref, wq_ref)
    kv_ref[...] = kv
    q_ref[...] = q
    n_seq = h.shape[0] // n_new
    rows = Q_PER_KV * n_new
    qpos_c = lax.broadcasted_iota(jnp.int32, (rows, WINDOW), 0) % n_new
    cache_ok = lax.broadcasted_iota(jnp.int32, (rows, WINDOW), 1) > qpos_c
    qpos_n = lax.broadcasted_iota(jnp.int32, (rows, n_new), 0) % n_new
    new_ok = lax.broadcasted_iota(jnp.int32, (rows, n_new), 1) <= qpos_n

    def per_seq(n, carry):
        r0 = pl.multiple_of(n * n_new, n_new)
        kvn = kv_ref[pl.ds(r0, n_new), :]
        qn = q_ref[pl.ds(r0, n_new), :]
        ck = ck_ref[n]
        cv = cv_ref[n]
        kw_ref[n, 0:WINDOW - n_new, :] = ck[n_new:, :]
        kw_ref[n, WINDOW - n_new:WINDOW, :] = kvn[:, 0:KV_WIDTH]
        vw_ref[n, 0:WINDOW - n_new, :] = cv[n_new:, :]
        vw_ref[n, WINDOW - n_new:WINDOW, :] = kvn[:, KV_WIDTH:2 * KV_WIDTH]
        outs = []
        for kvh in range(N_KV_HEADS):
            ks = slice(kvh * HEAD_DIM, (kvh + 1) * HEAD_DIM)
            vs = slice(KV_WIDTH + kvh * HEAD_DIM, KV_WIDTH + (kvh + 1) * HEAD_DIM)
            qs = jnp.concatenate(
                [qn[:, (kvh * Q_PER_KV + g) * HEAD_DIM:(kvh * Q_PER_KV + g + 1) * HEAD_DIM] for g in range(Q_PER_KV)],
                axis=0)
            s_c = jnp.where(cache_ok, _nt_dot(qs, ck[:, ks]) * ATTN_SCALE, -jnp.inf)
            s_n = jnp.where(new_ok, _nt_dot(qs, kvn[:, ks]) * ATTN_SCALE, -jnp.inf)
            sink = sink_ref[kvh][:, 0:1]
            o = _sink_softmax_pv([(s_c, cv[:, ks]), (s_n, kvn[:, vs])], sink)
            outs += [o[g * n_new:(g + 1) * n_new, :] for g in range(Q_PER_KV)]
        att_ref[pl.ds(r0, n_new), :] = jnp.concatenate(outs, axis=1)
        return carry

    lax.fori_loop(0, n_seq, per_seq, 0)
    o_ref[...] = h + jnp.dot(att_ref[...].astype(BF16), wo_ref[...], preferred_element_type=jnp.float32)


def _attn_sample(h, cache_k, cache_v, gkv, gq, wkv, wq, wo, sinks, *, n_new):
    n_seq = cache_k.shape[0]
    sb = ATTN_SEQS
    rows = sb * n_new
    row_spec = pl.BlockSpec((rows, D_MODEL), lambda i: (i, 0))
    win_spec = pl.BlockSpec((sb, WINDOW, KV_WIDTH), lambda i: (i, 0, 0))
    full = lambda shape: pl.BlockSpec(shape, lambda i: (0,) * len(shape))
    sink_rows = jnp.repeat(sinks.reshape(N_KV_HEADS, Q_PER_KV), n_new, axis=1)[:, :, None]
    sink_rows = jnp.broadcast_to(sink_rows, (N_KV_HEADS, Q_PER_KV * n_new, LANES))
    return pl.pallas_call(
        functools.partial(_attn_sample_kernel, n_new=n_new),
        grid=(n_seq // sb,),
        in_specs=[
            row_spec, win_spec, win_spec, full((1, D_MODEL)), full((1, D_MODEL)),
            full((D_MODEL, 2 * KV_WIDTH)), full((D_MODEL, D_MODEL)), full((D_MODEL, D_MODEL)),
            full((N_KV_HEADS, Q_PER_KV * n_new, LANES)),
        ],
        out_specs=[row_spec, win_spec, win_spec],
        out_shape=[
            jax.ShapeDtypeStruct(h.shape, jnp.float32),
            jax.ShapeDtypeStruct((n_seq, WINDOW, KV_WIDTH), jnp.float32),
            jax.ShapeDtypeStruct((n_seq, WINDOW, KV_WIDTH), jnp.float32),
        ],
        scratch_shapes=[
            pltpu.VMEM((rows, 2 * KV_WIDTH), jnp.float32),
            pltpu.VMEM((rows, D_MODEL), jnp.float32),
            pltpu.VMEM((rows, D_MODEL), jnp.float32),
        ],
        compiler_params=pltpu.CompilerParams(
            dimension_semantics=("arbitrary",), vmem_limit_bytes=VMEM_LIMIT_BYTES),
        name="attn_sample",
    )(h, cache_k, cache_v, gkv.reshape(1, D_MODEL), gq.reshape(1, D_MODEL), wkv, wq, wo, sink_rows)


def kernel(x_prompt, x_sample, state_ssm_re, state_ssm_im, cache_k_win, cache_v_win, norm_mix, norm_ffn, norm_kv, norm_final, ssm_lam_re, ssm_lam_im, ssm_log_dt, ssm_b_re, ssm_b_im, ssm_c_re, ssm_c_im, ssm_d, ssm_w_glu, w_kv, w_q, attn_sinks, w_o, peer_w_q, peer_sub_keys, peer_u, peer_v):
    bmat, cmat, apr, api = _s5_discretize(ssm_lam_re[0], ssm_lam_im[0], ssm_log_dt[0], ssm_b_re[0], ssm_b_im[0], ssm_c_re[0], ssm_c_im[0])
    cmat = cmat.astype(jnp.bfloat16)
    wglu = ssm_w_glu[0].astype(jnp.bfloat16)
    wkv = w_kv.astype(BF16)
    wq = w_q[0].astype(BF16)
    wo = w_o[0].astype(BF16)
    peer_wq = [peer_w_q[layer].astype(BF16) for layer in range(2)]
    peer_sk = [peer_sub_keys[layer].astype(BF16) for layer in range(2)]

    def peer(h, layer, final_norm, after=None):
        xn, eid_t, gate_t = _peer_route(h, norm_ffn[layer], peer_wq[layer], peer_sk[layer])
        return _peer_experts(eid_t.T, xn, gate_t.T, h, norm_final, peer_u, peer_v, layer, final_norm=final_norm,
                             after=after)

    win = lambda a: a.reshape(a.shape[0], WINDOW, N_KV_HEADS, HEAD_DIM)

    seq_len = x_prompt.shape[1]
    xp_all = x_prompt.reshape(-1, D_MODEL)

    def trunk_prompt(seq0, n_p, after):
        row0 = seq0 * seq_len
        z0 = jnp.zeros((n_p, N_CHUNKS, 1, CHUNK_STATE), jnp.float32)
        z, sr, si = _s5_mixer(xp_all, z0, z0, norm_mix[0], ssm_d[0], bmat, cmat, apr, api,
                              chain=True, seq_len=seq_len, row0=row0)
        h2, _ = peer(_glu_residual(z, xp_all, wglu, res_row0=row0), 0, False, after=after)
        h3, kvw = _attn_prompt(h2, norm_kv, norm_mix[1], wkv, wq, wo, attn_sinks[0], n_seq=n_p, seq_len=seq_len)
        y, _ = peer(h3, 1, True)
        return (y.reshape(n_p, seq_len, D_MODEL), _chunks_to_state(sr), _chunks_to_state(si),
                win(kvw[:, :, :KV_WIDTH]), win(kvw[:, :, KV_WIDTH:]))

    def trunk_sample(x_s, s_re, s_im, c_k, c_v):
        n_s = x_s.shape[0]
        xs = x_s.reshape(-1, D_MODEL)
        z, sr, si = _s5_mixer(xs, _state_to_chunks(s_re), _state_to_chunks(s_im), norm_mix[0], ssm_d[0],
                              bmat, cmat, apr, api, chain=False, seq_len=x_s.shape[1])
        h2, c0 = peer(_glu_residual(z, xs, wglu), 0, False)
        h3, kw, vw = _attn_sample(h2, c_k.reshape(n_s, WINDOW, KV_WIDTH), c_v.reshape(n_s, WINDOW, KV_WIDTH),
                                  norm_kv, norm_mix[1], wkv, wq, wo, attn_sinks[0], n_new=x_s.shape[1])
        y, _ = peer(h3, 1, True)
        return (y.reshape(x_s.shape), _chunks_to_state(sr), _chunks_to_state(si), win(kw), win(vw)), c0

    (y_s, sre_s, sim_s, kw_s, vw_s), c0_s = trunk_sample(x_sample, state_ssm_re[0], state_ssm_im[0],
                                                         cache_k_win, cache_v_win)
    n_groups = 4
    gp = x_prompt.shape[0] // n_groups
    outs = [trunk_prompt(i * gp, gp, after=c0_s) for i in range(n_groups)]
    y_p, sre_p, sim_p, kw_p, vw_p = [jnp.concatenate(parts, axis=0) for parts in zip(*outs)]
    return (y_p, y_s, sre_p[None], sim_p[None], kw_p, vw_p, sre_s[None], sim_s[None], kw_s, vw_s)
```

```python
import functools
import math

import jax
import jax.numpy as jnp
from jax import lax
from jax.experimental import pallas as pl
from jax.experimental.pallas import tpu as pltpu
from jax.experimental.pallas import tpu_sc as plsc

D_MODEL = 1024
GROUP_SIZE = 16
N_GROUPS = D_MODEL // GROUP_SIZE
STATE_DIM = 64
HEAD_DIM = 64
N_Q_HEADS = D_MODEL // HEAD_DIM
N_KV_HEADS = N_Q_HEADS // 8
Q_PER_KV = N_Q_HEADS // N_KV_HEADS
WINDOW = 128
ATTN_SCALE = 1.0 / math.sqrt(HEAD_DIM)
PEER_HEADS = 8
N_KEYS = 128
PEER_TOPK = 16
PEER_DHALF = 128
EPS = 1e-5

LANES = 128
SUBLANES = 8
VMEM_LIMIT_BYTES = 56 * 1024 * 1024

GROUPS_PER_CHUNK = LANES // GROUP_SIZE
N_CHUNKS = N_GROUPS // GROUPS_PER_CHUNK
CHUNK_STATE = GROUPS_PER_CHUNK * STATE_DIM
S5_ROWS = 256


def _rmsnorm_rows(x, g):
    r = lax.rsqrt(jnp.mean(x * x, axis=-1, keepdims=True) + EPS)
    return x * r * g


def _gelu(x):
    return 0.5 * x * (1.0 + lax.erf(x * (1.0 / math.sqrt(2.0))))


def _s5_discretize(lam_re, lam_im, log_dt, b_re, b_im, c_re, c_im):
    f32 = jnp.float32
    lr = lam_re.astype(f32)
    li = lam_im.astype(f32)
    dt = jnp.exp(log_dt.astype(f32))[:, None]
    mag = jnp.exp(lr * dt)
    ab_re = mag * jnp.cos(li * dt)
    ab_im = mag * jnp.sin(li * dt)
    den = lr * lr + li * li
    f_re = ((ab_re - 1.0) * lr + ab_im * li) / den
    f_im = (ab_im * lr - (ab_re - 1.0) * li) / den
    br = b_re.astype(f32)
    bi = b_im.astype(f32)
    bb_re = f_re[..., None] * br - f_im[..., None] * bi
    bb_im = f_re[..., None] * bi + f_im[..., None] * br
    eye = jnp.eye(GROUPS_PER_CHUNK, dtype=f32)

    def chunk_rows(v):
        return v.reshape(N_CHUNKS, 1, CHUNK_STATE)

    def in_blocks(bb):
        t = bb.reshape(N_CHUNKS, GROUPS_PER_CHUNK, STATE_DIM, GROUP_SIZE).transpose(0, 1, 3, 2)
        return jnp.einsum('mgjp,gh->mgjhp', t, eye).reshape(N_CHUNKS, LANES, CHUNK_STATE)

    def out_blocks(c):
        t = c.astype(f32).reshape(N_CHUNKS, GROUPS_PER_CHUNK, GROUP_SIZE, STATE_DIM).transpose(0, 1, 3, 2)
        return jnp.einsum('mgpj,gh->mgphj', t, eye).reshape(N_CHUNKS, CHUNK_STATE, LANES)

    bfull = jnp.concatenate([in_blocks(bb_re), in_blocks(bb_im)], axis=2)
    b_hi = bfull.astype(jnp.bfloat16)
    b_lo = (bfull - b_hi.astype(f32)).astype(jnp.bfloat16)
    bmat = (jnp.concatenate([b_hi, b_hi], axis=1), b_lo)
    cmat = jnp.concatenate([out_blocks(c_re), -out_blocks(c_im)], axis=1)
    pr, pi = [ab_re], [ab_im]
    for _ in range(SUBLANES - 1):
        pr, pi = pr + [pr[-1] * ab_re - pi[-1] * ab_im], pi + [pr[-1] * ab_im + pi[-1] * ab_re]
    apr = jnp.concatenate([chunk_rows(v) for v in pr], axis=1)
    api = jnp.concatenate([chunk_rows(v) for v in pi], axis=1)
    return bmat, cmat, apr, api


def _split_bf16(x):
    hi = x.astype(jnp.bfloat16)
    return hi, (x - hi.astype(jnp.float32)).astype(jnp.bfloat16)


def _s5_kernel(x_ref, g_ref, d_ref, bhh_ref, blo_ref, c_ref, apr_ref, api_ref, h0r_ref, h0i_ref,
               z_ref, sr_ref, si_ref, u_ref, us_ref, st_ref, cr_ref, ci_ref, *, chain, blocks_per_seq):
    rb = pl.program_id(0)
    m = pl.program_id(1)
    rows = x_ref.shape[0]
    n_tiles = rows // SUBLANES

    @pl.when(m == 0)
    def _():
        u = _rmsnorm_rows(x_ref[...], g_ref[...])
        for mm in range(N_CHUNKS):
            uc = u[:, mm * LANES:(mm + 1) * LANES]
            u_ref[mm] = uc
            hi, lo = _split_bf16(uc)
            us_ref[mm, :, 0:LANES] = hi
            us_ref[mm, :, LANES:2 * LANES] = lo

    u = u_ref[m]
    us = us_ref[m]
    st_ref[...] = (jnp.dot(us, bhh_ref[0], preferred_element_type=jnp.float32)
                   + jnp.dot(us[:, 0:LANES], blo_ref[0], preferred_element_type=jnp.float32))

    apr = apr_ref[0]
    api = api_ref[0]
    row = lax.broadcasted_iota(jnp.int32, (SUBLANES, CHUNK_STATE), 0)

    if chain:
        @pl.when(rb % blocks_per_seq == 0)
        def _():
            cr_ref[m] = h0r_ref[0, 0]
            ci_ref[m] = h0i_ref[0, 0]

    def tile_step(k, carry):
        r0 = pl.multiple_of(k * SUBLANES, SUBLANES)
        xr = st_ref[pl.ds(r0, SUBLANES), 0:CHUNK_STATE]
        xi = st_ref[pl.ds(r0, SUBLANES), CHUNK_STATE:2 * CHUNK_STATE]
        for d in (1, 2, 4):
            ar = apr[d - 1:d, :]
            ai = api[d - 1:d, :]
            sr = jnp.where(row >= d, pltpu.roll(xr, d, axis=0), 0.0)
            si = jnp.where(row >= d, pltpu.roll(xi, d, axis=0), 0.0)
            xr, xi = xr + ar * sr - ai * si, xi + ar * si + ai * sr
        if chain:
            cr, ci = carry
        else:
            cr = h0r_ref[k, 0]
            ci = h0i_ref[k, 0]
        hr = xr + apr * cr - api * ci
        hi = xi + apr * ci + api * cr
        st_ref[pl.ds(r0, SUBLANES), 0:CHUNK_STATE] = hr
        st_ref[pl.ds(r0, SUBLANES), CHUNK_STATE:2 * CHUNK_STATE] = hi
        lr_ = hr[SUBLANES - 1:SUBLANES, :]
        li_ = hi[SUBLANES - 1:SUBLANES, :]
        if chain:
            return lr_, li_
        sr_ref[k, m] = lr_
        si_ref[k, m] = li_
        return carry

    if chain:
        cr, ci = lax.fori_loop(0, n_tiles, tile_step, (cr_ref[m], ci_ref[m]))
        cr_ref[m] = cr
        ci_ref[m] = ci
        sr_ref[0, m] = cr
        si_ref[0, m] = ci
    else:
        lax.fori_loop(0, n_tiles, tile_step, 0)

    y = jnp.dot(st_ref[...].astype(jnp.bfloat16), c_ref[0], preferred_element_type=jnp.float32)
    y = y + d_ref[0] * u
    z_ref[...] = _gelu(y)


def _s5_mixer(x, h0r, h0i, g, d_skip, bmat, cmat_bf16, apr, api, *, chain, seq_len, row0=0):
    nseq = h0r.shape[0]
    T = nseq * seq_len
    rows = S5_ROWS
    blk0 = row0 // rows
    if chain:
        blocks_per_seq = seq_len // rows
        seq_blk = 1
        seq_map = lambda rb, m: (rb // blocks_per_seq, m, 0, 0)
        out_map = lambda rb, m: (rb // blocks_per_seq, 0, 0, 0)
    else:
        assert seq_len == SUBLANES
        blocks_per_seq = 1
        seq_blk = rows // SUBLANES
        seq_map = lambda rb, m: (rb, m, 0, 0)
        out_map = lambda rb, m: (rb, 0, 0, 0)
    grid = (T // rows, N_CHUNKS)
    kern = functools.partial(_s5_kernel, chain=chain, blocks_per_seq=blocks_per_seq)
    st_spec = pl.BlockSpec((seq_blk, 1, 1, CHUNK_STATE), seq_map)
    out_st_spec = pl.BlockSpec((seq_blk, N_CHUNKS, 1, CHUNK_STATE), out_map)
    z, sr, si = pl.pallas_call(
        kern,
        grid=grid,
        in_specs=[
            pl.BlockSpec((rows, D_MODEL), lambda rb, m: (rb + blk0, 0)),
            pl.BlockSpec((1, D_MODEL), lambda rb, m: (0, 0)),
            pl.BlockSpec((1, 1, LANES), lambda rb, m: (m, 0, 0)),
            pl.BlockSpec((1, 2 * LANES, 2 * CHUNK_STATE), lambda rb, m: (m, 0, 0)),
            pl.BlockSpec((1, LANES, 2 * CHUNK_STATE), lambda rb, m: (m, 0, 0)),
            pl.BlockSpec((1, 2 * CHUNK_STATE, LANES), lambda rb, m: (m, 0, 0)),
            pl.BlockSpec((1, SUBLANES, CHUNK_STATE), lambda rb, m: (m, 0, 0)),
            pl.BlockSpec((1, SUBLANES, CHUNK_STATE), lambda rb, m: (m, 0, 0)),
            st_spec, st_spec,
        ],
        out_specs=[
            pl.BlockSpec((rows, LANES), lambda rb, m: (rb, m)),
            out_st_spec, out_st_spec,
        ],
        out_shape=[
            jax.ShapeDtypeStruct((T, D_MODEL), jnp.float32),
            jax.ShapeDtypeStruct((nseq, N_CHUNKS, 1, CHUNK_STATE), jnp.float32),
            jax.ShapeDtypeStruct((nseq, N_CHUNKS, 1, CHUNK_STATE), jnp.float32),
        ],
        scratch_shapes=[
            pltpu.VMEM((N_CHUNKS, rows, LANES), jnp.float32),
            pltpu.VMEM((N_CHUNKS, rows, 2 * LANES), jnp.bfloat16),
            pltpu.VMEM((rows, 2 * CHUNK_STATE), jnp.float32),
            pltpu.VMEM((N_CHUNKS, 1, CHUNK_STATE), jnp.float32),
            pltpu.VMEM((N_CHUNKS, 1, CHUNK_STATE), jnp.float32),
        ],
        compiler_params=pltpu.CompilerParams(
            dimension_semantics=("arbitrary", "arbitrary"), vmem_limit_bytes=VMEM_LIMIT_BYTES),
        name="s5_mixer",
    )(x, g.reshape(1, D_MODEL), d_skip.reshape(N_CHUNKS, 1, LANES), bmat[0], bmat[1], cmat_bf16, apr, api,
      h0r, h0i)
    return z, sr, si


def _state_to_chunks(h):
    return h.reshape(h.shape[0], N_CHUNKS, 1, CHUNK_STATE)


def _chunks_to_state(s):
    return s.reshape(s.shape[0], N_GROUPS, STATE_DIM)


def _glu_kernel(z_ref, res_ref, w_ref, o_ref):
    zz = jnp.dot(z_ref[...].astype(jnp.bfloat16), w_ref[...], preferred_element_type=jnp.float32)
    a = zz[:, :D_MODEL]
    b = zz[:, D_MODEL:]
    o_ref[...] = res_ref[...] + a * (1.0 / (1.0 + jnp.exp(-b)))


def _glu_residual(z, res, w_bf16, res_row0=0):
    T = z.shape[0]
    rows = math.gcd(T, 512)
    blk0 = res_row0 // rows
    return pl.pallas_call(
        _glu_kernel,
        grid=(T // rows,),
        in_specs=[
            pl.BlockSpec((rows, D_MODEL), lambda i: (i, 0)),
            pl.BlockSpec((rows, D_MODEL), lambda i: (i + blk0, 0)),
            pl.BlockSpec((D_MODEL, 2 * D_MODEL), lambda i: (0, 0)),
        ],
        out_specs=pl.BlockSpec((rows, D_MODEL), lambda i: (i, 0)),
        out_shape=jax.ShapeDtypeStruct((T, D_MODEL), jnp.float32),
        compiler_params=pltpu.CompilerParams(
            dimension_semantics=("arbitrary",), vmem_limit_bytes=VMEM_LIMIT_BYTES),
        name="glu_residual",
    )(z, res, w_bf16)


ROUTE_ROWS = 256


def _topk_rows(s, payload):
    n_rows = s.shape[0]
    row = lax.broadcasted_iota(jnp.int32, s.shape, 0)
    vals, picks = [], []
    for _ in range(PEER_TOPK):
        m = jnp.max(s, axis=0, keepdims=True)
        pos = jnp.min(jnp.where(s == m, row, n_rows), axis=0, keepdims=True)
        sel = row == pos
        vals.append(m)
        if payload is None:
            picks.append(pos)
        else:
            picks.append(jnp.max(jnp.where(sel, payload, -1), axis=0, keepdims=True))
        s = jnp.where(sel, -jnp.inf, s)
    return jnp.concatenate(vals, axis=0), jnp.concatenate(picks, axis=0)


def _pair_rows(a0, a1, combine):
    half = PEER_TOPK // 2
    parts = [combine(a0[0:1, :], a1)]
    parts += [combine(a0[i:i + 1, :], a1[0:half, :]) for i in range(1, half)]
    parts.append(combine(a0[half:PEER_TOPK, :], a1[0:1, :]))
    return jnp.concatenate(parts, axis=0)


def _peer_route_kernel(h_ref, g_ref, wq_ref, sk_ref, xn_ref, eid_ref, gate_ref, xb_ref):
    hd = pl.program_id(1)

    @pl.when(hd == 0)
    def _():
        xn = _rmsnorm_rows(h_ref[...], g_ref[...])
        xn_ref[...] = xn
        xb_ref[...] = xn.astype(jnp.bfloat16)

    q = jnp.dot(xb_ref[...], wq_ref[...], preferred_element_type=jnp.float32)
    sv, si = [], []
    for c in range(2):
        qc = q[:, c * PEER_DHALF:(c + 1) * PEER_DHALF].astype(jnp.bfloat16)
        st = lax.dot_general(sk_ref[0, c], qc, (((1,), (1,)), ((), ())), preferred_element_type=jnp.float32)
        v, i = _topk_rows(st, None)
        sv.append(v)
        si.append(i)
    cand = _pair_rows(sv[0], sv[1], lambda a, b: a + b)
    cid = _pair_rows(si[0], si[1], lambda a, b: a * N_KEYS + b)
    fv, eid = _topk_rows(cand, cid)
    e = jnp.exp(fv - fv[0:1, :])
    gate_ref[...] = e / jnp.sum(e, axis=0, keepdims=True)
    eid_ref[...] = eid


def _peer_route(h, g, wq_bf16, sk_bf16):
    T = h.shape[0]
    rows = ROUTE_ROWS
    n_sel = PEER_HEADS * PEER_TOPK
    return pl.pallas_call(
        _peer_route_kernel,
        grid=(T // rows, PEER_HEADS),
        in_specs=[
            pl.BlockSpec((rows, D_MODEL), lambda tb, hd: (tb, 0)),
            pl.BlockSpec((1, D_MODEL), lambda tb, hd: (0, 0)),
            pl.BlockSpec((D_MODEL, 2 * PEER_DHALF), lambda tb, hd: (0, hd)),
            pl.BlockSpec((1, 2, N_KEYS, PEER_DHALF), lambda tb, hd: (hd, 0, 0, 0)),
        ],
        out_specs=[
            pl.BlockSpec((rows, D_MODEL), lambda tb, hd: (tb, 0)),
            pl.BlockSpec((PEER_TOPK, rows), lambda tb, hd: (hd, tb)),
            pl.BlockSpec((PEER_TOPK, rows), lambda tb, hd: (hd, tb)),
        ],
        out_shape=[
            jax.ShapeDtypeStruct((T, D_MODEL), jnp.float32),
            jax.ShapeDtypeStruct((n_sel, T), jnp.int32),
            jax.ShapeDtypeStruct((n_sel, T), jnp.float32),
        ],
        scratch_shapes=[pltpu.VMEM((rows, D_MODEL), jnp.bfloat16)],
        compiler_params=pltpu.CompilerParams(
            dimension_semantics=("arbitrary", "arbitrary"), vmem_limit_bytes=VMEM_LIMIT_BYTES),
        name="peer_route",
    )(h, g.reshape(1, D_MODEL), wq_bf16, sk_bf16)


N_SEL = PEER_HEADS * PEER_TOPK
SC_LANES = 16
GATHER_ROWS = PEER_TOPK
GATHERS_PER_TOKEN = N_SEL // GATHER_ROWS
GATHER_BUFS = 4
SC_TOKENS = 8


def _sc_gather_stream(wid, n_batches, tab_hbm, stage_srcs, stage_bufs, out_hbm, o_v, bufs, gsem, ssem, osem,
                      compute):
    idx_v = stage_bufs[0]

    def stage_copies(bi, slot):
        base = (wid * n_batches + bi) * SC_TOKENS
        return [pltpu.make_async_copy(src.at[pl.ds(base, SC_TOKENS)], buf.at[slot], ssem.at[slot])
                for src, buf in zip(stage_srcs, stage_bufs)]

    def out_copy(bi, slot):
        base = (wid * n_batches + bi) * SC_TOKENS
        return pltpu.make_async_copy(o_v.at[slot], out_hbm.at[pl.ds(base, SC_TOKENS)], osem.at[slot])

    def start(slot, t, kk, b):
        idx = idx_v[slot, t, pl.ds(kk * GATHER_ROWS, GATHER_ROWS)]
        pltpu.async_copy(tab_hbm.at[idx], bufs.at[b], gsem.at[b])

    def wait(b):
        pltpu.make_async_copy(tab_hbm.at[pl.ds(0, GATHER_ROWS)], bufs.at[b], gsem.at[b]).wait()

    for c in stage_copies(0, 0):
        c.start()
    for c in stage_copies(0, 0):
        c.wait()
    for q in range(GATHER_BUFS - 1):
        start(0, q // GATHERS_PER_TOKEN, q % GATHERS_PER_TOKEN, q % GATHER_BUFS)

    def batch(bi, carry):
        slot = bi % 2
        has_next = bi + 1 < n_batches

        @pl.when(has_next)
        def _():
            for c in stage_copies(bi + 1, 1 - slot):
                c.start()

        @pl.when(bi >= 2)
        def _():
            out_copy(bi, slot).wait()

        def tok(t, carry):
            @pl.when(jnp.logical_and(t == SC_TOKENS - 1, has_next))
            def _():
                for c in stage_copies(bi + 1, 1 - slot):
                    c.wait()

            for kk in range(GATHERS_PER_TOKEN):
                nq = kk + GATHER_BUFS - 1
                nk, nb = nq % GATHERS_PER_TOKEN, nq % GATHER_BUFS
                if nq < GATHERS_PER_TOKEN:
                    start(slot, t, nk, nb)
                else:
                    @pl.when(t + 1 < SC_TOKENS)
                    def _():
                        start(slot, t + 1, nk, nb)

                    @pl.when(jnp.logical_and(t + 1 == SC_TOKENS, has_next))
                    def _():
                        start(1 - slot, 0, nk, nb)

                wait(kk % GATHER_BUFS)
                compute(slot, t, kk, kk % GATHER_BUFS)
            return carry

        lax.fori_loop(0, SC_TOKENS, tok, 0)
        out_copy(bi, slot).start()
        return carry

    lax.fori_loop(0, n_batches, batch, 0)
    if n_batches >= 2:
        out_copy(n_batches - 2, (n_batches - 2) % 2).wait()
    out_copy(n_batches - 1, (n_batches - 1) % 2).wait()


def _sc_mesh_and_batches(n_tokens):
    info = plsc.get_sparse_core_info()
    assert info.num_lanes == SC_LANES
    n_workers = info.num_cores * info.num_subcores
    assert n_tokens % (n_workers * SC_TOKENS) == 0
    mesh = plsc.VectorSubcoreMesh(core_axis_name="c", subcore_axis_name="s")
    return info, mesh, n_tokens // (n_workers * SC_TOKENS)


def _peer_hidden_sc(eid, xn, u_tabs, layer):
    T = eid.shape[0]
    info, mesh, n_batches = _sc_mesh_and_batches(T)

    @functools.partial(
        pl.kernel, mesh=mesh,
        out_type=jax.ShapeDtypeStruct((T, N_SEL), jnp.float32),
        scratch_types=[
            pltpu.VMEM((2, SC_TOKENS, N_SEL), jnp.int32),
            pltpu.VMEM((2, SC_TOKENS, D_MODEL), jnp.float32),
            pltpu.VMEM((2, SC_TOKENS, N_SEL), jnp.float32),
            pltpu.VMEM((GATHER_BUFS, GATHER_ROWS, D_MODEL), jnp.float32),
            pltpu.VMEM((GATHER_ROWS, SC_LANES), jnp.float32),
            pltpu.SemaphoreType.DMA((GATHER_BUFS,)),
            pltpu.SemaphoreType.DMA((2,)),
            pltpu.SemaphoreType.DMA((2,)),
        ],
        compiler_params=pltpu.CompilerParams(needs_layout_passes=False),
        name="peer_hidden_sc",
    )
    def k(eid_hbm, xn_hbm, u_hbm, out_hbm, idx_v, x_v, o_v, bufs, acc_v, gsem, ssem, osem):
        wid = lax.axis_index("s") * info.num_cores + lax.axis_index("c")
        lane = lax.iota(jnp.int32, SC_LANES)
        zero = jnp.zeros((SC_LANES,), jnp.float32)

        def compute(slot, t, kk, b):
            @plsc.parallel_loop(0, D_MODEL // SC_LANES, carry=(zero,) * GATHER_ROWS)
            def accs(c, accs):
                xc = x_v[slot, t, pl.ds(c * SC_LANES, SC_LANES)]
                return tuple(accs[r] + bufs[b, r, pl.ds(c * SC_LANES, SC_LANES)] * xc for r in range(GATHER_ROWS))

            for r in range(GATHER_ROWS):
                acc_v[r, :] = accs[r]
            tot = zero
            for c in range(SC_LANES):
                tot = tot + plsc.load_gather(acc_v, [lane, jnp.full((SC_LANES,), c, jnp.int32)])
            o_v[slot, t, pl.ds(kk * GATHER_ROWS, GATHER_ROWS)] = tot

        _sc_gather_stream(wid, n_batches, u_hbm.at[layer], [eid_hbm, xn_hbm], [idx_v, x_v], out_hbm, o_v, bufs,
                          gsem, ssem, osem, compute)

    return k(eid, xn, u_tabs)


def _peer_combine_sc(eid, a, v_tabs, layer):
    T = eid.shape[0]
    info, mesh, n_batches = _sc_mesh_and_batches(T)

    @functools.partial(
        pl.kernel, mesh=mesh,
        out_type=jax.ShapeDtypeStruct((T, D_MODEL), jnp.float32),
        scratch_types=[
            pltpu.VMEM((2, SC_TOKENS, N_SEL), jnp.int32),
            pltpu.VMEM((2, SC_TOKENS, N_SEL), jnp.float32),
            pltpu.VMEM((2, SC_TOKENS, D_MODEL), jnp.float32),
            pltpu.VMEM((GATHER_BUFS, GATHER_ROWS, D_MODEL), jnp.float32),
            pltpu.SemaphoreType.DMA((GATHER_BUFS,)),
            pltpu.SemaphoreType.DMA((2,)),
            pltpu.SemaphoreType.DMA((2,)),
        ],
        compiler_params=pltpu.CompilerParams(needs_layout_passes=False),
        name="peer_combine_sc",
    )
    def k(eid_hbm, a_hbm, v_hbm, out_hbm, idx_v, a_v, o_v, bufs, gsem, ssem, osem):
        wid = lax.axis_index("s") * info.num_cores + lax.axis_index("c")

        def compute(slot, t, kk, b):
            svec = jnp.full((SC_LANES,), slot, jnp.int32)
            tvec = jnp.full((SC_LANES,), t, jnp.int32)
            ws = [plsc.load_gather(a_v, [svec, tvec, jnp.full((SC_LANES,), kk * GATHER_ROWS + r, jnp.int32)])
                  for r in range(GATHER_ROWS)]

            @plsc.parallel_loop(0, D_MODEL // SC_LANES, unroll=2)
            def _(c):
                sl = pl.ds(c * SC_LANES, SC_LANES)
                terms = [ws[r] * bufs[b, r, sl] for r in range(GATHER_ROWS)]
                if kk != 0:
                    terms.append(o_v[slot, t, sl])
                while len(terms) > 1:
                    pairs = [terms[i] + terms[i + 1] for i in range(0, len(terms) - 1, 2)]
                    terms = pairs + ([terms[-1]] if len(terms) % 2 else [])
                o_v[slot, t, sl] = terms[0]

        _sc_gather_stream(wid, n_batches, v_hbm.at[layer], [eid_hbm, a_hbm], [idx_v, a_v], out_hbm, o_v, bufs,
                          gsem, ssem, osem, compute)

    return k(eid, a, v_tabs)


def _peer_act_kernel(hp_ref, gate_ref, a_ref):
    a_ref[...] = _gelu(hp_ref[...]) * gate_ref[...]


ELEMENTWISE_ROWS = 1024


def _peer_act(hpre, gate):
    T = hpre.shape[0]
    rows = ELEMENTWISE_ROWS
    assert T % rows == 0
    spec = pl.BlockSpec((rows, N_SEL), lambda i: (i, 0))
    return pl.pallas_call(
        _peer_act_kernel, grid=(T // rows,), in_specs=[spec, spec], out_specs=spec,
        out_shape=jax.ShapeDtypeStruct((T, N_SEL), jnp.float32),
        compiler_params=pltpu.CompilerParams(dimension_semantics=("arbitrary",)),
        name="peer_act",
    )(hpre, gate)


def _residual_kernel(h_ref, c_ref, g_ref, o_ref, *, final_norm):
    y = h_ref[...] + c_ref[...]
    o_ref[...] = _rmsnorm_rows(y, g_ref[...]) if final_norm else y


def _residual(h, c, gfin, *, final_norm):
    T = h.shape[0]
    rows = ELEMENTWISE_ROWS
    assert T % rows == 0
    spec = pl.BlockSpec((rows, D_MODEL), lambda i: (i, 0))
    return pl.pallas_call(
        functools.partial(_residual_kernel, final_norm=final_norm), grid=(T // rows,),
        in_specs=[spec, spec, pl.BlockSpec((1, D_MODEL), lambda i: (0, 0))], out_specs=spec,
        out_shape=jax.ShapeDtypeStruct((T, D_MODEL), jnp.float32),
        compiler_params=pltpu.CompilerParams(dimension_semantics=("arbitrary",)),
        name="peer_residual",
    )(h, c, gfin.reshape(1, D_MODEL))


def _peer_experts(eid, xn, gate, h, gfin, u_tabs, v_tabs, layer, *, final_norm, after=None, act_after=None):
    if after is not None:
        eid, _ = lax.optimization_barrier((eid, after))
    hpre = _peer_hidden_sc(eid, xn, u_tabs, layer)
    if act_after is not None:
        hpre, _ = lax.optimization_barrier((hpre, act_after))
    a = _peer_act(hpre, gate)
    c = _peer_combine_sc(eid, a, v_tabs, layer)
    return _residual(h, c, gfin, final_norm=final_norm), c


KV_WIDTH = N_KV_HEADS * HEAD_DIM
BF16 = jnp.bfloat16


def _qkv(h, gkv, gq, wkv_ref, wq_ref):
    kv = jnp.dot(_rmsnorm_rows(h, gkv).astype(BF16), wkv_ref[...], preferred_element_type=jnp.float32)
    q = jnp.dot(_rmsnorm_rows(h, gq).astype(BF16), wq_ref[...], preferred_element_type=jnp.float32)
    return kv, q


def _sink_softmax_pv(parts, sink):
    m = sink
    for s, _ in parts:
        m = jnp.maximum(m, jnp.max(s, axis=-1, keepdims=True))
    den = jnp.exp(sink - m)
    acc = None
    for s, v in parts:
        e = jnp.exp(s - m)
        den = den + jnp.sum(e, axis=-1, keepdims=True)
        pv = jnp.dot(e.astype(BF16), v.astype(BF16), preferred_element_type=jnp.float32)
        acc = pv if acc is None else acc + pv
    return acc / den


def _nt_dot(a, b):
    return lax.dot_general(a.astype(BF16), b.astype(BF16), (((1,), (1,)), ((), ())),
                           preferred_element_type=jnp.float32)


def _attn_prompt_kernel(h_ref, gkv_ref, gq_ref, wkv_ref, wq_ref, wo_ref, sink_ref,
                        o_ref, kvw_ref, prev_ref):
    blk = pl.program_id(1)
    h = h_ref[...]
    kv, q = _qkv(h, gkv_ref[...], gq_ref[...], wkv_ref, wq_ref)
    kvw_ref[0] = kv

    @pl.when(blk == 0)
    def _():
        prev_ref[...] = jnp.zeros_like(prev_ref)

    prev = prev_ref[...]
    qi = lax.broadcasted_iota(jnp.int32, (WINDOW, WINDOW), 0)
    kj = lax.broadcasted_iota(jnp.int32, (WINDOW, WINDOW), 1)
    prev_ok = jnp.logical_and(kj > qi, blk > 0)
    cur_ok = kj <= qi
    heads = []
    for kvh in range(N_KV_HEADS):
        ks = slice(kvh * HEAD_DIM, (kvh + 1) * HEAD_DIM)
        vs = slice(KV_WIDTH + kvh * HEAD_DIM, KV_WIDTH + (kvh + 1) * HEAD_DIM)
        for g in range(Q_PER_KV):
            hq = kvh * Q_PER_KV + g
            qh = q[:, hq * HEAD_DIM:(hq + 1) * HEAD_DIM]
            sp = jnp.where(prev_ok, _nt_dot(qh, prev[:, ks]) * ATTN_SCALE, -jnp.inf)
            sc = jnp.where(cur_ok, _nt_dot(qh, kv[:, ks]) * ATTN_SCALE, -jnp.inf)
            heads.append(_sink_softmax_pv([(sp, prev[:, vs]), (sc, kv[:, vs])], sink_ref[hq]))
    o = jnp.concatenate(heads, axis=1)
    o_ref[...] = h + jnp.dot(o.astype(BF16), wo_ref[...], preferred_element_type=jnp.float32)
    prev_ref[...] = kv


def _attn_prompt(h, gkv, gq, wkv, wq, wo, sinks, *, n_seq, seq_len):
    nb = seq_len // WINDOW
    row_spec = pl.BlockSpec((WINDOW, D_MODEL), lambda n, b: (n * nb + b, 0))
    full = lambda shape: pl.BlockSpec(shape, lambda n, b: (0,) * len(shape))
    return pl.pallas_call(
        _attn_prompt_kernel,
        grid=(n_seq, nb),
        in_specs=[
            row_spec, full((1, D_MODEL)), full((1, D_MODEL)),
            full((D_MODEL, 2 * KV_WIDTH)), full((D_MODEL, D_MODEL)), full((D_MODEL, D_MODEL)),
            pl.BlockSpec(memory_space=pltpu.SMEM),
        ],
        out_specs=[row_spec, pl.BlockSpec((1, WINDOW, 2 * KV_WIDTH), lambda n, b: (n, 0, 0))],
        out_shape=[
            jax.ShapeDtypeStruct(h.shape, jnp.float32),
            jax.ShapeDtypeStruct((n_seq, WINDOW, 2 * KV_WIDTH), jnp.float32),
        ],
        scratch_shapes=[pltpu.VMEM((WINDOW, 2 * KV_WIDTH), jnp.float32)],
        compiler_params=pltpu.CompilerParams(
            dimension_semantics=("arbitrary", "arbitrary"), vmem_limit_bytes=VMEM_LIMIT_BYTES),
        name="attn_prompt",
    )(h, gkv.reshape(1, D_MODEL), gq.reshape(1, D_MODEL), wkv, wq, wo, sinks)


ATTN_SEQS = 16


def _attn_sample_kernel(h_ref, ck_ref, cv_ref, gkv_ref, gq_ref, wkv_ref, wq_ref, wo_ref, sink_ref,
                        o_ref, kw_ref, vw_ref, kv_ref, q_ref, att_ref, *, n_new):
    h = h_ref[...]
    kv, q = _qkv(h, gkv_ref[...], gq_ref[...], wkv_ref, wq_ref)
    kv_ref[...] = kv
    q_ref[...] = q
    n_seq = h.shape[0] // n_new
    rows = Q_PER_KV * n_new
    qpos_c = lax.broadcasted_iota(jnp.int32, (rows, WINDOW), 0) % n_new
    cache_ok = lax.broadcasted_iota(jnp.int32, (rows, WINDOW), 1) > qpos_c
    qpos_n = lax.broadcasted_iota(jnp.int32, (rows, n_new), 0) % n_new
    new_ok = lax.broadcasted_iota(jnp.int32, (rows, n_new), 1) <= qpos_n

    def per_seq(n, carry):
        r0 = pl.multiple_of(n * n_new, n_new)
        kvn = kv_ref[pl.ds(r0, n_new), :]
        qn = q_ref[pl.ds(r0, n_new), :]
        ck = ck_ref[n]
        cv = cv_ref[n]
        kw_ref[n, 0:WINDOW - n_new, :] = ck[n_new:, :]
        kw_ref[n, WINDOW - n_new:WINDOW, :] = kvn[:, 0:KV_WIDTH]
        vw_ref[n, 0:WINDOW - n_new, :] = cv[n_new:, :]
        vw_ref[n, WINDOW - n_new:WINDOW, :] = kvn[:, KV_WIDTH:2 * KV_WIDTH]
        outs = []
        for kvh in range(N_KV_HEADS):
            ks = slice(kvh * HEAD_DIM, (kvh + 1) * HEAD_DIM)
            vs = slice(KV_WIDTH + kvh * HEAD_DIM, KV_WIDTH + (kvh + 1) * HEAD_DIM)
            qs = jnp.concatenate(
                [qn[:, (kvh * Q_PER_KV + g) * HEAD_DIM:(kvh * Q_PER_KV + g + 1) * HEAD_DIM] for g in range(Q_PER_KV)],
                axis=0)
            s_c = jnp.where(cache_ok, _nt_dot(qs, ck[:, ks]) * ATTN_SCALE, -jnp.inf)
            s_n = jnp.where(new_ok, _nt_dot(qs, kvn[:, ks]) * ATTN_SCALE, -jnp.inf)
            sink = sink_ref[kvh][:, 0:1]
            o = _sink_softmax_pv([(s_c, cv[:, ks]), (s_n, kvn[:, vs])], sink)
            outs += [o[g * n_new:(g + 1) * n_new, :] for g in range(Q_PER_KV)]
        att_ref[pl.ds(r0, n_new), :] = jnp.concatenate(outs, axis=1)
        return carry

    lax.fori_loop(0, n_seq, per_seq, 0)
    o_ref[...] = h + jnp.dot(att_ref[...].astype(BF16), wo_ref[...], preferred_element_type=jnp.float32)


def _attn_sample(h, cache_k, cache_v, gkv, gq, wkv, wq, wo, sinks, *, n_new):
    n_seq = cache_k.shape[0]
    sb = ATTN_SEQS
    rows = sb * n_new
    row_spec = pl.BlockSpec((rows, D_MODEL), lambda i: (i, 0))
    win_spec = pl.BlockSpec((sb, WINDOW, KV_WIDTH), lambda i: (i, 0, 0))
    full = lambda shape: pl.BlockSpec(shape, lambda i: (0,) * len(shape))
    sink_rows = jnp.repeat(sinks.reshape(N_KV_HEADS, Q_PER_KV), n_new, axis=1)[:, :, None]
    sink_rows = jnp.broadcast_to(sink_rows, (N_KV_HEADS, Q_PER_KV * n_new, LANES))
    return pl.pallas_call(
        functools.partial(_attn_sample_kernel, n_new=n_new),
        grid=(n_seq // sb,),
        in_specs=[
            row_spec, win_spec, win_spec, full((1, D_MODEL)), full((1, D_MODEL)),
            full((D_MODEL, 2 * KV_WIDTH)), full((D_MODEL, D_MODEL)), full((D_MODEL, D_MODEL)),
            full((N_KV_HEADS, Q_PER_KV * n_new, LANES)),
        ],
        out_specs=[row_spec, win_spec, win_spec],
        out_shape=[
            jax.ShapeDtypeStruct(h.shape, jnp.float32),
            jax.ShapeDtypeStruct((n_seq, WINDOW, KV_WIDTH), jnp.float32),
            jax.ShapeDtypeStruct((n_seq, WINDOW, KV_WIDTH), jnp.float32),
        ],
        scratch_shapes=[
            pltpu.VMEM((rows, 2 * KV_WIDTH), jnp.float32),
            pltpu.VMEM((rows, D_MODEL), jnp.float32),
            pltpu.VMEM((rows, D_MODEL), jnp.float32),
        ],
        compiler_params=pltpu.CompilerParams(
            dimension_semantics=("arbitrary",), vmem_limit_bytes=VMEM_LIMIT_BYTES),
        name="attn_sample",
    )(h, cache_k, cache_v, gkv.reshape(1, D_MODEL), gq.reshape(1, D_MODEL), wkv, wq, wo, sink_rows)


def kernel(x_prompt, x_sample, state_ssm_re, state_ssm_im, cache_k_win, cache_v_win, norm_mix, norm_ffn, norm_kv, norm_final, ssm_lam_re, ssm_lam_im, ssm_log_dt, ssm_b_re, ssm_b_im, ssm_c_re, ssm_c_im, ssm_d, ssm_w_glu, w_kv, w_q, attn_sinks, w_o, peer_w_q, peer_sub_keys, peer_u, peer_v):
    bmat, cmat, apr, api = _s5_discretize(ssm_lam_re[0], ssm_lam_im[0], ssm_log_dt[0], ssm_b_re[0], ssm_b_im[0], ssm_c_re[0], ssm_c_im[0])
    cmat = cmat.astype(jnp.bfloat16)
    wglu = ssm_w_glu[0].astype(jnp.bfloat16)
    wkv = w_kv.astype(BF16)
    wq = w_q[0].astype(BF16)
    wo = w_o[0].astype(BF16)
    peer_wq = [peer_w_q[layer].astype(BF16) for layer in range(2)]
    peer_sk = [peer_sub_keys[layer].astype(BF16) for layer in range(2)]

    def peer(h, layer, final_norm, after=None, act_after=None):
        xn, eid_t, gate_t = _peer_route(h, norm_ffn[layer], peer_wq[layer], peer_sk[layer])
        return _peer_experts(eid_t.T, xn, gate_t.T, h, norm_final, peer_u, peer_v, layer, final_norm=final_norm,
                             after=after, act_after=act_after)

    win = lambda a: a.reshape(a.shape[0], WINDOW, N_KV_HEADS, HEAD_DIM)

    seq_len = x_prompt.shape[1]
    xp_all = x_prompt.reshape(-1, D_MODEL)

    def prompt_s5(seq0, n_p):
        z0 = jnp.zeros((n_p, N_CHUNKS, 1, CHUNK_STATE), jnp.float32)
        return _s5_mixer(xp_all, z0, z0, norm_mix[0], ssm_d[0], bmat, cmat, apr, api,
                         chain=True, seq_len=seq_len, row0=seq0 * seq_len)

    def trunk_prompt(seq0, n_p, s5_out, after):
        z, sr, si = s5_out
        h2, _ = peer(_glu_residual(z, xp_all, wglu, res_row0=seq0 * seq_len), 0, False, after=after)
        h3, kvw = _attn_prompt(h2, norm_kv, norm_mix[1], wkv, wq, wo, attn_sinks[0], n_seq=n_p, seq_len=seq_len)
        y, _ = peer(h3, 1, True)
        return (y.reshape(n_p, seq_len, D_MODEL), _chunks_to_state(sr), _chunks_to_state(si),
                win(kvw[:, :, :KV_WIDTH]), win(kvw[:, :, KV_WIDTH:]))

    def trunk_sample(x_s, s_re, s_im, c_k, c_v, act_after):
        n_s = x_s.shape[0]
        xs = x_s.reshape(-1, D_MODEL)
        z, sr, si = _s5_mixer(xs, _state_to_chunks(s_re), _state_to_chunks(s_im), norm_mix[0], ssm_d[0],
                              bmat, cmat, apr, api, chain=False, seq_len=x_s.shape[1])
        h2, c0 = peer(_glu_residual(z, xs, wglu), 0, False, act_after=act_after)
        h3, kw, vw = _attn_sample(h2, c_k.reshape(n_s, WINDOW, KV_WIDTH), c_v.reshape(n_s, WINDOW, KV_WIDTH),
                                  norm_kv, norm_mix[1], wkv, wq, wo, attn_sinks[0], n_new=x_s.shape[1])
        y, _ = peer(h3, 1, True)
        return (y.reshape(x_s.shape), _chunks_to_state(sr), _chunks_to_state(si), win(kw), win(vw)), c0

    n_groups = 4
    gp = x_prompt.shape[0] // n_groups
    s5_outs = [prompt_s5(i * gp, gp) for i in range(n_groups)]
    (y_s, sre_s, sim_s, kw_s, vw_s), c0_s = trunk_sample(x_sample, state_ssm_re[0], state_ssm_im[0],
                                                         cache_k_win, cache_v_win, act_after=s5_outs[0][0])
    outs = [trunk_prompt(i * gp, gp, s5_outs[i], after=c0_s) for i in range(n_groups)]
    y_p, sre_p, sim_p, kw_p, vw_p = [jnp.concatenate(parts, axis=0) for parts in zip(*outs)]
    return (y_p, y_s, sre_p[None], sim_p[None], kw_p, vw_p, sre_s[None], sim_s[None], kw_s, vw_s)
```

```python
import functools
import math

import jax
import jax.numpy as jnp
from jax import lax
from jax.experimental import pallas as pl
from jax.experimental.pallas import tpu as pltpu
from jax.experimental.pallas import tpu_sc as plsc

D_MODEL = 1024
GROUP_SIZE = 16
N_GROUPS = D_MODEL // GROUP_SIZE
STATE_DIM = 64
HEAD_DIM = 64
N_Q_HEADS = D_MODEL // HEAD_DIM
N_KV_HEADS = N_Q_HEADS // 8
Q_PER_KV = N_Q_HEADS // N_KV_HEADS
WINDOW = 128
ATTN_SCALE = 1.0 / math.sqrt(HEAD_DIM)
PEER_HEADS = 8
N_KEYS = 128
PEER_TOPK = 16
PEER_DHALF = 128
EPS = 1e-5

LANES = 128
SUBLANES = 8
VMEM_LIMIT_BYTES = 56 * 1024 * 1024

GROUPS_PER_CHUNK = LANES // GROUP_SIZE
N_CHUNKS = N_GROUPS // GROUPS_PER_CHUNK
CHUNK_STATE = GROUPS_PER_CHUNK * STATE_DIM
S5_ROWS = 256


def _rmsnorm_rows(x, g):
    r = lax.rsqrt(jnp.mean(x * x, axis=-1, keepdims=True) + EPS)
    return x * r * g


def _gelu(x):
    return 0.5 * x * (1.0 + lax.erf(x * (1.0 / math.sqrt(2.0))))


def _s5_discretize(lam_re, lam_im, log_dt, b_re, b_im, c_re, c_im):
    f32 = jnp.float32
    lr = lam_re.astype(f32)
    li = lam_im.astype(f32)
    dt = jnp.exp(log_dt.astype(f32))[:, None]
    mag = jnp.exp(lr * dt)
    ab_re = mag * jnp.cos(li * dt)
    ab_im = mag * jnp.sin(li * dt)
    den = lr * lr + li * li
    f_re = ((ab_re - 1.0) * lr + ab_im * li) / den
    f_im = (ab_im * lr - (ab_re - 1.0) * li) / den
    br = b_re.astype(f32)
    bi = b_im.astype(f32)
    bb_re = f_re[..., None] * br - f_im[..., None] * bi
    bb_im = f_re[..., None] * bi + f_im[..., None] * br
    eye = jnp.eye(GROUPS_PER_CHUNK, dtype=f32)

    def chunk_rows(v):
        return v.reshape(N_CHUNKS, 1, CHUNK_STATE)

    def in_blocks(bb):
        t = bb.reshape(N_CHUNKS, GROUPS_PER_CHUNK, STATE_DIM, GROUP_SIZE).transpose(0, 1, 3, 2)
        return jnp.einsum('mgjp,gh->mgjhp', t, eye).reshape(N_CHUNKS, LANES, CHUNK_STATE)

    def out_blocks(c):
        t = c.astype(f32).reshape(N_CHUNKS, GROUPS_PER_CHUNK, GROUP_SIZE, STATE_DIM).transpose(0, 1, 3, 2)
        return jnp.einsum('mgpj,gh->mgphj', t, eye).reshape(N_CHUNKS, CHUNK_STATE, LANES)

    bfull = jnp.concatenate([in_blocks(bb_re), in_blocks(bb_im)], axis=2)
    b_hi = bfull.astype(jnp.bfloat16)
    b_lo = (bfull - b_hi.astype(f32)).astype(jnp.bfloat16)
    bmat = (jnp.concatenate([b_hi, b_hi], axis=1), b_lo)
    cmat = jnp.concatenate([out_blocks(c_re), -out_blocks(c_im)], axis=1)
    pr, pi = [ab_re], [ab_im]
    for _ in range(SUBLANES - 1):
        pr, pi = pr + [pr[-1] * ab_re - pi[-1] * ab_im], pi + [pr[-1] * ab_im + pi[-1] * ab_re]
    apr = jnp.concatenate([chunk_rows(v) for v in pr], axis=1)
    api = jnp.concatenate([chunk_rows(v) for v in pi], axis=1)
    return bmat, cmat, apr, api


def _split_bf16(x):
    hi = x.astype(jnp.bfloat16)
    return hi, (x - hi.astype(jnp.float32)).astype(jnp.bfloat16)


def _s5_kernel(x_ref, g_ref, d_ref, bhh_ref, blo_ref, c_ref, apr_ref, api_ref, h0r_ref, h0i_ref,
               z_ref, sr_ref, si_ref, u_ref, us_ref, st_ref, cr_ref, ci_ref, *, chain, blocks_per_seq):
    rb = pl.program_id(0)
    m = pl.program_id(1)
    rows = x_ref.shape[0]
    n_tiles = rows // SUBLANES

    @pl.when(m == 0)
    def _():
        u = _rmsnorm_rows(x_ref[...], g_ref[...])
        for mm in range(N_CHUNKS):
            uc = u[:, mm * LANES:(mm + 1) * LANES]
            u_ref[mm] = uc
            hi, lo = _split_bf16(uc)
            us_ref[mm, :, 0:LANES] = hi
            us_ref[mm, :, LANES:2 * LANES] = lo

    u = u_ref[m]
    us = us_ref[m]
    st_ref[...] = (jnp.dot(us, bhh_ref[0], preferred_element_type=jnp.float32)
                   + jnp.dot(us[:, 0:LANES], blo_ref[0], preferred_element_type=jnp.float32))

    apr = apr_ref[0]
    api = api_ref[0]
    row = lax.broadcasted_iota(jnp.int32, (SUBLANES, CHUNK_STATE), 0)

    if chain:
        @pl.when(rb % blocks_per_seq == 0)
        def _():
            cr_ref[m] = h0r_ref[0, 0]
            ci_ref[m] = h0i_ref[0, 0]

    def tile_step(k, carry):
        r0 = pl.multiple_of(k * SUBLANES, SUBLANES)
        xr = st_ref[pl.ds(r0, SUBLANES), 0:CHUNK_STATE]
        xi = st_ref[pl.ds(r0, SUBLANES), CHUNK_STATE:2 * CHUNK_STATE]
        for d in (1, 2, 4):
            ar = apr[d - 1:d, :]
            ai = api[d - 1:d, :]
            sr = jnp.where(row >= d, pltpu.roll(xr, d, axis=0), 0.0)
            si = jnp.where(row >= d, pltpu.roll(xi, d, axis=0), 0.0)
            xr, xi = xr + ar * sr - ai * si, xi + ar * si + ai * sr
        if chain:
            cr, ci = carry
        else:
            cr = h0r_ref[k, 0]
            ci = h0i_ref[k, 0]
        hr = xr + apr * cr - api * ci
        hi = xi + apr * ci + api * cr
        st_ref[pl.ds(r0, SUBLANES), 0:CHUNK_STATE] = hr
        st_ref[pl.ds(r0, SUBLANES), CHUNK_STATE:2 * CHUNK_STATE] = hi
        lr_ = hr[SUBLANES - 1:SUBLANES, :]
        li_ = hi[SUBLANES - 1:SUBLANES, :]
        if chain:
            return lr_, li_
        sr_ref[k, m] = lr_
        si_ref[k, m] = li_
        return carry

    if chain:
        cr, ci = lax.fori_loop(0, n_tiles, tile_step, (cr_ref[m], ci_ref[m]))
        cr_ref[m] = cr
        ci_ref[m] = ci
        sr_ref[0, m] = cr
        si_ref[0, m] = ci
    else:
        lax.fori_loop(0, n_tiles, tile_step, 0)

    y = jnp.dot(st_ref[...].astype(jnp.bfloat16), c_ref[0], preferred_element_type=jnp.float32)
    y = y + d_ref[0] * u
    z_ref[...] = _gelu(y)


def _s5_mixer(x, h0r, h0i, g, d_skip, bmat, cmat_bf16, apr, api, *, chain, seq_len, row0=0):
    nseq = h0r.shape[0]
    T = nseq * seq_len
    rows = S5_ROWS
    blk0 = row0 // rows
    if chain:
        blocks_per_seq = seq_len // rows
        seq_blk = 1
        seq_map = lambda rb, m: (rb // blocks_per_seq, m, 0, 0)
        out_map = lambda rb, m: (rb // blocks_per_seq, 0, 0, 0)
    else:
        assert seq_len == SUBLANES
        blocks_per_seq = 1
        seq_blk = rows // SUBLANES
        seq_map = lambda rb, m: (rb, m, 0, 0)
        out_map = lambda rb, m: (rb, 0, 0, 0)
    grid = (T // rows, N_CHUNKS)
    kern = functools.partial(_s5_kernel, chain=chain, blocks_per_seq=blocks_per_seq)
    st_spec = pl.BlockSpec((seq_blk, 1, 1, CHUNK_STATE), seq_map)
    out_st_spec = pl.BlockSpec((seq_blk, N_CHUNKS, 1, CHUNK_STATE), out_map)
    z, sr, si = pl.pallas_call(
        kern,
        grid=grid,
        in_specs=[
            pl.BlockSpec((rows, D_MODEL), lambda rb, m: (rb + blk0, 0)),
            pl.BlockSpec((1, D_MODEL), lambda rb, m: (0, 0)),
            pl.BlockSpec((1, 1, LANES), lambda rb, m: (m, 0, 0)),
            pl.BlockSpec((1, 2 * LANES, 2 * CHUNK_STATE), lambda rb, m: (m, 0, 0)),
            pl.BlockSpec((1, LANES, 2 * CHUNK_STATE), lambda rb, m: (m, 0, 0)),
            pl.BlockSpec((1, 2 * CHUNK_STATE, LANES), lambda rb, m: (m, 0, 0)),
            pl.BlockSpec((1, SUBLANES, CHUNK_STATE), lambda rb, m: (m, 0, 0)),
            pl.BlockSpec((1, SUBLANES, CHUNK_STATE), lambda rb, m: (m, 0, 0)),
            st_spec, st_spec,
        ],
        out_specs=[
            pl.BlockSpec((rows, LANES), lambda rb, m: (rb, m)),
            out_st_spec, out_st_spec,
        ],
        out_shape=[
            jax.ShapeDtypeStruct((T, D_MODEL), jnp.float32),
            jax.ShapeDtypeStruct((nseq, N_CHUNKS, 1, CHUNK_STATE), jnp.float32),
            jax.ShapeDtypeStruct((nseq, N_CHUNKS, 1, CHUNK_STATE), jnp.float32),
        ],
        scratch_shapes=[
            pltpu.VMEM((N_CHUNKS, rows, LANES), jnp.float32),
            pltpu.VMEM((N_CHUNKS, rows, 2 * LANES), jnp.bfloat16),
            pltpu.VMEM((rows, 2 * CHUNK_STATE), jnp.float32),
            pltpu.VMEM((N_CHUNKS, 1, CHUNK_STATE), jnp.float32),
            pltpu.VMEM((N_CHUNKS, 1, CHUNK_STATE), jnp.float32),
        ],
        compiler_params=pltpu.CompilerParams(
            dimension_semantics=("arbitrary", "arbitrary"), vmem_limit_bytes=VMEM_LIMIT_BYTES),
        name="s5_mixer",
    )(x, g.reshape(1, D_MODEL), d_skip.reshape(N_CHUNKS, 1, LANES), bmat[0], bmat[1], cmat_bf16, apr, api,
      h0r, h0i)
    return z, sr, si


def _state_to_chunks(h):
    return h.reshape(h.shape[0], N_CHUNKS, 1, CHUNK_STATE)


def _chunks_to_state(s):
    return s.reshape(s.shape[0], N_GROUPS, STATE_DIM)


def _glu_kernel(z_ref, res_ref, w_ref, o_ref):
    zz = jnp.dot(z_ref[...].astype(jnp.bfloat16), w_ref[...], preferred_element_type=jnp.float32)
    a = zz[:, :D_MODEL]
    b = zz[:, D_MODEL:]
    o_ref[...] = res_ref[...] + a * (1.0 / (1.0 + jnp.exp(-b)))


def _glu_residual(z, res, w_bf16, res_row0=0):
    T = z.shape[0]
    rows = math.gcd(T, 512)
    blk0 = res_row0 // rows
    return pl.pallas_call(
        _glu_kernel,
        grid=(T // rows,),
        in_specs=[
            pl.BlockSpec((rows, D_MODEL), lambda i: (i, 0)),
            pl.BlockSpec((rows, D_MODEL), lambda i: (i + blk0, 0)),
            pl.BlockSpec((D_MODEL, 2 * D_MODEL), lambda i: (0, 0)),
        ],
        out_specs=pl.BlockSpec((rows, D_MODEL), lambda i: (i, 0)),
        out_shape=jax.ShapeDtypeStruct((T, D_MODEL), jnp.float32),
        compiler_params=pltpu.CompilerParams(
            dimension_semantics=("arbitrary",), vmem_limit_bytes=VMEM_LIMIT_BYTES),
        name="glu_residual",
    )(z, res, w_bf16)


ROUTE_ROWS = 256


def _topk_rows(s, payload):
    n_rows = s.shape[0]
    row = lax.broadcasted_iota(jnp.int32, s.shape, 0)
    vals, picks = [], []
    for _ in range(PEER_TOPK):
        m = jnp.max(s, axis=0, keepdims=True)
        pos = jnp.min(jnp.where(s == m, row, n_rows), axis=0, keepdims=True)
        sel = row == pos
        vals.append(m)
        if payload is None:
            picks.append(pos)
        else:
            picks.append(jnp.max(jnp.where(sel, payload, -1), axis=0, keepdims=True))
        s = jnp.where(sel, -jnp.inf, s)
    return jnp.concatenate(vals, axis=0), jnp.concatenate(picks, axis=0)


def _pair_rows(a0, a1, combine):
    half = PEER_TOPK // 2
    parts = [combine(a0[0:1, :], a1)]
    parts += [combine(a0[i:i + 1, :], a1[0:half, :]) for i in range(1, half)]
    parts.append(combine(a0[half:PEER_TOPK, :], a1[0:1, :]))
    return jnp.concatenate(parts, axis=0)


def _peer_route_kernel(h_ref, g_ref, wq_ref, sk_ref, xn_ref, eid_ref, gate_ref, xb_ref):
    hd = pl.program_id(1)

    @pl.when(hd == 0)
    def _():
        xn = _rmsnorm_rows(h_ref[...], g_ref[...])
        xn_ref[...] = xn
        xb_ref[...] = xn.astype(jnp.bfloat16)

    q = jnp.dot(xb_ref[...], wq_ref[...], preferred_element_type=jnp.float32)
    sv, si = [], []
    for c in range(2):
        qc = q[:, c * PEER_DHALF:(c + 1) * PEER_DHALF].astype(jnp.bfloat16)
        st = lax.dot_general(sk_ref[0, c], qc, (((1,), (1,)), ((), ())), preferred_element_type=jnp.float32)
        v, i = _topk_rows(st, None)
        sv.append(v)
        si.append(i)
    cand = _pair_rows(sv[0], sv[1], lambda a, b: a + b)
    cid = _pair_rows(si[0], si[1], lambda a, b: a * N_KEYS + b)
    fv, eid = _topk_rows(cand, cid)
    e = jnp.exp(fv - fv[0:1, :])
    gate_ref[...] = e / jnp.sum(e, axis=0, keepdims=True)
    eid_ref[...] = eid


def _peer_route(h, g, wq_bf16, sk_bf16):
    T = h.shape[0]
    rows = ROUTE_ROWS
    n_sel = PEER_HEADS * PEER_TOPK
    return pl.pallas_call(
        _peer_route_kernel,
        grid=(T // rows, PEER_HEADS),
        in_specs=[
            pl.BlockSpec((rows, D_MODEL), lambda tb, hd: (tb, 0)),
            pl.BlockSpec((1, D_MODEL), lambda tb, hd: (0, 0)),
            pl.BlockSpec((D_MODEL, 2 * PEER_DHALF), lambda tb, hd: (0, hd)),
            pl.BlockSpec((1, 2, N_KEYS, PEER_DHALF), lambda tb, hd: (hd, 0, 0, 0)),
        ],
        out_specs=[
            pl.BlockSpec((rows, D_MODEL), lambda tb, hd: (tb, 0)),
            pl.BlockSpec((PEER_TOPK, rows), lambda tb, hd: (hd, tb)),
            pl.BlockSpec((PEER_TOPK, rows), lambda tb, hd: (hd, tb)),
        ],
        out_shape=[
            jax.ShapeDtypeStruct((T, D_MODEL), jnp.float32),
            jax.ShapeDtypeStruct((n_sel, T), jnp.int32),
            jax.ShapeDtypeStruct((n_sel, T), jnp.float32),
        ],
        scratch_shapes=[pltpu.VMEM((rows, D_MODEL), jnp.bfloat16)],
        compiler_params=pltpu.CompilerParams(
            dimension_semantics=("arbitrary", "arbitrary"), vmem_limit_bytes=VMEM_LIMIT_BYTES),
        name="peer_route",
    )(h, g.reshape(1, D_MODEL), wq_bf16, sk_bf16)


N_SEL = PEER_HEADS * PEER_TOPK
SC_LANES = 16
GATHER_ROWS = PEER_TOPK
GATHERS_PER_TOKEN = N_SEL // GATHER_ROWS
GATHER_BUFS = 4
SC_TOKENS = 8


def _sc_gather_stream(wid, n_batches, tab_hbm, stage_srcs, stage_bufs, out_hbm, o_v, bufs, gsem, ssem, osem,
                      compute):
    idx_v = stage_bufs[0]

    def stage_copies(bi, slot):
        base = (wid * n_batches + bi) * SC_TOKENS
        return [pltpu.make_async_copy(src.at[pl.ds(base, SC_TOKENS)], buf.at[slot], ssem.at[slot])
                for src, buf in zip(stage_srcs, stage_bufs)]

    def out_copy(bi, slot):
        base = (wid * n_batches + bi) * SC_TOKENS
        return pltpu.make_async_copy(o_v.at[slot], out_hbm.at[pl.ds(base, SC_TOKENS)], osem.at[slot])

    def start(slot, t, kk, b):
        idx = idx_v[slot, t, pl.ds(kk * GATHER_ROWS, GATHER_ROWS)]
        pltpu.async_copy(tab_hbm.at[idx], bufs.at[b], gsem.at[b])

    def wait(b):
        pltpu.make_async_copy(tab_hbm.at[pl.ds(0, GATHER_ROWS)], bufs.at[b], gsem.at[b]).wait()

    for c in stage_copies(0, 0):
        c.start()
    for c in stage_copies(0, 0):
        c.wait()
    for q in range(GATHER_BUFS - 1):
        start(0, q // GATHERS_PER_TOKEN, q % GATHERS_PER_TOKEN, q % GATHER_BUFS)

    def batch(bi, carry):
        slot = bi % 2
        has_next = bi + 1 < n_batches

        @pl.when(has_next)
        def _():
            for c in stage_copies(bi + 1, 1 - slot):
                c.start()

        @pl.when(bi >= 2)
        def _():
            out_copy(bi, slot).wait()

        def tok(t, carry):
            @pl.when(jnp.logical_and(t == SC_TOKENS - 1, has_next))
            def _():
                for c in stage_copies(bi + 1, 1 - slot):
                    c.wait()

            for kk in range(GATHERS_PER_TOKEN):
                nq = kk + GATHER_BUFS - 1
                nk, nb = nq % GATHERS_PER_TOKEN, nq % GATHER_BUFS
                if nq < GATHERS_PER_TOKEN:
                    start(slot, t, nk, nb)
                else:
                    @pl.when(t + 1 < SC_TOKENS)
                    def _():
                        start(slot, t + 1, nk, nb)

                    @pl.when(jnp.logical_and(t + 1 == SC_TOKENS, has_next))
                    def _():
                        start(1 - slot, 0, nk, nb)

                wait(kk % GATHER_BUFS)
                compute(slot, t, kk, kk % GATHER_BUFS)
            return carry

        lax.fori_loop(0, SC_TOKENS, tok, 0)
        out_copy(bi, slot).start()
        return carry

    lax.fori_loop(0, n_batches, batch, 0)
    if n_batches >= 2:
        out_copy(n_batches - 2, (n_batches - 2) % 2).wait()
    out_copy(n_batches - 1, (n_batches - 1) % 2).wait()


def _sc_mesh_and_batches(n_tokens):
    info = plsc.get_sparse_core_info()
    assert info.num_lanes == SC_LANES
    n_workers = info.num_cores * info.num_subcores
    assert n_tokens % (n_workers * SC_TOKENS) == 0
    mesh = plsc.VectorSubcoreMesh(core_axis_name="c", subcore_axis_name="s")
    return info, mesh, n_tokens // (n_workers * SC_TOKENS)


def _peer_hidden_sc(eid, xn, u_tabs, layer):
    T = eid.shape[0]
    info, mesh, n_batches = _sc_mesh_and_batches(T)

    @functools.partial(
        pl.kernel, mesh=mesh,
        out_type=jax.ShapeDtypeStruct((T, N_SEL), jnp.float32),
        scratch_types=[
            pltpu.VMEM((2, SC_TOKENS, N_SEL), jnp.int32),
            pltpu.VMEM((2, SC_TOKENS, D_MODEL), jnp.float32),
            pltpu.VMEM((2, SC_TOKENS, N_SEL), jnp.float32),
            pltpu.VMEM((GATHER_BUFS, GATHER_ROWS, D_MODEL), jnp.float32),
            pltpu.VMEM((GATHER_ROWS, SC_LANES), jnp.float32),
            pltpu.SemaphoreType.DMA((GATHER_BUFS,)),
            pltpu.SemaphoreType.DMA((2,)),
            pltpu.SemaphoreType.DMA((2,)),
        ],
        compiler_params=pltpu.CompilerParams(needs_layout_passes=False),
        name="peer_hidden_sc",
    )
    def k(eid_hbm, xn_hbm, u_hbm, out_hbm, idx_v, x_v, o_v, bufs, acc_v, gsem, ssem, osem):
        wid = lax.axis_index("s") * info.num_cores + lax.axis_index("c")
        lane = lax.iota(jnp.int32, SC_LANES)
        zero = jnp.zeros((SC_LANES,), jnp.float32)

        def compute(slot, t, kk, b):
            @plsc.parallel_loop(0, D_MODEL // SC_LANES, carry=(zero,) * GATHER_ROWS)
            def accs(c, accs):
                xc = x_v[slot, t, pl.ds(c * SC_LANES, SC_LANES)]
                return tuple(accs[r] + bufs[b, r, pl.ds(c * SC_LANES, SC_LANES)] * xc for r in range(GATHER_ROWS))

            for r in range(GATHER_ROWS):
                acc_v[r, :] = accs[r]
            tot = zero
            for c in range(SC_LANES):
                tot = tot + plsc.load_gather(acc_v, [lane, jnp.full((SC_LANES,), c, jnp.int32)])
            o_v[slot, t, pl.ds(kk * GATHER_ROWS, GATHER_ROWS)] = tot

        _sc_gather_stream(wid, n_batches, u_hbm.at[layer], [eid_hbm, xn_hbm], [idx_v, x_v], out_hbm, o_v, bufs,
                          gsem, ssem, osem, compute)

    return k(eid, xn, u_tabs)


def _peer_combine_sc(eid, a, v_tabs, layer):
    T = eid.shape[0]
    info, mesh, n_batches = _sc_mesh_and_batches(T)

    @functools.partial(
        pl.kernel, mesh=mesh,
        out_type=jax.ShapeDtypeStruct((T, D_MODEL), jnp.float32),
        scratch_types=[
            pltpu.VMEM((2, SC_TOKENS, N_SEL), jnp.int32),
            pltpu.VMEM((2, SC_TOKENS, N_SEL), jnp.float32),
            pltpu.VMEM((2, SC_TOKENS, D_MODEL), jnp.float32),
            pltpu.VMEM((GATHER_BUFS, GATHER_ROWS, D_MODEL), jnp.float32),
            pltpu.SemaphoreType.DMA((GATHER_BUFS,)),
            pltpu.SemaphoreType.DMA((2,)),
            pltpu.SemaphoreType.DMA((2,)),
        ],
        compiler_params=pltpu.CompilerParams(needs_layout_passes=False),
        name="peer_combine_sc",
    )
    def k(eid_hbm, a_hbm, v_hbm, out_hbm, idx_v, a_v, o_v, bufs, gsem, ssem, osem):
        wid = lax.axis_index("s") * info.num_cores + lax.axis_index("c")

        def compute(slot, t, kk, b):
            svec = jnp.full((SC_LANES,), slot, jnp.int32)
            tvec = jnp.full((SC_LANES,), t, jnp.int32)
            ws = [plsc.load_gather(a_v, [svec, tvec, jnp.full((SC_LANES,), kk * GATHER_ROWS + r, jnp.int32)])
                  for r in range(GATHER_ROWS)]

            @plsc.parallel_loop(0, D_MODEL // SC_LANES, unroll=2)
            def _(c):
                sl = pl.ds(c * SC_LANES, SC_LANES)
                terms = [ws[r] * bufs[b, r, sl] for r in range(GATHER_ROWS)]
                if kk != 0:
                    terms.append(o_v[slot, t, sl])
                while len(terms) > 1:
                    pairs = [terms[i] + terms[i + 1] for i in range(0, len(terms) - 1, 2)]
                    terms = pairs + ([terms[-1]] if len(terms) % 2 else [])
                o_v[slot, t, sl] = terms[0]

        _sc_gather_stream(wid, n_batches, v_hbm.at[layer], [eid_hbm, a_hbm], [idx_v, a_v], out_hbm, o_v, bufs,
                          gsem, ssem, osem, compute)

    return k(eid, a, v_tabs)


def _peer_act_kernel(hp_ref, gate_ref, a_ref):
    a_ref[...] = _gelu(hp_ref[...]) * gate_ref[...]


ELEMENTWISE_ROWS = 1024


def _peer_act(hpre, gate):
    T = hpre.shape[0]
    rows = ELEMENTWISE_ROWS
    assert T % rows == 0
    spec = pl.BlockSpec((rows, N_SEL), lambda i: (i, 0))
    return pl.pallas_call(
        _peer_act_kernel, grid=(T // rows,), in_specs=[spec, spec], out_specs=spec,
        out_shape=jax.ShapeDtypeStruct((T, N_SEL), jnp.float32),
        compiler_params=pltpu.CompilerParams(dimension_semantics=("arbitrary",)),
        name="peer_act",
    )(hpre, gate)


def _residual_kernel(h_ref, c_ref, g_ref, o_ref, *, final_norm):
    y = h_ref[...] + c_ref[...]
    o_ref[...] = _rmsnorm_rows(y, g_ref[...]) if final_norm else y


def _residual(h, c, gfin, *, final_norm):
    T = h.shape[0]
    rows = ELEMENTWISE_ROWS
    assert T % rows == 0
    spec = pl.BlockSpec((rows, D_MODEL), lambda i: (i, 0))
    return pl.pallas_call(
        functools.partial(_residual_kernel, final_norm=final_norm), grid=(T // rows,),
        in_specs=[spec, spec, pl.BlockSpec((1, D_MODEL), lambda i: (0, 0))], out_specs=spec,
        out_shape=jax.ShapeDtypeStruct((T, D_MODEL), jnp.float32),
        compiler_params=pltpu.CompilerParams(dimension_semantics=("arbitrary",)),
        name="peer_residual",
    )(h, c, gfin.reshape(1, D_MODEL))


def _peer_experts(eid, xn, gate, h, gfin, u_tabs, v_tabs, layer, *, final_norm, after=None, act_after=None):
    if after is not None:
        eid, _ = lax.optimization_barrier((eid, after))
    hpre = _peer_hidden_sc(eid, xn, u_tabs, layer)
    if act_after is not None:
        hpre, _ = lax.optimization_barrier((hpre, act_after))
    a = _peer_act(hpre, gate)
    c = _peer_combine_sc(eid, a, v_tabs, layer)
    return _residual(h, c, gfin, final_norm=final_norm), c


KV_WIDTH = N_KV_HEADS * HEAD_DIM
BF16 = jnp.bfloat16


def _qkv(h, gkv, gq, wkv_ref, wq_ref):
    kv = jnp.dot(_rmsnorm_rows(h, gkv).astype(BF16), wkv_ref[...], preferred_element_type=jnp.float32)
    q = jnp.dot(_rmsnorm_rows(h, gq).astype(BF16), wq_ref[...], preferred_element_type=jnp.float32)
    return kv, q


def _sink_softmax_pv(parts, sink):
    m = sink
    for s, _ in parts:
        m = jnp.maximum(m, jnp.max(s, axis=-1, keepdims=True))
    den = jnp.exp(sink - m)
    acc = None
    for s, v in parts:
        e = jnp.exp(s - m)
        den = den + jnp.sum(e, axis=-1, keepdims=True)
        pv = jnp.dot(e.astype(BF16), v.astype(BF16), preferred_element_type=jnp.float32)
        acc = pv if acc is None else acc + pv
    return acc / den


def _nt_dot(a, b):
    return lax.dot_general(a.astype(BF16), b.astype(BF16), (((1,), (1,)), ((), ())),
                           preferred_element_type=jnp.float32)


def _attn_prompt_kernel(h_ref, gkv_ref, gq_ref, wkv_ref, wq_ref, wo_ref, sink_ref,
                        o_ref, kvw_ref, prev_ref):
    blk = pl.program_id(1)
    h = h_ref[...]
    kv, q = _qkv(h, gkv_ref[...], gq_ref[...], wkv_ref, wq_ref)
    kvw_ref[0] = kv

    @pl.when(blk == 0)
    def _():
        prev_ref[...] = jnp.zeros_like(prev_ref)

    prev = prev_ref[...]
    qi = lax.broadcasted_iota(jnp.int32, (WINDOW, WINDOW), 0)
    kj = lax.broadcasted_iota(jnp.int32, (WINDOW, WINDOW), 1)
    prev_ok = jnp.logical_and(kj > qi, blk > 0)
    cur_ok = kj <= qi
    heads = []
    for kvh in range(N_KV_HEADS):
        ks = slice(kvh * HEAD_DIM, (kvh + 1) * HEAD_DIM)
        vs = slice(KV_WIDTH + kvh * HEAD_DIM, KV_WIDTH + (kvh + 1) * HEAD_DIM)
        for g in range(Q_PER_KV):
            hq = kvh * Q_PER_KV + g
            qh = q[:, hq * HEAD_DIM:(hq + 1) * HEAD_DIM]
            sp = jnp.where(prev_ok, _nt_dot(qh, prev[:, ks]) * ATTN_SCALE, -jnp.inf)
            sc = jnp.where(cur_ok, _nt_dot(qh, kv[:, ks]) * ATTN_SCALE, -jnp.inf)
            heads.append(_sink_softmax_pv([(sp, prev[:, vs]), (sc, kv[:, vs])], sink_ref[hq]))
    o = jnp.concatenate(heads, axis=1)
    o_ref[...] = h + jnp.dot(o.astype(BF16), wo_ref[...], preferred_element_type=jnp.float32)
    prev_ref[...] = kv


def _attn_prompt(h, gkv, gq, wkv, wq, wo, sinks, *, n_seq, seq_len):
    nb = seq_len // WINDOW
    row_spec = pl.BlockSpec((WINDOW, D_MODEL), lambda n, b: (n * nb + b, 0))
    full = lambda shape: pl.BlockSpec(shape, lambda n, b: (0,) * len(shape))
    return pl.pallas_call(
        _attn_prompt_kernel,
        grid=(n_seq, nb),
        in_specs=[
            row_spec, full((1, D_MODEL)), full((1, D_MODEL)),
            full((D_MODEL, 2 * KV_WIDTH)), full((D_MODEL, D_MODEL)), full((D_MODEL, D_MODEL)),
            pl.BlockSpec(memory_space=pltpu.SMEM),
        ],
        out_specs=[row_spec, pl.BlockSpec((1, WINDOW, 2 * KV_WIDTH), lambda n, b: (n, 0, 0))],
        out_shape=[
            jax.ShapeDtypeStruct(h.shape, jnp.float32),
            jax.ShapeDtypeStruct((n_seq, WINDOW, 2 * KV_WIDTH), jnp.float32),
        ],
        scratch_shapes=[pltpu.VMEM((WINDOW, 2 * KV_WIDTH), jnp.float32)],
        compiler_params=pltpu.CompilerParams(
            dimension_semantics=("arbitrary", "arbitrary"), vmem_limit_bytes=VMEM_LIMIT_BYTES),
        name="attn_prompt",
    )(h, gkv.reshape(1, D_MODEL), gq.reshape(1, D_MODEL), wkv, wq, wo, sinks)


ATTN_SEQS = 16


def _attn_sample_kernel(h_ref, ck_ref, cv_ref, gkv_ref, gq_ref, wkv_ref, wq_ref, wo_ref, sink_ref,
                        o_ref, kw_ref, vw_ref, kv_ref, q_ref, att_ref, *, n_new):
    h = h_ref[...]
    kv, q = _qkv(h, gkv_ref[...], gq_ref[...], wkv_ref, wq_ref)
    kv_ref[...] = kv
    q_ref[...] = q
    n_seq = h.shape[0] // n_new
    rows = Q_PER_KV * n_new
    qpos_c = lax.broadcasted_iota(jnp.int32, (rows, WINDOW), 0) % n_new
    cache_ok = lax.broadcasted_iota(jnp.int32, (rows, WINDOW), 1) > qpos_c
    qpos_n = lax.broadcasted_iota(jnp.int32, (rows, n_new), 0) % n_new
    new_ok = lax.broadcasted_iota(jnp.int32, (rows, n_new), 1) <= qpos_n

    def per_seq(n, carry):
        r0 = pl.multiple_of(n * n_new, n_new)
        kvn = kv_ref[pl.ds(r0, n_new), :]
        qn = q_ref[pl.ds(r0, n_new), :]
        ck = ck_ref[n]
        cv = cv_ref[n]
        kw_ref[n, 0:WINDOW - n_new, :] = ck[n_new:, :]
        kw_ref[n, WINDOW - n_new:WINDOW, :] = kvn[:, 0:KV_WIDTH]
        vw_ref[n, 0:WINDOW - n_new, :] = cv[n_new:, :]
        vw_ref[n, WINDOW - n_new:WINDOW, :] = kvn[:, KV_WIDTH:2 * KV_WIDTH]
        outs = []
        for kvh in range(N_KV_HEADS):
            ks = slice(kvh * HEAD_DIM, (kvh + 1) * HEAD_DIM)
            vs = slice(KV_WIDTH + kvh * HEAD_DIM, KV_WIDTH + (kvh + 1) * HEAD_DIM)
            qs = jnp.concatenate(
                [qn[:, (kvh * Q_PER_KV + g) * HEAD_DIM:(kvh * Q_PER_KV + g + 1) * HEAD_DIM] for g in range(Q_PER_KV)],
                axis=0)
            s_c = jnp.where(cache_ok, _nt_dot(qs, ck[:, ks]) * ATTN_SCALE, -jnp.inf)
            s_n = jnp.where(new_ok, _nt_dot(qs, kvn[:, ks]) * ATTN_SCALE, -jnp.inf)
            sink = sink_ref[kvh][:, 0:1]
            o = _sink_softmax_pv([(s_c, cv[:, ks]), (s_n, kvn[:, vs])], sink)
            outs += [o[g * n_new:(g + 1) * n_new, :] for g in range(Q_PER_KV)]
        att_ref[pl.ds(r0, n_new), :] = jnp.concatenate(outs, axis=1)
        return carry

    lax.fori_loop(0, n_seq, per_seq, 0)
    o_ref[...] = h + jnp.dot(att_ref[...].astype(BF16), wo_ref[...], preferred_element_type=jnp.float32)


def _attn_sample(h, cache_k, cache_v, gkv, gq, wkv, wq, wo, sinks, *, n_new):
    n_seq = cache_k.shape[0]
    sb = ATTN_SEQS
    rows = sb * n_new
    row_spec = pl.BlockSpec((rows, D_MODEL), lambda i: (i, 0))
    win_spec = pl.BlockSpec((sb, WINDOW, KV_WIDTH), lambda i: (i, 0, 0))
    full = lambda shape: pl.BlockSpec(shape, lambda i: (0,) * len(shape))
    sink_rows = jnp.repeat(sinks.reshape(N_KV_HEADS, Q_PER_KV), n_new, axis=1)[:, :, None]
    sink_rows = jnp.broadcast_to(sink_rows, (N_KV_HEADS, Q_PER_KV * n_new, LANES))
    return pl.pallas_call(
        functools.partial(_attn_sample_kernel, n_new=n_new),
        grid=(n_seq // sb,),
        in_specs=[
            row_spec, win_spec, win_spec, full((1, D_MODEL)), full((1, D_MODEL)),
            full((D_MODEL, 2 * KV_WIDTH)), full((D_MODEL, D_MODEL)), full((D_MODEL, D_MODEL)),
            full((N_KV_HEADS, Q_PER_KV * n_new, LANES)),
        ],
        out_specs=[row_spec, win_spec, win_spec],
        out_shape=[
            jax.ShapeDtypeStruct(h.shape, jnp.float32),
            jax.ShapeDtypeStruct((n_seq, WINDOW, KV_WIDTH), jnp.float32),
            jax.ShapeDtypeStruct((n_seq, WINDOW, KV_WIDTH), jnp.float32),
        ],
        scratch_shapes=[
            pltpu.VMEM((rows, 2 * KV_WIDTH), jnp.float32),
            pltpu.VMEM((rows, D_MODEL), jnp.float32),
            pltpu.VMEM((rows, D_MODEL), jnp.float32),
        ],
        compiler_params=pltpu.CompilerParams(
            dimension_semantics=("arbitrary",), vmem_limit_bytes=VMEM_LIMIT_BYTES),
        name="attn_sample",
    )(h, cache_k, cache_v, gkv.reshape(1, D_MODEL), gq.reshape(1, D_MODEL), wkv, wq, wo, sink_rows)


def kernel(x_prompt, x_sample, state_ssm_re, state_ssm_im, cache_k_win, cache_v_win, norm_mix, norm_ffn, norm_kv, norm_final, ssm_lam_re, ssm_lam_im, ssm_log_dt, ssm_b_re, ssm_b_im, ssm_c_re, ssm_c_im, ssm_d, ssm_w_glu, w_kv, w_q, attn_sinks, w_o, peer_w_q, peer_sub_keys, peer_u, peer_v):
    bmat, cmat, apr, api = _s5_discretize(ssm_lam_re[0], ssm_lam_im[0], ssm_log_dt[0], ssm_b_re[0], ssm_b_im[0], ssm_c_re[0], ssm_c_im[0])
    cmat = cmat.astype(jnp.bfloat16)
    wglu = ssm_w_glu[0].astype(jnp.bfloat16)
    wkv = w_kv.astype(BF16)
    wq = w_q[0].astype(BF16)
    wo = w_o[0].astype(BF16)
    peer_wq = [peer_w_q[layer].astype(BF16) for layer in range(2)]
    peer_sk = [peer_sub_keys[layer].astype(BF16) for layer in range(2)]

    def peer(h, layer, final_norm, after=None, act_after=None):
        xn, eid_t, gate_t = _peer_route(h, norm_ffn[layer], peer_wq[layer], peer_sk[layer])
        return _peer_experts(eid_t.T, xn, gate_t.T, h, norm_final, peer_u, peer_v, layer, final_norm=final_norm,
                             after=after, act_after=act_after)

    win = lambda a: a.reshape(a.shape[0], WINDOW, N_KV_HEADS, HEAD_DIM)

    seq_len = x_prompt.shape[1]
    xp_all = x_prompt.reshape(-1, D_MODEL)

    def prompt_s5(seq0, n_p, tie=None):
        z0 = jnp.zeros((n_p, N_CHUNKS, 1, CHUNK_STATE), jnp.float32)
        if tie is not None:
            z0, _ = lax.optimization_barrier((z0, tie))
        return _s5_mixer(xp_all, z0, z0, norm_mix[0], ssm_d[0], bmat, cmat, apr, api,
                         chain=True, seq_len=seq_len, row0=seq0 * seq_len)

    def trunk_prompt(seq0, n_p, s5_out, after):
        z, sr, si = s5_out
        h1 = _glu_residual(z, xp_all, wglu, res_row0=seq0 * seq_len)
        xn, eid_t, gate_t = _peer_route(h1, norm_ffn[0], peer_wq[0], peer_sk[0])
        h2, _ = _peer_experts(eid_t.T, xn, gate_t.T, h1, norm_final, peer_u, peer_v, 0, final_norm=False, after=after)
        h3, kvw = _attn_prompt(h2, norm_kv, norm_mix[1], wkv, wq, wo, attn_sinks[0], n_seq=n_p, seq_len=seq_len)
        y, _ = peer(h3, 1, True)
        return (y.reshape(n_p, seq_len, D_MODEL), _chunks_to_state(sr), _chunks_to_state(si),
                win(kvw[:, :, :KV_WIDTH]), win(kvw[:, :, KV_WIDTH:])), eid_t

    def trunk_sample(x_s, s_re, s_im, c_k, c_v, act_after):
        n_s = x_s.shape[0]
        xs = x_s.reshape(-1, D_MODEL)
        z, sr, si = _s5_mixer(xs, _state_to_chunks(s_re), _state_to_chunks(s_im), norm_mix[0], ssm_d[0],
                              bmat, cmat, apr, api, chain=False, seq_len=x_s.shape[1])
        h2, c0 = peer(_glu_residual(z, xs, wglu), 0, False, act_after=act_after)
        h3, kw, vw = _attn_sample(h2, c_k.reshape(n_s, WINDOW, KV_WIDTH), c_v.reshape(n_s, WINDOW, KV_WIDTH),
                                  norm_kv, norm_mix[1], wkv, wq, wo, attn_sinks[0], n_new=x_s.shape[1])
        y, _ = peer(h3, 1, True)
        return (y.reshape(x_s.shape), _chunks_to_state(sr), _chunks_to_state(si), win(kw), win(vw)), c0

    n_groups = 4
    gp = x_prompt.shape[0] // n_groups
    s5_first = prompt_s5(0, gp)
    (y_s, sre_s, sim_s, kw_s, vw_s), c0_s = trunk_sample(x_sample, state_ssm_re[0], state_ssm_im[0],
                                                         cache_k_win, cache_v_win, act_after=s5_first[0])
    outs, tie = [], None
    for i in range(n_groups):
        s5_out = s5_first if i == 0 else prompt_s5(i * gp, gp, tie=tie)
        out, tie = trunk_prompt(i * gp, gp, s5_out, after=c0_s)
        outs.append(out)
    y_p, sre_p, sim_p, kw_p, vw_p = [jnp.concatenate(parts, axis=0) for parts in zip(*outs)]
    return (y_p, y_s, sre_p[None], sim_p[None], kw_p, vw_p, sre_s[None], sim_s[None], kw_s, vw_s)
```

```python
import functools
import math

import jax
import jax.numpy as jnp
from jax import lax
from jax.experimental import pallas as pl
from jax.experimental.pallas import tpu as pltpu
from jax.experimental.pallas import tpu_sc as plsc

D_MODEL = 1024
GROUP_SIZE = 16
N_GROUPS = D_MODEL // GROUP_SIZE
STATE_DIM = 64
HEAD_DIM = 64
N_Q_HEADS = D_MODEL // HEAD_DIM
N_KV_HEADS = N_Q_HEADS // 8
Q_PER_KV = N_Q_HEADS // N_KV_HEADS
WINDOW = 128
ATTN_SCALE = 1.0 / math.sqrt(HEAD_DIM)
PEER_HEADS = 8
N_KEYS = 128
PEER_TOPK = 16
PEER_DHALF = 128
EPS = 1e-5

LANES = 128
SUBLANES = 8
VMEM_LIMIT_BYTES = 56 * 1024 * 1024

GROUPS_PER_CHUNK = LANES // GROUP_SIZE
N_CHUNKS = N_GROUPS // GROUPS_PER_CHUNK
CHUNK_STATE = GROUPS_PER_CHUNK * STATE_DIM
S5_ROWS = 256


def _rmsnorm_rows(x, g):
    r = lax.rsqrt(jnp.mean(x * x, axis=-1, keepdims=True) + EPS)
    return x * r * g


def _gelu(x):
    return 0.5 * x * (1.0 + lax.erf(x * (1.0 / math.sqrt(2.0))))


def _s5_discretize(lam_re, lam_im, log_dt, b_re, b_im, c_re, c_im):
    f32 = jnp.float32
    lr = lam_re.astype(f32)
    li = lam_im.astype(f32)
    dt = jnp.exp(log_dt.astype(f32))[:, None]
    mag = jnp.exp(lr * dt)
    ab_re = mag * jnp.cos(li * dt)
    ab_im = mag * jnp.sin(li * dt)
    den = lr * lr + li * li
    f_re = ((ab_re - 1.0) * lr + ab_im * li) / den
    f_im = (ab_im * lr - (ab_re - 1.0) * li) / den
    br = b_re.astype(f32)
    bi = b_im.astype(f32)
    bb_re = f_re[..., None] * br - f_im[..., None] * bi
    bb_im = f_re[..., None] * bi + f_im[..., None] * br
    eye = jnp.eye(GROUPS_PER_CHUNK, dtype=f32)

    def chunk_rows(v):
        return v.reshape(N_CHUNKS, 1, CHUNK_STATE)

    def in_blocks(bb):
        t = bb.reshape(N_CHUNKS, GROUPS_PER_CHUNK, STATE_DIM, GROUP_SIZE).transpose(0, 1, 3, 2)
        return jnp.einsum('mgjp,gh->mgjhp', t, eye).reshape(N_CHUNKS, LANES, CHUNK_STATE)

    def out_blocks(c):
        t = c.astype(f32).reshape(N_CHUNKS, GROUPS_PER_CHUNK, GROUP_SIZE, STATE_DIM).transpose(0, 1, 3, 2)
        return jnp.einsum('mgpj,gh->mgphj', t, eye).reshape(N_CHUNKS, CHUNK_STATE, LANES)

    bfull = jnp.concatenate([in_blocks(bb_re), in_blocks(bb_im)], axis=2)
    b_hi = bfull.astype(jnp.bfloat16)
    b_lo = (bfull - b_hi.astype(f32)).astype(jnp.bfloat16)
    bmat = (jnp.concatenate([b_hi, b_hi], axis=1), b_lo)
    cmat = jnp.concatenate([out_blocks(c_re), -out_blocks(c_im)], axis=1)
    pr, pi = [ab_re], [ab_im]
    for _ in range(SUBLANES - 1):
        pr, pi = pr + [pr[-1] * ab_re - pi[-1] * ab_im], pi + [pr[-1] * ab_im + pi[-1] * ab_re]
    apr = jnp.concatenate([chunk_rows(v) for v in pr], axis=1)
    api = jnp.concatenate([chunk_rows(v) for v in pi], axis=1)
    return bmat, cmat, apr, api


def _split_bf16(x):
    hi = x.astype(jnp.bfloat16)
    return hi, (x - hi.astype(jnp.float32)).astype(jnp.bfloat16)


def _s5_kernel(x_ref, g_ref, d_ref, bhh_ref, blo_ref, c_ref, apr_ref, api_ref, h0r_ref, h0i_ref,
               z_ref, sr_ref, si_ref, u_ref, us_ref, st_ref, cr_ref, ci_ref, *, chain, blocks_per_seq):
    rb = pl.program_id(0)
    m = pl.program_id(1)
    rows = x_ref.shape[0]
    n_tiles = rows // SUBLANES

    @pl.when(m == 0)
    def _():
        u = _rmsnorm_rows(x_ref[...], g_ref[...])
        for mm in range(N_CHUNKS):
            uc = u[:, mm * LANES:(mm + 1) * LANES]
            u_ref[mm] = uc
            hi, lo = _split_bf16(uc)
            us_ref[mm, :, 0:LANES] = hi
            us_ref[mm, :, LANES:2 * LANES] = lo

    u = u_ref[m]
    us = us_ref[m]
    st_ref[...] = (jnp.dot(us, bhh_ref[0], preferred_element_type=jnp.float32)
                   + jnp.dot(us[:, 0:LANES], blo_ref[0], preferred_element_type=jnp.float32))

    apr = apr_ref[0]
    api = api_ref[0]
    row = lax.broadcasted_iota(jnp.int32, (SUBLANES, CHUNK_STATE), 0)

    if chain:
        @pl.when(rb % blocks_per_seq == 0)
        def _():
            cr_ref[m] = h0r_ref[0, 0]
            ci_ref[m] = h0i_ref[0, 0]

    def tile_step(k, carry):
        r0 = pl.multiple_of(k * SUBLANES, SUBLANES)
        xr = st_ref[pl.ds(r0, SUBLANES), 0:CHUNK_STATE]
        xi = st_ref[pl.ds(r0, SUBLANES), CHUNK_STATE:2 * CHUNK_STATE]
        for d in (1, 2, 4):
            ar = apr[d - 1:d, :]
            ai = api[d - 1:d, :]
            sr = jnp.where(row >= d, pltpu.roll(xr, d, axis=0), 0.0)
            si = jnp.where(row >= d, pltpu.roll(xi, d, axis=0), 0.0)
            xr, xi = xr + ar * sr - ai * si, xi + ar * si + ai * sr
        if chain:
            cr, ci = carry
        else:
            cr = h0r_ref[k, 0]
            ci = h0i_ref[k, 0]
        hr = xr + apr * cr - api * ci
        hi = xi + apr * ci + api * cr
        st_ref[pl.ds(r0, SUBLANES), 0:CHUNK_STATE] = hr
        st_ref[pl.ds(r0, SUBLANES), CHUNK_STATE:2 * CHUNK_STATE] = hi
        lr_ = hr[SUBLANES - 1:SUBLANES, :]
        li_ = hi[SUBLANES - 1:SUBLANES, :]
        if chain:
            return lr_, li_
        sr_ref[k, m] = lr_
        si_ref[k, m] = li_
        return carry

    if chain:
        cr, ci = lax.fori_loop(0, n_tiles, tile_step, (cr_ref[m], ci_ref[m]))
        cr_ref[m] = cr
        ci_ref[m] = ci
        sr_ref[0, m] = cr
        si_ref[0, m] = ci
    else:
        lax.fori_loop(0, n_tiles, tile_step, 0)

    y = jnp.dot(st_ref[...].astype(jnp.bfloat16), c_ref[0], preferred_element_type=jnp.float32)
    y = y + d_ref[0] * u
    z_ref[...] = _gelu(y)


def _s5_mixer(x, h0r, h0i, g, d_skip, bmat, cmat_bf16, apr, api, *, chain, seq_len, row0=0):
    nseq = h0r.shape[0]
    T = nseq * seq_len
    rows = S5_ROWS
    blk0 = row0 // rows
    if chain:
        blocks_per_seq = seq_len // rows
        seq_blk = 1
        seq_map = lambda rb, m: (rb // blocks_per_seq, m, 0, 0)
        out_map = lambda rb, m: (rb // blocks_per_seq, 0, 0, 0)
    else:
        assert seq_len == SUBLANES
        blocks_per_seq = 1
        seq_blk = rows // SUBLANES
        seq_map = lambda rb, m: (rb, m, 0, 0)
        out_map = lambda rb, m: (rb, 0, 0, 0)
    grid = (T // rows, N_CHUNKS)
    kern = functools.partial(_s5_kernel, chain=chain, blocks_per_seq=blocks_per_seq)
    st_spec = pl.BlockSpec((seq_blk, 1, 1, CHUNK_STATE), seq_map)
    out_st_spec = pl.BlockSpec((seq_blk, N_CHUNKS, 1, CHUNK_STATE), out_map)
    z, sr, si = pl.pallas_call(
        kern,
        grid=grid,
        in_specs=[
            pl.BlockSpec((rows, D_MODEL), lambda rb, m: (rb + blk0, 0)),
            pl.BlockSpec((1, D_MODEL), lambda rb, m: (0, 0)),
            pl.BlockSpec((1, 1, LANES), lambda rb, m: (m, 0, 0)),
            pl.BlockSpec((1, 2 * LANES, 2 * CHUNK_STATE), lambda rb, m: (m, 0, 0)),
            pl.BlockSpec((1, LANES, 2 * CHUNK_STATE), lambda rb, m: (m, 0, 0)),
            pl.BlockSpec((1, 2 * CHUNK_STATE, LANES), lambda rb, m: (m, 0, 0)),
            pl.BlockSpec((1, SUBLANES, CHUNK_STATE), lambda rb, m: (m, 0, 0)),
            pl.BlockSpec((1, SUBLANES, CHUNK_STATE), lambda rb, m: (m, 0, 0)),
            st_spec, st_spec,
        ],
        out_specs=[
            pl.BlockSpec((rows, LANES), lambda rb, m: (rb, m)),
            out_st_spec, out_st_spec,
        ],
        out_shape=[
            jax.ShapeDtypeStruct((T, D_MODEL), jnp.float32),
            jax.ShapeDtypeStruct((nseq, N_CHUNKS, 1, CHUNK_STATE), jnp.float32),
            jax.ShapeDtypeStruct((nseq, N_CHUNKS, 1, CHUNK_STATE), jnp.float32),
        ],
        scratch_shapes=[
            pltpu.VMEM((N_CHUNKS, rows, LANES), jnp.float32),
            pltpu.VMEM((N_CHUNKS, rows, 2 * LANES), jnp.bfloat16),
            pltpu.VMEM((rows, 2 * CHUNK_STATE), jnp.float32),
            pltpu.VMEM((N_CHUNKS, 1, CHUNK_STATE), jnp.float32),
            pltpu.VMEM((N_CHUNKS, 1, CHUNK_STATE), jnp.float32),
        ],
        compiler_params=pltpu.CompilerParams(
            dimension_semantics=("arbitrary", "arbitrary"), vmem_limit_bytes=VMEM_LIMIT_BYTES),
        name="s5_mixer",
    )(x, g.reshape(1, D_MODEL), d_skip.reshape(N_CHUNKS, 1, LANES), bmat[0], bmat[1], cmat_bf16, apr, api,
      h0r, h0i)
    return z, sr, si


def _state_to_chunks(h):
    return h.reshape(h.shape[0], N_CHUNKS, 1, CHUNK_STATE)


def _chunks_to_state(s):
    return s.reshape(s.shape[0], N_GROUPS, STATE_DIM)


def _glu_kernel(z_ref, res_ref, w_ref, o_ref):
    zz = jnp.dot(z_ref[...].astype(jnp.bfloat16), w_ref[...], preferred_element_type=jnp.float32)
    a = zz[:, :D_MODEL]
    b = zz[:, D_MODEL:]
    o_ref[...] = res_ref[...] + a * (1.0 / (1.0 + jnp.exp(-b)))


def _glu_residual(z, res, w_bf16, res_row0=0):
    T = z.shape[0]
    rows = math.gcd(T, 512)
    blk0 = res_row0 // rows
    return pl.pallas_call(
        _glu_kernel,
        grid=(T // rows,),
        in_specs=[
            pl.BlockSpec((rows, D_MODEL), lambda i: (i, 0)),
            pl.BlockSpec((rows, D_MODEL), lambda i: (i + blk0, 0)),
            pl.BlockSpec((D_MODEL, 2 * D_MODEL), lambda i: (0, 0)),
        ],
        out_specs=pl.BlockSpec((rows, D_MODEL), lambda i: (i, 0)),
        out_shape=jax.ShapeDtypeStruct((T, D_MODEL), jnp.float32),
        compiler_params=pltpu.CompilerParams(
            dimension_semantics=("arbitrary",), vmem_limit_bytes=VMEM_LIMIT_BYTES),
        name="glu_residual",
    )(z, res, w_bf16)


ROUTE_ROWS = 256


def _topk_rows(s, payload):
    n_rows = s.shape[0]
    row = lax.broadcasted_iota(jnp.int32, s.shape, 0)
    vals, picks = [], []
    for _ in range(PEER_TOPK):
        m = jnp.max(s, axis=0, keepdims=True)
        pos = jnp.min(jnp.where(s == m, row, n_rows), axis=0, keepdims=True)
        sel = row == pos
        vals.append(m)
        if payload is None:
            picks.append(pos)
        else:
            picks.append(jnp.max(jnp.where(sel, payload, -1), axis=0, keepdims=True))
        s = jnp.where(sel, -jnp.inf, s)
    return jnp.concatenate(vals, axis=0), jnp.concatenate(picks, axis=0)


def _pair_rows(a0, a1, combine):
    half = PEER_TOPK // 2
    parts = [combine(a0[0:1, :], a1)]
    parts += [combine(a0[i:i + 1, :], a1[0:half, :]) for i in range(1, half)]
    parts.append(combine(a0[half:PEER_TOPK, :], a1[0:1, :]))
    return jnp.concatenate(parts, axis=0)


def _peer_route_kernel(h_ref, g_ref, wq_ref, sk_ref, xn_ref, eid_ref, gate_ref, xb_ref):
    hd = pl.program_id(1)

    @pl.when(hd == 0)
    def _():
        xn = _rmsnorm_rows(h_ref[...], g_ref[...])
        xn_ref[...] = xn
        xb_ref[...] = xn.astype(jnp.bfloat16)

    q = jnp.dot(xb_ref[...], wq_ref[...], preferred_element_type=jnp.float32)
    sv, si = [], []
    for c in range(2):
        qc = q[:, c * PEER_DHALF:(c + 1) * PEER_DHALF].astype(jnp.bfloat16)
        st = lax.dot_general(sk_ref[0, c], qc, (((1,), (1,)), ((), ())), preferred_element_type=jnp.float32)
        v, i = _topk_rows(st, None)
        sv.append(v)
        si.append(i)
    cand = _pair_rows(sv[0], sv[1], lambda a, b: a + b)
    cid = _pair_rows(si[0], si[1], lambda a, b: a * N_KEYS + b)
    fv, eid = _topk_rows(cand, cid)
    e = jnp.exp(fv - fv[0:1, :])
    gate_ref[...] = e / jnp.sum(e, axis=0, keepdims=True)
    eid_ref[...] = eid


def _peer_route(h, g, wq_bf16, sk_bf16):
    T = h.shape[0]
    rows = ROUTE_ROWS
    n_sel = PEER_HEADS * PEER_TOPK
    return pl.pallas_call(
        _peer_route_kernel,
        grid=(T // rows, PEER_HEADS),
        in_specs=[
            pl.BlockSpec((rows, D_MODEL), lambda tb, hd: (tb, 0)),
            pl.BlockSpec((1, D_MODEL), lambda tb, hd: (0, 0)),
            pl.BlockSpec((D_MODEL, 2 * PEER_DHALF), lambda tb, hd: (0, hd)),
            pl.BlockSpec((1, 2, N_KEYS, PEER_DHALF), lambda tb, hd: (hd, 0, 0, 0)),
        ],
        out_specs=[
            pl.BlockSpec((rows, D_MODEL), lambda tb, hd: (tb, 0)),
            pl.BlockSpec((PEER_TOPK, rows), lambda tb, hd: (hd, tb)),
            pl.BlockSpec((PEER_TOPK, rows), lambda tb, hd: (hd, tb)),
        ],
        out_shape=[
            jax.ShapeDtypeStruct((T, D_MODEL), jnp.float32),
            jax.ShapeDtypeStruct((n_sel, T), jnp.int32),
            jax.ShapeDtypeStruct((n_sel, T), jnp.float32),
        ],
        scratch_shapes=[pltpu.VMEM((rows, D_MODEL), jnp.bfloat16)],
        compiler_params=pltpu.CompilerParams(
            dimension_semantics=("arbitrary", "arbitrary"), vmem_limit_bytes=VMEM_LIMIT_BYTES),
        name="peer_route",
    )(h, g.reshape(1, D_MODEL), wq_bf16, sk_bf16)


N_SEL = PEER_HEADS * PEER_TOPK
SC_LANES = 16
GATHER_ROWS = PEER_TOPK
GATHERS_PER_TOKEN = N_SEL // GATHER_ROWS
GATHER_BUFS = 4
SC_TOKENS = 8
ACC_STRIDE = SC_LANES + 1


def _sc_gather_stream(wid, n_batches, tab_hbm, stage_srcs, stage_bufs, out_hbm, o_v, bufs, gsem, ssem, osem,
                      compute):
    idx_v = stage_bufs[0]

    def stage_copies(bi, slot):
        base = (wid * n_batches + bi) * SC_TOKENS
        return [pltpu.make_async_copy(src.at[pl.ds(base, SC_TOKENS)], buf.at[slot], ssem.at[slot])
                for src, buf in zip(stage_srcs, stage_bufs)]

    def out_copy(bi, slot):
        base = (wid * n_batches + bi) * SC_TOKENS
        return pltpu.make_async_copy(o_v.at[slot], out_hbm.at[pl.ds(base, SC_TOKENS)], osem.at[slot])

    def start(slot, t, kk, b):
        idx = idx_v[slot, t, pl.ds(kk * GATHER_ROWS, GATHER_ROWS)]
        pltpu.async_copy(tab_hbm.at[idx], bufs.at[b], gsem.at[b])

    def wait(b):
        pltpu.make_async_copy(tab_hbm.at[pl.ds(0, GATHER_ROWS)], bufs.at[b], gsem.at[b]).wait()

    for c in stage_copies(0, 0):
        c.start()
    for c in stage_copies(0, 0):
        c.wait()
    for q in range(GATHER_BUFS - 1):
        start(0, q // GATHERS_PER_TOKEN, q % GATHERS_PER_TOKEN, q % GATHER_BUFS)

    def batch(bi, carry):
        slot = bi % 2
        has_next = bi + 1 < n_batches

        @pl.when(has_next)
        def _():
            for c in stage_copies(bi + 1, 1 - slot):
                c.start()

        @pl.when(bi >= 2)
        def _():
            out_copy(bi, slot).wait()

        def tok(t, carry):
            @pl.when(jnp.logical_and(t == SC_TOKENS - 1, has_next))
            def _():
                for c in stage_copies(bi + 1, 1 - slot):
                    c.wait()

            for kk in range(GATHERS_PER_TOKEN):
                nq = kk + GATHER_BUFS - 1
                nk, nb = nq % GATHERS_PER_TOKEN, nq % GATHER_BUFS
                if nq < GATHERS_PER_TOKEN:
                    start(slot, t, nk, nb)
                else:
                    @pl.when(t + 1 < SC_TOKENS)
                    def _():
                        start(slot, t + 1, nk, nb)

                    @pl.when(jnp.logical_and(t + 1 == SC_TOKENS, has_next))
                    def _():
                        start(1 - slot, 0, nk, nb)

                wait(kk % GATHER_BUFS)
                compute(slot, t, kk, kk % GATHER_BUFS)
            return carry

        lax.fori_loop(0, SC_TOKENS, tok, 0)
        out_copy(bi, slot).start()
        return carry

    lax.fori_loop(0, n_batches, batch, 0)
    if n_batches >= 2:
        out_copy(n_batches - 2, (n_batches - 2) % 2).wait()
    out_copy(n_batches - 1, (n_batches - 1) % 2).wait()


def _sc_mesh_and_batches(n_tokens):
    info = plsc.get_sparse_core_info()
    assert info.num_lanes == SC_LANES
    n_workers = info.num_cores * info.num_subcores
    assert n_tokens % (n_workers * SC_TOKENS) == 0
    mesh = plsc.VectorSubcoreMesh(core_axis_name="c", subcore_axis_name="s")
    return info, mesh, n_tokens // (n_workers * SC_TOKENS)


def _peer_hidden_sc(eid, xn, u_tabs, layer):
    T = eid.shape[0]
    info, mesh, n_batches = _sc_mesh_and_batches(T)

    @functools.partial(
        pl.kernel, mesh=mesh,
        out_type=jax.ShapeDtypeStruct((T, N_SEL), jnp.float32),
        scratch_types=[
            pltpu.VMEM((2, SC_TOKENS, N_SEL), jnp.int32),
            pltpu.VMEM((2, SC_TOKENS, D_MODEL), jnp.float32),
            pltpu.VMEM((2, SC_TOKENS, N_SEL), jnp.float32),
            pltpu.VMEM((GATHER_BUFS, GATHER_ROWS, D_MODEL), jnp.float32),
            pltpu.VMEM((GATHER_ROWS * ACC_STRIDE,), jnp.float32),
            pltpu.SemaphoreType.DMA((GATHER_BUFS,)),
            pltpu.SemaphoreType.DMA((2,)),
            pltpu.SemaphoreType.DMA((2,)),
        ],
        compiler_params=pltpu.CompilerParams(needs_layout_passes=False),
        name="peer_hidden_sc",
    )
    def k(eid_hbm, xn_hbm, u_hbm, out_hbm, idx_v, x_v, o_v, bufs, acc_v, gsem, ssem, osem):
        wid = lax.axis_index("s") * info.num_cores + lax.axis_index("c")
        lane = lax.iota(jnp.int32, SC_LANES)
        zero = jnp.zeros((SC_LANES,), jnp.float32)

        def compute(slot, t, kk, b):
            @plsc.parallel_loop(0, D_MODEL // SC_LANES, carry=(zero,) * GATHER_ROWS)
            def accs(c, accs):
                xc = x_v[slot, t, pl.ds(c * SC_LANES, SC_LANES)]
                return tuple(accs[r] + bufs[b, r, pl.ds(c * SC_LANES, SC_LANES)] * xc for r in range(GATHER_ROWS))

            for r in range(GATHER_ROWS):
                acc_v[pl.ds(r * ACC_STRIDE, SC_LANES)] = accs[r]
            tot = zero
            for c in range(SC_LANES):
                tot = tot + plsc.load_gather(acc_v, [lane * ACC_STRIDE + c])
            o_v[slot, t, pl.ds(kk * GATHER_ROWS, GATHER_ROWS)] = tot

        _sc_gather_stream(wid, n_batches, u_hbm.at[layer], [eid_hbm, xn_hbm], [idx_v, x_v], out_hbm, o_v, bufs,
                          gsem, ssem, osem, compute)

    return k(eid, xn, u_tabs)


def _peer_combine_sc(eid, a, v_tabs, layer):
    T = eid.shape[0]
    info, mesh, n_batches = _sc_mesh_and_batches(T)

    @functools.partial(
        pl.kernel, mesh=mesh,
        out_type=jax.ShapeDtypeStruct((T, D_MODEL), jnp.float32),
        scratch_types=[
            pltpu.VMEM((2, SC_TOKENS, N_SEL), jnp.int32),
            pltpu.VMEM((2, SC_TOKENS, N_SEL), jnp.float32),
            pltpu.VMEM((2, SC_TOKENS, D_MODEL), jnp.float32),
            pltpu.VMEM((GATHER_BUFS, GATHER_ROWS, D_MODEL), jnp.float32),
            pltpu.SemaphoreType.DMA((GATHER_BUFS,)),
            pltpu.SemaphoreType.DMA((2,)),
            pltpu.SemaphoreType.DMA((2,)),
        ],
        compiler_params=pltpu.CompilerParams(needs_layout_passes=False),
        name="peer_combine_sc",
    )
    def k(eid_hbm, a_hbm, v_hbm, out_hbm, idx_v, a_v, o_v, bufs, gsem, ssem, osem):
        wid = lax.axis_index("s") * info.num_cores + lax.axis_index("c")

        def compute(slot, t, kk, b):
            svec = jnp.full((SC_LANES,), slot, jnp.int32)
            tvec = jnp.full((SC_LANES,), t, jnp.int32)
            ws = [plsc.load_gather(a_v, [svec, tvec, jnp.full((SC_LANES,), kk * GATHER_ROWS + r, jnp.int32)])
                  for r in range(GATHER_ROWS)]

            @plsc.parallel_loop(0, D_MODEL // SC_LANES, unroll=2)
            def _(c):
                sl = pl.ds(c * SC_LANES, SC_LANES)
                terms = [ws[r] * bufs[b, r, sl] for r in range(GATHER_ROWS)]
                if kk != 0:
                    terms.append(o_v[slot, t, sl])
                while len(terms) > 1:
                    pairs = [terms[i] + terms[i + 1] for i in range(0, len(terms) - 1, 2)]
                    terms = pairs + ([terms[-1]] if len(terms) % 2 else [])
                o_v[slot, t, sl] = terms[0]

        _sc_gather_stream(wid, n_batches, v_hbm.at[layer], [eid_hbm, a_hbm], [idx_v, a_v], out_hbm, o_v, bufs,
                          gsem, ssem, osem, compute)

    return k(eid, a, v_tabs)


def _peer_act_kernel(hp_ref, gate_ref, a_ref):
    a_ref[...] = _gelu(hp_ref[...]) * gate_ref[...]


ELEMENTWISE_ROWS = 1024


def _peer_act(hpre, gate):
    T = hpre.shape[0]
    rows = ELEMENTWISE_ROWS
    assert T % rows == 0
    spec = pl.BlockSpec((rows, N_SEL), lambda i: (i, 0))
    return pl.pallas_call(
        _peer_act_kernel, grid=(T // rows,), in_specs=[spec, spec], out_specs=spec,
        out_shape=jax.ShapeDtypeStruct((T, N_SEL), jnp.float32),
        compiler_params=pltpu.CompilerParams(dimension_semantics=("arbitrary",)),
        name="peer_act",
    )(hpre, gate)


def _residual_kernel(h_ref, c_ref, g_ref, o_ref, *, final_norm):
    y = h_ref[...] + c_ref[...]
    o_ref[...] = _rmsnorm_rows(y, g_ref[...]) if final_norm else y


def _residual(h, c, gfin, *, final_norm):
    T = h.shape[0]
    rows = ELEMENTWISE_ROWS
    assert T % rows == 0
    spec = pl.BlockSpec((rows, D_MODEL), lambda i: (i, 0))
    return pl.pallas_call(
        functools.partial(_residual_kernel, final_norm=final_norm), grid=(T // rows,),
        in_specs=[spec, spec, pl.BlockSpec((1, D_MODEL), lambda i: (0, 0))], out_specs=spec,
        out_shape=jax.ShapeDtypeStruct((T, D_MODEL), jnp.float32),
        compiler_params=pltpu.CompilerParams(dimension_semantics=("arbitrary",)),
        name="peer_residual",
    )(h, c, gfin.reshape(1, D_MODEL))


def _peer_experts(eid, xn, gate, h, gfin, u_tabs, v_tabs, layer, *, final_norm):
    hpre = _peer_hidden_sc(eid, xn, u_tabs, layer)
    a = _peer_act(hpre, gate)
    c = _peer_combine_sc(eid, a, v_tabs, layer)
    return _residual(h, c, gfin, final_norm=final_norm)


KV_WIDTH = N_KV_HEADS * HEAD_DIM
BF16 = jnp.bfloat16


def _qkv(h, gkv, gq, wkv_ref, wq_ref):
    kv = jnp.dot(_rmsnorm_rows(h, gkv).astype(BF16), wkv_ref[...], preferred_element_type=jnp.float32)
    q = jnp.dot(_rmsnorm_rows(h, gq).astype(BF16), wq_ref[...], preferred_element_type=jnp.float32)
    return kv, q


def _sink_softmax_pv(parts, sink):
    m = sink
    for s, _ in parts:
        m = jnp.maximum(m, jnp.max(s, axis=-1, keepdims=True))
    den = jnp.exp(sink - m)
    acc = None
    for s, v in parts:
        e = jnp.exp(s - m)
        den = den + jnp.sum(e, axis=-1, keepdims=True)
        pv = jnp.dot(e.astype(BF16), v.astype(BF16), preferred_element_type=jnp.float32)
        acc = pv if acc is None else acc + pv
    return acc / den


def _nt_dot(a, b):
    return lax.dot_general(a.astype(BF16), b.astype(BF16), (((1,), (1,)), ((), ())),
                           preferred_element_type=jnp.float32)


def _attn_prompt_kernel(h_ref, gkv_ref, gq_ref, wkv_ref, wq_ref, wo_ref, sink_ref,
                        o_ref, kvw_ref, prev_ref):
    blk = pl.program_id(1)
    h = h_ref[...]
    kv, q = _qkv(h, gkv_ref[...], gq_ref[...], wkv_ref, wq_ref)
    kvw_ref[0] = kv

    @pl.when(blk == 0)
    def _():
        prev_ref[...] = jnp.zeros_like(prev_ref)

    prev = prev_ref[...]
    qi = lax.broadcasted_iota(jnp.int32, (WINDOW, WINDOW), 0)
    kj = lax.broadcasted_iota(jnp.int32, (WINDOW, WINDOW), 1)
    prev_ok = jnp.logical_and(kj > qi, blk > 0)
    cur_ok = kj <= qi
    heads = []
    for kvh in range(N_KV_HEADS):
        ks = slice(kvh * HEAD_DIM, (kvh + 1) * HEAD_DIM)
        vs = slice(KV_WIDTH + kvh * HEAD_DIM, KV_WIDTH + (kvh + 1) * HEAD_DIM)
        for g in range(Q_PER_KV):
            hq = kvh * Q_PER_KV + g
            qh = q[:, hq * HEAD_DIM:(hq + 1) * HEAD_DIM]
            sp = jnp.where(prev_ok, _nt_dot(qh, prev[:, ks]) * ATTN_SCALE, -jnp.inf)
            sc = jnp.where(cur_ok, _nt_dot(qh, kv[:, ks]) * ATTN_SCALE, -jnp.inf)
            heads.append(_sink_softmax_pv([(sp, prev[:, vs]), (sc, kv[:, vs])], sink_ref[hq]))
    o = jnp.concatenate(heads, axis=1)
    o_ref[...] = h + jnp.dot(o.astype(BF16), wo_ref[...], preferred_element_type=jnp.float32)
    prev_ref[...] = kv


def _attn_prompt(h, gkv, gq, wkv, wq, wo, sinks, *, n_seq, seq_len):
    nb = seq_len // WINDOW
    row_spec = pl.BlockSpec((WINDOW, D_MODEL), lambda n, b: (n * nb + b, 0))
    full = lambda shape: pl.BlockSpec(shape, lambda n, b: (0,) * len(shape))
    return pl.pallas_call(
        _attn_prompt_kernel,
        grid=(n_seq, nb),
        in_specs=[
            row_spec, full((1, D_MODEL)), full((1, D_MODEL)),
            full((D_MODEL, 2 * KV_WIDTH)), full((D_MODEL, D_MODEL)), full((D_MODEL, D_MODEL)),
            pl.BlockSpec(memory_space=pltpu.SMEM),
        ],
        out_specs=[row_spec, pl.BlockSpec((1, WINDOW, 2 * KV_WIDTH), lambda n, b: (n, 0, 0))],
        out_shape=[
            jax.ShapeDtypeStruct(h.shape, jnp.float32),
            jax.ShapeDtypeStruct((n_seq, WINDOW, 2 * KV_WIDTH), jnp.float32),
        ],
        scratch_shapes=[pltpu.VMEM((WINDOW, 2 * KV_WIDTH), jnp.float32)],
        compiler_params=pltpu.CompilerParams(
            dimension_semantics=("arbitrary", "arbitrary"), vmem_limit_bytes=VMEM_LIMIT_BYTES),
        name="attn_prompt",
    )(h, gkv.reshape(1, D_MODEL), gq.reshape(1, D_MODEL), wkv, wq, wo, sinks)


ATTN_SEQS = 16


def _attn_sample_kernel(h_ref, ck_ref, cv_ref, gkv_ref, gq_ref, wkv_ref, wq_ref, wo_ref, sink_ref,
                        o_ref, kw_ref, vw_ref, kv_ref, q_ref, att_ref, *, n_new):
    h = h_ref[...]
    kv, q = _qkv(h, gkv_ref[...], gq_ref[...], wkv_ref, wq_ref)
    kv_ref[...] = kv
    q_ref[...] = q
    n_seq = h.shape[0] // n_new
    rows = Q_PER_KV * n_new
    qpos_c = lax.broadcasted_iota(jnp.int32, (rows, WINDOW), 0) % n_new
    cache_ok = lax.broadcasted_iota(jnp.int32, (rows, WINDOW), 1) > qpos_c
    qpos_n = lax.broadcasted_iota(jnp.int32, (rows, n_new), 0) % n_new
    new_ok = lax.broadcasted_iota(jnp.int32, (rows, n_new), 1) <= qpos_n

    def per_seq(n, carry):
        r0 = pl.multiple_of(n * n_new, n_new)
        kvn = kv_ref[pl.ds(r0, n_new), :]
        qn = q_ref[pl.ds(r0, n_new), :]
        ck = ck_ref[n]
        cv = cv_ref[n]
        kw_ref[n, 0:WINDOW - n_new, :] = ck[n_new:, :]
        kw_ref[n, WINDOW - n_new:WINDOW, :] = kvn[:, 0:KV_WIDTH]
        vw_ref[n, 0:WINDOW - n_new, :] = cv[n_new:, :]
        vw_ref[n, WINDOW - n_new:WINDOW, :] = kvn[:, KV_WIDTH:2 * KV_WIDTH]
        outs = []
        for kvh in range(N_KV_HEADS):
            ks = slice(kvh * HEAD_DIM, (kvh + 1) * HEAD_DIM)
            vs = slice(KV_WIDTH + kvh * HEAD_DIM, KV_WIDTH + (kvh + 1) * HEAD_DIM)
            qs = jnp.concatenate(
                [qn[:, (kvh * Q_PER_KV + g) * HEAD_DIM:(kvh * Q_PER_KV + g + 1) * HEAD_DIM] for g in range(Q_PER_KV)],
                axis=0)
            s_c = jnp.where(cache_ok, _nt_dot(qs, ck[:, ks]) * ATTN_SCALE, -jnp.inf)
            s_n = jnp.where(new_ok, _nt_dot(qs, kvn[:, ks]) * ATTN_SCALE, -jnp.inf)
            sink = sink_ref[kvh][:, 0:1]
            o = _sink_softmax_pv([(s_c, cv[:, ks]), (s_n, kvn[:, vs])], sink)
            outs += [o[g * n_new:(g + 1) * n_new, :] for g in range(Q_PER_KV)]
        att_ref[pl.ds(r0, n_new), :] = jnp.concatenate(outs, axis=1)
        return carry

    lax.fori_loop(0, n_seq, per_seq, 0)
    o_ref[...] = h + jnp.dot(att_ref[...].astype(BF16), wo_ref[...], preferred_element_type=jnp.float32)


def _attn_sample(h, cache_k, cache_v, gkv, gq, wkv, wq, wo, sinks, *, n_new):
    n_seq = cache_k.shape[0]
    sb = ATTN_SEQS
    rows = sb * n_new
    row_spec = pl.BlockSpec((rows, D_MODEL), lambda i: (i, 0))
    win_spec = pl.BlockSpec((sb, WINDOW, KV_WIDTH), lambda i: (i, 0, 0))
    full = lambda shape: pl.BlockSpec(shape, lambda i: (0,) * len(shape))
    sink_rows = jnp.repeat(sinks.reshape(N_KV_HEADS, Q_PER_KV), n_new, axis=1)[:, :, None]
    sink_rows = jnp.broadcast_to(sink_rows, (N_KV_HEADS, Q_PER_KV * n_new, LANES))
    return pl.pallas_call(
        functools.partial(_attn_sample_kernel, n_new=n_new),
        grid=(n_seq // sb,),
        in_specs=[
            row_spec, win_spec, win_spec, full((1, D_MODEL)), full((1, D_MODEL)),
            full((D_MODEL, 2 * KV_WIDTH)), full((D_MODEL, D_MODEL)), full((D_MODEL, D_MODEL)),
            full((N_KV_HEADS, Q_PER_KV * n_new, LANES)),
        ],
        out_specs=[row_spec, win_spec, win_spec],
        out_shape=[
            jax.ShapeDtypeStruct(h.shape, jnp.float32),
            jax.ShapeDtypeStruct((n_seq, WINDOW, KV_WIDTH), jnp.float32),
            jax.ShapeDtypeStruct((n_seq, WINDOW, KV_WIDTH), jnp.float32),
        ],
        scratch_shapes=[
            pltpu.VMEM((rows, 2 * KV_WIDTH), jnp.float32),
            pltpu.VMEM((rows, D_MODEL), jnp.float32),
            pltpu.VMEM((rows, D_MODEL), jnp.float32),
        ],
        compiler_params=pltpu.CompilerParams(
            dimension_semantics=("arbitrary",), vmem_limit_bytes=VMEM_LIMIT_BYTES),
        name="attn_sample",
    )(h, cache_k, cache_v, gkv.reshape(1, D_MODEL), gq.reshape(1, D_MODEL), wkv, wq, wo, sink_rows)


def kernel(x_prompt, x_sample, state_ssm_re, state_ssm_im, cache_k_win, cache_v_win, norm_mix, norm_ffn, norm_kv, norm_final, ssm_lam_re, ssm_lam_im, ssm_log_dt, ssm_b_re, ssm_b_im, ssm_c_re, ssm_c_im, ssm_d, ssm_w_glu, w_kv, w_q, attn_sinks, w_o, peer_w_q, peer_sub_keys, peer_u, peer_v):
    bmat, cmat, apr, api = _s5_discretize(ssm_lam_re[0], ssm_lam_im[0], ssm_log_dt[0], ssm_b_re[0], ssm_b_im[0], ssm_c_re[0], ssm_c_im[0])
    cmat = cmat.astype(jnp.bfloat16)
    wglu = ssm_w_glu[0].astype(jnp.bfloat16)
    wkv = w_kv.astype(BF16)
    wq = w_q[0].astype(BF16)
    wo = w_o[0].astype(BF16)
    peer_wq = [peer_w_q[layer].astype(BF16) for layer in range(2)]
    peer_sk = [peer_sub_keys[layer].astype(BF16) for layer in range(2)]

    def peer(h, layer, final_norm):
        xn, eid_t, gate_t = _peer_route(h, norm_ffn[layer], peer_wq[layer], peer_sk[layer])
        return _peer_experts(eid_t.T, xn, gate_t.T, h, norm_final, peer_u, peer_v, layer, final_norm=final_norm)

    win = lambda a: a.reshape(a.shape[0], WINDOW, N_KV_HEADS, HEAD_DIM)

    seq_len = x_prompt.shape[1]
    xp_all = x_prompt.reshape(-1, D_MODEL)

    def trunk_prompt(seq0, n_p):
        row0 = seq0 * seq_len
        z0 = jnp.zeros((n_p, N_CHUNKS, 1, CHUNK_STATE), jnp.float32)
        z, sr, si = _s5_mixer(xp_all, z0, z0, norm_mix[0], ssm_d[0], bmat, cmat, apr, api,
                              chain=True, seq_len=seq_len, row0=row0)
        h2 = peer(_glu_residual(z, xp_all, wglu, res_row0=row0), 0, False)
        h3, kvw = _attn_prompt(h2, norm_kv, norm_mix[1], wkv, wq, wo, attn_sinks[0], n_seq=n_p, seq_len=seq_len)
        y = peer(h3, 1, True)
        return (y.reshape(n_p, seq_len, D_MODEL), _chunks_to_state(sr), _chunks_to_state(si),
                win(kvw[:, :, :KV_WIDTH]), win(kvw[:, :, KV_WIDTH:]))

    def trunk_sample(x_s, s_re, s_im, c_k, c_v):
        n_s = x_s.shape[0]
        xs = x_s.reshape(-1, D_MODEL)
        z, sr, si = _s5_mixer(xs, _state_to_chunks(s_re), _state_to_chunks(s_im), norm_mix[0], ssm_d[0],
                              bmat, cmat, apr, api, chain=False, seq_len=x_s.shape[1])
        h2 = peer(_glu_residual(z, xs, wglu), 0, False)
        h3, kw, vw = _attn_sample(h2, c_k.reshape(n_s, WINDOW, KV_WIDTH), c_v.reshape(n_s, WINDOW, KV_WIDTH),
                                  norm_kv, norm_mix[1], wkv, wq, wo, attn_sinks[0], n_new=x_s.shape[1])
        y = peer(h3, 1, True)
        return y.reshape(x_s.shape), _chunks_to_state(sr), _chunks_to_state(si), win(kw), win(vw)

    y_s, sre_s, sim_s, kw_s, vw_s = trunk_sample(x_sample, state_ssm_re[0], state_ssm_im[0], cache_k_win, cache_v_win)
    n_groups = 4
    gp = x_prompt.shape[0] // n_groups
    outs = [trunk_prompt(i * gp, gp) for i in range(n_groups)]
    y_p, sre_p, sim_p, kw_p, vw_p = [jnp.concatenate(parts, axis=0) for parts in zip(*outs)]
    return (y_p, y_s, sre_p[None], sim_p[None], kw_p, vw_p, sre_s[None], sim_s[None], kw_s, vw_s)
```

```python
import functools
import math

import jax
import jax.numpy as jnp
from jax import lax
from jax.experimental import pallas as pl
from jax.experimental.pallas import tpu as pltpu
from jax.experimental.pallas import tpu_sc as plsc

D_MODEL = 1024
GROUP_SIZE = 16
N_GROUPS = D_MODEL // GROUP_SIZE
STATE_DIM = 64
HEAD_DIM = 64
N_Q_HEADS = D_MODEL // HEAD_DIM
N_KV_HEADS = N_Q_HEADS // 8
Q_PER_KV = N_Q_HEADS // N_KV_HEADS
WINDOW = 128
ATTN_SCALE = 1.0 / math.sqrt(HEAD_DIM)
PEER_HEADS = 8
N_KEYS = 128
PEER_TOPK = 16
PEER_DHALF = 128
EPS = 1e-5

LANES = 128
SUBLANES = 8
VMEM_LIMIT_BYTES = 56 * 1024 * 1024

GROUPS_PER_CHUNK = LANES // GROUP_SIZE
N_CHUNKS = N_GROUPS // GROUPS_PER_CHUNK
CHUNK_STATE = GROUPS_PER_CHUNK * STATE_DIM
S5_ROWS = 256


def _rmsnorm_rows(x, g):
    r = lax.rsqrt(jnp.mean(x * x, axis=-1, keepdims=True) + EPS)
    return x * r * g


def _gelu(x):
    return 0.5 * x * (1.0 + lax.erf(x * (1.0 / math.sqrt(2.0))))


def _s5_discretize(lam_re, lam_im, log_dt, b_re, b_im, c_re, c_im):
    f32 = jnp.float32
    lr = lam_re.astype(f32)
    li = lam_im.astype(f32)
    dt = jnp.exp(log_dt.astype(f32))[:, None]
    mag = jnp.exp(lr * dt)
    ab_re = mag * jnp.cos(li * dt)
    ab_im = mag * jnp.sin(li * dt)
    den = lr * lr + li * li
    f_re = ((ab_re - 1.0) * lr + ab_im * li) / den
    f_im = (ab_im * lr - (ab_re - 1.0) * li) / den
    br = b_re.astype(f32)
    bi = b_im.astype(f32)
    bb_re = f_re[..., None] * br - f_im[..., None] * bi
    bb_im = f_re[..., None] * bi + f_im[..., None] * br
    eye = jnp.eye(GROUPS_PER_CHUNK, dtype=f32)

    def chunk_rows(v):
        return v.reshape(N_CHUNKS, 1, CHUNK_STATE)

    def in_blocks(bb):
        t = bb.reshape(N_CHUNKS, GROUPS_PER_CHUNK, STATE_DIM, GROUP_SIZE).transpose(0, 1, 3, 2)
        return jnp.einsum('mgjp,gh->mgjhp', t, eye).reshape(N_CHUNKS, LANES, CHUNK_STATE)

    def out_blocks(c):
        t = c.astype(f32).reshape(N_CHUNKS, GROUPS_PER_CHUNK, GROUP_SIZE, STATE_DIM).transpose(0, 1, 3, 2)
        return jnp.einsum('mgpj,gh->mgphj', t, eye).reshape(N_CHUNKS, CHUNK_STATE, LANES)

    bfull = jnp.concatenate([in_blocks(bb_re), in_blocks(bb_im)], axis=2)
    b_hi = bfull.astype(jnp.bfloat16)
    b_lo = (bfull - b_hi.astype(f32)).astype(jnp.bfloat16)
    bmat = (jnp.concatenate([b_hi, b_hi], axis=1), b_lo)
    cmat = jnp.concatenate([out_blocks(c_re), -out_blocks(c_im)], axis=1)
    pr, pi = [ab_re], [ab_im]
    for _ in range(SUBLANES - 1):
        pr, pi = pr + [pr[-1] * ab_re - pi[-1] * ab_im], pi + [pr[-1] * ab_im + pi[-1] * ab_re]
    apr = jnp.concatenate([chunk_rows(v) for v in pr], axis=1)
    api = jnp.concatenate([chunk_rows(v) for v in pi], axis=1)
    return bmat, cmat, apr, api


def _split_bf16(x):
    hi = x.astype(jnp.bfloat16)
    return hi, (x - hi.astype(jnp.float32)).astype(jnp.bfloat16)


def _s5_kernel(x_ref, g_ref, d_ref, bhh_ref, blo_ref, c_ref, apr_ref, api_ref, h0r_ref, h0i_ref,
               z_ref, sr_ref, si_ref, u_ref, us_ref, st_ref, cr_ref, ci_ref, *, chain, blocks_per_seq):
    rb = pl.program_id(0)
    m = pl.program_id(1)
    rows = x_ref.shape[0]
    n_tiles = rows // SUBLANES

    @pl.when(m == 0)
    def _():
        u = _rmsnorm_rows(x_ref[...], g_ref[...])
        for mm in range(N_CHUNKS):
            uc = u[:, mm * LANES:(mm + 1) * LANES]
            u_ref[mm] = uc
            hi, lo = _split_bf16(uc)
            us_ref[mm, :, 0:LANES] = hi
            us_ref[mm, :, LANES:2 * LANES] = lo

    u = u_ref[m]
    us = us_ref[m]
    st_ref[...] = (jnp.dot(us, bhh_ref[m], preferred_element_type=jnp.float32)
                   + jnp.dot(us[:, 0:LANES], blo_ref[m], preferred_element_type=jnp.float32))

    apr = apr_ref[0]
    api = api_ref[0]
    row = lax.broadcasted_iota(jnp.int32, (SUBLANES, CHUNK_STATE), 0)

    if chain:
        @pl.when(rb % blocks_per_seq == 0)
        def _():
            cr_ref[m] = h0r_ref[0, 0]
            ci_ref[m] = h0i_ref[0, 0]

    def tile_step(k, carry):
        r0 = pl.multiple_of(k * SUBLANES, SUBLANES)
        xr = st_ref[pl.ds(r0, SUBLANES), 0:CHUNK_STATE]
        xi = st_ref[pl.ds(r0, SUBLANES), CHUNK_STATE:2 * CHUNK_STATE]
        for d in (1, 2, 4):
            ar = apr[d - 1:d, :]
            ai = api[d - 1:d, :]
            sr = jnp.where(row >= d, pltpu.roll(xr, d, axis=0), 0.0)
            si = jnp.where(row >= d, pltpu.roll(xi, d, axis=0), 0.0)
            xr, xi = xr + ar * sr - ai * si, xi + ar * si + ai * sr
        if chain:
            cr, ci = carry
        else:
            cr = h0r_ref[k, 0]
            ci = h0i_ref[k, 0]
        hr = xr + apr * cr - api * ci
        hi = xi + apr * ci + api * cr
        st_ref[pl.ds(r0, SUBLANES), 0:CHUNK_STATE] = hr
        st_ref[pl.ds(r0, SUBLANES), CHUNK_STATE:2 * CHUNK_STATE] = hi
        lr_ = hr[SUBLANES - 1:SUBLANES, :]
        li_ = hi[SUBLANES - 1:SUBLANES, :]
        if chain:
            return lr_, li_
        sr_ref[k, m] = lr_
        si_ref[k, m] = li_
        return carry

    if chain:
        cr, ci = lax.fori_loop(0, n_tiles, tile_step, (cr_ref[m], ci_ref[m]))
        cr_ref[m] = cr
        ci_ref[m] = ci
        sr_ref[0, m] = cr
        si_ref[0, m] = ci
    else:
        lax.fori_loop(0, n_tiles, tile_step, 0)

    y = jnp.dot(st_ref[...].astype(jnp.bfloat16), c_ref[m], preferred_element_type=jnp.float32)
    y = y + d_ref[0] * u
    z_ref[...] = _gelu(y)


def _s5_mixer(x, h0r, h0i, g, d_skip, bmat, cmat_bf16, apr, api, *, chain, seq_len, row0=0):
    nseq = h0r.shape[0]
    T = nseq * seq_len
    rows = S5_ROWS
    blk0 = row0 // rows
    if chain:
        blocks_per_seq = seq_len // rows
        seq_blk = 1
        seq_map = lambda rb, m: (rb // blocks_per_seq, m, 0, 0)
        out_map = lambda rb, m: (rb // blocks_per_seq, 0, 0, 0)
    else:
        assert seq_len == SUBLANES
        blocks_per_seq = 1
        seq_blk = rows // SUBLANES
        seq_map = lambda rb, m: (rb, m, 0, 0)
        out_map = lambda rb, m: (rb, 0, 0, 0)
    grid = (T // rows, N_CHUNKS)
    kern = functools.partial(_s5_kernel, chain=chain, blocks_per_seq=blocks_per_seq)
    st_spec = pl.BlockSpec((seq_blk, 1, 1, CHUNK_STATE), seq_map)
    out_st_spec = pl.BlockSpec((seq_blk, N_CHUNKS, 1, CHUNK_STATE), out_map)
    z, sr, si = pl.pallas_call(
        kern,
        grid=grid,
        in_specs=[
            pl.BlockSpec((rows, D_MODEL), lambda rb, m: (rb + blk0, 0)),
            pl.BlockSpec((1, D_MODEL), lambda rb, m: (0, 0)),
            pl.BlockSpec((1, 1, LANES), lambda rb, m: (m, 0, 0)),
            pl.BlockSpec((N_CHUNKS, 2 * LANES, 2 * CHUNK_STATE), lambda rb, m: (0, 0, 0)),
            pl.BlockSpec((N_CHUNKS, LANES, 2 * CHUNK_STATE), lambda rb, m: (0, 0, 0)),
            pl.BlockSpec((N_CHUNKS, 2 * CHUNK_STATE, LANES), lambda rb, m: (0, 0, 0)),
            pl.BlockSpec((1, SUBLANES, CHUNK_STATE), lambda rb, m: (m, 0, 0)),
            pl.BlockSpec((1, SUBLANES, CHUNK_STATE), lambda rb, m: (m, 0, 0)),
            st_spec, st_spec,
        ],
        out_specs=[
            pl.BlockSpec((rows, LANES), lambda rb, m: (rb, m)),
            out_st_spec, out_st_spec,
        ],
        out_shape=[
            jax.ShapeDtypeStruct((T, D_MODEL), jnp.float32),
            jax.ShapeDtypeStruct((nseq, N_CHUNKS, 1, CHUNK_STATE), jnp.float32),
            jax.ShapeDtypeStruct((nseq, N_CHUNKS, 1, CHUNK_STATE), jnp.float32),
        ],
        scratch_shapes=[
            pltpu.VMEM((N_CHUNKS, rows, LANES), jnp.float32),
            pltpu.VMEM((N_CHUNKS, rows, 2 * LANES), jnp.bfloat16),
            pltpu.VMEM((rows, 2 * CHUNK_STATE), jnp.float32),
            pltpu.VMEM((N_CHUNKS, 1, CHUNK_STATE), jnp.float32),
            pltpu.VMEM((N_CHUNKS, 1, CHUNK_STATE), jnp.float32),
        ],
        compiler_params=pltpu.CompilerParams(
            dimension_semantics=("arbitrary", "arbitrary"), vmem_limit_bytes=VMEM_LIMIT_BYTES),
        name="s5_mixer",
    )(x, g.reshape(1, D_MODEL), d_skip.reshape(N_CHUNKS, 1, LANES), bmat[0], bmat[1], cmat_bf16, apr, api,
      h0r, h0i)
    return z, sr, si


def _state_to_chunks(h):
    return h.reshape(h.shape[0], N_CHUNKS, 1, CHUNK_STATE)


def _chunks_to_state(s):
    return s.reshape(s.shape[0], N_GROUPS, STATE_DIM)


def _glu_kernel(z_ref, res_ref, w_ref, o_ref):
    zz = jnp.dot(z_ref[...].astype(jnp.bfloat16), w_ref[...], preferred_element_type=jnp.float32)
    a = zz[:, :D_MODEL]
    b = zz[:, D_MODEL:]
    o_ref[...] = res_ref[...] + a * (1.0 / (1.0 + jnp.exp(-b)))


def _glu_residual(z, res, w_bf16, res_row0=0):
    T = z.shape[0]
    rows = math.gcd(T, 512)
    blk0 = res_row0 // rows
    return pl.pallas_call(
        _glu_kernel,
        grid=(T // rows,),
        in_specs=[
            pl.BlockSpec((rows, D_MODEL), lambda i: (i, 0)),
            pl.BlockSpec((rows, D_MODEL), lambda i: (i + blk0, 0)),
            pl.BlockSpec((D_MODEL, 2 * D_MODEL), lambda i: (0, 0)),
        ],
        out_specs=pl.BlockSpec((rows, D_MODEL), lambda i: (i, 0)),
        out_shape=jax.ShapeDtypeStruct((T, D_MODEL), jnp.float32),
        compiler_params=pltpu.CompilerParams(
            dimension_semantics=("arbitrary",), vmem_limit_bytes=VMEM_LIMIT_BYTES),
        name="glu_residual",
    )(z, res, w_bf16)


ROUTE_ROWS = 256


def _topk_rows(s, payload):
    n_rows = s.shape[0]
    row = lax.broadcasted_iota(jnp.int32, s.shape, 0)
    vals, picks = [], []
    for _ in range(PEER_TOPK):
        m = jnp.max(s, axis=0, keepdims=True)
        pos = jnp.min(jnp.where(s == m, row, n_rows), axis=0, keepdims=True)
        sel = row == pos
        vals.append(m)
        if payload is None:
            picks.append(pos)
        else:
            picks.append(jnp.max(jnp.where(sel, payload, -1), axis=0, keepdims=True))
        s = jnp.where(sel, -jnp.inf, s)
    return jnp.concatenate(vals, axis=0), jnp.concatenate(picks, axis=0)


def _pair_rows(a0, a1, combine):
    half = PEER_TOPK // 2
    parts = [combine(a0[0:1, :], a1)]
    parts += [combine(a0[i:i + 1, :], a1[0:half, :]) for i in range(1, half)]
    parts.append(combine(a0[half:PEER_TOPK, :], a1[0:1, :]))
    return jnp.concatenate(parts, axis=0)


def _peer_route_kernel(h_ref, g_ref, wq_ref, sk_ref, xn_ref, eid_ref, gate_ref, xb_ref):
    hd = pl.program_id(1)

    @pl.when(hd == 0)
    def _():
        xn = _rmsnorm_rows(h_ref[...], g_ref[...])
        xn_ref[...] = xn
        xb_ref[...] = xn.astype(jnp.bfloat16)

    q = jnp.dot(xb_ref[...], wq_ref[hd], preferred_element_type=jnp.float32)
    sv, si = [], []
    for c in range(2):
        qc = q[:, c * PEER_DHALF:(c + 1) * PEER_DHALF].astype(jnp.bfloat16)
        st = lax.dot_general(sk_ref[hd, c], qc, (((1,), (1,)), ((), ())), preferred_element_type=jnp.float32)
        v, i = _topk_rows(st, None)
        sv.append(v)
        si.append(i)
    cand = _pair_rows(sv[0], sv[1], lambda a, b: a + b)
    cid = _pair_rows(si[0], si[1], lambda a, b: a * N_KEYS + b)
    fv, eid = _topk_rows(cand, cid)
    e = jnp.exp(fv - fv[0:1, :])
    gate_ref[...] = e / jnp.sum(e, axis=0, keepdims=True)
    eid_ref[...] = eid


def _peer_route(h, g, wq_bf16, sk_bf16):
    T = h.shape[0]
    rows = ROUTE_ROWS
    n_sel = PEER_HEADS * PEER_TOPK
    return pl.pallas_call(
        _peer_route_kernel,
        grid=(T // rows, PEER_HEADS),
        in_specs=[
            pl.BlockSpec((rows, D_MODEL), lambda tb, hd: (tb, 0)),
            pl.BlockSpec((1, D_MODEL), lambda tb, hd: (0, 0)),
            pl.BlockSpec((PEER_HEADS, D_MODEL, 2 * PEER_DHALF), lambda tb, hd: (0, 0, 0)),
            pl.BlockSpec((PEER_HEADS, 2, N_KEYS, PEER_DHALF), lambda tb, hd: (0, 0, 0, 0)),
        ],
        out_specs=[
            pl.BlockSpec((rows, D_MODEL), lambda tb, hd: (tb, 0)),
            pl.BlockSpec((PEER_TOPK, rows), lambda tb, hd: (hd, tb)),
            pl.BlockSpec((PEER_TOPK, rows), lambda tb, hd: (hd, tb)),
        ],
        out_shape=[
            jax.ShapeDtypeStruct((T, D_MODEL), jnp.float32),
            jax.ShapeDtypeStruct((n_sel, T), jnp.int32),
            jax.ShapeDtypeStruct((n_sel, T), jnp.float32),
        ],
        scratch_shapes=[pltpu.VMEM((rows, D_MODEL), jnp.bfloat16)],
        compiler_params=pltpu.CompilerParams(
            dimension_semantics=("arbitrary", "arbitrary"), vmem_limit_bytes=VMEM_LIMIT_BYTES),
        name="peer_route",
    )(h, g.reshape(1, D_MODEL), wq_bf16, sk_bf16)


N_SEL = PEER_HEADS * PEER_TOPK
SC_LANES = 16
GATHER_ROWS = PEER_TOPK
GATHERS_PER_TOKEN = N_SEL // GATHER_ROWS
GATHER_BUFS = 4
SC_TOKENS = 8
ACC_STRIDE = SC_LANES + 1


def _sc_gather_stream(wid, n_batches, tab_hbm, stage_srcs, stage_bufs, out_hbm, o_v, bufs, gsem, ssem, osem,
                      compute):
    idx_v = stage_bufs[0]

    def stage_copies(bi, slot):
        base = (wid * n_batches + bi) * SC_TOKENS
        return [pltpu.make_async_copy(src.at[pl.ds(base, SC_TOKENS)], buf.at[slot], ssem.at[slot])
                for src, buf in zip(stage_srcs, stage_bufs)]

    def out_copy(bi, slot):
        base = (wid * n_batches + bi) * SC_TOKENS
        return pltpu.make_async_copy(o_v.at[slot], out_hbm.at[pl.ds(base, SC_TOKENS)], osem.at[slot])

    def start(slot, t, kk, b):
        idx = idx_v[slot, t, pl.ds(kk * GATHER_ROWS, GATHER_ROWS)]
        pltpu.async_copy(tab_hbm.at[idx], bufs.at[b], gsem.at[b])

    def wait(b):
        pltpu.make_async_copy(tab_hbm.at[pl.ds(0, GATHER_ROWS)], bufs.at[b], gsem.at[b]).wait()

    for c in stage_copies(0, 0):
        c.start()
    for c in stage_copies(0, 0):
        c.wait()
    for q in range(GATHER_BUFS - 1):
        start(0, q // GATHERS_PER_TOKEN, q % GATHERS_PER_TOKEN, q % GATHER_BUFS)

    def batch(bi, carry):
        slot = bi % 2
        has_next = bi + 1 < n_batches

        @pl.when(has_next)
        def _():
            for c in stage_copies(bi + 1, 1 - slot):
                c.start()

        @pl.when(bi >= 2)
        def _():
            out_copy(bi, slot).wait()

        def tok(t, carry):
            @pl.when(jnp.logical_and(t == SC_TOKENS - 1, has_next))
            def _():
                for c in stage_copies(bi + 1, 1 - slot):
                    c.wait()

            for kk in range(GATHERS_PER_TOKEN):
                nq = kk + GATHER_BUFS - 1
                nk, nb = nq % GATHERS_PER_TOKEN, nq % GATHER_BUFS
                if nq < GATHERS_PER_TOKEN:
                    start(slot, t, nk, nb)
                else:
                    @pl.when(t + 1 < SC_TOKENS)
                    def _():
                        start(slot, t + 1, nk, nb)

                    @pl.when(jnp.logical_and(t + 1 == SC_TOKENS, has_next))
                    def _():
                        start(1 - slot, 0, nk, nb)

                wait(kk % GATHER_BUFS)
                compute(slot, t, kk, kk % GATHER_BUFS)
            return carry

        lax.fori_loop(0, SC_TOKENS, tok, 0)
        out_copy(bi, slot).start()
        return carry

    lax.fori_loop(0, n_batches, batch, 0)
    if n_batches >= 2:
        out_copy(n_batches - 2, (n_batches - 2) % 2).wait()
    out_copy(n_batches - 1, (n_batches - 1) % 2).wait()


def _sc_mesh_and_batches(n_tokens):
    info = plsc.get_sparse_core_info()
    assert info.num_lanes == SC_LANES
    n_workers = info.num_cores * info.num_subcores
    assert n_tokens % (n_workers * SC_TOKENS) == 0
    mesh = plsc.VectorSubcoreMesh(core_axis_name="c", subcore_axis_name="s")
    return info, mesh, n_tokens // (n_workers * SC_TOKENS)


def _peer_hidden_sc(eid, xn, u_tabs, layer):
    T = eid.shape[0]
    info, mesh, n_batches = _sc_mesh_and_batches(T)

    @functools.partial(
        pl.kernel, mesh=mesh,
        out_type=jax.ShapeDtypeStruct((T, N_SEL), jnp.float32),
        scratch_types=[
            pltpu.VMEM((2, SC_TOKENS, N_SEL), jnp.int32),
            pltpu.VMEM((2, SC_TOKENS, D_MODEL), jnp.float32),
            pltpu.VMEM((2, SC_TOKENS, N_SEL), jnp.float32),
            pltpu.VMEM((GATHER_BUFS, GATHER_ROWS, D_MODEL), jnp.float32),
            pltpu.VMEM((GATHER_ROWS * ACC_STRIDE,), jnp.float32),
            pltpu.SemaphoreType.DMA((GATHER_BUFS,)),
            pltpu.SemaphoreType.DMA((2,)),
            pltpu.SemaphoreType.DMA((2,)),
        ],
        compiler_params=pltpu.CompilerParams(needs_layout_passes=False),
        name="peer_hidden_sc",
    )
    def k(eid_hbm, xn_hbm, u_hbm, out_hbm, idx_v, x_v, o_v, bufs, acc_v, gsem, ssem, osem):
        wid = lax.axis_index("s") * info.num_cores + lax.axis_index("c")
        lane = lax.iota(jnp.int32, SC_LANES)
        zero = jnp.zeros((SC_LANES,), jnp.float32)

        def compute(slot, t, kk, b):
            @plsc.parallel_loop(0, D_MODEL // SC_LANES, carry=(zero,) * GATHER_ROWS)
            def accs(c, accs):
                xc = x_v[slot, t, pl.ds(c * SC_LANES, SC_LANES)]
                return tuple(accs[r] + bufs[b, r, pl.ds(c * SC_LANES, SC_LANES)] * xc for r in range(GATHER_ROWS))

            for r in range(GATHER_ROWS):
                acc_v[pl.ds(r * ACC_STRIDE, SC_LANES)] = accs[r]
            tot = zero
            for c in range(SC_LANES):
                tot = tot + plsc.load_gather(acc_v, [lane * ACC_STRIDE + c])
            o_v[slot, t, pl.ds(kk * GATHER_ROWS, GATHER_ROWS)] = tot

        _sc_gather_stream(wid, n_batches, u_hbm.at[layer], [eid_hbm, xn_hbm], [idx_v, x_v], out_hbm, o_v, bufs,
                          gsem, ssem, osem, compute)

    return k(eid, xn, u_tabs)


def _peer_combine_sc(eid, a, v_tabs, layer):
    T = eid.shape[0]
    info, mesh, n_batches = _sc_mesh_and_batches(T)

    @functools.partial(
        pl.kernel, mesh=mesh,
        out_type=jax.ShapeDtypeStruct((T, D_MODEL), jnp.float32),
        scratch_types=[
            pltpu.VMEM((2, SC_TOKENS, N_SEL), jnp.int32),
            pltpu.VMEM((2, SC_TOKENS, N_SEL), jnp.float32),
            pltpu.VMEM((2, SC_TOKENS, D_MODEL), jnp.float32),
            pltpu.VMEM((GATHER_BUFS, GATHER_ROWS, D_MODEL), jnp.float32),
            pltpu.SemaphoreType.DMA((GATHER_BUFS,)),
            pltpu.SemaphoreType.DMA((2,)),
            pltpu.SemaphoreType.DMA((2,)),
        ],
        compiler_params=pltpu.CompilerParams(needs_layout_passes=False),
        name="peer_combine_sc",
    )
    def k(eid_hbm, a_hbm, v_hbm, out_hbm, idx_v, a_v, o_v, bufs, gsem, ssem, osem):
        wid = lax.axis_index("s") * info.num_cores + lax.axis_index("c")

        def compute(slot, t, kk, b):
            svec = jnp.full((SC_LANES,), slot, jnp.int32)
            tvec = jnp.full((SC_LANES,), t, jnp.int32)
            ws = [plsc.load_gather(a_v, [svec, tvec, jnp.full((SC_LANES,), kk * GATHER_ROWS + r, jnp.int32)])
                  for r in range(GATHER_ROWS)]

            @plsc.parallel_loop(0, D_MODEL // SC_LANES, unroll=2)
            def _(c):
                sl = pl.ds(c * SC_LANES, SC_LANES)
                terms = [ws[r] * bufs[b, r, sl] for r in range(GATHER_ROWS)]
                if kk != 0:
                    terms.append(o_v[slot, t, sl])
                while len(terms) > 1:
                    pairs = [terms[i] + terms[i + 1] for i in range(0, len(terms) - 1, 2)]
                    terms = pairs + ([terms[-1]] if len(terms) % 2 else [])
                o_v[slot, t, sl] = terms[0]

        _sc_gather_stream(wid, n_batches, v_hbm.at[layer], [eid_hbm, a_hbm], [idx_v, a_v], out_hbm, o_v, bufs,
                          gsem, ssem, osem, compute)

    return k(eid, a, v_tabs)


def _peer_act_kernel(hp_ref, gate_ref, a_ref):
    a_ref[...] = _gelu(hp_ref[...]) * gate_ref[...]


ELEMENTWISE_ROWS = 1024


def _peer_act(hpre, gate):
    T = hpre.shape[0]
    rows = ELEMENTWISE_ROWS
    assert T % rows == 0
    spec = pl.BlockSpec((rows, N_SEL), lambda i: (i, 0))
    return pl.pallas_call(
        _peer_act_kernel, grid=(T // rows,), in_specs=[spec, spec], out_specs=spec,
        out_shape=jax.ShapeDtypeStruct((T, N_SEL), jnp.float32),
        compiler_params=pltpu.CompilerParams(dimension_semantics=("arbitrary",)),
        name="peer_act",
    )(hpre, gate)


def _residual_kernel(h_ref, c_ref, g_ref, o_ref, *, final_norm):
    y = h_ref[...] + c_ref[...]
    o_ref[...] = _rmsnorm_rows(y, g_ref[...]) if final_norm else y


def _residual(h, c, gfin, *, final_norm):
    T = h.shape[0]
    rows = ELEMENTWISE_ROWS
    assert T % rows == 0
    spec = pl.BlockSpec((rows, D_MODEL), lambda i: (i, 0))
    return pl.pallas_call(
        functools.partial(_residual_kernel, final_norm=final_norm), grid=(T // rows,),
        in_specs=[spec, spec, pl.BlockSpec((1, D_MODEL), lambda i: (0, 0))], out_specs=spec,
        out_shape=jax.ShapeDtypeStruct((T, D_MODEL), jnp.float32),
        compiler_params=pltpu.CompilerParams(dimension_semantics=("arbitrary",)),
        name="peer_residual",
    )(h, c, gfin.reshape(1, D_MODEL))


def _peer_experts(eid, xn, gate, h, gfin, u_tabs, v_tabs, layer, *, final_norm):
    hpre = _peer_hidden_sc(eid, xn, u_tabs, layer)
    a = _peer_act(hpre, gate)
    c = _peer_combine_sc(eid, a, v_tabs, layer)
    return _residual(h, c, gfin, final_norm=final_norm)


KV_WIDTH = N_KV_HEADS * HEAD_DIM
BF16 = jnp.bfloat16


def _qkv(h, gkv, gq, wkv_ref, wq_ref):
    kv = jnp.dot(_rmsnorm_rows(h, gkv).astype(BF16), wkv_ref[...], preferred_element_type=jnp.float32)
    q = jnp.dot(_rmsnorm_rows(h, gq).astype(BF16), wq_ref[...], preferred_element_type=jnp.float32)
    return kv, q


def _sink_softmax_pv(parts, sink):
    m = sink
    for s, _ in parts:
        m = jnp.maximum(m, jnp.max(s, axis=-1, keepdims=True))
    den = jnp.exp(sink - m)
    acc = None
    for s, v in parts:
        e = jnp.exp(s - m)
        den = den + jnp.sum(e, axis=-1, keepdims=True)
        pv = jnp.dot(e.astype(BF16), v.astype(BF16), preferred_element_type=jnp.float32)
        acc = pv if acc is None else acc + pv
    return acc / den


def _nt_dot(a, b):
    return lax.dot_general(a.astype(BF16), b.astype(BF16), (((1,), (1,)), ((), ())),
                           preferred_element_type=jnp.float32)


def _attn_prompt_kernel(h_ref, gkv_ref, gq_ref, wkv_ref, wq_ref, wo_ref, sink_ref,
                        o_ref, kvw_ref, prev_ref):
    blk = pl.program_id(1)
    h = h_ref[...]
    kv, q = _qkv(h, gkv_ref[...], gq_ref[...], wkv_ref, wq_ref)
    kvw_ref[0] = kv

    @pl.when(blk == 0)
    def _():
        prev_ref[...] = jnp.zeros_like(prev_ref)

    prev = prev_ref[...]
    qi = lax.broadcasted_iota(jnp.int32, (WINDOW, WINDOW), 0)
    kj = lax.broadcasted_iota(jnp.int32, (WINDOW, WINDOW), 1)
    prev_ok = jnp.logical_and(kj > qi, blk > 0)
    cur_ok = kj <= qi
    heads = []
    for kvh in range(N_KV_HEADS):
        ks = slice(kvh * HEAD_DIM, (kvh + 1) * HEAD_DIM)
        vs = slice(KV_WIDTH + kvh * HEAD_DIM, KV_WIDTH + (kvh + 1) * HEAD_DIM)
        for g in range(Q_PER_KV):
            hq = kvh * Q_PER_KV + g
            qh = q[:, hq * HEAD_DIM:(hq + 1) * HEAD_DIM]
            sp = jnp.where(prev_ok, _nt_dot(qh, prev[:, ks]) * ATTN_SCALE, -jnp.inf)
            sc = jnp.where(cur_ok, _nt_dot(qh, kv[:, ks]) * ATTN_SCALE, -jnp.inf)
            heads.append(_sink_softmax_pv([(sp, prev[:, vs]), (sc, kv[:, vs])], sink_ref[hq]))
    o = jnp.concatenate(heads, axis=1)
    o_ref[...] = h + jnp.dot(o.astype(BF16), wo_ref[...], preferred_element_type=jnp.float32)
    prev_ref[...] = kv


def _attn_prompt(h, gkv, gq, wkv, wq, wo, sinks, *, n_seq, seq_len):
    nb = seq_len // WINDOW
    row_spec = pl.BlockSpec((WINDOW, D_MODEL), lambda n, b: (n * nb + b, 0))
    full = lambda shape: pl.BlockSpec(shape, lambda n, b: (0,) * len(shape))
    return pl.pallas_call(
        _attn_prompt_kernel,
        grid=(n_seq, nb),
        in_specs=[
            row_spec, full((1, D_MODEL)), full((1, D_MODEL)),
            full((D_MODEL, 2 * KV_WIDTH)), full((D_MODEL, D_MODEL)), full((D_MODEL, D_MODEL)),
            pl.BlockSpec(memory_space=pltpu.SMEM),
        ],
        out_specs=[row_spec, pl.BlockSpec((1, WINDOW, 2 * KV_WIDTH), lambda n, b: (n, 0, 0))],
        out_shape=[
            jax.ShapeDtypeStruct(h.shape, jnp.float32),
            jax.ShapeDtypeStruct((n_seq, WINDOW, 2 * KV_WIDTH), jnp.float32),
        ],
        scratch_shapes=[pltpu.VMEM((WINDOW, 2 * KV_WIDTH), jnp.float32)],
        compiler_params=pltpu.CompilerParams(
            dimension_semantics=("arbitrary", "arbitrary"), vmem_limit_bytes=VMEM_LIMIT_BYTES),
        name="attn_prompt",
    )(h, gkv.reshape(1, D_MODEL), gq.reshape(1, D_MODEL), wkv, wq, wo, sinks)


ATTN_SEQS = 16


def _attn_sample_kernel(h_ref, ck_ref, cv_ref, gkv_ref, gq_ref, wkv_ref, wq_ref, wo_ref, sink_ref,
                        o_ref, kw_ref, vw_ref, kv_ref, q_ref, att_ref, *, n_new):
    h = h_ref[...]
    kv, q = _qkv(h, gkv_ref[...], gq_ref[...], wkv_ref, wq_ref)
    kv_ref[...] = kv
    q_ref[...] = q
    n_seq = h.shape[0] // n_new
    rows = Q_PER_KV * n_new
    qpos_c = lax.broadcasted_iota(jnp.int32, (rows, WINDOW), 0) % n_new
    cache_ok = lax.broadcasted_iota(jnp.int32, (rows, WINDOW), 1) > qpos_c
    qpos_n = lax.broadcasted_iota(jnp.int32, (rows, n_new), 0) % n_new
    new_ok = lax.broadcasted_iota(jnp.int32, (rows, n_new), 1) <= qpos_n

    def per_seq(n, carry):
        r0 = pl.multiple_of(n * n_new, n_new)
        kvn = kv_ref[pl.ds(r0, n_new), :]
        qn = q_ref[pl.ds(r0, n_new), :]
        ck = ck_ref[n]
        cv = cv_ref[n]
        kw_ref[n, 0:WINDOW - n_new, :] = ck[n_new:, :]
        kw_ref[n, WINDOW - n_new:WINDOW, :] = kvn[:, 0:KV_WIDTH]
        vw_ref[n, 0:WINDOW - n_new, :] = cv[n_new:, :]
        vw_ref[n, WINDOW - n_new:WINDOW, :] = kvn[:, KV_WIDTH:2 * KV_WIDTH]
        outs = []
        for kvh in range(N_KV_HEADS):
            ks = slice(kvh * HEAD_DIM, (kvh + 1) * HEAD_DIM)
            vs = slice(KV_WIDTH + kvh * HEAD_DIM, KV_WIDTH + (kvh + 1) * HEAD_DIM)
            qs = jnp.concatenate(
                [qn[:, (kvh * Q_PER_KV + g) * HEAD_DIM:(kvh * Q_PER_KV + g + 1) * HEAD_DIM] for g in range(Q_PER_KV)],
                axis=0)
            s_c = jnp.where(cache_ok, _nt_dot(qs, ck[:, ks]) * ATTN_SCALE, -jnp.inf)
            s_n = jnp.where(new_ok, _nt_dot(qs, kvn[:, ks]) * ATTN_SCALE, -jnp.inf)
            sink = sink_ref[kvh][:, 0:1]
            o = _sink_softmax_pv([(s_c, cv[:, ks]), (s_n, kvn[:, vs])], sink)
            outs += [o[g * n_new:(g + 1) * n_new, :] for g in range(Q_PER_KV)]
        att_ref[pl.ds(r0, n_new), :] = jnp.concatenate(outs, axis=1)
        return carry

    lax.fori_loop(0, n_seq, per_seq, 0)
    o_ref[...] = h + jnp.dot(att_ref[...].astype(BF16), wo_ref[...], preferred_element_type=jnp.float32)


def _attn_sample(h, cache_k, cache_v, gkv, gq, wkv, wq, wo, sinks, *, n_new):
    n_seq = cache_k.shape[0]
    sb = ATTN_SEQS
    rows = sb * n_new
    row_spec = pl.BlockSpec((rows, D_MODEL), lambda i: (i, 0))
    win_spec = pl.BlockSpec((sb, WINDOW, KV_WIDTH), lambda i: (i, 0, 0))
    full = lambda shape: pl.BlockSpec(shape, lambda i: (0,) * len(shape))
    sink_rows = jnp.repeat(sinks.reshape(N_KV_HEADS, Q_PER_KV), n_new, axis=1)[:, :, None]
    sink_rows = jnp.broadcast_to(sink_rows, (N_KV_HEADS, Q_PER_KV * n_new, LANES))
    return pl.pallas_call(
        functools.partial(_attn_sample_kernel, n_new=n_new),
        grid=(n_seq // sb,),
        in_specs=[
            row_spec, win_spec, win_spec, full((1, D_MODEL)), full((1, D_MODEL)),
            full((D_MODEL, 2 * KV_WIDTH)), full((D_MODEL, D_MODEL)), full((D_MODEL, D_MODEL)),
            full((N_KV_HEADS, Q_PER_KV * n_new, LANES)),
        ],
        out_specs=[row_spec, win_spec, win_spec],
        out_shape=[
            jax.ShapeDtypeStruct(h.shape, jnp.float32),
            jax.ShapeDtypeStruct((n_seq, WINDOW, KV_WIDTH), jnp.float32),
            jax.ShapeDtypeStruct((n_seq, WINDOW, KV_WIDTH), jnp.float32),
        ],
        scratch_shapes=[
            pltpu.VMEM((rows, 2 * KV_WIDTH), jnp.float32),
            pltpu.VMEM((rows, D_MODEL), jnp.float32),
            pltpu.VMEM((rows, D_MODEL), jnp.float32),
        ],
        compiler_params=pltpu.CompilerParams(
            dimension_semantics=("arbitrary",), vmem_limit_bytes=VMEM_LIMIT_BYTES),
        name="attn_sample",
    )(h, cache_k, cache_v, gkv.reshape(1, D_MODEL), gq.reshape(1, D_MODEL), wkv, wq, wo, sink_rows)


def kernel(x_prompt, x_sample, state_ssm_re, state_ssm_im, cache_k_win, cache_v_win, norm_mix, norm_ffn, norm_kv, norm_final, ssm_lam_re, ssm_lam_im, ssm_log_dt, ssm_b_re, ssm_b_im, ssm_c_re, ssm_c_im, ssm_d, ssm_w_glu, w_kv, w_q, attn_sinks, w_o, peer_w_q, peer_sub_keys, peer_u, peer_v):
    bmat, cmat, apr, api = _s5_discretize(ssm_lam_re[0], ssm_lam_im[0], ssm_log_dt[0], ssm_b_re[0], ssm_b_im[0], ssm_c_re[0], ssm_c_im[0])
    cmat = cmat.astype(jnp.bfloat16)
    wglu = ssm_w_glu[0].astype(jnp.bfloat16)
    wkv = w_kv.astype(BF16)
    wq = w_q[0].astype(BF16)
    wo = w_o[0].astype(BF16)
    peer_wq = [peer_w_q[layer].astype(BF16).reshape(D_MODEL, PEER_HEADS, 2 * PEER_DHALF).transpose(1, 0, 2)
               for layer in range(2)]
    peer_sk = [peer_sub_keys[layer].astype(BF16) for layer in range(2)]

    def peer(h, layer, final_norm):
        xn, eid_t, gate_t = _peer_route(h, norm_ffn[layer], peer_wq[layer], peer_sk[layer])
        return _peer_experts(eid_t.T, xn, gate_t.T, h, norm_final, peer_u, peer_v, layer, final_norm=final_norm)

    win = lambda a: a.reshape(a.shape[0], WINDOW, N_KV_HEADS, HEAD_DIM)

    seq_len = x_prompt.shape[1]
    xp_all = x_prompt.reshape(-1, D_MODEL)

    def trunk_prompt(seq0, n_p):
        row0 = seq0 * seq_len
        z0 = jnp.zeros((n_p, N_CHUNKS, 1, CHUNK_STATE), jnp.float32)
        z, sr, si = _s5_mixer(xp_all, z0, z0, norm_mix[0], ssm_d[0], bmat, cmat, apr, api,
                              chain=True, seq_len=seq_len, row0=row0)
        h2 = peer(_glu_residual(z, xp_all, wglu, res_row0=row0), 0, False)
        h3, kvw = _attn_prompt(h2, norm_kv, norm_mix[1], wkv, wq, wo, attn_sinks[0], n_seq=n_p, seq_len=seq_len)
        y = peer(h3, 1, True)
        return (y.reshape(n_p, seq_len, D_MODEL), _chunks_to_state(sr), _chunks_to_state(si),
                win(kvw[:, :, :KV_WIDTH]), win(kvw[:, :, KV_WIDTH:]))

    def trunk_sample(x_s, s_re, s_im, c_k, c_v):
        n_s = x_s.shape[0]
        xs = x_s.reshape(-1, D_MODEL)
        z, sr, si = _s5_mixer(xs, _state_to_chunks(s_re), _state_to_chunks(s_im), norm_mix[0], ssm_d[0],
                              bmat, cmat, apr, api, chain=False, seq_len=x_s.shape[1])
        h2 = peer(_glu_residual(z, xs, wglu), 0, False)
        h3, kw, vw = _attn_sample(h2, c_k.reshape(n_s, WINDOW, KV_WIDTH), c_v.reshape(n_s, WINDOW, KV_WIDTH),
                                  norm_kv, norm_mix[1], wkv, wq, wo, attn_sinks[0], n_new=x_s.shape[1])
        y = peer(h3, 1, True)
        return y.reshape(x_s.shape), _chunks_to_state(sr), _chunks_to_state(si), win(kw), win(vw)

    y_s, sre_s, sim_s, kw_s, vw_s = trunk_sample(x_sample, state_ssm_re[0], state_ssm_im[0], cache_k_win, cache_v_win)
    n_groups = 4
    gp = x_prompt.shape[0] // n_groups
    outs = [trunk_prompt(i * gp, gp) for i in range(n_groups)]
    y_p, sre_p, sim_p, kw_p, vw_p = [jnp.concatenate(parts, axis=0) for parts in zip(*outs)]
    return (y_p, y_s, sre_p[None], sim_p[None], kw_p, vw_p, sre_s[None], sim_s[None], kw_s, vw_s)
```

```python
import functools
import math

import jax
import jax.numpy as jnp
from jax import lax
from jax.experimental import pallas as pl
from jax.experimental.pallas import tpu as pltpu
from jax.experimental.pallas import tpu_sc as plsc

D_MODEL = 1024
GROUP_SIZE = 16
N_GROUPS = D_MODEL // GROUP_SIZE
STATE_DIM = 64
HEAD_DIM = 64
N_Q_HEADS = D_MODEL // HEAD_DIM
N_KV_HEADS = N_Q_HEADS // 8
Q_PER_KV = N_Q_HEADS // N_KV_HEADS
WINDOW = 128
ATTN_SCALE = 1.0 / math.sqrt(HEAD_DIM)
PEER_HEADS = 8
N_KEYS = 128
PEER_TOPK = 16
PEER_DHALF = 128
EPS = 1e-5

LANES = 128
SUBLANES = 8
VMEM_LIMIT_BYTES = 56 * 1024 * 1024

GROUPS_PER_CHUNK = LANES // GROUP_SIZE
N_CHUNKS = N_GROUPS // GROUPS_PER_CHUNK
CHUNK_STATE = GROUPS_PER_CHUNK * STATE_DIM
S5_ROWS = 256


def _rmsnorm_rows(x, g):
    r = lax.rsqrt(jnp.mean(x * x, axis=-1, keepdims=True) + EPS)
    return x * r * g


def _gelu(x):
    return 0.5 * x * (1.0 + lax.erf(x * (1.0 / math.sqrt(2.0))))


def _s5_discretize(lam_re, lam_im, log_dt, b_re, b_im, c_re, c_im):
    f32 = jnp.float32
    lr = lam_re.astype(f32)
    li = lam_im.astype(f32)
    dt = jnp.exp(log_dt.astype(f32))[:, None]
    mag = jnp.exp(lr * dt)
    ab_re = mag * jnp.cos(li * dt)
    ab_im = mag * jnp.sin(li * dt)
    den = lr * lr + li * li
    f_re = ((ab_re - 1.0) * lr + ab_im * li) / den
    f_im = (ab_im * lr - (ab_re - 1.0) * li) / den
    br = b_re.astype(f32)
    bi = b_im.astype(f32)
    bb_re = f_re[..., None] * br - f_im[..., None] * bi
    bb_im = f_re[..., None] * bi + f_im[..., None] * br
    eye = jnp.eye(GROUPS_PER_CHUNK, dtype=f32)

    def chunk_rows(v):
        return v.reshape(N_CHUNKS, 1, CHUNK_STATE)

    def in_blocks(bb):
        t = bb.reshape(N_CHUNKS, GROUPS_PER_CHUNK, STATE_DIM, GROUP_SIZE).transpose(0, 1, 3, 2)
        return jnp.einsum('mgjp,gh->mgjhp', t, eye).reshape(N_CHUNKS, LANES, CHUNK_STATE)

    def out_blocks(c):
        t = c.astype(f32).reshape(N_CHUNKS, GROUPS_PER_CHUNK, GROUP_SIZE, STATE_DIM).transpose(0, 1, 3, 2)
        return jnp.einsum('mgpj,gh->mgphj', t, eye).reshape(N_CHUNKS, CHUNK_STATE, LANES)

    bfull = jnp.concatenate([in_blocks(bb_re), in_blocks(bb_im)], axis=2)
    b_hi = bfull.astype(jnp.bfloat16)
    b_lo = (bfull - b_hi.astype(f32)).astype(jnp.bfloat16)
    bmat = (jnp.concatenate([b_hi, b_hi], axis=1), b_lo)
    cmat = jnp.concatenate([out_blocks(c_re), -out_blocks(c_im)], axis=1)
    pr, pi = [ab_re], [ab_im]
    for _ in range(SUBLANES - 1):
        pr, pi = pr + [pr[-1] * ab_re - pi[-1] * ab_im], pi + [pr[-1] * ab_im + pi[-1] * ab_re]
    apr = jnp.concatenate([chunk_rows(v) for v in pr], axis=1)
    api = jnp.concatenate([chunk_rows(v) for v in pi], axis=1)
    return bmat, cmat, apr, api


def _split_bf16(x):
    hi = x.astype(jnp.bfloat16)
    return hi, (x - hi.astype(jnp.float32)).astype(jnp.bfloat16)


def _s5_kernel(x_ref, g_ref, d_ref, bhh_ref, blo_ref, c_ref, apr_ref, api_ref, h0r_ref, h0i_ref,
               z_ref, sr_ref, si_ref, u_ref, us_ref, st_ref, cr_ref, ci_ref, *, chain, blocks_per_seq):
    rb = pl.program_id(0)
    m = pl.program_id(1)
    rows = x_ref.shape[0]
    n_tiles = rows // SUBLANES

    @pl.when(m == 0)
    def _():
        u = _rmsnorm_rows(x_ref[...], g_ref[...])
        for mm in range(N_CHUNKS):
            uc = u[:, mm * LANES:(mm + 1) * LANES]
            u_ref[mm] = uc
            hi, lo = _split_bf16(uc)
            us_ref[mm, :, 0:LANES] = hi
            us_ref[mm, :, LANES:2 * LANES] = lo

    u = u_ref[m]
    us = us_ref[m]
    st_ref[...] = (jnp.dot(us, bhh_ref[m], preferred_element_type=jnp.float32)
                   + jnp.dot(us[:, 0:LANES], blo_ref[m], preferred_element_type=jnp.float32))

    apr = apr_ref[0]
    api = api_ref[0]
    row = lax.broadcasted_iota(jnp.int32, (SUBLANES, CHUNK_STATE), 0)

    if chain:
        @pl.when(rb % blocks_per_seq == 0)
        def _():
            cr_ref[m] = h0r_ref[0, 0]
            ci_ref[m] = h0i_ref[0, 0]

    def tile_step(k, carry):
        r0 = pl.multiple_of(k * SUBLANES, SUBLANES)
        xr = st_ref[pl.ds(r0, SUBLANES), 0:CHUNK_STATE]
        xi = st_ref[pl.ds(r0, SUBLANES), CHUNK_STATE:2 * CHUNK_STATE]
        for d in (1, 2, 4):
            ar = apr[d - 1:d, :]
            ai = api[d - 1:d, :]
            sr = jnp.where(row >= d, pltpu.roll(xr, d, axis=0), 0.0)
            si = jnp.where(row >= d, pltpu.roll(xi, d, axis=0), 0.0)
            xr, xi = xr + ar * sr - ai * si, xi + ar * si + ai * sr
        if chain:
            cr, ci = carry
        else:
            cr = h0r_ref[k, 0]
            ci = h0i_ref[k, 0]
        hr = xr + apr * cr - api * ci
        hi = xi + apr * ci + api * cr
        st_ref[pl.ds(r0, SUBLANES), 0:CHUNK_STATE] = hr
        st_ref[pl.ds(r0, SUBLANES), CHUNK_STATE:2 * CHUNK_STATE] = hi
        lr_ = hr[SUBLANES - 1:SUBLANES, :]
        li_ = hi[SUBLANES - 1:SUBLANES, :]
        if chain:
            return lr_, li_
        sr_ref[k, m] = lr_
        si_ref[k, m] = li_
        return carry

    if chain:
        cr, ci = lax.fori_loop(0, n_tiles, tile_step, (cr_ref[m], ci_ref[m]))
        cr_ref[m] = cr
        ci_ref[m] = ci
        sr_ref[0, m] = cr
        si_ref[0, m] = ci
    else:
        lax.fori_loop(0, n_tiles, tile_step, 0)

    y = jnp.dot(st_ref[...].astype(jnp.bfloat16), c_ref[m], preferred_element_type=jnp.float32)
    y = y + d_ref[0] * u
    z_ref[...] = _gelu(y)


def _s5_mixer(x, h0r, h0i, g, d_skip, bmat, cmat_bf16, apr, api, *, chain, seq_len, row0=0):
    nseq = h0r.shape[0]
    T = nseq * seq_len
    rows = S5_ROWS
    blk0 = row0 // rows
    if chain:
        blocks_per_seq = seq_len // rows
        seq_blk = 1
        seq_map = lambda rb, m: (rb // blocks_per_seq, m, 0, 0)
        out_map = lambda rb, m: (rb // blocks_per_seq, 0, 0, 0)
    else:
        assert seq_len == SUBLANES
        blocks_per_seq = 1
        seq_blk = rows // SUBLANES
        seq_map = lambda rb, m: (rb, m, 0, 0)
        out_map = lambda rb, m: (rb, 0, 0, 0)
    grid = (T // rows, N_CHUNKS)
    kern = functools.partial(_s5_kernel, chain=chain, blocks_per_seq=blocks_per_seq)
    st_spec = pl.BlockSpec((seq_blk, 1, 1, CHUNK_STATE), seq_map)
    out_st_spec = pl.BlockSpec((seq_blk, N_CHUNKS, 1, CHUNK_STATE), out_map)
    z, sr, si = pl.pallas_call(
        kern,
        grid=grid,
        in_specs=[
            pl.BlockSpec((rows, D_MODEL), lambda rb, m: (rb + blk0, 0)),
            pl.BlockSpec((1, D_MODEL), lambda rb, m: (0, 0)),
            pl.BlockSpec((1, 1, LANES), lambda rb, m: (m, 0, 0)),
            pl.BlockSpec((N_CHUNKS, 2 * LANES, 2 * CHUNK_STATE), lambda rb, m: (0, 0, 0)),
            pl.BlockSpec((N_CHUNKS, LANES, 2 * CHUNK_STATE), lambda rb, m: (0, 0, 0)),
            pl.BlockSpec((N_CHUNKS, 2 * CHUNK_STATE, LANES), lambda rb, m: (0, 0, 0)),
            pl.BlockSpec((1, SUBLANES, CHUNK_STATE), lambda rb, m: (m, 0, 0)),
            pl.BlockSpec((1, SUBLANES, CHUNK_STATE), lambda rb, m: (m, 0, 0)),
            st_spec, st_spec,
        ],
        out_specs=[
            pl.BlockSpec((rows, LANES), lambda rb, m: (rb, m)),
            out_st_spec, out_st_spec,
        ],
        out_shape=[
            jax.ShapeDtypeStruct((T, D_MODEL), jnp.float32),
            jax.ShapeDtypeStruct((nseq, N_CHUNKS, 1, CHUNK_STATE), jnp.float32),
            jax.ShapeDtypeStruct((nseq, N_CHUNKS, 1, CHUNK_STATE), jnp.float32),
        ],
        scratch_shapes=[
            pltpu.VMEM((N_CHUNKS, rows, LANES), jnp.float32),
            pltpu.VMEM((N_CHUNKS, rows, 2 * LANES), jnp.bfloat16),
            pltpu.VMEM((rows, 2 * CHUNK_STATE), jnp.float32),
            pltpu.VMEM((N_CHUNKS, 1, CHUNK_STATE), jnp.float32),
            pltpu.VMEM((N_CHUNKS, 1, CHUNK_STATE), jnp.float32),
        ],
        compiler_params=pltpu.CompilerParams(
            dimension_semantics=("arbitrary", "arbitrary"), vmem_limit_bytes=VMEM_LIMIT_BYTES),
        name="s5_mixer",
    )(x, g.reshape(1, D_MODEL), d_skip.reshape(N_CHUNKS, 1, LANES), bmat[0], bmat[1], cmat_bf16, apr, api,
      h0r, h0i)
    return z, sr, si


def _state_to_chunks(h):
    return h.reshape(h.shape[0], N_CHUNKS, 1, CHUNK_STATE)


def _chunks_to_state(s):
    return s.reshape(s.shape[0], N_GROUPS, STATE_DIM)


def _glu_kernel(z_ref, res_ref, w_ref, o_ref):
    zz = jnp.dot(z_ref[...].astype(jnp.bfloat16), w_ref[...], preferred_element_type=jnp.float32)
    a = zz[:, :D_MODEL]
    b = zz[:, D_MODEL:]
    o_ref[...] = res_ref[...] + a * (1.0 / (1.0 + jnp.exp(-b)))


def _glu_residual(z, res, w_bf16, res_row0=0):
    T = z.shape[0]
    rows = math.gcd(T, 512)
    blk0 = res_row0 // rows
    return pl.pallas_call(
        _glu_kernel,
        grid=(T // rows,),
        in_specs=[
            pl.BlockSpec((rows, D_MODEL), lambda i: (i, 0)),
            pl.BlockSpec((rows, D_MODEL), lambda i: (i + blk0, 0)),
            pl.BlockSpec((D_MODEL, 2 * D_MODEL), lambda i: (0, 0)),
        ],
        out_specs=pl.BlockSpec((rows, D_MODEL), lambda i: (i, 0)),
        out_shape=jax.ShapeDtypeStruct((T, D_MODEL), jnp.float32),
        compiler_params=pltpu.CompilerParams(
            dimension_semantics=("arbitrary",), vmem_limit_bytes=VMEM_LIMIT_BYTES),
        name="glu_residual",
    )(z, res, w_bf16)


ROUTE_ROWS = 256


def _topk_rows(s, payload):
    n_rows = s.shape[0]
    row = lax.broadcasted_iota(jnp.int32, s.shape, 0)
    vals, picks = [], []
    for _ in range(PEER_TOPK):
        m = jnp.max(s, axis=0, keepdims=True)
        pos = jnp.min(jnp.where(s == m, row, n_rows), axis=0, keepdims=True)
        sel = row == pos
        vals.append(m)
        if payload is None:
            picks.append(pos)
        else:
            picks.append(jnp.max(jnp.where(sel, payload, -1), axis=0, keepdims=True))
        s = jnp.where(sel, -jnp.inf, s)
    return jnp.concatenate(vals, axis=0), jnp.concatenate(picks, axis=0)


def _pair_rows(a0, a1, combine):
    half = PEER_TOPK // 2
    parts = [combine(a0[0:1, :], a1)]
    parts += [combine(a0[i:i + 1, :], a1[0:half, :]) for i in range(1, half)]
    parts.append(combine(a0[half:PEER_TOPK, :], a1[0:1, :]))
    return jnp.concatenate(parts, axis=0)


def _peer_route_kernel(h_ref, g_ref, wq_ref, sk_ref, xn_ref, eid_ref, gate_ref, xb_ref):
    hd = pl.program_id(1)

    @pl.when(hd == 0)
    def _():
        xn = _rmsnorm_rows(h_ref[...], g_ref[...])
        xn_ref[...] = xn
        xb_ref[...] = xn.astype(jnp.bfloat16)

    q = jnp.dot(xb_ref[...], wq_ref[hd], preferred_element_type=jnp.float32)
    sv, si = [], []
    for c in range(2):
        qc = q[:, c * PEER_DHALF:(c + 1) * PEER_DHALF].astype(jnp.bfloat16)
        st = lax.dot_general(sk_ref[hd, c], qc, (((1,), (1,)), ((), ())), preferred_element_type=jnp.float32)
        v, i = _topk_rows(st, None)
        sv.append(v)
        si.append(i)
    cand = _pair_rows(sv[0], sv[1], lambda a, b: a + b)
    cid = _pair_rows(si[0], si[1], lambda a, b: a * N_KEYS + b)
    fv, eid = _topk_rows(cand, cid)
    e = jnp.exp(fv - fv[0:1, :])
    gate_ref[...] = e / jnp.sum(e, axis=0, keepdims=True)
    eid_ref[...] = eid


def _peer_route(h, g, wq_bf16, sk_bf16):
    T = h.shape[0]
    rows = ROUTE_ROWS
    n_sel = PEER_HEADS * PEER_TOPK
    return pl.pallas_call(
        _peer_route_kernel,
        grid=(T // rows, PEER_HEADS),
        in_specs=[
            pl.BlockSpec((rows, D_MODEL), lambda tb, hd: (tb, 0)),
            pl.BlockSpec((1, D_MODEL), lambda tb, hd: (0, 0)),
            pl.BlockSpec((PEER_HEADS, D_MODEL, 2 * PEER_DHALF), lambda tb, hd: (0, 0, 0)),
            pl.BlockSpec((PEER_HEADS, 2, N_KEYS, PEER_DHALF), lambda tb, hd: (0, 0, 0, 0)),
        ],
        out_specs=[
            pl.BlockSpec((rows, D_MODEL), lambda tb, hd: (tb, 0)),
            pl.BlockSpec((PEER_TOPK, rows), lambda tb, hd: (hd, tb)),
            pl.BlockSpec((PEER_TOPK, rows), lambda tb, hd: (hd, tb)),
        ],
        out_shape=[
            jax.ShapeDtypeStruct((T, D_MODEL), jnp.float32),
            jax.ShapeDtypeStruct((n_sel, T), jnp.int32),
            jax.ShapeDtypeStruct((n_sel, T), jnp.float32),
        ],
        scratch_shapes=[pltpu.VMEM((rows, D_MODEL), jnp.bfloat16)],
        compiler_params=pltpu.CompilerParams(
            dimension_semantics=("arbitrary", "arbitrary"), vmem_limit_bytes=VMEM_LIMIT_BYTES),
        name="peer_route",
    )(h, g.reshape(1, D_MODEL), wq_bf16, sk_bf16)


N_SEL = PEER_HEADS * PEER_TOPK
SC_LANES = 16
GATHER_ROWS = PEER_TOPK
GATHERS_PER_TOKEN = N_SEL // GATHER_ROWS
GATHER_BUFS = 4
SC_TOKENS = 8
ACC_STRIDE = SC_LANES + 1


def _sc_gather_stream(wid, n_batches, tab_hbm, stage_srcs, stage_bufs, out_hbm, o_v, bufs, gsem, ssem, osem,
                      compute):
    idx_v = stage_bufs[0]

    def stage_copies(bi, slot):
        base = (wid * n_batches + bi) * SC_TOKENS
        return [pltpu.make_async_copy(src.at[pl.ds(base, SC_TOKENS)], buf.at[slot], ssem.at[slot])
                for src, buf in zip(stage_srcs, stage_bufs)]

    def out_copy(bi, slot):
        base = (wid * n_batches + bi) * SC_TOKENS
        return pltpu.make_async_copy(o_v.at[slot], out_hbm.at[pl.ds(base, SC_TOKENS)], osem.at[slot])

    def start(slot, t, kk, b):
        idx = idx_v[slot, t, pl.ds(kk * GATHER_ROWS, GATHER_ROWS)]
        pltpu.async_copy(tab_hbm.at[idx], bufs.at[b], gsem.at[b])

    def wait(b):
        pltpu.make_async_copy(tab_hbm.at[pl.ds(0, GATHER_ROWS)], bufs.at[b], gsem.at[b]).wait()

    for c in stage_copies(0, 0):
        c.start()
    for c in stage_copies(0, 0):
        c.wait()
    for q in range(GATHER_BUFS - 1):
        start(0, q // GATHERS_PER_TOKEN, q % GATHERS_PER_TOKEN, q % GATHER_BUFS)

    def batch(bi, carry):
        slot = bi % 2
        has_next = bi + 1 < n_batches

        @pl.when(has_next)
        def _():
            for c in stage_copies(bi + 1, 1 - slot):
                c.start()

        @pl.when(bi >= 2)
        def _():
            out_copy(bi, slot).wait()

        def tok(t, carry):
            @pl.when(jnp.logical_and(t == SC_TOKENS - 1, has_next))
            def _():
                for c in stage_copies(bi + 1, 1 - slot):
                    c.wait()

            for kk in range(GATHERS_PER_TOKEN):
                nq = kk + GATHER_BUFS - 1
                nk, nb = nq % GATHERS_PER_TOKEN, nq % GATHER_BUFS
                if nq < GATHERS_PER_TOKEN:
                    start(slot, t, nk, nb)
                else:
                    @pl.when(t + 1 < SC_TOKENS)
                    def _():
                        start(slot, t + 1, nk, nb)

                    @pl.when(jnp.logical_and(t + 1 == SC_TOKENS, has_next))
                    def _():
                        start(1 - slot, 0, nk, nb)

                wait(kk % GATHER_BUFS)
                compute(slot, t, kk, kk % GATHER_BUFS)
            return carry

        lax.fori_loop(0, SC_TOKENS, tok, 0)
        out_copy(bi, slot).start()
        return carry

    lax.fori_loop(0, n_batches, batch, 0)
    if n_batches >= 2:
        out_copy(n_batches - 2, (n_batches - 2) % 2).wait()
    out_copy(n_batches - 1, (n_batches - 1) % 2).wait()


def _sc_mesh_and_batches(n_tokens):
    info = plsc.get_sparse_core_info()
    assert info.num_lanes == SC_LANES
    n_workers = info.num_cores * info.num_subcores
    assert n_tokens % (n_workers * SC_TOKENS) == 0
    mesh = plsc.VectorSubcoreMesh(core_axis_name="c", subcore_axis_name="s")
    return info, mesh, n_tokens // (n_workers * SC_TOKENS)


def _peer_hidden_sc(eid, xn, u_tabs, layer):
    T = eid.shape[0]
    info, mesh, n_batches = _sc_mesh_and_batches(T)

    @functools.partial(
        pl.kernel, mesh=mesh,
        out_type=jax.ShapeDtypeStruct((T, N_SEL), jnp.float32),
        scratch_types=[
            pltpu.VMEM((2, SC_TOKENS, N_SEL), jnp.int32),
            pltpu.VMEM((2, SC_TOKENS, D_MODEL), jnp.float32),
            pltpu.VMEM((2, SC_TOKENS, N_SEL), jnp.float32),
            pltpu.VMEM((GATHER_BUFS, GATHER_ROWS, D_MODEL), jnp.float32),
            pltpu.VMEM((GATHER_ROWS * ACC_STRIDE,), jnp.float32),
            pltpu.SemaphoreType.DMA((GATHER_BUFS,)),
            pltpu.SemaphoreType.DMA((2,)),
            pltpu.SemaphoreType.DMA((2,)),
        ],
        compiler_params=pltpu.CompilerParams(needs_layout_passes=False),
        name="peer_hidden_sc",
    )
    def k(eid_hbm, xn_hbm, u_hbm, out_hbm, idx_v, x_v, o_v, bufs, acc_v, gsem, ssem, osem):
        wid = lax.axis_index("s") * info.num_cores + lax.axis_index("c")
        lane = lax.iota(jnp.int32, SC_LANES)
        zero = jnp.zeros((SC_LANES,), jnp.float32)

        def compute(slot, t, kk, b):
            @plsc.parallel_loop(0, D_MODEL // SC_LANES, carry=(zero,) * GATHER_ROWS)
            def accs(c, accs):
                xc = x_v[slot, t, pl.ds(c * SC_LANES, SC_LANES)]
                return tuple(accs[r] + bufs[b, r, pl.ds(c * SC_LANES, SC_LANES)] * xc for r in range(GATHER_ROWS))

            for r in range(GATHER_ROWS):
                acc_v[pl.ds(r * ACC_STRIDE, SC_LANES)] = accs[r]
            tot = zero
            for c in range(SC_LANES):
                tot = tot + plsc.load_gather(acc_v, [lane * ACC_STRIDE + c])
            o_v[slot, t, pl.ds(kk * GATHER_ROWS, GATHER_ROWS)] = tot

        _sc_gather_stream(wid, n_batches, u_hbm.at[layer], [eid_hbm, xn_hbm], [idx_v, x_v], out_hbm, o_v, bufs,
                          gsem, ssem, osem, compute)

    return k(eid, xn, u_tabs)


def _peer_combine_sc(eid, a, v_tabs, layer):
    T = eid.shape[0]
    info, mesh, n_batches = _sc_mesh_and_batches(T)

    @functools.partial(
        pl.kernel, mesh=mesh,
        out_type=jax.ShapeDtypeStruct((T, D_MODEL), jnp.float32),
        scratch_types=[
            pltpu.VMEM((2, SC_TOKENS, N_SEL), jnp.int32),
            pltpu.VMEM((2, SC_TOKENS, N_SEL), jnp.float32),
            pltpu.VMEM((2, SC_TOKENS, D_MODEL), jnp.float32),
            pltpu.VMEM((GATHER_BUFS, GATHER_ROWS, D_MODEL), jnp.float32),
            pltpu.SemaphoreType.DMA((GATHER_BUFS,)),
            pltpu.SemaphoreType.DMA((2,)),
            pltpu.SemaphoreType.DMA((2,)),
        ],
        compiler_params=pltpu.CompilerParams(needs_layout_passes=False),
        name="peer_combine_sc",
    )
    def k(eid_hbm, a_hbm, v_hbm, out_hbm, idx_v, a_v, o_v, bufs, gsem, ssem, osem):
        wid = lax.axis_index("s") * info.num_cores + lax.axis_index("c")

        def compute(slot, t, kk, b):
            svec = jnp.full((SC_LANES,), slot, jnp.int32)
            tvec = jnp.full((SC_LANES,), t, jnp.int32)
            ws = [plsc.load_gather(a_v, [svec, tvec, jnp.full((SC_LANES,), kk * GATHER_ROWS + r, jnp.int32)])
                  for r in range(GATHER_ROWS)]

            @plsc.parallel_loop(0, D_MODEL // SC_LANES, unroll=2)
            def _(c):
                sl = pl.ds(c * SC_LANES, SC_LANES)
                terms = [ws[r] * bufs[b, r, sl] for r in range(GATHER_ROWS)]
                if kk != 0:
                    terms.append(o_v[slot, t, sl])
                while len(terms) > 1:
                    pairs = [terms[i] + terms[i + 1] for i in range(0, len(terms) - 1, 2)]
                    terms = pairs + ([terms[-1]] if len(terms) % 2 else [])
                o_v[slot, t, sl] = terms[0]

        _sc_gather_stream(wid, n_batches, v_hbm.at[layer], [eid_hbm, a_hbm], [idx_v, a_v], out_hbm, o_v, bufs,
                          gsem, ssem, osem, compute)

    return k(eid, a, v_tabs)


def _peer_act_kernel(hp_ref, gate_ref, a_ref):
    a_ref[...] = _gelu(hp_ref[...]) * gate_ref[...]


ELEMENTWISE_ROWS = 1024


def _peer_act(hpre, gate):
    T = hpre.shape[0]
    rows = ELEMENTWISE_ROWS
    assert T % rows == 0
    spec = pl.BlockSpec((rows, N_SEL), lambda i: (i, 0))
    return pl.pallas_call(
        _peer_act_kernel, grid=(T // rows,), in_specs=[spec, spec], out_specs=spec,
        out_shape=jax.ShapeDtypeStruct((T, N_SEL), jnp.float32),
        compiler_params=pltpu.CompilerParams(dimension_semantics=("arbitrary",)),
        name="peer_act",
    )(hpre, gate)


def _residual_kernel(h_ref, c_ref, g_ref, o_ref, *, final_norm):
    y = h_ref[...] + c_ref[...]
    o_ref[...] = _rmsnorm_rows(y, g_ref[...]) if final_norm else y


def _residual(h, c, gfin, *, final_norm):
    T = h.shape[0]
    rows = ELEMENTWISE_ROWS
    assert T % rows == 0
    spec = pl.BlockSpec((rows, D_MODEL), lambda i: (i, 0))
    return pl.pallas_call(
        functools.partial(_residual_kernel, final_norm=final_norm), grid=(T // rows,),
        in_specs=[spec, spec, pl.BlockSpec((1, D_MODEL), lambda i: (0, 0))], out_specs=spec,
        out_shape=jax.ShapeDtypeStruct((T, D_MODEL), jnp.float32),
        compiler_params=pltpu.CompilerParams(dimension_semantics=("arbitrary",)),
        name="peer_residual",
    )(h, c, gfin.reshape(1, D_MODEL))


def _peer_experts(eid, xn, gate, h, gfin, u_tabs, v_tabs, layer, *, final_norm):
    hpre = _peer_hidden_sc(eid, xn, u_tabs, layer)
    a = _peer_act(hpre, gate)
    c = _peer_combine_sc(eid, a, v_tabs, layer)
    return _residual(h, c, gfin, final_norm=final_norm)


KV_WIDTH = N_KV_HEADS * HEAD_DIM
BF16 = jnp.bfloat16


def _qkv(h, gkv, gq, wkv_ref, wq_ref):
    kv = jnp.dot(_rmsnorm_rows(h, gkv).astype(BF16), wkv_ref[...], preferred_element_type=jnp.float32)
    q = jnp.dot(_rmsnorm_rows(h, gq).astype(BF16), wq_ref[...], preferred_element_type=jnp.float32)
    return kv, q


def _sink_softmax_pv(parts, sink):
    m = sink
    for s, _ in parts:
        m = jnp.maximum(m, jnp.max(s, axis=-1, keepdims=True))
    den = jnp.exp(sink - m)
    acc = None
    for s, v in parts:
        e = jnp.exp(s - m)
        den = den + jnp.sum(e, axis=-1, keepdims=True)
        pv = jnp.dot(e.astype(BF16), v.astype(BF16), preferred_element_type=jnp.float32)
        acc = pv if acc is None else acc + pv
    return acc / den


def _nt_dot(a, b):
    return lax.dot_general(a.astype(BF16), b.astype(BF16), (((1,), (1,)), ((), ())),
                           preferred_element_type=jnp.float32)


def _attn_prompt_kernel(h_ref, gkv_ref, gq_ref, wkv_ref, wq_ref, wo_ref, sink_ref,
                        o_ref, kvw_ref, prev_ref):
    blk = pl.program_id(1)
    h = h_ref[...]
    kv, q = _qkv(h, gkv_ref[...], gq_ref[...], wkv_ref, wq_ref)
    kvw_ref[0] = kv

    @pl.when(blk == 0)
    def _():
        prev_ref[...] = jnp.zeros_like(prev_ref)

    prev = prev_ref[...]
    qi = lax.broadcasted_iota(jnp.int32, (WINDOW, WINDOW), 0)
    kj = lax.broadcasted_iota(jnp.int32, (WINDOW, WINDOW), 1)
    prev_ok = jnp.logical_and(kj > qi, blk > 0)
    cur_ok = kj <= qi
    heads = []
    for kvh in range(N_KV_HEADS):
        ks = slice(kvh * HEAD_DIM, (kvh + 1) * HEAD_DIM)
        vs = slice(KV_WIDTH + kvh * HEAD_DIM, KV_WIDTH + (kvh + 1) * HEAD_DIM)
        for g in range(Q_PER_KV):
            hq = kvh * Q_PER_KV + g
            qh = q[:, hq * HEAD_DIM:(hq + 1) * HEAD_DIM]
            sp = jnp.where(prev_ok, _nt_dot(qh, prev[:, ks]) * ATTN_SCALE, -jnp.inf)
            sc = jnp.where(cur_ok, _nt_dot(qh, kv[:, ks]) * ATTN_SCALE, -jnp.inf)
            heads.append(_sink_softmax_pv([(sp, prev[:, vs]), (sc, kv[:, vs])], sink_ref[hq]))
    o = jnp.concatenate(heads, axis=1)
    o_ref[...] = h + jnp.dot(o.astype(BF16), wo_ref[...], preferred_element_type=jnp.float32)
    prev_ref[...] = kv


def _attn_prompt(h, gkv, gq, wkv, wq, wo, sinks, *, n_seq, seq_len):
    nb = seq_len // WINDOW
    row_spec = pl.BlockSpec((WINDOW, D_MODEL), lambda n, b: (n * nb + b, 0))
    full = lambda shape: pl.BlockSpec(shape, lambda n, b: (0,) * len(shape))
    return pl.pallas_call(
        _attn_prompt_kernel,
        grid=(n_seq, nb),
        in_specs=[
            row_spec, full((1, D_MODEL)), full((1, D_MODEL)),
            full((D_MODEL, 2 * KV_WIDTH)), full((D_MODEL, D_MODEL)), full((D_MODEL, D_MODEL)),
            pl.BlockSpec(memory_space=pltpu.SMEM),
        ],
        out_specs=[row_spec, pl.BlockSpec((1, WINDOW, 2 * KV_WIDTH), lambda n, b: (n, 0, 0))],
        out_shape=[
            jax.ShapeDtypeStruct(h.shape, jnp.float32),
            jax.ShapeDtypeStruct((n_seq, WINDOW, 2 * KV_WIDTH), jnp.float32),
        ],
        scratch_shapes=[pltpu.VMEM((WINDOW, 2 * KV_WIDTH), jnp.float32)],
        compiler_params=pltpu.CompilerParams(
            dimension_semantics=("arbitrary", "arbitrary"), vmem_limit_bytes=VMEM_LIMIT_BYTES),
        name="attn_prompt",
    )(h, gkv.reshape(1, D_MODEL), gq.reshape(1, D_MODEL), wkv, wq, wo, sinks)


ATTN_SEQS = 16
PROMPT_PIECES = 2


def _attn_sample_kernel(h_ref, ck_ref, cv_ref, gkv_ref, gq_ref, wkv_ref, wq_ref, wo_ref, sink_ref,
                        o_ref, kw_ref, vw_ref, kv_ref, q_ref, att_ref, *, n_new):
    h = h_ref[...]
    kv, q = _qkv(h, gkv_ref[...], gq_ref[...], wkv_ref, wq_ref)
    kv_ref[...] = kv
    q_ref[...] = q
    n_seq = h.shape[0] // n_new
    rows = Q_PER_KV * n_new
    qpos_c = lax.broadcasted_iota(jnp.int32, (rows, WINDOW), 0) % n_new
    cache_ok = lax.broadcasted_iota(jnp.int32, (rows, WINDOW), 1) > qpos_c
    qpos_n = lax.broadcasted_iota(jnp.int32, (rows, n_new), 0) % n_new
    new_ok = lax.broadcasted_iota(jnp.int32, (rows, n_new), 1) <= qpos_n

    def per_seq(n, carry):
        r0 = pl.multiple_of(n * n_new, n_new)
        kvn = kv_ref[pl.ds(r0, n_new), :]
        qn = q_ref[pl.ds(r0, n_new), :]
        ck = ck_ref[n]
        cv = cv_ref[n]
        kw_ref[n, 0:WINDOW - n_new, :] = ck[n_new:, :]
        kw_ref[n, WINDOW - n_new:WINDOW, :] = kvn[:, 0:KV_WIDTH]
        vw_ref[n, 0:WINDOW - n_new, :] = cv[n_new:, :]
        vw_ref[n, WINDOW - n_new:WINDOW, :] = kvn[:, KV_WIDTH:2 * KV_WIDTH]
        outs = []
        for kvh in range(N_KV_HEADS):
            ks = slice(kvh * HEAD_DIM, (kvh + 1) * HEAD_DIM)
            vs = slice(KV_WIDTH + kvh * HEAD_DIM, KV_WIDTH + (kvh + 1) * HEAD_DIM)
            qs = jnp.concatenate(
                [qn[:, (kvh * Q_PER_KV + g) * HEAD_DIM:(kvh * Q_PER_KV + g + 1) * HEAD_DIM] for g in range(Q_PER_KV)],
                axis=0)
            s_c = jnp.where(cache_ok, _nt_dot(qs, ck[:, ks]) * ATTN_SCALE, -jnp.inf)
            s_n = jnp.where(new_ok, _nt_dot(qs, kvn[:, ks]) * ATTN_SCALE, -jnp.inf)
            sink = sink_ref[kvh][:, 0:1]
            o = _sink_softmax_pv([(s_c, cv[:, ks]), (s_n, kvn[:, vs])], sink)
            outs += [o[g * n_new:(g + 1) * n_new, :] for g in range(Q_PER_KV)]
        att_ref[pl.ds(r0, n_new), :] = jnp.concatenate(outs, axis=1)
        return carry

    lax.fori_loop(0, n_seq, per_seq, 0)
    o_ref[...] = h + jnp.dot(att_ref[...].astype(BF16), wo_ref[...], preferred_element_type=jnp.float32)


def _attn_sample(h, cache_k, cache_v, gkv, gq, wkv, wq, wo, sinks, *, n_new):
    n_seq = cache_k.shape[0]
    sb = ATTN_SEQS
    rows = sb * n_new
    row_spec = pl.BlockSpec((rows, D_MODEL), lambda i: (i, 0))
    win_spec = pl.BlockSpec((sb, WINDOW, KV_WIDTH), lambda i: (i, 0, 0))
    full = lambda shape: pl.BlockSpec(shape, lambda i: (0,) * len(shape))
    sink_rows = jnp.repeat(sinks.reshape(N_KV_HEADS, Q_PER_KV), n_new, axis=1)[:, :, None]
    sink_rows = jnp.broadcast_to(sink_rows, (N_KV_HEADS, Q_PER_KV * n_new, LANES))
    return pl.pallas_call(
        functools.partial(_attn_sample_kernel, n_new=n_new),
        grid=(n_seq // sb,),
        in_specs=[
            row_spec, win_spec, win_spec, full((1, D_MODEL)), full((1, D_MODEL)),
            full((D_MODEL, 2 * KV_WIDTH)), full((D_MODEL, D_MODEL)), full((D_MODEL, D_MODEL)),
            full((N_KV_HEADS, Q_PER_KV * n_new, LANES)),
        ],
        out_specs=[row_spec, win_spec, win_spec],
        out_shape=[
            jax.ShapeDtypeStruct(h.shape, jnp.float32),
            jax.ShapeDtypeStruct((n_seq, WINDOW, KV_WIDTH), jnp.float32),
            jax.ShapeDtypeStruct((n_seq, WINDOW, KV_WIDTH), jnp.float32),
        ],
        scratch_shapes=[
            pltpu.VMEM((rows, 2 * KV_WIDTH), jnp.float32),
            pltpu.VMEM((rows, D_MODEL), jnp.float32),
            pltpu.VMEM((rows, D_MODEL), jnp.float32),
        ],
        compiler_params=pltpu.CompilerParams(
            dimension_semantics=("arbitrary",), vmem_limit_bytes=VMEM_LIMIT_BYTES),
        name="attn_sample",
    )(h, cache_k, cache_v, gkv.reshape(1, D_MODEL), gq.reshape(1, D_MODEL), wkv, wq, wo, sink_rows)


def kernel(x_prompt, x_sample, state_ssm_re, state_ssm_im, cache_k_win, cache_v_win, norm_mix, norm_ffn, norm_kv, norm_final, ssm_lam_re, ssm_lam_im, ssm_log_dt, ssm_b_re, ssm_b_im, ssm_c_re, ssm_c_im, ssm_d, ssm_w_glu, w_kv, w_q, attn_sinks, w_o, peer_w_q, peer_sub_keys, peer_u, peer_v):
    bmat, cmat, apr, api = _s5_discretize(ssm_lam_re[0], ssm_lam_im[0], ssm_log_dt[0], ssm_b_re[0], ssm_b_im[0], ssm_c_re[0], ssm_c_im[0])
    cmat = cmat.astype(jnp.bfloat16)
    wglu = ssm_w_glu[0].astype(jnp.bfloat16)
    wkv = w_kv.astype(BF16)
    wq = w_q[0].astype(BF16)
    wo = w_o[0].astype(BF16)
    peer_wq = [peer_w_q[layer].astype(BF16).reshape(D_MODEL, PEER_HEADS, 2 * PEER_DHALF).transpose(1, 0, 2)
               for layer in range(2)]
    peer_sk = [peer_sub_keys[layer].astype(BF16) for layer in range(2)]

    def peer(h, layer, final_norm):
        xn, eid_t, gate_t = _peer_route(h, norm_ffn[layer], peer_wq[layer], peer_sk[layer])
        return _peer_experts(eid_t.T, xn, gate_t.T, h, norm_final, peer_u, peer_v, layer, final_norm=final_norm)

    win = lambda a: a.reshape(a.shape[0], WINDOW, N_KV_HEADS, HEAD_DIM)

    seq_len = x_prompt.shape[1]
    xp_all = x_prompt.reshape(-1, D_MODEL)

    def trunk_prompt(seq):
        piece = seq_len // PROMPT_PIECES
        sr = si = jnp.zeros((1, N_CHUNKS, 1, CHUNK_STATE), jnp.float32)
        h2 = []
        for p in range(PROMPT_PIECES):
            row0 = seq * seq_len + p * piece
            z, sr, si = _s5_mixer(xp_all, sr, si, norm_mix[0], ssm_d[0], bmat, cmat, apr, api,
                                  chain=True, seq_len=piece, row0=row0)
            h2.append(peer(_glu_residual(z, xp_all, wglu, res_row0=row0), 0, False))
        h3, kvw = _attn_prompt(jnp.concatenate(h2, axis=0), norm_kv, norm_mix[1], wkv, wq, wo, attn_sinks[0],
                               n_seq=1, seq_len=seq_len)
        y = peer(h3, 1, True)
        return (y.reshape(1, seq_len, D_MODEL), _chunks_to_state(sr), _chunks_to_state(si),
                win(kvw[:, :, :KV_WIDTH]), win(kvw[:, :, KV_WIDTH:]))

    def trunk_sample(x_s, s_re, s_im, c_k, c_v):
        n_s = x_s.shape[0]
        xs = x_s.reshape(-1, D_MODEL)
        z, sr, si = _s5_mixer(xs, _state_to_chunks(s_re), _state_to_chunks(s_im), norm_mix[0], ssm_d[0],
                              bmat, cmat, apr, api, chain=False, seq_len=x_s.shape[1])
        h2 = peer(_glu_residual(z, xs, wglu), 0, False)
        h3, kw, vw = _attn_sample(h2, c_k.reshape(n_s, WINDOW, KV_WIDTH), c_v.reshape(n_s, WINDOW, KV_WIDTH),
                                  norm_kv, norm_mix[1], wkv, wq, wo, attn_sinks[0], n_new=x_s.shape[1])
        y = peer(h3, 1, True)
        return y.reshape(x_s.shape), _chunks_to_state(sr), _chunks_to_state(si), win(kw), win(vw)

    y_s, sre_s, sim_s, kw_s, vw_s = trunk_sample(x_sample, state_ssm_re[0], state_ssm_im[0], cache_k_win, cache_v_win)
    outs = [trunk_prompt(seq) for seq in range(x_prompt.shape[0])]
    y_p, sre_p, sim_p, kw_p, vw_p = [jnp.concatenate(parts, axis=0) for parts in zip(*outs)]
    return (y_p, y_s, sre_p[None], sim_p[None], kw_p, vw_p, sre_s[None], sim_s[None], kw_s, vw_s)
```

```python
import functools
import math

import jax
import jax.numpy as jnp
from jax import lax
from jax.experimental import pallas as pl
from jax.experimental.pallas import tpu as pltpu
from jax.experimental.pallas import tpu_sc as plsc

D_MODEL = 1024
GROUP_SIZE = 16
N_GROUPS = D_MODEL // GROUP_SIZE
STATE_DIM = 64
HEAD_DIM = 64
N_Q_HEADS = D_MODEL // HEAD_DIM
N_KV_HEADS = N_Q_HEADS // 8
Q_PER_KV = N_Q_HEADS // N_KV_HEADS
WINDOW = 128
ATTN_SCALE = 1.0 / math.sqrt(HEAD_DIM)
PEER_HEADS = 8
N_KEYS = 128
PEER_TOPK = 16
PEER_DHALF = 128
EPS = 1e-5

LANES = 128
SUBLANES = 8
VMEM_LIMIT_BYTES = 56 * 1024 * 1024

GROUPS_PER_CHUNK = LANES // GROUP_SIZE
N_CHUNKS = N_GROUPS // GROUPS_PER_CHUNK
CHUNK_STATE = GROUPS_PER_CHUNK * STATE_DIM
S5_ROWS = 256


def _rmsnorm_rows(x, g):
    r = lax.rsqrt(jnp.mean(x * x, axis=-1, keepdims=True) + EPS)
    return x * r * g


def _gelu(x):
    return 0.5 * x * (1.0 + lax.erf(x * (1.0 / math.sqrt(2.0))))


def _s5_discretize(lam_re, lam_im, log_dt, b_re, b_im, c_re, c_im):
    f32 = jnp.float32
    lr = lam_re.astype(f32)
    li = lam_im.astype(f32)
    dt = jnp.exp(log_dt.astype(f32))[:, None]
    mag = jnp.exp(lr * dt)
    ab_re = mag * jnp.cos(li * dt)
    ab_im = mag * jnp.sin(li * dt)
    den = lr * lr + li * li
    f_re = ((ab_re - 1.0) * lr + ab_im * li) / den
    f_im = (ab_im * lr - (ab_re - 1.0) * li) / den
    br = b_re.astype(f32)
    bi = b_im.astype(f32)
    bb_re = f_re[..., None] * br - f_im[..., None] * bi
    bb_im = f_re[..., None] * bi + f_im[..., None] * br
    eye = jnp.eye(GROUPS_PER_CHUNK, dtype=f32)

    def chunk_rows(v):
        return v.reshape(N_CHUNKS, 1, CHUNK_STATE)

    def in_blocks(bb):
        t = bb.reshape(N_CHUNKS, GROUPS_PER_CHUNK, STATE_DIM, GROUP_SIZE).transpose(0, 1, 3, 2)
        return jnp.einsum('mgjp,gh->mgjhp', t, eye).reshape(N_CHUNKS, LANES, CHUNK_STATE)

    def out_blocks(c):
        t = c.astype(f32).reshape(N_CHUNKS, GROUPS_PER_CHUNK, GROUP_SIZE, STATE_DIM).transpose(0, 1, 3, 2)
        return jnp.einsum('mgpj,gh->mgphj', t, eye).reshape(N_CHUNKS, CHUNK_STATE, LANES)

    bfull = jnp.concatenate([in_blocks(bb_re), in_blocks(bb_im)], axis=2)
    b_hi = bfull.astype(jnp.bfloat16)
    b_lo = (bfull - b_hi.astype(f32)).astype(jnp.bfloat16)
    bmat = (jnp.concatenate([b_hi, b_hi], axis=1), b_lo)
    cmat = jnp.concatenate([out_blocks(c_re), -out_blocks(c_im)], axis=1)
    pr, pi = [ab_re], [ab_im]
    for _ in range(SUBLANES - 1):
        pr, pi = pr + [pr[-1] * ab_re - pi[-1] * ab_im], pi + [pr[-1] * ab_im + pi[-1] * ab_re]
    apr = jnp.concatenate([chunk_rows(v) for v in pr], axis=1)
    api = jnp.concatenate([chunk_rows(v) for v in pi], axis=1)
    return bmat, cmat, apr, api


def _split_bf16(x):
    hi = x.astype(jnp.bfloat16)
    return hi, (x - hi.astype(jnp.float32)).astype(jnp.bfloat16)


def _s5_kernel(x_ref, g_ref, d_ref, bhh_ref, blo_ref, c_ref, apr_ref, api_ref, h0r_ref, h0i_ref,
               z_ref, sr_ref, si_ref, u_ref, us_ref, st_ref, cr_ref, ci_ref, *, chain, blocks_per_seq):
    rb = pl.program_id(0)
    m = pl.program_id(1)
    rows = x_ref.shape[0]
    n_tiles = rows // SUBLANES

    @pl.when(m == 0)
    def _():
        u = _rmsnorm_rows(x_ref[...], g_ref[...])
        for mm in range(N_CHUNKS):
            uc = u[:, mm * LANES:(mm + 1) * LANES]
            u_ref[mm] = uc
            hi, lo = _split_bf16(uc)
            us_ref[mm, :, 0:LANES] = hi
            us_ref[mm, :, LANES:2 * LANES] = lo

    u = u_ref[m]
    us = us_ref[m]
    st_ref[...] = (jnp.dot(us, bhh_ref[m], preferred_element_type=jnp.float32)
                   + jnp.dot(us[:, 0:LANES], blo_ref[m], preferred_element_type=jnp.float32))

    apr = apr_ref[0]
    api = api_ref[0]
    row = lax.broadcasted_iota(jnp.int32, (SUBLANES, CHUNK_STATE), 0)

    if chain:
        @pl.when(rb % blocks_per_seq == 0)
        def _():
            cr_ref[m] = h0r_ref[0, 0]
            ci_ref[m] = h0i_ref[0, 0]

    def tile_step(k, carry):
        r0 = pl.multiple_of(k * SUBLANES, SUBLANES)
        xr = st_ref[pl.ds(r0, SUBLANES), 0:CHUNK_STATE]
        xi = st_ref[pl.ds(r0, SUBLANES), CHUNK_STATE:2 * CHUNK_STATE]
        for d in (1, 2, 4):
            ar = apr[d - 1:d, :]
            ai = api[d - 1:d, :]
            sr = jnp.where(row >= d, pltpu.roll(xr, d, axis=0), 0.0)
            si = jnp.where(row >= d, pltpu.roll(xi, d, axis=0), 0.0)
            xr, xi = xr + ar * sr - ai * si, xi + ar * si + ai * sr
        if chain:
            cr, ci = carry
        else:
            cr = h0r_ref[k, 0]
            ci = h0i_ref[k, 0]
        hr = xr + apr * cr - api * ci
        hi = xi + apr * ci + api * cr
        st_ref[pl.ds(r0, SUBLANES), 0:CHUNK_STATE] = hr
        st_ref[pl.ds(r0, SUBLANES), CHUNK_STATE:2 * CHUNK_STATE] = hi
        lr_ = hr[SUBLANES - 1:SUBLANES, :]
        li_ = hi[SUBLANES - 1:SUBLANES, :]
        if chain:
            return lr_, li_
        sr_ref[k, m] = lr_
        si_ref[k, m] = li_
        return carry

    if chain:
        cr, ci = lax.fori_loop(0, n_tiles, tile_step, (cr_ref[m], ci_ref[m]))
        cr_ref[m] = cr
        ci_ref[m] = ci
        sr_ref[0, m] = cr
        si_ref[0, m] = ci
    else:
        lax.fori_loop(0, n_tiles, tile_step, 0)

    y = jnp.dot(st_ref[...].astype(jnp.bfloat16), c_ref[m], preferred_element_type=jnp.float32)
    y = y + d_ref[0] * u
    z_ref[...] = _gelu(y)


def _s5_mixer(x, h0r, h0i, g, d_skip, bmat, cmat_bf16, apr, api, *, chain, seq_len, row0=0):
    nseq = h0r.shape[0]
    T = nseq * seq_len
    rows = S5_ROWS
    blk0 = row0 // rows
    if chain:
        blocks_per_seq = seq_len // rows
        seq_blk = 1
        seq_map = lambda rb, m: (rb // blocks_per_seq, m, 0, 0)
        out_map = lambda rb, m: (rb // blocks_per_seq, 0, 0, 0)
    else:
        assert seq_len == SUBLANES
        blocks_per_seq = 1
        seq_blk = rows // SUBLANES
        seq_map = lambda rb, m: (rb, m, 0, 0)
        out_map = lambda rb, m: (rb, 0, 0, 0)
    grid = (T // rows, N_CHUNKS)
    kern = functools.partial(_s5_kernel, chain=chain, blocks_per_seq=blocks_per_seq)
    st_spec = pl.BlockSpec((seq_blk, 1, 1, CHUNK_STATE), seq_map)
    out_st_spec = pl.BlockSpec((seq_blk, N_CHUNKS, 1, CHUNK_STATE), out_map)
    z, sr, si = pl.pallas_call(
        kern,
        grid=grid,
        in_specs=[
            pl.BlockSpec((rows, D_MODEL), lambda rb, m: (rb + blk0, 0)),
            pl.BlockSpec((1, D_MODEL), lambda rb, m: (0, 0)),
            pl.BlockSpec((1, 1, LANES), lambda rb, m: (m, 0, 0)),
            pl.BlockSpec((N_CHUNKS, 2 * LANES, 2 * CHUNK_STATE), lambda rb, m: (0, 0, 0)),
            pl.BlockSpec((N_CHUNKS, LANES, 2 * CHUNK_STATE), lambda rb, m: (0, 0, 0)),
            pl.BlockSpec((N_CHUNKS, 2 * CHUNK_STATE, LANES), lambda rb, m: (0, 0, 0)),
            pl.BlockSpec((1, SUBLANES, CHUNK_STATE), lambda rb, m: (m, 0, 0)),
            pl.BlockSpec((1, SUBLANES, CHUNK_STATE), lambda rb, m: (m, 0, 0)),
            st_spec, st_spec,
        ],
        out_specs=[
            pl.BlockSpec((rows, LANES), lambda rb, m: (rb, m)),
            out_st_spec, out_st_spec,
        ],
        out_shape=[
            jax.ShapeDtypeStruct((T, D_MODEL), jnp.float32),
            jax.ShapeDtypeStruct((nseq, N_CHUNKS, 1, CHUNK_STATE), jnp.float32),
            jax.ShapeDtypeStruct((nseq, N_CHUNKS, 1, CHUNK_STATE), jnp.float32),
        ],
        scratch_shapes=[
            pltpu.VMEM((N_CHUNKS, rows, LANES), jnp.float32),
            pltpu.VMEM((N_CHUNKS, rows, 2 * LANES), jnp.bfloat16),
            pltpu.VMEM((rows, 2 * CHUNK_STATE), jnp.float32),
            pltpu.VMEM((N_CHUNKS, 1, CHUNK_STATE), jnp.float32),
            pltpu.VMEM((N_CHUNKS, 1, CHUNK_STATE), jnp.float32),
        ],
        compiler_params=pltpu.CompilerParams(
            dimension_semantics=("arbitrary", "arbitrary"), vmem_limit_bytes=VMEM_LIMIT_BYTES),
        name="s5_mixer",
    )(x, g.reshape(1, D_MODEL), d_skip.reshape(N_CHUNKS, 1, LANES), bmat[0], bmat[1], cmat_bf16, apr, api,
      h0r, h0i)
    return z, sr, si


def _state_to_chunks(h):
    return h.reshape(h.shape[0], N_CHUNKS, 1, CHUNK_STATE)


def _chunks_to_state(s):
    return s.reshape(s.shape[0], N_GROUPS, STATE_DIM)


def _glu_kernel(z_ref, res_ref, w_ref, o_ref):
    zz = jnp.dot(z_ref[...].astype(jnp.bfloat16), w_ref[...], preferred_element_type=jnp.float32)
    a = zz[:, :D_MODEL]
    b = zz[:, D_MODEL:]
    o_ref[...] = res_ref[...] + a * (1.0 / (1.0 + jnp.exp(-b)))


def _glu_residual(z, res, w_bf16, res_row0=0):
    T = z.shape[0]
    rows = math.gcd(T, 512)
    blk0 = res_row0 // rows
    return pl.pallas_call(
        _glu_kernel,
        grid=(T // rows,),
        in_specs=[
            pl.BlockSpec((rows, D_MODEL), lambda i: (i, 0)),
            pl.BlockSpec((rows, D_MODEL), lambda i: (i + blk0, 0)),
            pl.BlockSpec((D_MODEL, 2 * D_MODEL), lambda i: (0, 0)),
        ],
        out_specs=pl.BlockSpec((rows, D_MODEL), lambda i: (i, 0)),
        out_shape=jax.ShapeDtypeStruct((T, D_MODEL), jnp.float32),
        compiler_params=pltpu.CompilerParams(
            dimension_semantics=("arbitrary",), vmem_limit_bytes=VMEM_LIMIT_BYTES),
        name="glu_residual",
    )(z, res, w_bf16)


ROUTE_ROWS = 256


def _topk_rows(s, payload):
    n_rows = s.shape[0]
    row = lax.broadcasted_iota(jnp.int32, s.shape, 0)
    vals, picks = [], []
    for _ in range(PEER_TOPK):
        m = jnp.max(s, axis=0, keepdims=True)
        pos = jnp.min(jnp.where(s == m, row, n_rows), axis=0, keepdims=True)
        sel = row == pos
        vals.append(m)
        if payload is None:
            picks.append(pos)
        else:
            picks.append(jnp.max(jnp.where(sel, payload, -1), axis=0, keepdims=True))
        s = jnp.where(sel, -jnp.inf, s)
    return jnp.concatenate(vals, axis=0), jnp.concatenate(picks, axis=0)


def _pair_rows(a0, a1, combine):
    half = PEER_TOPK // 2
    parts = [combine(a0[0:1, :], a1)]
    parts += [combine(a0[i:i + 1, :], a1[0:half, :]) for i in range(1, half)]
    parts.append(combine(a0[half:PEER_TOPK, :], a1[0:1, :]))
    return jnp.concatenate(parts, axis=0)


def _peer_route_kernel(h_ref, g_ref, wq_ref, sk_ref, xn_ref, eid_ref, gate_ref, xb_ref):
    hd = pl.program_id(1)

    @pl.when(hd == 0)
    def _():
        xn = _rmsnorm_rows(h_ref[...], g_ref[...])
        xn_ref[...] = xn
        xb_ref[...] = xn.astype(jnp.bfloat16)

    q = jnp.dot(xb_ref[...], wq_ref[hd], preferred_element_type=jnp.float32)
    sv, si = [], []
    for c in range(2):
        qc = q[:, c * PEER_DHALF:(c + 1) * PEER_DHALF].astype(jnp.bfloat16)
        st = lax.dot_general(sk_ref[hd, c], qc, (((1,), (1,)), ((), ())), preferred_element_type=jnp.float32)
        v, i = _topk_rows(st, None)
        sv.append(v)
        si.append(i)
    cand = _pair_rows(sv[0], sv[1], lambda a, b: a + b)
    cid = _pair_rows(si[0], si[1], lambda a, b: a * N_KEYS + b)
    fv, eid = _topk_rows(cand, cid)
    e = jnp.exp(fv - fv[0:1, :])
    gate_ref[...] = e / jnp.sum(e, axis=0, keepdims=True)
    eid_ref[...] = eid


def _peer_route(h, g, wq_bf16, sk_bf16):
    T = h.shape[0]
    rows = ROUTE_ROWS
    n_sel = PEER_HEADS * PEER_TOPK
    return pl.pallas_call(
        _peer_route_kernel,
        grid=(T // rows, PEER_HEADS),
        in_specs=[
            pl.BlockSpec((rows, D_MODEL), lambda tb, hd: (tb, 0)),
            pl.BlockSpec((1, D_MODEL), lambda tb, hd: (0, 0)),
            pl.BlockSpec((PEER_HEADS, D_MODEL, 2 * PEER_DHALF), lambda tb, hd: (0, 0, 0)),
            pl.BlockSpec((PEER_HEADS, 2, N_KEYS, PEER_DHALF), lambda tb, hd: (0, 0, 0, 0)),
        ],
        out_specs=[
            pl.BlockSpec((rows, D_MODEL), lambda tb, hd: (tb, 0)),
            pl.BlockSpec((PEER_TOPK, rows), lambda tb, hd: (hd, tb)),
            pl.BlockSpec((PEER_TOPK, rows), lambda tb, hd: (hd, tb)),
        ],
        out_shape=[
            jax.ShapeDtypeStruct((T, D_MODEL), jnp.float32),
            jax.ShapeDtypeStruct((n_sel, T), jnp.int32),
            jax.ShapeDtypeStruct((n_sel, T), jnp.float32),
        ],
        scratch_shapes=[pltpu.VMEM((rows, D_MODEL), jnp.bfloat16)],
        compiler_params=pltpu.CompilerParams(
            dimension_semantics=("arbitrary", "arbitrary"), vmem_limit_bytes=VMEM_LIMIT_BYTES),
        name="peer_route",
    )(h, g.reshape(1, D_MODEL), wq_bf16, sk_bf16)


N_SEL = PEER_HEADS * PEER_TOPK
SC_LANES = 16
GATHER_ROWS = PEER_TOPK
GATHERS_PER_TOKEN = N_SEL // GATHER_ROWS
GATHER_BUFS = 4
SC_TOKENS = 8
ACC_STRIDE = SC_LANES + 1


def _sc_gather_stream(wid, n_batches, tab_hbm, stage_srcs, stage_bufs, out_hbm, o_v, bufs, gsem, ssem, osem,
                      compute):
    idx_v = stage_bufs[0]

    def stage_copies(bi, slot):
        base = (wid * n_batches + bi) * SC_TOKENS
        return [pltpu.make_async_copy(src.at[pl.ds(base, SC_TOKENS)], buf.at[slot], ssem.at[slot])
                for src, buf in zip(stage_srcs, stage_bufs)]

    def out_copy(bi, slot):
        base = (wid * n_batches + bi) * SC_TOKENS
        return pltpu.make_async_copy(o_v.at[slot], out_hbm.at[pl.ds(base, SC_TOKENS)], osem.at[slot])

    def start(slot, t, kk, b):
        idx = idx_v[slot, t, pl.ds(kk * GATHER_ROWS, GATHER_ROWS)]
        pltpu.async_copy(tab_hbm.at[idx], bufs.at[b], gsem.at[b])

    def wait(b):
        pltpu.make_async_copy(tab_hbm.at[pl.ds(0, GATHER_ROWS)], bufs.at[b], gsem.at[b]).wait()

    for c in stage_copies(0, 0):
        c.start()
    for c in stage_copies(0, 0):
        c.wait()
    for q in range(GATHER_BUFS - 1):
        start(0, q // GATHERS_PER_TOKEN, q % GATHERS_PER_TOKEN, q % GATHER_BUFS)

    def batch(bi, carry):
        slot = bi % 2
        has_next = bi + 1 < n_batches

        @pl.when(has_next)
        def _():
            for c in stage_copies(bi + 1, 1 - slot):
                c.start()

        @pl.when(bi >= 2)
        def _():
            out_copy(bi, slot).wait()

        def tok(t, carry):
            @pl.when(jnp.logical_and(t == SC_TOKENS - 1, has_next))
            def _():
                for c in stage_copies(bi + 1, 1 - slot):
                    c.wait()

            for kk in range(GATHERS_PER_TOKEN):
                nq = kk + GATHER_BUFS - 1
                nk, nb = nq % GATHERS_PER_TOKEN, nq % GATHER_BUFS
                if nq < GATHERS_PER_TOKEN:
                    start(slot, t, nk, nb)
                else:
                    @pl.when(t + 1 < SC_TOKENS)
                    def _():
                        start(slot, t + 1, nk, nb)

                    @pl.when(jnp.logical_and(t + 1 == SC_TOKENS, has_next))
                    def _():
                        start(1 - slot, 0, nk, nb)

                wait(kk % GATHER_BUFS)
                compute(slot, t, kk, kk % GATHER_BUFS)
            return carry

        lax.fori_loop(0, SC_TOKENS, tok, 0)
        out_copy(bi, slot).start()
        return carry

    lax.fori_loop(0, n_batches, batch, 0)
    if n_batches >= 2:
        out_copy(n_batches - 2, (n_batches - 2) % 2).wait()
    out_copy(n_batches - 1, (n_batches - 1) % 2).wait()


def _sc_mesh_and_batches(n_tokens):
    info = plsc.get_sparse_core_info()
    assert info.num_lanes == SC_LANES
    n_workers = info.num_cores * info.num_subcores
    assert n_tokens % (n_workers * SC_TOKENS) == 0
    mesh = plsc.VectorSubcoreMesh(core_axis_name="c", subcore_axis_name="s")
    return info, mesh, n_tokens // (n_workers * SC_TOKENS)


def _peer_hidden_sc(eid, xn, u_tabs, layer):
    T = eid.shape[0]
    info, mesh, n_batches = _sc_mesh_and_batches(T)

    @functools.partial(
        pl.kernel, mesh=mesh,
        out_type=jax.ShapeDtypeStruct((T, N_SEL), jnp.float32),
        scratch_types=[
            pltpu.VMEM((2, SC_TOKENS, N_SEL), jnp.int32),
            pltpu.VMEM((2, SC_TOKENS, D_MODEL), jnp.float32),
            pltpu.VMEM((2, SC_TOKENS, N_SEL), jnp.float32),
            pltpu.VMEM((GATHER_BUFS, GATHER_ROWS, D_MODEL), jnp.float32),
            pltpu.VMEM((GATHER_ROWS * ACC_STRIDE,), jnp.float32),
            pltpu.SemaphoreType.DMA((GATHER_BUFS,)),
            pltpu.SemaphoreType.DMA((2,)),
            pltpu.SemaphoreType.DMA((2,)),
        ],
        compiler_params=pltpu.CompilerParams(needs_layout_passes=False),
        name="peer_hidden_sc",
    )
    def k(eid_hbm, xn_hbm, u_hbm, out_hbm, idx_v, x_v, o_v, bufs, acc_v, gsem, ssem, osem):
        wid = lax.axis_index("s") * info.num_cores + lax.axis_index("c")
        lane = lax.iota(jnp.int32, SC_LANES)
        zero = jnp.zeros((SC_LANES,), jnp.float32)

        def compute(slot, t, kk, b):
            @plsc.parallel_loop(0, D_MODEL // SC_LANES, carry=(zero,) * GATHER_ROWS)
            def accs(c, accs):
                xc = x_v[slot, t, pl.ds(c * SC_LANES, SC_LANES)]
                return tuple(accs[r] + bufs[b, r, pl.ds(c * SC_LANES, SC_LANES)] * xc for r in range(GATHER_ROWS))

            for r in range(GATHER_ROWS):
                acc_v[pl.ds(r * ACC_STRIDE, SC_LANES)] = accs[r]
            tot = zero
            for c in range(SC_LANES):
                tot = tot + plsc.load_gather(acc_v, [lane * ACC_STRIDE + c])
            o_v[slot, t, pl.ds(kk * GATHER_ROWS, GATHER_ROWS)] = tot

        _sc_gather_stream(wid, n_batches, u_hbm.at[layer], [eid_hbm, xn_hbm], [idx_v, x_v], out_hbm, o_v, bufs,
                          gsem, ssem, osem, compute)

    return k(eid, xn, u_tabs)


def _peer_combine_sc(eid, a, v_tabs, layer):
    T = eid.shape[0]
    info, mesh, n_batches = _sc_mesh_and_batches(T)

    @functools.partial(
        pl.kernel, mesh=mesh,
        out_type=jax.ShapeDtypeStruct((T, D_MODEL), jnp.float32),
        scratch_types=[
            pltpu.VMEM((2, SC_TOKENS, N_SEL), jnp.int32),
            pltpu.VMEM((2, SC_TOKENS, N_SEL), jnp.float32),
            pltpu.VMEM((2, SC_TOKENS, D_MODEL), jnp.float32),
            pltpu.VMEM((GATHER_BUFS, GATHER_ROWS, D_MODEL), jnp.float32),
            pltpu.SemaphoreType.DMA((GATHER_BUFS,)),
            pltpu.SemaphoreType.DMA((2,)),
            pltpu.SemaphoreType.DMA((2,)),
        ],
        compiler_params=pltpu.CompilerParams(needs_layout_passes=False),
        name="peer_combine_sc",
    )
    def k(eid_hbm, a_hbm, v_hbm, out_hbm, idx_v, a_v, o_v, bufs, gsem, ssem, osem):
        wid = lax.axis_index("s") * info.num_cores + lax.axis_index("c")

        def compute(slot, t, kk, b):
            svec = jnp.full((SC_LANES,), slot, jnp.int32)
            tvec = jnp.full((SC_LANES,), t, jnp.int32)
            ws = [plsc.load_gather(a_v, [svec, tvec, jnp.full((SC_LANES,), kk * GATHER_ROWS + r, jnp.int32)])
                  for r in range(GATHER_ROWS)]

            @plsc.parallel_loop(0, D_MODEL // SC_LANES, unroll=2)
            def _(c):
                sl = pl.ds(c * SC_LANES, SC_LANES)
                terms = [ws[r] * bufs[b, r, sl] for r in range(GATHER_ROWS)]
                if kk != 0:
                    terms.append(o_v[slot, t, sl])
                while len(terms) > 1:
                    pairs = [terms[i] + terms[i + 1] for i in range(0, len(terms) - 1, 2)]
                    terms = pairs + ([terms[-1]] if len(terms) % 2 else [])
                o_v[slot, t, sl] = terms[0]

        _sc_gather_stream(wid, n_batches, v_hbm.at[layer], [eid_hbm, a_hbm], [idx_v, a_v], out_hbm, o_v, bufs,
                          gsem, ssem, osem, compute)

    return k(eid, a, v_tabs)


def _peer_act_kernel(hp_ref, gate_ref, a_ref):
    a_ref[...] = _gelu(hp_ref[...]) * gate_ref[...]


ELEMENTWISE_ROWS = 1024


def _peer_act(hpre, gate):
    T = hpre.shape[0]
    rows = ELEMENTWISE_ROWS
    assert T % rows == 0
    spec = pl.BlockSpec((rows, N_SEL), lambda i: (i, 0))
    return pl.pallas_call(
        _peer_act_kernel, grid=(T // rows,), in_specs=[spec, spec], out_specs=spec,
        out_shape=jax.ShapeDtypeStruct((T, N_SEL), jnp.float32),
        compiler_params=pltpu.CompilerParams(dimension_semantics=("arbitrary",)),
        name="peer_act",
    )(hpre, gate)


def _residual_kernel(h_ref, c_ref, g_ref, o_ref, *, final_norm):
    y = h_ref[...] + c_ref[...]
    o_ref[...] = _rmsnorm_rows(y, g_ref[...]) if final_norm else y


def _residual(h, c, gfin, *, final_norm):
    T = h.shape[0]
    rows = ELEMENTWISE_ROWS
    assert T % rows == 0
    spec = pl.BlockSpec((rows, D_MODEL), lambda i: (i, 0))
    return pl.pallas_call(
        functools.partial(_residual_kernel, final_norm=final_norm), grid=(T // rows,),
        in_specs=[spec, spec, pl.BlockSpec((1, D_MODEL), lambda i: (0, 0))], out_specs=spec,
        out_shape=jax.ShapeDtypeStruct((T, D_MODEL), jnp.float32),
        compiler_params=pltpu.CompilerParams(dimension_semantics=("arbitrary",)),
        name="peer_residual",
    )(h, c, gfin.reshape(1, D_MODEL))


def _peer_experts(eid, xn, gate, h, gfin, u_tabs, v_tabs, layer, *, final_norm):
    hpre = _peer_hidden_sc(eid, xn, u_tabs, layer)
    a = _peer_act(hpre, gate)
    c = _peer_combine_sc(eid, a, v_tabs, layer)
    return _residual(h, c, gfin, final_norm=final_norm)


KV_WIDTH = N_KV_HEADS * HEAD_DIM
BF16 = jnp.bfloat16


def _qkv(h, gkv, gq, wkv_ref, wq_ref):
    kv = jnp.dot(_rmsnorm_rows(h, gkv).astype(BF16), wkv_ref[...], preferred_element_type=jnp.float32)
    q = jnp.dot(_rmsnorm_rows(h, gq).astype(BF16), wq_ref[...], preferred_element_type=jnp.float32)
    return kv, q


def _sink_softmax_pv(parts, sink):
    m = sink
    for s, _ in parts:
        m = jnp.maximum(m, jnp.max(s, axis=-1, keepdims=True))
    den = jnp.exp(sink - m)
    acc = None
    for s, v in parts:
        e = jnp.exp(s - m)
        den = den + jnp.sum(e, axis=-1, keepdims=True)
        pv = jnp.dot(e.astype(BF16), v.astype(BF16), preferred_element_type=jnp.float32)
        acc = pv if acc is None else acc + pv
    return acc / den


def _nt_dot(a, b):
    return lax.dot_general(a.astype(BF16), b.astype(BF16), (((1,), (1,)), ((), ())),
                           preferred_element_type=jnp.float32)


def _attn_prompt_kernel(h_ref, gkv_ref, gq_ref, wkv_ref, wq_ref, wo_ref, sink_ref,
                        o_ref, kvw_ref, prev_ref):
    blk = pl.program_id(1)
    h = h_ref[...]
    kv, q = _qkv(h, gkv_ref[...], gq_ref[...], wkv_ref, wq_ref)
    kvw_ref[0] = kv

    @pl.when(blk == 0)
    def _():
        prev_ref[...] = jnp.zeros_like(prev_ref)

    prev = prev_ref[...]
    qi = lax.broadcasted_iota(jnp.int32, (WINDOW, WINDOW), 0)
    kj = lax.broadcasted_iota(jnp.int32, (WINDOW, WINDOW), 1)
    prev_ok = jnp.logical_and(kj > qi, blk > 0)
    cur_ok = kj <= qi
    heads = []
    for kvh in range(N_KV_HEADS):
        ks = slice(kvh * HEAD_DIM, (kvh + 1) * HEAD_DIM)
        vs = slice(KV_WIDTH + kvh * HEAD_DIM, KV_WIDTH + (kvh + 1) * HEAD_DIM)
        for g in range(Q_PER_KV):
            hq = kvh * Q_PER_KV + g
            qh = q[:, hq * HEAD_DIM:(hq + 1) * HEAD_DIM]
            sp = jnp.where(prev_ok, _nt_dot(qh, prev[:, ks]) * ATTN_SCALE, -jnp.inf)
            sc = jnp.where(cur_ok, _nt_dot(qh, kv[:, ks]) * ATTN_SCALE, -jnp.inf)
            heads.append(_sink_softmax_pv([(sp, prev[:, vs]), (sc, kv[:, vs])], sink_ref[hq]))
    o = jnp.concatenate(heads, axis=1)
    o_ref[...] = h + jnp.dot(o.astype(BF16), wo_ref[...], preferred_element_type=jnp.float32)
    prev_ref[...] = kv


def _attn_prompt(h, gkv, gq, wkv, wq, wo, sinks, *, n_seq, seq_len):
    nb = seq_len // WINDOW
    row_spec = pl.BlockSpec((WINDOW, D_MODEL), lambda n, b: (n * nb + b, 0))
    full = lambda shape: pl.BlockSpec(shape, lambda n, b: (0,) * len(shape))
    return pl.pallas_call(
        _attn_prompt_kernel,
        grid=(n_seq, nb),
        in_specs=[
            row_spec, full((1, D_MODEL)), full((1, D_MODEL)),
            full((D_MODEL, 2 * KV_WIDTH)), full((D_MODEL, D_MODEL)), full((D_MODEL, D_MODEL)),
            pl.BlockSpec(memory_space=pltpu.SMEM),
        ],
        out_specs=[row_spec, pl.BlockSpec((1, WINDOW, 2 * KV_WIDTH), lambda n, b: (n, 0, 0))],
        out_shape=[
            jax.ShapeDtypeStruct(h.shape, jnp.float32),
            jax.ShapeDtypeStruct((n_seq, WINDOW, 2 * KV_WIDTH), jnp.float32),
        ],
        scratch_shapes=[pltpu.VMEM((WINDOW, 2 * KV_WIDTH), jnp.float32)],
        compiler_params=pltpu.CompilerParams(
            dimension_semantics=("arbitrary", "arbitrary"), vmem_limit_bytes=VMEM_LIMIT_BYTES),
        name="attn_prompt",
    )(h, gkv.reshape(1, D_MODEL), gq.reshape(1, D_MODEL), wkv, wq, wo, sinks)


ATTN_SEQS = 16
PROMPT_PIECES = 2


def _attn_sample_kernel(h_ref, ck_ref, cv_ref, gkv_ref, gq_ref, wkv_ref, wq_ref, wo_ref, sink_ref,
                        o_ref, kw_ref, vw_ref, kv_ref, q_ref, att_ref, *, n_new):
    h = h_ref[...]
    kv, q = _qkv(h, gkv_ref[...], gq_ref[...], wkv_ref, wq_ref)
    kv_ref[...] = kv
    q_ref[...] = q
    n_seq = h.shape[0] // n_new
    rows = Q_PER_KV * n_new
    qpos_c = lax.broadcasted_iota(jnp.int32, (rows, WINDOW), 0) % n_new
    cache_ok = lax.broadcasted_iota(jnp.int32, (rows, WINDOW), 1) > qpos_c
    qpos_n = lax.broadcasted_iota(jnp.int32, (rows, n_new), 0) % n_new
    new_ok = lax.broadcasted_iota(jnp.int32, (rows, n_new), 1) <= qpos_n

    def per_seq(n, carry):
        r0 = pl.multiple_of(n * n_new, n_new)
        kvn = kv_ref[pl.ds(r0, n_new), :]
        qn = q_ref[pl.ds(r0, n_new), :]
        ck = ck_ref[n]
        cv = cv_ref[n]
        kw_ref[n, 0:WINDOW - n_new, :] = ck[n_new:, :]
        kw_ref[n, WINDOW - n_new:WINDOW, :] = kvn[:, 0:KV_WIDTH]
        vw_ref[n, 0:WINDOW - n_new, :] = cv[n_new:, :]
        vw_ref[n, WINDOW - n_new:WINDOW, :] = kvn[:, KV_WIDTH:2 * KV_WIDTH]
        outs = []
        for kvh in range(N_KV_HEADS):
            ks = slice(kvh * HEAD_DIM, (kvh + 1) * HEAD_DIM)
            vs = slice(KV_WIDTH + kvh * HEAD_DIM, KV_WIDTH + (kvh + 1) * HEAD_DIM)
            qs = jnp.concatenate(
                [qn[:, (kvh * Q_PER_KV + g) * HEAD_DIM:(kvh * Q_PER_KV + g + 1) * HEAD_DIM] for g in range(Q_PER_KV)],
                axis=0)
            s_c = jnp.where(cache_ok, _nt_dot(qs, ck[:, ks]) * ATTN_SCALE, -jnp.inf)
            s_n = jnp.where(new_ok, _nt_dot(qs, kvn[:, ks]) * ATTN_SCALE, -jnp.inf)
            sink = sink_ref[kvh][:, 0:1]
            o = _sink_softmax_pv([(s_c, cv[:, ks]), (s_n, kvn[:, vs])], sink)
            outs += [o[g * n_new:(g + 1) * n_new, :] for g in range(Q_PER_KV)]
        att_ref[pl.ds(r0, n_new), :] = jnp.concatenate(outs, axis=1)
        return carry

    lax.fori_loop(0, n_seq, per_seq, 0)
    o_ref[...] = h + jnp.dot(att_ref[...].astype(BF16), wo_ref[...], preferred_element_type=jnp.float32)


def _attn_sample(h, cache_k, cache_v, gkv, gq, wkv, wq, wo, sinks, *, n_new):
    n_seq = cache_k.shape[0]
    sb = ATTN_SEQS
    rows = sb * n_new
    row_spec = pl.BlockSpec((rows, D_MODEL), lambda i: (i, 0))
    win_spec = pl.BlockSpec((sb, WINDOW, KV_WIDTH), lambda i: (i, 0, 0))
    full = lambda shape: pl.BlockSpec(shape, lambda i: (0,) * len(shape))
    sink_rows = jnp.repeat(sinks.reshape(N_KV_HEADS, Q_PER_KV), n_new, axis=1)[:, :, None]
    sink_rows = jnp.broadcast_to(sink_rows, (N_KV_HEADS, Q_PER_KV * n_new, LANES))
    return pl.pallas_call(
        functools.partial(_attn_sample_kernel, n_new=n_new),
        grid=(n_seq // sb,),
        in_specs=[
            row_spec, win_spec, win_spec, full((1, D_MODEL)), full((1, D_MODEL)),
            full((D_MODEL, 2 * KV_WIDTH)), full((D_MODEL, D_MODEL)), full((D_MODEL, D_MODEL)),
            full((N_KV_HEADS, Q_PER_KV * n_new, LANES)),
        ],
        out_specs=[row_spec, win_spec, win_spec],
        out_shape=[
            jax.ShapeDtypeStruct(h.shape, jnp.float32),
            jax.ShapeDtypeStruct((n_seq, WINDOW, KV_WIDTH), jnp.float32),
            jax.ShapeDtypeStruct((n_seq, WINDOW, KV_WIDTH), jnp.float32),
        ],
        scratch_shapes=[
            pltpu.VMEM((rows, 2 * KV_WIDTH), jnp.float32),
            pltpu.VMEM((rows, D_MODEL), jnp.float32),
            pltpu.VMEM((rows, D_MODEL), jnp.float32),
        ],
        compiler_params=pltpu.CompilerParams(
            dimension_semantics=("arbitrary",), vmem_limit_bytes=VMEM_LIMIT_BYTES),
        name="attn_sample",
    )(h, cache_k, cache_v, gkv.reshape(1, D_MODEL), gq.reshape(1, D_MODEL), wkv, wq, wo, sink_rows)


def kernel(x_prompt, x_sample, state_ssm_re, state_ssm_im, cache_k_win, cache_v_win, norm_mix, norm_ffn, norm_kv, norm_final, ssm_lam_re, ssm_lam_im, ssm_log_dt, ssm_b_re, ssm_b_im, ssm_c_re, ssm_c_im, ssm_d, ssm_w_glu, w_kv, w_q, attn_sinks, w_o, peer_w_q, peer_sub_keys, peer_u, peer_v):
    bmat, cmat, apr, api = _s5_discretize(ssm_lam_re[0], ssm_lam_im[0], ssm_log_dt[0], ssm_b_re[0], ssm_b_im[0], ssm_c_re[0], ssm_c_im[0])
    cmat = cmat.astype(jnp.bfloat16)
    wglu = ssm_w_glu[0].astype(jnp.bfloat16)
    wkv = w_kv.astype(BF16)
    wq = w_q[0].astype(BF16)
    wo = w_o[0].astype(BF16)
    peer_wq = [peer_w_q[layer].astype(BF16).reshape(D_MODEL, PEER_HEADS, 2 * PEER_DHALF).transpose(1, 0, 2)
               for layer in range(2)]
    peer_sk = [peer_sub_keys[layer].astype(BF16) for layer in range(2)]

    def peer(h, layer, final_norm):
        xn, eid_t, gate_t = _peer_route(h, norm_ffn[layer], peer_wq[layer], peer_sk[layer])
        return _peer_experts(eid_t.T, xn, gate_t.T, h, norm_final, peer_u, peer_v, layer, final_norm=final_norm)

    win = lambda a: a.reshape(a.shape[0], WINDOW, N_KV_HEADS, HEAD_DIM)

    seq_len = x_prompt.shape[1]
    xp_all = x_prompt.reshape(-1, D_MODEL)

    def trunk_prompt(seq, n_pieces):
        piece = seq_len // n_pieces
        sr = si = jnp.zeros((1, N_CHUNKS, 1, CHUNK_STATE), jnp.float32)
        h2 = []
        for p in range(n_pieces):
            row0 = seq * seq_len + p * piece
            z, sr, si = _s5_mixer(xp_all, sr, si, norm_mix[0], ssm_d[0], bmat, cmat, apr, api,
                                  chain=True, seq_len=piece, row0=row0)
            h2.append(peer(_glu_residual(z, xp_all, wglu, res_row0=row0), 0, False))
        h3, kvw = _attn_prompt(jnp.concatenate(h2, axis=0), norm_kv, norm_mix[1], wkv, wq, wo, attn_sinks[0],
                               n_seq=1, seq_len=seq_len)
        y = peer(h3, 1, True)
        return (y.reshape(1, seq_len, D_MODEL), _chunks_to_state(sr), _chunks_to_state(si),
                win(kvw[:, :, :KV_WIDTH]), win(kvw[:, :, KV_WIDTH:]))

    def trunk_sample(x_s, s_re, s_im, c_k, c_v):
        n_s = x_s.shape[0]
        xs = x_s.reshape(-1, D_MODEL)
        z, sr, si = _s5_mixer(xs, _state_to_chunks(s_re), _state_to_chunks(s_im), norm_mix[0], ssm_d[0],
                              bmat, cmat, apr, api, chain=False, seq_len=x_s.shape[1])
        h2 = peer(_glu_residual(z, xs, wglu), 0, False)
        h3, kw, vw = _attn_sample(h2, c_k.reshape(n_s, WINDOW, KV_WIDTH), c_v.reshape(n_s, WINDOW, KV_WIDTH),
                                  norm_kv, norm_mix[1], wkv, wq, wo, attn_sinks[0], n_new=x_s.shape[1])
        y = peer(h3, 1, True)
        return y.reshape(x_s.shape), _chunks_to_state(sr), _chunks_to_state(si), win(kw), win(vw)

    y_s, sre_s, sim_s, kw_s, vw_s = trunk_sample(x_sample, state_ssm_re[0], state_ssm_im[0], cache_k_win, cache_v_win)
    n_prompt = x_prompt.shape[0]
    outs = [trunk_prompt(seq, PROMPT_PIECES if seq == n_prompt - 1 else 1) for seq in range(n_prompt)]
    y_p, sre_p, sim_p, kw_p, vw_p = [jnp.concatenate(parts, axis=0) for parts in zip(*outs)]
    return (y_p, y_s, sre_p[None], sim_p[None], kw_p, vw_p, sre_s[None], sim_s[None], kw_s, vw_s)
```

```python
import functools
import math

import jax
import jax.numpy as jnp
from jax import lax
from jax.experimental import pallas as pl
from jax.experimental.pallas import tpu as pltpu
from jax.experimental.pallas import tpu_sc as plsc

D_MODEL = 1024
GROUP_SIZE = 16
N_GROUPS = D_MODEL // GROUP_SIZE
STATE_DIM = 64
HEAD_DIM = 64
N_Q_HEADS = D_MODEL // HEAD_DIM
N_KV_HEADS = N_Q_HEADS // 8
Q_PER_KV = N_Q_HEADS // N_KV_HEADS
WINDOW = 128
ATTN_SCALE = 1.0 / math.sqrt(HEAD_DIM)
PEER_HEADS = 8
N_KEYS = 128
PEER_TOPK = 16
PEER_DHALF = 128
EPS = 1e-5

LANES = 128
SUBLANES = 8
VMEM_LIMIT_BYTES = 56 * 1024 * 1024

GROUPS_PER_CHUNK = LANES // GROUP_SIZE
N_CHUNKS = N_GROUPS // GROUPS_PER_CHUNK
CHUNK_STATE = GROUPS_PER_CHUNK * STATE_DIM
S5_ROWS = 256


def _rmsnorm_rows(x, g):
    r = lax.rsqrt(jnp.mean(x * x, axis=-1, keepdims=True) + EPS)
    return x * r * g


def _gelu(x):
    return 0.5 * x * (1.0 + lax.erf(x * (1.0 / math.sqrt(2.0))))


def _s5_discretize(lam_re, lam_im, log_dt, b_re, b_im, c_re, c_im):
    f32 = jnp.float32
    lr = lam_re.astype(f32)
    li = lam_im.astype(f32)
    dt = jnp.exp(log_dt.astype(f32))[:, None]
    mag = jnp.exp(lr * dt)
    ab_re = mag * jnp.cos(li * dt)
    ab_im = mag * jnp.sin(li * dt)
    den = lr * lr + li * li
    f_re = ((ab_re - 1.0) * lr + ab_im * li) / den
    f_im = (ab_im * lr - (ab_re - 1.0) * li) / den
    br = b_re.astype(f32)
    bi = b_im.astype(f32)
    bb_re = f_re[..., None] * br - f_im[..., None] * bi
    bb_im = f_re[..., None] * bi + f_im[..., None] * br
    eye = jnp.eye(GROUPS_PER_CHUNK, dtype=f32)

    def chunk_rows(v):
        return v.reshape(N_CHUNKS, 1, CHUNK_STATE)

    def in_blocks(bb):
        t = bb.reshape(N_CHUNKS, GROUPS_PER_CHUNK, STATE_DIM, GROUP_SIZE).transpose(0, 1, 3, 2)
        return jnp.einsum('mgjp,gh->mgjhp', t, eye).reshape(N_CHUNKS, LANES, CHUNK_STATE)

    def out_blocks(c):
        t = c.astype(f32).reshape(N_CHUNKS, GROUPS_PER_CHUNK, GROUP_SIZE, STATE_DIM).transpose(0, 1, 3, 2)
        return jnp.einsum('mgpj,gh->mgphj', t, eye).reshape(N_CHUNKS, CHUNK_STATE, LANES)

    bfull = jnp.concatenate([in_blocks(bb_re), in_blocks(bb_im)], axis=2)
    b_hi = bfull.astype(jnp.bfloat16)
    b_lo = (bfull - b_hi.astype(f32)).astype(jnp.bfloat16)
    bmat = (jnp.concatenate([b_hi, b_hi], axis=1), b_lo)
    cmat = jnp.concatenate([out_blocks(c_re), -out_blocks(c_im)], axis=1)
    pr, pi = [ab_re], [ab_im]
    for _ in range(SUBLANES - 1):
        pr, pi = pr + [pr[-1] * ab_re - pi[-1] * ab_im], pi + [pr[-1] * ab_im + pi[-1] * ab_re]
    apr = jnp.concatenate([chunk_rows(v) for v in pr], axis=1)
    api = jnp.concatenate([chunk_rows(v) for v in pi], axis=1)
    return bmat, cmat, apr, api


def _split_bf16(x):
    hi = x.astype(jnp.bfloat16)
    return hi, (x - hi.astype(jnp.float32)).astype(jnp.bfloat16)


def _s5_kernel(x_ref, g_ref, d_ref, bhh_ref, blo_ref, c_ref, apr_ref, api_ref, h0r_ref, h0i_ref,
               z_ref, sr_ref, si_ref, u_ref, us_ref, st_ref, cr_ref, ci_ref, *, chain, blocks_per_seq):
    rb = pl.program_id(0)
    m = pl.program_id(1)
    rows = x_ref.shape[0]
    n_tiles = rows // SUBLANES

    @pl.when(m == 0)
    def _():
        u = _rmsnorm_rows(x_ref[...], g_ref[...])
        for mm in range(N_CHUNKS):
            uc = u[:, mm * LANES:(mm + 1) * LANES]
            u_ref[mm] = uc
            hi, lo = _split_bf16(uc)
            us_ref[mm, :, 0:LANES] = hi
            us_ref[mm, :, LANES:2 * LANES] = lo

    u = u_ref[m]
    us = us_ref[m]
    st_ref[...] = (jnp.dot(us, bhh_ref[m], preferred_element_type=jnp.float32)
                   + jnp.dot(us[:, 0:LANES], blo_ref[m], preferred_element_type=jnp.float32))

    apr = apr_ref[0]
    api = api_ref[0]
    row = lax.broadcasted_iota(jnp.int32, (SUBLANES, CHUNK_STATE), 0)

    if chain:
        @pl.when(rb % blocks_per_seq == 0)
        def _():
            cr_ref[m] = h0r_ref[0, 0]
            ci_ref[m] = h0i_ref[0, 0]

    def tile_step(k, carry):
        r0 = pl.multiple_of(k * SUBLANES, SUBLANES)
        xr = st_ref[pl.ds(r0, SUBLANES), 0:CHUNK_STATE]
        xi = st_ref[pl.ds(r0, SUBLANES), CHUNK_STATE:2 * CHUNK_STATE]
        for d in (1, 2, 4):
            ar = apr[d - 1:d, :]
            ai = api[d - 1:d, :]
            sr = jnp.where(row >= d, pltpu.roll(xr, d, axis=0), 0.0)
            si = jnp.where(row >= d, pltpu.roll(xi, d, axis=0), 0.0)
            xr, xi = xr + ar * sr - ai * si, xi + ar * si + ai * sr
        if chain:
            cr, ci = carry
        else:
            cr = h0r_ref[k, 0]
            ci = h0i_ref[k, 0]
        hr = xr + apr * cr - api * ci
        hi = xi + apr * ci + api * cr
        st_ref[pl.ds(r0, SUBLANES), 0:CHUNK_STATE] = hr
        st_ref[pl.ds(r0, SUBLANES), CHUNK_STATE:2 * CHUNK_STATE] = hi
        lr_ = hr[SUBLANES - 1:SUBLANES, :]
        li_ = hi[SUBLANES - 1:SUBLANES, :]
        if chain:
            return lr_, li_
        sr_ref[k, m] = lr_
        si_ref[k, m] = li_
        return carry

    if chain:
        cr, ci = lax.fori_loop(0, n_tiles, tile_step, (cr_ref[m], ci_ref[m]))
        cr_ref[m] = cr
        ci_ref[m] = ci
        sr_ref[0, m] = cr
        si_ref[0, m] = ci
    else:
        lax.fori_loop(0, n_tiles, tile_step, 0)

    y = jnp.dot(st_ref[...].astype(jnp.bfloat16), c_ref[m], preferred_element_type=jnp.float32)
    y = y + d_ref[0] * u
    z_ref[...] = _gelu(y)


def _s5_mixer(x, h0r, h0i, g, d_skip, bmat, cmat_bf16, apr, api, *, chain, seq_len, row0=0):
    nseq = h0r.shape[0]
    T = nseq * seq_len
    rows = S5_ROWS
    blk0 = row0 // rows
    if chain:
        blocks_per_seq = seq_len // rows
        seq_blk = 1
        seq_map = lambda rb, m: (rb // blocks_per_seq, m, 0, 0)
        out_map = lambda rb, m: (rb // blocks_per_seq, 0, 0, 0)
    else:
        assert seq_len == SUBLANES
        blocks_per_seq = 1
        seq_blk = rows // SUBLANES
        seq_map = lambda rb, m: (rb, m, 0, 0)
        out_map = lambda rb, m: (rb, 0, 0, 0)
    grid = (T // rows, N_CHUNKS)
    kern = functools.partial(_s5_kernel, chain=chain, blocks_per_seq=blocks_per_seq)
    st_spec = pl.BlockSpec((seq_blk, 1, 1, CHUNK_STATE), seq_map)
    out_st_spec = pl.BlockSpec((seq_blk, N_CHUNKS, 1, CHUNK_STATE), out_map)
    z, sr, si = pl.pallas_call(
        kern,
        grid=grid,
        in_specs=[
            pl.BlockSpec((rows, D_MODEL), lambda rb, m: (rb + blk0, 0)),
            pl.BlockSpec((1, D_MODEL), lambda rb, m: (0, 0)),
            pl.BlockSpec((1, 1, LANES), lambda rb, m: (m, 0, 0)),
            pl.BlockSpec((N_CHUNKS, 2 * LANES, 2 * CHUNK_STATE), lambda rb, m: (0, 0, 0)),
            pl.BlockSpec((N_CHUNKS, LANES, 2 * CHUNK_STATE), lambda rb, m: (0, 0, 0)),
            pl.BlockSpec((N_CHUNKS, 2 * CHUNK_STATE, LANES), lambda rb, m: (0, 0, 0)),
            pl.BlockSpec((1, SUBLANES, CHUNK_STATE), lambda rb, m: (m, 0, 0)),
            pl.BlockSpec((1, SUBLANES, CHUNK_STATE), lambda rb, m: (m, 0, 0)),
            st_spec, st_spec,
        ],
        out_specs=[
            pl.BlockSpec((rows, LANES), lambda rb, m: (rb, m)),
            out_st_spec, out_st_spec,
        ],
        out_shape=[
            jax.ShapeDtypeStruct((T, D_MODEL), jnp.float32),
            jax.ShapeDtypeStruct((nseq, N_CHUNKS, 1, CHUNK_STATE), jnp.float32),
            jax.ShapeDtypeStruct((nseq, N_CHUNKS, 1, CHUNK_STATE), jnp.float32),
        ],
        scratch_shapes=[
            pltpu.VMEM((N_CHUNKS, rows, LANES), jnp.float32),
            pltpu.VMEM((N_CHUNKS, rows, 2 * LANES), jnp.bfloat16),
            pltpu.VMEM((rows, 2 * CHUNK_STATE), jnp.float32),
            pltpu.VMEM((N_CHUNKS, 1, CHUNK_STATE), jnp.float32),
            pltpu.VMEM((N_CHUNKS, 1, CHUNK_STATE), jnp.float32),
        ],
        compiler_params=pltpu.CompilerParams(
            dimension_semantics=("arbitrary", "arbitrary"), vmem_limit_bytes=VMEM_LIMIT_BYTES),
        name="s5_mixer",
    )(x, g.reshape(1, D_MODEL), d_skip.reshape(N_CHUNKS, 1, LANES), bmat[0], bmat[1], cmat_bf16, apr, api,
      h0r, h0i)
    return z, sr, si


def _state_to_chunks(h):
    return h.reshape(h.shape[0], N_CHUNKS, 1, CHUNK_STATE)


def _chunks_to_state(s):
    return s.reshape(s.shape[0], N_GROUPS, STATE_DIM)


def _glu_kernel(z_ref, res_ref, w_ref, o_ref):
    zz = jnp.dot(z_ref[...].astype(jnp.bfloat16), w_ref[...], preferred_element_type=jnp.float32)
    a = zz[:, :D_MODEL]
    b = zz[:, D_MODEL:]
    o_ref[...] = res_ref[...] + a * (1.0 / (1.0 + jnp.exp(-b)))


def _glu_residual(z, res, w_bf16, res_row0=0):
    T = z.shape[0]
    rows = math.gcd(T, 512)
    blk0 = res_row0 // rows
    return pl.pallas_call(
        _glu_kernel,
        grid=(T // rows,),
        in_specs=[
            pl.BlockSpec((rows, D_MODEL), lambda i: (i, 0)),
            pl.BlockSpec((rows, D_MODEL), lambda i: (i + blk0, 0)),
            pl.BlockSpec((D_MODEL, 2 * D_MODEL), lambda i: (0, 0)),
        ],
        out_specs=pl.BlockSpec((rows, D_MODEL), lambda i: (i, 0)),
        out_shape=jax.ShapeDtypeStruct((T, D_MODEL), jnp.float32),
        compiler_params=pltpu.CompilerParams(
            dimension_semantics=("arbitrary",), vmem_limit_bytes=VMEM_LIMIT_BYTES),
        name="glu_residual",
    )(z, res, w_bf16)


ROUTE_ROWS = 256


def _topk_rows(s, payload):
    n_rows = s.shape[0]
    row = lax.broadcasted_iota(jnp.int32, s.shape, 0)
    vals, picks = [], []
    for _ in range(PEER_TOPK):
        m = jnp.max(s, axis=0, keepdims=True)
        pos = jnp.min(jnp.where(s == m, row, n_rows), axis=0, keepdims=True)
        sel = row == pos
        vals.append(m)
        if payload is None:
            picks.append(pos)
        else:
            picks.append(jnp.max(jnp.where(sel, payload, -1), axis=0, keepdims=True))
        s = jnp.where(sel, -jnp.inf, s)
    return jnp.concatenate(vals, axis=0), jnp.concatenate(picks, axis=0)


def _pair_rows(a0, a1, combine):
    half = PEER_TOPK // 2
    parts = [combine(a0[0:1, :], a1)]
    parts += [combine(a0[i:i + 1, :], a1[0:half, :]) for i in range(1, half)]
    parts.append(combine(a0[half:PEER_TOPK, :], a1[0:1, :]))
    return jnp.concatenate(parts, axis=0)


def _peer_route_kernel(h_ref, g_ref, wq_ref, sk_ref, xn_ref, eid_ref, gate_ref, xb_ref):
    hd = pl.program_id(1)

    @pl.when(hd == 0)
    def _():
        xn = _rmsnorm_rows(h_ref[...], g_ref[...])
        xn_ref[...] = xn
        xb_ref[...] = xn.astype(jnp.bfloat16)

    q = jnp.dot(xb_ref[...], wq_ref[hd], preferred_element_type=jnp.float32)
    sv, si = [], []
    for c in range(2):
        qc = q[:, c * PEER_DHALF:(c + 1) * PEER_DHALF].astype(jnp.bfloat16)
        st = lax.dot_general(sk_ref[hd, c], qc, (((1,), (1,)), ((), ())), preferred_element_type=jnp.float32)
        v, i = _topk_rows(st, None)
        sv.append(v)
        si.append(i)
    cand = _pair_rows(sv[0], sv[1], lambda a, b: a + b)
    cid = _pair_rows(si[0], si[1], lambda a, b: a * N_KEYS + b)
    fv, eid = _topk_rows(cand, cid)
    e = jnp.exp(fv - fv[0:1, :])
    gate_ref[...] = e / jnp.sum(e, axis=0, keepdims=True)
    eid_ref[...] = eid


def _peer_route(h, g, wq_bf16, sk_bf16):
    T = h.shape[0]
    rows = ROUTE_ROWS
    n_sel = PEER_HEADS * PEER_TOPK
    return pl.pallas_call(
        _peer_route_kernel,
        grid=(T // rows, PEER_HEADS),
        in_specs=[
            pl.BlockSpec((rows, D_MODEL), lambda tb, hd: (tb, 0)),
            pl.BlockSpec((1, D_MODEL), lambda tb, hd: (0, 0)),
            pl.BlockSpec((PEER_HEADS, D_MODEL, 2 * PEER_DHALF), lambda tb, hd: (0, 0, 0)),
            pl.BlockSpec((PEER_HEADS, 2, N_KEYS, PEER_DHALF), lambda tb, hd: (0, 0, 0, 0)),
        ],
        out_specs=[
            pl.BlockSpec((rows, D_MODEL), lambda tb, hd: (tb, 0)),
            pl.BlockSpec((PEER_TOPK, rows), lambda tb, hd: (hd, tb)),
            pl.BlockSpec((PEER_TOPK, rows), lambda tb, hd: (hd, tb)),
        ],
        out_shape=[
            jax.ShapeDtypeStruct((T, D_MODEL), jnp.float32),
            jax.ShapeDtypeStruct((n_sel, T), jnp.int32),
            jax.ShapeDtypeStruct((n_sel, T), jnp.float32),
        ],
        scratch_shapes=[pltpu.VMEM((rows, D_MODEL), jnp.bfloat16)],
        compiler_params=pltpu.CompilerParams(
            dimension_semantics=("arbitrary", "arbitrary"), vmem_limit_bytes=VMEM_LIMIT_BYTES),
        name="peer_route",
    )(h, g.reshape(1, D_MODEL), wq_bf16, sk_bf16)


N_SEL = PEER_HEADS * PEER_TOPK
SC_LANES = 16
GATHER_ROWS = PEER_TOPK
GATHERS_PER_TOKEN = N_SEL // GATHER_ROWS
GATHER_BUFS = 4
SC_TOKENS = 8
ACC_STRIDE = SC_LANES + 1


def _sc_gather_stream(wid, n_batches, tab_hbm, stage_srcs, stage_bufs, out_hbm, o_v, bufs, gsem, ssem, osem,
                      compute):
    idx_v = stage_bufs[0]

    def stage_copies(bi, slot):
        base = (wid * n_batches + bi) * SC_TOKENS
        return [pltpu.make_async_copy(src.at[pl.ds(base, SC_TOKENS)], buf.at[slot], ssem.at[slot])
                for src, buf in zip(stage_srcs, stage_bufs)]

    def out_copy(bi, slot):
        base = (wid * n_batches + bi) * SC_TOKENS
        return pltpu.make_async_copy(o_v.at[slot], out_hbm.at[pl.ds(base, SC_TOKENS)], osem.at[slot])

    def start(slot, t, kk, b):
        idx = idx_v[slot, t, pl.ds(kk * GATHER_ROWS, GATHER_ROWS)]
        pltpu.async_copy(tab_hbm.at[idx], bufs.at[b], gsem.at[b])

    def wait(b):
        pltpu.make_async_copy(tab_hbm.at[pl.ds(0, GATHER_ROWS)], bufs.at[b], gsem.at[b]).wait()

    for c in stage_copies(0, 0):
        c.start()
    for c in stage_copies(0, 0):
        c.wait()
    for q in range(GATHER_BUFS - 1):
        start(0, q // GATHERS_PER_TOKEN, q % GATHERS_PER_TOKEN, q % GATHER_BUFS)

    def batch(bi, carry):
        slot = bi % 2
        has_next = bi + 1 < n_batches

        @pl.when(has_next)
        def _():
            for c in stage_copies(bi + 1, 1 - slot):
                c.start()

        @pl.when(bi >= 2)
        def _():
            out_copy(bi, slot).wait()

        def tok(t, carry):
            @pl.when(jnp.logical_and(t == SC_TOKENS - 1, has_next))
            def _():
                for c in stage_copies(bi + 1, 1 - slot):
                    c.wait()

            for kk in range(GATHERS_PER_TOKEN):
                nq = kk + GATHER_BUFS - 1
                nk, nb = nq % GATHERS_PER_TOKEN, nq % GATHER_BUFS
                if nq < GATHERS_PER_TOKEN:
                    start(slot, t, nk, nb)
                else:
                    @pl.when(t + 1 < SC_TOKENS)
                    def _():
                        start(slot, t + 1, nk, nb)

                    @pl.when(jnp.logical_and(t + 1 == SC_TOKENS, has_next))
                    def _():
                        start(1 - slot, 0, nk, nb)

                wait(kk % GATHER_BUFS)
                compute(slot, t, kk, kk % GATHER_BUFS)
            return carry

        lax.fori_loop(0, SC_TOKENS, tok, 0)
        out_copy(bi, slot).start()
        return carry

    lax.fori_loop(0, n_batches, batch, 0)
    if n_batches >= 2:
        out_copy(n_batches - 2, (n_batches - 2) % 2).wait()
    out_copy(n_batches - 1, (n_batches - 1) % 2).wait()


def _sc_mesh_and_batches(n_tokens):
    info = plsc.get_sparse_core_info()
    assert info.num_lanes == SC_LANES
    n_workers = info.num_cores * info.num_subcores
    assert n_tokens % (n_workers * SC_TOKENS) == 0
    mesh = plsc.VectorSubcoreMesh(core_axis_name="c", subcore_axis_name="s")
    return info, mesh, n_tokens // (n_workers * SC_TOKENS)


def _peer_hidden_sc(eid, xn, u_tabs, layer):
    T = eid.shape[0]
    info, mesh, n_batches = _sc_mesh_and_batches(T)

    @functools.partial(
        pl.kernel, mesh=mesh,
        out_type=jax.ShapeDtypeStruct((T, N_SEL), jnp.float32),
        scratch_types=[
            pltpu.VMEM((2, SC_TOKENS, N_SEL), jnp.int32),
            pltpu.VMEM((2, SC_TOKENS, D_MODEL), jnp.float32),
            pltpu.VMEM((2, SC_TOKENS, N_SEL), jnp.float32),
            pltpu.VMEM((GATHER_BUFS, GATHER_ROWS, D_MODEL), jnp.float32),
            pltpu.VMEM((GATHER_ROWS * ACC_STRIDE,), jnp.float32),
            pltpu.SemaphoreType.DMA((GATHER_BUFS,)),
            pltpu.SemaphoreType.DMA((2,)),
            pltpu.SemaphoreType.DMA((2,)),
        ],
        compiler_params=pltpu.CompilerParams(needs_layout_passes=False),
        name="peer_hidden_sc",
    )
    def k(eid_hbm, xn_hbm, u_hbm, out_hbm, idx_v, x_v, o_v, bufs, acc_v, gsem, ssem, osem):
        wid = lax.axis_index("s") * info.num_cores + lax.axis_index("c")
        lane = lax.iota(jnp.int32, SC_LANES)
        zero = jnp.zeros((SC_LANES,), jnp.float32)

        def compute(slot, t, kk, b):
            @plsc.parallel_loop(0, D_MODEL // SC_LANES, carry=(zero,) * GATHER_ROWS)
            def accs(c, accs):
                xc = x_v[slot, t, pl.ds(c * SC_LANES, SC_LANES)]
                return tuple(accs[r] + bufs[b, r, pl.ds(c * SC_LANES, SC_LANES)] * xc for r in range(GATHER_ROWS))

            for r in range(GATHER_ROWS):
                acc_v[pl.ds(r * ACC_STRIDE, SC_LANES)] = accs[r]
            tot = zero
            for c in range(SC_LANES):
                tot = tot + plsc.load_gather(acc_v, [lane * ACC_STRIDE + c])
            o_v[slot, t, pl.ds(kk * GATHER_ROWS, GATHER_ROWS)] = tot

        _sc_gather_stream(wid, n_batches, u_hbm.at[layer], [eid_hbm, xn_hbm], [idx_v, x_v], out_hbm, o_v, bufs,
                          gsem, ssem, osem, compute)

    return k(eid, xn, u_tabs)


def _peer_combine_sc(eid, a, v_tabs, layer):
    T = eid.shape[0]
    info, mesh, n_batches = _sc_mesh_and_batches(T)

    @functools.partial(
        pl.kernel, mesh=mesh,
        out_type=jax.ShapeDtypeStruct((T, D_MODEL), jnp.float32),
        scratch_types=[
            pltpu.VMEM((2, SC_TOKENS, N_SEL), jnp.int32),
            pltpu.VMEM((2, SC_TOKENS, N_SEL), jnp.float32),
            pltpu.VMEM((2, SC_TOKENS, D_MODEL), jnp.float32),
            pltpu.VMEM((GATHER_BUFS, GATHER_ROWS, D_MODEL), jnp.float32),
            pltpu.SemaphoreType.DMA((GATHER_BUFS,)),
            pltpu.SemaphoreType.DMA((2,)),
            pltpu.SemaphoreType.DMA((2,)),
        ],
        compiler_params=pltpu.CompilerParams(needs_layout_passes=False),
        name="peer_combine_sc",
    )
    def k(eid_hbm, a_hbm, v_hbm, out_hbm, idx_v, a_v, o_v, bufs, gsem, ssem, osem):
        wid = lax.axis_index("s") * info.num_cores + lax.axis_index("c")

        def compute(slot, t, kk, b):
            svec = jnp.full((SC_LANES,), slot, jnp.int32)
            tvec = jnp.full((SC_LANES,), t, jnp.int32)
            ws = [plsc.load_gather(a_v, [svec, tvec, jnp.full((SC_LANES,), kk * GATHER_ROWS + r, jnp.int32)])
                  for r in range(GATHER_ROWS)]

            @plsc.parallel_loop(0, D_MODEL // SC_LANES, unroll=2)
            def _(c):
                sl = pl.ds(c * SC_LANES, SC_LANES)
                terms = [ws[r] * bufs[b, r, sl] for r in range(GATHER_ROWS)]
                if kk != 0:
                    terms.append(o_v[slot, t, sl])
                while len(terms) > 1:
                    pairs = [terms[i] + terms[i + 1] for i in range(0, len(terms) - 1, 2)]
                    terms = pairs + ([terms[-1]] if len(terms) % 2 else [])
                o_v[slot, t, sl] = terms[0]

        _sc_gather_stream(wid, n_batches, v_hbm.at[layer], [eid_hbm, a_hbm], [idx_v, a_v], out_hbm, o_v, bufs,
                          gsem, ssem, osem, compute)

    return k(eid, a, v_tabs)


def _peer_act_kernel(hp_ref, gate_ref, a_ref):
    a_ref[...] = _gelu(hp_ref[...]) * gate_ref[...]


ELEMENTWISE_ROWS = 1024


def _peer_act(hpre, gate):
    T = hpre.shape[0]
    rows = ELEMENTWISE_ROWS
    assert T % rows == 0
    spec = pl.BlockSpec((rows, N_SEL), lambda i: (i, 0))
    return pl.pallas_call(
        _peer_act_kernel, grid=(T // rows,), in_specs=[spec, spec], out_specs=spec,
        out_shape=jax.ShapeDtypeStruct((T, N_SEL), jnp.float32),
        compiler_params=pltpu.CompilerParams(dimension_semantics=("arbitrary",)),
        name="peer_act",
    )(hpre, gate)


def _residual_kernel(h_ref, c_ref, g_ref, o_ref, *, final_norm):
    y = h_ref[...] + c_ref[...]
    o_ref[...] = _rmsnorm_rows(y, g_ref[...]) if final_norm else y


def _residual(h, c, gfin, *, final_norm):
    T = h.shape[0]
    rows = ELEMENTWISE_ROWS
    assert T % rows == 0
    spec = pl.BlockSpec((rows, D_MODEL), lambda i: (i, 0))
    return pl.pallas_call(
        functools.partial(_residual_kernel, final_norm=final_norm), grid=(T // rows,),
        in_specs=[spec, spec, pl.BlockSpec((1, D_MODEL), lambda i: (0, 0))], out_specs=spec,
        out_shape=jax.ShapeDtypeStruct((T, D_MODEL), jnp.float32),
        compiler_params=pltpu.CompilerParams(dimension_semantics=("arbitrary",)),
        name="peer_residual",
    )(h, c, gfin.reshape(1, D_MODEL))


def _peer_weights(eid, xn, gate, u_tabs, layer):
    return _peer_act(_peer_hidden_sc(eid, xn, u_tabs, layer), gate)


def _peer_apply(eid, a, h, gfin, v_tabs, layer, *, final_norm):
    return _residual(h, _peer_combine_sc(eid, a, v_tabs, layer), gfin, final_norm=final_norm)


KV_WIDTH = N_KV_HEADS * HEAD_DIM
BF16 = jnp.bfloat16


def _qkv(h, gkv, gq, wkv_ref, wq_ref):
    kv = jnp.dot(_rmsnorm_rows(h, gkv).astype(BF16), wkv_ref[...], preferred_element_type=jnp.float32)
    q = jnp.dot(_rmsnorm_rows(h, gq).astype(BF16), wq_ref[...], preferred_element_type=jnp.float32)
    return kv, q


def _sink_softmax_pv(parts, sink):
    m = sink
    for s, _ in parts:
        m = jnp.maximum(m, jnp.max(s, axis=-1, keepdims=True))
    den = jnp.exp(sink - m)
    acc = None
    for s, v in parts:
        e = jnp.exp(s - m)
        den = den + jnp.sum(e, axis=-1, keepdims=True)
        pv = jnp.dot(e.astype(BF16), v.astype(BF16), preferred_element_type=jnp.float32)
        acc = pv if acc is None else acc + pv
    return acc / den


def _nt_dot(a, b):
    return lax.dot_general(a.astype(BF16), b.astype(BF16), (((1,), (1,)), ((), ())),
                           preferred_element_type=jnp.float32)


def _attn_prompt_kernel(h_ref, gkv_ref, gq_ref, wkv_ref, wq_ref, wo_ref, sink_ref,
                        o_ref, kvw_ref, prev_ref):
    blk = pl.program_id(1)
    h = h_ref[...]
    kv, q = _qkv(h, gkv_ref[...], gq_ref[...], wkv_ref, wq_ref)
    kvw_ref[0] = kv

    @pl.when(blk == 0)
    def _():
        prev_ref[...] = jnp.zeros_like(prev_ref)

    prev = prev_ref[...]
    qi = lax.broadcasted_iota(jnp.int32, (WINDOW, WINDOW), 0)
    kj = lax.broadcasted_iota(jnp.int32, (WINDOW, WINDOW), 1)
    prev_ok = jnp.logical_and(kj > qi, blk > 0)
    cur_ok = kj <= qi
    heads = []
    for kvh in range(N_KV_HEADS):
        ks = slice(kvh * HEAD_DIM, (kvh + 1) * HEAD_DIM)
        vs = slice(KV_WIDTH + kvh * HEAD_DIM, KV_WIDTH + (kvh + 1) * HEAD_DIM)
        for g in range(Q_PER_KV):
            hq = kvh * Q_PER_KV + g
            qh = q[:, hq * HEAD_DIM:(hq + 1) * HEAD_DIM]
            sp = jnp.where(prev_ok, _nt_dot(qh, prev[:, ks]) * ATTN_SCALE, -jnp.inf)
            sc = jnp.where(cur_ok, _nt_dot(qh, kv[:, ks]) * ATTN_SCALE, -jnp.inf)
            heads.append(_sink_softmax_pv([(sp, prev[:, vs]), (sc, kv[:, vs])], sink_ref[hq]))
    o = jnp.concatenate(heads, axis=1)
    o_ref[...] = h + jnp.dot(o.astype(BF16), wo_ref[...], preferred_element_type=jnp.float32)
    prev_ref[...] = kv


def _attn_prompt(h, gkv, gq, wkv, wq, wo, sinks, *, n_seq, seq_len):
    nb = seq_len // WINDOW
    row_spec = pl.BlockSpec((WINDOW, D_MODEL), lambda n, b: (n * nb + b, 0))
    full = lambda shape: pl.BlockSpec(shape, lambda n, b: (0,) * len(shape))
    return pl.pallas_call(
        _attn_prompt_kernel,
        grid=(n_seq, nb),
        in_specs=[
            row_spec, full((1, D_MODEL)), full((1, D_MODEL)),
            full((D_MODEL, 2 * KV_WIDTH)), full((D_MODEL, D_MODEL)), full((D_MODEL, D_MODEL)),
            pl.BlockSpec(memory_space=pltpu.SMEM),
        ],
        out_specs=[row_spec, pl.BlockSpec((1, WINDOW, 2 * KV_WIDTH), lambda n, b: (n, 0, 0))],
        out_shape=[
            jax.ShapeDtypeStruct(h.shape, jnp.float32),
            jax.ShapeDtypeStruct((n_seq, WINDOW, 2 * KV_WIDTH), jnp.float32),
        ],
        scratch_shapes=[pltpu.VMEM((WINDOW, 2 * KV_WIDTH), jnp.float32)],
        compiler_params=pltpu.CompilerParams(
            dimension_semantics=("arbitrary", "arbitrary"), vmem_limit_bytes=VMEM_LIMIT_BYTES),
        name="attn_prompt",
    )(h, gkv.reshape(1, D_MODEL), gq.reshape(1, D_MODEL), wkv, wq, wo, sinks)


ATTN_SEQS = 16
PROMPT_PIECES = 2


def _attn_sample_kernel(h_ref, ck_ref, cv_ref, gkv_ref, gq_ref, wkv_ref, wq_ref, wo_ref, sink_ref,
                        o_ref, kw_ref, vw_ref, kv_ref, q_ref, att_ref, *, n_new):
    h = h_ref[...]
    kv, q = _qkv(h, gkv_ref[...], gq_ref[...], wkv_ref, wq_ref)
    kv_ref[...] = kv
    q_ref[...] = q
    n_seq = h.shape[0] // n_new
    rows = Q_PER_KV * n_new
    qpos_c = lax.broadcasted_iota(jnp.int32, (rows, WINDOW), 0) % n_new
    cache_ok = lax.broadcasted_iota(jnp.int32, (rows, WINDOW), 1) > qpos_c
    qpos_n = lax.broadcasted_iota(jnp.int32, (rows, n_new), 0) % n_new
    new_ok = lax.broadcasted_iota(jnp.int32, (rows, n_new), 1) <= qpos_n

    def per_seq(n, carry):
        r0 = pl.multiple_of(n * n_new, n_new)
        kvn = kv_ref[pl.ds(r0, n_new), :]
        qn = q_ref[pl.ds(r0, n_new), :]
        ck = ck_ref[n]
        cv = cv_ref[n]
        kw_ref[n, 0:WINDOW - n_new, :] = ck[n_new:, :]
        kw_ref[n, WINDOW - n_new:WINDOW, :] = kvn[:, 0:KV_WIDTH]
        vw_ref[n, 0:WINDOW - n_new, :] = cv[n_new:, :]
        vw_ref[n, WINDOW - n_new:WINDOW, :] = kvn[:, KV_WIDTH:2 * KV_WIDTH]
        outs = []
        for kvh in range(N_KV_HEADS):
            ks = slice(kvh * HEAD_DIM, (kvh + 1) * HEAD_DIM)
            vs = slice(KV_WIDTH + kvh * HEAD_DIM, KV_WIDTH + (kvh + 1) * HEAD_DIM)
            qs = jnp.concatenate(
                [qn[:, (kvh * Q_PER_KV + g) * HEAD_DIM:(kvh * Q_PER_KV + g + 1) * HEAD_DIM] for g in range(Q_PER_KV)],
                axis=0)
            s_c = jnp.where(cache_ok, _nt_dot(qs, ck[:, ks]) * ATTN_SCALE, -jnp.inf)
            s_n = jnp.where(new_ok, _nt_dot(qs, kvn[:, ks]) * ATTN_SCALE, -jnp.inf)
            sink = sink_ref[kvh][:, 0:1]
            o = _sink_softmax_pv([(s_c, cv[:, ks]), (s_n, kvn[:, vs])], sink)
            outs += [o[g * n_new:(g + 1) * n_new, :] for g in range(Q_PER_KV)]
        att_ref[pl.ds(r0, n_new), :] = jnp.concatenate(outs, axis=1)
        return carry

    lax.fori_loop(0, n_seq, per_seq, 0)
    o_ref[...] = h + jnp.dot(att_ref[...].astype(BF16), wo_ref[...], preferred_element_type=jnp.float32)


def _attn_sample(h, cache_k, cache_v, gkv, gq, wkv, wq, wo, sinks, *, n_new):
    n_seq = cache_k.shape[0]
    sb = ATTN_SEQS
    rows = sb * n_new
    row_spec = pl.BlockSpec((rows, D_MODEL), lambda i: (i, 0))
    win_spec = pl.BlockSpec((sb, WINDOW, KV_WIDTH), lambda i: (i, 0, 0))
    full = lambda shape: pl.BlockSpec(shape, lambda i: (0,) * len(shape))
    sink_rows = jnp.repeat(sinks.reshape(N_KV_HEADS, Q_PER_KV), n_new, axis=1)[:, :, None]
    sink_rows = jnp.broadcast_to(sink_rows, (N_KV_HEADS, Q_PER_KV * n_new, LANES))
    return pl.pallas_call(
        functools.partial(_attn_sample_kernel, n_new=n_new),
        grid=(n_seq // sb,),
        in_specs=[
            row_spec, win_spec, win_spec, full((1, D_MODEL)), full((1, D_MODEL)),
            full((D_MODEL, 2 * KV_WIDTH)), full((D_MODEL, D_MODEL)), full((D_MODEL, D_MODEL)),
            full((N_KV_HEADS, Q_PER_KV * n_new, LANES)),
        ],
        out_specs=[row_spec, win_spec, win_spec],
        out_shape=[
            jax.ShapeDtypeStruct(h.shape, jnp.float32),
            jax.ShapeDtypeStruct((n_seq, WINDOW, KV_WIDTH), jnp.float32),
            jax.ShapeDtypeStruct((n_seq, WINDOW, KV_WIDTH), jnp.float32),
        ],
        scratch_shapes=[
            pltpu.VMEM((rows, 2 * KV_WIDTH), jnp.float32),
            pltpu.VMEM((rows, D_MODEL), jnp.float32),
            pltpu.VMEM((rows, D_MODEL), jnp.float32),
        ],
        compiler_params=pltpu.CompilerParams(
            dimension_semantics=("arbitrary",), vmem_limit_bytes=VMEM_LIMIT_BYTES),
        name="attn_sample",
    )(h, cache_k, cache_v, gkv.reshape(1, D_MODEL), gq.reshape(1, D_MODEL), wkv, wq, wo, sink_rows)


def kernel(x_prompt, x_sample, state_ssm_re, state_ssm_im, cache_k_win, cache_v_win, norm_mix, norm_ffn, norm_kv, norm_final, ssm_lam_re, ssm_lam_im, ssm_log_dt, ssm_b_re, ssm_b_im, ssm_c_re, ssm_c_im, ssm_d, ssm_w_glu, w_kv, w_q, attn_sinks, w_o, peer_w_q, peer_sub_keys, peer_u, peer_v):
    bmat, cmat, apr, api = _s5_discretize(ssm_lam_re[0], ssm_lam_im[0], ssm_log_dt[0], ssm_b_re[0], ssm_b_im[0], ssm_c_re[0], ssm_c_im[0])
    cmat = cmat.astype(jnp.bfloat16)
    wglu = ssm_w_glu[0].astype(jnp.bfloat16)
    wkv = w_kv.astype(BF16)
    wq = w_q[0].astype(BF16)
    wo = w_o[0].astype(BF16)
    peer_wq = [peer_w_q[layer].astype(BF16).reshape(D_MODEL, PEER_HEADS, 2 * PEER_DHALF).transpose(1, 0, 2)
               for layer in range(2)]
    peer_sk = [peer_sub_keys[layer].astype(BF16) for layer in range(2)]

    def peer_weights(h, layer):
        xn, eid_t, gate_t = _peer_route(h, norm_ffn[layer], peer_wq[layer], peer_sk[layer])
        eid = eid_t.T
        return eid, _peer_weights(eid, xn, gate_t.T, peer_u, layer)

    def peer(h, layer, final_norm):
        eid, a = peer_weights(h, layer)
        return _peer_apply(eid, a, h, norm_final, peer_v, layer, final_norm=final_norm)

    win = lambda a: a.reshape(a.shape[0], WINDOW, N_KV_HEADS, HEAD_DIM)

    seq_len = x_prompt.shape[1]
    xp_all = x_prompt.reshape(-1, D_MODEL)

    def trunk_prompt(seq):
        piece = seq_len // PROMPT_PIECES
        sr = si = jnp.zeros((1, N_CHUNKS, 1, CHUNK_STATE), jnp.float32)
        h1, eid, a = [], [], []
        for p in range(PROMPT_PIECES):
            row0 = seq * seq_len + p * piece
            z, sr, si = _s5_mixer(xp_all, sr, si, norm_mix[0], ssm_d[0], bmat, cmat, apr, api,
                                  chain=True, seq_len=piece, row0=row0)
            h1.append(_glu_residual(z, xp_all, wglu, res_row0=row0))
            eid_p, a_p = peer_weights(h1[-1], 0)
            eid.append(eid_p)
            a.append(a_p)
        cat = lambda parts: jnp.concatenate(parts, axis=0)
        h2 = _peer_apply(cat(eid), cat(a), cat(h1), norm_final, peer_v, 0, final_norm=False)
        h3, kvw = _attn_prompt(h2, norm_kv, norm_mix[1], wkv, wq, wo, attn_sinks[0], n_seq=1, seq_len=seq_len)
        y = peer(h3, 1, True)
        return (y.reshape(1, seq_len, D_MODEL), _chunks_to_state(sr), _chunks_to_state(si),
                win(kvw[:, :, :KV_WIDTH]), win(kvw[:, :, KV_WIDTH:]))

    def trunk_sample(x_s, s_re, s_im, c_k, c_v):
        n_s = x_s.shape[0]
        xs = x_s.reshape(-1, D_MODEL)
        z, sr, si = _s5_mixer(xs, _state_to_chunks(s_re), _state_to_chunks(s_im), norm_mix[0], ssm_d[0],
                              bmat, cmat, apr, api, chain=False, seq_len=x_s.shape[1])
        h2 = peer(_glu_residual(z, xs, wglu), 0, False)
        h3, kw, vw = _attn_sample(h2, c_k.reshape(n_s, WINDOW, KV_WIDTH), c_v.reshape(n_s, WINDOW, KV_WIDTH),
                                  norm_kv, norm_mix[1], wkv, wq, wo, attn_sinks[0], n_new=x_s.shape[1])
        y = peer(h3, 1, True)
        return y.reshape(x_s.shape), _chunks_to_state(sr), _chunks_to_state(si), win(kw), win(vw)

    y_s, sre_s, sim_s, kw_s, vw_s = trunk_sample(x_sample, state_ssm_re[0], state_ssm_im[0], cache_k_win, cache_v_win)
    outs = [trunk_prompt(seq) for seq in range(x_prompt.shape[0])]
    y_p, sre_p, sim_p, kw_p, vw_p = [jnp.concatenate(parts, axis=0) for parts in zip(*outs)]
    return (y_p, y_s, sre_p[None], sim_p[None], kw_p, vw_p, sre_s[None], sim_s[None], kw_s, vw_s)
```

```python
import functools
import math

import jax
import jax.numpy as jnp
from jax import lax
from jax.experimental import pallas as pl
from jax.experimental.pallas import tpu as pltpu
from jax.experimental.pallas import tpu_sc as plsc

D_MODEL = 1024
GROUP_SIZE = 16
N_GROUPS = D_MODEL // GROUP_SIZE
STATE_DIM = 64
HEAD_DIM = 64
N_Q_HEADS = D_MODEL // HEAD_DIM
N_KV_HEADS = N_Q_HEADS // 8
Q_PER_KV = N_Q_HEADS // N_KV_HEADS
WINDOW = 128
ATTN_SCALE = 1.0 / math.sqrt(HEAD_DIM)
PEER_HEADS = 8
N_KEYS = 128
PEER_TOPK = 16
PEER_DHALF = 128
EPS = 1e-5

LANES = 128
SUBLANES = 8
VMEM_LIMIT_BYTES = 56 * 1024 * 1024

GROUPS_PER_CHUNK = LANES // GROUP_SIZE
N_CHUNKS = N_GROUPS // GROUPS_PER_CHUNK
CHUNK_STATE = GROUPS_PER_CHUNK * STATE_DIM
S5_ROWS = 256


def _rmsnorm_rows(x, g):
    r = lax.rsqrt(jnp.mean(x * x, axis=-1, keepdims=True) + EPS)
    return x * r * g


def _gelu(x):
    return 0.5 * x * (1.0 + lax.erf(x * (1.0 / math.sqrt(2.0))))


def _s5_discretize(lam_re, lam_im, log_dt, b_re, b_im, c_re, c_im):
    f32 = jnp.float32
    lr = lam_re.astype(f32)
    li = lam_im.astype(f32)
    dt = jnp.exp(log_dt.astype(f32))[:, None]
    mag = jnp.exp(lr * dt)
    ab_re = mag * jnp.cos(li * dt)
    ab_im = mag * jnp.sin(li * dt)
    den = lr * lr + li * li
    f_re = ((ab_re - 1.0) * lr + ab_im * li) / den
    f_im = (ab_im * lr - (ab_re - 1.0) * li) / den
    br = b_re.astype(f32)
    bi = b_im.astype(f32)
    bb_re = f_re[..., None] * br - f_im[..., None] * bi
    bb_im = f_re[..., None] * bi + f_im[..., None] * br
    eye = jnp.eye(GROUPS_PER_CHUNK, dtype=f32)

    def chunk_rows(v):
        return v.reshape(N_CHUNKS, 1, CHUNK_STATE)

    def in_blocks(bb):
        t = bb.reshape(N_CHUNKS, GROUPS_PER_CHUNK, STATE_DIM, GROUP_SIZE).transpose(0, 1, 3, 2)
        return jnp.einsum('mgjp,gh->mgjhp', t, eye).reshape(N_CHUNKS, LANES, CHUNK_STATE)

    def out_blocks(c):
        t = c.astype(f32).reshape(N_CHUNKS, GROUPS_PER_CHUNK, GROUP_SIZE, STATE_DIM).transpose(0, 1, 3, 2)
        return jnp.einsum('mgpj,gh->mgphj', t, eye).reshape(N_CHUNKS, CHUNK_STATE, LANES)

    bfull = jnp.concatenate([in_blocks(bb_re), in_blocks(bb_im)], axis=2)
    b_hi = bfull.astype(jnp.bfloat16)
    b_lo = (bfull - b_hi.astype(f32)).astype(jnp.bfloat16)
    bmat = (jnp.concatenate([b_hi, b_hi], axis=1), b_lo)
    cmat = jnp.concatenate([out_blocks(c_re), -out_blocks(c_im)], axis=1)
    pr, pi = [ab_re], [ab_im]
    for _ in range(SUBLANES - 1):
        pr, pi = pr + [pr[-1] * ab_re - pi[-1] * ab_im], pi + [pr[-1] * ab_im + pi[-1] * ab_re]
    apr = jnp.concatenate([chunk_rows(v) for v in pr], axis=1)
    api = jnp.concatenate([chunk_rows(v) for v in pi], axis=1)
    return bmat, cmat, apr, api


def _split_bf16(x):
    hi = x.astype(jnp.bfloat16)
    return hi, (x - hi.astype(jnp.float32)).astype(jnp.bfloat16)


def _s5_kernel(x_ref, g_ref, d_ref, bhh_ref, blo_ref, c_ref, apr_ref, api_ref, h0r_ref, h0i_ref,
               z_ref, sr_ref, si_ref, u_ref, us_ref, st_ref, cr_ref, ci_ref, *, chain, blocks_per_seq):
    rb = pl.program_id(0)
    m = pl.program_id(1)
    rows = x_ref.shape[0]
    n_tiles = rows // SUBLANES

    @pl.when(m == 0)
    def _():
        u = _rmsnorm_rows(x_ref[...], g_ref[...])
        for mm in range(N_CHUNKS):
            uc = u[:, mm * LANES:(mm + 1) * LANES]
            u_ref[mm] = uc
            hi, lo = _split_bf16(uc)
            us_ref[mm, :, 0:LANES] = hi
            us_ref[mm, :, LANES:2 * LANES] = lo

    u = u_ref[m]
    us = us_ref[m]
    st_ref[...] = (jnp.dot(us, bhh_ref[m], preferred_element_type=jnp.float32)
                   + jnp.dot(us[:, 0:LANES], blo_ref[m], preferred_element_type=jnp.float32))

    apr = apr_ref[0]
    api = api_ref[0]
    row = lax.broadcasted_iota(jnp.int32, (SUBLANES, CHUNK_STATE), 0)

    if chain:
        @pl.when(rb % blocks_per_seq == 0)
        def _():
            cr_ref[m] = h0r_ref[0, 0]
            ci_ref[m] = h0i_ref[0, 0]

    def tile_step(k, carry):
        r0 = pl.multiple_of(k * SUBLANES, SUBLANES)
        xr = st_ref[pl.ds(r0, SUBLANES), 0:CHUNK_STATE]
        xi = st_ref[pl.ds(r0, SUBLANES), CHUNK_STATE:2 * CHUNK_STATE]
        for d in (1, 2, 4):
            ar = apr[d - 1:d, :]
            ai = api[d - 1:d, :]
            sr = jnp.where(row >= d, pltpu.roll(xr, d, axis=0), 0.0)
            si = jnp.where(row >= d, pltpu.roll(xi, d, axis=0), 0.0)
            xr, xi = xr + ar * sr - ai * si, xi + ar * si + ai * sr
        if chain:
            cr, ci = carry
        else:
            cr = h0r_ref[k, 0]
            ci = h0i_ref[k, 0]
        hr = xr + apr * cr - api * ci
        hi = xi + apr * ci + api * cr
        st_ref[pl.ds(r0, SUBLANES), 0:CHUNK_STATE] = hr
        st_ref[pl.ds(r0, SUBLANES), CHUNK_STATE:2 * CHUNK_STATE] = hi
        lr_ = hr[SUBLANES - 1:SUBLANES, :]
        li_ = hi[SUBLANES - 1:SUBLANES, :]
        if chain:
            return lr_, li_
        sr_ref[k, m] = lr_
        si_ref[k, m] = li_
        return carry

    if chain:
        cr, ci = lax.fori_loop(0, n_tiles, tile_step, (cr_ref[m], ci_ref[m]))
        cr_ref[m] = cr
        ci_ref[m] = ci
        sr_ref[0, m] = cr
        si_ref[0, m] = ci
    else:
        lax.fori_loop(0, n_tiles, tile_step, 0)

    y = jnp.dot(st_ref[...].astype(jnp.bfloat16), c_ref[m], preferred_element_type=jnp.float32)
    y = y + d_ref[0] * u
    z_ref[...] = _gelu(y)


def _s5_mixer(x, h0r, h0i, g, d_skip, bmat, cmat_bf16, apr, api, *, chain, seq_len, row0=0):
    nseq = h0r.shape[0]
    T = nseq * seq_len
    rows = S5_ROWS
    blk0 = row0 // rows
    if chain:
        blocks_per_seq = seq_len // rows
        seq_blk = 1
        seq_map = lambda rb, m: (rb // blocks_per_seq, m, 0, 0)
        out_map = lambda rb, m: (rb // blocks_per_seq, 0, 0, 0)
    else:
        assert seq_len == SUBLANES
        blocks_per_seq = 1
        seq_blk = rows // SUBLANES
        seq_map = lambda rb, m: (rb, m, 0, 0)
        out_map = lambda rb, m: (rb, 0, 0, 0)
    grid = (T // rows, N_CHUNKS)
    kern = functools.partial(_s5_kernel, chain=chain, blocks_per_seq=blocks_per_seq)
    st_spec = pl.BlockSpec((seq_blk, 1, 1, CHUNK_STATE), seq_map)
    out_st_spec = pl.BlockSpec((seq_blk, N_CHUNKS, 1, CHUNK_STATE), out_map)
    z, sr, si = pl.pallas_call(
        kern,
        grid=grid,
        in_specs=[
            pl.BlockSpec((rows, D_MODEL), lambda rb, m: (rb + blk0, 0)),
            pl.BlockSpec((1, D_MODEL), lambda rb, m: (0, 0)),
            pl.BlockSpec((1, 1, LANES), lambda rb, m: (m, 0, 0)),
            pl.BlockSpec((N_CHUNKS, 2 * LANES, 2 * CHUNK_STATE), lambda rb, m: (0, 0, 0)),
            pl.BlockSpec((N_CHUNKS, LANES, 2 * CHUNK_STATE), lambda rb, m: (0, 0, 0)),
            pl.BlockSpec((N_CHUNKS, 2 * CHUNK_STATE, LANES), lambda rb, m: (0, 0, 0)),
            pl.BlockSpec((1, SUBLANES, CHUNK_STATE), lambda rb, m: (m, 0, 0)),
            pl.BlockSpec((1, SUBLANES, CHUNK_STATE), lambda rb, m: (m, 0, 0)),
            st_spec, st_spec,
        ],
        out_specs=[
            pl.BlockSpec((rows, LANES), lambda rb, m: (rb, m)),
            out_st_spec, out_st_spec,
        ],
        out_shape=[
            jax.ShapeDtypeStruct((T, D_MODEL), jnp.float32),
            jax.ShapeDtypeStruct((nseq, N_CHUNKS, 1, CHUNK_STATE), jnp.float32),
            jax.ShapeDtypeStruct((nseq, N_CHUNKS, 1, CHUNK_STATE), jnp.float32),
        ],
        scratch_shapes=[
            pltpu.VMEM((N_CHUNKS, rows, LANES), jnp.float32),
            pltpu.VMEM((N_CHUNKS, rows, 2 * LANES), jnp.bfloat16),
            pltpu.VMEM((rows, 2 * CHUNK_STATE), jnp.float32),
            pltpu.VMEM((N_CHUNKS, 1, CHUNK_STATE), jnp.float32),
            pltpu.VMEM((N_CHUNKS, 1, CHUNK_STATE), jnp.float32),
        ],
        compiler_params=pltpu.CompilerParams(
            dimension_semantics=("arbitrary", "arbitrary"), vmem_limit_bytes=VMEM_LIMIT_BYTES),
        name="s5_mixer",
    )(x, g.reshape(1, D_MODEL), d_skip.reshape(N_CHUNKS, 1, LANES), bmat[0], bmat[1], cmat_bf16, apr, api,
      h0r, h0i)
    return z, sr, si


def _state_to_chunks(h):
    return h.reshape(h.shape[0], N_CHUNKS, 1, CHUNK_STATE)


def _chunks_to_state(s):
    return s.reshape(s.shape[0], N_GROUPS, STATE_DIM)


def _glu_kernel(z_ref, res_ref, w_ref, o_ref):
    zz = jnp.dot(z_ref[...].astype(jnp.bfloat16), w_ref[...], preferred_element_type=jnp.float32)
    a = zz[:, :D_MODEL]
    b = zz[:, D_MODEL:]
    o_ref[...] = res_ref[...] + a * (1.0 / (1.0 + jnp.exp(-b)))


def _glu_residual(z, res, w_bf16, res_row0=0):
    T = z.shape[0]
    rows = math.gcd(T, 512)
    blk0 = res_row0 // rows
    return pl.pallas_call(
        _glu_kernel,
        grid=(T // rows,),
        in_specs=[
            pl.BlockSpec((rows, D_MODEL), lambda i: (i, 0)),
            pl.BlockSpec((rows, D_MODEL), lambda i: (i + blk0, 0)),
            pl.BlockSpec((D_MODEL, 2 * D_MODEL), lambda i: (0, 0)),
        ],
        out_specs=pl.BlockSpec((rows, D_MODEL), lambda i: (i, 0)),
        out_shape=jax.ShapeDtypeStruct((T, D_MODEL), jnp.float32),
        compiler_params=pltpu.CompilerParams(
            dimension_semantics=("arbitrary",), vmem_limit_bytes=VMEM_LIMIT_BYTES),
        name="glu_residual",
    )(z, res, w_bf16)


ROUTE_ROWS = 256


def _topk_rows(s, payload):
    n_rows = s.shape[0]
    row = lax.broadcasted_iota(jnp.int32, s.shape, 0)
    vals, picks = [], []
    for _ in range(PEER_TOPK):
        m = jnp.max(s, axis=0, keepdims=True)
        pos = jnp.min(jnp.where(s == m, row, n_rows), axis=0, keepdims=True)
        sel = row == pos
        vals.append(m)
        if payload is None:
            picks.append(pos)
        else:
            picks.append(jnp.max(jnp.where(sel, payload, -1), axis=0, keepdims=True))
        s = jnp.where(sel, -jnp.inf, s)
    return jnp.concatenate(vals, axis=0), jnp.concatenate(picks, axis=0)


def _pair_rows(a0, a1, combine):
    half = PEER_TOPK // 2
    parts = [combine(a0[0:1, :], a1)]
    parts += [combine(a0[i:i + 1, :], a1[0:half, :]) for i in range(1, half)]
    parts.append(combine(a0[half:PEER_TOPK, :], a1[0:1, :]))
    return jnp.concatenate(parts, axis=0)


def _peer_route_kernel(h_ref, g_ref, wq_ref, sk_ref, xn_ref, eid_ref, gate_ref, xb_ref):
    hd = pl.program_id(1)

    @pl.when(hd == 0)
    def _():
        xn = _rmsnorm_rows(h_ref[...], g_ref[...])
        xn_ref[...] = xn
        xb_ref[...] = xn.astype(jnp.bfloat16)

    q = jnp.dot(xb_ref[...], wq_ref[hd], preferred_element_type=jnp.float32)
    sv, si = [], []
    for c in range(2):
        qc = q[:, c * PEER_DHALF:(c + 1) * PEER_DHALF].astype(jnp.bfloat16)
        st = lax.dot_general(sk_ref[hd, c], qc, (((1,), (1,)), ((), ())), preferred_element_type=jnp.float32)
        v, i = _topk_rows(st, None)
        sv.append(v)
        si.append(i)
    cand = _pair_rows(sv[0], sv[1], lambda a, b: a + b)
    cid = _pair_rows(si[0], si[1], lambda a, b: a * N_KEYS + b)
    fv, eid = _topk_rows(cand, cid)
    e = jnp.exp(fv - fv[0:1, :])
    gate_ref[...] = e / jnp.sum(e, axis=0, keepdims=True)
    eid_ref[...] = eid


def _peer_route(h, g, wq_bf16, sk_bf16):
    T = h.shape[0]
    rows = ROUTE_ROWS
    n_sel = PEER_HEADS * PEER_TOPK
    return pl.pallas_call(
        _peer_route_kernel,
        grid=(T // rows, PEER_HEADS),
        in_specs=[
            pl.BlockSpec((rows, D_MODEL), lambda tb, hd: (tb, 0)),
            pl.BlockSpec((1, D_MODEL), lambda tb, hd: (0, 0)),
            pl.BlockSpec((PEER_HEADS, D_MODEL, 2 * PEER_DHALF), lambda tb, hd: (0, 0, 0)),
            pl.BlockSpec((PEER_HEADS, 2, N_KEYS, PEER_DHALF), lambda tb, hd: (0, 0, 0, 0)),
        ],
        out_specs=[
            pl.BlockSpec((rows, D_MODEL), lambda tb, hd: (tb, 0)),
            pl.BlockSpec((PEER_TOPK, rows), lambda tb, hd: (hd, tb)),
            pl.BlockSpec((PEER_TOPK, rows), lambda tb, hd: (hd, tb)),
        ],
        out_shape=[
            jax.ShapeDtypeStruct((T, D_MODEL), jnp.float32),
            jax.ShapeDtypeStruct((n_sel, T), jnp.int32),
            jax.ShapeDtypeStruct((n_sel, T), jnp.float32),
        ],
        scratch_shapes=[pltpu.VMEM((rows, D_MODEL), jnp.bfloat16)],
        compiler_params=pltpu.CompilerParams(
            dimension_semantics=("arbitrary", "arbitrary"), vmem_limit_bytes=VMEM_LIMIT_BYTES),
        name="peer_route",
    )(h, g.reshape(1, D_MODEL), wq_bf16, sk_bf16)


N_SEL = PEER_HEADS * PEER_TOPK
SC_LANES = 16
GATHER_ROWS = PEER_TOPK
GATHERS_PER_TOKEN = N_SEL // GATHER_ROWS
GATHER_BUFS = 4
SC_TOKENS = 8
ACC_STRIDE = SC_LANES + 1


def _sc_gather_stream(wid, n_batches, tab_hbm, stage_srcs, stage_bufs, out_hbm, o_v, bufs, gsem, ssem, osem,
                      compute):
    idx_v = stage_bufs[0]

    def stage_copies(bi, slot):
        base = (wid * n_batches + bi) * SC_TOKENS
        return [pltpu.make_async_copy(src.at[pl.ds(base, SC_TOKENS)], buf.at[slot], ssem.at[slot])
                for src, buf in zip(stage_srcs, stage_bufs)]

    def out_copy(bi, slot):
        base = (wid * n_batches + bi) * SC_TOKENS
        return pltpu.make_async_copy(o_v.at[slot], out_hbm.at[pl.ds(base, SC_TOKENS)], osem.at[slot])

    def start(slot, t, kk, b):
        idx = idx_v[slot, t, pl.ds(kk * GATHER_ROWS, GATHER_ROWS)]
        pltpu.async_copy(tab_hbm.at[idx], bufs.at[b], gsem.at[b])

    def wait(b):
        pltpu.make_async_copy(tab_hbm.at[pl.ds(0, GATHER_ROWS)], bufs.at[b], gsem.at[b]).wait()

    for c in stage_copies(0, 0):
        c.start()
    for c in stage_copies(0, 0):
        c.wait()
    for q in range(GATHER_BUFS - 1):
        start(0, q // GATHERS_PER_TOKEN, q % GATHERS_PER_TOKEN, q % GATHER_BUFS)

    def batch(bi, carry):
        slot = bi % 2
        has_next = bi + 1 < n_batches

        @pl.when(has_next)
        def _():
            for c in stage_copies(bi + 1, 1 - slot):
                c.start()

        @pl.when(bi >= 2)
        def _():
            out_copy(bi, slot).wait()

        def tok(t, carry):
            @pl.when(jnp.logical_and(t == SC_TOKENS - 1, has_next))
            def _():
                for c in stage_copies(bi + 1, 1 - slot):
                    c.wait()

            for kk in range(GATHERS_PER_TOKEN):
                nq = kk + GATHER_BUFS - 1
                nk, nb = nq % GATHERS_PER_TOKEN, nq % GATHER_BUFS
                if nq < GATHERS_PER_TOKEN:
                    start(slot, t, nk, nb)
                else:
                    @pl.when(t + 1 < SC_TOKENS)
                    def _():
                        start(slot, t + 1, nk, nb)

                    @pl.when(jnp.logical_and(t + 1 == SC_TOKENS, has_next))
                    def _():
                        start(1 - slot, 0, nk, nb)

                wait(kk % GATHER_BUFS)
                compute(slot, t, kk, kk % GATHER_BUFS)
            return carry

        lax.fori_loop(0, SC_TOKENS, tok, 0)
        out_copy(bi, slot).start()
        return carry

    lax.fori_loop(0, n_batches, batch, 0)
    if n_batches >= 2:
        out_copy(n_batches - 2, (n_batches - 2) % 2).wait()
    out_copy(n_batches - 1, (n_batches - 1) % 2).wait()


def _sc_mesh_and_batches(n_tokens):
    info = plsc.get_sparse_core_info()
    assert info.num_lanes == SC_LANES
    n_workers = info.num_cores * info.num_subcores
    assert n_tokens % (n_workers * SC_TOKENS) == 0
    mesh = plsc.VectorSubcoreMesh(core_axis_name="c", subcore_axis_name="s")
    return info, mesh, n_tokens // (n_workers * SC_TOKENS)


def _hidden_compute(x_v, o_v, bufs, acc_v):
    lane = lax.iota(jnp.int32, SC_LANES)
    zero = jnp.zeros((SC_LANES,), jnp.float32)

    def compute(slot, t, kk, b):
        @plsc.parallel_loop(0, D_MODEL // SC_LANES, carry=(zero,) * GATHER_ROWS)
        def accs(c, accs):
            xc = x_v[slot, t, pl.ds(c * SC_LANES, SC_LANES)]
            return tuple(accs[r] + bufs[b, r, pl.ds(c * SC_LANES, SC_LANES)] * xc for r in range(GATHER_ROWS))

        for r in range(GATHER_ROWS):
            acc_v[pl.ds(r * ACC_STRIDE, SC_LANES)] = accs[r]
        tot = zero
        for c in range(SC_LANES):
            tot = tot + plsc.load_gather(acc_v, [lane * ACC_STRIDE + c])
        o_v[slot, t, pl.ds(kk * GATHER_ROWS, GATHER_ROWS)] = tot

    return compute


def _combine_compute(a_v, o_v, bufs):
    def compute(slot, t, kk, b):
        svec = jnp.full((SC_LANES,), slot, jnp.int32)
        tvec = jnp.full((SC_LANES,), t, jnp.int32)
        ws = [plsc.load_gather(a_v, [svec, tvec, jnp.full((SC_LANES,), kk * GATHER_ROWS + r, jnp.int32)])
              for r in range(GATHER_ROWS)]

        @plsc.parallel_loop(0, D_MODEL // SC_LANES, unroll=2)
        def _(c):
            sl = pl.ds(c * SC_LANES, SC_LANES)
            terms = [ws[r] * bufs[b, r, sl] for r in range(GATHER_ROWS)]
            if kk != 0:
                terms.append(o_v[slot, t, sl])
            while len(terms) > 1:
                pairs = [terms[i] + terms[i + 1] for i in range(0, len(terms) - 1, 2)]
                terms = pairs + ([terms[-1]] if len(terms) % 2 else [])
            o_v[slot, t, sl] = terms[0]

    return compute


def _peer_step_sc(eid_c, a, layer_c, eid_h, xn, layer_h, u_tabs, v_tabs):
    Tc, Th = eid_c.shape[0], eid_h.shape[0]
    info, mesh, nb_c = _sc_mesh_and_batches(Tc)
    _, _, nb_h = _sc_mesh_and_batches(Th)

    @functools.partial(
        pl.kernel, mesh=mesh,
        out_type=(jax.ShapeDtypeStruct((Tc, D_MODEL), jnp.float32), jax.ShapeDtypeStruct((Th, N_SEL), jnp.float32)),
        scratch_types=[
            pltpu.VMEM((2, SC_TOKENS, N_SEL), jnp.int32),
            pltpu.VMEM((2, SC_TOKENS, N_SEL), jnp.float32),
            pltpu.VMEM((2, SC_TOKENS, D_MODEL), jnp.float32),
            pltpu.VMEM((2, SC_TOKENS, N_SEL), jnp.int32),
            pltpu.VMEM((2, SC_TOKENS, D_MODEL), jnp.float32),
            pltpu.VMEM((2, SC_TOKENS, N_SEL), jnp.float32),
            pltpu.VMEM((GATHER_BUFS, GATHER_ROWS, D_MODEL), jnp.float32),
            pltpu.VMEM((GATHER_ROWS * ACC_STRIDE,), jnp.float32),
            pltpu.SemaphoreType.DMA((GATHER_BUFS,)),
            pltpu.SemaphoreType.DMA((2,)),
            pltpu.SemaphoreType.DMA((2,)),
        ],
        compiler_params=pltpu.CompilerParams(needs_layout_passes=False),
        name="peer_step_sc",
    )
    def k(eidc_hbm, a_hbm, eidh_hbm, xn_hbm, u_hbm, v_hbm, c_hbm, hp_hbm,
          idxc_v, a_v, oc_v, idxh_v, x_v, oh_v, bufs, acc_v, gsem, ssem, osem):
        wid = lax.axis_index("s") * info.num_cores + lax.axis_index("c")
        _sc_gather_stream(wid, nb_c, v_hbm.at[layer_c], [eidc_hbm, a_hbm], [idxc_v, a_v], c_hbm, oc_v, bufs,
                          gsem, ssem, osem, _combine_compute(a_v, oc_v, bufs))
        _sc_gather_stream(wid, nb_h, u_hbm.at[layer_h], [eidh_hbm, xn_hbm], [idxh_v, x_v], hp_hbm, oh_v, bufs,
                          gsem, ssem, osem, _hidden_compute(x_v, oh_v, bufs, acc_v))

    return k(eid_c, a, eid_h, xn, u_tabs, v_tabs)


def _peer_hidden_sc(eid, xn, u_tabs, layer):
    T = eid.shape[0]
    info, mesh, n_batches = _sc_mesh_and_batches(T)

    @functools.partial(
        pl.kernel, mesh=mesh,
        out_type=jax.ShapeDtypeStruct((T, N_SEL), jnp.float32),
        scratch_types=[
            pltpu.VMEM((2, SC_TOKENS, N_SEL), jnp.int32),
            pltpu.VMEM((2, SC_TOKENS, D_MODEL), jnp.float32),
            pltpu.VMEM((2, SC_TOKENS, N_SEL), jnp.float32),
            pltpu.VMEM((GATHER_BUFS, GATHER_ROWS, D_MODEL), jnp.float32),
            pltpu.VMEM((GATHER_ROWS * ACC_STRIDE,), jnp.float32),
            pltpu.SemaphoreType.DMA((GATHER_BUFS,)),
            pltpu.SemaphoreType.DMA((2,)),
            pltpu.SemaphoreType.DMA((2,)),
        ],
        compiler_params=pltpu.CompilerParams(needs_layout_passes=False),
        name="peer_hidden_sc",
    )
    def k(eid_hbm, xn_hbm, u_hbm, out_hbm, idx_v, x_v, o_v, bufs, acc_v, gsem, ssem, osem):
        wid = lax.axis_index("s") * info.num_cores + lax.axis_index("c")
        _sc_gather_stream(wid, n_batches, u_hbm.at[layer], [eid_hbm, xn_hbm], [idx_v, x_v], out_hbm, o_v, bufs,
                          gsem, ssem, osem, _hidden_compute(x_v, o_v, bufs, acc_v))

    return k(eid, xn, u_tabs)


def _peer_combine_sc(eid, a, v_tabs, layer):
    T = eid.shape[0]
    info, mesh, n_batches = _sc_mesh_and_batches(T)

    @functools.partial(
        pl.kernel, mesh=mesh,
        out_type=jax.ShapeDtypeStruct((T, D_MODEL), jnp.float32),
        scratch_types=[
            pltpu.VMEM((2, SC_TOKENS, N_SEL), jnp.int32),
            pltpu.VMEM((2, SC_TOKENS, N_SEL), jnp.float32),
            pltpu.VMEM((2, SC_TOKENS, D_MODEL), jnp.float32),
            pltpu.VMEM((GATHER_BUFS, GATHER_ROWS, D_MODEL), jnp.float32),
            pltpu.SemaphoreType.DMA((GATHER_BUFS,)),
            pltpu.SemaphoreType.DMA((2,)),
            pltpu.SemaphoreType.DMA((2,)),
        ],
        compiler_params=pltpu.CompilerParams(needs_layout_passes=False),
        name="peer_combine_sc",
    )
    def k(eid_hbm, a_hbm, v_hbm, out_hbm, idx_v, a_v, o_v, bufs, gsem, ssem, osem):
        wid = lax.axis_index("s") * info.num_cores + lax.axis_index("c")
        _sc_gather_stream(wid, n_batches, v_hbm.at[layer], [eid_hbm, a_hbm], [idx_v, a_v], out_hbm, o_v, bufs,
                          gsem, ssem, osem, _combine_compute(a_v, o_v, bufs))

    return k(eid, a, v_tabs)


def _peer_act_kernel(hp_ref, gate_ref, a_ref):
    a_ref[...] = _gelu(hp_ref[...]) * gate_ref[...]


ELEMENTWISE_ROWS = 1024


def _peer_act(hpre, gate):
    T = hpre.shape[0]
    rows = ELEMENTWISE_ROWS
    assert T % rows == 0
    spec = pl.BlockSpec((rows, N_SEL), lambda i: (i, 0))
    return pl.pallas_call(
        _peer_act_kernel, grid=(T // rows,), in_specs=[spec, spec], out_specs=spec,
        out_shape=jax.ShapeDtypeStruct((T, N_SEL), jnp.float32),
        compiler_params=pltpu.CompilerParams(dimension_semantics=("arbitrary",)),
        name="peer_act",
    )(hpre, gate)


def _residual_kernel(h_ref, c_ref, g_ref, o_ref, *, final_norm):
    y = h_ref[...] + c_ref[...]
    o_ref[...] = _rmsnorm_rows(y, g_ref[...]) if final_norm else y


def _residual(h, c, gfin, *, final_norm):
    T = h.shape[0]
    rows = ELEMENTWISE_ROWS
    assert T % rows == 0
    spec = pl.BlockSpec((rows, D_MODEL), lambda i: (i, 0))
    return pl.pallas_call(
        functools.partial(_residual_kernel, final_norm=final_norm), grid=(T // rows,),
        in_specs=[spec, spec, pl.BlockSpec((1, D_MODEL), lambda i: (0, 0))], out_specs=spec,
        out_shape=jax.ShapeDtypeStruct((T, D_MODEL), jnp.float32),
        compiler_params=pltpu.CompilerParams(dimension_semantics=("arbitrary",)),
        name="peer_residual",
    )(h, c, gfin.reshape(1, D_MODEL))


KV_WIDTH = N_KV_HEADS * HEAD_DIM
BF16 = jnp.bfloat16


def _qkv(h, gkv, gq, wkv_ref, wq_ref):
    kv = jnp.dot(_rmsnorm_rows(h, gkv).astype(BF16), wkv_ref[...], preferred_element_type=jnp.float32)
    q = jnp.dot(_rmsnorm_rows(h, gq).astype(BF16), wq_ref[...], preferred_element_type=jnp.float32)
    return kv, q


def _sink_softmax_pv(parts, sink):
    m = sink
    for s, _ in parts:
        m = jnp.maximum(m, jnp.max(s, axis=-1, keepdims=True))
    den = jnp.exp(sink - m)
    acc = None
    for s, v in parts:
        e = jnp.exp(s - m)
        den = den + jnp.sum(e, axis=-1, keepdims=True)
        pv = jnp.dot(e.astype(BF16), v.astype(BF16), preferred_element_type=jnp.float32)
        acc = pv if acc is None else acc + pv
    return acc / den


def _nt_dot(a, b):
    return lax.dot_general(a.astype(BF16), b.astype(BF16), (((1,), (1,)), ((), ())),
                           preferred_element_type=jnp.float32)


def _attn_prompt_kernel(h_ref, gkv_ref, gq_ref, wkv_ref, wq_ref, wo_ref, sink_ref,
                        o_ref, kvw_ref, prev_ref):
    blk = pl.program_id(1)
    h = h_ref[...]
    kv, q = _qkv(h, gkv_ref[...], gq_ref[...], wkv_ref, wq_ref)
    kvw_ref[0] = kv

    @pl.when(blk == 0)
    def _():
        prev_ref[...] = jnp.zeros_like(prev_ref)

    prev = prev_ref[...]
    qi = lax.broadcasted_iota(jnp.int32, (WINDOW, WINDOW), 0)
    kj = lax.broadcasted_iota(jnp.int32, (WINDOW, WINDOW), 1)
    prev_ok = jnp.logical_and(kj > qi, blk > 0)
    cur_ok = kj <= qi
    heads = []
    for kvh in range(N_KV_HEADS):
        ks = slice(kvh * HEAD_DIM, (kvh + 1) * HEAD_DIM)
        vs = slice(KV_WIDTH + kvh * HEAD_DIM, KV_WIDTH + (kvh + 1) * HEAD_DIM)
        for g in range(Q_PER_KV):
            hq = kvh * Q_PER_KV + g
            qh = q[:, hq * HEAD_DIM:(hq + 1) * HEAD_DIM]
            sp = jnp.where(prev_ok, _nt_dot(qh, prev[:, ks]) * ATTN_SCALE, -jnp.inf)
            sc = jnp.where(cur_ok, _nt_dot(qh, kv[:, ks]) * ATTN_SCALE, -jnp.inf)
            heads.append(_sink_softmax_pv([(sp, prev[:, vs]), (sc, kv[:, vs])], sink_ref[hq]))
    o = jnp.concatenate(heads, axis=1)
    o_ref[...] = h + jnp.dot(o.astype(BF16), wo_ref[...], preferred_element_type=jnp.float32)
    prev_ref[...] = kv


def _attn_prompt(h, gkv, gq, wkv, wq, wo, sinks, *, n_seq, seq_len):
    nb = seq_len // WINDOW
    row_spec = pl.BlockSpec((WINDOW, D_MODEL), lambda n, b: (n * nb + b, 0))
    full = lambda shape: pl.BlockSpec(shape, lambda n, b: (0,) * len(shape))
    return pl.pallas_call(
        _attn_prompt_kernel,
        grid=(n_seq, nb),
        in_specs=[
            row_spec, full((1, D_MODEL)), full((1, D_MODEL)),
            full((D_MODEL, 2 * KV_WIDTH)), full((D_MODEL, D_MODEL)), full((D_MODEL, D_MODEL)),
            pl.BlockSpec(memory_space=pltpu.SMEM),
        ],
        out_specs=[row_spec, pl.BlockSpec((1, WINDOW, 2 * KV_WIDTH), lambda n, b: (n, 0, 0))],
        out_shape=[
            jax.ShapeDtypeStruct(h.shape, jnp.float32),
            jax.ShapeDtypeStruct((n_seq, WINDOW, 2 * KV_WIDTH), jnp.float32),
        ],
        scratch_shapes=[pltpu.VMEM((WINDOW, 2 * KV_WIDTH), jnp.float32)],
        compiler_params=pltpu.CompilerParams(
            dimension_semantics=("arbitrary", "arbitrary"), vmem_limit_bytes=VMEM_LIMIT_BYTES),
        name="attn_prompt",
    )(h, gkv.reshape(1, D_MODEL), gq.reshape(1, D_MODEL), wkv, wq, wo, sinks)


ATTN_SEQS = 16
PROMPT_PIECES = 2


def _attn_sample_kernel(h_ref, ck_ref, cv_ref, gkv_ref, gq_ref, wkv_ref, wq_ref, wo_ref, sink_ref,
                        o_ref, kw_ref, vw_ref, kv_ref, q_ref, att_ref, *, n_new):
    h = h_ref[...]
    kv, q = _qkv(h, gkv_ref[...], gq_ref[...], wkv_ref, wq_ref)
    kv_ref[...] = kv
    q_ref[...] = q
    n_seq = h.shape[0] // n_new
    rows = Q_PER_KV * n_new
    qpos_c = lax.broadcasted_iota(jnp.int32, (rows, WINDOW), 0) % n_new
    cache_ok = lax.broadcasted_iota(jnp.int32, (rows, WINDOW), 1) > qpos_c
    qpos_n = lax.broadcasted_iota(jnp.int32, (rows, n_new), 0) % n_new
    new_ok = lax.broadcasted_iota(jnp.int32, (rows, n_new), 1) <= qpos_n

    def per_seq(n, carry):
        r0 = pl.multiple_of(n * n_new, n_new)
        kvn = kv_ref[pl.ds(r0, n_new), :]
        qn = q_ref[pl.ds(r0, n_new), :]
        ck = ck_ref[n]
        cv = cv_ref[n]
        kw_ref[n, 0:WINDOW - n_new, :] = ck[n_new:, :]
        kw_ref[n, WINDOW - n_new:WINDOW, :] = kvn[:, 0:KV_WIDTH]
        vw_ref[n, 0:WINDOW - n_new, :] = cv[n_new:, :]
        vw_ref[n, WINDOW - n_new:WINDOW, :] = kvn[:, KV_WIDTH:2 * KV_WIDTH]
        outs = []
        for kvh in range(N_KV_HEADS):
            ks = slice(kvh * HEAD_DIM, (kvh + 1) * HEAD_DIM)
            vs = slice(KV_WIDTH + kvh * HEAD_DIM, KV_WIDTH + (kvh + 1) * HEAD_DIM)
            qs = jnp.concatenate(
                [qn[:, (kvh * Q_PER_KV + g) * HEAD_DIM:(kvh * Q_PER_KV + g + 1) * HEAD_DIM] for g in range(Q_PER_KV)],
                axis=0)
            s_c = jnp.where(cache_ok, _nt_dot(qs, ck[:, ks]) * ATTN_SCALE, -jnp.inf)
            s_n = jnp.where(new_ok, _nt_dot(qs, kvn[:, ks]) * ATTN_SCALE, -jnp.inf)
            sink = sink_ref[kvh][:, 0:1]
            o = _sink_softmax_pv([(s_c, cv[:, ks]), (s_n, kvn[:, vs])], sink)
            outs += [o[g * n_new:(g + 1) * n_new, :] for g in range(Q_PER_KV)]
        att_ref[pl.ds(r0, n_new), :] = jnp.concatenate(outs, axis=1)
        return carry

    lax.fori_loop(0, n_seq, per_seq, 0)
    o_ref[...] = h + jnp.dot(att_ref[...].astype(BF16), wo_ref[...], preferred_element_type=jnp.float32)


def _attn_sample(h, cache_k, cache_v, gkv, gq, wkv, wq, wo, sinks, *, n_new):
    n_seq = cache_k.shape[0]
    sb = ATTN_SEQS
    rows = sb * n_new
    row_spec = pl.BlockSpec((rows, D_MODEL), lambda i: (i, 0))
    win_spec = pl.BlockSpec((sb, WINDOW, KV_WIDTH), lambda i: (i, 0, 0))
    full = lambda shape: pl.BlockSpec(shape, lambda i: (0,) * len(shape))
    sink_rows = jnp.repeat(sinks.reshape(N_KV_HEADS, Q_PER_KV), n_new, axis=1)[:, :, None]
    sink_rows = jnp.broadcast_to(sink_rows, (N_KV_HEADS, Q_PER_KV * n_new, LANES))
    return pl.pallas_call(
        functools.partial(_attn_sample_kernel, n_new=n_new),
        grid=(n_seq // sb,),
        in_specs=[
            row_spec, win_spec, win_spec, full((1, D_MODEL)), full((1, D_MODEL)),
            full((D_MODEL, 2 * KV_WIDTH)), full((D_MODEL, D_MODEL)), full((D_MODEL, D_MODEL)),
            full((N_KV_HEADS, Q_PER_KV * n_new, LANES)),
        ],
        out_specs=[row_spec, win_spec, win_spec],
        out_shape=[
            jax.ShapeDtypeStruct(h.shape, jnp.float32),
            jax.ShapeDtypeStruct((n_seq, WINDOW, KV_WIDTH), jnp.float32),
            jax.ShapeDtypeStruct((n_seq, WINDOW, KV_WIDTH), jnp.float32),
        ],
        scratch_shapes=[
            pltpu.VMEM((rows, 2 * KV_WIDTH), jnp.float32),
            pltpu.VMEM((rows, D_MODEL), jnp.float32),
            pltpu.VMEM((rows, D_MODEL), jnp.float32),
        ],
        compiler_params=pltpu.CompilerParams(
            dimension_semantics=("arbitrary",), vmem_limit_bytes=VMEM_LIMIT_BYTES),
        name="attn_sample",
    )(h, cache_k, cache_v, gkv.reshape(1, D_MODEL), gq.reshape(1, D_MODEL), wkv, wq, wo, sink_rows)


def kernel(x_prompt, x_sample, state_ssm_re, state_ssm_im, cache_k_win, cache_v_win, norm_mix, norm_ffn, norm_kv, norm_final, ssm_lam_re, ssm_lam_im, ssm_log_dt, ssm_b_re, ssm_b_im, ssm_c_re, ssm_c_im, ssm_d, ssm_w_glu, w_kv, w_q, attn_sinks, w_o, peer_w_q, peer_sub_keys, peer_u, peer_v):
    bmat, cmat, apr, api = _s5_discretize(ssm_lam_re[0], ssm_lam_im[0], ssm_log_dt[0], ssm_b_re[0], ssm_b_im[0], ssm_c_re[0], ssm_c_im[0])
    cmat = cmat.astype(jnp.bfloat16)
    wglu = ssm_w_glu[0].astype(jnp.bfloat16)
    wkv = w_kv.astype(BF16)
    wq = w_q[0].astype(BF16)
    wo = w_o[0].astype(BF16)
    peer_wq = [peer_w_q[layer].astype(BF16).reshape(D_MODEL, PEER_HEADS, 2 * PEER_DHALF).transpose(1, 0, 2)
               for layer in range(2)]
    peer_sk = [peer_sub_keys[layer].astype(BF16) for layer in range(2)]

    def route(h, layer):
        xn, eid_t, gate_t = _peer_route(h, norm_ffn[layer], peer_wq[layer], peer_sk[layer])
        return dict(h=h, xn=xn, eid=eid_t.T, gate=gate_t.T, layer=layer)

    win = lambda a: a.reshape(a.shape[0], WINDOW, N_KV_HEADS, HEAD_DIM)

    seq_len = x_prompt.shape[1]
    xp_all = x_prompt.reshape(-1, D_MODEL)

    n_prompt, n_s = x_prompt.shape[0], x_sample.shape[0]
    piece = seq_len // PROMPT_PIECES
    xs = x_sample.reshape(-1, D_MODEL)
    z, sre_s, sim_s = _s5_mixer(xs, _state_to_chunks(state_ssm_re[0]), _state_to_chunks(state_ssm_im[0]), norm_mix[0],
                                ssm_d[0], bmat, cmat, apr, api, chain=False, seq_len=x_sample.shape[1])
    units = [route(_glu_residual(z, xs, wglu), 0)]
    prompt_states = []
    for seq in range(n_prompt):
        sr = si = jnp.zeros((1, N_CHUNKS, 1, CHUNK_STATE), jnp.float32)
        for p in range(PROMPT_PIECES):
            row0 = seq * seq_len + p * piece
            z, sr, si = _s5_mixer(xp_all, sr, si, norm_mix[0], ssm_d[0], bmat, cmat, apr, api,
                                  chain=True, seq_len=piece, row0=row0)
            units.append(route(_glu_residual(z, xp_all, wglu, res_row0=row0), 0))
        prompt_states.append((sr, si))
    n_layer0 = len(units)
    windows = {}

    def layer1_unit(g):
        mine = units[0:1] if g == 0 else units[1 + (g - 1) * PROMPT_PIECES:1 + g * PROMPT_PIECES]
        h2 = jnp.concatenate([_residual(u['h'], u['c'], norm_final, final_norm=False) for u in mine], axis=0)
        if g == 0:
            h3, kw, vw = _attn_sample(h2, cache_k_win.reshape(n_s, WINDOW, KV_WIDTH),
                                      cache_v_win.reshape(n_s, WINDOW, KV_WIDTH), norm_kv, norm_mix[1],
                                      wkv, wq, wo, attn_sinks[0], n_new=x_sample.shape[1])
            windows[g] = (win(kw), win(vw))
        else:
            h3, kvw = _attn_prompt(h2, norm_kv, norm_mix[1], wkv, wq, wo, attn_sinks[0], n_seq=1, seq_len=seq_len)
            windows[g] = (win(kvw[:, :, :KV_WIDTH]), win(kvw[:, :, KV_WIDTH:]))
        return route(h3, 1)

    n_units = n_layer0 + 1 + n_prompt
    for k in range(n_units + 2):
        if n_layer0 <= k < n_units:
            units.append(layer1_unit(k - n_layer0))
        hid = units[k] if k < n_units else None
        comb = units[k - 2] if k >= 2 else None
        if comb is not None:
            comb['a'] = _peer_act(comb['hpre'], comb['gate'])
        if hid is not None and comb is not None:
            comb['c'], hid['hpre'] = _peer_step_sc(comb['eid'], comb['a'], comb['layer'],
                                                   hid['eid'], hid['xn'], hid['layer'], peer_u, peer_v)
        elif hid is not None:
            hid['hpre'] = _peer_hidden_sc(hid['eid'], hid['xn'], peer_u, hid['layer'])
        else:
            comb['c'] = _peer_combine_sc(comb['eid'], comb['a'], peer_v, comb['layer'])

    ys = [_residual(u['h'], u['c'], norm_final, final_norm=True) for u in units[n_layer0:]]
    y_s = ys[0].reshape(x_sample.shape)
    y_p = jnp.concatenate(ys[1:], axis=0).reshape(x_prompt.shape)
    cat = lambda parts: jnp.concatenate(parts, axis=0)
    sre_p = cat([_chunks_to_state(sr) for sr, _ in prompt_states])
    sim_p = cat([_chunks_to_state(si) for _, si in prompt_states])
    kw_p = cat([windows[g][0] for g in range(1, n_prompt + 1)])
    vw_p = cat([windows[g][1] for g in range(1, n_prompt + 1)])
    return (y_p, y_s, sre_p[None], sim_p[None], kw_p, vw_p,
            _chunks_to_state(sre_s)[None], _chunks_to_state(sim_s)[None], windows[0][0], windows[0][1])
```

```python
import functools
import math

import jax
import jax.numpy as jnp
from jax import lax
from jax.experimental import pallas as pl
from jax.experimental.pallas import tpu as pltpu
from jax.experimental.pallas import tpu_sc as plsc

D_MODEL = 1024
GROUP_SIZE = 16
N_GROUPS = D_MODEL // GROUP_SIZE
STATE_DIM = 64
HEAD_DIM = 64
N_Q_HEADS = D_MODEL // HEAD_DIM
N_KV_HEADS = N_Q_HEADS // 8
Q_PER_KV = N_Q_HEADS // N_KV_HEADS
WINDOW = 128
ATTN_SCALE = 1.0 / math.sqrt(HEAD_DIM)
PEER_HEADS = 8
N_KEYS = 128
PEER_TOPK = 16
PEER_DHALF = 128
EPS = 1e-5

LANES = 128
SUBLANES = 8
VMEM_LIMIT_BYTES = 56 * 1024 * 1024

GROUPS_PER_CHUNK = LANES // GROUP_SIZE
N_CHUNKS = N_GROUPS // GROUPS_PER_CHUNK
CHUNK_STATE = GROUPS_PER_CHUNK * STATE_DIM
S5_ROWS = 256


def _rmsnorm_rows(x, g):
    r = lax.rsqrt(jnp.mean(x * x, axis=-1, keepdims=True) + EPS)
    return x * r * g


def _gelu(x):
    return 0.5 * x * (1.0 + lax.erf(x * (1.0 / math.sqrt(2.0))))


def _s5_discretize(lam_re, lam_im, log_dt, b_re, b_im, c_re, c_im):
    f32 = jnp.float32
    lr = lam_re.astype(f32)
    li = lam_im.astype(f32)
    dt = jnp.exp(log_dt.astype(f32))[:, None]
    mag = jnp.exp(lr * dt)
    ab_re = mag * jnp.cos(li * dt)
    ab_im = mag * jnp.sin(li * dt)
    den = lr * lr + li * li
    f_re = ((ab_re - 1.0) * lr + ab_im * li) / den
    f_im = (ab_im * lr - (ab_re - 1.0) * li) / den
    br = b_re.astype(f32)
    bi = b_im.astype(f32)
    bb_re = f_re[..., None] * br - f_im[..., None] * bi
    bb_im = f_re[..., None] * bi + f_im[..., None] * br
    eye = jnp.eye(GROUPS_PER_CHUNK, dtype=f32)

    def chunk_rows(v):
        return v.reshape(N_CHUNKS, 1, CHUNK_STATE)

    def in_blocks(bb):
        t = bb.reshape(N_CHUNKS, GROUPS_PER_CHUNK, STATE_DIM, GROUP_SIZE).transpose(0, 1, 3, 2)
        return jnp.einsum('mgjp,gh->mgjhp', t, eye).reshape(N_CHUNKS, LANES, CHUNK_STATE)

    def out_blocks(c):
        t = c.astype(f32).reshape(N_CHUNKS, GROUPS_PER_CHUNK, GROUP_SIZE, STATE_DIM).transpose(0, 1, 3, 2)
        return jnp.einsum('mgpj,gh->mgphj', t, eye).reshape(N_CHUNKS, CHUNK_STATE, LANES)

    bfull = jnp.concatenate([in_blocks(bb_re), in_blocks(bb_im)], axis=2)
    b_hi = bfull.astype(jnp.bfloat16)
    b_lo = (bfull - b_hi.astype(f32)).astype(jnp.bfloat16)
    bmat = (jnp.concatenate([b_hi, b_hi], axis=1), b_lo)
    cmat = jnp.concatenate([out_blocks(c_re), -out_blocks(c_im)], axis=1)
    pr, pi = [ab_re], [ab_im]
    for _ in range(SUBLANES - 1):
        pr, pi = pr + [pr[-1] * ab_re - pi[-1] * ab_im], pi + [pr[-1] * ab_im + pi[-1] * ab_re]
    apr = jnp.concatenate([chunk_rows(v) for v in pr], axis=1)
    api = jnp.concatenate([chunk_rows(v) for v in pi], axis=1)
    return bmat, cmat, apr, api


def _split_bf16(x):
    hi = x.astype(jnp.bfloat16)
    return hi, (x - hi.astype(jnp.float32)).astype(jnp.bfloat16)


def _s5_kernel(x_ref, g_ref, d_ref, bhh_ref, blo_ref, c_ref, apr_ref, api_ref, h0r_ref, h0i_ref,
               z_ref, sr_ref, si_ref, u_ref, us_ref, st_ref, cr_ref, ci_ref, *, chain, blocks_per_seq):
    rb = pl.program_id(0)
    m = pl.program_id(1)
    rows = x_ref.shape[0]
    n_tiles = rows // SUBLANES

    @pl.when(m == 0)
    def _():
        u = _rmsnorm_rows(x_ref[...], g_ref[...])
        for mm in range(N_CHUNKS):
            uc = u[:, mm * LANES:(mm + 1) * LANES]
            u_ref[mm] = uc
            hi, lo = _split_bf16(uc)
            us_ref[mm, :, 0:LANES] = hi
            us_ref[mm, :, LANES:2 * LANES] = lo

    u = u_ref[m]
    us = us_ref[m]
    st_ref[...] = (jnp.dot(us, bhh_ref[m], preferred_element_type=jnp.float32)
                   + jnp.dot(us[:, 0:LANES], blo_ref[m], preferred_element_type=jnp.float32))

    apr = apr_ref[0]
    api = api_ref[0]
    row = lax.broadcasted_iota(jnp.int32, (SUBLANES, CHUNK_STATE), 0)

    if chain:
        @pl.when(rb % blocks_per_seq == 0)
        def _():
            cr_ref[m] = h0r_ref[0, 0]
            ci_ref[m] = h0i_ref[0, 0]

    def tile_step(k, carry):
        r0 = pl.multiple_of(k * SUBLANES, SUBLANES)
        xr = st_ref[pl.ds(r0, SUBLANES), 0:CHUNK_STATE]
        xi = st_ref[pl.ds(r0, SUBLANES), CHUNK_STATE:2 * CHUNK_STATE]
        for d in (1, 2, 4):
            ar = apr[d - 1:d, :]
            ai = api[d - 1:d, :]
            sr = jnp.where(row >= d, pltpu.roll(xr, d, axis=0), 0.0)
            si = jnp.where(row >= d, pltpu.roll(xi, d, axis=0), 0.0)
            xr, xi = xr + ar * sr - ai * si, xi + ar * si + ai * sr
        if chain:
            cr, ci = carry
        else:
            cr = h0r_ref[k, 0]
            ci = h0i_ref[k, 0]
        hr = xr + apr * cr - api * ci
        hi = xi + apr * ci + api * cr
        st_ref[pl.ds(r0, SUBLANES), 0:CHUNK_STATE] = hr
        st_ref[pl.ds(r0, SUBLANES), CHUNK_STATE:2 * CHUNK_STATE] = hi
        lr_ = hr[SUBLANES - 1:SUBLANES, :]
        li_ = hi[SUBLANES - 1:SUBLANES, :]
        if chain:
            return lr_, li_
        sr_ref[k, m] = lr_
        si_ref[k, m] = li_
        return carry

    if chain:
        cr, ci = lax.fori_loop(0, n_tiles, tile_step, (cr_ref[m], ci_ref[m]))
        cr_ref[m] = cr
        ci_ref[m] = ci
        sr_ref[0, m] = cr
        si_ref[0, m] = ci
    else:
        lax.fori_loop(0, n_tiles, tile_step, 0)

    y = jnp.dot(st_ref[...].astype(jnp.bfloat16), c_ref[m], preferred_element_type=jnp.float32)
    y = y + d_ref[0] * u
    z_ref[...] = _gelu(y)


def _s5_mixer(x, h0r, h0i, g, d_skip, bmat, cmat_bf16, apr, api, *, chain, seq_len, row0=0):
    nseq = h0r.shape[0]
    T = nseq * seq_len
    rows = S5_ROWS
    blk0 = row0 // rows
    if chain:
        blocks_per_seq = seq_len // rows
        seq_blk = 1
        seq_map = lambda rb, m: (rb // blocks_per_seq, m, 0, 0)
        out_map = lambda rb, m: (rb // blocks_per_seq, 0, 0, 0)
    else:
        assert seq_len == SUBLANES
        blocks_per_seq = 1
        seq_blk = rows // SUBLANES
        seq_map = lambda rb, m: (rb, m, 0, 0)
        out_map = lambda rb, m: (rb, 0, 0, 0)
    grid = (T // rows, N_CHUNKS)
    kern = functools.partial(_s5_kernel, chain=chain, blocks_per_seq=blocks_per_seq)
    st_spec = pl.BlockSpec((seq_blk, 1, 1, CHUNK_STATE), seq_map)
    out_st_spec = pl.BlockSpec((seq_blk, N_CHUNKS, 1, CHUNK_STATE), out_map)
    z, sr, si = pl.pallas_call(
        kern,
        grid=grid,
        in_specs=[
            pl.BlockSpec((rows, D_MODEL), lambda rb, m: (rb + blk0, 0)),
            pl.BlockSpec((1, D_MODEL), lambda rb, m: (0, 0)),
            pl.BlockSpec((1, 1, LANES), lambda rb, m: (m, 0, 0)),
            pl.BlockSpec((N_CHUNKS, 2 * LANES, 2 * CHUNK_STATE), lambda rb, m: (0, 0, 0)),
            pl.BlockSpec((N_CHUNKS, LANES, 2 * CHUNK_STATE), lambda rb, m: (0, 0, 0)),
            pl.BlockSpec((N_CHUNKS, 2 * CHUNK_STATE, LANES), lambda rb, m: (0, 0, 0)),
            pl.BlockSpec((1, SUBLANES, CHUNK_STATE), lambda rb, m: (m, 0, 0)),
            pl.BlockSpec((1, SUBLANES, CHUNK_STATE), lambda rb, m: (m, 0, 0)),
            st_spec, st_spec,
        ],
        out_specs=[
            pl.BlockSpec((rows, LANES), lambda rb, m: (rb, m)),
            out_st_spec, out_st_spec,
        ],
        out_shape=[
            jax.ShapeDtypeStruct((T, D_MODEL), jnp.float32),
            jax.ShapeDtypeStruct((nseq, N_CHUNKS, 1, CHUNK_STATE), jnp.float32),
            jax.ShapeDtypeStruct((nseq, N_CHUNKS, 1, CHUNK_STATE), jnp.float32),
        ],
        scratch_shapes=[
            pltpu.VMEM((N_CHUNKS, rows, LANES), jnp.float32),
            pltpu.VMEM((N_CHUNKS, rows, 2 * LANES), jnp.bfloat16),
            pltpu.VMEM((rows, 2 * CHUNK_STATE), jnp.float32),
            pltpu.VMEM((N_CHUNKS, 1, CHUNK_STATE), jnp.float32),
            pltpu.VMEM((N_CHUNKS, 1, CHUNK_STATE), jnp.float32),
        ],
        compiler_params=pltpu.CompilerParams(
            dimension_semantics=("arbitrary", "arbitrary"), vmem_limit_bytes=VMEM_LIMIT_BYTES),
        name="s5_mixer",
    )(x, g.reshape(1, D_MODEL), d_skip.reshape(N_CHUNKS, 1, LANES), bmat[0], bmat[1], cmat_bf16, apr, api,
      h0r, h0i)
    return z, sr, si


def _state_to_chunks(h):
    return h.reshape(h.shape[0], N_CHUNKS, 1, CHUNK_STATE)


def _chunks_to_state(s):
    return s.reshape(s.shape[0], N_GROUPS, STATE_DIM)


def _glu_kernel(z_ref, res_ref, w_ref, o_ref):
    zz = jnp.dot(z_ref[...].astype(jnp.bfloat16), w_ref[...], preferred_element_type=jnp.float32)
    a = zz[:, :D_MODEL]
    b = zz[:, D_MODEL:]
    o_ref[...] = res_ref[...] + a * (1.0 / (1.0 + jnp.exp(-b)))


def _glu_residual(z, res, w_bf16, res_row0=0):
    T = z.shape[0]
    rows = math.gcd(T, 512)
    blk0 = res_row0 // rows
    return pl.pallas_call(
        _glu_kernel,
        grid=(T // rows,),
        in_specs=[
            pl.BlockSpec((rows, D_MODEL), lambda i: (i, 0)),
            pl.BlockSpec((rows, D_MODEL), lambda i: (i + blk0, 0)),
            pl.BlockSpec((D_MODEL, 2 * D_MODEL), lambda i: (0, 0)),
        ],
        out_specs=pl.BlockSpec((rows, D_MODEL), lambda i: (i, 0)),
        out_shape=jax.ShapeDtypeStruct((T, D_MODEL), jnp.float32),
        compiler_params=pltpu.CompilerParams(
            dimension_semantics=("arbitrary",), vmem_limit_bytes=VMEM_LIMIT_BYTES),
        name="glu_residual",
    )(z, res, w_bf16)


ROUTE_ROWS = 256


def _topk_rows(s, payload):
    n_rows = s.shape[0]
    row = lax.broadcasted_iota(jnp.int32, s.shape, 0)
    vals, picks = [], []
    for _ in range(PEER_TOPK):
        m = jnp.max(s, axis=0, keepdims=True)
        pos = jnp.min(jnp.where(s == m, row, n_rows), axis=0, keepdims=True)
        sel = row == pos
        vals.append(m)
        if payload is None:
            picks.append(pos)
        else:
            picks.append(jnp.max(jnp.where(sel, payload, -1), axis=0, keepdims=True))
        s = jnp.where(sel, -jnp.inf, s)
    return jnp.concatenate(vals, axis=0), jnp.concatenate(picks, axis=0)


def _pair_rows(a0, a1, combine):
    half = PEER_TOPK // 2
    parts = [combine(a0[0:1, :], a1)]
    parts += [combine(a0[i:i + 1, :], a1[0:half, :]) for i in range(1, half)]
    parts.append(combine(a0[half:PEER_TOPK, :], a1[0:1, :]))
    return jnp.concatenate(parts, axis=0)


def _peer_route_kernel(h_ref, g_ref, wq_ref, sk_ref, xn_ref, eid_ref, gate_ref, xb_ref):
    hd = pl.program_id(1)

    @pl.when(hd == 0)
    def _():
        xn = _rmsnorm_rows(h_ref[...], g_ref[...])
        xn_ref[...] = xn
        xb_ref[...] = xn.astype(jnp.bfloat16)

    q = jnp.dot(xb_ref[...], wq_ref[hd], preferred_element_type=jnp.float32)
    sv, si = [], []
    for c in range(2):
        qc = q[:, c * PEER_DHALF:(c + 1) * PEER_DHALF].astype(jnp.bfloat16)
        st = lax.dot_general(sk_ref[hd, c], qc, (((1,), (1,)), ((), ())), preferred_element_type=jnp.float32)
        v, i = _topk_rows(st, None)
        sv.append(v)
        si.append(i)
    cand = _pair_rows(sv[0], sv[1], lambda a, b: a + b)
    cid = _pair_rows(si[0], si[1], lambda a, b: a * N_KEYS + b)
    fv, eid = _topk_rows(cand, cid)
    e = jnp.exp(fv - fv[0:1, :])
    gate_ref[...] = e / jnp.sum(e, axis=0, keepdims=True)
    eid_ref[...] = eid


def _peer_route(h, g, wq_bf16, sk_bf16):
    T = h.shape[0]
    rows = ROUTE_ROWS
    n_sel = PEER_HEADS * PEER_TOPK
    return pl.pallas_call(
        _peer_route_kernel,
        grid=(T // rows, PEER_HEADS),
        in_specs=[
            pl.BlockSpec((rows, D_MODEL), lambda tb, hd: (tb, 0)),
            pl.BlockSpec((1, D_MODEL), lambda tb, hd: (0, 0)),
            pl.BlockSpec((PEER_HEADS, D_MODEL, 2 * PEER_DHALF), lambda tb, hd: (0, 0, 0)),
            pl.BlockSpec((PEER_HEADS, 2, N_KEYS, PEER_DHALF), lambda tb, hd: (0, 0, 0, 0)),
        ],
        out_specs=[
            pl.BlockSpec((rows, D_MODEL), lambda tb, hd: (tb, 0)),
            pl.BlockSpec((PEER_TOPK, rows), lambda tb, hd: (hd, tb)),
            pl.BlockSpec((PEER_TOPK, rows), lambda tb, hd: (hd, tb)),
        ],
        out_shape=[
            jax.ShapeDtypeStruct((T, D_MODEL), jnp.float32),
            jax.ShapeDtypeStruct((n_sel, T), jnp.int32),
            jax.ShapeDtypeStruct((n_sel, T), jnp.float32),
        ],
        scratch_shapes=[pltpu.VMEM((rows, D_MODEL), jnp.bfloat16)],
        compiler_params=pltpu.CompilerParams(
            dimension_semantics=("arbitrary", "arbitrary"), vmem_limit_bytes=VMEM_LIMIT_BYTES),
        name="peer_route",
    )(h, g.reshape(1, D_MODEL), wq_bf16, sk_bf16)


N_SEL = PEER_HEADS * PEER_TOPK
SC_LANES = 16
GATHER_ROWS = PEER_TOPK
GATHERS_PER_TOKEN = N_SEL // GATHER_ROWS
GATHER_BUFS = 4
SC_TOKENS = 8
ACC_STRIDE = SC_LANES + 1


def _sc_gather_stream(wid, n_batches, tab_hbm, stage_srcs, stage_bufs, out_hbm, o_v, bufs, gsem, ssem, osem,
                      compute):
    idx_v = stage_bufs[0]

    def stage_copies(bi, slot):
        base = (wid * n_batches + bi) * SC_TOKENS
        return [pltpu.make_async_copy(src.at[pl.ds(base, SC_TOKENS)], buf.at[slot], ssem.at[slot])
                for src, buf in zip(stage_srcs, stage_bufs)]

    def out_copy(bi, slot):
        base = (wid * n_batches + bi) * SC_TOKENS
        return pltpu.make_async_copy(o_v.at[slot], out_hbm.at[pl.ds(base, SC_TOKENS)], osem.at[slot])

    def start(slot, t, kk, b):
        idx = idx_v[slot, t, pl.ds(kk * GATHER_ROWS, GATHER_ROWS)]
        pltpu.async_copy(tab_hbm.at[idx], bufs.at[b], gsem.at[b])

    def wait(b):
        pltpu.make_async_copy(tab_hbm.at[pl.ds(0, GATHER_ROWS)], bufs.at[b], gsem.at[b]).wait()

    for c in stage_copies(0, 0):
        c.start()
    for c in stage_copies(0, 0):
        c.wait()
    for q in range(GATHER_BUFS - 1):
        start(0, q // GATHERS_PER_TOKEN, q % GATHERS_PER_TOKEN, q % GATHER_BUFS)

    def batch(bi, carry):
        slot = bi % 2
        has_next = bi + 1 < n_batches

        @pl.when(has_next)
        def _():
            for c in stage_copies(bi + 1, 1 - slot):
                c.start()

        @pl.when(bi >= 2)
        def _():
            out_copy(bi, slot).wait()

        def tok(t, carry):
            @pl.when(jnp.logical_and(t == SC_TOKENS - 1, has_next))
            def _():
                for c in stage_copies(bi + 1, 1 - slot):
                    c.wait()

            for kk in range(GATHERS_PER_TOKEN):
                nq = kk + GATHER_BUFS - 1
                nk, nb = nq % GATHERS_PER_TOKEN, nq % GATHER_BUFS
                if nq < GATHERS_PER_TOKEN:
                    start(slot, t, nk, nb)
                else:
                    @pl.when(t + 1 < SC_TOKENS)
                    def _():
                        start(slot, t + 1, nk, nb)

                    @pl.when(jnp.logical_and(t + 1 == SC_TOKENS, has_next))
                    def _():
                        start(1 - slot, 0, nk, nb)

                wait(kk % GATHER_BUFS)
                compute(slot, t, kk, kk % GATHER_BUFS)
            return carry

        lax.fori_loop(0, SC_TOKENS, tok, 0)
        out_copy(bi, slot).start()
        return carry

    lax.fori_loop(0, n_batches, batch, 0)
    if n_batches >= 2:
        out_copy(n_batches - 2, (n_batches - 2) % 2).wait()
    out_copy(n_batches - 1, (n_batches - 1) % 2).wait()


def _sc_mesh_and_batches(n_tokens):
    info = plsc.get_sparse_core_info()
    assert info.num_lanes == SC_LANES
    n_workers = info.num_cores * info.num_subcores
    assert n_tokens % (n_workers * SC_TOKENS) == 0
    mesh = plsc.VectorSubcoreMesh(core_axis_name="c", subcore_axis_name="s")
    return info, mesh, n_tokens // (n_workers * SC_TOKENS)


def _hidden_compute(x_v, o_v, bufs, acc_v):
    lane = lax.iota(jnp.int32, SC_LANES)
    zero = jnp.zeros((SC_LANES,), jnp.float32)

    def compute(slot, t, kk, b):
        @plsc.parallel_loop(0, D_MODEL // SC_LANES, carry=(zero,) * GATHER_ROWS)
        def accs(c, accs):
            xc = x_v[slot, t, pl.ds(c * SC_LANES, SC_LANES)]
            return tuple(accs[r] + bufs[b, r, pl.ds(c * SC_LANES, SC_LANES)] * xc for r in range(GATHER_ROWS))

        for r in range(GATHER_ROWS):
            acc_v[pl.ds(r * ACC_STRIDE, SC_LANES)] = accs[r]
        tot = zero
        for c in range(SC_LANES):
            tot = tot + plsc.load_gather(acc_v, [lane * ACC_STRIDE + c])
        o_v[slot, t, pl.ds(kk * GATHER_ROWS, GATHER_ROWS)] = tot

    return compute


def _combine_compute(a_v, o_v, bufs):
    def compute(slot, t, kk, b):
        svec = jnp.full((SC_LANES,), slot, jnp.int32)
        tvec = jnp.full((SC_LANES,), t, jnp.int32)
        ws = [plsc.load_gather(a_v, [svec, tvec, jnp.full((SC_LANES,), kk * GATHER_ROWS + r, jnp.int32)])
              for r in range(GATHER_ROWS)]

        @plsc.parallel_loop(0, D_MODEL // SC_LANES, unroll=2)
        def _(c):
            sl = pl.ds(c * SC_LANES, SC_LANES)
            terms = [ws[r] * bufs[b, r, sl] for r in range(GATHER_ROWS)]
            if kk != 0:
                terms.append(o_v[slot, t, sl])
            while len(terms) > 1:
                pairs = [terms[i] + terms[i + 1] for i in range(0, len(terms) - 1, 2)]
                terms = pairs + ([terms[-1]] if len(terms) % 2 else [])
            o_v[slot, t, sl] = terms[0]

    return compute


def _peer_step_sc(eid_c, a, layer_c, eid_h, xn, layer_h, u_tabs, v_tabs):
    Tc, Th = eid_c.shape[0], eid_h.shape[0]
    info, mesh, nb_c = _sc_mesh_and_batches(Tc)
    _, _, nb_h = _sc_mesh_and_batches(Th)

    @functools.partial(
        pl.kernel, mesh=mesh,
        out_type=(jax.ShapeDtypeStruct((Tc, D_MODEL), jnp.float32), jax.ShapeDtypeStruct((Th, N_SEL), jnp.float32)),
        scratch_types=[
            pltpu.VMEM((2, SC_TOKENS, N_SEL), jnp.int32),
            pltpu.VMEM((2, SC_TOKENS, N_SEL), jnp.float32),
            pltpu.VMEM((2, SC_TOKENS, D_MODEL), jnp.float32),
            pltpu.VMEM((2, SC_TOKENS, N_SEL), jnp.int32),
            pltpu.VMEM((2, SC_TOKENS, D_MODEL), jnp.float32),
            pltpu.VMEM((2, SC_TOKENS, N_SEL), jnp.float32),
            pltpu.VMEM((GATHER_BUFS, GATHER_ROWS, D_MODEL), jnp.float32),
            pltpu.VMEM((GATHER_ROWS * ACC_STRIDE,), jnp.float32),
            pltpu.SemaphoreType.DMA((GATHER_BUFS,)),
            pltpu.SemaphoreType.DMA((2,)),
            pltpu.SemaphoreType.DMA((2,)),
        ],
        compiler_params=pltpu.CompilerParams(needs_layout_passes=False),
        name="peer_step_sc",
    )
    def k(eidc_hbm, a_hbm, eidh_hbm, xn_hbm, u_hbm, v_hbm, c_hbm, hp_hbm,
          idxc_v, a_v, oc_v, idxh_v, x_v, oh_v, bufs, acc_v, gsem, ssem, osem):
        wid = lax.axis_index("s") * info.num_cores + lax.axis_index("c")
        _sc_gather_stream(wid, nb_c, v_hbm.at[layer_c], [eidc_hbm, a_hbm], [idxc_v, a_v], c_hbm, oc_v, bufs,
                          gsem, ssem, osem, _combine_compute(a_v, oc_v, bufs))
        _sc_gather_stream(wid, nb_h, u_hbm.at[layer_h], [eidh_hbm, xn_hbm], [idxh_v, x_v], hp_hbm, oh_v, bufs,
                          gsem, ssem, osem, _hidden_compute(x_v, oh_v, bufs, acc_v))

    return k(eid_c, a, eid_h, xn, u_tabs, v_tabs)


def _peer_hidden_sc(eid, xn, u_tabs, layer):
    T = eid.shape[0]
    info, mesh, n_batches = _sc_mesh_and_batches(T)

    @functools.partial(
        pl.kernel, mesh=mesh,
        out_type=jax.ShapeDtypeStruct((T, N_SEL), jnp.float32),
        scratch_types=[
            pltpu.VMEM((2, SC_TOKENS, N_SEL), jnp.int32),
            pltpu.VMEM((2, SC_TOKENS, D_MODEL), jnp.float32),
            pltpu.VMEM((2, SC_TOKENS, N_SEL), jnp.float32),
            pltpu.VMEM((GATHER_BUFS, GATHER_ROWS, D_MODEL), jnp.float32),
            pltpu.VMEM((GATHER_ROWS * ACC_STRIDE,), jnp.float32),
            pltpu.SemaphoreType.DMA((GATHER_BUFS,)),
            pltpu.SemaphoreType.DMA((2,)),
            pltpu.SemaphoreType.DMA((2,)),
        ],
        compiler_params=pltpu.CompilerParams(needs_layout_passes=False),
        name="peer_hidden_sc",
    )
    def k(eid_hbm, xn_hbm, u_hbm, out_hbm, idx_v, x_v, o_v, bufs, acc_v, gsem, ssem, osem):
        wid = lax.axis_index("s") * info.num_cores + lax.axis_index("c")
        _sc_gather_stream(wid, n_batches, u_hbm.at[layer], [eid_hbm, xn_hbm], [idx_v, x_v], out_hbm, o_v, bufs,
                          gsem, ssem, osem, _hidden_compute(x_v, o_v, bufs, acc_v))

    return k(eid, xn, u_tabs)


def _peer_combine_sc(eid, a, v_tabs, layer):
    T = eid.shape[0]
    info, mesh, n_batches = _sc_mesh_and_batches(T)

    @functools.partial(
        pl.kernel, mesh=mesh,
        out_type=jax.ShapeDtypeStruct((T, D_MODEL), jnp.float32),
        scratch_types=[
            pltpu.VMEM((2, SC_TOKENS, N_SEL), jnp.int32),
            pltpu.VMEM((2, SC_TOKENS, N_SEL), jnp.float32),
            pltpu.VMEM((2, SC_TOKENS, D_MODEL), jnp.float32),
            pltpu.VMEM((GATHER_BUFS, GATHER_ROWS, D_MODEL), jnp.float32),
            pltpu.SemaphoreType.DMA((GATHER_BUFS,)),
            pltpu.SemaphoreType.DMA((2,)),
            pltpu.SemaphoreType.DMA((2,)),
        ],
        compiler_params=pltpu.CompilerParams(needs_layout_passes=False),
        name="peer_combine_sc",
    )
    def k(eid_hbm, a_hbm, v_hbm, out_hbm, idx_v, a_v, o_v, bufs, gsem, ssem, osem):
        wid = lax.axis_index("s") * info.num_cores + lax.axis_index("c")
        _sc_gather_stream(wid, n_batches, v_hbm.at[layer], [eid_hbm, a_hbm], [idx_v, a_v], out_hbm, o_v, bufs,
                          gsem, ssem, osem, _combine_compute(a_v, o_v, bufs))

    return k(eid, a, v_tabs)


def _peer_act_kernel(hp_ref, gate_ref, a_ref):
    a_ref[...] = _gelu(hp_ref[...]) * gate_ref[...]


ELEMENTWISE_ROWS = 1024


def _peer_act(hpre, gate):
    T = hpre.shape[0]
    rows = ELEMENTWISE_ROWS
    assert T % rows == 0
    spec = pl.BlockSpec((rows, N_SEL), lambda i: (i, 0))
    return pl.pallas_call(
        _peer_act_kernel, grid=(T // rows,), in_specs=[spec, spec], out_specs=spec,
        out_shape=jax.ShapeDtypeStruct((T, N_SEL), jnp.float32),
        compiler_params=pltpu.CompilerParams(dimension_semantics=("arbitrary",)),
        name="peer_act",
    )(hpre, gate)


def _residual_kernel(h_ref, c_ref, g_ref, o_ref, *, final_norm):
    y = h_ref[...] + c_ref[...]
    o_ref[...] = _rmsnorm_rows(y, g_ref[...]) if final_norm else y


def _residual(h, c, gfin, *, final_norm):
    T = h.shape[0]
    rows = ELEMENTWISE_ROWS
    assert T % rows == 0
    spec = pl.BlockSpec((rows, D_MODEL), lambda i: (i, 0))
    return pl.pallas_call(
        functools.partial(_residual_kernel, final_norm=final_norm), grid=(T // rows,),
        in_specs=[spec, spec, pl.BlockSpec((1, D_MODEL), lambda i: (0, 0))], out_specs=spec,
        out_shape=jax.ShapeDtypeStruct((T, D_MODEL), jnp.float32),
        compiler_params=pltpu.CompilerParams(dimension_semantics=("arbitrary",)),
        name="peer_residual",
    )(h, c, gfin.reshape(1, D_MODEL))


KV_WIDTH = N_KV_HEADS * HEAD_DIM
BF16 = jnp.bfloat16


def _qkv(h, gkv, gq, wkv_ref, wq_ref):
    kv = jnp.dot(_rmsnorm_rows(h, gkv).astype(BF16), wkv_ref[...], preferred_element_type=jnp.float32)
    q = jnp.dot(_rmsnorm_rows(h, gq).astype(BF16), wq_ref[...], preferred_element_type=jnp.float32)
    return kv, q


def _sink_softmax_pv(parts, sink):
    m = sink
    for s, _ in parts:
        m = jnp.maximum(m, jnp.max(s, axis=-1, keepdims=True))
    den = jnp.exp(sink - m)
    acc = None
    for s, v in parts:
        e = jnp.exp(s - m)
        den = den + jnp.sum(e, axis=-1, keepdims=True)
        pv = jnp.dot(e.astype(BF16), v.astype(BF16), preferred_element_type=jnp.float32)
        acc = pv if acc is None else acc + pv
    return acc / den


def _nt_dot(a, b):
    return lax.dot_general(a.astype(BF16), b.astype(BF16), (((1,), (1,)), ((), ())),
                           preferred_element_type=jnp.float32)


def _attn_prompt_kernel(h_ref, gkv_ref, gq_ref, wkv_ref, wq_ref, wo_ref, sink_ref,
                        o_ref, kvw_ref, prev_ref):
    blk = pl.program_id(1)
    h = h_ref[...]
    kv, q = _qkv(h, gkv_ref[...], gq_ref[...], wkv_ref, wq_ref)
    kvw_ref[0] = kv

    @pl.when(blk == 0)
    def _():
        prev_ref[...] = jnp.zeros_like(prev_ref)

    prev = prev_ref[...]
    qi = lax.broadcasted_iota(jnp.int32, (WINDOW, WINDOW), 0)
    kj = lax.broadcasted_iota(jnp.int32, (WINDOW, WINDOW), 1)
    prev_ok = jnp.logical_and(kj > qi, blk > 0)
    cur_ok = kj <= qi
    heads = []
    for kvh in range(N_KV_HEADS):
        ks = slice(kvh * HEAD_DIM, (kvh + 1) * HEAD_DIM)
        vs = slice(KV_WIDTH + kvh * HEAD_DIM, KV_WIDTH + (kvh + 1) * HEAD_DIM)
        for g in range(Q_PER_KV):
            hq = kvh * Q_PER_KV + g
            qh = q[:, hq * HEAD_DIM:(hq + 1) * HEAD_DIM]
            sp = jnp.where(prev_ok, _nt_dot(qh, prev[:, ks]) * ATTN_SCALE, -jnp.inf)
            sc = jnp.where(cur_ok, _nt_dot(qh, kv[:, ks]) * ATTN_SCALE, -jnp.inf)
            heads.append(_sink_softmax_pv([(sp, prev[:, vs]), (sc, kv[:, vs])], sink_ref[hq]))
    o = jnp.concatenate(heads, axis=1)
    o_ref[...] = h + jnp.dot(o.astype(BF16), wo_ref[...], preferred_element_type=jnp.float32)
    prev_ref[...] = kv


def _attn_prompt(h, gkv, gq, wkv, wq, wo, sinks, *, n_seq, seq_len):
    nb = seq_len // WINDOW
    row_spec = pl.BlockSpec((WINDOW, D_MODEL), lambda n, b: (n * nb + b, 0))
    full = lambda shape: pl.BlockSpec(shape, lambda n, b: (0,) * len(shape))
    return pl.pallas_call(
        _attn_prompt_kernel,
        grid=(n_seq, nb),
        in_specs=[
            row_spec, full((1, D_MODEL)), full((1, D_MODEL)),
            full((D_MODEL, 2 * KV_WIDTH)), full((D_MODEL, D_MODEL)), full((D_MODEL, D_MODEL)),
            pl.BlockSpec(memory_space=pltpu.SMEM),
        ],
        out_specs=[row_spec, pl.BlockSpec((1, WINDOW, 2 * KV_WIDTH), lambda n, b: (n, 0, 0))],
        out_shape=[
            jax.ShapeDtypeStruct(h.shape, jnp.float32),
            jax.ShapeDtypeStruct((n_seq, WINDOW, 2 * KV_WIDTH), jnp.float32),
        ],
        scratch_shapes=[pltpu.VMEM((WINDOW, 2 * KV_WIDTH), jnp.float32)],
        compiler_params=pltpu.CompilerParams(
            dimension_semantics=("arbitrary", "arbitrary"), vmem_limit_bytes=VMEM_LIMIT_BYTES),
        name="attn_prompt",
    )(h, gkv.reshape(1, D_MODEL), gq.reshape(1, D_MODEL), wkv, wq, wo, sinks)


ATTN_SEQS = 16
FIRST_PROMPT_PIECES = 4


def _attn_sample_kernel(h_ref, ck_ref, cv_ref, gkv_ref, gq_ref, wkv_ref, wq_ref, wo_ref, sink_ref,
                        o_ref, kw_ref, vw_ref, kv_ref, q_ref, att_ref, *, n_new):
    h = h_ref[...]
    kv, q = _qkv(h, gkv_ref[...], gq_ref[...], wkv_ref, wq_ref)
    kv_ref[...] = kv
    q_ref[...] = q
    n_seq = h.shape[0] // n_new
    rows = Q_PER_KV * n_new
    qpos_c = lax.broadcasted_iota(jnp.int32, (rows, WINDOW), 0) % n_new
    cache_ok = lax.broadcasted_iota(jnp.int32, (rows, WINDOW), 1) > qpos_c
    qpos_n = lax.broadcasted_iota(jnp.int32, (rows, n_new), 0) % n_new
    new_ok = lax.broadcasted_iota(jnp.int32, (rows, n_new), 1) <= qpos_n

    def per_seq(n, carry):
        r0 = pl.multiple_of(n * n_new, n_new)
        kvn = kv_ref[pl.ds(r0, n_new), :]
        qn = q_ref[pl.ds(r0, n_new), :]
        ck = ck_ref[n]
        cv = cv_ref[n]
        kw_ref[n, 0:WINDOW - n_new, :] = ck[n_new:, :]
        kw_ref[n, WINDOW - n_new:WINDOW, :] = kvn[:, 0:KV_WIDTH]
        vw_ref[n, 0:WINDOW - n_new, :] = cv[n_new:, :]
        vw_ref[n, WINDOW - n_new:WINDOW, :] = kvn[:, KV_WIDTH:2 * KV_WIDTH]
        outs = []
        for kvh in range(N_KV_HEADS):
            ks = slice(kvh * HEAD_DIM, (kvh + 1) * HEAD_DIM)
            vs = slice(KV_WIDTH + kvh * HEAD_DIM, KV_WIDTH + (kvh + 1) * HEAD_DIM)
            qs = jnp.concatenate(
                [qn[:, (kvh * Q_PER_KV + g) * HEAD_DIM:(kvh * Q_PER_KV + g + 1) * HEAD_DIM] for g in range(Q_PER_KV)],
                axis=0)
            s_c = jnp.where(cache_ok, _nt_dot(qs, ck[:, ks]) * ATTN_SCALE, -jnp.inf)
            s_n = jnp.where(new_ok, _nt_dot(qs, kvn[:, ks]) * ATTN_SCALE, -jnp.inf)
            sink = sink_ref[kvh][:, 0:1]
            o = _sink_softmax_pv([(s_c, cv[:, ks]), (s_n, kvn[:, vs])], sink)
            outs += [o[g * n_new:(g + 1) * n_new, :] for g in range(Q_PER_KV)]
        att_ref[pl.ds(r0, n_new), :] = jnp.concatenate(outs, axis=1)
        return carry

    lax.fori_loop(0, n_seq, per_seq, 0)
    o_ref[...] = h + jnp.dot(att_ref[...].astype(BF16), wo_ref[...], preferred_element_type=jnp.float32)


def _attn_sample(h, cache_k, cache_v, gkv, gq, wkv, wq, wo, sinks, *, n_new):
    n_seq = cache_k.shape[0]
    sb = ATTN_SEQS
    rows = sb * n_new
    row_spec = pl.BlockSpec((rows, D_MODEL), lambda i: (i, 0))
    win_spec = pl.BlockSpec((sb, WINDOW, KV_WIDTH), lambda i: (i, 0, 0))
    full = lambda shape: pl.BlockSpec(shape, lambda i: (0,) * len(shape))
    sink_rows = jnp.repeat(sinks.reshape(N_KV_HEADS, Q_PER_KV), n_new, axis=1)[:, :, None]
    sink_rows = jnp.broadcast_to(sink_rows, (N_KV_HEADS, Q_PER_KV * n_new, LANES))
    return pl.pallas_call(
        functools.partial(_attn_sample_kernel, n_new=n_new),
        grid=(n_seq // sb,),
        in_specs=[
            row_spec, win_spec, win_spec, full((1, D_MODEL)), full((1, D_MODEL)),
            full((D_MODEL, 2 * KV_WIDTH)), full((D_MODEL, D_MODEL)), full((D_MODEL, D_MODEL)),
            full((N_KV_HEADS, Q_PER_KV * n_new, LANES)),
        ],
        out_specs=[row_spec, win_spec, win_spec],
        out_shape=[
            jax.ShapeDtypeStruct(h.shape, jnp.float32),
            jax.ShapeDtypeStruct((n_seq, WINDOW, KV_WIDTH), jnp.float32),
            jax.ShapeDtypeStruct((n_seq, WINDOW, KV_WIDTH), jnp.float32),
        ],
        scratch_shapes=[
            pltpu.VMEM((rows, 2 * KV_WIDTH), jnp.float32),
            pltpu.VMEM((rows, D_MODEL), jnp.float32),
            pltpu.VMEM((rows, D_MODEL), jnp.float32),
        ],
        compiler_params=pltpu.CompilerParams(
            dimension_semantics=("arbitrary",), vmem_limit_bytes=VMEM_LIMIT_BYTES),
        name="attn_sample",
    )(h, cache_k, cache_v, gkv.reshape(1, D_MODEL), gq.reshape(1, D_MODEL), wkv, wq, wo, sink_rows)


def kernel(x_prompt, x_sample, state_ssm_re, state_ssm_im, cache_k_win, cache_v_win, norm_mix, norm_ffn, norm_kv, norm_final, ssm_lam_re, ssm_lam_im, ssm_log_dt, ssm_b_re, ssm_b_im, ssm_c_re, ssm_c_im, ssm_d, ssm_w_glu, w_kv, w_q, attn_sinks, w_o, peer_w_q, peer_sub_keys, peer_u, peer_v):
    bmat, cmat, apr, api = _s5_discretize(ssm_lam_re[0], ssm_lam_im[0], ssm_log_dt[0], ssm_b_re[0], ssm_b_im[0], ssm_c_re[0], ssm_c_im[0])
    cmat = cmat.astype(jnp.bfloat16)
    wglu = ssm_w_glu[0].astype(jnp.bfloat16)
    wkv = w_kv.astype(BF16)
    wq = w_q[0].astype(BF16)
    wo = w_o[0].astype(BF16)
    peer_wq = [peer_w_q[layer].astype(BF16).reshape(D_MODEL, PEER_HEADS, 2 * PEER_DHALF).transpose(1, 0, 2)
               for layer in range(2)]
    peer_sk = [peer_sub_keys[layer].astype(BF16) for layer in range(2)]

    def route(h, layer):
        xn, eid_t, gate_t = _peer_route(h, norm_ffn[layer], peer_wq[layer], peer_sk[layer])
        return dict(h=h, xn=xn, eid=eid_t.T, gate=gate_t.T, layer=layer)

    win = lambda a: a.reshape(a.shape[0], WINDOW, N_KV_HEADS, HEAD_DIM)

    seq_len = x_prompt.shape[1]
    xp_all = x_prompt.reshape(-1, D_MODEL)

    n_prompt, n_s = x_prompt.shape[0], x_sample.shape[0]
    xs = x_sample.reshape(-1, D_MODEL)
    z, sre_s, sim_s = _s5_mixer(xs, _state_to_chunks(state_ssm_re[0]), _state_to_chunks(state_ssm_im[0]), norm_mix[0],
                                ssm_d[0], bmat, cmat, apr, api, chain=False, seq_len=x_sample.shape[1])
    units = [route(_glu_residual(z, xs, wglu), 0)]
    group_units = [[0]]
    prompt_states = []
    for seq in range(n_prompt):
        n_pieces = FIRST_PROMPT_PIECES if seq == 0 else 1
        piece = seq_len // n_pieces
        sr = si = jnp.zeros((1, N_CHUNKS, 1, CHUNK_STATE), jnp.float32)
        group_units.append([])
        for p in range(n_pieces):
            row0 = seq * seq_len + p * piece
            z, sr, si = _s5_mixer(xp_all, sr, si, norm_mix[0], ssm_d[0], bmat, cmat, apr, api,
                                  chain=True, seq_len=piece, row0=row0)
            group_units[-1].append(len(units))
            units.append(route(_glu_residual(z, xp_all, wglu, res_row0=row0), 0))
        prompt_states.append((sr, si))
    n_layer0 = len(units)
    windows = {}

    def layer1_unit(g):
        mine = [units[i] for i in group_units[g]]
        h2 = jnp.concatenate([_residual(u['h'], u['c'], norm_final, final_norm=False) for u in mine], axis=0)
        if g == 0:
            h3, kw, vw = _attn_sample(h2, cache_k_win.reshape(n_s, WINDOW, KV_WIDTH),
                                      cache_v_win.reshape(n_s, WINDOW, KV_WIDTH), norm_kv, norm_mix[1],
                                      wkv, wq, wo, attn_sinks[0], n_new=x_sample.shape[1])
            windows[g] = (win(kw), win(vw))
        else:
            h3, kvw = _attn_prompt(h2, norm_kv, norm_mix[1], wkv, wq, wo, attn_sinks[0], n_seq=1, seq_len=seq_len)
            windows[g] = (win(kvw[:, :, :KV_WIDTH]), win(kvw[:, :, KV_WIDTH:]))
        return route(h3, 1)

    n_units = n_layer0 + 1 + n_prompt
    for k in range(n_units + 2):
        if n_layer0 <= k < n_units:
            units.append(layer1_unit(k - n_layer0))
        hid = units[k] if k < n_units else None
        comb = units[k - 2] if k >= 2 else None
        if comb is not None:
            comb['a'] = _peer_act(comb['hpre'], comb['gate'])
        if hid is not None and comb is not None:
            comb['c'], hid['hpre'] = _peer_step_sc(comb['eid'], comb['a'], comb['layer'],
                                                   hid['eid'], hid['xn'], hid['layer'], peer_u, peer_v)
        elif hid is not None:
            hid['hpre'] = _peer_hidden_sc(hid['eid'], hid['xn'], peer_u, hid['layer'])
        else:
            comb['c'] = _peer_combine_sc(comb['eid'], comb['a'], peer_v, comb['layer'])

    ys = [_residual(u['h'], u['c'], norm_final, final_norm=True) for u in units[n_layer0:]]
    y_s = ys[0].reshape(x_sample.shape)
    y_p = jnp.concatenate(ys[1:], axis=0).reshape(x_prompt.shape)
    cat = lambda parts: jnp.concatenate(parts, axis=0)
    sre_p = cat([_chunks_to_state(sr) for sr, _ in prompt_states])
    sim_p = cat([_chunks_to_state(si) for _, si in prompt_states])
    kw_p = cat([windows[g][0] for g in range(1, n_prompt + 1)])
    vw_p = cat([windows[g][1] for g in range(1, n_prompt + 1)])
    return (y_p, y_s, sre_p[None], sim_p[None], kw_p, vw_p,
            _chunks_to_state(sre_s)[None], _chunks_to_state(sim_s)[None], windows[0][0], windows[0][1])
```

```python
import functools
import math

import jax
import jax.numpy as jnp
from jax import lax
from jax.experimental import pallas as pl
from jax.experimental.pallas import tpu as pltpu
from jax.experimental.pallas import tpu_sc as plsc

D_MODEL = 1024
GROUP_SIZE = 16
N_GROUPS = D_MODEL // GROUP_SIZE
STATE_DIM = 64
HEAD_DIM = 64
N_Q_HEADS = D_MODEL // HEAD_DIM
N_KV_HEADS = N_Q_HEADS // 8
Q_PER_KV = N_Q_HEADS // N_KV_HEADS
WINDOW = 128
ATTN_SCALE = 1.0 / math.sqrt(HEAD_DIM)
PEER_HEADS = 8
N_KEYS = 128
PEER_TOPK = 16
PEER_DHALF = 128
EPS = 1e-5

LANES = 128
SUBLANES = 8
VMEM_LIMIT_BYTES = 56 * 1024 * 1024

GROUPS_PER_CHUNK = LANES // GROUP_SIZE
N_CHUNKS = N_GROUPS // GROUPS_PER_CHUNK
CHUNK_STATE = GROUPS_PER_CHUNK * STATE_DIM
S5_ROWS = 256


def _rmsnorm_rows(x, g):
    r = lax.rsqrt(jnp.mean(x * x, axis=-1, keepdims=True) + EPS)
    return x * r * g


def _gelu(x):
    return 0.5 * x * (1.0 + lax.erf(x * (1.0 / math.sqrt(2.0))))


def _s5_discretize(lam_re, lam_im, log_dt, b_re, b_im, c_re, c_im):
    f32 = jnp.float32
    lr = lam_re.astype(f32)
    li = lam_im.astype(f32)
    dt = jnp.exp(log_dt.astype(f32))[:, None]
    mag = jnp.exp(lr * dt)
    ab_re = mag * jnp.cos(li * dt)
    ab_im = mag * jnp.sin(li * dt)
    den = lr * lr + li * li
    f_re = ((ab_re - 1.0) * lr + ab_im * li) / den
    f_im = (ab_im * lr - (ab_re - 1.0) * li) / den
    br = b_re.astype(f32)
    bi = b_im.astype(f32)
    bb_re = f_re[..., None] * br - f_im[..., None] * bi
    bb_im = f_re[..., None] * bi + f_im[..., None] * br
    eye = jnp.eye(GROUPS_PER_CHUNK, dtype=f32)

    def chunk_rows(v):
        return v.reshape(N_CHUNKS, 1, CHUNK_STATE)

    def in_blocks(bb):
        t = bb.reshape(N_CHUNKS, GROUPS_PER_CHUNK, STATE_DIM, GROUP_SIZE).transpose(0, 1, 3, 2)
        return jnp.einsum('mgjp,gh->mgjhp', t, eye).reshape(N_CHUNKS, LANES, CHUNK_STATE)

    def out_blocks(c):
        t = c.astype(f32).reshape(N_CHUNKS, GROUPS_PER_CHUNK, GROUP_SIZE, STATE_DIM).transpose(0, 1, 3, 2)
        return jnp.einsum('mgpj,gh->mgphj', t, eye).reshape(N_CHUNKS, CHUNK_STATE, LANES)

    bfull = jnp.concatenate([in_blocks(bb_re), in_blocks(bb_im)], axis=2)
    b_hi = bfull.astype(jnp.bfloat16)
    b_lo = (bfull - b_hi.astype(f32)).astype(jnp.bfloat16)
    bmat = (jnp.concatenate([b_hi, b_hi], axis=1), b_lo)
    cmat = jnp.concatenate([out_blocks(c_re), -out_blocks(c_im)], axis=1)
    pr, pi = [ab_re], [ab_im]
    for _ in range(SUBLANES - 1):
        pr, pi = pr + [pr[-1] * ab_re - pi[-1] * ab_im], pi + [pr[-1] * ab_im + pi[-1] * ab_re]
    apr = jnp.concatenate([chunk_rows(v) for v in pr], axis=1)
    api = jnp.concatenate([chunk_rows(v) for v in pi], axis=1)
    return bmat, cmat, apr, api


def _split_bf16(x):
    hi = x.astype(jnp.bfloat16)
    return hi, (x - hi.astype(jnp.float32)).astype(jnp.bfloat16)


def _s5_kernel(x_ref, g_ref, d_ref, bhh_ref, blo_ref, c_ref, apr_ref, api_ref, h0r_ref, h0i_ref,
               z_ref, sr_ref, si_ref, u_ref, us_ref, st_ref, cr_ref, ci_ref, *, chain, blocks_per_seq):
    rb = pl.program_id(0)
    m = pl.program_id(1)
    rows = x_ref.shape[0]
    n_tiles = rows // SUBLANES

    @pl.when(m == 0)
    def _():
        u = _rmsnorm_rows(x_ref[...], g_ref[...])
        for mm in range(N_CHUNKS):
            uc = u[:, mm * LANES:(mm + 1) * LANES]
            u_ref[mm] = uc
            hi, lo = _split_bf16(uc)
            us_ref[mm, :, 0:LANES] = hi
            us_ref[mm, :, LANES:2 * LANES] = lo

    u = u_ref[m]
    us = us_ref[m]
    st_ref[...] = (jnp.dot(us, bhh_ref[m], preferred_element_type=jnp.float32)
                   + jnp.dot(us[:, 0:LANES], blo_ref[m], preferred_element_type=jnp.float32))

    apr = apr_ref[0]
    api = api_ref[0]
    row = lax.broadcasted_iota(jnp.int32, (SUBLANES, CHUNK_STATE), 0)

    if chain:
        @pl.when(rb % blocks_per_seq == 0)
        def _():
            cr_ref[m] = h0r_ref[0, 0]
            ci_ref[m] = h0i_ref[0, 0]

    def tile_step(k, carry):
        r0 = pl.multiple_of(k * SUBLANES, SUBLANES)
        xr = st_ref[pl.ds(r0, SUBLANES), 0:CHUNK_STATE]
        xi = st_ref[pl.ds(r0, SUBLANES), CHUNK_STATE:2 * CHUNK_STATE]
        for d in (1, 2, 4):
            ar = apr[d - 1:d, :]
            ai = api[d - 1:d, :]
            sr = jnp.where(row >= d, pltpu.roll(xr, d, axis=0), 0.0)
            si = jnp.where(row >= d, pltpu.roll(xi, d, axis=0), 0.0)
            xr, xi = xr + ar * sr - ai * si, xi + ar * si + ai * sr
        if chain:
            cr, ci = carry
        else:
            cr = h0r_ref[k, 0]
            ci = h0i_ref[k, 0]
        hr = xr + apr * cr - api * ci
        hi = xi + apr * ci + api * cr
        st_ref[pl.ds(r0, SUBLANES), 0:CHUNK_STATE] = hr
        st_ref[pl.ds(r0, SUBLANES), CHUNK_STATE:2 * CHUNK_STATE] = hi
        lr_ = hr[SUBLANES - 1:SUBLANES, :]
        li_ = hi[SUBLANES - 1:SUBLANES, :]
        if chain:
            return lr_, li_
        sr_ref[k, m] = lr_
        si_ref[k, m] = li_
        return carry

    if chain:
        cr, ci = lax.fori_loop(0, n_tiles, tile_step, (cr_ref[m], ci_ref[m]))
        cr_ref[m] = cr
        ci_ref[m] = ci
        sr_ref[0, m] = cr
        si_ref[0, m] = ci
    else:
        lax.fori_loop(0, n_tiles, tile_step, 0)

    y = jnp.dot(st_ref[...].astype(jnp.bfloat16), c_ref[m], preferred_element_type=jnp.float32)
    y = y + d_ref[0] * u
    z_ref[...] = _gelu(y)


def _s5_mixer(x, h0r, h0i, g, d_skip, bmat, cmat_bf16, apr, api, *, chain, seq_len, row0=0):
    nseq = h0r.shape[0]
    T = nseq * seq_len
    rows = S5_ROWS
    blk0 = row0 // rows
    if chain:
        blocks_per_seq = seq_len // rows
        seq_blk = 1
        seq_map = lambda rb, m: (rb // blocks_per_seq, m, 0, 0)
        out_map = lambda rb, m: (rb // blocks_per_seq, 0, 0, 0)
    else:
        assert seq_len == SUBLANES
        blocks_per_seq = 1
        seq_blk = rows // SUBLANES
        seq_map = lambda rb, m: (rb, m, 0, 0)
        out_map = lambda rb, m: (rb, 0, 0, 0)
    grid = (T // rows, N_CHUNKS)
    kern = functools.partial(_s5_kernel, chain=chain, blocks_per_seq=blocks_per_seq)
    st_spec = pl.BlockSpec((seq_blk, 1, 1, CHUNK_STATE), seq_map)
    out_st_spec = pl.BlockSpec((seq_blk, N_CHUNKS, 1, CHUNK_STATE), out_map)
    z, sr, si = pl.pallas_call(
        kern,
        grid=grid,
        in_specs=[
            pl.BlockSpec((rows, D_MODEL), lambda rb, m: (rb + blk0, 0)),
            pl.BlockSpec((1, D_MODEL), lambda rb, m: (0, 0)),
            pl.BlockSpec((1, 1, LANES), lambda rb, m: (m, 0, 0)),
            pl.BlockSpec((N_CHUNKS, 2 * LANES, 2 * CHUNK_STATE), lambda rb, m: (0, 0, 0)),
            pl.BlockSpec((N_CHUNKS, LANES, 2 * CHUNK_STATE), lambda rb, m: (0, 0, 0)),
            pl.BlockSpec((N_CHUNKS, 2 * CHUNK_STATE, LANES), lambda rb, m: (0, 0, 0)),
            pl.BlockSpec((1, SUBLANES, CHUNK_STATE), lambda rb, m: (m, 0, 0)),
            pl.BlockSpec((1, SUBLANES, CHUNK_STATE), lambda rb, m: (m, 0, 0)),
            st_spec, st_spec,
        ],
        out_specs=[
            pl.BlockSpec((rows, LANES), lambda rb, m: (rb, m)),
            out_st_spec, out_st_spec,
        ],
        out_shape=[
            jax.ShapeDtypeStruct((T, D_MODEL), jnp.float32),
            jax.ShapeDtypeStruct((nseq, N_CHUNKS, 1, CHUNK_STATE), jnp.float32),
            jax.ShapeDtypeStruct((nseq, N_CHUNKS, 1, CHUNK_STATE), jnp.float32),
        ],
        scratch_shapes=[
            pltpu.VMEM((N_CHUNKS, rows, LANES), jnp.float32),
            pltpu.VMEM((N_CHUNKS, rows, 2 * LANES), jnp.bfloat16),
            pltpu.VMEM((rows, 2 * CHUNK_STATE), jnp.float32),
            pltpu.VMEM((N_CHUNKS, 1, CHUNK_STATE), jnp.float32),
            pltpu.VMEM((N_CHUNKS, 1, CHUNK_STATE), jnp.float32),
        ],
        compiler_params=pltpu.CompilerParams(
            dimension_semantics=("arbitrary", "arbitrary"), vmem_limit_bytes=VMEM_LIMIT_BYTES),
        name="s5_mixer",
    )(x, g.reshape(1, D_MODEL), d_skip.reshape(N_CHUNKS, 1, LANES), bmat[0], bmat[1], cmat_bf16, apr, api,
      h0r, h0i)
    return z, sr, si


def _state_to_chunks(h):
    return h.reshape(h.shape[0], N_CHUNKS, 1, CHUNK_STATE)


def _chunks_to_state(s):
    return s.reshape(s.shape[0], N_GROUPS, STATE_DIM)


def _glu_kernel(z_ref, res_ref, w_ref, o_ref):
    zz = jnp.dot(z_ref[...].astype(jnp.bfloat16), w_ref[...], preferred_element_type=jnp.float32)
    a = zz[:, :D_MODEL]
    b = zz[:, D_MODEL:]
    o_ref[...] = res_ref[...] + a * (1.0 / (1.0 + jnp.exp(-b)))


def _glu_residual(z, res, w_bf16, res_row0=0):
    T = z.shape[0]
    rows = math.gcd(T, 512)
    blk0 = res_row0 // rows
    return pl.pallas_call(
        _glu_kernel,
        grid=(T // rows,),
        in_specs=[
            pl.BlockSpec((rows, D_MODEL), lambda i: (i, 0)),
            pl.BlockSpec((rows, D_MODEL), lambda i: (i + blk0, 0)),
            pl.BlockSpec((D_MODEL, 2 * D_MODEL), lambda i: (0, 0)),
        ],
        out_specs=pl.BlockSpec((rows, D_MODEL), lambda i: (i, 0)),
        out_shape=jax.ShapeDtypeStruct((T, D_MODEL), jnp.float32),
        compiler_params=pltpu.CompilerParams(
            dimension_semantics=("arbitrary",), vmem_limit_bytes=VMEM_LIMIT_BYTES),
        name="glu_residual",
    )(z, res, w_bf16)


ROUTE_ROWS = 256


def _topk_rows(s, payload):
    n_rows = s.shape[0]
    row = lax.broadcasted_iota(jnp.int32, s.shape, 0)
    vals, picks = [], []
    for _ in range(PEER_TOPK):
        m = jnp.max(s, axis=0, keepdims=True)
        pos = jnp.min(jnp.where(s == m, row, n_rows), axis=0, keepdims=True)
        sel = row == pos
        vals.append(m)
        if payload is None:
            picks.append(pos)
        else:
            picks.append(jnp.max(jnp.where(sel, payload, -1), axis=0, keepdims=True))
        s = jnp.where(sel, -jnp.inf, s)
    return jnp.concatenate(vals, axis=0), jnp.concatenate(picks, axis=0)


def _pair_rows(a0, a1, combine):
    half = PEER_TOPK // 2
    parts = [combine(a0[0:1, :], a1)]
    parts += [combine(a0[i:i + 1, :], a1[0:half, :]) for i in range(1, half)]
    parts.append(combine(a0[half:PEER_TOPK, :], a1[0:1, :]))
    return jnp.concatenate(parts, axis=0)


def _peer_route_kernel(h_ref, g_ref, wq_ref, sk_ref, xn_ref, eid_ref, gate_ref, xb_ref):
    hd = pl.program_id(1)

    @pl.when(hd == 0)
    def _():
        xn = _rmsnorm_rows(h_ref[...], g_ref[...])
        xn_ref[...] = xn
        xb_ref[...] = xn.astype(jnp.bfloat16)

    q = jnp.dot(xb_ref[...], wq_ref[hd], preferred_element_type=jnp.float32)
    sv, si = [], []
    for c in range(2):
        qc = q[:, c * PEER_DHALF:(c + 1) * PEER_DHALF].astype(jnp.bfloat16)
        st = lax.dot_general(sk_ref[hd, c], qc, (((1,), (1,)), ((), ())), preferred_element_type=jnp.float32)
        v, i = _topk_rows(st, None)
        sv.append(v)
        si.append(i)
    cand = _pair_rows(sv[0], sv[1], lambda a, b: a + b)
    cid = _pair_rows(si[0], si[1], lambda a, b: a * N_KEYS + b)
    fv, eid = _topk_rows(cand, cid)
    e = jnp.exp(fv - fv[0:1, :])
    gate_ref[...] = e / jnp.sum(e, axis=0, keepdims=True)
    eid_ref[...] = eid


def _peer_route(h, g, wq_bf16, sk_bf16):
    T = h.shape[0]
    rows = ROUTE_ROWS
    n_sel = PEER_HEADS * PEER_TOPK
    return pl.pallas_call(
        _peer_route_kernel,
        grid=(T // rows, PEER_HEADS),
        in_specs=[
            pl.BlockSpec((rows, D_MODEL), lambda tb, hd: (tb, 0)),
            pl.BlockSpec((1, D_MODEL), lambda tb, hd: (0, 0)),
            pl.BlockSpec((PEER_HEADS, D_MODEL, 2 * PEER_DHALF), lambda tb, hd: (0, 0, 0)),
            pl.BlockSpec((PEER_HEADS, 2, N_KEYS, PEER_DHALF), lambda tb, hd: (0, 0, 0, 0)),
        ],
        out_specs=[
            pl.BlockSpec((rows, D_MODEL), lambda tb, hd: (tb, 0)),
            pl.BlockSpec((PEER_TOPK, rows), lambda tb, hd: (hd, tb)),
            pl.BlockSpec((PEER_TOPK, rows), lambda tb, hd: (hd, tb)),
        ],
        out_shape=[
            jax.ShapeDtypeStruct((T, D_MODEL), jnp.float32),
            jax.ShapeDtypeStruct((n_sel, T), jnp.int32),
            jax.ShapeDtypeStruct((n_sel, T), jnp.float32),
        ],
        scratch_shapes=[pltpu.VMEM((rows, D_MODEL), jnp.bfloat16)],
        compiler_params=pltpu.CompilerParams(
            dimension_semantics=("arbitrary", "arbitrary"), vmem_limit_bytes=VMEM_LIMIT_BYTES),
        name="peer_route",
    )(h, g.reshape(1, D_MODEL), wq_bf16, sk_bf16)


N_SEL = PEER_HEADS * PEER_TOPK
SC_LANES = 16
GATHER_ROWS = PEER_TOPK
GATHERS_PER_TOKEN = N_SEL // GATHER_ROWS
GATHER_BUFS = 4
SC_TOKENS = 8
ACC_STRIDE = SC_LANES + 1


def _sc_gather_stream(wid, n_batches, tab_hbm, stage_srcs, stage_bufs, out_hbm, o_v, bufs, gsem, ssem, osem,
                      compute):
    idx_v = stage_bufs[0]

    def stage_copies(bi, slot):
        base = (wid * n_batches + bi) * SC_TOKENS
        return [pltpu.make_async_copy(src.at[pl.ds(base, SC_TOKENS)], buf.at[slot], ssem.at[slot])
                for src, buf in zip(stage_srcs, stage_bufs)]

    def out_copy(bi, slot):
        base = (wid * n_batches + bi) * SC_TOKENS
        return pltpu.make_async_copy(o_v.at[slot], out_hbm.at[pl.ds(base, SC_TOKENS)], osem.at[slot])

    def start(slot, t, kk, b):
        idx = idx_v[slot, t, pl.ds(kk * GATHER_ROWS, GATHER_ROWS)]
        pltpu.async_copy(tab_hbm.at[idx], bufs.at[b], gsem.at[b])

    def wait(b):
        pltpu.make_async_copy(tab_hbm.at[pl.ds(0, GATHER_ROWS)], bufs.at[b], gsem.at[b]).wait()

    for c in stage_copies(0, 0):
        c.start()
    for c in stage_copies(0, 0):
        c.wait()
    for q in range(GATHER_BUFS - 1):
        start(0, q // GATHERS_PER_TOKEN, q % GATHERS_PER_TOKEN, q % GATHER_BUFS)

    def batch(bi, carry):
        slot = bi % 2
        has_next = bi + 1 < n_batches

        @pl.when(has_next)
        def _():
            for c in stage_copies(bi + 1, 1 - slot):
                c.start()

        @pl.when(bi >= 2)
        def _():
            out_copy(bi, slot).wait()

        def tok(t, carry):
            @pl.when(jnp.logical_and(t == SC_TOKENS - 1, has_next))
            def _():
                for c in stage_copies(bi + 1, 1 - slot):
                    c.wait()

            for kk in range(GATHERS_PER_TOKEN):
                nq = kk + GATHER_BUFS - 1
                nk, nb = nq % GATHERS_PER_TOKEN, nq % GATHER_BUFS
                if nq < GATHERS_PER_TOKEN:
                    start(slot, t, nk, nb)
                else:
                    @pl.when(t + 1 < SC_TOKENS)
                    def _():
                        start(slot, t + 1, nk, nb)

                    @pl.when(jnp.logical_and(t + 1 == SC_TOKENS, has_next))
                    def _():
                        start(1 - slot, 0, nk, nb)

                wait(kk % GATHER_BUFS)
                compute(slot, t, kk, kk % GATHER_BUFS)
            return carry

        lax.fori_loop(0, SC_TOKENS, tok, 0)
        out_copy(bi, slot).start()
        return carry

    lax.fori_loop(0, n_batches, batch, 0)
    if n_batches >= 2:
        out_copy(n_batches - 2, (n_batches - 2) % 2).wait()
    out_copy(n_batches - 1, (n_batches - 1) % 2).wait()


def _sc_mesh_and_batches(n_tokens):
    info = plsc.get_sparse_core_info()
    assert info.num_lanes == SC_LANES
    n_workers = info.num_cores * info.num_subcores
    assert n_tokens % (n_workers * SC_TOKENS) == 0
    mesh = plsc.VectorSubcoreMesh(core_axis_name="c", subcore_axis_name="s")
    return info, mesh, n_tokens // (n_workers * SC_TOKENS)


def _hidden_compute(x_v, o_v, bufs, acc_v):
    lane = lax.iota(jnp.int32, SC_LANES)
    zero = jnp.zeros((SC_LANES,), jnp.float32)

    def compute(slot, t, kk, b):
        @plsc.parallel_loop(0, D_MODEL // SC_LANES, carry=(zero,) * GATHER_ROWS)
        def accs(c, accs):
            xc = x_v[slot, t, pl.ds(c * SC_LANES, SC_LANES)]
            return tuple(accs[r] + bufs[b, r, pl.ds(c * SC_LANES, SC_LANES)] * xc for r in range(GATHER_ROWS))

        for r in range(GATHER_ROWS):
            acc_v[pl.ds(r * ACC_STRIDE, SC_LANES)] = accs[r]
        tot = zero
        for c in range(SC_LANES):
            tot = tot + plsc.load_gather(acc_v, [lane * ACC_STRIDE + c])
        o_v[slot, t, pl.ds(kk * GATHER_ROWS, GATHER_ROWS)] = tot

    return compute


def _combine_compute(a_v, o_v, bufs):
    def compute(slot, t, kk, b):
        svec = jnp.full((SC_LANES,), slot, jnp.int32)
        tvec = jnp.full((SC_LANES,), t, jnp.int32)
        ws = [plsc.load_gather(a_v, [svec, tvec, jnp.full((SC_LANES,), kk * GATHER_ROWS + r, jnp.int32)])
              for r in range(GATHER_ROWS)]

        @plsc.parallel_loop(0, D_MODEL // SC_LANES, unroll=2)
        def _(c):
            sl = pl.ds(c * SC_LANES, SC_LANES)
            terms = [ws[r] * bufs[b, r, sl] for r in range(GATHER_ROWS)]
            if kk != 0:
                terms.append(o_v[slot, t, sl])
            while len(terms) > 1:
                pairs = [terms[i] + terms[i + 1] for i in range(0, len(terms) - 1, 2)]
                terms = pairs + ([terms[-1]] if len(terms) % 2 else [])
            o_v[slot, t, sl] = terms[0]

    return compute


def _peer_step_sc(eid_c, a, layer_c, eid_h, xn, layer_h, u_tabs, v_tabs):
    Tc, Th = eid_c.shape[0], eid_h.shape[0]
    info, mesh, nb_c = _sc_mesh_and_batches(Tc)
    _, _, nb_h = _sc_mesh_and_batches(Th)

    @functools.partial(
        pl.kernel, mesh=mesh,
        out_type=(jax.ShapeDtypeStruct((Tc, D_MODEL), jnp.float32), jax.ShapeDtypeStruct((Th, N_SEL), jnp.float32)),
        scratch_types=[
            pltpu.VMEM((2, SC_TOKENS, N_SEL), jnp.int32),
            pltpu.VMEM((2, SC_TOKENS, N_SEL), jnp.float32),
            pltpu.VMEM((2, SC_TOKENS, D_MODEL), jnp.float32),
            pltpu.VMEM((2, SC_TOKENS, N_SEL), jnp.int32),
            pltpu.VMEM((2, SC_TOKENS, D_MODEL), jnp.float32),
            pltpu.VMEM((2, SC_TOKENS, N_SEL), jnp.float32),
            pltpu.VMEM((GATHER_BUFS, GATHER_ROWS, D_MODEL), jnp.float32),
            pltpu.VMEM((GATHER_ROWS * ACC_STRIDE,), jnp.float32),
            pltpu.SemaphoreType.DMA((GATHER_BUFS,)),
            pltpu.SemaphoreType.DMA((2,)),
            pltpu.SemaphoreType.DMA((2,)),
        ],
        compiler_params=pltpu.CompilerParams(needs_layout_passes=False),
        name="peer_step_sc",
    )
    def k(eidc_hbm, a_hbm, eidh_hbm, xn_hbm, u_hbm, v_hbm, c_hbm, hp_hbm,
          idxc_v, a_v, oc_v, idxh_v, x_v, oh_v, bufs, acc_v, gsem, ssem, osem):
        wid = lax.axis_index("s") * info.num_cores + lax.axis_index("c")
        _sc_gather_stream(wid, nb_c, v_hbm.at[layer_c], [eidc_hbm, a_hbm], [idxc_v, a_v], c_hbm, oc_v, bufs,
                          gsem, ssem, osem, _combine_compute(a_v, oc_v, bufs))
        _sc_gather_stream(wid, nb_h, u_hbm.at[layer_h], [eidh_hbm, xn_hbm], [idxh_v, x_v], hp_hbm, oh_v, bufs,
                          gsem, ssem, osem, _hidden_compute(x_v, oh_v, bufs, acc_v))

    return k(eid_c, a, eid_h, xn, u_tabs, v_tabs)


def _peer_hidden_sc(eid, xn, u_tabs, layer):
    T = eid.shape[0]
    info, mesh, n_batches = _sc_mesh_and_batches(T)

    @functools.partial(
        pl.kernel, mesh=mesh,
        out_type=jax.ShapeDtypeStruct((T, N_SEL), jnp.float32),
        scratch_types=[
            pltpu.VMEM((2, SC_TOKENS, N_SEL), jnp.int32),
            pltpu.VMEM((2, SC_TOKENS, D_MODEL), jnp.float32),
            pltpu.VMEM((2, SC_TOKENS, N_SEL), jnp.float32),
            pltpu.VMEM((GATHER_BUFS, GATHER_ROWS, D_MODEL), jnp.float32),
            pltpu.VMEM((GATHER_ROWS * ACC_STRIDE,), jnp.float32),
            pltpu.SemaphoreType.DMA((GATHER_BUFS,)),
            pltpu.SemaphoreType.DMA((2,)),
            pltpu.SemaphoreType.DMA((2,)),
        ],
        compiler_params=pltpu.CompilerParams(needs_layout_passes=False),
        name="peer_hidden_sc",
    )
    def k(eid_hbm, xn_hbm, u_hbm, out_hbm, idx_v, x_v, o_v, bufs, acc_v, gsem, ssem, osem):
        wid = lax.axis_index("s") * info.num_cores + lax.axis_index("c")
        _sc_gather_stream(wid, n_batches, u_hbm.at[layer], [eid_hbm, xn_hbm], [idx_v, x_v], out_hbm, o_v, bufs,
                          gsem, ssem, osem, _hidden_compute(x_v, o_v, bufs, acc_v))

    return k(eid, xn, u_tabs)


def _peer_combine_sc(eid, a, v_tabs, layer):
    T = eid.shape[0]
    info, mesh, n_batches = _sc_mesh_and_batches(T)

    @functools.partial(
        pl.kernel, mesh=mesh,
        out_type=jax.ShapeDtypeStruct((T, D_MODEL), jnp.float32),
        scratch_types=[
            pltpu.VMEM((2, SC_TOKENS, N_SEL), jnp.int32),
            pltpu.VMEM((2, SC_TOKENS, N_SEL), jnp.float32),
            pltpu.VMEM((2, SC_TOKENS, D_MODEL), jnp.float32),
            pltpu.VMEM((GATHER_BUFS, GATHER_ROWS, D_MODEL), jnp.float32),
            pltpu.SemaphoreType.DMA((GATHER_BUFS,)),
            pltpu.SemaphoreType.DMA((2,)),
            pltpu.SemaphoreType.DMA((2,)),
        ],
        compiler_params=pltpu.CompilerParams(needs_layout_passes=False),
        name="peer_combine_sc",
    )
    def k(eid_hbm, a_hbm, v_hbm, out_hbm, idx_v, a_v, o_v, bufs, gsem, ssem, osem):
        wid = lax.axis_index("s") * info.num_cores + lax.axis_index("c")
        _sc_gather_stream(wid, n_batches, v_hbm.at[layer], [eid_hbm, a_hbm], [idx_v, a_v], out_hbm, o_v, bufs,
                          gsem, ssem, osem, _combine_compute(a_v, o_v, bufs))

    return k(eid, a, v_tabs)


def _peer_act_kernel(hp_ref, gate_ref, a_ref):
    a_ref[...] = _gelu(hp_ref[...]) * gate_ref[...]


ELEMENTWISE_ROWS = 1024


def _peer_act(hpre, gate):
    T = hpre.shape[0]
    rows = ELEMENTWISE_ROWS
    assert T % rows == 0
    spec = pl.BlockSpec((rows, N_SEL), lambda i: (i, 0))
    return pl.pallas_call(
        _peer_act_kernel, grid=(T // rows,), in_specs=[spec, spec], out_specs=spec,
        out_shape=jax.ShapeDtypeStruct((T, N_SEL), jnp.float32),
        compiler_params=pltpu.CompilerParams(dimension_semantics=("arbitrary",)),
        name="peer_act",
    )(hpre, gate)


def _residual_kernel(h_ref, c_ref, g_ref, o_ref, *, final_norm):
    y = h_ref[...] + c_ref[...]
    o_ref[...] = _rmsnorm_rows(y, g_ref[...]) if final_norm else y


def _residual(h, c, gfin, *, final_norm):
    T = h.shape[0]
    rows = ELEMENTWISE_ROWS
    assert T % rows == 0
    spec = pl.BlockSpec((rows, D_MODEL), lambda i: (i, 0))
    return pl.pallas_call(
        functools.partial(_residual_kernel, final_norm=final_norm), grid=(T // rows,),
        in_specs=[spec, spec, pl.BlockSpec((1, D_MODEL), lambda i: (0, 0))], out_specs=spec,
        out_shape=jax.ShapeDtypeStruct((T, D_MODEL), jnp.float32),
        compiler_params=pltpu.CompilerParams(dimension_semantics=("arbitrary",)),
        name="peer_residual",
    )(h, c, gfin.reshape(1, D_MODEL))


KV_WIDTH = N_KV_HEADS * HEAD_DIM
BF16 = jnp.bfloat16


def _qkv(h, gkv, gq, wkv_ref, wq_ref):
    kv = jnp.dot(_rmsnorm_rows(h, gkv).astype(BF16), wkv_ref[...], preferred_element_type=jnp.float32)
    q = jnp.dot(_rmsnorm_rows(h, gq).astype(BF16), wq_ref[...], preferred_element_type=jnp.float32)
    return kv, q


def _sink_softmax_pv(parts, sink):
    m = sink
    for s, _ in parts:
        m = jnp.maximum(m, jnp.max(s, axis=-1, keepdims=True))
    den = jnp.exp(sink - m)
    acc = None
    for s, v in parts:
        e = jnp.exp(s - m)
        den = den + jnp.sum(e, axis=-1, keepdims=True)
        pv = jnp.dot(e.astype(BF16), v.astype(BF16), preferred_element_type=jnp.float32)
        acc = pv if acc is None else acc + pv
    return acc / den


def _nt_dot(a, b):
    return lax.dot_general(a.astype(BF16), b.astype(BF16), (((1,), (1,)), ((), ())),
                           preferred_element_type=jnp.float32)


def _attn_prompt_kernel(h_ref, gkv_ref, gq_ref, wkv_ref, wq_ref, wo_ref, sink_ref,
                        o_ref, kvw_ref, prev_ref):
    blk = pl.program_id(1)
    h = h_ref[...]
    kv, q = _qkv(h, gkv_ref[...], gq_ref[...], wkv_ref, wq_ref)
    kvw_ref[0] = kv

    @pl.when(blk == 0)
    def _():
        prev_ref[...] = jnp.zeros_like(prev_ref)

    prev = prev_ref[...]
    qi = lax.broadcasted_iota(jnp.int32, (WINDOW, WINDOW), 0)
    kj = lax.broadcasted_iota(jnp.int32, (WINDOW, WINDOW), 1)
    prev_ok = jnp.logical_and(kj > qi, blk > 0)
    cur_ok = kj <= qi
    heads = []
    for kvh in range(N_KV_HEADS):
        ks = slice(kvh * HEAD_DIM, (kvh + 1) * HEAD_DIM)
        vs = slice(KV_WIDTH + kvh * HEAD_DIM, KV_WIDTH + (kvh + 1) * HEAD_DIM)
        for g in range(Q_PER_KV):
            hq = kvh * Q_PER_KV + g
            qh = q[:, hq * HEAD_DIM:(hq + 1) * HEAD_DIM]
            sp = jnp.where(prev_ok, _nt_dot(qh, prev[:, ks]) * ATTN_SCALE, -jnp.inf)
            sc = jnp.where(cur_ok, _nt_dot(qh, kv[:, ks]) * ATTN_SCALE, -jnp.inf)
            heads.append(_sink_softmax_pv([(sp, prev[:, vs]), (sc, kv[:, vs])], sink_ref[hq]))
    o = jnp.concatenate(heads, axis=1)
    o_ref[...] = h + jnp.dot(o.astype(BF16), wo_ref[...], preferred_element_type=jnp.float32)
    prev_ref[...] = kv


def _attn_prompt(h, gkv, gq, wkv, wq, wo, sinks, *, n_seq, seq_len):
    nb = seq_len // WINDOW
    row_spec = pl.BlockSpec((WINDOW, D_MODEL), lambda n, b: (n * nb + b, 0))
    full = lambda shape: pl.BlockSpec(shape, lambda n, b: (0,) * len(shape))
    return pl.pallas_call(
        _attn_prompt_kernel,
        grid=(n_seq, nb),
        in_specs=[
            row_spec, full((1, D_MODEL)), full((1, D_MODEL)),
            full((D_MODEL, 2 * KV_WIDTH)), full((D_MODEL, D_MODEL)), full((D_MODEL, D_MODEL)),
            pl.BlockSpec(memory_space=pltpu.SMEM),
        ],
        out_specs=[row_spec, pl.BlockSpec((1, WINDOW, 2 * KV_WIDTH), lambda n, b: (n, 0, 0))],
        out_shape=[
            jax.ShapeDtypeStruct(h.shape, jnp.float32),
            jax.ShapeDtypeStruct((n_seq, WINDOW, 2 * KV_WIDTH), jnp.float32),
        ],
        scratch_shapes=[pltpu.VMEM((WINDOW, 2 * KV_WIDTH), jnp.float32)],
        compiler_params=pltpu.CompilerParams(
            dimension_semantics=("arbitrary", "arbitrary"), vmem_limit_bytes=VMEM_LIMIT_BYTES),
        name="attn_prompt",
    )(h, gkv.reshape(1, D_MODEL), gq.reshape(1, D_MODEL), wkv, wq, wo, sinks)


ATTN_SEQS = 16
FIRST_PROMPT_PIECES = 4


def _attn_sample_kernel(h_ref, ck_ref, cv_ref, gkv_ref, gq_ref, wkv_ref, wq_ref, wo_ref, sink_ref,
                        o_ref, kw_ref, vw_ref, kv_ref, q_ref, att_ref, *, n_new):
    h = h_ref[...]
    kv, q = _qkv(h, gkv_ref[...], gq_ref[...], wkv_ref, wq_ref)
    kv_ref[...] = kv
    q_ref[...] = q
    n_seq = h.shape[0] // n_new
    rows = Q_PER_KV * n_new
    qpos_c = lax.broadcasted_iota(jnp.int32, (rows, WINDOW), 0) % n_new
    cache_ok = lax.broadcasted_iota(jnp.int32, (rows, WINDOW), 1) > qpos_c
    qpos_n = lax.broadcasted_iota(jnp.int32, (rows, n_new), 0) % n_new
    new_ok = lax.broadcasted_iota(jnp.int32, (rows, n_new), 1) <= qpos_n

    def per_seq(n, carry):
        r0 = pl.multiple_of(n * n_new, n_new)
        kvn = kv_ref[pl.ds(r0, n_new), :]
        qn = q_ref[pl.ds(r0, n_new), :]
        ck = ck_ref[n]
        cv = cv_ref[n]
        kw_ref[n, 0:WINDOW - n_new, :] = ck[n_new:, :]
        kw_ref[n, WINDOW - n_new:WINDOW, :] = kvn[:, 0:KV_WIDTH]
        vw_ref[n, 0:WINDOW - n_new, :] = cv[n_new:, :]
        vw_ref[n, WINDOW - n_new:WINDOW, :] = kvn[:, KV_WIDTH:2 * KV_WIDTH]
        outs = []
        for kvh in range(N_KV_HEADS):
            ks = slice(kvh * HEAD_DIM, (kvh + 1) * HEAD_DIM)
            vs = slice(KV_WIDTH + kvh * HEAD_DIM, KV_WIDTH + (kvh + 1) * HEAD_DIM)
            qs = jnp.concatenate(
                [qn[:, (kvh * Q_PER_KV + g) * HEAD_DIM:(kvh * Q_PER_KV + g + 1) * HEAD_DIM] for g in range(Q_PER_KV)],
                axis=0)
            s_c = jnp.where(cache_ok, _nt_dot(qs, ck[:, ks]) * ATTN_SCALE, -jnp.inf)
            s_n = jnp.where(new_ok, _nt_dot(qs, kvn[:, ks]) * ATTN_SCALE, -jnp.inf)
            sink = sink_ref[kvh][:, 0:1]
            o = _sink_softmax_pv([(s_c, cv[:, ks]), (s_n, kvn[:, vs])], sink)
            outs += [o[g * n_new:(g + 1) * n_new, :] for g in range(Q_PER_KV)]
        att_ref[pl.ds(r0, n_new), :] = jnp.concatenate(outs, axis=1)
        return carry

    lax.fori_loop(0, n_seq, per_seq, 0)
    o_ref[...] = h + jnp.dot(att_ref[...].astype(BF16), wo_ref[...], preferred_element_type=jnp.float32)


def _attn_sample(h, cache_k, cache_v, gkv, gq, wkv, wq, wo, sinks, *, n_new):
    n_seq = cache_k.shape[0]
    sb = ATTN_SEQS
    rows = sb * n_new
    row_spec = pl.BlockSpec((rows, D_MODEL), lambda i: (i, 0))
    win_spec = pl.BlockSpec((sb, WINDOW, KV_WIDTH), lambda i: (i, 0, 0))
    full = lambda shape: pl.BlockSpec(shape, lambda i: (0,) * len(shape))
    sink_rows = jnp.repeat(sinks.reshape(N_KV_HEADS, Q_PER_KV), n_new, axis=1)[:, :, None]
    sink_rows = jnp.broadcast_to(sink_rows, (N_KV_HEADS, Q_PER_KV * n_new, LANES))
    return pl.pallas_call(
        functools.partial(_attn_sample_kernel, n_new=n_new),
        grid=(n_seq // sb,),
        in_specs=[
            row_spec, win_spec, win_spec, full((1, D_MODEL)), full((1, D_MODEL)),
            full((D_MODEL, 2 * KV_WIDTH)), full((D_MODEL, D_MODEL)), full((D_MODEL, D_MODEL)),
            full((N_KV_HEADS, Q_PER_KV * n_new, LANES)),
        ],
        out_specs=[row_spec, win_spec, win_spec],
        out_shape=[
            jax.ShapeDtypeStruct(h.shape, jnp.float32),
            jax.ShapeDtypeStruct((n_seq, WINDOW, KV_WIDTH), jnp.float32),
            jax.ShapeDtypeStruct((n_seq, WINDOW, KV_WIDTH), jnp.float32),
        ],
        scratch_shapes=[
            pltpu.VMEM((rows, 2 * KV_WIDTH), jnp.float32),
            pltpu.VMEM((rows, D_MODEL), jnp.float32),
            pltpu.VMEM((rows, D_MODEL), jnp.float32),
        ],
        compiler_params=pltpu.CompilerParams(
            dimension_semantics=("arbitrary",), vmem_limit_bytes=VMEM_LIMIT_BYTES),
        name="attn_sample",
    )(h, cache_k, cache_v, gkv.reshape(1, D_MODEL), gq.reshape(1, D_MODEL), wkv, wq, wo, sink_rows)


def kernel(x_prompt, x_sample, state_ssm_re, state_ssm_im, cache_k_win, cache_v_win, norm_mix, norm_ffn, norm_kv, norm_final, ssm_lam_re, ssm_lam_im, ssm_log_dt, ssm_b_re, ssm_b_im, ssm_c_re, ssm_c_im, ssm_d, ssm_w_glu, w_kv, w_q, attn_sinks, w_o, peer_w_q, peer_sub_keys, peer_u, peer_v):
    bmat, cmat, apr, api = _s5_discretize(ssm_lam_re[0], ssm_lam_im[0], ssm_log_dt[0], ssm_b_re[0], ssm_b_im[0], ssm_c_re[0], ssm_c_im[0])
    cmat = cmat.astype(jnp.bfloat16)
    wglu = ssm_w_glu[0].astype(jnp.bfloat16)
    wkv = w_kv.astype(BF16)
    wq = w_q[0].astype(BF16)
    wo = w_o[0].astype(BF16)
    peer_wq = [peer_w_q[layer].astype(BF16).reshape(D_MODEL, PEER_HEADS, 2 * PEER_DHALF).transpose(1, 0, 2)
               for layer in range(2)]
    peer_sk = [peer_sub_keys[layer].astype(BF16) for layer in range(2)]

    def route(h, layer):
        xn, eid_t, gate_t = _peer_route(h, norm_ffn[layer], peer_wq[layer], peer_sk[layer])
        return dict(h=h, xn=xn, eid=eid_t.T, gate=gate_t.T, layer=layer)

    win = lambda a: a.reshape(a.shape[0], WINDOW, N_KV_HEADS, HEAD_DIM)

    seq_len = x_prompt.shape[1]
    xp_all = x_prompt.reshape(-1, D_MODEL)

    n_prompt, n_s = x_prompt.shape[0], x_sample.shape[0]
    xs = x_sample.reshape(-1, D_MODEL)
    z, sre_s, sim_s = _s5_mixer(xs, _state_to_chunks(state_ssm_re[0]), _state_to_chunks(state_ssm_im[0]), norm_mix[0],
                                ssm_d[0], bmat, cmat, apr, api, chain=False, seq_len=x_sample.shape[1])
    units = [route(_glu_residual(z, xs, wglu), 0)]
    group_units = [[0]]
    prompt_states = []
    for seq in range(n_prompt):
        n_pieces = FIRST_PROMPT_PIECES if seq == 0 else 1
        piece = seq_len // n_pieces
        sr = si = jnp.zeros((1, N_CHUNKS, 1, CHUNK_STATE), jnp.float32)
        group_units.append([])
        for p in range(n_pieces):
            row0 = seq * seq_len + p * piece
            z, sr, si = _s5_mixer(xp_all, sr, si, norm_mix[0], ssm_d[0], bmat, cmat, apr, api,
                                  chain=True, seq_len=piece, row0=row0)
            group_units[-1].append(len(units))
            units.append(route(_glu_residual(z, xp_all, wglu, res_row0=row0), 0))
        prompt_states.append((sr, si))
    n_layer0 = len(units)
    windows = {}

    def layer1_unit(g):
        mine = [units[i] for i in group_units[g]]
        h2 = jnp.concatenate([_residual(u['h'], u['c'], norm_final, final_norm=False) for u in mine], axis=0)
        if g == 0:
            h3, kw, vw = _attn_sample(h2, cache_k_win.reshape(n_s, WINDOW, KV_WIDTH),
                                      cache_v_win.reshape(n_s, WINDOW, KV_WIDTH), norm_kv, norm_mix[1],
                                      wkv, wq, wo, attn_sinks[0], n_new=x_sample.shape[1])
            windows[g] = (win(kw), win(vw))
        else:
            h3, kvw = _attn_prompt(h2, norm_kv, norm_mix[1], wkv, wq, wo, attn_sinks[0], n_seq=1, seq_len=seq_len)
            windows[g] = (win(kvw[:, :, :KV_WIDTH]), win(kvw[:, :, KV_WIDTH:]))
        return route(h3, 1)

    layer1_order = list(range(1, n_prompt + 1)) + [0]
    n_units = n_layer0 + 1 + n_prompt
    for k in range(n_units + 2):
        if n_layer0 <= k < n_units:
            units.append(layer1_unit(layer1_order[k - n_layer0]))
        hid = units[k] if k < n_units else None
        comb = units[k - 2] if k >= 2 else None
        if comb is not None:
            comb['a'] = _peer_act(comb['hpre'], comb['gate'])
        if hid is not None and comb is not None:
            comb['c'], hid['hpre'] = _peer_step_sc(comb['eid'], comb['a'], comb['layer'],
                                                   hid['eid'], hid['xn'], hid['layer'], peer_u, peer_v)
        elif hid is not None:
            hid['hpre'] = _peer_hidden_sc(hid['eid'], hid['xn'], peer_u, hid['layer'])
        else:
            comb['c'] = _peer_combine_sc(comb['eid'], comb['a'], peer_v, comb['layer'])

    ys = [_residual(u['h'], u['c'], norm_final, final_norm=True) for u in units[n_layer0:]]
    y_s = ys[-1].reshape(x_sample.shape)
    y_p = jnp.concatenate(ys[:-1], axis=0).reshape(x_prompt.shape)
    cat = lambda parts: jnp.concatenate(parts, axis=0)
    sre_p = cat([_chunks_to_state(sr) for sr, _ in prompt_states])
    sim_p = cat([_chunks_to_state(si) for _, si in prompt_states])
    kw_p = cat([windows[g][0] for g in range(1, n_prompt + 1)])
    vw_p = cat([windows[g][1] for g in range(1, n_prompt + 1)])
    return (y_p, y_s, sre_p[None], sim_p[None], kw_p, vw_p,
            _chunks_to_state(sre_s)[None], _chunks_to_state(sim_s)[None], windows[0][0], windows[0][1])
```

```python
import functools
import math

import jax
import jax.numpy as jnp
from jax import lax
from jax.experimental import pallas as pl
from jax.experimental.pallas import tpu as pltpu
from jax.experimental.pallas import tpu_sc as plsc

D_MODEL = 1024
GROUP_SIZE = 16
N_GROUPS = D_MODEL // GROUP_SIZE
STATE_DIM = 64
HEAD_DIM = 64
N_Q_HEADS = D_MODEL // HEAD_DIM
N_KV_HEADS = N_Q_HEADS // 8
Q_PER_KV = N_Q_HEADS // N_KV_HEADS
WINDOW = 128
ATTN_SCALE = 1.0 / math.sqrt(HEAD_DIM)
PEER_HEADS = 8
N_KEYS = 128
PEER_TOPK = 16
PEER_DHALF = 128
EPS = 1e-5

LANES = 128
SUBLANES = 8
VMEM_LIMIT_BYTES = 56 * 1024 * 1024

GROUPS_PER_CHUNK = LANES // GROUP_SIZE
N_CHUNKS = N_GROUPS // GROUPS_PER_CHUNK
CHUNK_STATE = GROUPS_PER_CHUNK * STATE_DIM
S5_ROWS = 256


def _rmsnorm_rows(x, g):
    r = lax.rsqrt(jnp.mean(x * x, axis=-1, keepdims=True) + EPS)
    return x * r * g


def _gelu(x):
    return 0.5 * x * (1.0 + lax.erf(x * (1.0 / math.sqrt(2.0))))


def _s5_discretize(lam_re, lam_im, log_dt, b_re, b_im, c_re, c_im):
    f32 = jnp.float32
    lr = lam_re.astype(f32)
    li = lam_im.astype(f32)
    dt = jnp.exp(log_dt.astype(f32))[:, None]
    mag = jnp.exp(lr * dt)
    ab_re = mag * jnp.cos(li * dt)
    ab_im = mag * jnp.sin(li * dt)
    den = lr * lr + li * li
    f_re = ((ab_re - 1.0) * lr + ab_im * li) / den
    f_im = (ab_im * lr - (ab_re - 1.0) * li) / den
    br = b_re.astype(f32)
    bi = b_im.astype(f32)
    bb_re = f_re[..., None] * br - f_im[..., None] * bi
    bb_im = f_re[..., None] * bi + f_im[..., None] * br
    eye = jnp.eye(GROUPS_PER_CHUNK, dtype=f32)

    def chunk_rows(v):
        return v.reshape(N_CHUNKS, 1, CHUNK_STATE)

    def in_blocks(bb):
        t = bb.reshape(N_CHUNKS, GROUPS_PER_CHUNK, STATE_DIM, GROUP_SIZE).transpose(0, 1, 3, 2)
        return jnp.einsum('mgjp,gh->mgjhp', t, eye).reshape(N_CHUNKS, LANES, CHUNK_STATE)

    def out_blocks(c):
        t = c.astype(f32).reshape(N_CHUNKS, GROUPS_PER_CHUNK, GROUP_SIZE, STATE_DIM).transpose(0, 1, 3, 2)
        return jnp.einsum('mgpj,gh->mgphj', t, eye).reshape(N_CHUNKS, CHUNK_STATE, LANES)

    bfull = jnp.concatenate([in_blocks(bb_re), in_blocks(bb_im)], axis=2)
    b_hi = bfull.astype(jnp.bfloat16)
    b_lo = (bfull - b_hi.astype(f32)).astype(jnp.bfloat16)
    bmat = (jnp.concatenate([b_hi, b_hi], axis=1), b_lo)
    cmat = jnp.concatenate([out_blocks(c_re), -out_blocks(c_im)], axis=1)
    pr, pi = [ab_re], [ab_im]
    for _ in range(SUBLANES - 1):
        pr, pi = pr + [pr[-1] * ab_re - pi[-1] * ab_im], pi + [pr[-1] * ab_im + pi[-1] * ab_re]
    apr = jnp.concatenate([chunk_rows(v) for v in pr], axis=1)
    api = jnp.concatenate([chunk_rows(v) for v in pi], axis=1)
    return bmat, cmat, apr, api


def _split_bf16(x):
    hi = x.astype(jnp.bfloat16)
    return hi, (x - hi.astype(jnp.float32)).astype(jnp.bfloat16)


def _s5_kernel(x_ref, g_ref, d_ref, bhh_ref, blo_ref, c_ref, apr_ref, api_ref, h0r_ref, h0i_ref,
               z_ref, sr_ref, si_ref, u_ref, us_ref, st_ref, cr_ref, ci_ref, *, chain, blocks_per_seq):
    rb = pl.program_id(0)
    m = pl.program_id(1)
    rows = x_ref.shape[0]
    n_tiles = rows // SUBLANES

    @pl.when(m == 0)
    def _():
        u = _rmsnorm_rows(x_ref[...], g_ref[...])
        for mm in range(N_CHUNKS):
            uc = u[:, mm * LANES:(mm + 1) * LANES]
            u_ref[mm] = uc
            hi, lo = _split_bf16(uc)
            us_ref[mm, :, 0:LANES] = hi
            us_ref[mm, :, LANES:2 * LANES] = lo

    u = u_ref[m]
    us = us_ref[m]
    st_ref[...] = (jnp.dot(us, bhh_ref[m], preferred_element_type=jnp.float32)
                   + jnp.dot(us[:, 0:LANES], blo_ref[m], preferred_element_type=jnp.float32))

    apr = apr_ref[0]
    api = api_ref[0]
    row = lax.broadcasted_iota(jnp.int32, (SUBLANES, CHUNK_STATE), 0)

    if chain:
        @pl.when(rb % blocks_per_seq == 0)
        def _():
            cr_ref[m] = h0r_ref[0, 0]
            ci_ref[m] = h0i_ref[0, 0]

    def tile_step(k, carry):
        r0 = pl.multiple_of(k * SUBLANES, SUBLANES)
        xr = st_ref[pl.ds(r0, SUBLANES), 0:CHUNK_STATE]
        xi = st_ref[pl.ds(r0, SUBLANES), CHUNK_STATE:2 * CHUNK_STATE]
        for d in (1, 2, 4):
            ar = apr[d - 1:d, :]
            ai = api[d - 1:d, :]
            sr = jnp.where(row >= d, pltpu.roll(xr, d, axis=0), 0.0)
            si = jnp.where(row >= d, pltpu.roll(xi, d, axis=0), 0.0)
            xr, xi = xr + ar * sr - ai * si, xi + ar * si + ai * sr
        if chain:
            cr, ci = carry
        else:
            cr = h0r_ref[k, 0]
            ci = h0i_ref[k, 0]
        hr = xr + apr * cr - api * ci
        hi = xi + apr * ci + api * cr
        st_ref[pl.ds(r0, SUBLANES), 0:CHUNK_STATE] = hr
        st_ref[pl.ds(r0, SUBLANES), CHUNK_STATE:2 * CHUNK_STATE] = hi
        lr_ = hr[SUBLANES - 1:SUBLANES, :]
        li_ = hi[SUBLANES - 1:SUBLANES, :]
        if chain:
            return lr_, li_
        sr_ref[k, m] = lr_
        si_ref[k, m] = li_
        return carry

    if chain:
        cr, ci = lax.fori_loop(0, n_tiles, tile_step, (cr_ref[m], ci_ref[m]))
        cr_ref[m] = cr
        ci_ref[m] = ci
        sr_ref[0, m] = cr
        si_ref[0, m] = ci
    else:
        lax.fori_loop(0, n_tiles, tile_step, 0)

    y = jnp.dot(st_ref[...].astype(jnp.bfloat16), c_ref[m], preferred_element_type=jnp.float32)
    y = y + d_ref[0] * u
    z_ref[...] = _gelu(y)


def _s5_mixer(x, h0r, h0i, g, d_skip, bmat, cmat_bf16, apr, api, *, chain, seq_len, row0=0):
    nseq = h0r.shape[0]
    T = nseq * seq_len
    rows = S5_ROWS
    blk0 = row0 // rows
    if chain:
        blocks_per_seq = seq_len // rows
        seq_blk = 1
        seq_map = lambda rb, m: (rb // blocks_per_seq, m, 0, 0)
        out_map = lambda rb, m: (rb // blocks_per_seq, 0, 0, 0)
    else:
        assert seq_len == SUBLANES
        blocks_per_seq = 1
        seq_blk = rows // SUBLANES
        seq_map = lambda rb, m: (rb, m, 0, 0)
        out_map = lambda rb, m: (rb, 0, 0, 0)
    grid = (T // rows, N_CHUNKS)
    kern = functools.partial(_s5_kernel, chain=chain, blocks_per_seq=blocks_per_seq)
    st_spec = pl.BlockSpec((seq_blk, 1, 1, CHUNK_STATE), seq_map)
    out_st_spec = pl.BlockSpec((seq_blk, N_CHUNKS, 1, CHUNK_STATE), out_map)
    z, sr, si = pl.pallas_call(
        kern,
        grid=grid,
        in_specs=[
            pl.BlockSpec((rows, D_MODEL), lambda rb, m: (rb + blk0, 0)),
            pl.BlockSpec((1, D_MODEL), lambda rb, m: (0, 0)),
            pl.BlockSpec((1, 1, LANES), lambda rb, m: (m, 0, 0)),
            pl.BlockSpec((N_CHUNKS, 2 * LANES, 2 * CHUNK_STATE), lambda rb, m: (0, 0, 0)),
            pl.BlockSpec((N_CHUNKS, LANES, 2 * CHUNK_STATE), lambda rb, m: (0, 0, 0)),
            pl.BlockSpec((N_CHUNKS, 2 * CHUNK_STATE, LANES), lambda rb, m: (0, 0, 0)),
            pl.BlockSpec((1, SUBLANES, CHUNK_STATE), lambda rb, m: (m, 0, 0)),
            pl.BlockSpec((1, SUBLANES, CHUNK_STATE), lambda rb, m: (m, 0, 0)),
            st_spec, st_spec,
        ],
        out_specs=[
            pl.BlockSpec((rows, LANES), lambda rb, m: (rb, m)),
            out_st_spec, out_st_spec,
        ],
        out_shape=[
            jax.ShapeDtypeStruct((T, D_MODEL), jnp.float32),
            jax.ShapeDtypeStruct((nseq, N_CHUNKS, 1, CHUNK_STATE), jnp.float32),
            jax.ShapeDtypeStruct((nseq, N_CHUNKS, 1, CHUNK_STATE), jnp.float32),
        ],
        scratch_shapes=[
            pltpu.VMEM((N_CHUNKS, rows, LANES), jnp.float32),
            pltpu.VMEM((N_CHUNKS, rows, 2 * LANES), jnp.bfloat16),
            pltpu.VMEM((rows, 2 * CHUNK_STATE), jnp.float32),
            pltpu.VMEM((N_CHUNKS, 1, CHUNK_STATE), jnp.float32),
            pltpu.VMEM((N_CHUNKS, 1, CHUNK_STATE), jnp.float32),
        ],
        compiler_params=pltpu.CompilerParams(
            dimension_semantics=("arbitrary", "arbitrary"), vmem_limit_bytes=VMEM_LIMIT_BYTES),
        name="s5_mixer",
    )(x, g.reshape(1, D_MODEL), d_skip.reshape(N_CHUNKS, 1, LANES), bmat[0], bmat[1], cmat_bf16, apr, api,
      h0r, h0i)
    return z, sr, si


def _state_to_chunks(h):
    return h.reshape(h.shape[0], N_CHUNKS, 1, CHUNK_STATE)


def _chunks_to_state(s):
    return s.reshape(s.shape[0], N_GROUPS, STATE_DIM)


def _glu_kernel(z_ref, res_ref, w_ref, o_ref):
    zz = jnp.dot(z_ref[...].astype(jnp.bfloat16), w_ref[...], preferred_element_type=jnp.float32)
    a = zz[:, :D_MODEL]
    b = zz[:, D_MODEL:]
    o_ref[...] = res_ref[...] + a * (1.0 / (1.0 + jnp.exp(-b)))


def _glu_residual(z, res, w_bf16, res_row0=0):
    T = z.shape[0]
    rows = math.gcd(T, 512)
    blk0 = res_row0 // rows
    return pl.pallas_call(
        _glu_kernel,
        grid=(T // rows,),
        in_specs=[
            pl.BlockSpec((rows, D_MODEL), lambda i: (i, 0)),
            pl.BlockSpec((rows, D_MODEL), lambda i: (i + blk0, 0)),
            pl.BlockSpec((D_MODEL, 2 * D_MODEL), lambda i: (0, 0)),
        ],
        out_specs=pl.BlockSpec((rows, D_MODEL), lambda i: (i, 0)),
        out_shape=jax.ShapeDtypeStruct((T, D_MODEL), jnp.float32),
        compiler_params=pltpu.CompilerParams(
            dimension_semantics=("arbitrary",), vmem_limit_bytes=VMEM_LIMIT_BYTES),
        name="glu_residual",
    )(z, res, w_bf16)


ROUTE_ROWS = 256


def _topk_rows(s, payload):
    n_rows = s.shape[0]
    row = lax.broadcasted_iota(jnp.int32, s.shape, 0)
    vals, picks = [], []
    for _ in range(PEER_TOPK):
        m = jnp.max(s, axis=0, keepdims=True)
        pos = jnp.min(jnp.where(s == m, row, n_rows), axis=0, keepdims=True)
        sel = row == pos
        vals.append(m)
        if payload is None:
            picks.append(pos)
        else:
            picks.append(jnp.max(jnp.where(sel, payload, -1), axis=0, keepdims=True))
        s = jnp.where(sel, -jnp.inf, s)
    return jnp.concatenate(vals, axis=0), jnp.concatenate(picks, axis=0)


def _pair_rows(a0, a1, combine):
    half = PEER_TOPK // 2
    parts = [combine(a0[0:1, :], a1)]
    parts += [combine(a0[i:i + 1, :], a1[0:half, :]) for i in range(1, half)]
    parts.append(combine(a0[half:PEER_TOPK, :], a1[0:1, :]))
    return jnp.concatenate(parts, axis=0)


def _peer_route_kernel(h_ref, g_ref, wq_ref, sk_ref, xn_ref, eid_ref, gate_ref, xb_ref):
    hd = pl.program_id(1)

    @pl.when(hd == 0)
    def _():
        xn = _rmsnorm_rows(h_ref[...], g_ref[...])
        xn_ref[...] = xn
        xb_ref[...] = xn.astype(jnp.bfloat16)

    q = jnp.dot(xb_ref[...], wq_ref[hd], preferred_element_type=jnp.float32)
    sv, si = [], []
    for c in range(2):
        qc = q[:, c * PEER_DHALF:(c + 1) * PEER_DHALF].astype(jnp.bfloat16)
        st = lax.dot_general(sk_ref[hd, c], qc, (((1,), (1,)), ((), ())), preferred_element_type=jnp.float32)
        v, i = _topk_rows(st, None)
        sv.append(v)
        si.append(i)
    cand = _pair_rows(sv[0], sv[1], lambda a, b: a + b)
    cid = _pair_rows(si[0], si[1], lambda a, b: a * N_KEYS + b)
    fv, eid = _topk_rows(cand, cid)
    e = jnp.exp(fv - fv[0:1, :])
    gate_ref[...] = e / jnp.sum(e, axis=0, keepdims=True)
    eid_ref[...] = eid


def _peer_route(h, g, wq_bf16, sk_bf16):
    T = h.shape[0]
    rows = ROUTE_ROWS
    n_sel = PEER_HEADS * PEER_TOPK
    return pl.pallas_call(
        _peer_route_kernel,
        grid=(T // rows, PEER_HEADS),
        in_specs=[
            pl.BlockSpec((rows, D_MODEL), lambda tb, hd: (tb, 0)),
            pl.BlockSpec((1, D_MODEL), lambda tb, hd: (0, 0)),
            pl.BlockSpec((PEER_HEADS, D_MODEL, 2 * PEER_DHALF), lambda tb, hd: (0, 0, 0)),
            pl.BlockSpec((PEER_HEADS, 2, N_KEYS, PEER_DHALF), lambda tb, hd: (0, 0, 0, 0)),
        ],
        out_specs=[
            pl.BlockSpec((rows, D_MODEL), lambda tb, hd: (tb, 0)),
            pl.BlockSpec((PEER_TOPK, rows), lambda tb, hd: (hd, tb)),
            pl.BlockSpec((PEER_TOPK, rows), lambda tb, hd: (hd, tb)),
        ],
        out_shape=[
            jax.ShapeDtypeStruct((T, D_MODEL), jnp.float32),
            jax.ShapeDtypeStruct((n_sel, T), jnp.int32),
            jax.ShapeDtypeStruct((n_sel, T), jnp.float32),
        ],
        scratch_shapes=[pltpu.VMEM((rows, D_MODEL), jnp.bfloat16)],
        compiler_params=pltpu.CompilerParams(
            dimension_semantics=("arbitrary", "arbitrary"), vmem_limit_bytes=VMEM_LIMIT_BYTES),
        name="peer_route",
    )(h, g.reshape(1, D_MODEL), wq_bf16, sk_bf16)


N_SEL = PEER_HEADS * PEER_TOPK
SC_LANES = 16
GATHER_ROWS = PEER_TOPK
GATHERS_PER_TOKEN = N_SEL // GATHER_ROWS
GATHER_BUFS = 4
SC_TOKENS = 8
ACC_STRIDE = SC_LANES + 1


def _stage_copies(wid, n_batches, stage_srcs, stage_bufs, ssem, bi, slot):
    base = (wid * n_batches + bi) * SC_TOKENS
    return [pltpu.make_async_copy(src.at[pl.ds(base, SC_TOKENS)], buf.at[slot], ssem.at[slot])
            for src, buf in zip(stage_srcs, stage_bufs)]


def _sc_gather_stream(wid, n_batches, tab_hbm, stage_srcs, stage_bufs, out_hbm, o_v, bufs, gsem, ssem, osem,
                      compute, handed_over=False, hand_over=None):
    idx_v = stage_bufs[0]

    def stage_copies(bi, slot):
        return _stage_copies(wid, n_batches, stage_srcs, stage_bufs, ssem, bi, slot)

    def out_copy(bi, slot):
        base = (wid * n_batches + bi) * SC_TOKENS
        return pltpu.make_async_copy(o_v.at[slot], out_hbm.at[pl.ds(base, SC_TOKENS)], osem.at[slot])

    def start(slot, t, kk, b):
        idx = idx_v[slot, t, pl.ds(kk * GATHER_ROWS, GATHER_ROWS)]
        pltpu.async_copy(tab_hbm.at[idx], bufs.at[b], gsem.at[b])

    def wait(b):
        pltpu.make_async_copy(tab_hbm.at[pl.ds(0, GATHER_ROWS)], bufs.at[b], gsem.at[b]).wait()

    if not handed_over:
        for c in stage_copies(0, 0):
            c.start()
        for c in stage_copies(0, 0):
            c.wait()
        for q in range(GATHER_BUFS - 1):
            start(0, q // GATHERS_PER_TOKEN, q % GATHERS_PER_TOKEN, q % GATHER_BUFS)

    def batch(bi, carry):
        slot = bi % 2
        has_next = bi + 1 < n_batches

        @pl.when(has_next)
        def _():
            for c in stage_copies(bi + 1, 1 - slot):
                c.start()

        @pl.when(bi >= 2)
        def _():
            out_copy(bi, slot).wait()

        def tok(t, carry):
            @pl.when(jnp.logical_and(t == SC_TOKENS - 1, has_next))
            def _():
                for c in stage_copies(bi + 1, 1 - slot):
                    c.wait()

            for kk in range(GATHERS_PER_TOKEN):
                nq = kk + GATHER_BUFS - 1
                nk, nb = nq % GATHERS_PER_TOKEN, nq % GATHER_BUFS
                if nq < GATHERS_PER_TOKEN:
                    start(slot, t, nk, nb)
                else:
                    @pl.when(t + 1 < SC_TOKENS)
                    def _():
                        start(slot, t + 1, nk, nb)

                    @pl.when(jnp.logical_and(t + 1 == SC_TOKENS, has_next))
                    def _():
                        start(1 - slot, 0, nk, nb)

                    if hand_over is not None:
                        @pl.when(jnp.logical_and(t + 1 == SC_TOKENS, jnp.logical_not(has_next)))
                        def _():
                            hand_over(nk, nb)

                wait(kk % GATHER_BUFS)
                compute(slot, t, kk, kk % GATHER_BUFS)
            return carry

        lax.fori_loop(0, SC_TOKENS, tok, 0)
        out_copy(bi, slot).start()
        return carry

    lax.fori_loop(0, n_batches, batch, 0)
    if n_batches >= 2:
        out_copy(n_batches - 2, (n_batches - 2) % 2).wait()
    out_copy(n_batches - 1, (n_batches - 1) % 2).wait()


def _sc_mesh_and_batches(n_tokens):
    info = plsc.get_sparse_core_info()
    assert info.num_lanes == SC_LANES
    n_workers = info.num_cores * info.num_subcores
    assert n_tokens % (n_workers * SC_TOKENS) == 0
    mesh = plsc.VectorSubcoreMesh(core_axis_name="c", subcore_axis_name="s")
    return info, mesh, n_tokens // (n_workers * SC_TOKENS)


def _hidden_compute(x_v, o_v, bufs, acc_v):
    lane = lax.iota(jnp.int32, SC_LANES)
    zero = jnp.zeros((SC_LANES,), jnp.float32)

    def compute(slot, t, kk, b):
        @plsc.parallel_loop(0, D_MODEL // SC_LANES, carry=(zero,) * GATHER_ROWS)
        def accs(c, accs):
            xc = x_v[slot, t, pl.ds(c * SC_LANES, SC_LANES)]
            return tuple(accs[r] + bufs[b, r, pl.ds(c * SC_LANES, SC_LANES)] * xc for r in range(GATHER_ROWS))

        for r in range(GATHER_ROWS):
            acc_v[pl.ds(r * ACC_STRIDE, SC_LANES)] = accs[r]
        tot = zero
        for c in range(SC_LANES):
            tot = tot + plsc.load_gather(acc_v, [lane * ACC_STRIDE + c])
        o_v[slot, t, pl.ds(kk * GATHER_ROWS, GATHER_ROWS)] = tot

    return compute


def _combine_compute(a_v, o_v, bufs):
    def compute(slot, t, kk, b):
        svec = jnp.full((SC_LANES,), slot, jnp.int32)
        tvec = jnp.full((SC_LANES,), t, jnp.int32)
        ws = [plsc.load_gather(a_v, [svec, tvec, jnp.full((SC_LANES,), kk * GATHER_ROWS + r, jnp.int32)])
              for r in range(GATHER_ROWS)]

        @plsc.parallel_loop(0, D_MODEL // SC_LANES, unroll=2)
        def _(c):
            sl = pl.ds(c * SC_LANES, SC_LANES)
            terms = [ws[r] * bufs[b, r, sl] for r in range(GATHER_ROWS)]
            if kk != 0:
                terms.append(o_v[slot, t, sl])
            while len(terms) > 1:
                pairs = [terms[i] + terms[i + 1] for i in range(0, len(terms) - 1, 2)]
                terms = pairs + ([terms[-1]] if len(terms) % 2 else [])
            o_v[slot, t, sl] = terms[0]

    return compute


def _peer_step_sc(eid_c, a, layer_c, eid_h, xn, layer_h, u_tabs, v_tabs):
    Tc, Th = eid_c.shape[0], eid_h.shape[0]
    info, mesh, nb_c = _sc_mesh_and_batches(Tc)
    _, _, nb_h = _sc_mesh_and_batches(Th)

    @functools.partial(
        pl.kernel, mesh=mesh,
        out_type=(jax.ShapeDtypeStruct((Tc, D_MODEL), jnp.float32), jax.ShapeDtypeStruct((Th, N_SEL), jnp.float32)),
        scratch_types=[
            pltpu.VMEM((2, SC_TOKENS, N_SEL), jnp.int32),
            pltpu.VMEM((2, SC_TOKENS, N_SEL), jnp.float32),
            pltpu.VMEM((2, SC_TOKENS, D_MODEL), jnp.float32),
            pltpu.VMEM((2, SC_TOKENS, N_SEL), jnp.int32),
            pltpu.VMEM((2, SC_TOKENS, D_MODEL), jnp.float32),
            pltpu.VMEM((2, SC_TOKENS, N_SEL), jnp.float32),
            pltpu.VMEM((GATHER_BUFS, GATHER_ROWS, D_MODEL), jnp.float32),
            pltpu.VMEM((GATHER_ROWS * ACC_STRIDE,), jnp.float32),
            pltpu.SemaphoreType.DMA((GATHER_BUFS,)),
            pltpu.SemaphoreType.DMA((2,)),
            pltpu.SemaphoreType.DMA((2,)),
            pltpu.SemaphoreType.DMA((2,)),
            pltpu.SemaphoreType.DMA((2,)),
        ],
        compiler_params=pltpu.CompilerParams(needs_layout_passes=False),
        name="peer_step_sc",
    )
    def k(eidc_hbm, a_hbm, eidh_hbm, xn_hbm, u_hbm, v_hbm, c_hbm, hp_hbm,
          idxc_v, a_v, oc_v, idxh_v, x_v, oh_v, bufs, acc_v, gsem, ssem_c, osem_c, ssem_h, osem_h):
        wid = lax.axis_index("s") * info.num_cores + lax.axis_index("c")
        u_tab = u_hbm.at[layer_h]
        first_h = lambda: _stage_copies(wid, nb_h, [eidh_hbm, xn_hbm], [idxh_v, x_v], ssem_h, 0, 0)
        for c in first_h():
            c.start()

        def hand_over(head, b):
            if head == 0:
                for c in first_h():
                    c.wait()
            idx = idxh_v[0, 0, pl.ds(head * GATHER_ROWS, GATHER_ROWS)]
            pltpu.async_copy(u_tab.at[idx], bufs.at[b], gsem.at[b])

        _sc_gather_stream(wid, nb_c, v_hbm.at[layer_c], [eidc_hbm, a_hbm], [idxc_v, a_v], c_hbm, oc_v, bufs,
                          gsem, ssem_c, osem_c, _combine_compute(a_v, oc_v, bufs), hand_over=hand_over)
        _sc_gather_stream(wid, nb_h, u_tab, [eidh_hbm, xn_hbm], [idxh_v, x_v], hp_hbm, oh_v, bufs,
                          gsem, ssem_h, osem_h, _hidden_compute(x_v, oh_v, bufs, acc_v), handed_over=True)

    return k(eid_c, a, eid_h, xn, u_tabs, v_tabs)


def _peer_hidden_sc(eid, xn, u_tabs, layer):
    T = eid.shape[0]
    info, mesh, n_batches = _sc_mesh_and_batches(T)

    @functools.partial(
        pl.kernel, mesh=mesh,
        out_type=jax.ShapeDtypeStruct((T, N_SEL), jnp.float32),
        scratch_types=[
            pltpu.VMEM((2, SC_TOKENS, N_SEL), jnp.int32),
            pltpu.VMEM((2, SC_TOKENS, D_MODEL), jnp.float32),
            pltpu.VMEM((2, SC_TOKENS, N_SEL), jnp.float32),
            pltpu.VMEM((GATHER_BUFS, GATHER_ROWS, D_MODEL), jnp.float32),
            pltpu.VMEM((GATHER_ROWS * ACC_STRIDE,), jnp.float32),
            pltpu.SemaphoreType.DMA((GATHER_BUFS,)),
            pltpu.SemaphoreType.DMA((2,)),
            pltpu.SemaphoreType.DMA((2,)),
        ],
        compiler_params=pltpu.CompilerParams(needs_layout_passes=False),
        name="peer_hidden_sc",
    )
    def k(eid_hbm, xn_hbm, u_hbm, out_hbm, idx_v, x_v, o_v, bufs, acc_v, gsem, ssem, osem):
        wid = lax.axis_index("s") * info.num_cores + lax.axis_index("c")
        _sc_gather_stream(wid, n_batches, u_hbm.at[layer], [eid_hbm, xn_hbm], [idx_v, x_v], out_hbm, o_v, bufs,
                          gsem, ssem, osem, _hidden_compute(x_v, o_v, bufs, acc_v))

    return k(eid, xn, u_tabs)


def _peer_combine_sc(eid, a, v_tabs, layer):
    T = eid.shape[0]
    info, mesh, n_batches = _sc_mesh_and_batches(T)

    @functools.partial(
        pl.kernel, mesh=mesh,
        out_type=jax.ShapeDtypeStruct((T, D_MODEL), jnp.float32),
        scratch_types=[
            pltpu.VMEM((2, SC_TOKENS, N_SEL), jnp.int32),
            pltpu.VMEM((2, SC_TOKENS, N_SEL), jnp.float32),
            pltpu.VMEM((2, SC_TOKENS, D_MODEL), jnp.float32),
            pltpu.VMEM((GATHER_BUFS, GATHER_ROWS, D_MODEL), jnp.float32),
            pltpu.SemaphoreType.DMA((GATHER_BUFS,)),
            pltpu.SemaphoreType.DMA((2,)),
            pltpu.SemaphoreType.DMA((2,)),
        ],
        compiler_params=pltpu.CompilerParams(needs_layout_passes=False),
        name="peer_combine_sc",
    )
    def k(eid_hbm, a_hbm, v_hbm, out_hbm, idx_v, a_v, o_v, bufs, gsem, ssem, osem):
        wid = lax.axis_index("s") * info.num_cores + lax.axis_index("c")
        _sc_gather_stream(wid, n_batches, v_hbm.at[layer], [eid_hbm, a_hbm], [idx_v, a_v], out_hbm, o_v, bufs,
                          gsem, ssem, osem, _combine_compute(a_v, o_v, bufs))

    return k(eid, a, v_tabs)


def _peer_act_kernel(hp_ref, gate_ref, a_ref):
    a_ref[...] = _gelu(hp_ref[...]) * gate_ref[...]


ELEMENTWISE_ROWS = 1024


def _peer_act(hpre, gate):
    T = hpre.shape[0]
    rows = ELEMENTWISE_ROWS
    assert T % rows == 0
    spec = pl.BlockSpec((rows, N_SEL), lambda i: (i, 0))
    return pl.pallas_call(
        _peer_act_kernel, grid=(T // rows,), in_specs=[spec, spec], out_specs=spec,
        out_shape=jax.ShapeDtypeStruct((T, N_SEL), jnp.float32),
        compiler_params=pltpu.CompilerParams(dimension_semantics=("arbitrary",)),
        name="peer_act",
    )(hpre, gate)


def _residual_kernel(h_ref, c_ref, g_ref, o_ref, *, final_norm):
    y = h_ref[...] + c_ref[...]
    o_ref[...] = _rmsnorm_rows(y, g_ref[...]) if final_norm else y


def _residual(h, c, gfin, *, final_norm):
    T = h.shape[0]
    rows = ELEMENTWISE_ROWS
    assert T % rows == 0
    spec = pl.BlockSpec((rows, D_MODEL), lambda i: (i, 0))
    return pl.pallas_call(
        functools.partial(_residual_kernel, final_norm=final_norm), grid=(T // rows,),
        in_specs=[spec, spec, pl.BlockSpec((1, D_MODEL), lambda i: (0, 0))], out_specs=spec,
        out_shape=jax.ShapeDtypeStruct((T, D_MODEL), jnp.float32),
        compiler_params=pltpu.CompilerParams(dimension_semantics=("arbitrary",)),
        name="peer_residual",
    )(h, c, gfin.reshape(1, D_MODEL))


KV_WIDTH = N_KV_HEADS * HEAD_DIM
BF16 = jnp.bfloat16


def _qkv(h, gkv, gq, wkv_ref, wq_ref):
    kv = jnp.dot(_rmsnorm_rows(h, gkv).astype(BF16), wkv_ref[...], preferred_element_type=jnp.float32)
    q = jnp.dot(_rmsnorm_rows(h, gq).astype(BF16), wq_ref[...], preferred_element_type=jnp.float32)
    return kv, q


def _sink_softmax_pv(parts, sink):
    m = sink
    for s, _ in parts:
        m = jnp.maximum(m, jnp.max(s, axis=-1, keepdims=True))
    den = jnp.exp(sink - m)
    acc = None
    for s, v in parts:
        e = jnp.exp(s - m)
        den = den + jnp.sum(e, axis=-1, keepdims=True)
        pv = jnp.dot(e.astype(BF16), v.astype(BF16), preferred_element_type=jnp.float32)
        acc = pv if acc is None else acc + pv
    return acc / den


def _nt_dot(a, b):
    return lax.dot_general(a.astype(BF16), b.astype(BF16), (((1,), (1,)), ((), ())),
                           preferred_element_type=jnp.float32)


def _attn_prompt_kernel(h_ref, gkv_ref, gq_ref, wkv_ref, wq_ref, wo_ref, sink_ref,
                        o_ref, kvw_ref, prev_ref):
    blk = pl.program_id(1)
    h = h_ref[...]
    kv, q = _qkv(h, gkv_ref[...], gq_ref[...], wkv_ref, wq_ref)
    kvw_ref[0] = kv

    @pl.when(blk == 0)
    def _():
        prev_ref[...] = jnp.zeros_like(prev_ref)

    prev = prev_ref[...]
    qi = lax.broadcasted_iota(jnp.int32, (WINDOW, WINDOW), 0)
    kj = lax.broadcasted_iota(jnp.int32, (WINDOW, WINDOW), 1)
    prev_ok = jnp.logical_and(kj > qi, blk > 0)
    cur_ok = kj <= qi
    heads = []
    for kvh in range(N_KV_HEADS):
        ks = slice(kvh * HEAD_DIM, (kvh + 1) * HEAD_DIM)
        vs = slice(KV_WIDTH + kvh * HEAD_DIM, KV_WIDTH + (kvh + 1) * HEAD_DIM)
        for g in range(Q_PER_KV):
            hq = kvh * Q_PER_KV + g
            qh = q[:, hq * HEAD_DIM:(hq + 1) * HEAD_DIM]
            sp = jnp.where(prev_ok, _nt_dot(qh, prev[:, ks]) * ATTN_SCALE, -jnp.inf)
            sc = jnp.where(cur_ok, _nt_dot(qh, kv[:, ks]) * ATTN_SCALE, -jnp.inf)
            heads.append(_sink_softmax_pv([(sp, prev[:, vs]), (sc, kv[:, vs])], sink_ref[hq]))
    o = jnp.concatenate(heads, axis=1)
    o_ref[...] = h + jnp.dot(o.astype(BF16), wo_ref[...], preferred_element_type=jnp.float32)
    prev_ref[...] = kv


def _attn_prompt(h, gkv, gq, wkv, wq, wo, sinks, *, n_seq, seq_len):
    nb = seq_len // WINDOW
    row_spec = pl.BlockSpec((WINDOW, D_MODEL), lambda n, b: (n * nb + b, 0))
    full = lambda shape: pl.BlockSpec(shape, lambda n, b: (0,) * len(shape))
    return pl.pallas_call(
        _attn_prompt_kernel,
        grid=(n_seq, nb),
        in_specs=[
            row_spec, full((1, D_MODEL)), full((1, D_MODEL)),
            full((D_MODEL, 2 * KV_WIDTH)), full((D_MODEL, D_MODEL)), full((D_MODEL, D_MODEL)),
            pl.BlockSpec(memory_space=pltpu.SMEM),
        ],
        out_specs=[row_spec, pl.BlockSpec((1, WINDOW, 2 * KV_WIDTH), lambda n, b: (n, 0, 0))],
        out_shape=[
            jax.ShapeDtypeStruct(h.shape, jnp.float32),
            jax.ShapeDtypeStruct((n_seq, WINDOW, 2 * KV_WIDTH), jnp.float32),
        ],
        scratch_shapes=[pltpu.VMEM((WINDOW, 2 * KV_WIDTH), jnp.float32)],
        compiler_params=pltpu.CompilerParams(
            dimension_semantics=("arbitrary", "arbitrary"), vmem_limit_bytes=VMEM_LIMIT_BYTES),
        name="attn_prompt",
    )(h, gkv.reshape(1, D_MODEL), gq.reshape(1, D_MODEL), wkv, wq, wo, sinks)


ATTN_SEQS = 16
FIRST_PROMPT_PIECES = 4


def _attn_sample_kernel(h_ref, ck_ref, cv_ref, gkv_ref, gq_ref, wkv_ref, wq_ref, wo_ref, sink_ref,
                        o_ref, kw_ref, vw_ref, kv_ref, q_ref, att_ref, *, n_new):
    h = h_ref[...]
    kv, q = _qkv(h, gkv_ref[...], gq_ref[...], wkv_ref, wq_ref)
    kv_ref[...] = kv
    q_ref[...] = q
    n_seq = h.shape[0] // n_new
    rows = Q_PER_KV * n_new
    qpos_c = lax.broadcasted_iota(jnp.int32, (rows, WINDOW), 0) % n_new
    cache_ok = lax.broadcasted_iota(jnp.int32, (rows, WINDOW), 1) > qpos_c
    qpos_n = lax.broadcasted_iota(jnp.int32, (rows, n_new), 0) % n_new
    new_ok = lax.broadcasted_iota(jnp.int32, (rows, n_new), 1) <= qpos_n

    def per_seq(n, carry):
        r0 = pl.multiple_of(n * n_new, n_new)
        kvn = kv_ref[pl.ds(r0, n_new), :]
        qn = q_ref[pl.ds(r0, n_new), :]
        ck = ck_ref[n]
        cv = cv_ref[n]
        kw_ref[n, 0:WINDOW - n_new, :] = ck[n_new:, :]
        kw_ref[n, WINDOW - n_new:WINDOW, :] = kvn[:, 0:KV_WIDTH]
        vw_ref[n, 0:WINDOW - n_new, :] = cv[n_new:, :]
        vw_ref[n, WINDOW - n_new:WINDOW, :] = kvn[:, KV_WIDTH:2 * KV_WIDTH]
        outs = []
        for kvh in range(N_KV_HEADS):
            ks = slice(kvh * HEAD_DIM, (kvh + 1) * HEAD_DIM)
            vs = slice(KV_WIDTH + kvh * HEAD_DIM, KV_WIDTH + (kvh + 1) * HEAD_DIM)
            qs = jnp.concatenate(
                [qn[:, (kvh * Q_PER_KV + g) * HEAD_DIM:(kvh * Q_PER_KV + g + 1) * HEAD_DIM] for g in range(Q_PER_KV)],
                axis=0)
            s_c = jnp.where(cache_ok, _nt_dot(qs, ck[:, ks]) * ATTN_SCALE, -jnp.inf)
            s_n = jnp.where(new_ok, _nt_dot(qs, kvn[:, ks]) * ATTN_SCALE, -jnp.inf)
            sink = sink_ref[kvh][:, 0:1]
            o = _sink_softmax_pv([(s_c, cv[:, ks]), (s_n, kvn[:, vs])], sink)
            outs += [o[g * n_new:(g + 1) * n_new, :] for g in range(Q_PER_KV)]
        att_ref[pl.ds(r0, n_new), :] = jnp.concatenate(outs, axis=1)
        return carry

    lax.fori_loop(0, n_seq, per_seq, 0)
    o_ref[...] = h + jnp.dot(att_ref[...].astype(BF16), wo_ref[...], preferred_element_type=jnp.float32)


def _attn_sample(h, cache_k, cache_v, gkv, gq, wkv, wq, wo, sinks, *, n_new):
    n_seq = cache_k.shape[0]
    sb = ATTN_SEQS
    rows = sb * n_new
    row_spec = pl.BlockSpec((rows, D_MODEL), lambda i: (i, 0))
    win_spec = pl.BlockSpec((sb, WINDOW, KV_WIDTH), lambda i: (i, 0, 0))
    full = lambda shape: pl.BlockSpec(shape, lambda i: (0,) * len(shape))
    sink_rows = jnp.repeat(sinks.reshape(N_KV_HEADS, Q_PER_KV), n_new, axis=1)[:, :, None]
    sink_rows = jnp.broadcast_to(sink_rows, (N_KV_HEADS, Q_PER_KV * n_new, LANES))
    return pl.pallas_call(
        functools.partial(_attn_sample_kernel, n_new=n_new),
        grid=(n_seq // sb,),
        in_specs=[
            row_spec, win_spec, win_spec, full((1, D_MODEL)), full((1, D_MODEL)),
            full((D_MODEL, 2 * KV_WIDTH)), full((D_MODEL, D_MODEL)), full((D_MODEL, D_MODEL)),
            full((N_KV_HEADS, Q_PER_KV * n_new, LANES)),
        ],
        out_specs=[row_spec, win_spec, win_spec],
        out_shape=[
            jax.ShapeDtypeStruct(h.shape, jnp.float32),
            jax.ShapeDtypeStruct((n_seq, WINDOW, KV_WIDTH), jnp.float32),
            jax.ShapeDtypeStruct((n_seq, WINDOW, KV_WIDTH), jnp.float32),
        ],
        scratch_shapes=[
            pltpu.VMEM((rows, 2 * KV_WIDTH), jnp.float32),
            pltpu.VMEM((rows, D_MODEL), jnp.float32),
            pltpu.VMEM((rows, D_MODEL), jnp.float32),
        ],
        compiler_params=pltpu.CompilerParams(
            dimension_semantics=("arbitrary",), vmem_limit_bytes=VMEM_LIMIT_BYTES),
        name="attn_sample",
    )(h, cache_k, cache_v, gkv.reshape(1, D_MODEL), gq.reshape(1, D_MODEL), wkv, wq, wo, sink_rows)


def kernel(x_prompt, x_sample, state_ssm_re, state_ssm_im, cache_k_win, cache_v_win, norm_mix, norm_ffn, norm_kv, norm_final, ssm_lam_re, ssm_lam_im, ssm_log_dt, ssm_b_re, ssm_b_im, ssm_c_re, ssm_c_im, ssm_d, ssm_w_glu, w_kv, w_q, attn_sinks, w_o, peer_w_q, peer_sub_keys, peer_u, peer_v):
    bmat, cmat, apr, api = _s5_discretize(ssm_lam_re[0], ssm_lam_im[0], ssm_log_dt[0], ssm_b_re[0], ssm_b_im[0], ssm_c_re[0], ssm_c_im[0])
    cmat = cmat.astype(jnp.bfloat16)
    wglu = ssm_w_glu[0].astype(jnp.bfloat16)
    wkv = w_kv.astype(BF16)
    wq = w_q[0].astype(BF16)
    wo = w_o[0].astype(BF16)
    peer_wq = [peer_w_q[layer].astype(BF16).reshape(D_MODEL, PEER_HEADS, 2 * PEER_DHALF).transpose(1, 0, 2)
               for layer in range(2)]
    peer_sk = [peer_sub_keys[layer].astype(BF16) for layer in range(2)]

    def route(h, layer):
        xn, eid_t, gate_t = _peer_route(h, norm_ffn[layer], peer_wq[layer], peer_sk[layer])
        return dict(h=h, xn=xn, eid=eid_t.T, gate=gate_t.T, layer=layer)

    win = lambda a: a.reshape(a.shape[0], WINDOW, N_KV_HEADS, HEAD_DIM)

    seq_len = x_prompt.shape[1]
    xp_all = x_prompt.reshape(-1, D_MODEL)

    n_prompt, n_s = x_prompt.shape[0], x_sample.shape[0]
    xs = x_sample.reshape(-1, D_MODEL)
    z, sre_s, sim_s = _s5_mixer(xs, _state_to_chunks(state_ssm_re[0]), _state_to_chunks(state_ssm_im[0]), norm_mix[0],
                                ssm_d[0], bmat, cmat, apr, api, chain=False, seq_len=x_sample.shape[1])
    units = [route(_glu_residual(z, xs, wglu), 0)]
    group_units = [[0]]
    prompt_states = []
    for seq in range(n_prompt):
        n_pieces = FIRST_PROMPT_PIECES if seq == 0 else 1
        piece = seq_len // n_pieces
        sr = si = jnp.zeros((1, N_CHUNKS, 1, CHUNK_STATE), jnp.float32)
        group_units.append([])
        for p in range(n_pieces):
            row0 = seq * seq_len + p * piece
            z, sr, si = _s5_mixer(xp_all, sr, si, norm_mix[0], ssm_d[0], bmat, cmat, apr, api,
                                  chain=True, seq_len=piece, row0=row0)
            group_units[-1].append(len(units))
            units.append(route(_glu_residual(z, xp_all, wglu, res_row0=row0), 0))
        prompt_states.append((sr, si))
    n_layer0 = len(units)
    windows = {}

    def layer1_unit(g):
        mine = [units[i] for i in group_units[g]]
        h2 = jnp.concatenate([_residual(u['h'], u['c'], norm_final, final_norm=False) for u in mine], axis=0)
        if g == 0:
            h3, kw, vw = _attn_sample(h2, cache_k_win.reshape(n_s, WINDOW, KV_WIDTH),
                                      cache_v_win.reshape(n_s, WINDOW, KV_WIDTH), norm_kv, norm_mix[1],
                                      wkv, wq, wo, attn_sinks[0], n_new=x_sample.shape[1])
            windows[g] = (win(kw), win(vw))
        else:
            h3, kvw = _attn_prompt(h2, norm_kv, norm_mix[1], wkv, wq, wo, attn_sinks[0], n_seq=1, seq_len=seq_len)
            windows[g] = (win(kvw[:, :, :KV_WIDTH]), win(kvw[:, :, KV_WIDTH:]))
        return route(h3, 1)

    layer1_order = list(range(1, n_prompt + 1)) + [0]
    n_units = n_layer0 + 1 + n_prompt
    for k in range(n_units + 2):
        if n_layer0 <= k < n_units:
            units.append(layer1_unit(layer1_order[k - n_layer0]))
        hid = units[k] if k < n_units else None
        comb = units[k - 2] if k >= 2 else None
        if comb is not None:
            comb['a'] = _peer_act(comb['hpre'], comb['gate'])
        if hid is not None and comb is not None:
            comb['c'], hid['hpre'] = _peer_step_sc(comb['eid'], comb['a'], comb['layer'],
                                                   hid['eid'], hid['xn'], hid['layer'], peer_u, peer_v)
        elif hid is not None:
            hid['hpre'] = _peer_hidden_sc(hid['eid'], hid['xn'], peer_u, hid['layer'])
        else:
            comb['c'] = _peer_combine_sc(comb['eid'], comb['a'], peer_v, comb['layer'])

    ys = [_residual(u['h'], u['c'], norm_final, final_norm=True) for u in units[n_layer0:]]
    y_s = ys[-1].reshape(x_sample.shape)
    y_p = jnp.concatenate(ys[:-1], axis=0).reshape(x_prompt.shape)
    cat = lambda parts: jnp.concatenate(parts, axis=0)
    sre_p = cat([_chunks_to_state(sr) for sr, _ in prompt_states])
    sim_p = cat([_chunks_to_state(si) for _, si in prompt_states])
    kw_p = cat([windows[g][0] for g in range(1, n_prompt + 1)])
    vw_p = cat([windows[g][1] for g in range(1, n_prompt + 1)])
    return (y_p, y_s, sre_p[None], sim_p[None], kw_p, vw_p,
            _chunks_to_state(sre_s)[None], _chunks_to_state(sim_s)[None], windows[0][0], windows[0][1])
```

```python
import functools
import math

import jax
import jax.numpy as jnp
from jax import lax
from jax.experimental import pallas as pl
from jax.experimental.pallas import tpu as pltpu
from jax.experimental.pallas import tpu_sc as plsc

D_MODEL = 1024
GROUP_SIZE = 16
N_GROUPS = D_MODEL // GROUP_SIZE
STATE_DIM = 64
HEAD_DIM = 64
N_Q_HEADS = D_MODEL // HEAD_DIM
N_KV_HEADS = N_Q_HEADS // 8
Q_PER_KV = N_Q_HEADS // N_KV_HEADS
WINDOW = 128
ATTN_SCALE = 1.0 / math.sqrt(HEAD_DIM)
PEER_HEADS = 8
N_KEYS = 128
PEER_TOPK = 16
PEER_DHALF = 128
EPS = 1e-5

LANES = 128
SUBLANES = 8
VMEM_LIMIT_BYTES = 56 * 1024 * 1024

GROUPS_PER_CHUNK = LANES // GROUP_SIZE
N_CHUNKS = N_GROUPS // GROUPS_PER_CHUNK
CHUNK_STATE = GROUPS_PER_CHUNK * STATE_DIM
S5_ROWS = 256


def _rmsnorm_rows(x, g):
    r = lax.rsqrt(jnp.mean(x * x, axis=-1, keepdims=True) + EPS)
    return x * r * g


def _gelu(x):
    return 0.5 * x * (1.0 + lax.erf(x * (1.0 / math.sqrt(2.0))))


def _s5_discretize(lam_re, lam_im, log_dt, b_re, b_im, c_re, c_im):
    f32 = jnp.float32
    lr = lam_re.astype(f32)
    li = lam_im.astype(f32)
    dt = jnp.exp(log_dt.astype(f32))[:, None]
    mag = jnp.exp(lr * dt)
    ab_re = mag * jnp.cos(li * dt)
    ab_im = mag * jnp.sin(li * dt)
    den = lr * lr + li * li
    f_re = ((ab_re - 1.0) * lr + ab_im * li) / den
    f_im = (ab_im * lr - (ab_re - 1.0) * li) / den
    br = b_re.astype(f32)
    bi = b_im.astype(f32)
    bb_re = f_re[..., None] * br - f_im[..., None] * bi
    bb_im = f_re[..., None] * bi + f_im[..., None] * br
    eye = jnp.eye(GROUPS_PER_CHUNK, dtype=f32)

    def chunk_rows(v):
        return v.reshape(N_CHUNKS, 1, CHUNK_STATE)

    def in_blocks(bb):
        t = bb.reshape(N_CHUNKS, GROUPS_PER_CHUNK, STATE_DIM, GROUP_SIZE).transpose(0, 1, 3, 2)
        return jnp.einsum('mgjp,gh->mgjhp', t, eye).reshape(N_CHUNKS, LANES, CHUNK_STATE)

    def out_blocks(c):
        t = c.astype(f32).reshape(N_CHUNKS, GROUPS_PER_CHUNK, GROUP_SIZE, STATE_DIM).transpose(0, 1, 3, 2)
        return jnp.einsum('mgpj,gh->mgphj', t, eye).reshape(N_CHUNKS, CHUNK_STATE, LANES)

    bfull = jnp.concatenate([in_blocks(bb_re), in_blocks(bb_im)], axis=2)
    b_hi = bfull.astype(jnp.bfloat16)
    b_lo = (bfull - b_hi.astype(f32)).astype(jnp.bfloat16)
    bmat = (jnp.concatenate([b_hi, b_hi], axis=1), b_lo)
    cmat = jnp.concatenate([out_blocks(c_re), -out_blocks(c_im)], axis=1)
    pr, pi = [ab_re], [ab_im]
    for _ in range(SUBLANES - 1):
        pr, pi = pr + [pr[-1] * ab_re - pi[-1] * ab_im], pi + [pr[-1] * ab_im + pi[-1] * ab_re]
    apr = jnp.concatenate([chunk_rows(v) for v in pr], axis=1)
    api = jnp.concatenate([chunk_rows(v) for v in pi], axis=1)
    return bmat, cmat, apr, api


def _split_bf16(x):
    hi = x.astype(jnp.bfloat16)
    return hi, (x - hi.astype(jnp.float32)).astype(jnp.bfloat16)


def _s5_kernel(x_ref, g_ref, d_ref, bhh_ref, blo_ref, c_ref, apr_ref, api_ref, h0r_ref, h0i_ref,
               z_ref, sr_ref, si_ref, u_ref, us_ref, st_ref, cr_ref, ci_ref, *, chain, blocks_per_seq):
    rb = pl.program_id(0)
    m = pl.program_id(1)
    rows = x_ref.shape[0]
    n_tiles = rows // SUBLANES

    @pl.when(m == 0)
    def _():
        u = _rmsnorm_rows(x_ref[...], g_ref[...])
        for mm in range(N_CHUNKS):
            uc = u[:, mm * LANES:(mm + 1) * LANES]
            u_ref[mm] = uc
            hi, lo = _split_bf16(uc)
            us_ref[mm, :, 0:LANES] = hi
            us_ref[mm, :, LANES:2 * LANES] = lo

    u = u_ref[m]
    us = us_ref[m]
    st_ref[...] = (jnp.dot(us, bhh_ref[m], preferred_element_type=jnp.float32)
                   + jnp.dot(us[:, 0:LANES], blo_ref[m], preferred_element_type=jnp.float32))

    apr = apr_ref[0]
    api = api_ref[0]
    row = lax.broadcasted_iota(jnp.int32, (SUBLANES, CHUNK_STATE), 0)

    if chain:
        @pl.when(rb % blocks_per_seq == 0)
        def _():
            cr_ref[m] = h0r_ref[0, 0]
            ci_ref[m] = h0i_ref[0, 0]

    def tile_step(k, carry):
        r0 = pl.multiple_of(k * SUBLANES, SUBLANES)
        xr = st_ref[pl.ds(r0, SUBLANES), 0:CHUNK_STATE]
        xi = st_ref[pl.ds(r0, SUBLANES), CHUNK_STATE:2 * CHUNK_STATE]
        for d in (1, 2, 4):
            ar = apr[d - 1:d, :]
            ai = api[d - 1:d, :]
            sr = jnp.where(row >= d, pltpu.roll(xr, d, axis=0), 0.0)
            si = jnp.where(row >= d, pltpu.roll(xi, d, axis=0), 0.0)
            xr, xi = xr + ar * sr - ai * si, xi + ar * si + ai * sr
        if chain:
            cr, ci = carry
        else:
            cr = h0r_ref[k, 0]
            ci = h0i_ref[k, 0]
        hr = xr + apr * cr - api * ci
        hi = xi + apr * ci + api * cr
        st_ref[pl.ds(r0, SUBLANES), 0:CHUNK_STATE] = hr
        st_ref[pl.ds(r0, SUBLANES), CHUNK_STATE:2 * CHUNK_STATE] = hi
        lr_ = hr[SUBLANES - 1:SUBLANES, :]
        li_ = hi[SUBLANES - 1:SUBLANES, :]
        if chain:
            return lr_, li_
        sr_ref[k, m] = lr_
        si_ref[k, m] = li_
        return carry

    if chain:
        cr, ci = lax.fori_loop(0, n_tiles, tile_step, (cr_ref[m], ci_ref[m]))
        cr_ref[m] = cr
        ci_ref[m] = ci
        sr_ref[0, m] = cr
        si_ref[0, m] = ci
    else:
        lax.fori_loop(0, n_tiles, tile_step, 0)

    y = jnp.dot(st_ref[...].astype(jnp.bfloat16), c_ref[m], preferred_element_type=jnp.float32)
    y = y + d_ref[0] * u
    z_ref[...] = _gelu(y)


def _s5_mixer(x, h0r, h0i, g, d_skip, bmat, cmat_bf16, apr, api, *, chain, seq_len, row0=0):
    nseq = h0r.shape[0]
    T = nseq * seq_len
    rows = S5_ROWS
    blk0 = row0 // rows
    if chain:
        blocks_per_seq = seq_len // rows
        seq_blk = 1
        seq_map = lambda rb, m: (rb // blocks_per_seq, m, 0, 0)
        out_map = lambda rb, m: (rb // blocks_per_seq, 0, 0, 0)
    else:
        assert seq_len == SUBLANES
        blocks_per_seq = 1
        seq_blk = rows // SUBLANES
        seq_map = lambda rb, m: (rb, m, 0, 0)
        out_map = lambda rb, m: (rb, 0, 0, 0)
    grid = (T // rows, N_CHUNKS)
    kern = functools.partial(_s5_kernel, chain=chain, blocks_per_seq=blocks_per_seq)
    st_spec = pl.BlockSpec((seq_blk, 1, 1, CHUNK_STATE), seq_map)
    out_st_spec = pl.BlockSpec((seq_blk, N_CHUNKS, 1, CHUNK_STATE), out_map)
    z, sr, si = pl.pallas_call(
        kern,
        grid=grid,
        in_specs=[
            pl.BlockSpec((rows, D_MODEL), lambda rb, m: (rb + blk0, 0)),
            pl.BlockSpec((1, D_MODEL), lambda rb, m: (0, 0)),
            pl.BlockSpec((1, 1, LANES), lambda rb, m: (m, 0, 0)),
            pl.BlockSpec((N_CHUNKS, 2 * LANES, 2 * CHUNK_STATE), lambda rb, m: (0, 0, 0)),
            pl.BlockSpec((N_CHUNKS, LANES, 2 * CHUNK_STATE), lambda rb, m: (0, 0, 0)),
            pl.BlockSpec((N_CHUNKS, 2 * CHUNK_STATE, LANES), lambda rb, m: (0, 0, 0)),
            pl.BlockSpec((1, SUBLANES, CHUNK_STATE), lambda rb, m: (m, 0, 0)),
            pl.BlockSpec((1, SUBLANES, CHUNK_STATE), lambda rb, m: (m, 0, 0)),
            st_spec, st_spec,
        ],
        out_specs=[
            pl.BlockSpec((rows, LANES), lambda rb, m: (rb, m)),
            out_st_spec, out_st_spec,
        ],
        out_shape=[
            jax.ShapeDtypeStruct((T, D_MODEL), jnp.float32),
            jax.ShapeDtypeStruct((nseq, N_CHUNKS, 1, CHUNK_STATE), jnp.float32),
            jax.ShapeDtypeStruct((nseq, N_CHUNKS, 1, CHUNK_STATE), jnp.float32),
        ],
        scratch_shapes=[
            pltpu.VMEM((N_CHUNKS, rows, LANES), jnp.float32),
            pltpu.VMEM((N_CHUNKS, rows, 2 * LANES), jnp.bfloat16),
            pltpu.VMEM((rows, 2 * CHUNK_STATE), jnp.float32),
            pltpu.VMEM((N_CHUNKS, 1, CHUNK_STATE), jnp.float32),
            pltpu.VMEM((N_CHUNKS, 1, CHUNK_STATE), jnp.float32),
        ],
        compiler_params=pltpu.CompilerParams(
            dimension_semantics=("arbitrary", "arbitrary"), vmem_limit_bytes=VMEM_LIMIT_BYTES),
        name="s5_mixer",
    )(x, g.reshape(1, D_MODEL), d_skip.reshape(N_CHUNKS, 1, LANES), bmat[0], bmat[1], cmat_bf16, apr, api,
      h0r, h0i)
    return z, sr, si


def _state_to_chunks(h):
    return h.reshape(h.shape[0], N_CHUNKS, 1, CHUNK_STATE)


def _chunks_to_state(s):
    return s.reshape(s.shape[0], N_GROUPS, STATE_DIM)


def _glu_kernel(z_ref, res_ref, w_ref, o_ref):
    zz = jnp.dot(z_ref[...].astype(jnp.bfloat16), w_ref[...], preferred_element_type=jnp.float32)
    a = zz[:, :D_MODEL]
    b = zz[:, D_MODEL:]
    o_ref[...] = res_ref[...] + a * (1.0 / (1.0 + jnp.exp(-b)))


def _glu_residual(z, res, w_bf16, res_row0=0):
    T = z.shape[0]
    rows = math.gcd(T, 512)
    blk0 = res_row0 // rows
    return pl.pallas_call(
        _glu_kernel,
        grid=(T // rows,),
        in_specs=[
            pl.BlockSpec((rows, D_MODEL), lambda i: (i, 0)),
            pl.BlockSpec((rows, D_MODEL), lambda i: (i + blk0, 0)),
            pl.BlockSpec((D_MODEL, 2 * D_MODEL), lambda i: (0, 0)),
        ],
        out_specs=pl.BlockSpec((rows, D_MODEL), lambda i: (i, 0)),
        out_shape=jax.ShapeDtypeStruct((T, D_MODEL), jnp.float32),
        compiler_params=pltpu.CompilerParams(
            dimension_semantics=("arbitrary",), vmem_limit_bytes=VMEM_LIMIT_BYTES),
        name="glu_residual",
    )(z, res, w_bf16)


ROUTE_ROWS = 256


def _topk_rows(s, payload):
    n_rows = s.shape[0]
    row = lax.broadcasted_iota(jnp.int32, s.shape, 0)
    vals, picks = [], []
    for _ in range(PEER_TOPK):
        m = jnp.max(s, axis=0, keepdims=True)
        pos = jnp.min(jnp.where(s == m, row, n_rows), axis=0, keepdims=True)
        sel = row == pos
        vals.append(m)
        if payload is None:
            picks.append(pos)
        else:
            picks.append(jnp.max(jnp.where(sel, payload, -1), axis=0, keepdims=True))
        s = jnp.where(sel, -jnp.inf, s)
    return jnp.concatenate(vals, axis=0), jnp.concatenate(picks, axis=0)


def _pair_rows(a0, a1, combine):
    half = PEER_TOPK // 2
    parts = [combine(a0[0:1, :], a1)]
    parts += [combine(a0[i:i + 1, :], a1[0:half, :]) for i in range(1, half)]
    parts.append(combine(a0[half:PEER_TOPK, :], a1[0:1, :]))
    return jnp.concatenate(parts, axis=0)


def _peer_route_kernel(h_ref, g_ref, wq_ref, sk_ref, xn_ref, eid_ref, gate_ref, xb_ref):
    hd = pl.program_id(1)

    @pl.when(hd == 0)
    def _():
        xn = _rmsnorm_rows(h_ref[...], g_ref[...])
        xn_ref[...] = xn
        xb_ref[...] = xn.astype(jnp.bfloat16)

    q = jnp.dot(xb_ref[...], wq_ref[hd], preferred_element_type=jnp.float32)
    sv, si = [], []
    for c in range(2):
        qc = q[:, c * PEER_DHALF:(c + 1) * PEER_DHALF].astype(jnp.bfloat16)
        st = lax.dot_general(sk_ref[hd, c], qc, (((1,), (1,)), ((), ())), preferred_element_type=jnp.float32)
        v, i = _topk_rows(st, None)
        sv.append(v)
        si.append(i)
    cand = _pair_rows(sv[0], sv[1], lambda a, b: a + b)
    cid = _pair_rows(si[0], si[1], lambda a, b: a * N_KEYS + b)
    fv, eid = _topk_rows(cand, cid)
    e = jnp.exp(fv - fv[0:1, :])
    gate_ref[...] = e / jnp.sum(e, axis=0, keepdims=True)
    eid_ref[...] = eid


def _peer_route(h, g, wq_bf16, sk_bf16):
    T = h.shape[0]
    rows = ROUTE_ROWS
    n_sel = PEER_HEADS * PEER_TOPK
    return pl.pallas_call(
        _peer_route_kernel,
        grid=(T // rows, PEER_HEADS),
        in_specs=[
            pl.BlockSpec((rows, D_MODEL), lambda tb, hd: (tb, 0)),
            pl.BlockSpec((1, D_MODEL), lambda tb, hd: (0, 0)),
            pl.BlockSpec((PEER_HEADS, D_MODEL, 2 * PEER_DHALF), lambda tb, hd: (0, 0, 0)),
            pl.BlockSpec((PEER_HEADS, 2, N_KEYS, PEER_DHALF), lambda tb, hd: (0, 0, 0, 0)),
        ],
        out_specs=[
            pl.BlockSpec((rows, D_MODEL), lambda tb, hd: (tb, 0)),
            pl.BlockSpec((PEER_TOPK, rows), lambda tb, hd: (hd, tb)),
            pl.BlockSpec((PEER_TOPK, rows), lambda tb, hd: (hd, tb)),
        ],
        out_shape=[
            jax.ShapeDtypeStruct((T, D_MODEL), jnp.float32),
            jax.ShapeDtypeStruct((n_sel, T), jnp.int32),
            jax.ShapeDtypeStruct((n_sel, T), jnp.float32),
        ],
        scratch_shapes=[pltpu.VMEM((rows, D_MODEL), jnp.bfloat16)],
        compiler_params=pltpu.CompilerParams(
            dimension_semantics=("arbitrary", "arbitrary"), vmem_limit_bytes=VMEM_LIMIT_BYTES),
        name="peer_route",
    )(h, g.reshape(1, D_MODEL), wq_bf16, sk_bf16)


N_SEL = PEER_HEADS * PEER_TOPK
SC_LANES = 16
GATHER_ROWS = PEER_TOPK
GATHERS_PER_TOKEN = N_SEL // GATHER_ROWS
GATHER_BUFS = 4
SC_TOKENS = 8
ACC_STRIDE = SC_LANES + 1


def _stage_copies(wid, n_batches, stage_srcs, stage_bufs, ssem, bi, slot):
    base = (wid * n_batches + bi) * SC_TOKENS
    return [pltpu.make_async_copy(src.at[pl.ds(base, SC_TOKENS)], buf.at[slot], ssem.at[slot])
            for src, buf in zip(stage_srcs, stage_bufs)]


def _sc_gather_stream(wid, n_batches, tab_hbm, stage_srcs, stage_bufs, out_hbm, o_v, bufs, gsem, ssem, osem,
                      compute, handed_over=False, hand_over=None):
    idx_v = stage_bufs[0]

    def stage_copies(bi, slot):
        return _stage_copies(wid, n_batches, stage_srcs, stage_bufs, ssem, bi, slot)

    def out_copy(bi, slot):
        base = (wid * n_batches + bi) * SC_TOKENS
        return pltpu.make_async_copy(o_v.at[slot], out_hbm.at[pl.ds(base, SC_TOKENS)], osem.at[slot])

    def start(slot, t, kk, b):
        idx = idx_v[slot, t, pl.ds(kk * GATHER_ROWS, GATHER_ROWS)]
        pltpu.async_copy(tab_hbm.at[idx], bufs.at[b], gsem.at[b])

    def wait(b):
        pltpu.make_async_copy(tab_hbm.at[pl.ds(0, GATHER_ROWS)], bufs.at[b], gsem.at[b]).wait()

    if not handed_over:
        for c in stage_copies(0, 0):
            c.start()
        for c in stage_copies(0, 0):
            c.wait()
        for q in range(GATHER_BUFS - 1):
            start(0, q // GATHERS_PER_TOKEN, q % GATHERS_PER_TOKEN, q % GATHER_BUFS)

    def batch(bi, carry):
        slot = bi % 2
        has_next = bi + 1 < n_batches

        @pl.when(has_next)
        def _():
            for c in stage_copies(bi + 1, 1 - slot):
                c.start()

        @pl.when(bi >= 2)
        def _():
            out_copy(bi, slot).wait()

        def tok(t, carry):
            @pl.when(jnp.logical_and(t == SC_TOKENS - 1, has_next))
            def _():
                for c in stage_copies(bi + 1, 1 - slot):
                    c.wait()

            for kk in range(GATHERS_PER_TOKEN):
                nq = kk + GATHER_BUFS - 1
                nk, nb = nq % GATHERS_PER_TOKEN, nq % GATHER_BUFS
                if nq < GATHERS_PER_TOKEN:
                    start(slot, t, nk, nb)
                else:
                    @pl.when(t + 1 < SC_TOKENS)
                    def _():
                        start(slot, t + 1, nk, nb)

                    @pl.when(jnp.logical_and(t + 1 == SC_TOKENS, has_next))
                    def _():
                        start(1 - slot, 0, nk, nb)

                    if hand_over is not None:
                        @pl.when(jnp.logical_and(t + 1 == SC_TOKENS, jnp.logical_not(has_next)))
                        def _():
                            hand_over(nk, nb)

                wait(kk % GATHER_BUFS)
                compute(slot, t, kk, kk % GATHER_BUFS)
            return carry

        lax.fori_loop(0, SC_TOKENS, tok, 0)
        out_copy(bi, slot).start()
        return carry

    lax.fori_loop(0, n_batches, batch, 0)
    if n_batches >= 2:
        out_copy(n_batches - 2, (n_batches - 2) % 2).wait()
    out_copy(n_batches - 1, (n_batches - 1) % 2).wait()


def _sc_mesh_and_batches(n_tokens):
    info = plsc.get_sparse_core_info()
    assert info.num_lanes == SC_LANES
    n_workers = info.num_cores * info.num_subcores
    assert n_tokens % (n_workers * SC_TOKENS) == 0
    mesh = plsc.VectorSubcoreMesh(core_axis_name="c", subcore_axis_name="s")
    return info, mesh, n_tokens // (n_workers * SC_TOKENS)


def _hidden_compute(x_v, o_v, bufs, acc_v):
    lane = lax.iota(jnp.int32, SC_LANES)
    zero = jnp.zeros((SC_LANES,), jnp.float32)

    def compute(slot, t, kk, b):
        @plsc.parallel_loop(0, D_MODEL // SC_LANES, carry=(zero,) * GATHER_ROWS)
        def accs(c, accs):
            xc = x_v[slot, t, pl.ds(c * SC_LANES, SC_LANES)]
            return tuple(accs[r] + bufs[b, r, pl.ds(c * SC_LANES, SC_LANES)] * xc for r in range(GATHER_ROWS))

        for r in range(GATHER_ROWS):
            acc_v[pl.ds(r * ACC_STRIDE, SC_LANES)] = accs[r]
        tot = zero
        for c in range(SC_LANES):
            tot = tot + plsc.load_gather(acc_v, [lane * ACC_STRIDE + c])
        o_v[slot, t, pl.ds(kk * GATHER_ROWS, GATHER_ROWS)] = tot

    return compute


def _combine_compute(a_v, o_v, bufs):
    def compute(slot, t, kk, b):
        svec = jnp.full((SC_LANES,), slot, jnp.int32)
        tvec = jnp.full((SC_LANES,), t, jnp.int32)
        ws = [plsc.load_gather(a_v, [svec, tvec, jnp.full((SC_LANES,), kk * GATHER_ROWS + r, jnp.int32)])
              for r in range(GATHER_ROWS)]

        @plsc.parallel_loop(0, D_MODEL // SC_LANES, unroll=2)
        def _(c):
            sl = pl.ds(c * SC_LANES, SC_LANES)
            terms = [ws[r] * bufs[b, r, sl] for r in range(GATHER_ROWS)]
            if kk != 0:
                terms.append(o_v[slot, t, sl])
            while len(terms) > 1:
                pairs = [terms[i] + terms[i + 1] for i in range(0, len(terms) - 1, 2)]
                terms = pairs + ([terms[-1]] if len(terms) % 2 else [])
            o_v[slot, t, sl] = terms[0]

    return compute


def _peer_step_sc(eid_c, a, layer_c, eid_h, xn, layer_h, u_tabs, v_tabs):
    Tc, Th = eid_c.shape[0], eid_h.shape[0]
    info, mesh, nb_c = _sc_mesh_and_batches(Tc)
    _, _, nb_h = _sc_mesh_and_batches(Th)

    @functools.partial(
        pl.kernel, mesh=mesh,
        out_type=(jax.ShapeDtypeStruct((Tc, D_MODEL), jnp.float32), jax.ShapeDtypeStruct((Th, N_SEL), jnp.float32)),
        scratch_types=[
            pltpu.VMEM((2, SC_TOKENS, N_SEL), jnp.int32),
            pltpu.VMEM((2, SC_TOKENS, N_SEL), jnp.float32),
            pltpu.VMEM((2, SC_TOKENS, D_MODEL), jnp.float32),
            pltpu.VMEM((2, SC_TOKENS, N_SEL), jnp.int32),
            pltpu.VMEM((2, SC_TOKENS, D_MODEL), jnp.float32),
            pltpu.VMEM((2, SC_TOKENS, N_SEL), jnp.float32),
            pltpu.VMEM((GATHER_BUFS, GATHER_ROWS, D_MODEL), jnp.float32),
            pltpu.VMEM((GATHER_ROWS * ACC_STRIDE,), jnp.float32),
            pltpu.SemaphoreType.DMA((GATHER_BUFS,)),
            pltpu.SemaphoreType.DMA((2,)),
            pltpu.SemaphoreType.DMA((2,)),
            pltpu.SemaphoreType.DMA((2,)),
            pltpu.SemaphoreType.DMA((2,)),
        ],
        compiler_params=pltpu.CompilerParams(needs_layout_passes=False),
        name="peer_step_sc",
    )
    def k(eidc_hbm, a_hbm, eidh_hbm, xn_hbm, u_hbm, v_hbm, c_hbm, hp_hbm,
          idxc_v, a_v, oc_v, idxh_v, x_v, oh_v, bufs, acc_v, gsem, ssem_c, osem_c, ssem_h, osem_h):
        wid = lax.axis_index("s") * info.num_cores + lax.axis_index("c")
        u_tab = u_hbm.at[layer_h]
        first_h = lambda: _stage_copies(wid, nb_h, [eidh_hbm, xn_hbm], [idxh_v, x_v], ssem_h, 0, 0)
        for c in first_h():
            c.start()

        def hand_over(head, b):
            if head == 0:
                for c in first_h():
                    c.wait()
            idx = idxh_v[0, 0, pl.ds(head * GATHER_ROWS, GATHER_ROWS)]
            pltpu.async_copy(u_tab.at[idx], bufs.at[b], gsem.at[b])

        _sc_gather_stream(wid, nb_c, v_hbm.at[layer_c], [eidc_hbm, a_hbm], [idxc_v, a_v], c_hbm, oc_v, bufs,
                          gsem, ssem_c, osem_c, _combine_compute(a_v, oc_v, bufs), hand_over=hand_over)
        _sc_gather_stream(wid, nb_h, u_tab, [eidh_hbm, xn_hbm], [idxh_v, x_v], hp_hbm, oh_v, bufs,
                          gsem, ssem_h, osem_h, _hidden_compute(x_v, oh_v, bufs, acc_v), handed_over=True)

    return k(eid_c, a, eid_h, xn, u_tabs, v_tabs)


def _peer_hidden_sc(eid, xn, u_tabs, layer):
    T = eid.shape[0]
    info, mesh, n_batches = _sc_mesh_and_batches(T)

    @functools.partial(
        pl.kernel, mesh=mesh,
        out_type=jax.ShapeDtypeStruct((T, N_SEL), jnp.float32),
        scratch_types=[
            pltpu.VMEM((2, SC_TOKENS, N_SEL), jnp.int32),
            pltpu.VMEM((2, SC_TOKENS, D_MODEL), jnp.float32),
            pltpu.VMEM((2, SC_TOKENS, N_SEL), jnp.float32),
            pltpu.VMEM((GATHER_BUFS, GATHER_ROWS, D_MODEL), jnp.float32),
            pltpu.VMEM((GATHER_ROWS * ACC_STRIDE,), jnp.float32),
            pltpu.SemaphoreType.DMA((GATHER_BUFS,)),
            pltpu.SemaphoreType.DMA((2,)),
            pltpu.SemaphoreType.DMA((2,)),
        ],
        compiler_params=pltpu.CompilerParams(needs_layout_passes=False),
        name="peer_hidden_sc",
    )
    def k(eid_hbm, xn_hbm, u_hbm, out_hbm, idx_v, x_v, o_v, bufs, acc_v, gsem, ssem, osem):
        wid = lax.axis_index("s") * info.num_cores + lax.axis_index("c")
        _sc_gather_stream(wid, n_batches, u_hbm.at[layer], [eid_hbm, xn_hbm], [idx_v, x_v], out_hbm, o_v, bufs,
                          gsem, ssem, osem, _hidden_compute(x_v, o_v, bufs, acc_v))

    return k(eid, xn, u_tabs)


def _peer_combine_sc(eid, a, v_tabs, layer):
    T = eid.shape[0]
    info, mesh, n_batches = _sc_mesh_and_batches(T)

    @functools.partial(
        pl.kernel, mesh=mesh,
        out_type=jax.ShapeDtypeStruct((T, D_MODEL), jnp.float32),
        scratch_types=[
            pltpu.VMEM((2, SC_TOKENS, N_SEL), jnp.int32),
            pltpu.VMEM((2, SC_TOKENS, N_SEL), jnp.float32),
            pltpu.VMEM((2, SC_TOKENS, D_MODEL), jnp.float32),
            pltpu.VMEM((GATHER_BUFS, GATHER_ROWS, D_MODEL), jnp.float32),
            pltpu.SemaphoreType.DMA((GATHER_BUFS,)),
            pltpu.SemaphoreType.DMA((2,)),
            pltpu.SemaphoreType.DMA((2,)),
        ],
        compiler_params=pltpu.CompilerParams(needs_layout_passes=False),
        name="peer_combine_sc",
    )
    def k(eid_hbm, a_hbm, v_hbm, out_hbm, idx_v, a_v, o_v, bufs, gsem, ssem, osem):
        wid = lax.axis_index("s") * info.num_cores + lax.axis_index("c")
        _sc_gather_stream(wid, n_batches, v_hbm.at[layer], [eid_hbm, a_hbm], [idx_v, a_v], out_hbm, o_v, bufs,
                          gsem, ssem, osem, _combine_compute(a_v, o_v, bufs))

    return k(eid, a, v_tabs)


def _peer_act_kernel(hp_ref, gate_ref, a_ref):
    a_ref[...] = _gelu(hp_ref[...]) * gate_ref[...]


ELEMENTWISE_ROWS = 256


def _peer_act(hpre, gate):
    T = hpre.shape[0]
    rows = ELEMENTWISE_ROWS
    assert T % rows == 0
    spec = pl.BlockSpec((rows, N_SEL), lambda i: (i, 0))
    return pl.pallas_call(
        _peer_act_kernel, grid=(T // rows,), in_specs=[spec, spec], out_specs=spec,
        out_shape=jax.ShapeDtypeStruct((T, N_SEL), jnp.float32),
        compiler_params=pltpu.CompilerParams(dimension_semantics=("arbitrary",)),
        name="peer_act",
    )(hpre, gate)


def _residual_kernel(h_ref, c_ref, g_ref, o_ref, *, final_norm):
    y = h_ref[...] + c_ref[...]
    o_ref[...] = _rmsnorm_rows(y, g_ref[...]) if final_norm else y


def _residual(h, c, gfin, *, final_norm):
    T = h.shape[0]
    rows = ELEMENTWISE_ROWS
    assert T % rows == 0
    spec = pl.BlockSpec((rows, D_MODEL), lambda i: (i, 0))
    return pl.pallas_call(
        functools.partial(_residual_kernel, final_norm=final_norm), grid=(T // rows,),
        in_specs=[spec, spec, pl.BlockSpec((1, D_MODEL), lambda i: (0, 0))], out_specs=spec,
        out_shape=jax.ShapeDtypeStruct((T, D_MODEL), jnp.float32),
        compiler_params=pltpu.CompilerParams(dimension_semantics=("arbitrary",)),
        name="peer_residual",
    )(h, c, gfin.reshape(1, D_MODEL))


KV_WIDTH = N_KV_HEADS * HEAD_DIM
BF16 = jnp.bfloat16


def _qkv(h, gkv, gq, wkv_ref, wq_ref):
    kv = jnp.dot(_rmsnorm_rows(h, gkv).astype(BF16), wkv_ref[...], preferred_element_type=jnp.float32)
    q = jnp.dot(_rmsnorm_rows(h, gq).astype(BF16), wq_ref[...], preferred_element_type=jnp.float32)
    return kv, q


def _sink_softmax_pv(parts, sink):
    m = sink
    for s, _ in parts:
        m = jnp.maximum(m, jnp.max(s, axis=-1, keepdims=True))
    den = jnp.exp(sink - m)
    acc = None
    for s, v in parts:
        e = jnp.exp(s - m)
        den = den + jnp.sum(e, axis=-1, keepdims=True)
        pv = jnp.dot(e.astype(BF16), v.astype(BF16), preferred_element_type=jnp.float32)
        acc = pv if acc is None else acc + pv
    return acc / den


def _nt_dot(a, b):
    return lax.dot_general(a.astype(BF16), b.astype(BF16), (((1,), (1,)), ((), ())),
                           preferred_element_type=jnp.float32)


def _attn_prompt_kernel(h_ref, gkv_ref, gq_ref, wkv_ref, wq_ref, wo_ref, sink_ref,
                        o_ref, kvw_ref, prev_ref):
    blk = pl.program_id(1)
    h = h_ref[...]
    kv, q = _qkv(h, gkv_ref[...], gq_ref[...], wkv_ref, wq_ref)
    kvw_ref[0] = kv

    @pl.when(blk == 0)
    def _():
        prev_ref[...] = jnp.zeros_like(prev_ref)

    prev = prev_ref[...]
    qi = lax.broadcasted_iota(jnp.int32, (WINDOW, WINDOW), 0)
    kj = lax.broadcasted_iota(jnp.int32, (WINDOW, WINDOW), 1)
    prev_ok = jnp.logical_and(kj > qi, blk > 0)
    cur_ok = kj <= qi
    heads = []
    for kvh in range(N_KV_HEADS):
        ks = slice(kvh * HEAD_DIM, (kvh + 1) * HEAD_DIM)
        vs = slice(KV_WIDTH + kvh * HEAD_DIM, KV_WIDTH + (kvh + 1) * HEAD_DIM)
        for g in range(Q_PER_KV):
            hq = kvh * Q_PER_KV + g
            qh = q[:, hq * HEAD_DIM:(hq + 1) * HEAD_DIM]
            sp = jnp.where(prev_ok, _nt_dot(qh, prev[:, ks]) * ATTN_SCALE, -jnp.inf)
            sc = jnp.where(cur_ok, _nt_dot(qh, kv[:, ks]) * ATTN_SCALE, -jnp.inf)
            heads.append(_sink_softmax_pv([(sp, prev[:, vs]), (sc, kv[:, vs])], sink_ref[hq]))
    o = jnp.concatenate(heads, axis=1)
    o_ref[...] = h + jnp.dot(o.astype(BF16), wo_ref[...], preferred_element_type=jnp.float32)
    prev_ref[...] = kv


def _attn_prompt(h, gkv, gq, wkv, wq, wo, sinks, *, n_seq, seq_len):
    nb = seq_len // WINDOW
    row_spec = pl.BlockSpec((WINDOW, D_MODEL), lambda n, b: (n * nb + b, 0))
    full = lambda shape: pl.BlockSpec(shape, lambda n, b: (0,) * len(shape))
    return pl.pallas_call(
        _attn_prompt_kernel,
        grid=(n_seq, nb),
        in_specs=[
            row_spec, full((1, D_MODEL)), full((1, D_MODEL)),
            full((D_MODEL, 2 * KV_WIDTH)), full((D_MODEL, D_MODEL)), full((D_MODEL, D_MODEL)),
            pl.BlockSpec(memory_space=pltpu.SMEM),
        ],
        out_specs=[row_spec, pl.BlockSpec((1, WINDOW, 2 * KV_WIDTH), lambda n, b: (n, 0, 0))],
        out_shape=[
            jax.ShapeDtypeStruct(h.shape, jnp.float32),
            jax.ShapeDtypeStruct((n_seq, WINDOW, 2 * KV_WIDTH), jnp.float32),
        ],
        scratch_shapes=[pltpu.VMEM((WINDOW, 2 * KV_WIDTH), jnp.float32)],
        compiler_params=pltpu.CompilerParams(
            dimension_semantics=("arbitrary", "arbitrary"), vmem_limit_bytes=VMEM_LIMIT_BYTES),
        name="attn_prompt",
    )(h, gkv.reshape(1, D_MODEL), gq.reshape(1, D_MODEL), wkv, wq, wo, sinks)


ATTN_SEQS = 16
FIRST_PROMPT_PIECES = 4
DECODE_PIECE_ROWS = (256, 256, 512)


def _attn_sample_kernel(h_ref, ck_ref, cv_ref, gkv_ref, gq_ref, wkv_ref, wq_ref, wo_ref, sink_ref,
                        o_ref, kw_ref, vw_ref, kv_ref, q_ref, att_ref, *, n_new):
    h = h_ref[...]
    kv, q = _qkv(h, gkv_ref[...], gq_ref[...], wkv_ref, wq_ref)
    kv_ref[...] = kv
    q_ref[...] = q
    n_seq = h.shape[0] // n_new
    rows = Q_PER_KV * n_new
    qpos_c = lax.broadcasted_iota(jnp.int32, (rows, WINDOW), 0) % n_new
    cache_ok = lax.broadcasted_iota(jnp.int32, (rows, WINDOW), 1) > qpos_c
    qpos_n = lax.broadcasted_iota(jnp.int32, (rows, n_new), 0) % n_new
    new_ok = lax.broadcasted_iota(jnp.int32, (rows, n_new), 1) <= qpos_n

    def per_seq(n, carry):
        r0 = pl.multiple_of(n * n_new, n_new)
        kvn = kv_ref[pl.ds(r0, n_new), :]
        qn = q_ref[pl.ds(r0, n_new), :]
        ck = ck_ref[n]
        cv = cv_ref[n]
        kw_ref[n, 0:WINDOW - n_new, :] = ck[n_new:, :]
        kw_ref[n, WINDOW - n_new:WINDOW, :] = kvn[:, 0:KV_WIDTH]
        vw_ref[n, 0:WINDOW - n_new, :] = cv[n_new:, :]
        vw_ref[n, WINDOW - n_new:WINDOW, :] = kvn[:, KV_WIDTH:2 * KV_WIDTH]
        outs = []
        for kvh in range(N_KV_HEADS):
            ks = slice(kvh * HEAD_DIM, (kvh + 1) * HEAD_DIM)
            vs = slice(KV_WIDTH + kvh * HEAD_DIM, KV_WIDTH + (kvh + 1) * HEAD_DIM)
            qs = jnp.concatenate(
                [qn[:, (kvh * Q_PER_KV + g) * HEAD_DIM:(kvh * Q_PER_KV + g + 1) * HEAD_DIM] for g in range(Q_PER_KV)],
                axis=0)
            s_c = jnp.where(cache_ok, _nt_dot(qs, ck[:, ks]) * ATTN_SCALE, -jnp.inf)
            s_n = jnp.where(new_ok, _nt_dot(qs, kvn[:, ks]) * ATTN_SCALE, -jnp.inf)
            sink = sink_ref[kvh][:, 0:1]
            o = _sink_softmax_pv([(s_c, cv[:, ks]), (s_n, kvn[:, vs])], sink)
            outs += [o[g * n_new:(g + 1) * n_new, :] for g in range(Q_PER_KV)]
        att_ref[pl.ds(r0, n_new), :] = jnp.concatenate(outs, axis=1)
        return carry

    lax.fori_loop(0, n_seq, per_seq, 0)
    o_ref[...] = h + jnp.dot(att_ref[...].astype(BF16), wo_ref[...], preferred_element_type=jnp.float32)


def _attn_sample(h, cache_k, cache_v, gkv, gq, wkv, wq, wo, sinks, *, n_new):
    n_seq = cache_k.shape[0]
    sb = ATTN_SEQS
    rows = sb * n_new
    row_spec = pl.BlockSpec((rows, D_MODEL), lambda i: (i, 0))
    win_spec = pl.BlockSpec((sb, WINDOW, KV_WIDTH), lambda i: (i, 0, 0))
    full = lambda shape: pl.BlockSpec(shape, lambda i: (0,) * len(shape))
    sink_rows = jnp.repeat(sinks.reshape(N_KV_HEADS, Q_PER_KV), n_new, axis=1)[:, :, None]
    sink_rows = jnp.broadcast_to(sink_rows, (N_KV_HEADS, Q_PER_KV * n_new, LANES))
    return pl.pallas_call(
        functools.partial(_attn_sample_kernel, n_new=n_new),
        grid=(n_seq // sb,),
        in_specs=[
            row_spec, win_spec, win_spec, full((1, D_MODEL)), full((1, D_MODEL)),
            full((D_MODEL, 2 * KV_WIDTH)), full((D_MODEL, D_MODEL)), full((D_MODEL, D_MODEL)),
            full((N_KV_HEADS, Q_PER_KV * n_new, LANES)),
        ],
        out_specs=[row_spec, win_spec, win_spec],
        out_shape=[
            jax.ShapeDtypeStruct(h.shape, jnp.float32),
            jax.ShapeDtypeStruct((n_seq, WINDOW, KV_WIDTH), jnp.float32),
            jax.ShapeDtypeStruct((n_seq, WINDOW, KV_WIDTH), jnp.float32),
        ],
        scratch_shapes=[
            pltpu.VMEM((rows, 2 * KV_WIDTH), jnp.float32),
            pltpu.VMEM((rows, D_MODEL), jnp.float32),
            pltpu.VMEM((rows, D_MODEL), jnp.float32),
        ],
        compiler_params=pltpu.CompilerParams(
            dimension_semantics=("arbitrary",), vmem_limit_bytes=VMEM_LIMIT_BYTES),
        name="attn_sample",
    )(h, cache_k, cache_v, gkv.reshape(1, D_MODEL), gq.reshape(1, D_MODEL), wkv, wq, wo, sink_rows)


def kernel(x_prompt, x_sample, state_ssm_re, state_ssm_im, cache_k_win, cache_v_win, norm_mix, norm_ffn, norm_kv, norm_final, ssm_lam_re, ssm_lam_im, ssm_log_dt, ssm_b_re, ssm_b_im, ssm_c_re, ssm_c_im, ssm_d, ssm_w_glu, w_kv, w_q, attn_sinks, w_o, peer_w_q, peer_sub_keys, peer_u, peer_v):
    bmat, cmat, apr, api = _s5_discretize(ssm_lam_re[0], ssm_lam_im[0], ssm_log_dt[0], ssm_b_re[0], ssm_b_im[0], ssm_c_re[0], ssm_c_im[0])
    cmat = cmat.astype(jnp.bfloat16)
    wglu = ssm_w_glu[0].astype(jnp.bfloat16)
    wkv = w_kv.astype(BF16)
    wq = w_q[0].astype(BF16)
    wo = w_o[0].astype(BF16)
    peer_wq = [peer_w_q[layer].astype(BF16).reshape(D_MODEL, PEER_HEADS, 2 * PEER_DHALF).transpose(1, 0, 2)
               for layer in range(2)]
    peer_sk = [peer_sub_keys[layer].astype(BF16) for layer in range(2)]

    def route(h, layer):
        xn, eid_t, gate_t = _peer_route(h, norm_ffn[layer], peer_wq[layer], peer_sk[layer])
        return dict(h=h, xn=xn, eid=eid_t.T, gate=gate_t.T, layer=layer)

    win = lambda a: a.reshape(a.shape[0], WINDOW, N_KV_HEADS, HEAD_DIM)

    seq_len = x_prompt.shape[1]
    xp_all = x_prompt.reshape(-1, D_MODEL)

    n_prompt, n_s = x_prompt.shape[0], x_sample.shape[0]
    xs = x_sample.reshape(-1, D_MODEL)
    dec_len = x_sample.shape[1]
    h0r, h0i = _state_to_chunks(state_ssm_re[0]), _state_to_chunks(state_ssm_im[0])
    units = []
    group_units = [[]]
    dec_states = []
    row0 = 0
    for n_rows in DECODE_PIECE_ROWS:
        s0, n_seq = row0 // dec_len, n_rows // dec_len
        z, sr, si = _s5_mixer(xs, h0r[s0:s0 + n_seq], h0i[s0:s0 + n_seq], norm_mix[0], ssm_d[0], bmat, cmat, apr, api,
                              chain=False, seq_len=dec_len, row0=row0)
        dec_states.append((sr, si))
        group_units[0].append(len(units))
        units.append(route(_glu_residual(z, xs, wglu, res_row0=row0), 0))
        row0 += n_rows
    assert row0 == xs.shape[0]
    sre_s = jnp.concatenate([sr for sr, _ in dec_states], axis=0)
    sim_s = jnp.concatenate([si for _, si in dec_states], axis=0)
    prompt_states = []
    for seq in range(n_prompt):
        n_pieces = FIRST_PROMPT_PIECES if seq == 0 else 1
        piece = seq_len // n_pieces
        sr = si = jnp.zeros((1, N_CHUNKS, 1, CHUNK_STATE), jnp.float32)
        group_units.append([])
        for p in range(n_pieces):
            row0 = seq * seq_len + p * piece
            z, sr, si = _s5_mixer(xp_all, sr, si, norm_mix[0], ssm_d[0], bmat, cmat, apr, api,
                                  chain=True, seq_len=piece, row0=row0)
            group_units[-1].append(len(units))
            units.append(route(_glu_residual(z, xp_all, wglu, res_row0=row0), 0))
        prompt_states.append((sr, si))
    n_layer0 = len(units)
    windows = {}

    def layer1_unit(g):
        mine = [units[i] for i in group_units[g]]
        h2 = jnp.concatenate([_residual(u['h'], u['c'], norm_final, final_norm=False) for u in mine], axis=0)
        if g == 0:
            h3, kw, vw = _attn_sample(h2, cache_k_win.reshape(n_s, WINDOW, KV_WIDTH),
                                      cache_v_win.reshape(n_s, WINDOW, KV_WIDTH), norm_kv, norm_mix[1],
                                      wkv, wq, wo, attn_sinks[0], n_new=x_sample.shape[1])
            windows[g] = (win(kw), win(vw))
        else:
            h3, kvw = _attn_prompt(h2, norm_kv, norm_mix[1], wkv, wq, wo, attn_sinks[0], n_seq=1, seq_len=seq_len)
            windows[g] = (win(kvw[:, :, :KV_WIDTH]), win(kvw[:, :, KV_WIDTH:]))
        return route(h3, 1)

    layer1_order = list(range(1, n_prompt + 1)) + [0]
    n_units = n_layer0 + 1 + n_prompt
    for k in range(n_units + 2):
        if n_layer0 <= k < n_units:
            units.append(layer1_unit(layer1_order[k - n_layer0]))
        hid = units[k] if k < n_units else None
        comb = units[k - 2] if k >= 2 else None
        if comb is not None:
            comb['a'] = _peer_act(comb['hpre'], comb['gate'])
        if hid is not None and comb is not None:
            comb['c'], hid['hpre'] = _peer_step_sc(comb['eid'], comb['a'], comb['layer'],
                                                   hid['eid'], hid['xn'], hid['layer'], peer_u, peer_v)
        elif hid is not None:
            hid['hpre'] = _peer_hidden_sc(hid['eid'], hid['xn'], peer_u, hid['layer'])
        else:
            comb['c'] = _peer_combine_sc(comb['eid'], comb['a'], peer_v, comb['layer'])

    ys = [_residual(u['h'], u['c'], norm_final, final_norm=True) for u in units[n_layer0:]]
    y_s = ys[-1].reshape(x_sample.shape)
    y_p = jnp.concatenate(ys[:-1], axis=0).reshape(x_prompt.shape)
    cat = lambda parts: jnp.concatenate(parts, axis=0)
    sre_p = cat([_chunks_to_state(sr) for sr, _ in prompt_states])
    sim_p = cat([_chunks_to_state(si) for _, si in prompt_states])
    kw_p = cat([windows[g][0] for g in range(1, n_prompt + 1)])
    vw_p = cat([windows[g][1] for g in range(1, n_prompt + 1)])
    return (y_p, y_s, sre_p[None], sim_p[None], kw_p, vw_p,
            _chunks_to_state(sre_s)[None], _chunks_to_state(sim_s)[None], windows[0][0], windows[0][1])
```

```python
import functools
import math

import jax
import jax.numpy as jnp
from jax import lax
from jax.experimental import pallas as pl
from jax.experimental.pallas import tpu as pltpu
from jax.experimental.pallas import tpu_sc as plsc

D_MODEL = 1024
GROUP_SIZE = 16
N_GROUPS = D_MODEL // GROUP_SIZE
STATE_DIM = 64
HEAD_DIM = 64
N_Q_HEADS = D_MODEL // HEAD_DIM
N_KV_HEADS = N_Q_HEADS // 8
Q_PER_KV = N_Q_HEADS // N_KV_HEADS
WINDOW = 128
ATTN_SCALE = 1.0 / math.sqrt(HEAD_DIM)
PEER_HEADS = 8
N_KEYS = 128
PEER_TOPK = 16
PEER_DHALF = 128
EPS = 1e-5

LANES = 128
SUBLANES = 8
VMEM_LIMIT_BYTES = 56 * 1024 * 1024

GROUPS_PER_CHUNK = LANES // GROUP_SIZE
N_CHUNKS = N_GROUPS // GROUPS_PER_CHUNK
CHUNK_STATE = GROUPS_PER_CHUNK * STATE_DIM
S5_ROWS = 256


def _rmsnorm_rows(x, g):
    r = lax.rsqrt(jnp.mean(x * x, axis=-1, keepdims=True) + EPS)
    return x * r * g


def _gelu(x):
    return 0.5 * x * (1.0 + lax.erf(x * (1.0 / math.sqrt(2.0))))


def _s5_discretize(lam_re, lam_im, log_dt, b_re, b_im, c_re, c_im):
    f32 = jnp.float32
    lr = lam_re.astype(f32)
    li = lam_im.astype(f32)
    dt = jnp.exp(log_dt.astype(f32))[:, None]
    mag = jnp.exp(lr * dt)
    ab_re = mag * jnp.cos(li * dt)
    ab_im = mag * jnp.sin(li * dt)
    den = lr * lr + li * li
    f_re = ((ab_re - 1.0) * lr + ab_im * li) / den
    f_im = (ab_im * lr - (ab_re - 1.0) * li) / den
    br = b_re.astype(f32)
    bi = b_im.astype(f32)
    bb_re = f_re[..., None] * br - f_im[..., None] * bi
    bb_im = f_re[..., None] * bi + f_im[..., None] * br
    eye = jnp.eye(GROUPS_PER_CHUNK, dtype=f32)

    def chunk_rows(v):
        return v.reshape(N_CHUNKS, 1, CHUNK_STATE)

    def in_blocks(bb):
        t = bb.reshape(N_CHUNKS, GROUPS_PER_CHUNK, STATE_DIM, GROUP_SIZE).transpose(0, 1, 3, 2)
        return jnp.einsum('mgjp,gh->mgjhp', t, eye).reshape(N_CHUNKS, LANES, CHUNK_STATE)

    def out_blocks(c):
        t = c.astype(f32).reshape(N_CHUNKS, GROUPS_PER_CHUNK, GROUP_SIZE, STATE_DIM).transpose(0, 1, 3, 2)
        return jnp.einsum('mgpj,gh->mgphj', t, eye).reshape(N_CHUNKS, CHUNK_STATE, LANES)

    bfull = jnp.concatenate([in_blocks(bb_re), in_blocks(bb_im)], axis=2)
    b_hi = bfull.astype(jnp.bfloat16)
    b_lo = (bfull - b_hi.astype(f32)).astype(jnp.bfloat16)
    bmat = (jnp.concatenate([b_hi, b_hi], axis=1), b_lo)
    cmat = jnp.concatenate([out_blocks(c_re), -out_blocks(c_im)], axis=1)
    pr, pi = [ab_re], [ab_im]
    for _ in range(SUBLANES - 1):
        pr, pi = pr + [pr[-1] * ab_re - pi[-1] * ab_im], pi + [pr[-1] * ab_im + pi[-1] * ab_re]
    apr = jnp.concatenate([chunk_rows(v) for v in pr], axis=1)
    api = jnp.concatenate([chunk_rows(v) for v in pi], axis=1)
    return bmat, cmat, apr, api


def _split_bf16(x):
    hi = x.astype(jnp.bfloat16)
    return hi, (x - hi.astype(jnp.float32)).astype(jnp.bfloat16)


def _s5_kernel(x_ref, g_ref, d_ref, bhh_ref, blo_ref, c_ref, apr_ref, api_ref, h0r_ref, h0i_ref, w_ref,
               o_ref, sr_ref, si_ref, u_ref, us_ref, st_ref, cr_ref, ci_ref, z_ref, *, chain, blocks_per_seq):
    rb = pl.program_id(0)
    m = pl.program_id(1)
    rows = x_ref.shape[0]
    n_tiles = rows // SUBLANES

    @pl.when(m == 0)
    def _():
        u = _rmsnorm_rows(x_ref[...], g_ref[...])
        for mm in range(N_CHUNKS):
            uc = u[:, mm * LANES:(mm + 1) * LANES]
            u_ref[mm] = uc
            hi, lo = _split_bf16(uc)
            us_ref[mm, :, 0:LANES] = hi
            us_ref[mm, :, LANES:2 * LANES] = lo

    u = u_ref[m]
    us = us_ref[m]
    st_ref[...] = (jnp.dot(us, bhh_ref[m], preferred_element_type=jnp.float32)
                   + jnp.dot(us[:, 0:LANES], blo_ref[m], preferred_element_type=jnp.float32))

    apr = apr_ref[0]
    api = api_ref[0]
    row = lax.broadcasted_iota(jnp.int32, (SUBLANES, CHUNK_STATE), 0)

    if chain:
        @pl.when(rb % blocks_per_seq == 0)
        def _():
            cr_ref[m] = h0r_ref[0, 0]
            ci_ref[m] = h0i_ref[0, 0]

    def tile_step(k, carry):
        r0 = pl.multiple_of(k * SUBLANES, SUBLANES)
        xr = st_ref[pl.ds(r0, SUBLANES), 0:CHUNK_STATE]
        xi = st_ref[pl.ds(r0, SUBLANES), CHUNK_STATE:2 * CHUNK_STATE]
        for d in (1, 2, 4):
            ar = apr[d - 1:d, :]
            ai = api[d - 1:d, :]
            sr = jnp.where(row >= d, pltpu.roll(xr, d, axis=0), 0.0)
            si = jnp.where(row >= d, pltpu.roll(xi, d, axis=0), 0.0)
            xr, xi = xr + ar * sr - ai * si, xi + ar * si + ai * sr
        if chain:
            cr, ci = carry
        else:
            cr = h0r_ref[k, 0]
            ci = h0i_ref[k, 0]
        hr = xr + apr * cr - api * ci
        hi = xi + apr * ci + api * cr
        st_ref[pl.ds(r0, SUBLANES), 0:CHUNK_STATE] = hr
        st_ref[pl.ds(r0, SUBLANES), CHUNK_STATE:2 * CHUNK_STATE] = hi
        lr_ = hr[SUBLANES - 1:SUBLANES, :]
        li_ = hi[SUBLANES - 1:SUBLANES, :]
        if chain:
            return lr_, li_
        sr_ref[k, m] = lr_
        si_ref[k, m] = li_
        return carry

    if chain:
        cr, ci = lax.fori_loop(0, n_tiles, tile_step, (cr_ref[m], ci_ref[m]))
        cr_ref[m] = cr
        ci_ref[m] = ci
        sr_ref[0, m] = cr
        si_ref[0, m] = ci
    else:
        lax.fori_loop(0, n_tiles, tile_step, 0)

    y = jnp.dot(st_ref[...].astype(jnp.bfloat16), c_ref[m], preferred_element_type=jnp.float32)
    y = y + d_ref[0] * u
    z_ref[m] = _gelu(y).astype(jnp.bfloat16)

    @pl.when(m == N_CHUNKS - 1)
    def _():
        zfull = jnp.concatenate([z_ref[mm] for mm in range(N_CHUNKS)], axis=1)
        zz = jnp.dot(zfull, w_ref[...], preferred_element_type=jnp.float32)
        a = zz[:, :D_MODEL]
        b = zz[:, D_MODEL:]
        o_ref[...] = x_ref[...] + a * (1.0 / (1.0 + jnp.exp(-b)))


def _s5_mixer(x, h0r, h0i, g, d_skip, bmat, cmat_bf16, apr, api, wglu_bf16, *, chain, seq_len, row0=0):
    nseq = h0r.shape[0]
    T = nseq * seq_len
    rows = S5_ROWS
    blk0 = row0 // rows
    if chain:
        blocks_per_seq = seq_len // rows
        seq_blk = 1
        seq_map = lambda rb, m: (rb // blocks_per_seq, m, 0, 0)
        out_map = lambda rb, m: (rb // blocks_per_seq, 0, 0, 0)
    else:
        assert seq_len == SUBLANES
        blocks_per_seq = 1
        seq_blk = rows // SUBLANES
        seq_map = lambda rb, m: (rb, m, 0, 0)
        out_map = lambda rb, m: (rb, 0, 0, 0)
    grid = (T // rows, N_CHUNKS)
    kern = functools.partial(_s5_kernel, chain=chain, blocks_per_seq=blocks_per_seq)
    st_spec = pl.BlockSpec((seq_blk, 1, 1, CHUNK_STATE), seq_map)
    out_st_spec = pl.BlockSpec((seq_blk, N_CHUNKS, 1, CHUNK_STATE), out_map)
    z, sr, si = pl.pallas_call(
        kern,
        grid=grid,
        in_specs=[
            pl.BlockSpec((rows, D_MODEL), lambda rb, m: (rb + blk0, 0)),
            pl.BlockSpec((1, D_MODEL), lambda rb, m: (0, 0)),
            pl.BlockSpec((1, 1, LANES), lambda rb, m: (m, 0, 0)),
            pl.BlockSpec((N_CHUNKS, 2 * LANES, 2 * CHUNK_STATE), lambda rb, m: (0, 0, 0)),
            pl.BlockSpec((N_CHUNKS, LANES, 2 * CHUNK_STATE), lambda rb, m: (0, 0, 0)),
            pl.BlockSpec((N_CHUNKS, 2 * CHUNK_STATE, LANES), lambda rb, m: (0, 0, 0)),
            pl.BlockSpec((1, SUBLANES, CHUNK_STATE), lambda rb, m: (m, 0, 0)),
            pl.BlockSpec((1, SUBLANES, CHUNK_STATE), lambda rb, m: (m, 0, 0)),
            st_spec, st_spec,
            pl.BlockSpec((D_MODEL, 2 * D_MODEL), lambda rb, m: (0, 0)),
        ],
        out_specs=[
            pl.BlockSpec((rows, D_MODEL), lambda rb, m: (rb, 0)),
            out_st_spec, out_st_spec,
        ],
        out_shape=[
            jax.ShapeDtypeStruct((T, D_MODEL), jnp.float32),
            jax.ShapeDtypeStruct((nseq, N_CHUNKS, 1, CHUNK_STATE), jnp.float32),
            jax.ShapeDtypeStruct((nseq, N_CHUNKS, 1, CHUNK_STATE), jnp.float32),
        ],
        scratch_shapes=[
            pltpu.VMEM((N_CHUNKS, rows, LANES), jnp.float32),
            pltpu.VMEM((N_CHUNKS, rows, 2 * LANES), jnp.bfloat16),
            pltpu.VMEM((rows, 2 * CHUNK_STATE), jnp.float32),
            pltpu.VMEM((N_CHUNKS, 1, CHUNK_STATE), jnp.float32),
            pltpu.VMEM((N_CHUNKS, 1, CHUNK_STATE), jnp.float32),
            pltpu.VMEM((N_CHUNKS, rows, LANES), jnp.bfloat16),
        ],
        compiler_params=pltpu.CompilerParams(
            dimension_semantics=("arbitrary", "arbitrary"), vmem_limit_bytes=VMEM_LIMIT_BYTES),
        name="s5_mixer",
    )(x, g.reshape(1, D_MODEL), d_skip.reshape(N_CHUNKS, 1, LANES), bmat[0], bmat[1], cmat_bf16, apr, api,
      h0r, h0i, wglu_bf16)
    return z, sr, si


def _state_to_chunks(h):
    return h.reshape(h.shape[0], N_CHUNKS, 1, CHUNK_STATE)


def _chunks_to_state(s):
    return s.reshape(s.shape[0], N_GROUPS, STATE_DIM)


ROUTE_ROWS = 256


def _topk_rows(s, payload):
    n_rows = s.shape[0]
    row = lax.broadcasted_iota(jnp.int32, s.shape, 0)
    vals, picks = [], []
    for _ in range(PEER_TOPK):
        m = jnp.max(s, axis=0, keepdims=True)
        pos = jnp.min(jnp.where(s == m, row, n_rows), axis=0, keepdims=True)
        sel = row == pos
        vals.append(m)
        if payload is None:
            picks.append(pos)
        else:
            picks.append(jnp.max(jnp.where(sel, payload, -1), axis=0, keepdims=True))
        s = jnp.where(sel, -jnp.inf, s)
    return jnp.concatenate(vals, axis=0), jnp.concatenate(picks, axis=0)


def _pair_rows(a0, a1, combine):
    half = PEER_TOPK // 2
    parts = [combine(a0[0:1, :], a1)]
    parts += [combine(a0[i:i + 1, :], a1[0:half, :]) for i in range(1, half)]
    parts.append(combine(a0[half:PEER_TOPK, :], a1[0:1, :]))
    return jnp.concatenate(parts, axis=0)


def _peer_route_kernel(h_ref, g_ref, wq_ref, sk_ref, xn_ref, eid_ref, gate_ref, xb_ref):
    hd = pl.program_id(1)

    @pl.when(hd == 0)
    def _():
        xn = _rmsnorm_rows(h_ref[...], g_ref[...])
        xn_ref[...] = xn
        xb_ref[...] = xn.astype(jnp.bfloat16)

    q = jnp.dot(xb_ref[...], wq_ref[hd], preferred_element_type=jnp.float32)
    sv, si = [], []
    for c in range(2):
        qc = q[:, c * PEER_DHALF:(c + 1) * PEER_DHALF].astype(jnp.bfloat16)
        st = lax.dot_general(sk_ref[hd, c], qc, (((1,), (1,)), ((), ())), preferred_element_type=jnp.float32)
        v, i = _topk_rows(st, None)
        sv.append(v)
        si.append(i)
    cand = _pair_rows(sv[0], sv[1], lambda a, b: a + b)
    cid = _pair_rows(si[0], si[1], lambda a, b: a * N_KEYS + b)
    fv, eid = _topk_rows(cand, cid)
    e = jnp.exp(fv - fv[0:1, :])
    gate_ref[...] = e / jnp.sum(e, axis=0, keepdims=True)
    eid_ref[...] = eid


def _peer_route(h, g, wq_bf16, sk_bf16):
    T = h.shape[0]
    rows = ROUTE_ROWS
    n_sel = PEER_HEADS * PEER_TOPK
    return pl.pallas_call(
        _peer_route_kernel,
        grid=(T // rows, PEER_HEADS),
        in_specs=[
            pl.BlockSpec((rows, D_MODEL), lambda tb, hd: (tb, 0)),
            pl.BlockSpec((1, D_MODEL), lambda tb, hd: (0, 0)),
            pl.BlockSpec((PEER_HEADS, D_MODEL, 2 * PEER_DHALF), lambda tb, hd: (0, 0, 0)),
            pl.BlockSpec((PEER_HEADS, 2, N_KEYS, PEER_DHALF), lambda tb, hd: (0, 0, 0, 0)),
        ],
        out_specs=[
            pl.BlockSpec((rows, D_MODEL), lambda tb, hd: (tb, 0)),
            pl.BlockSpec((PEER_TOPK, rows), lambda tb, hd: (hd, tb)),
            pl.BlockSpec((PEER_TOPK, rows), lambda tb, hd: (hd, tb)),
        ],
        out_shape=[
            jax.ShapeDtypeStruct((T, D_MODEL), jnp.float32),
            jax.ShapeDtypeStruct((n_sel, T), jnp.int32),
            jax.ShapeDtypeStruct((n_sel, T), jnp.float32),
        ],
        scratch_shapes=[pltpu.VMEM((rows, D_MODEL), jnp.bfloat16)],
        compiler_params=pltpu.CompilerParams(
            dimension_semantics=("arbitrary", "arbitrary"), vmem_limit_bytes=VMEM_LIMIT_BYTES),
        name="peer_route",
    )(h, g.reshape(1, D_MODEL), wq_bf16, sk_bf16)


N_SEL = PEER_HEADS * PEER_TOPK
SC_LANES = 16
GATHER_ROWS = PEER_TOPK
GATHERS_PER_TOKEN = N_SEL // GATHER_ROWS
GATHER_BUFS = 4
SC_TOKENS = 8
ACC_STRIDE = SC_LANES + 1


def _stage_copies(wid, n_batches, stage_srcs, stage_bufs, ssem, bi, slot):
    base = (wid * n_batches + bi) * SC_TOKENS
    return [pltpu.make_async_copy(src.at[pl.ds(base, SC_TOKENS)], buf.at[slot], ssem.at[slot])
            for src, buf in zip(stage_srcs, stage_bufs)]


def _sc_gather_stream(wid, n_batches, tab_hbm, stage_srcs, stage_bufs, out_hbm, o_v, bufs, gsem, ssem, osem,
                      compute, handed_over=False, hand_over=None):
    idx_v = stage_bufs[0]

    def stage_copies(bi, slot):
        return _stage_copies(wid, n_batches, stage_srcs, stage_bufs, ssem, bi, slot)

    def out_copy(bi, slot):
        base = (wid * n_batches + bi) * SC_TOKENS
        return pltpu.make_async_copy(o_v.at[slot], out_hbm.at[pl.ds(base, SC_TOKENS)], osem.at[slot])

    def start(slot, t, kk, b):
        idx = idx_v[slot, t, pl.ds(kk * GATHER_ROWS, GATHER_ROWS)]
        pltpu.async_copy(tab_hbm.at[idx], bufs.at[b], gsem.at[b])

    def wait(b):
        pltpu.make_async_copy(tab_hbm.at[pl.ds(0, GATHER_ROWS)], bufs.at[b], gsem.at[b]).wait()

    if not handed_over:
        for c in stage_copies(0, 0):
            c.start()
        for c in stage_copies(0, 0):
            c.wait()
        for q in range(GATHER_BUFS - 1):
            start(0, q // GATHERS_PER_TOKEN, q % GATHERS_PER_TOKEN, q % GATHER_BUFS)

    def batch(bi, carry):
        slot = bi % 2
        has_next = bi + 1 < n_batches

        @pl.when(has_next)
        def _():
            for c in stage_copies(bi + 1, 1 - slot):
                c.start()

        @pl.when(bi >= 2)
        def _():
            out_copy(bi, slot).wait()

        def tok(t, carry):
            @pl.when(jnp.logical_and(t == SC_TOKENS - 1, has_next))
            def _():
                for c in stage_copies(bi + 1, 1 - slot):
                    c.wait()

            for kk in range(GATHERS_PER_TOKEN):
                nq = kk + GATHER_BUFS - 1
                nk, nb = nq % GATHERS_PER_TOKEN, nq % GATHER_BUFS
                if nq < GATHERS_PER_TOKEN:
                    start(slot, t, nk, nb)
                else:
                    @pl.when(t + 1 < SC_TOKENS)
                    def _():
                        start(slot, t + 1, nk, nb)

                    @pl.when(jnp.logical_and(t + 1 == SC_TOKENS, has_next))
                    def _():
                        start(1 - slot, 0, nk, nb)

                    if hand_over is not None:
                        @pl.when(jnp.logical_and(t + 1 == SC_TOKENS, jnp.logical_not(has_next)))
                        def _():
                            hand_over(nk, nb)

                wait(kk % GATHER_BUFS)
                compute(slot, t, kk, kk % GATHER_BUFS)
            return carry

        lax.fori_loop(0, SC_TOKENS, tok, 0)
        out_copy(bi, slot).start()
        return carry

    lax.fori_loop(0, n_batches, batch, 0)
    if n_batches >= 2:
        out_copy(n_batches - 2, (n_batches - 2) % 2).wait()
    out_copy(n_batches - 1, (n_batches - 1) % 2).wait()


def _sc_mesh_and_batches(n_tokens):
    info = plsc.get_sparse_core_info()
    assert info.num_lanes == SC_LANES
    n_workers = info.num_cores * info.num_subcores
    assert n_tokens % (n_workers * SC_TOKENS) == 0
    mesh = plsc.VectorSubcoreMesh(core_axis_name="c", subcore_axis_name="s")
    return info, mesh, n_tokens // (n_workers * SC_TOKENS)


def _hidden_compute(x_v, o_v, bufs, acc_v):
    lane = lax.iota(jnp.int32, SC_LANES)
    zero = jnp.zeros((SC_LANES,), jnp.float32)

    def compute(slot, t, kk, b):
        @plsc.parallel_loop(0, D_MODEL // SC_LANES, carry=(zero,) * GATHER_ROWS)
        def accs(c, accs):
            xc = x_v[slot, t, pl.ds(c * SC_LANES, SC_LANES)]
            return tuple(accs[r] + bufs[b, r, pl.ds(c * SC_LANES, SC_LANES)] * xc for r in range(GATHER_ROWS))

        for r in range(GATHER_ROWS):
            acc_v[pl.ds(r * ACC_STRIDE, SC_LANES)] = accs[r]
        tot = zero
        for c in range(SC_LANES):
            tot = tot + plsc.load_gather(acc_v, [lane * ACC_STRIDE + c])
        o_v[slot, t, pl.ds(kk * GATHER_ROWS, GATHER_ROWS)] = tot

    return compute


def _combine_compute(a_v, o_v, bufs):
    def compute(slot, t, kk, b):
        svec = jnp.full((SC_LANES,), slot, jnp.int32)
        tvec = jnp.full((SC_LANES,), t, jnp.int32)
        ws = [plsc.load_gather(a_v, [svec, tvec, jnp.full((SC_LANES,), kk * GATHER_ROWS + r, jnp.int32)])
              for r in range(GATHER_ROWS)]

        @plsc.parallel_loop(0, D_MODEL // SC_LANES, unroll=2)
        def _(c):
            sl = pl.ds(c * SC_LANES, SC_LANES)
            terms = [ws[r] * bufs[b, r, sl] for r in range(GATHER_ROWS)]
            if kk != 0:
                terms.append(o_v[slot, t, sl])
            while len(terms) > 1:
                pairs = [terms[i] + terms[i + 1] for i in range(0, len(terms) - 1, 2)]
                terms = pairs + ([terms[-1]] if len(terms) % 2 else [])
            o_v[slot, t, sl] = terms[0]

    return compute


def _peer_step_sc(eid_c, a, layer_c, eid_h, xn, layer_h, u_tabs, v_tabs):
    Tc, Th = eid_c.shape[0], eid_h.shape[0]
    info, mesh, nb_c = _sc_mesh_and_batches(Tc)
    _, _, nb_h = _sc_mesh_and_batches(Th)

    @functools.partial(
        pl.kernel, mesh=mesh,
        out_type=(jax.ShapeDtypeStruct((Tc, D_MODEL), jnp.float32), jax.ShapeDtypeStruct((Th, N_SEL), jnp.float32)),
        scratch_types=[
            pltpu.VMEM((2, SC_TOKENS, N_SEL), jnp.int32),
            pltpu.VMEM((2, SC_TOKENS, N_SEL), jnp.float32),
            pltpu.VMEM((2, SC_TOKENS, D_MODEL), jnp.float32),
            pltpu.VMEM((2, SC_TOKENS, N_SEL), jnp.int32),
            pltpu.VMEM((2, SC_TOKENS, D_MODEL), jnp.float32),
            pltpu.VMEM((2, SC_TOKENS, N_SEL), jnp.float32),
            pltpu.VMEM((GATHER_BUFS, GATHER_ROWS, D_MODEL), jnp.float32),
            pltpu.VMEM((GATHER_ROWS * ACC_STRIDE,), jnp.float32),
            pltpu.SemaphoreType.DMA((GATHER_BUFS,)),
            pltpu.SemaphoreType.DMA((2,)),
            pltpu.SemaphoreType.DMA((2,)),
            pltpu.SemaphoreType.DMA((2,)),
            pltpu.SemaphoreType.DMA((2,)),
        ],
        compiler_params=pltpu.CompilerParams(needs_layout_passes=False),
        name="peer_step_sc",
    )
    def k(eidc_hbm, a_hbm, eidh_hbm, xn_hbm, u_hbm, v_hbm, c_hbm, hp_hbm,
          idxc_v, a_v, oc_v, idxh_v, x_v, oh_v, bufs, acc_v, gsem, ssem_c, osem_c, ssem_h, osem_h):
        wid = lax.axis_index("s") * info.num_cores + lax.axis_index("c")
        u_tab = u_hbm.at[layer_h]
        first_h = lambda: _stage_copies(wid, nb_h, [eidh_hbm, xn_hbm], [idxh_v, x_v], ssem_h, 0, 0)
        for c in first_h():
            c.start()

        def hand_over(head, b):
            if head == 0:
                for c in first_h():
                    c.wait()
            idx = idxh_v[0, 0, pl.ds(head * GATHER_ROWS, GATHER_ROWS)]
            pltpu.async_copy(u_tab.at[idx], bufs.at[b], gsem.at[b])

        _sc_gather_stream(wid, nb_c, v_hbm.at[layer_c], [eidc_hbm, a_hbm], [idxc_v, a_v], c_hbm, oc_v, bufs,
                          gsem, ssem_c, osem_c, _combine_compute(a_v, oc_v, bufs), hand_over=hand_over)
        _sc_gather_stream(wid, nb_h, u_tab, [eidh_hbm, xn_hbm], [idxh_v, x_v], hp_hbm, oh_v, bufs,
                          gsem, ssem_h, osem_h, _hidden_compute(x_v, oh_v, bufs, acc_v), handed_over=True)

    return k(eid_c, a, eid_h, xn, u_tabs, v_tabs)


def _peer_hidden_sc(eid, xn, u_tabs, layer):
    T = eid.shape[0]
    info, mesh, n_batches = _sc_mesh_and_batches(T)

    @functools.partial(
        pl.kernel, mesh=mesh,
        out_type=jax.ShapeDtypeStruct((T, N_SEL), jnp.float32),
        scratch_types=[
            pltpu.VMEM((2, SC_TOKENS, N_SEL), jnp.int32),
            pltpu.VMEM((2, SC_TOKENS, D_MODEL), jnp.float32),
            pltpu.VMEM((2, SC_TOKENS, N_SEL), jnp.float32),
            pltpu.VMEM((GATHER_BUFS, GATHER_ROWS, D_MODEL), jnp.float32),
            pltpu.VMEM((GATHER_ROWS * ACC_STRIDE,), jnp.float32),
            pltpu.SemaphoreType.DMA((GATHER_BUFS,)),
            pltpu.SemaphoreType.DMA((2,)),
            pltpu.SemaphoreType.DMA((2,)),
        ],
        compiler_params=pltpu.CompilerParams(needs_layout_passes=False),
        name="peer_hidden_sc",
    )
    def k(eid_hbm, xn_hbm, u_hbm, out_hbm, idx_v, x_v, o_v, bufs, acc_v, gsem, ssem, osem):
        wid = lax.axis_index("s") * info.num_cores + lax.axis_index("c")
        _sc_gather_stream(wid, n_batches, u_hbm.at[layer], [eid_hbm, xn_hbm], [idx_v, x_v], out_hbm, o_v, bufs,
                          gsem, ssem, osem, _hidden_compute(x_v, o_v, bufs, acc_v))

    return k(eid, xn, u_tabs)


def _peer_combine_sc(eid, a, v_tabs, layer):
    T = eid.shape[0]
    info, mesh, n_batches = _sc_mesh_and_batches(T)

    @functools.partial(
        pl.kernel, mesh=mesh,
        out_type=jax.ShapeDtypeStruct((T, D_MODEL), jnp.float32),
        scratch_types=[
            pltpu.VMEM((2, SC_TOKENS, N_SEL), jnp.int32),
            pltpu.VMEM((2, SC_TOKENS, N_SEL), jnp.float32),
            pltpu.VMEM((2, SC_TOKENS, D_MODEL), jnp.float32),
            pltpu.VMEM((GATHER_BUFS, GATHER_ROWS, D_MODEL), jnp.float32),
            pltpu.SemaphoreType.DMA((GATHER_BUFS,)),
            pltpu.SemaphoreType.DMA((2,)),
            pltpu.SemaphoreType.DMA((2,)),
        ],
        compiler_params=pltpu.CompilerParams(needs_layout_passes=False),
        name="peer_combine_sc",
    )
    def k(eid_hbm, a_hbm, v_hbm, out_hbm, idx_v, a_v, o_v, bufs, gsem, ssem, osem):
        wid = lax.axis_index("s") * info.num_cores + lax.axis_index("c")
        _sc_gather_stream(wid, n_batches, v_hbm.at[layer], [eid_hbm, a_hbm], [idx_v, a_v], out_hbm, o_v, bufs,
                          gsem, ssem, osem, _combine_compute(a_v, o_v, bufs))

    return k(eid, a, v_tabs)


def _peer_act_kernel(hp_ref, gate_ref, a_ref):
    a_ref[...] = _gelu(hp_ref[...]) * gate_ref[...]


ELEMENTWISE_ROWS = 1024


def _peer_act(hpre, gate):
    T = hpre.shape[0]
    rows = ELEMENTWISE_ROWS
    assert T % rows == 0
    spec = pl.BlockSpec((rows, N_SEL), lambda i: (i, 0))
    return pl.pallas_call(
        _peer_act_kernel, grid=(T // rows,), in_specs=[spec, spec], out_specs=spec,
        out_shape=jax.ShapeDtypeStruct((T, N_SEL), jnp.float32),
        compiler_params=pltpu.CompilerParams(dimension_semantics=("arbitrary",)),
        name="peer_act",
    )(hpre, gate)


def _residual_kernel(h_ref, c_ref, g_ref, o_ref, *, final_norm):
    y = h_ref[...] + c_ref[...]
    o_ref[...] = _rmsnorm_rows(y, g_ref[...]) if final_norm else y


def _residual(h, c, gfin, *, final_norm):
    T = h.shape[0]
    rows = ELEMENTWISE_ROWS
    assert T % rows == 0
    spec = pl.BlockSpec((rows, D_MODEL), lambda i: (i, 0))
    return pl.pallas_call(
        functools.partial(_residual_kernel, final_norm=final_norm), grid=(T // rows,),
        in_specs=[spec, spec, pl.BlockSpec((1, D_MODEL), lambda i: (0, 0))], out_specs=spec,
        out_shape=jax.ShapeDtypeStruct((T, D_MODEL), jnp.float32),
        compiler_params=pltpu.CompilerParams(dimension_semantics=("arbitrary",)),
        name="peer_residual",
    )(h, c, gfin.reshape(1, D_MODEL))


KV_WIDTH = N_KV_HEADS * HEAD_DIM
BF16 = jnp.bfloat16


def _qkv(h, gkv, gq, wkv_ref, wq_ref):
    kv = jnp.dot(_rmsnorm_rows(h, gkv).astype(BF16), wkv_ref[...], preferred_element_type=jnp.float32)
    q = jnp.dot(_rmsnorm_rows(h, gq).astype(BF16), wq_ref[...], preferred_element_type=jnp.float32)
    return kv, q


def _sink_softmax_pv(parts, sink):
    m = sink
    for s, _ in parts:
        m = jnp.maximum(m, jnp.max(s, axis=-1, keepdims=True))
    den = jnp.exp(sink - m)
    acc = None
    for s, v in parts:
        e = jnp.exp(s - m)
        den = den + jnp.sum(e, axis=-1, keepdims=True)
        pv = jnp.dot(e.astype(BF16), v.astype(BF16), preferred_element_type=jnp.float32)
        acc = pv if acc is None else acc + pv
    return acc / den


def _nt_dot(a, b):
    return lax.dot_general(a.astype(BF16), b.astype(BF16), (((1,), (1,)), ((), ())),
                           preferred_element_type=jnp.float32)


def _attn_prompt_kernel(h_ref, gkv_ref, gq_ref, wkv_ref, wq_ref, wo_ref, sink_ref,
                        o_ref, kvw_ref, prev_ref):
    blk = pl.program_id(1)
    h = h_ref[...]
    kv, q = _qkv(h, gkv_ref[...], gq_ref[...], wkv_ref, wq_ref)
    kvw_ref[0] = kv

    @pl.when(blk == 0)
    def _():
        prev_ref[...] = jnp.zeros_like(prev_ref)

    prev = prev_ref[...]
    qi = lax.broadcasted_iota(jnp.int32, (WINDOW, WINDOW), 0)
    kj = lax.broadcasted_iota(jnp.int32, (WINDOW, WINDOW), 1)
    prev_ok = jnp.logical_and(kj > qi, blk > 0)
    cur_ok = kj <= qi
    heads = []
    for kvh in range(N_KV_HEADS):
        ks = slice(kvh * HEAD_DIM, (kvh + 1) * HEAD_DIM)
        vs = slice(KV_WIDTH + kvh * HEAD_DIM, KV_WIDTH + (kvh + 1) * HEAD_DIM)
        for g in range(Q_PER_KV):
            hq = kvh * Q_PER_KV + g
            qh = q[:, hq * HEAD_DIM:(hq + 1) * HEAD_DIM]
            sp = jnp.where(prev_ok, _nt_dot(qh, prev[:, ks]) * ATTN_SCALE, -jnp.inf)
            sc = jnp.where(cur_ok, _nt_dot(qh, kv[:, ks]) * ATTN_SCALE, -jnp.inf)
            heads.append(_sink_softmax_pv([(sp, prev[:, vs]), (sc, kv[:, vs])], sink_ref[hq]))
    o = jnp.concatenate(heads, axis=1)
    o_ref[...] = h + jnp.dot(o.astype(BF16), wo_ref[...], preferred_element_type=jnp.float32)
    prev_ref[...] = kv


def _attn_prompt(h, gkv, gq, wkv, wq, wo, sinks, *, n_seq, seq_len):
    nb = seq_len // WINDOW
    row_spec = pl.BlockSpec((WINDOW, D_MODEL), lambda n, b: (n * nb + b, 0))
    full = lambda shape: pl.BlockSpec(shape, lambda n, b: (0,) * len(shape))
    return pl.pallas_call(
        _attn_prompt_kernel,
        grid=(n_seq, nb),
        in_specs=[
            row_spec, full((1, D_MODEL)), full((1, D_MODEL)),
            full((D_MODEL, 2 * KV_WIDTH)), full((D_MODEL, D_MODEL)), full((D_MODEL, D_MODEL)),
            pl.BlockSpec(memory_space=pltpu.SMEM),
        ],
        out_specs=[row_spec, pl.BlockSpec((1, WINDOW, 2 * KV_WIDTH), lambda n, b: (n, 0, 0))],
        out_shape=[
            jax.ShapeDtypeStruct(h.shape, jnp.float32),
            jax.ShapeDtypeStruct((n_seq, WINDOW, 2 * KV_WIDTH), jnp.float32),
        ],
        scratch_shapes=[pltpu.VMEM((WINDOW, 2 * KV_WIDTH), jnp.float32)],
        compiler_params=pltpu.CompilerParams(
            dimension_semantics=("arbitrary", "arbitrary"), vmem_limit_bytes=VMEM_LIMIT_BYTES),
        name="attn_prompt",
    )(h, gkv.reshape(1, D_MODEL), gq.reshape(1, D_MODEL), wkv, wq, wo, sinks)


ATTN_SEQS = 16
FIRST_PROMPT_PIECES = 4


def _attn_sample_kernel(h_ref, ck_ref, cv_ref, gkv_ref, gq_ref, wkv_ref, wq_ref, wo_ref, sink_ref,
                        o_ref, kw_ref, vw_ref, kv_ref, q_ref, att_ref, *, n_new):
    h = h_ref[...]
    kv, q = _qkv(h, gkv_ref[...], gq_ref[...], wkv_ref, wq_ref)
    kv_ref[...] = kv
    q_ref[...] = q
    n_seq = h.shape[0] // n_new
    rows = Q_PER_KV * n_new
    qpos_c = lax.broadcasted_iota(jnp.int32, (rows, WINDOW), 0) % n_new
    cache_ok = lax.broadcasted_iota(jnp.int32, (rows, WINDOW), 1) > qpos_c
    qpos_n = lax.broadcasted_iota(jnp.int32, (rows, n_new), 0) % n_new
    new_ok = lax.broadcasted_iota(jnp.int32, (rows, n_new), 1) <= qpos_n

    def per_seq(n, carry):
        r0 = pl.multiple_of(n * n_new, n_new)
        kvn = kv_ref[pl.ds(r0, n_new), :]
        qn = q_ref[pl.ds(r0, n_new), :]
        ck = ck_ref[n]
        cv = cv_ref[n]
        kw_ref[n, 0:WINDOW - n_new, :] = ck[n_new:, :]
        kw_ref[n, WINDOW - n_new:WINDOW, :] = kvn[:, 0:KV_WIDTH]
        vw_ref[n, 0:WINDOW - n_new, :] = cv[n_new:, :]
        vw_ref[n, WINDOW - n_new:WINDOW, :] = kvn[:, KV_WIDTH:2 * KV_WIDTH]
        outs = []
        for kvh in range(N_KV_HEADS):
            ks = slice(kvh * HEAD_DIM, (kvh + 1) * HEAD_DIM)
            vs = slice(KV_WIDTH + kvh * HEAD_DIM, KV_WIDTH + (kvh + 1) * HEAD_DIM)
            qs = jnp.concatenate(
                [qn[:, (kvh * Q_PER_KV + g) * HEAD_DIM:(kvh * Q_PER_KV + g + 1) * HEAD_DIM] for g in range(Q_PER_KV)],
                axis=0)
            s_c = jnp.where(cache_ok, _nt_dot(qs, ck[:, ks]) * ATTN_SCALE, -jnp.inf)
            s_n = jnp.where(new_ok, _nt_dot(qs, kvn[:, ks]) * ATTN_SCALE, -jnp.inf)
            sink = sink_ref[kvh][:, 0:1]
            o = _sink_softmax_pv([(s_c, cv[:, ks]), (s_n, kvn[:, vs])], sink)
            outs += [o[g * n_new:(g + 1) * n_new, :] for g in range(Q_PER_KV)]
        att_ref[pl.ds(r0, n_new), :] = jnp.concatenate(outs, axis=1)
        return carry

    lax.fori_loop(0, n_seq, per_seq, 0)
    o_ref[...] = h + jnp.dot(att_ref[...].astype(BF16), wo_ref[...], preferred_element_type=jnp.float32)


def _attn_sample(h, cache_k, cache_v, gkv, gq, wkv, wq, wo, sinks, *, n_new):
    n_seq = cache_k.shape[0]
    sb = ATTN_SEQS
    rows = sb * n_new
    row_spec = pl.BlockSpec((rows, D_MODEL), lambda i: (i, 0))
    win_spec = pl.BlockSpec((sb, WINDOW, KV_WIDTH), lambda i: (i, 0, 0))
    full = lambda shape: pl.BlockSpec(shape, lambda i: (0,) * len(shape))
    sink_rows = jnp.repeat(sinks.reshape(N_KV_HEADS, Q_PER_KV), n_new, axis=1)[:, :, None]
    sink_rows = jnp.broadcast_to(sink_rows, (N_KV_HEADS, Q_PER_KV * n_new, LANES))
    return pl.pallas_call(
        functools.partial(_attn_sample_kernel, n_new=n_new),
        grid=(n_seq // sb,),
        in_specs=[
            row_spec, win_spec, win_spec, full((1, D_MODEL)), full((1, D_MODEL)),
            full((D_MODEL, 2 * KV_WIDTH)), full((D_MODEL, D_MODEL)), full((D_MODEL, D_MODEL)),
            full((N_KV_HEADS, Q_PER_KV * n_new, LANES)),
        ],
        out_specs=[row_spec, win_spec, win_spec],
        out_shape=[
            jax.ShapeDtypeStruct(h.shape, jnp.float32),
            jax.ShapeDtypeStruct((n_seq, WINDOW, KV_WIDTH), jnp.float32),
            jax.ShapeDtypeStruct((n_seq, WINDOW, KV_WIDTH), jnp.float32),
        ],
        scratch_shapes=[
            pltpu.VMEM((rows, 2 * KV_WIDTH), jnp.float32),
            pltpu.VMEM((rows, D_MODEL), jnp.float32),
            pltpu.VMEM((rows, D_MODEL), jnp.float32),
        ],
        compiler_params=pltpu.CompilerParams(
            dimension_semantics=("arbitrary",), vmem_limit_bytes=VMEM_LIMIT_BYTES),
        name="attn_sample",
    )(h, cache_k, cache_v, gkv.reshape(1, D_MODEL), gq.reshape(1, D_MODEL), wkv, wq, wo, sink_rows)


def kernel(x_prompt, x_sample, state_ssm_re, state_ssm_im, cache_k_win, cache_v_win, norm_mix, norm_ffn, norm_kv, norm_final, ssm_lam_re, ssm_lam_im, ssm_log_dt, ssm_b_re, ssm_b_im, ssm_c_re, ssm_c_im, ssm_d, ssm_w_glu, w_kv, w_q, attn_sinks, w_o, peer_w_q, peer_sub_keys, peer_u, peer_v):
    bmat, cmat, apr, api = _s5_discretize(ssm_lam_re[0], ssm_lam_im[0], ssm_log_dt[0], ssm_b_re[0], ssm_b_im[0], ssm_c_re[0], ssm_c_im[0])
    cmat = cmat.astype(jnp.bfloat16)
    wglu = ssm_w_glu[0].astype(jnp.bfloat16)
    wkv = w_kv.astype(BF16)
    wq = w_q[0].astype(BF16)
    wo = w_o[0].astype(BF16)
    peer_wq = [peer_w_q[layer].astype(BF16).reshape(D_MODEL, PEER_HEADS, 2 * PEER_DHALF).transpose(1, 0, 2)
               for layer in range(2)]
    peer_sk = [peer_sub_keys[layer].astype(BF16) for layer in range(2)]

    def route(h, layer):
        xn, eid_t, gate_t = _peer_route(h, norm_ffn[layer], peer_wq[layer], peer_sk[layer])
        return dict(h=h, xn=xn, eid=eid_t.T, gate=gate_t.T, layer=layer)

    win = lambda a: a.reshape(a.shape[0], WINDOW, N_KV_HEADS, HEAD_DIM)

    seq_len = x_prompt.shape[1]
    xp_all = x_prompt.reshape(-1, D_MODEL)

    n_prompt, n_s = x_prompt.shape[0], x_sample.shape[0]
    xs = x_sample.reshape(-1, D_MODEL)
    h1, sre_s, sim_s = _s5_mixer(xs, _state_to_chunks(state_ssm_re[0]), _state_to_chunks(state_ssm_im[0]), norm_mix[0],
                                 ssm_d[0], bmat, cmat, apr, api, wglu, chain=False, seq_len=x_sample.shape[1])
    units = [route(h1, 0)]
    group_units = [[0]]
    prompt_states = []
    for seq in range(n_prompt):
        n_pieces = FIRST_PROMPT_PIECES if seq == 0 else 1
        piece = seq_len // n_pieces
        sr = si = jnp.zeros((1, N_CHUNKS, 1, CHUNK_STATE), jnp.float32)
        group_units.append([])
        for p in range(n_pieces):
            row0 = seq * seq_len + p * piece
            h1, sr, si = _s5_mixer(xp_all, sr, si, norm_mix[0], ssm_d[0], bmat, cmat, apr, api, wglu,
                                   chain=True, seq_len=piece, row0=row0)
            group_units[-1].append(len(units))
            units.append(route(h1, 0))
        prompt_states.append((sr, si))
    n_layer0 = len(units)
    windows = {}

    def layer1_unit(g):
        mine = [units[i] for i in group_units[g]]
        h2 = jnp.concatenate([_residual(u['h'], u['c'], norm_final, final_norm=False) for u in mine], axis=0)
        if g == 0:
            h3, kw, vw = _attn_sample(h2, cache_k_win.reshape(n_s, WINDOW, KV_WIDTH),
                                      cache_v_win.reshape(n_s, WINDOW, KV_WIDTH), norm_kv, norm_mix[1],
                                      wkv, wq, wo, attn_sinks[0], n_new=x_sample.shape[1])
            windows[g] = (win(kw), win(vw))
        else:
            h3, kvw = _attn_prompt(h2, norm_kv, norm_mix[1], wkv, wq, wo, attn_sinks[0], n_seq=1, seq_len=seq_len)
            windows[g] = (win(kvw[:, :, :KV_WIDTH]), win(kvw[:, :, KV_WIDTH:]))
        return route(h3, 1)

    layer1_order = list(range(1, n_prompt + 1)) + [0]
    n_units = n_layer0 + 1 + n_prompt
    for k in range(n_units + 2):
        if n_layer0 <= k < n_units:
            units.append(layer1_unit(layer1_order[k - n_layer0]))
        hid = units[k] if k < n_units else None
        comb = units[k - 2] if k >= 2 else None
        if comb is not None:
            comb['a'] = _peer_act(comb['hpre'], comb['gate'])
        if hid is not None and comb is not None:
            comb['c'], hid['hpre'] = _peer_step_sc(comb['eid'], comb['a'], comb['layer'],
                                                   hid['eid'], hid['xn'], hid['layer'], peer_u, peer_v)
        elif hid is not None:
            hid['hpre'] = _peer_hidden_sc(hid['eid'], hid['xn'], peer_u, hid['layer'])
        else:
            comb['c'] = _peer_combine_sc(comb['eid'], comb['a'], peer_v, comb['layer'])

    ys = [_residual(u['h'], u['c'], norm_final, final_norm=True) for u in units[n_layer0:]]
    y_s = ys[-1].reshape(x_sample.shape)
    y_p = jnp.concatenate(ys[:-1], axis=0).reshape(x_prompt.shape)
    cat = lambda parts: jnp.concatenate(parts, axis=0)
    sre_p = cat([_chunks_to_state(sr) for sr, _ in prompt_states])
    sim_p = cat([_chunks_to_state(si) for _, si in prompt_states])
    kw_p = cat([windows[g][0] for g in range(1, n_prompt + 1)])
    vw_p = cat([windows[g][1] for g in range(1, n_prompt + 1)])
    return (y_p, y_s, sre_p[None], sim_p[None], kw_p, vw_p,
            _chunks_to_state(sre_s)[None], _chunks_to_state(sim_s)[None], windows[0][0], windows[0][1])
```

```python
import functools
import math

import jax
import jax.numpy as jnp
from jax import lax
from jax.experimental import pallas as pl
from jax.experimental.pallas import tpu as pltpu
from jax.experimental.pallas import tpu_sc as plsc

D_MODEL = 1024
GROUP_SIZE = 16
N_GROUPS = D_MODEL // GROUP_SIZE
STATE_DIM = 64
HEAD_DIM = 64
N_Q_HEADS = D_MODEL // HEAD_DIM
N_KV_HEADS = N_Q_HEADS // 8
Q_PER_KV = N_Q_HEADS // N_KV_HEADS
WINDOW = 128
ATTN_SCALE = 1.0 / math.sqrt(HEAD_DIM)
PEER_HEADS = 8
N_KEYS = 128
PEER_TOPK = 16
PEER_DHALF = 128
EPS = 1e-5

LANES = 128
SUBLANES = 8
VMEM_LIMIT_BYTES = 56 * 1024 * 1024

GROUPS_PER_CHUNK = LANES // GROUP_SIZE
N_CHUNKS = N_GROUPS // GROUPS_PER_CHUNK
CHUNK_STATE = GROUPS_PER_CHUNK * STATE_DIM
S5_ROWS = 256


def _rmsnorm_rows(x, g):
    r = lax.rsqrt(jnp.mean(x * x, axis=-1, keepdims=True) + EPS)
    return x * r * g


def _gelu(x):
    return 0.5 * x * (1.0 + lax.erf(x * (1.0 / math.sqrt(2.0))))


def _s5_discretize(lam_re, lam_im, log_dt, b_re, b_im, c_re, c_im):
    f32 = jnp.float32
    lr = lam_re.astype(f32)
    li = lam_im.astype(f32)
    dt = jnp.exp(log_dt.astype(f32))[:, None]
    mag = jnp.exp(lr * dt)
    ab_re = mag * jnp.cos(li * dt)
    ab_im = mag * jnp.sin(li * dt)
    den = lr * lr + li * li
    f_re = ((ab_re - 1.0) * lr + ab_im * li) / den
    f_im = (ab_im * lr - (ab_re - 1.0) * li) / den
    br = b_re.astype(f32)
    bi = b_im.astype(f32)
    bb_re = f_re[..., None] * br - f_im[..., None] * bi
    bb_im = f_re[..., None] * bi + f_im[..., None] * br
    eye = jnp.eye(GROUPS_PER_CHUNK, dtype=f32)

    def chunk_rows(v):
        return v.reshape(N_CHUNKS, 1, CHUNK_STATE)

    def in_blocks(bb):
        t = bb.reshape(N_CHUNKS, GROUPS_PER_CHUNK, STATE_DIM, GROUP_SIZE).transpose(0, 1, 3, 2)
        return jnp.einsum('mgjp,gh->mgjhp', t, eye).reshape(N_CHUNKS, LANES, CHUNK_STATE)

    def out_blocks(c):
        t = c.astype(f32).reshape(N_CHUNKS, GROUPS_PER_CHUNK, GROUP_SIZE, STATE_DIM).transpose(0, 1, 3, 2)
        return jnp.einsum('mgpj,gh->mgphj', t, eye).reshape(N_CHUNKS, CHUNK_STATE, LANES)

    bfull = jnp.concatenate([in_blocks(bb_re), in_blocks(bb_im)], axis=2)
    b_hi = bfull.astype(jnp.bfloat16)
    b_lo = (bfull - b_hi.astype(f32)).astype(jnp.bfloat16)
    bmat = (jnp.concatenate([b_hi, b_hi], axis=1), b_lo)
    cmat = jnp.concatenate([out_blocks(c_re), -out_blocks(c_im)], axis=1)
    pr, pi = [ab_re], [ab_im]
    for _ in range(SUBLANES - 1):
        pr, pi = pr + [pr[-1] * ab_re - pi[-1] * ab_im], pi + [pr[-1] * ab_im + pi[-1] * ab_re]
    apr = jnp.concatenate([chunk_rows(v) for v in pr], axis=1)
    api = jnp.concatenate([chunk_rows(v) for v in pi], axis=1)
    return bmat, cmat, apr, api


def _split_bf16(x):
    hi = x.astype(jnp.bfloat16)
    return hi, (x - hi.astype(jnp.float32)).astype(jnp.bfloat16)


def _s5_kernel(x_ref, g_ref, d_ref, bhh_ref, blo_ref, c_ref, apr_ref, api_ref, h0r_ref, h0i_ref, w_ref,
               o_ref, sr_ref, si_ref, u_ref, us_ref, st_ref, cr_ref, ci_ref, z_ref, *, chain, blocks_per_seq):
    rb = pl.program_id(0)
    m = pl.program_id(1)
    rows = x_ref.shape[0]
    n_tiles = rows // SUBLANES

    @pl.when(m == 0)
    def _():
        u = _rmsnorm_rows(x_ref[...], g_ref[...])
        for mm in range(N_CHUNKS):
            uc = u[:, mm * LANES:(mm + 1) * LANES]
            u_ref[mm] = uc
            hi, lo = _split_bf16(uc)
            us_ref[mm, :, 0:LANES] = hi
            us_ref[mm, :, LANES:2 * LANES] = lo

    u = u_ref[m]
    us = us_ref[m]
    st_ref[...] = (jnp.dot(us, bhh_ref[m], preferred_element_type=jnp.float32)
                   + jnp.dot(us[:, 0:LANES], blo_ref[m], preferred_element_type=jnp.float32))

    apr = apr_ref[0]
    api = api_ref[0]
    row = lax.broadcasted_iota(jnp.int32, (SUBLANES, CHUNK_STATE), 0)

    if chain:
        @pl.when(rb % blocks_per_seq == 0)
        def _():
            cr_ref[m] = h0r_ref[0, 0]
            ci_ref[m] = h0i_ref[0, 0]

    def tile_step(k, carry):
        r0 = pl.multiple_of(k * SUBLANES, SUBLANES)
        xr = st_ref[pl.ds(r0, SUBLANES), 0:CHUNK_STATE]
        xi = st_ref[pl.ds(r0, SUBLANES), CHUNK_STATE:2 * CHUNK_STATE]
        for d in (1, 2, 4):
            ar = apr[d - 1:d, :]
            ai = api[d - 1:d, :]
            sr = jnp.where(row >= d, pltpu.roll(xr, d, axis=0), 0.0)
            si = jnp.where(row >= d, pltpu.roll(xi, d, axis=0), 0.0)
            xr, xi = xr + ar * sr - ai * si, xi + ar * si + ai * sr
        if chain:
            cr, ci = carry
        else:
            cr = h0r_ref[k, 0]
            ci = h0i_ref[k, 0]
        hr = xr + apr * cr - api * ci
        hi = xi + apr * ci + api * cr
        st_ref[pl.ds(r0, SUBLANES), 0:CHUNK_STATE] = hr
        st_ref[pl.ds(r0, SUBLANES), CHUNK_STATE:2 * CHUNK_STATE] = hi
        lr_ = hr[SUBLANES - 1:SUBLANES, :]
        li_ = hi[SUBLANES - 1:SUBLANES, :]
        if chain:
            return lr_, li_
        sr_ref[k, m] = lr_
        si_ref[k, m] = li_
        return carry

    if chain:
        cr, ci = lax.fori_loop(0, n_tiles, tile_step, (cr_ref[m], ci_ref[m]))
        cr_ref[m] = cr
        ci_ref[m] = ci
        sr_ref[0, m] = cr
        si_ref[0, m] = ci
    else:
        lax.fori_loop(0, n_tiles, tile_step, 0)

    y = jnp.dot(st_ref[...].astype(jnp.bfloat16), c_ref[m], preferred_element_type=jnp.float32)
    y = y + d_ref[0] * u
    z_ref[m] = _gelu(y).astype(jnp.bfloat16)

    @pl.when(m == N_CHUNKS - 1)
    def _():
        zfull = jnp.concatenate([z_ref[mm] for mm in range(N_CHUNKS)], axis=1)
        zz = jnp.dot(zfull, w_ref[...], preferred_element_type=jnp.float32)
        a = zz[:, :D_MODEL]
        b = zz[:, D_MODEL:]
        o_ref[...] = x_ref[...] + a * (1.0 / (1.0 + jnp.exp(-b)))


def _s5_mixer(x, h0r, h0i, g, d_skip, bmat, cmat_bf16, apr, api, wglu_bf16, *, chain, seq_len, row0=0):
    nseq = h0r.shape[0]
    T = nseq * seq_len
    rows = S5_ROWS
    blk0 = row0 // rows
    if chain:
        blocks_per_seq = seq_len // rows
        seq_blk = 1
        seq_map = lambda rb, m: (rb // blocks_per_seq, m, 0, 0)
        out_map = lambda rb, m: (rb // blocks_per_seq, 0, 0, 0)
    else:
        assert seq_len == SUBLANES
        blocks_per_seq = 1
        seq_blk = rows // SUBLANES
        seq_map = lambda rb, m: (rb, m, 0, 0)
        out_map = lambda rb, m: (rb, 0, 0, 0)
    grid = (T // rows, N_CHUNKS)
    kern = functools.partial(_s5_kernel, chain=chain, blocks_per_seq=blocks_per_seq)
    st_spec = pl.BlockSpec((seq_blk, 1, 1, CHUNK_STATE), seq_map)
    out_st_spec = pl.BlockSpec((seq_blk, N_CHUNKS, 1, CHUNK_STATE), out_map)
    z, sr, si = pl.pallas_call(
        kern,
        grid=grid,
        in_specs=[
            pl.BlockSpec((rows, D_MODEL), lambda rb, m: (rb + blk0, 0)),
            pl.BlockSpec((1, D_MODEL), lambda rb, m: (0, 0)),
            pl.BlockSpec((1, 1, LANES), lambda rb, m: (m, 0, 0)),
            pl.BlockSpec((N_CHUNKS, 2 * LANES, 2 * CHUNK_STATE), lambda rb, m: (0, 0, 0)),
            pl.BlockSpec((N_CHUNKS, LANES, 2 * CHUNK_STATE), lambda rb, m: (0, 0, 0)),
            pl.BlockSpec((N_CHUNKS, 2 * CHUNK_STATE, LANES), lambda rb, m: (0, 0, 0)),
            pl.BlockSpec((1, SUBLANES, CHUNK_STATE), lambda rb, m: (m, 0, 0)),
            pl.BlockSpec((1, SUBLANES, CHUNK_STATE), lambda rb, m: (m, 0, 0)),
            st_spec, st_spec,
            pl.BlockSpec((D_MODEL, 2 * D_MODEL), lambda rb, m: (0, 0)),
        ],
        out_specs=[
            pl.BlockSpec((rows, D_MODEL), lambda rb, m: (rb, 0)),
            out_st_spec, out_st_spec,
        ],
        out_shape=[
            jax.ShapeDtypeStruct((T, D_MODEL), jnp.float32),
            jax.ShapeDtypeStruct((nseq, N_CHUNKS, 1, CHUNK_STATE), jnp.float32),
            jax.ShapeDtypeStruct((nseq, N_CHUNKS, 1, CHUNK_STATE), jnp.float32),
        ],
        scratch_shapes=[
            pltpu.VMEM((N_CHUNKS, rows, LANES), jnp.float32),
            pltpu.VMEM((N_CHUNKS, rows, 2 * LANES), jnp.bfloat16),
            pltpu.VMEM((rows, 2 * CHUNK_STATE), jnp.float32),
            pltpu.VMEM((N_CHUNKS, 1, CHUNK_STATE), jnp.float32),
            pltpu.VMEM((N_CHUNKS, 1, CHUNK_STATE), jnp.float32),
            pltpu.VMEM((N_CHUNKS, rows, LANES), jnp.bfloat16),
        ],
        compiler_params=pltpu.CompilerParams(
            dimension_semantics=("arbitrary", "arbitrary"), vmem_limit_bytes=VMEM_LIMIT_BYTES),
        name="s5_mixer",
    )(x, g.reshape(1, D_MODEL), d_skip.reshape(N_CHUNKS, 1, LANES), bmat[0], bmat[1], cmat_bf16, apr, api,
      h0r, h0i, wglu_bf16)
    return z, sr, si


def _state_to_chunks(h):
    return h.reshape(h.shape[0], N_CHUNKS, 1, CHUNK_STATE)


def _chunks_to_state(s):
    return s.reshape(s.shape[0], N_GROUPS, STATE_DIM)


ROUTE_ROWS = 256


def _topk_rows(s, payload):
    n_rows = s.shape[0]
    row = lax.broadcasted_iota(jnp.int32, s.shape, 0)
    vals, picks = [], []
    for _ in range(PEER_TOPK):
        m = jnp.max(s, axis=0, keepdims=True)
        pos = jnp.min(jnp.where(s == m, row, n_rows), axis=0, keepdims=True)
        sel = row == pos
        vals.append(m)
        if payload is None:
            picks.append(pos)
        else:
            picks.append(jnp.max(jnp.where(sel, payload, -1), axis=0, keepdims=True))
        s = jnp.where(sel, -jnp.inf, s)
    return jnp.concatenate(vals, axis=0), jnp.concatenate(picks, axis=0)


def _pair_rows(a0, a1, combine):
    half = PEER_TOPK // 2
    parts = [combine(a0[0:1, :], a1)]
    parts += [combine(a0[i:i + 1, :], a1[0:half, :]) for i in range(1, half)]
    parts.append(combine(a0[half:PEER_TOPK, :], a1[0:1, :]))
    return jnp.concatenate(parts, axis=0)


def _peer_route_kernel(h_ref, g_ref, wq_ref, sk_ref, xn_ref, eid_ref, gate_ref, xb_ref):
    hd = pl.program_id(1)

    @pl.when(hd == 0)
    def _():
        xn = _rmsnorm_rows(h_ref[...], g_ref[...])
        xn_ref[...] = xn
        xb_ref[...] = xn.astype(jnp.bfloat16)

    q = jnp.dot(xb_ref[...], wq_ref[hd], preferred_element_type=jnp.float32)
    sv, si = [], []
    for c in range(2):
        qc = q[:, c * PEER_DHALF:(c + 1) * PEER_DHALF].astype(jnp.bfloat16)
        st = lax.dot_general(sk_ref[hd, c], qc, (((1,), (1,)), ((), ())), preferred_element_type=jnp.float32)
        v, i = _topk_rows(st, None)
        sv.append(v)
        si.append(i)
    cand = _pair_rows(sv[0], sv[1], lambda a, b: a + b)
    cid = _pair_rows(si[0], si[1], lambda a, b: a * N_KEYS + b)
    fv, eid = _topk_rows(cand, cid)
    e = jnp.exp(fv - fv[0:1, :])
    gate_ref[...] = e / jnp.sum(e, axis=0, keepdims=True)
    eid_ref[...] = eid


def _peer_route(h, g, wq_bf16, sk_bf16):
    T = h.shape[0]
    rows = ROUTE_ROWS
    n_sel = PEER_HEADS * PEER_TOPK
    return pl.pallas_call(
        _peer_route_kernel,
        grid=(T // rows, PEER_HEADS),
        in_specs=[
            pl.BlockSpec((rows, D_MODEL), lambda tb, hd: (tb, 0)),
            pl.BlockSpec((1, D_MODEL), lambda tb, hd: (0, 0)),
            pl.BlockSpec((PEER_HEADS, D_MODEL, 2 * PEER_DHALF), lambda tb, hd: (0, 0, 0)),
            pl.BlockSpec((PEER_HEADS, 2, N_KEYS, PEER_DHALF), lambda tb, hd: (0, 0, 0, 0)),
        ],
        out_specs=[
            pl.BlockSpec((rows, D_MODEL), lambda tb, hd: (tb, 0)),
            pl.BlockSpec((PEER_TOPK, rows), lambda tb, hd: (hd, tb)),
            pl.BlockSpec((PEER_TOPK, rows), lambda tb, hd: (hd, tb)),
        ],
        out_shape=[
            jax.ShapeDtypeStruct((T, D_MODEL), jnp.float32),
            jax.ShapeDtypeStruct((n_sel, T), jnp.int32),
            jax.ShapeDtypeStruct((n_sel, T), jnp.float32),
        ],
        scratch_shapes=[pltpu.VMEM((rows, D_MODEL), jnp.bfloat16)],
        compiler_params=pltpu.CompilerParams(
            dimension_semantics=("arbitrary", "arbitrary"), vmem_limit_bytes=VMEM_LIMIT_BYTES),
        name="peer_route",
    )(h, g.reshape(1, D_MODEL), wq_bf16, sk_bf16)


N_SEL = PEER_HEADS * PEER_TOPK
SC_LANES = 16
GATHER_ROWS = PEER_TOPK
GATHERS_PER_TOKEN = N_SEL // GATHER_ROWS
GATHER_BUFS = 4
SC_TOKENS = 8
ACC_STRIDE = SC_LANES + 1


def _stage_copies(wid, n_batches, stage_srcs, stage_bufs, ssem, bi, slot):
    base = (wid * n_batches + bi) * SC_TOKENS
    return [pltpu.make_async_copy(src.at[pl.ds(base, SC_TOKENS)], buf.at[slot], ssem.at[slot])
            for src, buf in zip(stage_srcs, stage_bufs)]


def _sc_gather_stream(wid, n_batches, tab_hbm, stage_srcs, stage_bufs, out_hbm, o_v, bufs, gsem, ssem, osem,
                      compute, handed_over=False, hand_over=None):
    idx_v = stage_bufs[0]

    def stage_copies(bi, slot):
        return _stage_copies(wid, n_batches, stage_srcs, stage_bufs, ssem, bi, slot)

    def out_copy(bi, slot):
        base = (wid * n_batches + bi) * SC_TOKENS
        return pltpu.make_async_copy(o_v.at[slot], out_hbm.at[pl.ds(base, SC_TOKENS)], osem.at[slot])

    def start(slot, t, kk, b):
        idx = idx_v[slot, t, pl.ds(kk * GATHER_ROWS, GATHER_ROWS)]
        pltpu.async_copy(tab_hbm.at[idx], bufs.at[b], gsem.at[b])

    def wait(b):
        pltpu.make_async_copy(tab_hbm.at[pl.ds(0, GATHER_ROWS)], bufs.at[b], gsem.at[b]).wait()

    if not handed_over:
        for c in stage_copies(0, 0):
            c.start()
        for c in stage_copies(0, 0):
            c.wait()
        for q in range(GATHER_BUFS - 1):
            start(0, q // GATHERS_PER_TOKEN, q % GATHERS_PER_TOKEN, q % GATHER_BUFS)

    def batch(bi, carry):
        slot = bi % 2
        has_next = bi + 1 < n_batches

        @pl.when(has_next)
        def _():
            for c in stage_copies(bi + 1, 1 - slot):
                c.start()

        @pl.when(bi >= 2)
        def _():
            out_copy(bi, slot).wait()

        def tok(t, carry):
            @pl.when(jnp.logical_and(t == SC_TOKENS - 1, has_next))
            def _():
                for c in stage_copies(bi + 1, 1 - slot):
                    c.wait()

            for kk in range(GATHERS_PER_TOKEN):
                nq = kk + GATHER_BUFS - 1
                nk, nb = nq % GATHERS_PER_TOKEN, nq % GATHER_BUFS
                if nq < GATHERS_PER_TOKEN:
                    start(slot, t, nk, nb)
                else:
                    @pl.when(t + 1 < SC_TOKENS)
                    def _():
                        start(slot, t + 1, nk, nb)

                    @pl.when(jnp.logical_and(t + 1 == SC_TOKENS, has_next))
                    def _():
                        start(1 - slot, 0, nk, nb)

                    if hand_over is not None:
                        @pl.when(jnp.logical_and(t + 1 == SC_TOKENS, jnp.logical_not(has_next)))
                        def _():
                            hand_over(nk, nb)

                wait(kk % GATHER_BUFS)
                compute(slot, t, kk, kk % GATHER_BUFS)
            return carry

        lax.fori_loop(0, SC_TOKENS, tok, 0)
        out_copy(bi, slot).start()
        return carry

    lax.fori_loop(0, n_batches, batch, 0)
    if n_batches >= 2:
        out_copy(n_batches - 2, (n_batches - 2) % 2).wait()
    out_copy(n_batches - 1, (n_batches - 1) % 2).wait()


def _sc_mesh_and_batches(n_tokens):
    info = plsc.get_sparse_core_info()
    assert info.num_lanes == SC_LANES
    n_workers = info.num_cores * info.num_subcores
    assert n_tokens % (n_workers * SC_TOKENS) == 0
    mesh = plsc.VectorSubcoreMesh(core_axis_name="c", subcore_axis_name="s")
    return info, mesh, n_tokens // (n_workers * SC_TOKENS)


def _hidden_compute(x_v, o_v, bufs, acc_v):
    lane = lax.iota(jnp.int32, SC_LANES)
    zero = jnp.zeros((SC_LANES,), jnp.float32)

    def compute(slot, t, kk, b):
        @plsc.parallel_loop(0, D_MODEL // SC_LANES, carry=(zero,) * GATHER_ROWS)
        def accs(c, accs):
            xc = x_v[slot, t, pl.ds(c * SC_LANES, SC_LANES)]
            return tuple(accs[r] + bufs[b, r, pl.ds(c * SC_LANES, SC_LANES)] * xc for r in range(GATHER_ROWS))

        for r in range(GATHER_ROWS):
            acc_v[pl.ds(r * ACC_STRIDE, SC_LANES)] = accs[r]
        tot = zero
        for c in range(SC_LANES):
            tot = tot + plsc.load_gather(acc_v, [lane * ACC_STRIDE + c])
        o_v[slot, t, pl.ds(kk * GATHER_ROWS, GATHER_ROWS)] = tot

    return compute


def _combine_compute(a_v, o_v, bufs):
    def compute(slot, t, kk, b):
        svec = jnp.full((SC_LANES,), slot, jnp.int32)
        tvec = jnp.full((SC_LANES,), t, jnp.int32)
        ws = [plsc.load_gather(a_v, [svec, tvec, jnp.full((SC_LANES,), kk * GATHER_ROWS + r, jnp.int32)])
              for r in range(GATHER_ROWS)]

        @plsc.parallel_loop(0, D_MODEL // SC_LANES, unroll=2)
        def _(c):
            sl = pl.ds(c * SC_LANES, SC_LANES)
            terms = [ws[r] * bufs[b, r, sl] for r in range(GATHER_ROWS)]
            if kk != 0:
                terms.append(o_v[slot, t, sl])
            while len(terms) > 1:
                pairs = [terms[i] + terms[i + 1] for i in range(0, len(terms) - 1, 2)]
                terms = pairs + ([terms[-1]] if len(terms) % 2 else [])
            o_v[slot, t, sl] = terms[0]

    return compute


def _peer_step_sc(eid_c, a, layer_c, eid_h, xn, layer_h, u_tabs, v_tabs):
    Tc, Th = eid_c.shape[0], eid_h.shape[0]
    info, mesh, nb_c = _sc_mesh_and_batches(Tc)
    _, _, nb_h = _sc_mesh_and_batches(Th)

    @functools.partial(
        pl.kernel, mesh=mesh,
        out_type=(jax.ShapeDtypeStruct((Tc, D_MODEL), jnp.float32), jax.ShapeDtypeStruct((Th, N_SEL), jnp.float32)),
        scratch_types=[
            pltpu.VMEM((2, SC_TOKENS, N_SEL), jnp.int32),
            pltpu.VMEM((2, SC_TOKENS, N_SEL), jnp.float32),
            pltpu.VMEM((2, SC_TOKENS, D_MODEL), jnp.float32),
            pltpu.VMEM((2, SC_TOKENS, N_SEL), jnp.int32),
            pltpu.VMEM((2, SC_TOKENS, D_MODEL), jnp.float32),
            pltpu.VMEM((2, SC_TOKENS, N_SEL), jnp.float32),
            pltpu.VMEM((GATHER_BUFS, GATHER_ROWS, D_MODEL), jnp.float32),
            pltpu.VMEM((GATHER_ROWS * ACC_STRIDE,), jnp.float32),
            pltpu.SemaphoreType.DMA((GATHER_BUFS,)),
            pltpu.SemaphoreType.DMA((2,)),
            pltpu.SemaphoreType.DMA((2,)),
            pltpu.SemaphoreType.DMA((2,)),
            pltpu.SemaphoreType.DMA((2,)),
        ],
        compiler_params=pltpu.CompilerParams(needs_layout_passes=False),
        name="peer_step_sc",
    )
    def k(eidc_hbm, a_hbm, eidh_hbm, xn_hbm, u_hbm, v_hbm, c_hbm, hp_hbm,
          idxc_v, a_v, oc_v, idxh_v, x_v, oh_v, bufs, acc_v, gsem, ssem_c, osem_c, ssem_h, osem_h):
        wid = lax.axis_index("s") * info.num_cores + lax.axis_index("c")
        u_tab = u_hbm.at[layer_h]
        first_h = lambda: _stage_copies(wid, nb_h, [eidh_hbm, xn_hbm], [idxh_v, x_v], ssem_h, 0, 0)
        for c in first_h():
            c.start()

        def hand_over(head, b):
            if head == 0:
                for c in first_h():
                    c.wait()
            idx = idxh_v[0, 0, pl.ds(head * GATHER_ROWS, GATHER_ROWS)]
            pltpu.async_copy(u_tab.at[idx], bufs.at[b], gsem.at[b])

        _sc_gather_stream(wid, nb_c, v_hbm.at[layer_c], [eidc_hbm, a_hbm], [idxc_v, a_v], c_hbm, oc_v, bufs,
                          gsem, ssem_c, osem_c, _combine_compute(a_v, oc_v, bufs), hand_over=hand_over)
        _sc_gather_stream(wid, nb_h, u_tab, [eidh_hbm, xn_hbm], [idxh_v, x_v], hp_hbm, oh_v, bufs,
                          gsem, ssem_h, osem_h, _hidden_compute(x_v, oh_v, bufs, acc_v), handed_over=True)

    return k(eid_c, a, eid_h, xn, u_tabs, v_tabs)


def _peer_hidden_sc(eid, xn, u_tabs, layer):
    T = eid.shape[0]
    info, mesh, n_batches = _sc_mesh_and_batches(T)

    @functools.partial(
        pl.kernel, mesh=mesh,
        out_type=jax.ShapeDtypeStruct((T, N_SEL), jnp.float32),
        scratch_types=[
            pltpu.VMEM((2, SC_TOKENS, N_SEL), jnp.int32),
            pltpu.VMEM((2, SC_TOKENS, D_MODEL), jnp.float32),
            pltpu.VMEM((2, SC_TOKENS, N_SEL), jnp.float32),
            pltpu.VMEM((GATHER_BUFS, GATHER_ROWS, D_MODEL), jnp.float32),
            pltpu.VMEM((GATHER_ROWS * ACC_STRIDE,), jnp.float32),
            pltpu.SemaphoreType.DMA((GATHER_BUFS,)),
            pltpu.SemaphoreType.DMA((2,)),
            pltpu.SemaphoreType.DMA((2,)),
        ],
        compiler_params=pltpu.CompilerParams(needs_layout_passes=False),
        name="peer_hidden_sc",
    )
    def k(eid_hbm, xn_hbm, u_hbm, out_hbm, idx_v, x_v, o_v, bufs, acc_v, gsem, ssem, osem):
        wid = lax.axis_index("s") * info.num_cores + lax.axis_index("c")
        _sc_gather_stream(wid, n_batches, u_hbm.at[layer], [eid_hbm, xn_hbm], [idx_v, x_v], out_hbm, o_v, bufs,
                          gsem, ssem, osem, _hidden_compute(x_v, o_v, bufs, acc_v))

    return k(eid, xn, u_tabs)


def _peer_combine_sc(eid, a, v_tabs, layer):
    T = eid.shape[0]
    info, mesh, n_batches = _sc_mesh_and_batches(T)

    @functools.partial(
        pl.kernel, mesh=mesh,
        out_type=jax.ShapeDtypeStruct((T, D_MODEL), jnp.float32),
        scratch_types=[
            pltpu.VMEM((2, SC_TOKENS, N_SEL), jnp.int32),
            pltpu.VMEM((2, SC_TOKENS, N_SEL), jnp.float32),
            pltpu.VMEM((2, SC_TOKENS, D_MODEL), jnp.float32),
            pltpu.VMEM((GATHER_BUFS, GATHER_ROWS, D_MODEL), jnp.float32),
            pltpu.SemaphoreType.DMA((GATHER_BUFS,)),
            pltpu.SemaphoreType.DMA((2,)),
            pltpu.SemaphoreType.DMA((2,)),
        ],
        compiler_params=pltpu.CompilerParams(needs_layout_passes=False),
        name="peer_combine_sc",
    )
    def k(eid_hbm, a_hbm, v_hbm, out_hbm, idx_v, a_v, o_v, bufs, gsem, ssem, osem):
        wid = lax.axis_index("s") * info.num_cores + lax.axis_index("c")
        _sc_gather_stream(wid, n_batches, v_hbm.at[layer], [eid_hbm, a_hbm], [idx_v, a_v], out_hbm, o_v, bufs,
                          gsem, ssem, osem, _combine_compute(a_v, o_v, bufs))

    return k(eid, a, v_tabs)


def _peer_act_kernel(hp_ref, gate_ref, a_ref):
    a_ref[...] = _gelu(hp_ref[...]) * gate_ref[...]


ELEMENTWISE_ROWS = 1024


def _peer_act(hpre, gate):
    T = hpre.shape[0]
    rows = ELEMENTWISE_ROWS
    assert T % rows == 0
    spec = pl.BlockSpec((rows, N_SEL), lambda i: (i, 0))
    return pl.pallas_call(
        _peer_act_kernel, grid=(T // rows,), in_specs=[spec, spec], out_specs=spec,
        out_shape=jax.ShapeDtypeStruct((T, N_SEL), jnp.float32),
        compiler_params=pltpu.CompilerParams(dimension_semantics=("arbitrary",)),
        name="peer_act",
    )(hpre, gate)


def _residual_kernel(h_ref, c_ref, g_ref, o_ref, *, final_norm):
    y = h_ref[...] + c_ref[...]
    o_ref[...] = _rmsnorm_rows(y, g_ref[...]) if final_norm else y


def _residual(h, c, gfin, *, final_norm):
    T = h.shape[0]
    rows = ELEMENTWISE_ROWS
    assert T % rows == 0
    spec = pl.BlockSpec((rows, D_MODEL), lambda i: (i, 0))
    return pl.pallas_call(
        functools.partial(_residual_kernel, final_norm=final_norm), grid=(T // rows,),
        in_specs=[spec, spec, pl.BlockSpec((1, D_MODEL), lambda i: (0, 0))], out_specs=spec,
        out_shape=jax.ShapeDtypeStruct((T, D_MODEL), jnp.float32),
        compiler_params=pltpu.CompilerParams(dimension_semantics=("arbitrary",)),
        name="peer_residual",
    )(h, c, gfin.reshape(1, D_MODEL))


KV_WIDTH = N_KV_HEADS * HEAD_DIM
BF16 = jnp.bfloat16


def _qkv(h, gkv, gq, wkv_ref, wq_ref):
    kv = jnp.dot(_rmsnorm_rows(h, gkv).astype(BF16), wkv_ref[...], preferred_element_type=jnp.float32)
    q = jnp.dot(_rmsnorm_rows(h, gq).astype(BF16), wq_ref[...], preferred_element_type=jnp.float32)
    return kv, q


def _sink_softmax_pv(parts, sink):
    m = sink
    for s, _ in parts:
        m = jnp.maximum(m, jnp.max(s, axis=-1, keepdims=True))
    den = jnp.exp(sink - m)
    acc = None
    for s, v in parts:
        e = jnp.exp(s - m)
        den = den + jnp.sum(e, axis=-1, keepdims=True)
        pv = jnp.dot(e.astype(BF16), v.astype(BF16), preferred_element_type=jnp.float32)
        acc = pv if acc is None else acc + pv
    return acc / den


def _nt_dot(a, b):
    return lax.dot_general(a.astype(BF16), b.astype(BF16), (((1,), (1,)), ((), ())),
                           preferred_element_type=jnp.float32)


def _attn_prompt_kernel(h_ref, c_ref, gkv_ref, gq_ref, wkv_ref, wq_ref, wo_ref, sink_ref,
                        o_ref, kvw_ref, prev_ref):
    blk = pl.program_id(1)
    h = h_ref[...] + c_ref[...]
    kv, q = _qkv(h, gkv_ref[...], gq_ref[...], wkv_ref, wq_ref)
    kvw_ref[0] = kv

    @pl.when(blk == 0)
    def _():
        prev_ref[...] = jnp.zeros_like(prev_ref)

    prev = prev_ref[...]
    qi = lax.broadcasted_iota(jnp.int32, (WINDOW, WINDOW), 0)
    kj = lax.broadcasted_iota(jnp.int32, (WINDOW, WINDOW), 1)
    prev_ok = jnp.logical_and(kj > qi, blk > 0)
    cur_ok = kj <= qi
    heads = []
    for kvh in range(N_KV_HEADS):
        ks = slice(kvh * HEAD_DIM, (kvh + 1) * HEAD_DIM)
        vs = slice(KV_WIDTH + kvh * HEAD_DIM, KV_WIDTH + (kvh + 1) * HEAD_DIM)
        for g in range(Q_PER_KV):
            hq = kvh * Q_PER_KV + g
            qh = q[:, hq * HEAD_DIM:(hq + 1) * HEAD_DIM]
            sp = jnp.where(prev_ok, _nt_dot(qh, prev[:, ks]) * ATTN_SCALE, -jnp.inf)
            sc = jnp.where(cur_ok, _nt_dot(qh, kv[:, ks]) * ATTN_SCALE, -jnp.inf)
            heads.append(_sink_softmax_pv([(sp, prev[:, vs]), (sc, kv[:, vs])], sink_ref[hq]))
    o = jnp.concatenate(heads, axis=1)
    o_ref[...] = h + jnp.dot(o.astype(BF16), wo_ref[...], preferred_element_type=jnp.float32)
    prev_ref[...] = kv


def _attn_prompt(h, c, gkv, gq, wkv, wq, wo, sinks, *, n_seq, seq_len):
    nb = seq_len // WINDOW
    row_spec = pl.BlockSpec((WINDOW, D_MODEL), lambda n, b: (n * nb + b, 0))
    full = lambda shape: pl.BlockSpec(shape, lambda n, b: (0,) * len(shape))
    return pl.pallas_call(
        _attn_prompt_kernel,
        grid=(n_seq, nb),
        in_specs=[
            row_spec, row_spec, full((1, D_MODEL)), full((1, D_MODEL)),
            full((D_MODEL, 2 * KV_WIDTH)), full((D_MODEL, D_MODEL)), full((D_MODEL, D_MODEL)),
            pl.BlockSpec(memory_space=pltpu.SMEM),
        ],
        out_specs=[row_spec, pl.BlockSpec((1, WINDOW, 2 * KV_WIDTH), lambda n, b: (n, 0, 0))],
        out_shape=[
            jax.ShapeDtypeStruct(h.shape, jnp.float32),
            jax.ShapeDtypeStruct((n_seq, WINDOW, 2 * KV_WIDTH), jnp.float32),
        ],
        scratch_shapes=[pltpu.VMEM((WINDOW, 2 * KV_WIDTH), jnp.float32)],
        compiler_params=pltpu.CompilerParams(
            dimension_semantics=("arbitrary", "arbitrary"), vmem_limit_bytes=VMEM_LIMIT_BYTES),
        name="attn_prompt",
    )(h, c, gkv.reshape(1, D_MODEL), gq.reshape(1, D_MODEL), wkv, wq, wo, sinks)


ATTN_SEQS = 16
FIRST_PROMPT_PIECES = 4


def _attn_sample_kernel(h_ref, c_ref, ck_ref, cv_ref, gkv_ref, gq_ref, wkv_ref, wq_ref, wo_ref, sink_ref,
                        o_ref, kw_ref, vw_ref, kv_ref, q_ref, att_ref, *, n_new):
    h = h_ref[...] + c_ref[...]
    kv, q = _qkv(h, gkv_ref[...], gq_ref[...], wkv_ref, wq_ref)
    kv_ref[...] = kv
    q_ref[...] = q
    n_seq = h.shape[0] // n_new
    rows = Q_PER_KV * n_new
    qpos_c = lax.broadcasted_iota(jnp.int32, (rows, WINDOW), 0) % n_new
    cache_ok = lax.broadcasted_iota(jnp.int32, (rows, WINDOW), 1) > qpos_c
    qpos_n = lax.broadcasted_iota(jnp.int32, (rows, n_new), 0) % n_new
    new_ok = lax.broadcasted_iota(jnp.int32, (rows, n_new), 1) <= qpos_n

    def per_seq(n, carry):
        r0 = pl.multiple_of(n * n_new, n_new)
        kvn = kv_ref[pl.ds(r0, n_new), :]
        qn = q_ref[pl.ds(r0, n_new), :]
        ck = ck_ref[n]
        cv = cv_ref[n]
        kw_ref[n, 0:WINDOW - n_new, :] = ck[n_new:, :]
        kw_ref[n, WINDOW - n_new:WINDOW, :] = kvn[:, 0:KV_WIDTH]
        vw_ref[n, 0:WINDOW - n_new, :] = cv[n_new:, :]
        vw_ref[n, WINDOW - n_new:WINDOW, :] = kvn[:, KV_WIDTH:2 * KV_WIDTH]
        outs = []
        for kvh in range(N_KV_HEADS):
            ks = slice(kvh * HEAD_DIM, (kvh + 1) * HEAD_DIM)
            vs = slice(KV_WIDTH + kvh * HEAD_DIM, KV_WIDTH + (kvh + 1) * HEAD_DIM)
            qs = jnp.concatenate(
                [qn[:, (kvh * Q_PER_KV + g) * HEAD_DIM:(kvh * Q_PER_KV + g + 1) * HEAD_DIM] for g in range(Q_PER_KV)],
                axis=0)
            s_c = jnp.where(cache_ok, _nt_dot(qs, ck[:, ks]) * ATTN_SCALE, -jnp.inf)
            s_n = jnp.where(new_ok, _nt_dot(qs, kvn[:, ks]) * ATTN_SCALE, -jnp.inf)
            sink = sink_ref[kvh][:, 0:1]
            o = _sink_softmax_pv([(s_c, cv[:, ks]), (s_n, kvn[:, vs])], sink)
            outs += [o[g * n_new:(g + 1) * n_new, :] for g in range(Q_PER_KV)]
        att_ref[pl.ds(r0, n_new), :] = jnp.concatenate(outs, axis=1)
        return carry

    lax.fori_loop(0, n_seq, per_seq, 0)
    o_ref[...] = h + jnp.dot(att_ref[...].astype(BF16), wo_ref[...], preferred_element_type=jnp.float32)


def _attn_sample(h, c, cache_k, cache_v, gkv, gq, wkv, wq, wo, sinks, *, n_new):
    n_seq = cache_k.shape[0]
    sb = ATTN_SEQS
    rows = sb * n_new
    row_spec = pl.BlockSpec((rows, D_MODEL), lambda i: (i, 0))
    win_spec = pl.BlockSpec((sb, WINDOW, KV_WIDTH), lambda i: (i, 0, 0))
    full = lambda shape: pl.BlockSpec(shape, lambda i: (0,) * len(shape))
    sink_rows = jnp.repeat(sinks.reshape(N_KV_HEADS, Q_PER_KV), n_new, axis=1)[:, :, None]
    sink_rows = jnp.broadcast_to(sink_rows, (N_KV_HEADS, Q_PER_KV * n_new, LANES))
    return pl.pallas_call(
        functools.partial(_attn_sample_kernel, n_new=n_new),
        grid=(n_seq // sb,),
        in_specs=[
            row_spec, row_spec, win_spec, win_spec, full((1, D_MODEL)), full((1, D_MODEL)),
            full((D_MODEL, 2 * KV_WIDTH)), full((D_MODEL, D_MODEL)), full((D_MODEL, D_MODEL)),
            full((N_KV_HEADS, Q_PER_KV * n_new, LANES)),
        ],
        out_specs=[row_spec, win_spec, win_spec],
        out_shape=[
            jax.ShapeDtypeStruct(h.shape, jnp.float32),
            jax.ShapeDtypeStruct((n_seq, WINDOW, KV_WIDTH), jnp.float32),
            jax.ShapeDtypeStruct((n_seq, WINDOW, KV_WIDTH), jnp.float32),
        ],
        scratch_shapes=[
            pltpu.VMEM((rows, 2 * KV_WIDTH), jnp.float32),
            pltpu.VMEM((rows, D_MODEL), jnp.float32),
            pltpu.VMEM((rows, D_MODEL), jnp.float32),
        ],
        compiler_params=pltpu.CompilerParams(
            dimension_semantics=("arbitrary",), vmem_limit_bytes=VMEM_LIMIT_BYTES),
        name="attn_sample",
    )(h, c, cache_k, cache_v, gkv.reshape(1, D_MODEL), gq.reshape(1, D_MODEL), wkv, wq, wo, sink_rows)


def kernel(x_prompt, x_sample, state_ssm_re, state_ssm_im, cache_k_win, cache_v_win, norm_mix, norm_ffn, norm_kv, norm_final, ssm_lam_re, ssm_lam_im, ssm_log_dt, ssm_b_re, ssm_b_im, ssm_c_re, ssm_c_im, ssm_d, ssm_w_glu, w_kv, w_q, attn_sinks, w_o, peer_w_q, peer_sub_keys, peer_u, peer_v):
    bmat, cmat, apr, api = _s5_discretize(ssm_lam_re[0], ssm_lam_im[0], ssm_log_dt[0], ssm_b_re[0], ssm_b_im[0], ssm_c_re[0], ssm_c_im[0])
    cmat = cmat.astype(jnp.bfloat16)
    wglu = ssm_w_glu[0].astype(jnp.bfloat16)
    wkv = w_kv.astype(BF16)
    wq = w_q[0].astype(BF16)
    wo = w_o[0].astype(BF16)
    peer_wq = [peer_w_q[layer].astype(BF16).reshape(D_MODEL, PEER_HEADS, 2 * PEER_DHALF).transpose(1, 0, 2)
               for layer in range(2)]
    peer_sk = [peer_sub_keys[layer].astype(BF16) for layer in range(2)]

    def route(h, layer):
        xn, eid_t, gate_t = _peer_route(h, norm_ffn[layer], peer_wq[layer], peer_sk[layer])
        return dict(h=h, xn=xn, eid=eid_t.T, gate=gate_t.T, layer=layer)

    win = lambda a: a.reshape(a.shape[0], WINDOW, N_KV_HEADS, HEAD_DIM)

    seq_len = x_prompt.shape[1]
    xp_all = x_prompt.reshape(-1, D_MODEL)

    n_prompt, n_s = x_prompt.shape[0], x_sample.shape[0]
    xs = x_sample.reshape(-1, D_MODEL)
    h1, sre_s, sim_s = _s5_mixer(xs, _state_to_chunks(state_ssm_re[0]), _state_to_chunks(state_ssm_im[0]), norm_mix[0],
                                 ssm_d[0], bmat, cmat, apr, api, wglu, chain=False, seq_len=x_sample.shape[1])
    units = [route(h1, 0)]
    group_units = [[0]]
    prompt_states = []
    for seq in range(n_prompt):
        n_pieces = FIRST_PROMPT_PIECES if seq == 0 else 1
        piece = seq_len // n_pieces
        sr = si = jnp.zeros((1, N_CHUNKS, 1, CHUNK_STATE), jnp.float32)
        group_units.append([])
        for p in range(n_pieces):
            row0 = seq * seq_len + p * piece
            h1, sr, si = _s5_mixer(xp_all, sr, si, norm_mix[0], ssm_d[0], bmat, cmat, apr, api, wglu,
                                   chain=True, seq_len=piece, row0=row0)
            group_units[-1].append(len(units))
            units.append(route(h1, 0))
        prompt_states.append((sr, si))
    n_layer0 = len(units)
    windows = {}

    def layer1_unit(g):
        mine = [units[i] for i in group_units[g]]
        h1 = jnp.concatenate([u['h'] for u in mine], axis=0) if len(mine) > 1 else mine[0]['h']
        c1 = jnp.concatenate([u['c'] for u in mine], axis=0) if len(mine) > 1 else mine[0]['c']
        if g == 0:
            h3, kw, vw = _attn_sample(h1, c1, cache_k_win.reshape(n_s, WINDOW, KV_WIDTH),
                                      cache_v_win.reshape(n_s, WINDOW, KV_WIDTH), norm_kv, norm_mix[1],
                                      wkv, wq, wo, attn_sinks[0], n_new=x_sample.shape[1])
            windows[g] = (win(kw), win(vw))
        else:
            h3, kvw = _attn_prompt(h1, c1, norm_kv, norm_mix[1], wkv, wq, wo, attn_sinks[0], n_seq=1,
                                   seq_len=seq_len)
            windows[g] = (win(kvw[:, :, :KV_WIDTH]), win(kvw[:, :, KV_WIDTH:]))
        return route(h3, 1)

    layer1_order = list(range(1, n_prompt + 1)) + [0]
    n_units = n_layer0 + 1 + n_prompt
    for k in range(n_units + 2):
        if n_layer0 <= k < n_units:
            units.append(layer1_unit(layer1_order[k - n_layer0]))
        hid = units[k] if k < n_units else None
        comb = units[k - 2] if k >= 2 else None
        if comb is not None:
            comb['a'] = _peer_act(comb['hpre'], comb['gate'])
        if hid is not None and comb is not None:
            comb['c'], hid['hpre'] = _peer_step_sc(comb['eid'], comb['a'], comb['layer'],
                                                   hid['eid'], hid['xn'], hid['layer'], peer_u, peer_v)
        elif hid is not None:
            hid['hpre'] = _peer_hidden_sc(hid['eid'], hid['xn'], peer_u, hid['layer'])
        else:
            comb['c'] = _peer_combine_sc(comb['eid'], comb['a'], peer_v, comb['layer'])

    ys = [_residual(u['h'], u['c'], norm_final, final_norm=True) for u in units[n_layer0:]]
    y_s = ys[-1].reshape(x_sample.shape)
    y_p = jnp.concatenate(ys[:-1], axis=0).reshape(x_prompt.shape)
    cat = lambda parts: jnp.concatenate(parts, axis=0)
    sre_p = cat([_chunks_to_state(sr) for sr, _ in prompt_states])
    sim_p = cat([_chunks_to_state(si) for _, si in prompt_states])
    kw_p = cat([windows[g][0] for g in range(1, n_prompt + 1)])
    vw_p = cat([windows[g][1] for g in range(1, n_prompt + 1)])
    return (y_p, y_s, sre_p[None], sim_p[None], kw_p, vw_p,
            _chunks_to_state(sre_s)[None], _chunks_to_state(sim_s)[None], windows[0][0], windows[0][1])
```
